```python
import math
import jax
import jax.numpy as jnp
from jax import lax
import numpy as np

D_MODEL = 1024
BATCH = 8
SEQ = 2048
DEPTH = 4

MEM_LEN = 256
N_BRANCHES = 4
MIX_WIDTH = D_MODEL // N_BRANCHES
HEAD_DIM = 64
MIX_HEADS = MIX_WIDTH // HEAD_DIM

NSA_CMP_LEN = 32
NSA_CMP_STRIDE = 16
NSA_SLC_LEN = 64
NSA_TOP_N = 8
NSA_WINDOW = 512
Q_BLOCK = 128
FORCE_BONUS = 1e4
N_BUCKETS = 32
MAX_DISTANCE = 128

RET_CHUNK = 128
ROPE_BASE = 10000.0
RET_NORM_EPS = 1e-5

RWKV_DECAY_LORA = 32
RWKV_AAA_LORA = 32
RWKV_GATE_LORA = 64
RWKV_GN_EPS = 64e-5
RWKV_SIZES = (MIX_WIDTH, MIX_WIDTH, MIX_WIDTH, RWKV_DECAY_LORA, RWKV_AAA_LORA, RWKV_GATE_LORA)
RWKV_COLS = sum(RWKV_SIZES)

CONV_WIDTH = 3

XA_HEADS = 4
XA_HEAD_DIM = D_MODEL // XA_HEADS
D_FF = 4 * D_MODEL

IN_SIZES = (MIX_WIDTH, 6 * HEAD_DIM, 3 * MIX_HEADS, 4 * MIX_WIDTH, RWKV_COLS, 3 * MIX_WIDTH, N_BRANCHES * D_MODEL)
IN_COLS = sum(IN_SIZES)

RMS_EPS = 1e-6
NEG_INF = -1e30

kernel_name = 'hybrid_nsa_retention_rwkv7_shortconv_block'


def split_cols(z, sizes):
    return jnp.split(z, np.cumsum(sizes)[:-1].tolist(), axis=-1)


def rms_norm(x, g):
    xf = x.astype(jnp.float32)
    y = xf * lax.rsqrt(jnp.mean(xf * xf, axis=-1, keepdims=True) + RMS_EPS)
    return (y * g.astype(jnp.float32)).astype(x.dtype)


def head_norm(u, eps):
    mu = jnp.mean(u, axis=-1, keepdims=True)
    var = jnp.mean(jnp.square(u - mu), axis=-1, keepdims=True)
    return (u - mu) * lax.rsqrt(var + eps)


def t5_bucket(dist):
    n = jnp.maximum(dist, 0)
    max_exact = N_BUCKETS // 2
    nf = jnp.maximum(n, 1).astype(jnp.float32)
    large = max_exact + (jnp.log(nf / max_exact) / math.log(MAX_DISTANCE / max_exact)
                         * (N_BUCKETS - max_exact)).astype(jnp.int32)
    large = jnp.minimum(large, N_BUCKETS - 1)
    return jnp.where(n < max_exact, n, large)


def masked_softmax(s, valid):
    return jax.nn.softmax(jnp.where(valid, s, NEG_INF), axis=-1)


def nsa_mixer(q, kv, gate_logits, cmp_w, cmp_pe, rel_bias):
    B, S, H, Dh = q.shape
    dt = q.dtype
    f32 = jnp.float32
    scale = Dh ** -0.5
    k_c, v_c, k_s, v_s, k_w, v_w = jnp.split(kv, 6, axis=-1)

    n_cmp = (S - NSA_CMP_LEN) // NSA_CMP_STRIDE + 1
    cmp_start = jnp.arange(n_cmp) * NSA_CMP_STRIDE
    cmp_idx = cmp_start[:, None] + jnp.arange(NSA_CMP_LEN)[None, :]
    k_cmp = jnp.einsum('bnld,lde->bne', k_c[:, cmp_idx] + cmp_pe, cmp_w[0])
    v_cmp = jnp.einsum('bnld,lde->bne', v_c[:, cmp_idx] + cmp_pe, cmp_w[1])
    cmp_end = cmp_start + NSA_CMP_LEN - 1

    n_slc = S // NSA_SLC_LEN
    top_n = min(NSA_TOP_N, n_slc)
    slc_start = jnp.arange(n_slc) * NSA_SLC_LEN
    overlap = jnp.clip(jnp.minimum(cmp_start[:, None] + NSA_CMP_LEN, slc_start[None, :] + NSA_SLC_LEN)
                       - jnp.maximum(cmp_start[:, None], slc_start[None, :]), 0).astype(f32) / NSA_CMP_LEN
    k_blk = k_s.reshape(B, n_slc, NSA_SLC_LEN, Dh)
    v_blk = v_s.reshape(B, n_slc, NSA_SLC_LEN, Dh)
    b_idx = jnp.arange(B)[:, None, None]

    k_wp = jnp.pad(k_w, ((0, 0), (NSA_WINDOW, 0), (0, 0)))
    v_wp = jnp.pad(v_w, ((0, 0), (NSA_WINDOW, 0), (0, 0)))

    gates = jax.nn.sigmoid(gate_logits.astype(f32)).reshape(B, S, H, 3)
    n_qb = S // Q_BLOCK
    q_b = q.reshape(B, n_qb, Q_BLOCK, H, Dh).swapaxes(0, 1)
    g_b = gates.reshape(B, n_qb, Q_BLOCK, H, 3).swapaxes(0, 1)
    bias_f = rel_bias.astype(f32)

    def block_fn(args):
        qb, gb, bi = args
        t = bi * Q_BLOCK + jnp.arange(Q_BLOCK)

        d_c = t[:, None] - cmp_end[None, :]
        s_c = (jnp.einsum('bqhd,bnd->bhqn', qb, k_cmp).astype(f32) * scale
               + bias_f[t5_bucket(d_c)].transpose(2, 0, 1))
        p_c = masked_softmax(s_c, d_c >= 0) * (t >= NSA_CMP_LEN - 1)[:, None].astype(f32)
        o_c = jnp.einsum('bhqn,bnd->bqhd', p_c.astype(dt), v_cmp)

        imp = jnp.einsum('bhqn,nm->bqm', p_c, overlap)
        blk = jnp.arange(n_slc)
        cur = t // NSA_SLC_LEN
        forced = (blk[None, :] == 0) | (blk[None, :] == cur[:, None]) | (blk[None, :] == cur[:, None] - 1)
        imp = jnp.where(forced, imp + FORCE_BONUS, imp)
        imp = jnp.where(blk[None, :] <= cur[:, None], imp, NEG_INF)
        _, sel = lax.top_k(imp, top_n)

        k_g = k_blk[b_idx, sel].reshape(B, Q_BLOCK, top_n * NSA_SLC_LEN, Dh)
        v_g = v_blk[b_idx, sel].reshape(B, Q_BLOCK, top_n * NSA_SLC_LEN, Dh)
        pos_s = (sel[..., None] * NSA_SLC_LEN + jnp.arange(NSA_SLC_LEN)).reshape(B, Q_BLOCK, -1)
        d_s = t[None, :, None] - pos_s
        s_s = (jnp.einsum('bqhd,bqkd->bhqk', qb, k_g).astype(f32) * scale
               + bias_f[t5_bucket(d_s)].transpose(0, 3, 1, 2))
        p_s = masked_softmax(s_s, (d_s >= 0)[:, None])
        o_s = jnp.einsum('bhqk,bqkd->bqhd', p_s.astype(dt), v_g)

        k_wb = lax.dynamic_slice_in_dim(k_wp, bi * Q_BLOCK, NSA_WINDOW + Q_BLOCK, axis=1)
        v_wb = lax.dynamic_slice_in_dim(v_wp, bi * Q_BLOCK, NSA_WINDOW + Q_BLOCK, axis=1)
        pos_w = bi * Q_BLOCK - NSA_WINDOW + jnp.arange(NSA_WINDOW + Q_BLOCK)
        d_w = t[:, None] - pos_w[None, :]
        valid_w = (d_w >= 0) & (d_w < NSA_WINDOW) & (pos_w[None, :] >= 0)
        s_w = (jnp.einsum('bqhd,bkd->bhqk', qb, k_wb).astype(f32) * scale
               + bias_f[t5_bucket(d_w)].transpose(2, 0, 1))
        p_w = masked_softmax(s_w, valid_w)
        o_w = jnp.einsum('bhqk,bkd->bqhd', p_w.astype(dt), v_wb)

        out = gb[..., 0:1] * o_c + gb[..., 1:2] * o_s + gb[..., 2:3] * o_w
        return out.astype(dt)

    out = lax.map(block_fn, (q_b, g_b, jnp.arange(n_qb)))
    return out.swapaxes(0, 1).reshape(B, S, H * Dh)


def rotary(u, pos):
    half = u.shape[-1] // 2
    inv_freq = ROPE_BASE ** (-jnp.arange(half, dtype=jnp.float32) / half)
    ang = pos.astype(jnp.float32)[:, None] * inv_freq[None, :]
    cos = jnp.cos(ang)[None, :, None, :]
    sin = jnp.sin(ang)[None, :, None, :]
    u1, u2 = u[..., :half], u[..., half:]
    return jnp.concatenate([u1 * cos - u2 * sin, u2 * cos + u1 * sin], axis=-1)


def retention_mixer(q, k, v, g, norm_g):
    B, S, C = q.shape
    H = C // HEAD_DIM
    f32 = jnp.float32
    pos = jnp.arange(S)
    qh = rotary(q.reshape(B, S, H, HEAD_DIM).astype(f32), pos) * HEAD_DIM ** -0.5
    kh = rotary(k.reshape(B, S, H, HEAD_DIM).astype(f32), pos)
    vh = v.reshape(B, S, H, HEAD_DIM).astype(f32)
    L = min(RET_CHUNK, S)
    nc = S // L
    lg = jnp.log(1.0 - 2.0 ** (-5.0 - jnp.arange(H, dtype=f32)))
    n = jnp.arange(L, dtype=f32)
    diff = n[:, None] - n[None, :]
    inner_decay = jnp.where(diff >= 0, jnp.exp(jnp.maximum(diff, 0.0)[None] * lg[:, None, None]), 0.0)
    q_decay = jnp.exp((n + 1.0)[None, :] * lg[:, None])
    k_decay = jnp.exp((L - 1.0 - n)[None, :] * lg[:, None])
    chunk_decay = jnp.exp(L * lg)

    def to_chunks(u):
        return u.reshape(B, nc, L, H, HEAD_DIM).transpose(1, 0, 3, 2, 4)

    def step(state, inp):
        qc, kc, vc = inp
        att = jnp.einsum('bhld,bhmd->bhlm', qc, kc) * inner_decay
        o = (jnp.einsum('bhlm,bhmd->bhld', att, vc)
             + jnp.einsum('bhld,bhde->bhle', qc, state) * q_decay[None, :, :, None])
        state = (state * chunk_decay[None, :, None, None]
                 + jnp.einsum('bhmd,bhme->bhde', kc * k_decay[None, :, :, None], vc))
        return state, o

    state0 = jnp.zeros((B, H, HEAD_DIM, HEAD_DIM), f32)
    _, o = lax.scan(step, state0, (to_chunks(qh), to_chunks(kh), to_chunks(vh)))
    o = o.transpose(1, 0, 3, 2, 4).reshape(B, S, H, HEAD_DIM)
    o = head_norm(o, RET_NORM_EPS) * norm_g.astype(f32).reshape(H, HEAD_DIM)
    return (jax.nn.silu(g.astype(f32)) * o.reshape(B, S, C)).astype(q.dtype)


def rwkv7_mixer(z, mu, w0, w2, a0, a2, g2, k_k, k_a, r_k, ln_g, ln_b):
    B, S, _ = z.shape
    C = MIX_WIDTH
    H = C // HEAD_DIM
    N = HEAD_DIM
    dt = z.dtype
    f32 = jnp.float32
    zf = z.astype(f32)
    zf = zf + (jnp.pad(zf, ((0, 0), (1, 0), (0, 0)))[:, :-1] - zf) * mu
    r, k, v, wl, al, gl = split_cols(zf, RWKV_SIZES)
    w_log = -jax.nn.softplus(-(w0 + jnp.tanh(wl) @ w2)) - 0.5
    decay = jnp.exp(-jnp.exp(w_log))
    a = jax.nn.sigmoid(a0 + al @ a2)
    g = jax.nn.sigmoid(gl) @ g2
    kk = (k * k_k).reshape(B, S, H, N)
    kk = kk / jnp.maximum(jnp.sqrt(jnp.sum(kk * kk, axis=-1, keepdims=True)), 1e-12)
    k = k * (1.0 + (a - 1.0) * k_a)

    def heads(u):
        return u.reshape(B, S, H, N)

    rh, wh, kh, vh, ah = heads(r), heads(decay), heads(k), heads(v), heads(a)

    def seq_first(u):
        return u.transpose(1, 0, 2, 3)

    def step(state, inp):
        r_t, w_t, k_t, v_t, kk_t, a_t = inp
        sa = jnp.einsum('bhvk,bhk->bhv', state, -kk_t)
        state = (state * w_t[:, :, None, :]
                 + sa[..., None] * (kk_t * a_t)[:, :, None, :]
                 + v_t[..., None] * k_t[:, :, None, :])
        return state, jnp.einsum('bhvk,bhk->bhv', state, r_t)

    state0 = jnp.zeros((B, H, N, N), f32)
    xs = (seq_first(rh), seq_first(wh), seq_first(kh), seq_first(vh), seq_first(kk), seq_first(ah))
    _, y = lax.scan(step, state0, xs)
    y = y.transpose(1, 0, 2, 3)
    y = head_norm(y, RWKV_GN_EPS) * ln_g.astype(f32).reshape(H, N) + ln_b.astype(f32).reshape(H, N)
    y = y + jnp.sum(rh * kh * r_k.astype(f32).reshape(H, N), axis=-1, keepdims=True) * vh
    return (y.reshape(B, S, C) * g).astype(dt)


def short_conv_mixer(z, conv_w):
    b_g, c_g, xt = jnp.split(z, 3, axis=-1)
    u = c_g * xt
    C = u.shape[-1]
    y = lax.conv_general_dilated(u, conv_w[:, None, :].astype(u.dtype), window_strides=(1,),
                                 padding=[(CONV_WIDTH - 1, 0)],
                                 dimension_numbers=('NWC', 'WIO', 'NWC'),
                                 feature_group_count=C)
    return b_g * y


def cross_attention(h, m, wq, wkv, wo):
    B, S, D = h.shape
    M = m.shape[1]
    q = (h @ wq).reshape(B, S, XA_HEADS, XA_HEAD_DIM)
    k, v = jnp.split(m @ wkv, 2, axis=-1)
    k = k.reshape(B, M, XA_HEADS, XA_HEAD_DIM)
    v = v.reshape(B, M, XA_HEADS, XA_HEAD_DIM)
    s = jnp.einsum('bshd,bmhd->bhsm', q, k).astype(jnp.float32) * XA_HEAD_DIM ** -0.5
    p = jax.nn.softmax(s, axis=-1).astype(h.dtype)
    o = jnp.einsum('bhsm,bmhd->bshd', p, v).reshape(B, S, D)
    return o @ wo


def setup_inputs(seed: int = 0) -> dict:
    key = jax.random.key(seed)
    ks = iter(jax.random.split(key, 48))
    f32 = jnp.float32

    def nrm(shape, scale):
        return scale * jax.random.normal(next(ks), shape, f32)

    def gain(shape):
        return 1.0 + 0.02 * jax.random.normal(next(ks), shape, f32)

    def unif(shape, lo, hi):
        return jax.random.uniform(next(ks), shape, f32, lo, hi)

    return {
        'x': nrm((BATCH, SEQ, D_MODEL), 1.0),
        'mem': nrm((BATCH, MEM_LEN, D_MODEL), 1.0),
        'ln_mix_pre': gain((DEPTH, D_MODEL)),
        'w_in': nrm((DEPTH, D_MODEL, IN_COLS), D_MODEL ** -0.5),
        'nsa_cmp_w': nrm((DEPTH, 2, NSA_CMP_LEN, HEAD_DIM, HEAD_DIM), (NSA_CMP_LEN * HEAD_DIM) ** -0.5),
        'nsa_cmp_pe': nrm((DEPTH, NSA_CMP_LEN, HEAD_DIM), 0.02),
        'rel_bias': nrm((N_BUCKETS, MIX_HEADS), 0.5),
        'ret_norm_g': gain((DEPTH, MIX_WIDTH)),
        'rwkv_mu': unif((DEPTH, RWKV_COLS), 0.0, 1.0),
        'rwkv_w0': unif((DEPTH, MIX_WIDTH), -6.0, 1.0),
        'rwkv_w2': nrm((DEPTH, RWKV_DECAY_LORA, MIX_WIDTH), 0.1),
        'rwkv_a0': nrm((DEPTH, MIX_WIDTH), 0.1),
        'rwkv_a2': nrm((DEPTH, RWKV_AAA_LORA, MIX_WIDTH), 0.1),
        'rwkv_g2': nrm((DEPTH, RWKV_GATE_LORA, MIX_WIDTH), RWKV_GATE_LORA ** -0.5),
        'rwkv_k_k': 0.85 + 0.02 * jax.random.normal(next(ks), (DEPTH, MIX_WIDTH), f32),
        'rwkv_k_a': gain((DEPTH, MIX_WIDTH)),
        'rwkv_r_k': nrm((DEPTH, MIX_WIDTH), 0.1),
        'rwkv_ln_g': gain((DEPTH, MIX_WIDTH)),
        'rwkv_ln_b': nrm((DEPTH, MIX_WIDTH), 0.02),
        'conv_w': nrm((DEPTH, CONV_WIDTH, MIX_WIDTH), CONV_WIDTH ** -0.5),
        'w_branch': nrm((DEPTH, N_BRANCHES, MIX_WIDTH, D_MODEL), MIX_WIDTH ** -0.5),
        'w_mix_out': nrm((DEPTH, D_MODEL, D_MODEL), D_MODEL ** -0.5),
        'ln_mix_post': gain((DEPTH, D_MODEL)),
        'ln_xa_pre': gain((DEPTH, D_MODEL)),
        'ln_mem': gain((DEPTH, D_MODEL)),
        'xa_wq': nrm((DEPTH, D_MODEL, D_MODEL), D_MODEL ** -0.5),
        'xa_wkv': nrm((DEPTH, D_MODEL, 2 * D_MODEL), D_MODEL ** -0.5),
        'xa_wo': nrm((DEPTH, D_MODEL, D_MODEL), D_MODEL ** -0.5),
        'ln_xa_post': gain((DEPTH, D_MODEL)),
        'ln_mlp_pre': gain((DEPTH, D_MODEL)),
        'mlp_w1': nrm((DEPTH, D_MODEL, D_FF), D_MODEL ** -0.5),
        'mlp_w2': nrm((DEPTH, D_FF, D_MODEL), D_FF ** -0.5),
        'ln_mlp_post': gain((DEPTH, D_MODEL)),
    }


def reference(x, mem, ln_mix_pre, w_in, nsa_cmp_w, nsa_cmp_pe, rel_bias, ret_norm_g,
              rwkv_mu, rwkv_w0, rwkv_w2, rwkv_a0, rwkv_a2, rwkv_g2, rwkv_k_k, rwkv_k_a, rwkv_r_k,
              rwkv_ln_g, rwkv_ln_b, conv_w, w_branch, w_mix_out, ln_mix_post,
              ln_xa_pre, ln_mem, xa_wq, xa_wkv, xa_wo, ln_xa_post,
              ln_mlp_pre, mlp_w1, mlp_w2, ln_mlp_post):
    B, S, _ = x.shape
    for l in range(DEPTH):
        h = rms_norm(x, ln_mix_pre[l])
        z = h @ w_in[l]
        z_q, z_kv, z_g, z_ret, z_rwkv, z_conv, z_gate = split_cols(z, IN_SIZES)
        o_nsa = nsa_mixer(z_q.reshape(B, S, MIX_HEADS, HEAD_DIM), z_kv, z_g,
                          nsa_cmp_w[l], nsa_cmp_pe[l], rel_bias)
        r_q, r_k, r_v, r_g = jnp.split(z_ret, 4, axis=-1)
        o_ret = retention_mixer(r_q, r_k, r_v, r_g, ret_norm_g[l])
        o_rwkv = rwkv7_mixer(z_rwkv, rwkv_mu[l], rwkv_w0[l], rwkv_w2[l], rwkv_a0[l], rwkv_a2[l],
                             rwkv_g2[l], rwkv_k_k[l], rwkv_k_a[l], rwkv_r_k[l],
                             rwkv_ln_g[l], rwkv_ln_b[l])
        o_conv = short_conv_mixer(z_conv, conv_w[l])
        gates = jax.nn.sigmoid(z_gate).reshape(B, S, N_BRANCHES, D_MODEL)
        branches = (o_nsa, o_ret, o_rwkv, o_conv)
        merged = gates[:, :, 0] * (branches[0] @ w_branch[l, 0])
        for m in range(1, N_BRANCHES):
            merged = merged + gates[:, :, m] * (branches[m] @ w_branch[l, m])
        x = x + rms_norm(merged @ w_mix_out[l], ln_mix_post[l])

        h = rms_norm(x, ln_xa_pre[l])
        mn = rms_norm(mem, ln_mem[l])
        x = x + rms_norm(cross_attention(h, mn, xa_wq[l], xa_wkv[l], xa_wo[l]), ln_xa_post[l])

        h = rms_norm(x, ln_mlp_pre[l])
        x = x + rms_norm(jnp.square(jax.nn.relu(h @ mlp_w1[l])) @ mlp_w2[l], ln_mlp_post[l])
    return x
```

```python
import functools
import math

import numpy as np
import jax
import jax.numpy as jnp
from jax import lax
from jax.experimental import pallas as pl
from jax.experimental.pallas import tpu as pltpu

F32 = jnp.float32
BF16 = jnp.bfloat16

D_MODEL = 1024
N_BRANCHES = 4
MIX_WIDTH = 256
HEAD_DIM = 64
MIX_HEADS = 4

NSA_CMP_LEN = 32
NSA_CMP_STRIDE = 16
NSA_SLC_LEN = 64
NSA_TOP_N = 8
NSA_WINDOW = 512
Q_BLOCK = 128
FORCE_BONUS = 1e4
N_BUCKETS = 32
MAX_DISTANCE = 128

RET_CHUNK = 128
ROPE_BASE = 10000.0
RET_NORM_EPS = 1e-5

RWKV_DECAY_LORA = 32
RWKV_AAA_LORA = 32
RWKV_GATE_LORA = 64
RWKV_GN_EPS = 64e-5
RWKV_COLS = 3 * MIX_WIDTH + RWKV_DECAY_LORA + RWKV_AAA_LORA + RWKV_GATE_LORA
RWKV_CHUNK = 64

CONV_WIDTH = 3
XA_HEADS = 4
XA_HEAD_DIM = D_MODEL // XA_HEADS
D_FF = 4 * D_MODEL

RMS_EPS = 1e-6
NEG_INF = -1e30
NEG_BIG = -3e38

OFF_GATE = 0
OFF_RET = 4096
OFF_RWKV = 5120
OFF_CONV = 6144
OFF_NSAQ = 6912
OFF_NSAKV = 7168
OFF_NSAG = 7552
Z_COLS = 7680

VMEM_LIMIT = 56 * 1024 * 1024


def _cparams(sem):
    return pltpu.CompilerParams(dimension_semantics=sem, vmem_limit_bytes=VMEM_LIMIT)


def _dot(a, b):
    return jnp.dot(a, b, preferred_element_type=F32)


def _dot_nt(a, b):
    return lax.dot_general(a, b, (((1,), (1,)), ((), ())), preferred_element_type=F32)


def _dot_tn(a, b):
    return lax.dot_general(a, b, (((0,), (0,)), ((), ())), preferred_element_type=F32)


def _split2(x):
    hi = x.astype(BF16)
    lo = (x - hi.astype(F32)).astype(BF16)
    return hi, lo


def _dot_x2(x, w_bf16):
    hi, lo = _split2(x)
    return _dot(hi, w_bf16) + _dot(lo, w_bf16)


def _rms(x, g):
    ms = jnp.mean(x * x, axis=-1, keepdims=True)
    return x * lax.rsqrt(ms + RMS_EPS) * g


def _norm_matmul_kernel(x_ref, g_ref, w_ref, o_ref, h_ref):
    @pl.when(pl.program_id(1) == 0)
    def _():
        h_ref[...] = _rms(x_ref[...], g_ref[...]).astype(BF16)

    o_ref[...] = _dot(h_ref[...], w_ref[...]).astype(o_ref.dtype)


def norm_matmul(x, g, w, layer, tm, tn, out_dtype):
    M, D = x.shape
    N = w.shape[2]
    return pl.pallas_call(
        _norm_matmul_kernel,
        grid=(M // tm, N // tn),
        in_specs=[
            pl.BlockSpec((tm, D), lambda i, j: (i, 0)),
            pl.BlockSpec((None, 1, D), lambda i, j: (layer, 0, 0)),
            pl.BlockSpec((None, D, tn), lambda i, j: (layer, 0, j)),
        ],
        out_specs=pl.BlockSpec((tm, tn), lambda i, j: (i, j)),
        out_shape=jax.ShapeDtypeStruct((M, N), out_dtype),
        scratch_shapes=[pltpu.VMEM((tm, D), BF16)],
        compiler_params=_cparams(("parallel", "arbitrary")),
        name="norm_matmul",
    )(x, g, w)


def _nsa_compress_kernel(kv_ref, wlo_ref, whi_ref, pelo_ref, pehi_ref, o_ref):
    nblk = o_ref.shape[0]
    ylo = jnp.zeros((nblk, 128), F32)
    yhi = jnp.zeros((nblk, 128), F32)
    for r in range(NSA_CMP_STRIDE):
        xr = kv_ref[pl.ds(r, nblk, stride=NSA_CMP_STRIDE), :]
        ylo = ylo + _dot((xr + pelo_ref[r]).astype(BF16), wlo_ref[r])
        yhi = yhi + _dot((xr + pehi_ref[r]).astype(BF16), whi_ref[r])
    o_ref[...] = ylo + pltpu.roll(yhi, nblk - 1, axis=0)


def nsa_compress(z, wlo, whi, pelo, pehi, layer, B, S):
    nblk = S // NSA_CMP_STRIDE
    return pl.pallas_call(
        _nsa_compress_kernel,
        grid=(B,),
        in_specs=[
            pl.BlockSpec((S, 128), lambda b: (b, OFF_NSAKV // 128)),
            pl.BlockSpec((None, NSA_CMP_STRIDE, 128, 128), lambda b: (layer, 0, 0, 0)),
            pl.BlockSpec((None, NSA_CMP_STRIDE, 128, 128), lambda b: (layer, 0, 0, 0)),
            pl.BlockSpec((None, NSA_CMP_STRIDE, 1, 128), lambda b: (layer, 0, 0, 0)),
            pl.BlockSpec((None, NSA_CMP_STRIDE, 1, 128), lambda b: (layer, 0, 0, 0)),
        ],
        out_specs=pl.BlockSpec((None, nblk, 128), lambda b: (b, 0, 0)),
        out_shape=jax.ShapeDtypeStruct((B, nblk, 128), F32),
        compiler_params=_cparams(("parallel",)),
        name="nsa_compress",
    )(z, wlo, whi, pelo, pehi)


def _rowmax(s):
    return jnp.max(s, axis=-1, keepdims=True)


def _rowsum(s):
    return jnp.sum(s, axis=-1, keepdims=True)


def _nsa_kernel(q_ref, gl_ref, ks_ref, kw_ref, kvc_ref, bc_ref, tab_ref, ov_ref, ex_ref,
                o_ref, selx_ref):
    bi = pl.program_id(1)
    n_kt = selx_ref.shape[0]
    QB = Q_BLOCK
    R = MIX_HEADS * QB

    q = q_ref[...] * (HEAD_DIM ** -0.5)
    q4 = jnp.concatenate([q[:, h * HEAD_DIM:(h + 1) * HEAD_DIM] for h in range(MIX_HEADS)],
                         axis=0).astype(BF16)
    row = lax.broadcasted_iota(jnp.int32, (R, 128), 0)
    lane = lax.broadcasted_iota(jnp.int32, (R, 128), 1)
    i4 = row & (QB - 1)
    t4 = bi * QB + i4

    kvc = kvc_ref[...]
    kc = kvc[:, :HEAD_DIM].astype(BF16)
    vc = kvc[:, HEAD_DIM:].astype(BF16)
    s = _dot_nt(q4, kc) + bc_ref[...]
    d_c = t4 - (lane * NSA_CMP_STRIDE + (NSA_CMP_LEN - 1))
    s = jnp.where(d_c >= 0, s, NEG_INF)
    e = jnp.exp(s - _rowmax(s))
    p_c = e / _rowsum(e)
    p_c = p_c * (t4 >= NSA_CMP_LEN - 1).astype(F32)
    o_c = _dot(p_c.astype(BF16), vc)

    p_sum = p_c[0:QB] + p_c[QB:2 * QB] + p_c[2 * QB:3 * QB] + p_c[3 * QB:4 * QB]
    imp = _dot_x2(p_sum, ov_ref[...])
    blk = lax.broadcasted_iota(jnp.int32, (QB, 128), 1)
    cur = (bi * QB + lax.broadcasted_iota(jnp.int32, (QB, 128), 0)) >> 6
    forced = (blk == 0) | (blk == cur) | (blk == cur - 1)
    imp = jnp.where(forced, imp + FORCE_BONUS, imp)
    imp = jnp.where(blk <= cur, imp, NEG_INF)
    imp = jnp.where(blk < n_kt * 2, imp, NEG_BIG)
    blk_f = blk.astype(F32)
    sel = jnp.zeros((QB, 128), F32)
    for _ in range(NSA_TOP_N):
        mx = _rowmax(imp)
        idx = jnp.min(jnp.where(imp == mx, blk_f, 4096.0), axis=-1, keepdims=True)
        pick = blk_f == idx
        sel = jnp.where(pick, 1.0, sel)
        imp = jnp.where(pick, NEG_BIG, imp)
    selx = _dot(sel.astype(BF16), ex_ref[...])
    for kt in range(n_kt):
        selx_ref[kt] = selx[:, kt * 128:(kt + 1) * 128]

    def tile_qk(kref, kt):
        start = pl.multiple_of(kt * 128, 128)
        kv = kref[pl.ds(start, 128), :]
        return _dot_nt(q4, kv[:, :HEAD_DIM].astype(BF16)), kv[:, HEAD_DIM:].astype(BF16)

    def sel_tile(kt):
        m1 = selx_ref[kt]
        return jnp.concatenate([m1, m1, m1, m1], axis=0) > 0.5

    def online(carry, s, v):
        m, l, acc = carry
        m_new = jnp.maximum(m, _rowmax(s))
        alpha = jnp.exp(m - m_new)
        p = jnp.exp(s - m_new)
        return m_new, alpha * l + _rowsum(p), alpha * acc + _dot(p.astype(BF16), v)

    def first(s, v):
        m = _rowmax(s)
        p = jnp.exp(s - m)
        return m, _rowsum(p), _dot(p.astype(BF16), v)

    causal = lane <= i4

    s, v = tile_qk(ks_ref, bi)
    s = jnp.where(sel_tile(bi) & causal, s + tab_ref[0], NEG_INF)
    carry = first(s, v)

    def sel_body(kt, carry):
        s, v = tile_qk(ks_ref, kt)
        tb = tab_ref[jnp.where(kt == bi - 1, 1, 2)]
        s = jnp.where(sel_tile(kt), s + tb, NEG_INF)
        return online(carry, s, v)

    _, l_s, acc_s = lax.fori_loop(0, bi, sel_body, carry)
    o_s = acc_s / l_s

    s, v = tile_qk(kw_ref, bi)
    s = jnp.where(causal, s + tab_ref[0], NEG_INF)
    carry = first(s, v)
    n_off = NSA_WINDOW // 128

    def win_body(kt, carry):
        s, v = tile_qk(kw_ref, kt)
        off = bi - kt
        tb = tab_ref[jnp.where(off == 1, 1, 2)]
        s = jnp.where((off < n_off) | (lane > i4), s + tb, NEG_INF)
        return online(carry, s, v)

    _, l_w, acc_w = lax.fori_loop(jnp.maximum(bi - n_off, 0), bi, win_body, carry)
    o_w = acc_w / l_w

    g = jax.nn.sigmoid(gl_ref[...])
    outs = []
    for h in range(MIX_HEADS):
        rs = slice(h * QB, (h + 1) * QB)
        outs.append(g[:, 3 * h:3 * h + 1] * o_c[rs] + g[:, 3 * h + 1:3 * h + 2] * o_s[rs]
                    + g[:, 3 * h + 2:3 * h + 3] * o_w[rs])
    o_ref[...] = jnp.concatenate(outs, axis=1)


def nsa_attention(z, kvc, bias_c, tab, ov, ex, B, S):
    n_qb = S // Q_BLOCK
    R = MIX_HEADS * Q_BLOCK
    return pl.pallas_call(
        _nsa_kernel,
        grid=(B, n_qb),
        in_specs=[
            pl.BlockSpec((Q_BLOCK, MIX_WIDTH), lambda b, i: (b * n_qb + i, OFF_NSAQ // MIX_WIDTH)),
            pl.BlockSpec((Q_BLOCK, 128), lambda b, i: (b * n_qb + i, OFF_NSAG // 128)),
            pl.BlockSpec((S, 128), lambda b, i: (b, OFF_NSAKV // 128 + 1)),
            pl.BlockSpec((S, 128), lambda b, i: (b, OFF_NSAKV // 128 + 2)),
            pl.BlockSpec((None, S // NSA_CMP_STRIDE, 128), lambda b, i: (b, 0, 0)),
            pl.BlockSpec((None, R, 128), lambda b, i: (i, 0, 0)),
            pl.BlockSpec((3, R, 128), lambda b, i: (0, 0, 0)),
            pl.BlockSpec((128, 128), lambda b, i: (0, 0)),
            pl.BlockSpec((128, S), lambda b, i: (0, 0)),
        ],
        out_specs=pl.BlockSpec((Q_BLOCK, MIX_WIDTH), lambda b, i: (b * n_qb + i, 0)),
        out_shape=jax.ShapeDtypeStruct((B * S, MIX_WIDTH), F32),
        scratch_shapes=[pltpu.VMEM((n_qb, Q_BLOCK, 128), F32)],
        compiler_params=_cparams(("parallel", "arbitrary")),
        name="nsa_attention",
    )(z, z, z, z, kvc, bias_c, tab, ov, ex)


def _stack_heads(x, head_of_lane):
    return jnp.concatenate([jnp.where(head_of_lane == h, x, jnp.zeros_like(x))
                            for h in range(MIX_HEADS)], axis=0)


def _head_norm(y, seg_mean, eps):
    mu = _dot_x2(y, seg_mean)
    d = y - mu
    var = _dot_x2(d * d, seg_mean)
    return d * lax.rsqrt(var + eps)


def _retention_kernel(z_ref, cos_ref, sin_ref, indec_ref, qdec_ref, kdec_ref, cdec_ref, bd_ref,
                      segm_ref, ng_ref, o_ref, st_ref):
    L = RET_CHUNK

    @pl.when(pl.program_id(1) == 0)
    def _():
        st_ref[...] = jnp.zeros_like(st_ref)

    z = z_ref[...]
    q = z[:, 0:256]
    k = z[:, 256:512]
    v = z[:, 512:768]
    g = z[:, 768:1024]
    cos = cos_ref[...]
    sin = sin_ref[...]

    def rot(u):
        u1 = u[:, :128]
        u2 = u[:, 128:]
        return jnp.concatenate([u1 * cos - u2 * sin, u2 * cos + u1 * sin], axis=1)

    qr = rot(q) * (HEAD_DIM ** -0.5)
    kr = rot(k)
    lane = lax.broadcasted_iota(jnp.int32, (L, 256), 1)
    head_qk = (lane & 127) >> 5
    head_v = lane >> 6
    qb = qr.astype(BF16)
    kb = kr.astype(BF16)
    vb = v.astype(BF16)

    att = _dot_nt(_stack_heads(qb, head_qk), kb) * indec_ref[...]
    o_st = _dot(att.astype(BF16), vb)
    o = jnp.zeros((L, 256), F32)
    for h in range(MIX_HEADS):
        o = o + jnp.where(head_v == h, o_st[h * L:(h + 1) * L], 0.0)
    state = st_ref[...]
    o = o + _dot(qb, state.astype(BF16)) * qdec_ref[...]
    st_ref[...] = state * cdec_ref[...] + _dot_tn((kr * kdec_ref[...]).astype(BF16), vb) * bd_ref[...]

    o = _head_norm(o, segm_ref[...], RET_NORM_EPS) * ng_ref[...]
    o_ref[...] = g * jax.nn.sigmoid(g) * o


def retention(z, tabs, ng, layer, B, S):
    L = RET_CHUNK
    nc = S // L
    cos, sin, indec, qdec, kdec, cdec, bd, segm = tabs
    const2 = lambda b, c: (0, 0)
    return pl.pallas_call(
        _retention_kernel,
        grid=(B, nc),
        in_specs=[
            pl.BlockSpec((L, 1024), lambda b, c: (b * nc + c, OFF_RET // 1024)),
            pl.BlockSpec((L, 128), lambda b, c: (c, 0)),
            pl.BlockSpec((L, 128), lambda b, c: (c, 0)),
            pl.BlockSpec((MIX_HEADS * L, L), const2),
            pl.BlockSpec((L, 256), const2),
            pl.BlockSpec((L, 256), const2),
            pl.BlockSpec((1, 256), const2),
            pl.BlockSpec((256, 256), const2),
            pl.BlockSpec((256, 256), const2),
            pl.BlockSpec((None, 1, 256), lambda b, c: (layer, 0, 0)),
        ],
        out_specs=pl.BlockSpec((L, 256), lambda b, c: (b * nc + c, 0)),
        out_shape=jax.ShapeDtypeStruct((B * S, 256), F32),
        scratch_shapes=[pltpu.VMEM((256, 256), F32)],
        compiler_params=_cparams(("parallel", "arbitrary")),
        name="retention",
    )(z, cos, sin, indec, qdec, kdec, cdec, bd, segm, ng)


def _rwkv_kernel(z_ref, mu_ref, w2_ref, a2_ref, g2_ref, vec_ref, tri_ref, bdl_ref, wide_ref,
                 segs_ref, segm_ref, o_ref, st_ref, prev_ref):
    C = RWKV_CHUNK

    @pl.when(pl.program_id(1) == 0)
    def _():
        st_ref[...] = jnp.zeros_like(st_ref)
        prev_ref[...] = jnp.zeros_like(prev_ref)

    z = z_ref[...]
    rows = lax.broadcasted_iota(jnp.int32, (C, 1024), 0)
    zs = jnp.where(rows == 0, prev_ref[0:1, :], pltpu.roll(z, 1, axis=0))
    prev_ref[...] = jnp.broadcast_to(z[C - 1:C, :], prev_ref.shape)
    zf = z + (zs - z) * mu_ref[...]
    r = zf[:, 0:256]
    k = zf[:, 256:512]
    v = zf[:, 512:768]
    lora = zf[:, 768:896]

    vec = vec_ref[...]
    w0, a0, k_k, k_a, r_k, ln_g, ln_b = (vec[i:i + 1] for i in range(7))

    def lora_dot(x, w_ref):
        hi, lo = _split2(x)
        return _dot(hi, w_ref[0]) + _dot(lo, w_ref[0]) + _dot(hi, w_ref[1])

    wpre = w0 + lora_dot(jnp.tanh(lora), w2_ref)
    y = -wpre
    softplus = jnp.maximum(y, 0.0) + jnp.log(1.0 + jnp.exp(-jnp.abs(y)))
    w_log = -softplus - 0.5
    ld = -jnp.exp(w_log)
    a = jax.nn.sigmoid(a0 + lora_dot(lora, a2_ref))
    gate = lora_dot(jax.nn.sigmoid(lora), g2_ref)
    kk = k * k_k
    kk = kk / jnp.maximum(jnp.sqrt(_dot_x2(kk * kk, segs_ref[...])), 1e-12)
    k2 = k * (1.0 + (a - 1.0) * k_a)

    tri = tri_ref[...]
    l1 = ld.astype(BF16)
    r1 = ld - l1.astype(F32)
    l2 = r1.astype(BF16)
    l3 = (r1 - l2.astype(F32)).astype(BF16)
    cs = _dot(tri, l1) + _dot(tri, l2) + _dot(tri, l3)
    cs_end = cs[C - 1:C, :]
    e_neg = jnp.exp(-cs)
    e_end = jnp.exp(cs_end - cs)
    kka = kk * a
    a_t = -kk * jnp.exp(cs - ld)
    r_t = r * jnp.exp(cs)
    b_t = kka * e_neg
    k_t = k2 * e_neg
    b_g = kka * e_end
    k_g = k2 * e_end

    lane = lax.broadcasted_iota(jnp.int32, (C, 256), 1)
    head = lane >> 6
    stack = lambda x: _stack_heads(x.astype(BF16), head)
    a_st, b_st, k_st, v_st = stack(a_t), stack(b_t), stack(k_t), stack(v)

    n_pow = _dot_nt(a_st, b_st) * bdl_ref[...]
    ri = lax.broadcasted_iota(jnp.int32, (4 * C, 4 * C), 0)
    ci = lax.broadcasted_iota(jnp.int32, (4 * C, 4 * C), 1)
    t_inv = jnp.where(ri == ci, 1.0, 0.0) + n_pow
    sq = 2
    while sq < C:
        nb = n_pow.astype(BF16)
        n_pow = _dot(nb, nb)
        t_inv = t_inv + _dot(t_inv.astype(BF16), n_pow.astype(BF16))
        sq *= 2

    ar = jnp.concatenate([a_t, r_t], axis=0).astype(BF16)
    bk = jnp.concatenate([b_st, k_st], axis=0)
    wide = _dot_nt(ar, bk)
    strict = wide_ref[0]
    incl = wide_ref[1]
    a_ak = (wide[0:C, 4 * C:] * strict).astype(BF16)
    a_rb = (wide[C:, 0:4 * C] * incl).astype(BF16)
    a_rk = (wide[C:, 4 * C:] * incl).astype(BF16)

    state = st_ref[...]
    sb = state.astype(BF16)
    x = _dot_nt(a_t.astype(BF16), sb) + _dot(a_ak, v_st)
    u_st = _dot(t_inv.astype(BF16), stack(x)).astype(BF16)
    yv = _dot_nt(r_t.astype(BF16), sb) + _dot(a_rb, u_st) + _dot(a_rk, v_st)
    uv = jnp.concatenate([u_st, v_st], axis=0)
    bkg = jnp.concatenate([stack(b_g), stack(k_g)], axis=0)
    st_ref[...] = state * jnp.exp(cs_end) + _dot_tn(uv, bkg)

    yn = _head_norm(yv, segm_ref[...], RWKV_GN_EPS) * ln_g + ln_b
    yn = yn + _dot_x2(r * k2 * r_k, segs_ref[...]) * v
    o_ref[...] = yn * gate


def rwkv7(z, mu, w2, a2, g2, vec, consts, layer, B, S):
    C = RWKV_CHUNK
    nc = S // C
    tri, bdl, wide, segs, segm = consts
    const2 = lambda b, c: (0, 0)
    lw = lambda b, c: (layer, 0, 0, 0)
    return pl.pallas_call(
        _rwkv_kernel,
        grid=(B, nc),
        in_specs=[
            pl.BlockSpec((C, 1024), lambda b, c: (b * nc + c, OFF_RWKV // 1024)),
            pl.BlockSpec((None, 1, 1024), lambda b, c: (layer, 0, 0)),
            pl.BlockSpec((None, 2, 128, 256), lw),
            pl.BlockSpec((None, 2, 128, 256), lw),
            pl.BlockSpec((None, 2, 128, 256), lw),
            pl.BlockSpec((None, 8, 256), lambda b, c: (layer, 0, 0)),
            pl.BlockSpec((C, C), const2),
            pl.BlockSpec((4 * C, 4 * C), const2),
            pl.BlockSpec((2, C, 4 * C), lambda b, c: (0, 0, 0)),
            pl.BlockSpec((256, 256), const2),
            pl.BlockSpec((256, 256), const2),
        ],
        out_specs=pl.BlockSpec((C, 256), lambda b, c: (b * nc + c, 0)),
        out_shape=jax.ShapeDtypeStruct((B * S, 256), F32),
        scratch_shapes=[pltpu.VMEM((256, 256), F32), pltpu.VMEM((8, 1024), F32)],
        compiler_params=_cparams(("parallel", "arbitrary")),
        name="rwkv7",
    )(z, mu, w2, a2, g2, vec, tri, bdl, wide, segs, segm)


def _merge_kernel(x_ref, zc_ref, zg_ref, on_ref, or_ref, ow_ref, cw_ref, wb_ref, wo_ref, g_ref,
                  o_ref, carry_ref):
    tm = x_ref.shape[0]

    @pl.when(pl.program_id(1) == 0)
    def _():
        carry_ref[...] = jnp.zeros_like(carry_ref)

    zc = zc_ref[...]
    b_g = zc[:, 0:256]
    u = zc[:, 256:512] * zc[:, 512:768]
    prev = carry_ref[...]
    rows = lax.broadcasted_iota(jnp.int32, (tm, 256), 0)
    u1 = jnp.where(rows == 0, prev[7:8], pltpu.roll(u, 1, axis=0))
    u2 = jnp.where(rows == 0, prev[6:7], jnp.where(rows == 1, prev[7:8], pltpu.roll(u, 2, axis=0)))
    carry_ref[...] = u[tm - 8:tm]
    cw = cw_ref[...]
    o_conv = b_g * (cw[0:1] * u2 + cw[1:2] * u1 + cw[2:3] * u)

    branches = (on_ref[...], or_ref[...], ow_ref[...], o_conv)
    merged = jnp.zeros((tm, D_MODEL), F32)
    for m in range(N_BRANCHES):
        gm = jax.nn.sigmoid(zg_ref[:, m * D_MODEL:(m + 1) * D_MODEL])
        merged = merged + gm * _dot(branches[m].astype(BF16), wb_ref[m])
    y = _dot(merged.astype(BF16), wo_ref[...])
    o_ref[...] = x_ref[...] + _rms(y, g_ref[...])


def merge_mix(x, z, o_nsa, o_ret, o_rwkv, conv_w, w_branch, w_out, g_post, layer, B, S, tm):
    nt = S // tm
    rowmap = lambda b, i: (b * nt + i, 0)
    return pl.pallas_call(
        _merge_kernel,
        grid=(B, nt),
        in_specs=[
            pl.BlockSpec((tm, D_MODEL), rowmap),
            pl.BlockSpec((tm, 768), lambda b, i: (b * nt + i, OFF_CONV // 768)),
            pl.BlockSpec((tm, 4096), lambda b, i: (b * nt + i, 0)),
            pl.BlockSpec((tm, 256), rowmap),
            pl.BlockSpec((tm, 256), rowmap),
            pl.BlockSpec((tm, 256), rowmap),
            pl.BlockSpec((None, 8, 256), lambda b, i: (layer, 0, 0)),
            pl.BlockSpec((None, N_BRANCHES, MIX_WIDTH, D_MODEL), lambda b, i: (layer, 0, 0, 0)),
            pl.BlockSpec((None, D_MODEL, D_MODEL), lambda b, i: (layer, 0, 0)),
            pl.BlockSpec((None, 1, D_MODEL), lambda b, i: (layer, 0, 0)),
        ],
        out_specs=pl.BlockSpec((tm, D_MODEL), rowmap),
        out_shape=jax.ShapeDtypeStruct((B * S, D_MODEL), F32),
        scratch_shapes=[pltpu.VMEM((8, 256), F32)],
        compiler_params=_cparams(("parallel", "arbitrary")),
        name="merge_mix",
    )(x, z, z, o_nsa, o_ret, o_rwkv, conv_w, w_branch, w_out, g_post)


def _xattn_kernel(x_ref, kv_ref, gpre_ref, wq_ref, wo_ref, gpost_ref, o_ref):
    x = x_ref[...]
    h = _rms(x, gpre_ref[...]).astype(BF16)
    q = (_dot(h, wq_ref[...]) * (XA_HEAD_DIM ** -0.5)).astype(BF16)
    outs = []
    for hd in range(XA_HEADS):
        cs = slice(hd * XA_HEAD_DIM, (hd + 1) * XA_HEAD_DIM)
        kh = kv_ref[:, hd * XA_HEAD_DIM:(hd + 1) * XA_HEAD_DIM]
        vh = kv_ref[:, D_MODEL + hd * XA_HEAD_DIM:D_MODEL + (hd + 1) * XA_HEAD_DIM]
        s = _dot_nt(q[:, cs], kh)
        e = jnp.exp(s - _rowmax(s))
        p = e / _rowsum(e)
        outs.append(_dot(p.astype(BF16), vh).astype(BF16))
    o = jnp.concatenate(outs, axis=1)
    y = _dot(o, wo_ref[...])
    o_ref[...] = x + _rms(y, gpost_ref[...])


def cross_attention(x, kv, g_pre, wq, wo, g_post, layer, B, S, M, tm):
    nt = S // tm
    rowmap = lambda b, i: (b * nt + i, 0)
    lmap = lambda b, i: (layer, 0, 0)
    return pl.pallas_call(
        _xattn_kernel,
        grid=(B, nt),
        in_specs=[
            pl.BlockSpec((tm, D_MODEL), rowmap),
            pl.BlockSpec((M, 2 * D_MODEL), lambda b, i: (b, 0)),
            pl.BlockSpec((None, 1, D_MODEL), lmap),
            pl.BlockSpec((None, D_MODEL, D_MODEL), lmap),
            pl.BlockSpec((None, D_MODEL, D_MODEL), lmap),
            pl.BlockSpec((None, 1, D_MODEL), lmap),
        ],
        out_specs=pl.BlockSpec((tm, D_MODEL), rowmap),
        out_shape=jax.ShapeDtypeStruct((B * S, D_MODEL), F32),
        compiler_params=_cparams(("parallel", "arbitrary")),
        name="cross_attention",
    )(x, kv, g_pre, wq, wo, g_post)


def _mlp_kernel(x_ref, gpre_ref, w1_ref, w2_ref, gpost_ref, o_ref):
    x = x_ref[...]
    h = _rms(x, gpre_ref[...]).astype(BF16)
    tf = 1024
    acc = jnp.zeros(x.shape, F32)
    for f in range(D_FF // tf):
        a = jnp.maximum(_dot(h, w1_ref[:, f * tf:(f + 1) * tf]), 0.0)
        acc = acc + _dot((a * a).astype(BF16), w2_ref[f * tf:(f + 1) * tf, :])
    o_ref[...] = x + _rms(acc, gpost_ref[...])


def mlp(x, g_pre, w1, w2, g_post, layer, tm):
    M = x.shape[0]
    lmap = lambda i: (layer, 0, 0)
    return pl.pallas_call(
        _mlp_kernel,
        grid=(M // tm,),
        in_specs=[
            pl.BlockSpec((tm, D_MODEL), lambda i: (i, 0)),
            pl.BlockSpec((None, 1, D_MODEL), lmap),
            pl.BlockSpec((None, D_MODEL, D_FF), lmap, pipeline_mode=pl.Buffered(1)),
            pl.BlockSpec((None, D_FF, D_MODEL), lmap, pipeline_mode=pl.Buffered(1)),
            pl.BlockSpec((None, 1, D_MODEL), lmap),
        ],
        out_specs=pl.BlockSpec((tm, D_MODEL), lambda i: (i, 0)),
        out_shape=jax.ShapeDtypeStruct((M, D_MODEL), F32),
        compiler_params=_cparams(("parallel",)),
        name="mlp",
    )(x, g_pre, w1, w2, g_post)


def _t5_bucket(dist):
    n = jnp.maximum(dist, 0)
    max_exact = N_BUCKETS // 2
    nf = jnp.maximum(n, 1).astype(F32)
    large = max_exact + (jnp.log(nf / max_exact) / math.log(MAX_DISTANCE / max_exact)
                         * (N_BUCKETS - max_exact)).astype(jnp.int32)
    large = jnp.minimum(large, N_BUCKETS - 1)
    return jnp.where(n < max_exact, n, large)


def _nsa_tables(rel_bias, S):
    n_qb = S // Q_BLOCK
    bias_f = rel_bias.astype(F32)
    i = jnp.arange(Q_BLOCK)
    n = jnp.arange(128)
    t = (jnp.arange(n_qb) * Q_BLOCK)[:, None, None] + i[None, :, None]
    d_c = t - (n[None, None, :] * NSA_CMP_STRIDE + NSA_CMP_LEN - 1)
    bias_c = bias_f[_t5_bucket(d_c)]
    bias_c = bias_c.transpose(0, 3, 1, 2).reshape(n_qb, MIX_HEADS * Q_BLOCK, 128)
    d0 = i[:, None] - i[None, :]
    tab = jnp.stack([bias_f[_t5_bucket(d0)], bias_f[_t5_bucket(d0 + Q_BLOCK)],
                     jnp.broadcast_to(bias_f[N_BUCKETS - 1], (Q_BLOCK, Q_BLOCK, MIX_HEADS))])
    tab = tab.transpose(0, 3, 1, 2).reshape(3, MIX_HEADS * Q_BLOCK, Q_BLOCK)
    cs = np.arange(128)[:, None] * NSA_CMP_STRIDE
    ss = np.arange(128)[None, :] * NSA_SLC_LEN
    ov = np.clip(np.minimum(cs + NSA_CMP_LEN, ss + NSA_SLC_LEN) - np.maximum(cs, ss), 0, None)
    ov = ov.astype(np.float32) / NSA_CMP_LEN
    n_cmp = (S - NSA_CMP_LEN) // NSA_CMP_STRIDE + 1
    ov[n_cmp:, :] = 0.0
    ov[:, S // NSA_SLC_LEN:] = 0.0
    ex = (np.arange(128)[:, None] == (np.arange(S)[None, :] // NSA_SLC_LEN)).astype(np.float32)
    return bias_c, tab, jnp.asarray(ov, BF16), jnp.asarray(ex, BF16)


def _ret_tables(S):
    L = RET_CHUNK
    H = MIX_HEADS
    half = HEAD_DIM // 2
    pos = jnp.arange(S)
    inv_freq = ROPE_BASE ** (-jnp.arange(half, dtype=F32) / half)
    ang = pos.astype(F32)[:, None] * inv_freq[None, :]
    cos = jnp.tile(jnp.cos(ang), (1, H))
    sin = jnp.tile(jnp.sin(ang), (1, H))
    lg = jnp.log(1.0 - 2.0 ** (-5.0 - jnp.arange(H, dtype=F32)))
    n = jnp.arange(L, dtype=F32)
    diff = n[:, None] - n[None, :]
    inner = jnp.where(diff >= 0, jnp.exp(jnp.maximum(diff, 0.0)[None] * lg[:, None, None]), 0.0)
    indec = inner.reshape(H * L, L)
    q_decay = jnp.exp((n + 1.0)[None, :] * lg[:, None])
    k_decay = jnp.exp((L - 1.0 - n)[None, :] * lg[:, None])
    chunk_decay = jnp.exp(L * lg)
    lane = np.arange(256)
    head_v = lane // HEAD_DIM
    head_qk = (lane % 128) // half
    qdec = q_decay.T[:, head_v]
    kdec = k_decay.T[:, head_qk]
    cdec = chunk_decay[head_v][None, :]
    bd = jnp.asarray((head_qk[:, None] == head_v[None, :]).astype(np.float32))
    segm = jnp.asarray((head_v[:, None] == head_v[None, :]).astype(np.float32) / HEAD_DIM, BF16)
    return cos, sin, indec, qdec, kdec, cdec, bd, segm


def _rwkv_consts():
    C = RWKV_CHUNK
    t = np.arange(C)
    tri = (t[:, None] >= t[None, :]).astype(np.float32)
    r = np.arange(4 * C)
    bdl = ((r[:, None] // C == r[None, :] // C) & (r[:, None] % C > r[None, :] % C)).astype(np.float32)
    strict = (t[:, None] > (r[None, :] % C)).astype(np.float32)
    incl = (t[:, None] >= (r[None, :] % C)).astype(np.float32)
    lane = np.arange(256) // HEAD_DIM
    seg = (lane[:, None] == lane[None, :]).astype(np.float32)
    return (jnp.asarray(tri, BF16), jnp.asarray(bdl), jnp.asarray(np.stack([strict, incl])),
            jnp.asarray(seg, BF16), jnp.asarray(seg / HEAD_DIM, BF16))


def _hi_lo(w):
    hi = w.astype(BF16)
    lo = (w - hi.astype(F32)).astype(BF16)
    return jnp.stack([hi, lo], axis=1)


def _pad_rows(w, top, total):
    return jnp.pad(w, ((0, 0), (top, total - top - w.shape[1]), (0, 0)))


def _layout_w_in(w_in):
    L = w_in.shape[0]
    o = 0
    nsa_q = w_in[:, :, o:o + 256]; o += 256
    nsa_kv = w_in[:, :, o:o + 384]; o += 384
    nsa_g = w_in[:, :, o:o + 12]; o += 12
    ret = w_in[:, :, o:o + 1024]; o += 1024
    rwkv = w_in[:, :, o:o + RWKV_COLS]; o += RWKV_COLS
    conv = w_in[:, :, o:o + 768]; o += 768
    gate = w_in[:, :, o:o + 4096]

    def rot_perm(w):
        return w.reshape(L, D_MODEL, MIX_HEADS, 2, HEAD_DIM // 2).transpose(0, 1, 3, 2, 4).reshape(L, D_MODEL, 256)

    ret = jnp.concatenate([rot_perm(ret[:, :, 0:256]), rot_perm(ret[:, :, 256:512]), ret[:, :, 512:]], axis=2)
    zeros = lambda n: jnp.zeros((L, D_MODEL, n), w_in.dtype)
    out = jnp.concatenate([gate, ret, rwkv, zeros(1024 - RWKV_COLS), conv, nsa_q, nsa_kv, nsa_g,
                           zeros(128 - 12)], axis=2)
    assert out.shape[2] == Z_COLS
    return out.astype(BF16)


def _layout_cmp(cmp_w, cmp_pe):
    L = cmp_w.shape[0]
    wk = cmp_w[:, 0]
    wv = cmp_w[:, 1]
    zero = jnp.zeros_like(wk)
    blk = jnp.concatenate([jnp.concatenate([wk, zero], axis=3), jnp.concatenate([zero, wv], axis=3)], axis=2)
    pe2 = jnp.concatenate([cmp_pe, cmp_pe], axis=2)[:, :, None, :]
    s = NSA_CMP_STRIDE
    return blk[:, :s].astype(BF16), blk[:, s:].astype(BF16), pe2[:, :s], pe2[:, s:]


def kernel(x, mem, ln_mix_pre, w_in, nsa_cmp_w, nsa_cmp_pe, rel_bias, ret_norm_g, rwkv_mu, rwkv_w0, rwkv_w2, rwkv_a0, rwkv_a2, rwkv_g2, rwkv_k_k, rwkv_k_a, rwkv_r_k, rwkv_ln_g, rwkv_ln_b, conv_w, w_branch, w_mix_out, ln_mix_post, ln_xa_pre, ln_mem, xa_wq, xa_wkv, xa_wo, ln_xa_post, ln_mlp_pre, mlp_w1, mlp_w2, ln_mlp_post):
    B, S, D = x.shape
    M = mem.shape[1]
    depth = w_in.shape[0]
    row = lambda g: g[:, None, :]

    w_in_b = _layout_w_in(w_in)
    cmp_lo, cmp_hi, pe_lo, pe_hi = _layout_cmp(nsa_cmp_w, nsa_cmp_pe)
    nsa_tabs = _nsa_tables(rel_bias, S)
    ret_tabs = _ret_tables(S)
    rwkv_consts = _rwkv_consts()
    mu = jnp.pad(rwkv_mu, ((0, 0), (0, 1024 - RWKV_COLS)))[:, None, :]
    w2 = _hi_lo(_pad_rows(rwkv_w2, 0, 128))
    a2 = _hi_lo(_pad_rows(rwkv_a2, RWKV_DECAY_LORA, 128))
    g2 = _hi_lo(_pad_rows(rwkv_g2, RWKV_DECAY_LORA + RWKV_AAA_LORA, 128))
    vec = jnp.stack([rwkv_w0, rwkv_a0, rwkv_k_k, rwkv_k_a, rwkv_r_k, rwkv_ln_g, rwkv_ln_b,
                     jnp.zeros_like(rwkv_w0)], axis=1)
    conv_p = jnp.pad(conv_w, ((0, 0), (0, 8 - CONV_WIDTH), (0, 0)))
    w_branch_b = w_branch.astype(BF16)
    w_mix_out_b = w_mix_out.astype(BF16)
    xa_wq_b = xa_wq.astype(BF16)
    xa_wkv_b = xa_wkv.astype(BF16)
    xa_wo_b = xa_wo.astype(BF16)
    mlp_w1_b = mlp_w1.astype(BF16)
    mlp_w2_b = mlp_w2.astype(BF16)

    xf = x.reshape(B * S, D)
    memf = mem.reshape(B * M, D)
    for l in range(depth):
        z = norm_matmul(xf, row(ln_mix_pre), w_in_b, l, tm=1024, tn=1280, out_dtype=F32)
        kvc = nsa_compress(z, cmp_lo, cmp_hi, pe_lo, pe_hi, l, B, S)
        o_nsa = nsa_attention(z, kvc, *nsa_tabs, B, S)
        o_ret = retention(z, ret_tabs, row(ret_norm_g), l, B, S)
        o_rwkv = rwkv7(z, mu, w2, a2, g2, vec, rwkv_consts, l, B, S)
        xf = merge_mix(xf, z, o_nsa, o_ret, o_rwkv, conv_p, w_branch_b, w_mix_out_b,
                       row(ln_mix_post), l, B, S, tm=256)
        kvm = norm_matmul(memf, row(ln_mem), xa_wkv_b, l, tm=M, tn=1024, out_dtype=BF16)
        xf = cross_attention(xf, kvm, row(ln_xa_pre), xa_wq_b, xa_wo_b, row(ln_xa_post), l, B, S, M, tm=512)
        xf = mlp(xf, row(ln_mlp_pre), mlp_w1_b, mlp_w2_b, row(ln_mlp_post), l, tm=512)
    return xf.reshape(B, S, D)
```

```python
import functools
import math

import numpy as np
import jax
import jax.numpy as jnp
from jax import lax
from jax.experimental import pallas as pl
from jax.experimental.pallas import tpu as pltpu

F32 = jnp.float32
BF16 = jnp.bfloat16

D_MODEL = 1024
N_BRANCHES = 4
MIX_WIDTH = 256
HEAD_DIM = 64
MIX_HEADS = 4

NSA_CMP_LEN = 32
NSA_CMP_STRIDE = 16
NSA_SLC_LEN = 64
NSA_TOP_N = 8
NSA_WINDOW = 512
Q_BLOCK = 128
FORCE_BONUS = 1e4
N_BUCKETS = 32
MAX_DISTANCE = 128

RET_CHUNK = 128
ROPE_BASE = 10000.0
RET_NORM_EPS = 1e-5

RWKV_DECAY_LORA = 32
RWKV_AAA_LORA = 32
RWKV_GATE_LORA = 64
RWKV_GN_EPS = 64e-5
RWKV_COLS = 3 * MIX_WIDTH + RWKV_DECAY_LORA + RWKV_AAA_LORA + RWKV_GATE_LORA
RWKV_CHUNK = 64

CONV_WIDTH = 3
XA_HEADS = 4
XA_HEAD_DIM = D_MODEL // XA_HEADS
D_FF = 4 * D_MODEL

RMS_EPS = 1e-6
NEG_INF = -1e30
NEG_BIG = -3e38

OFF_GATE = 0
OFF_RET = 4096
OFF_RWKV = 5120
OFF_CONV = 6144
OFF_NSAQ = 6912
OFF_NSAKV = 7168
OFF_NSAG = 7552
Z_COLS = 7680

VMEM_LIMIT = 56 * 1024 * 1024


def _cparams(sem):
    return pltpu.CompilerParams(dimension_semantics=sem, vmem_limit_bytes=VMEM_LIMIT)


def _dot(a, b):
    return jnp.dot(a, b, preferred_element_type=F32)


def _dot_nt(a, b):
    return lax.dot_general(a, b, (((1,), (1,)), ((), ())), preferred_element_type=F32)


def _dot_tn(a, b):
    return lax.dot_general(a, b, (((0,), (0,)), ((), ())), preferred_element_type=F32)


def _split2(x):
    hi = x.astype(BF16)
    lo = (x - hi.astype(F32)).astype(BF16)
    return hi, lo


def _dot_x2(x, w_bf16):
    hi, lo = _split2(x)
    return _dot(hi, w_bf16) + _dot(lo, w_bf16)


def _rms(x, g):
    ms = jnp.mean(x * x, axis=-1, keepdims=True)
    return x * lax.rsqrt(ms + RMS_EPS) * g


def _norm_matmul_kernel(x_ref, g_ref, w_ref, o_ref, h_ref):
    @pl.when(pl.program_id(1) == 0)
    def _():
        h_ref[...] = _rms(x_ref[...], g_ref[...]).astype(BF16)

    o_ref[...] = _dot(h_ref[...], w_ref[...]).astype(o_ref.dtype)


def norm_matmul(x, g, w, layer, tm, tn, out_dtype):
    M, D = x.shape
    N = w.shape[2]
    return pl.pallas_call(
        _norm_matmul_kernel,
        grid=(M // tm, N // tn),
        in_specs=[
            pl.BlockSpec((tm, D), lambda i, j: (i, 0)),
            pl.BlockSpec((None, 1, D), lambda i, j: (layer, 0, 0)),
            pl.BlockSpec((None, D, tn), lambda i, j: (layer, 0, j)),
        ],
        out_specs=pl.BlockSpec((tm, tn), lambda i, j: (i, j)),
        out_shape=jax.ShapeDtypeStruct((M, N), out_dtype),
        scratch_shapes=[pltpu.VMEM((tm, D), BF16)],
        compiler_params=_cparams(("parallel", "arbitrary")),
        name="norm_matmul",
    )(x, g, w)


def _nsa_compress_kernel(kvc_ref, kvs_ref, kvw_ref, wlo_ref, whi_ref, pelo_ref, pehi_ref,
                         kc_ref, vct_ref, ks_ref, vst_ref, kw_ref, vwt_ref):
    nblk = kc_ref.shape[0]
    n_kt = vst_ref.shape[0]
    ylo = jnp.zeros((nblk, 128), F32)
    yhi = jnp.zeros((nblk, 128), F32)
    for r in range(NSA_CMP_STRIDE):
        xr = kvc_ref[pl.ds(r, nblk, stride=NSA_CMP_STRIDE), :]
        ylo = ylo + _dot((xr + pelo_ref[r]).astype(BF16), wlo_ref[r])
        yhi = yhi + _dot((xr + pehi_ref[r]).astype(BF16), whi_ref[r])
    y = ylo + pltpu.roll(yhi, nblk - 1, axis=0)
    kc_ref[...] = y.astype(BF16)
    vct_ref[...] = y.T[HEAD_DIM:, :].astype(BF16)
    ks_ref[...] = kvs_ref[...].astype(BF16)
    kw_ref[...] = kvw_ref[...].astype(BF16)
    for kt in range(n_kt):
        rs = slice(kt * 128, (kt + 1) * 128)
        vst_ref[kt] = kvs_ref[rs, :].T[HEAD_DIM:, :].astype(BF16)
        vwt_ref[kt] = kvw_ref[rs, :].T[HEAD_DIM:, :].astype(BF16)


def nsa_compress(z, wlo, whi, pelo, pehi, layer, B, S):
    nblk = S // NSA_CMP_STRIDE
    n_kt = S // 128
    c0 = OFF_NSAKV // 128
    wmap = lambda b: (layer, 0, 0, 0)
    b3 = lambda b: (b, 0, 0)
    b4 = lambda b: (b, 0, 0, 0)
    return pl.pallas_call(
        _nsa_compress_kernel,
        grid=(B,),
        in_specs=[
            pl.BlockSpec((S, 128), lambda b: (b, c0)),
            pl.BlockSpec((S, 128), lambda b: (b, c0 + 1)),
            pl.BlockSpec((S, 128), lambda b: (b, c0 + 2)),
            pl.BlockSpec((None, NSA_CMP_STRIDE, 128, 128), wmap),
            pl.BlockSpec((None, NSA_CMP_STRIDE, 128, 128), wmap),
            pl.BlockSpec((None, NSA_CMP_STRIDE, 1, 128), wmap),
            pl.BlockSpec((None, NSA_CMP_STRIDE, 1, 128), wmap),
        ],
        out_specs=[
            pl.BlockSpec((None, nblk, 128), b3),
            pl.BlockSpec((None, HEAD_DIM, nblk), b3),
            pl.BlockSpec((None, S, 128), b3),
            pl.BlockSpec((None, n_kt, HEAD_DIM, 128), b4),
            pl.BlockSpec((None, S, 128), b3),
            pl.BlockSpec((None, n_kt, HEAD_DIM, 128), b4),
        ],
        out_shape=[
            jax.ShapeDtypeStruct((B, nblk, 128), BF16),
            jax.ShapeDtypeStruct((B, HEAD_DIM, nblk), BF16),
            jax.ShapeDtypeStruct((B, S, 128), BF16),
            jax.ShapeDtypeStruct((B, n_kt, HEAD_DIM, 128), BF16),
            jax.ShapeDtypeStruct((B, S, 128), BF16),
            jax.ShapeDtypeStruct((B, n_kt, HEAD_DIM, 128), BF16),
        ],
        compiler_params=_cparams(("parallel",)),
        name="nsa_compress",
    )(z, z, z, wlo, whi, pelo, pehi)


def _rowmax(s):
    return jnp.max(s, axis=-1, keepdims=True)


def _rowsum(s):
    return jnp.sum(s, axis=-1, keepdims=True)


def _colmax(s):
    return jnp.max(s, axis=0, keepdims=True)


def _colsum(s):
    return jnp.sum(s, axis=0, keepdims=True)


def _nsa_kernel(q_ref, gl_ref, kc_ref, vct_ref, ks_ref, vst_ref, kw_ref, vwt_ref, bc_ref, tab_ref,
                ovt_ref, ext_ref, o_ref, selx_ref):
    bi = pl.program_id(1)
    n_kt = selx_ref.shape[0]
    n_blk = ovt_ref.shape[0]
    n_cmp = kc_ref.shape[0]
    QB = Q_BLOCK
    R = MIX_HEADS * QB

    qt = (q_ref[...] * (HEAD_DIM ** -0.5)).T
    q4 = jnp.concatenate([qt[h * HEAD_DIM:(h + 1) * HEAD_DIM] for h in range(MIX_HEADS)],
                         axis=1).astype(BF16)
    key = lax.broadcasted_iota(jnp.int32, (128, R), 0)
    i4 = lax.broadcasted_iota(jnp.int32, (128, R), 1) & (QB - 1)
    causal = key <= i4
    n_off = NSA_WINDOW // 128

    def tile_qk(kref, kt):
        start = pl.multiple_of(kt * 128, 128)
        return _dot(kref[pl.ds(start, 128), :HEAD_DIM], q4)

    def tile_bias(kt):
        return tab_ref[jnp.where(kt == bi - 1, 1, 2)]

    def first(s, vt):
        m = _colmax(s)
        p = jnp.exp(s - m)
        return m, _colsum(p), _dot(vt, p.astype(BF16))

    def online2(carry, s0, vt0, s1, vt1):
        m, l, acc = carry
        m_new = jnp.maximum(m, jnp.maximum(_colmax(s0), _colmax(s1)))
        alpha = jnp.exp(m - m_new)
        p0 = jnp.exp(s0 - m_new)
        p1 = jnp.exp(s1 - m_new)
        l = alpha * l + _colsum(p0) + _colsum(p1)
        acc = alpha * acc + _dot(vt0, p0.astype(BF16)) + _dot(vt1, p1.astype(BF16))
        return m_new, l, acc

    s = jnp.where(causal, tile_qk(kw_ref, bi) + tab_ref[0], NEG_INF)
    carry_w = first(s, vwt_ref[bi])

    g0 = pl.multiple_of((n_kt - 1 - bi) * 8, 8)
    s = _dot(kc_ref[:, :HEAD_DIM], q4) + bc_ref[pl.ds(g0, n_cmp), :]
    nrow = lax.broadcasted_iota(jnp.int32, (n_cmp, R), 0)
    tq = bi * QB + (lax.broadcasted_iota(jnp.int32, (n_cmp, R), 1) & (QB - 1))
    s = jnp.where(tq - (nrow * NSA_CMP_STRIDE + (NSA_CMP_LEN - 1)) >= 0, s, NEG_INF)
    e = jnp.exp(s - _colmax(s))
    p_c = e / _colsum(e)
    p_c = p_c * (tq >= NSA_CMP_LEN - 1).astype(F32)
    o_c = _dot(vct_ref[...], p_c.astype(BF16))

    p_sum = p_c[:, 0:QB] + p_c[:, QB:2 * QB] + p_c[:, 2 * QB:3 * QB] + p_c[:, 3 * QB:4 * QB]
    p_hi, p_lo = _split2(p_sum)
    imp = _dot(ovt_ref[...], p_hi) + _dot(ovt_ref[...], p_lo)
    blk = lax.broadcasted_iota(jnp.int32, (n_blk, QB), 0)
    cur = (bi * QB + lax.broadcasted_iota(jnp.int32, (n_blk, QB), 1)) >> 6
    forced = (blk == 0) | (blk == cur) | (blk == cur - 1)
    imp = jnp.where(forced, imp + FORCE_BONUS, imp)
    imp = jnp.where(blk <= cur, imp, NEG_INF)
    blk_f = blk.astype(F32)
    sel = jnp.zeros((n_blk, QB), F32)
    for _ in range(NSA_TOP_N):
        mx = _colmax(imp)
        idx = jnp.min(jnp.where(imp == mx, blk_f, 4096.0), axis=0, keepdims=True)
        pick = blk_f == idx
        sel = jnp.where(pick, 1.0, sel)
        imp = jnp.where(pick, NEG_BIG, imp)
    selx = _dot(ext_ref[...], sel.astype(BF16))
    for kt in range(n_kt):
        selx_ref[kt] = selx[kt * 128:(kt + 1) * 128, :]

    def sel_tile(kt, thr):
        m1 = selx_ref[kt]
        return jnp.concatenate([m1, m1, m1, m1], axis=1) > thr

    s = jnp.where(sel_tile(bi, 0.5) & causal, tile_qk(ks_ref, bi) + tab_ref[0], NEG_INF)
    carry_s = first(s, vst_ref[bi])

    def sel_body(j, carry):
        k0 = 2 * j
        k1 = jnp.minimum(k0 + 1, bi)
        thr1 = jnp.where(k0 + 1 < bi, 0.5, 2.0)
        s0 = jnp.where(sel_tile(k0, 0.5), tile_qk(ks_ref, k0) + tile_bias(k0), NEG_INF)
        s1 = jnp.where(sel_tile(k1, thr1), tile_qk(ks_ref, k1) + tile_bias(k1), NEG_INF)
        return online2(carry, s0, vst_ref[k0], s1, vst_ref[k1])

    _, l_s, acc_s = lax.fori_loop(0, (bi + 1) // 2, sel_body, carry_s)
    o_s = acc_s * (1.0 / l_s)

    lo = jnp.maximum(bi - n_off, 0)

    def win_mask(kt, valid):
        shift = jnp.where(valid, jnp.where(bi - kt < n_off, -4096, 0), 4096)
        return key > i4 + shift

    def win_body(j, carry):
        k0 = lo + 2 * j
        k1 = jnp.minimum(k0 + 1, bi)
        s0 = jnp.where(win_mask(k0, True), tile_qk(kw_ref, k0) + tile_bias(k0), NEG_INF)
        s1 = jnp.where(win_mask(k1, k0 + 1 < bi), tile_qk(kw_ref, k1) + tile_bias(k1), NEG_INF)
        return online2(carry, s0, vwt_ref[k0], s1, vwt_ref[k1])

    _, l_w, acc_w = lax.fori_loop(0, (bi - lo + 1) // 2, win_body, carry_w)
    o_w = acc_w * (1.0 / l_w)

    g = jax.nn.sigmoid(gl_ref[...].T)
    outs = []
    for h in range(MIX_HEADS):
        cs = slice(h * QB, (h + 1) * QB)
        outs.append(g[3 * h:3 * h + 1] * o_c[:, cs] + g[3 * h + 1:3 * h + 2] * o_s[:, cs]
                    + g[3 * h + 2:3 * h + 3] * o_w[:, cs])
    o_ref[...] = jnp.concatenate(outs, axis=0).T


def nsa_attention(z, prep, bias_c, tab, ovt, ext, B, S):
    n_qb = S // Q_BLOCK
    n_kt = S // 128
    n_cmp = S // NSA_CMP_STRIDE
    R = MIX_HEADS * Q_BLOCK
    kc, vct, ks, vst, kw, vwt = prep
    b3 = lambda b, i: (b, 0, 0)
    b4 = lambda b, i: (b, 0, 0, 0)
    return pl.pallas_call(
        _nsa_kernel,
        grid=(B, n_qb),
        in_specs=[
            pl.BlockSpec((Q_BLOCK, MIX_WIDTH), lambda b, i: (b * n_qb + i, OFF_NSAQ // MIX_WIDTH)),
            pl.BlockSpec((Q_BLOCK, 128), lambda b, i: (b * n_qb + i, OFF_NSAG // 128)),
            pl.BlockSpec((None, n_cmp, 128), b3),
            pl.BlockSpec((None, HEAD_DIM, n_cmp), b3),
            pl.BlockSpec((None, S, 128), b3),
            pl.BlockSpec((None, n_kt, HEAD_DIM, 128), b4),
            pl.BlockSpec((None, S, 128), b3),
            pl.BlockSpec((None, n_kt, HEAD_DIM, 128), b4),
            pl.BlockSpec(bias_c.shape, lambda b, i: (0, 0)),
            pl.BlockSpec((3, 128, R), lambda b, i: (0, 0, 0)),
            pl.BlockSpec(ovt.shape, lambda b, i: (0, 0)),
            pl.BlockSpec(ext.shape, lambda b, i: (0, 0)),
        ],
        out_specs=pl.BlockSpec((Q_BLOCK, MIX_WIDTH), lambda b, i: (b * n_qb + i, 0)),
        out_shape=jax.ShapeDtypeStruct((B * S, MIX_WIDTH), F32),
        scratch_shapes=[pltpu.VMEM((n_kt, 128, Q_BLOCK), F32)],
        compiler_params=_cparams(("parallel", "arbitrary")),
        name="nsa_attention",
    )(z, z, kc, vct, ks, vst, kw, vwt, bias_c, tab, ovt, ext)


def _stack_heads(x, head_of_lane):
    return jnp.concatenate([jnp.where(head_of_lane == h, x, jnp.zeros_like(x))
                            for h in range(MIX_HEADS)], axis=0)


def _head_norm(y, seg_mean, eps):
    mu = _dot_x2(y, seg_mean)
    d = y - mu
    var = _dot_x2(d * d, seg_mean)
    return d * lax.rsqrt(var + eps)


def _retention_kernel(z_ref, cos_ref, sin_ref, indec_ref, qdec_ref, kdec_ref, cdec_ref, bd_ref,
                      segm_ref, ng_ref, o_ref, st_ref):
    L = RET_CHUNK

    @pl.when(pl.program_id(1) == 0)
    def _():
        st_ref[...] = jnp.zeros_like(st_ref)

    z = z_ref[...]
    q = z[:, 0:256]
    k = z[:, 256:512]
    v = z[:, 512:768]
    g = z[:, 768:1024]
    cos = cos_ref[...]
    sin = sin_ref[...]

    def rot(u):
        u1 = u[:, :128]
        u2 = u[:, 128:]
        return jnp.concatenate([u1 * cos - u2 * sin, u2 * cos + u1 * sin], axis=1)

    qr = rot(q) * (HEAD_DIM ** -0.5)
    kr = rot(k)
    lane = lax.broadcasted_iota(jnp.int32, (L, 256), 1)
    head_qk = (lane & 127) >> 5
    head_v = lane >> 6
    qb = qr.astype(BF16)
    kb = kr.astype(BF16)
    vb = v.astype(BF16)

    att = _dot_nt(_stack_heads(qb, head_qk), kb) * indec_ref[...]
    o_st = _dot(att.astype(BF16), vb)
    o = jnp.zeros((L, 256), F32)
    for h in range(MIX_HEADS):
        o = o + jnp.where(head_v == h, o_st[h * L:(h + 1) * L], 0.0)
    state = st_ref[...]
    o = o + _dot(qb, state.astype(BF16)) * qdec_ref[...]
    st_ref[...] = state * cdec_ref[...] + _dot_tn((kr * kdec_ref[...]).astype(BF16), vb) * bd_ref[...]

    o = _head_norm(o, segm_ref[...], RET_NORM_EPS) * ng_ref[...]
    o_ref[...] = g * jax.nn.sigmoid(g) * o


def retention(z, tabs, ng, layer, B, S):
    L = RET_CHUNK
    nc = S // L
    cos, sin, indec, qdec, kdec, cdec, bd, segm = tabs
    const2 = lambda b, c: (0, 0)
    return pl.pallas_call(
        _retention_kernel,
        grid=(B, nc),
        in_specs=[
            pl.BlockSpec((L, 1024), lambda b, c: (b * nc + c, OFF_RET // 1024)),
            pl.BlockSpec((L, 128), lambda b, c: (c, 0)),
            pl.BlockSpec((L, 128), lambda b, c: (c, 0)),
            pl.BlockSpec((MIX_HEADS * L, L), const2),
            pl.BlockSpec((L, 256), const2),
            pl.BlockSpec((L, 256), const2),
            pl.BlockSpec((1, 256), const2),
            pl.BlockSpec((256, 256), const2),
            pl.BlockSpec((256, 256), const2),
            pl.BlockSpec((None, 1, 256), lambda b, c: (layer, 0, 0)),
        ],
        out_specs=pl.BlockSpec((L, 256), lambda b, c: (b * nc + c, 0)),
        out_shape=jax.ShapeDtypeStruct((B * S, 256), F32),
        scratch_shapes=[pltpu.VMEM((256, 256), F32)],
        compiler_params=_cparams(("parallel", "arbitrary")),
        name="retention",
    )(z, cos, sin, indec, qdec, kdec, cdec, bd, segm, ng)


def _rwkv_kernel(z_ref, mu_ref, w2_ref, a2_ref, g2_ref, vec_ref, tri_ref, bdl_ref, wide_ref,
                 segs_ref, segm_ref, o_ref, st_ref, prev_ref):
    C = RWKV_CHUNK

    @pl.when(pl.program_id(1) == 0)
    def _():
        st_ref[...] = jnp.zeros_like(st_ref)
        prev_ref[...] = jnp.zeros_like(prev_ref)

    z = z_ref[...]
    rows = lax.broadcasted_iota(jnp.int32, (C, 1024), 0)
    zs = jnp.where(rows == 0, prev_ref[0:1, :], pltpu.roll(z, 1, axis=0))
    prev_ref[...] = jnp.broadcast_to(z[C - 1:C, :], prev_ref.shape)
    zf = z + (zs - z) * mu_ref[...]
    r = zf[:, 0:256]
    k = zf[:, 256:512]
    v = zf[:, 512:768]
    lora = zf[:, 768:896]

    vec = vec_ref[...]
    w0, a0, k_k, k_a, r_k, ln_g, ln_b = (vec[i:i + 1] for i in range(7))

    def lora_dot(x, w_ref):
        hi, lo = _split2(x)
        return _dot(hi, w_ref[0]) + _dot(lo, w_ref[0]) + _dot(hi, w_ref[1])

    wpre = w0 + lora_dot(jnp.tanh(lora), w2_ref)
    y = -wpre
    softplus = jnp.maximum(y, 0.0) + jnp.log(1.0 + jnp.exp(-jnp.abs(y)))
    w_log = -softplus - 0.5
    ld = -jnp.exp(w_log)
    a = jax.nn.sigmoid(a0 + lora_dot(lora, a2_ref))
    gate = lora_dot(jax.nn.sigmoid(lora), g2_ref)
    kk = k * k_k
    kk = kk / jnp.maximum(jnp.sqrt(_dot_x2(kk * kk, segs_ref[...])), 1e-12)
    k2 = k * (1.0 + (a - 1.0) * k_a)

    tri = tri_ref[...]
    l1 = ld.astype(BF16)
    r1 = ld - l1.astype(F32)
    l2 = r1.astype(BF16)
    l3 = (r1 - l2.astype(F32)).astype(BF16)
    cs = _dot(tri, l1) + _dot(tri, l2) + _dot(tri, l3)
    cs_end = cs[C - 1:C, :]
    e_neg = jnp.exp(-cs)
    e_end = jnp.exp(cs_end - cs)
    kka = kk * a
    a_t = -kk * jnp.exp(cs - ld)
    r_t = r * jnp.exp(cs)
    b_t = kka * e_neg
    k_t = k2 * e_neg
    b_g = kka * e_end
    k_g = k2 * e_end

    lane = lax.broadcasted_iota(jnp.int32, (C, 256), 1)
    head = lane >> 6
    stack = lambda x: _stack_heads(x.astype(BF16), head)
    a_st, b_st, k_st, v_st = stack(a_t), stack(b_t), stack(k_t), stack(v)

    n_pow = _dot_nt(a_st, b_st) * bdl_ref[...]
    ri = lax.broadcasted_iota(jnp.int32, (4 * C, 4 * C), 0)
    ci = lax.broadcasted_iota(jnp.int32, (4 * C, 4 * C), 1)
    t_inv = jnp.where(ri == ci, 1.0, 0.0) + n_pow
    sq = 2
    while sq < C:
        nb = n_pow.astype(BF16)
        n_pow = _dot(nb, nb)
        t_inv = t_inv + _dot(t_inv.astype(BF16), n_pow.astype(BF16))
        sq *= 2

    ar = jnp.concatenate([a_t, r_t], axis=0).astype(BF16)
    bk = jnp.concatenate([b_st, k_st], axis=0)
    wide = _dot_nt(ar, bk)
    strict = wide_ref[0]
    incl = wide_ref[1]
    a_ak = (wide[0:C, 4 * C:] * strict).astype(BF16)
    a_rb = (wide[C:, 0:4 * C] * incl).astype(BF16)
    a_rk = (wide[C:, 4 * C:] * incl).astype(BF16)

    state = st_ref[...]
    sb = state.astype(BF16)
    x = _dot_nt(a_t.astype(BF16), sb) + _dot(a_ak, v_st)
    u_st = _dot(t_inv.astype(BF16), stack(x)).astype(BF16)
    yv = _dot_nt(r_t.astype(BF16), sb) + _dot(a_rb, u_st) + _dot(a_rk, v_st)
    uv = jnp.concatenate([u_st, v_st], axis=0)
    bkg = jnp.concatenate([stack(b_g), stack(k_g)], axis=0)
    st_ref[...] = state * jnp.exp(cs_end) + _dot_tn(uv, bkg)

    yn = _head_norm(yv, segm_ref[...], RWKV_GN_EPS) * ln_g + ln_b
    yn = yn + _dot_x2(r * k2 * r_k, segs_ref[...]) * v
    o_ref[...] = yn * gate


def rwkv7(z, mu, w2, a2, g2, vec, consts, layer, B, S):
    C = RWKV_CHUNK
    nc = S // C
    tri, bdl, wide, segs, segm = consts
    const2 = lambda b, c: (0, 0)
    lw = lambda b, c: (layer, 0, 0, 0)
    return pl.pallas_call(
        _rwkv_kernel,
        grid=(B, nc),
        in_specs=[
            pl.BlockSpec((C, 1024), lambda b, c: (b * nc + c, OFF_RWKV // 1024)),
            pl.BlockSpec((None, 1, 1024), lambda b, c: (layer, 0, 0)),
            pl.BlockSpec((None, 2, 128, 256), lw),
            pl.BlockSpec((None, 2, 128, 256), lw),
            pl.BlockSpec((None, 2, 128, 256), lw),
            pl.BlockSpec((None, 8, 256), lambda b, c: (layer, 0, 0)),
            pl.BlockSpec((C, C), const2),
            pl.BlockSpec((4 * C, 4 * C), const2),
            pl.BlockSpec((2, C, 4 * C), lambda b, c: (0, 0, 0)),
            pl.BlockSpec((256, 256), const2),
            pl.BlockSpec((256, 256), const2),
        ],
        out_specs=pl.BlockSpec((C, 256), lambda b, c: (b * nc + c, 0)),
        out_shape=jax.ShapeDtypeStruct((B * S, 256), F32),
        scratch_shapes=[pltpu.VMEM((256, 256), F32), pltpu.VMEM((8, 1024), F32)],
        compiler_params=_cparams(("parallel", "arbitrary")),
        name="rwkv7",
    )(z, mu, w2, a2, g2, vec, tri, bdl, wide, segs, segm)


def _merge_kernel(x_ref, zc_ref, zg_ref, on_ref, or_ref, ow_ref, cw_ref, wb_ref, wo_ref, g_ref,
                  o_ref, carry_ref):
    tm = x_ref.shape[0]

    @pl.when(pl.program_id(1) == 0)
    def _():
        carry_ref[...] = jnp.zeros_like(carry_ref)

    zc = zc_ref[...]
    b_g = zc[:, 0:256]
    u = zc[:, 256:512] * zc[:, 512:768]
    prev = carry_ref[...]
    rows = lax.broadcasted_iota(jnp.int32, (tm, 256), 0)
    u1 = jnp.where(rows == 0, prev[7:8], pltpu.roll(u, 1, axis=0))
    u2 = jnp.where(rows == 0, prev[6:7], jnp.where(rows == 1, prev[7:8], pltpu.roll(u, 2, axis=0)))
    carry_ref[...] = u[tm - 8:tm]
    cw = cw_ref[...]
    o_conv = b_g * (cw[0:1] * u2 + cw[1:2] * u1 + cw[2:3] * u)

    branches = (on_ref[...], or_ref[...], ow_ref[...], o_conv)
    merged = jnp.zeros((tm, D_MODEL), F32)
    for m in range(N_BRANCHES):
        gm = jax.nn.sigmoid(zg_ref[:, m * D_MODEL:(m + 1) * D_MODEL])
        merged = merged + gm * _dot(branches[m].astype(BF16), wb_ref[m])
    y = _dot(merged.astype(BF16), wo_ref[...])
    o_ref[...] = x_ref[...] + _rms(y, g_ref[...])


def merge_mix(x, z, o_nsa, o_ret, o_rwkv, conv_w, w_branch, w_out, g_post, layer, B, S, tm):
    nt = S // tm
    rowmap = lambda b, i: (b * nt + i, 0)
    return pl.pallas_call(
        _merge_kernel,
        grid=(B, nt),
        in_specs=[
            pl.BlockSpec((tm, D_MODEL), rowmap),
            pl.BlockSpec((tm, 768), lambda b, i: (b * nt + i, OFF_CONV // 768)),
            pl.BlockSpec((tm, 4096), lambda b, i: (b * nt + i, 0)),
            pl.BlockSpec((tm, 256), rowmap),
            pl.BlockSpec((tm, 256), rowmap),
            pl.BlockSpec((tm, 256), rowmap),
            pl.BlockSpec((None, 8, 256), lambda b, i: (layer, 0, 0)),
            pl.BlockSpec((None, N_BRANCHES, MIX_WIDTH, D_MODEL), lambda b, i: (layer, 0, 0, 0)),
            pl.BlockSpec((None, D_MODEL, D_MODEL), lambda b, i: (layer, 0, 0)),
            pl.BlockSpec((None, 1, D_MODEL), lambda b, i: (layer, 0, 0)),
        ],
        out_specs=pl.BlockSpec((tm, D_MODEL), rowmap),
        out_shape=jax.ShapeDtypeStruct((B * S, D_MODEL), F32),
        scratch_shapes=[pltpu.VMEM((8, 256), F32)],
        compiler_params=_cparams(("parallel", "arbitrary")),
        name="merge_mix",
    )(x, z, z, o_nsa, o_ret, o_rwkv, conv_w, w_branch, w_out, g_post)


def _xattn_kernel(x_ref, kv_ref, gpre_ref, wq_ref, wo_ref, gpost_ref, o_ref):
    x = x_ref[...]
    h = _rms(x, gpre_ref[...]).astype(BF16)
    q = (_dot(h, wq_ref[...]) * (XA_HEAD_DIM ** -0.5)).astype(BF16)
    outs = []
    for hd in range(XA_HEADS):
        cs = slice(hd * XA_HEAD_DIM, (hd + 1) * XA_HEAD_DIM)
        kh = kv_ref[:, hd * XA_HEAD_DIM:(hd + 1) * XA_HEAD_DIM]
        vh = kv_ref[:, D_MODEL + hd * XA_HEAD_DIM:D_MODEL + (hd + 1) * XA_HEAD_DIM]
        s = _dot_nt(q[:, cs], kh)
        e = jnp.exp(s - _rowmax(s))
        p = e / _rowsum(e)
        outs.append(_dot(p.astype(BF16), vh).astype(BF16))
    o = jnp.concatenate(outs, axis=1)
    y = _dot(o, wo_ref[...])
    o_ref[...] = x + _rms(y, gpost_ref[...])


def cross_attention(x, kv, g_pre, wq, wo, g_post, layer, B, S, M, tm):
    nt = S // tm
    rowmap = lambda b, i: (b * nt + i, 0)
    lmap = lambda b, i: (layer, 0, 0)
    return pl.pallas_call(
        _xattn_kernel,
        grid=(B, nt),
        in_specs=[
            pl.BlockSpec((tm, D_MODEL), rowmap),
            pl.BlockSpec((M, 2 * D_MODEL), lambda b, i: (b, 0)),
            pl.BlockSpec((None, 1, D_MODEL), lmap),
            pl.BlockSpec((None, D_MODEL, D_MODEL), lmap),
            pl.BlockSpec((None, D_MODEL, D_MODEL), lmap),
            pl.BlockSpec((None, 1, D_MODEL), lmap),
        ],
        out_specs=pl.BlockSpec((tm, D_MODEL), rowmap),
        out_shape=jax.ShapeDtypeStruct((B * S, D_MODEL), F32),
        compiler_params=_cparams(("parallel", "arbitrary")),
        name="cross_attention",
    )(x, kv, g_pre, wq, wo, g_post)


def _mlp_kernel(x_ref, gpre_ref, w1_ref, w2_ref, gpost_ref, o_ref):
    x = x_ref[...]
    h = _rms(x, gpre_ref[...]).astype(BF16)
    tf = 1024
    acc = jnp.zeros(x.shape, F32)
    for f in range(D_FF // tf):
        a = jnp.maximum(_dot(h, w1_ref[:, f * tf:(f + 1) * tf]), 0.0)
        acc = acc + _dot((a * a).astype(BF16), w2_ref[f * tf:(f + 1) * tf, :])
    o_ref[...] = x + _rms(acc, gpost_ref[...])


def mlp(x, g_pre, w1, w2, g_post, layer, tm):
    M = x.shape[0]
    lmap = lambda i: (layer, 0, 0)
    return pl.pallas_call(
        _mlp_kernel,
        grid=(M // tm,),
        in_specs=[
            pl.BlockSpec((tm, D_MODEL), lambda i: (i, 0)),
            pl.BlockSpec((None, 1, D_MODEL), lmap),
            pl.BlockSpec((None, D_MODEL, D_FF), lmap, pipeline_mode=pl.Buffered(1)),
            pl.BlockSpec((None, D_FF, D_MODEL), lmap, pipeline_mode=pl.Buffered(1)),
            pl.BlockSpec((None, 1, D_MODEL), lmap),
        ],
        out_specs=pl.BlockSpec((tm, D_MODEL), lambda i: (i, 0)),
        out_shape=jax.ShapeDtypeStruct((M, D_MODEL), F32),
        compiler_params=_cparams(("parallel",)),
        name="mlp",
    )(x, g_pre, w1, w2, g_post)


def _t5_bucket(dist):
    n = jnp.maximum(dist, 0)
    max_exact = N_BUCKETS // 2
    nf = jnp.maximum(n, 1).astype(F32)
    large = max_exact + (jnp.log(nf / max_exact) / math.log(MAX_DISTANCE / max_exact)
                         * (N_BUCKETS - max_exact)).astype(jnp.int32)
    large = jnp.minimum(large, N_BUCKETS - 1)
    return jnp.where(n < max_exact, n, large)


def _nsa_tables(rel_bias, S):
    n_qb = S // Q_BLOCK
    n_cmp = S // NSA_CMP_STRIDE
    n_blk = S // NSA_SLC_LEN
    bias_f = rel_bias.astype(F32)
    i = jnp.arange(Q_BLOCK)
    per_qb = Q_BLOCK // NSA_CMP_STRIDE
    g = jnp.arange(n_cmp + per_qb * (n_qb - 1)) - per_qb * (n_qb - 1)
    d_c = i[None, :] - (g[:, None] * NSA_CMP_STRIDE + NSA_CMP_LEN - 1)
    bias_c = bias_f[_t5_bucket(d_c)].transpose(0, 2, 1).reshape(g.shape[0], MIX_HEADS * Q_BLOCK)
    d0 = i[:, None] - i[None, :]
    tab = jnp.stack([bias_f[_t5_bucket(d0)], bias_f[_t5_bucket(d0 + Q_BLOCK)],
                     jnp.broadcast_to(bias_f[N_BUCKETS - 1], (Q_BLOCK, Q_BLOCK, MIX_HEADS))])
    tab = tab.transpose(0, 2, 3, 1).reshape(3, Q_BLOCK, MIX_HEADS * Q_BLOCK)
    cs = np.arange(n_cmp)[None, :] * NSA_CMP_STRIDE
    ss = np.arange(n_blk)[:, None] * NSA_SLC_LEN
    ovt = np.clip(np.minimum(cs + NSA_CMP_LEN, ss + NSA_SLC_LEN) - np.maximum(cs, ss), 0, None)
    ovt = ovt.astype(np.float32) / NSA_CMP_LEN
    ovt[:, (S - NSA_CMP_LEN) // NSA_CMP_STRIDE + 1:] = 0.0
    ext = ((np.arange(S)[:, None] // NSA_SLC_LEN) == np.arange(n_blk)[None, :]).astype(np.float32)
    return bias_c, tab, jnp.asarray(ovt, BF16), jnp.asarray(ext, BF16)


def _ret_tables(S):
    L = RET_CHUNK
    H = MIX_HEADS
    half = HEAD_DIM // 2
    pos = jnp.arange(S)
    inv_freq = ROPE_BASE ** (-jnp.arange(half, dtype=F32) / half)
    ang = pos.astype(F32)[:, None] * inv_freq[None, :]
    cos = jnp.tile(jnp.cos(ang), (1, H))
    sin = jnp.tile(jnp.sin(ang), (1, H))
    lg = jnp.log(1.0 - 2.0 ** (-5.0 - jnp.arange(H, dtype=F32)))
    n = jnp.arange(L, dtype=F32)
    diff = n[:, None] - n[None, :]
    inner = jnp.where(diff >= 0, jnp.exp(jnp.maximum(diff, 0.0)[None] * lg[:, None, None]), 0.0)
    indec = inner.reshape(H * L, L)
    q_decay = jnp.exp((n + 1.0)[None, :] * lg[:, None])
    k_decay = jnp.exp((L - 1.0 - n)[None, :] * lg[:, None])
    chunk_decay = jnp.exp(L * lg)
    lane = np.arange(256)
    head_v = lane // HEAD_DIM
    head_qk = (lane % 128) // half
    qdec = q_decay.T[:, head_v]
    kdec = k_decay.T[:, head_qk]
    cdec = chunk_decay[head_v][None, :]
    bd = jnp.asarray((head_qk[:, None] == head_v[None, :]).astype(np.float32))
    segm = jnp.asarray((head_v[:, None] == head_v[None, :]).astype(np.float32) / HEAD_DIM, BF16)
    return cos, sin, indec, qdec, kdec, cdec, bd, segm


def _rwkv_consts():
    C = RWKV_CHUNK
    t = np.arange(C)
    tri = (t[:, None] >= t[None, :]).astype(np.float32)
    r = np.arange(4 * C)
    bdl = ((r[:, None] // C == r[None, :] // C) & (r[:, None] % C > r[None, :] % C)).astype(np.float32)
    strict = (t[:, None] > (r[None, :] % C)).astype(np.float32)
    incl = (t[:, None] >= (r[None, :] % C)).astype(np.float32)
    lane = np.arange(256) // HEAD_DIM
    seg = (lane[:, None] == lane[None, :]).astype(np.float32)
    return (jnp.asarray(tri, BF16), jnp.asarray(bdl), jnp.asarray(np.stack([strict, incl])),
            jnp.asarray(seg, BF16), jnp.asarray(seg / HEAD_DIM, BF16))


def _hi_lo(w):
    hi = w.astype(BF16)
    lo = (w - hi.astype(F32)).astype(BF16)
    return jnp.stack([hi, lo], axis=1)


def _pad_rows(w, top, total):
    return jnp.pad(w, ((0, 0), (top, total - top - w.shape[1]), (0, 0)))


def _layout_w_in(w_in):
    L = w_in.shape[0]
    o = 0
    nsa_q = w_in[:, :, o:o + 256]; o += 256
    nsa_kv = w_in[:, :, o:o + 384]; o += 384
    nsa_g = w_in[:, :, o:o + 12]; o += 12
    ret = w_in[:, :, o:o + 1024]; o += 1024
    rwkv = w_in[:, :, o:o + RWKV_COLS]; o += RWKV_COLS
    conv = w_in[:, :, o:o + 768]; o += 768
    gate = w_in[:, :, o:o + 4096]

    def rot_perm(w):
        return w.reshape(L, D_MODEL, MIX_HEADS, 2, HEAD_DIM // 2).transpose(0, 1, 3, 2, 4).reshape(L, D_MODEL, 256)

    ret = jnp.concatenate([rot_perm(ret[:, :, 0:256]), rot_perm(ret[:, :, 256:512]), ret[:, :, 512:]], axis=2)
    zeros = lambda n: jnp.zeros((L, D_MODEL, n), w_in.dtype)
    out = jnp.concatenate([gate, ret, rwkv, zeros(1024 - RWKV_COLS), conv, nsa_q, nsa_kv, nsa_g,
                           zeros(128 - 12)], axis=2)
    assert out.shape[2] == Z_COLS
    return out.astype(BF16)


def _layout_cmp(cmp_w, cmp_pe):
    L = cmp_w.shape[0]
    wk = cmp_w[:, 0]
    wv = cmp_w[:, 1]
    zero = jnp.zeros_like(wk)
    blk = jnp.concatenate([jnp.concatenate([wk, zero], axis=3), jnp.concatenate([zero, wv], axis=3)], axis=2)
    pe2 = jnp.concatenate([cmp_pe, cmp_pe], axis=2)[:, :, None, :]
    s = NSA_CMP_STRIDE
    return blk[:, :s].astype(BF16), blk[:, s:].astype(BF16), pe2[:, :s], pe2[:, s:]


def kernel(x, mem, ln_mix_pre, w_in, nsa_cmp_w, nsa_cmp_pe, rel_bias, ret_norm_g, rwkv_mu, rwkv_w0, rwkv_w2, rwkv_a0, rwkv_a2, rwkv_g2, rwkv_k_k, rwkv_k_a, rwkv_r_k, rwkv_ln_g, rwkv_ln_b, conv_w, w_branch, w_mix_out, ln_mix_post, ln_xa_pre, ln_mem, xa_wq, xa_wkv, xa_wo, ln_xa_post, ln_mlp_pre, mlp_w1, mlp_w2, ln_mlp_post):
    B, S, D = x.shape
    M = mem.shape[1]
    depth = w_in.shape[0]
    row = lambda g: g[:, None, :]

    w_in_b = _layout_w_in(w_in)
    cmp_lo, cmp_hi, pe_lo, pe_hi = _layout_cmp(nsa_cmp_w, nsa_cmp_pe)
    nsa_tabs = _nsa_tables(rel_bias, S)
    ret_tabs = _ret_tables(S)
    rwkv_consts = _rwkv_consts()
    mu = jnp.pad(rwkv_mu, ((0, 0), (0, 1024 - RWKV_COLS)))[:, None, :]
    w2 = _hi_lo(_pad_rows(rwkv_w2, 0, 128))
    a2 = _hi_lo(_pad_rows(rwkv_a2, RWKV_DECAY_LORA, 128))
    g2 = _hi_lo(_pad_rows(rwkv_g2, RWKV_DECAY_LORA + RWKV_AAA_LORA, 128))
    vec = jnp.stack([rwkv_w0, rwkv_a0, rwkv_k_k, rwkv_k_a, rwkv_r_k, rwkv_ln_g, rwkv_ln_b,
                     jnp.zeros_like(rwkv_w0)], axis=1)
    conv_p = jnp.pad(conv_w, ((0, 0), (0, 8 - CONV_WIDTH), (0, 0)))
    w_branch_b = w_branch.astype(BF16)
    w_mix_out_b = w_mix_out.astype(BF16)
    xa_wq_b = xa_wq.astype(BF16)
    xa_wkv_b = xa_wkv.astype(BF16)
    xa_wo_b = xa_wo.astype(BF16)
    mlp_w1_b = mlp_w1.astype(BF16)
    mlp_w2_b = mlp_w2.astype(BF16)

    xf = x.reshape(B * S, D)
    memf = mem.reshape(B * M, D)
    for l in range(depth):
        z = norm_matmul(xf, row(ln_mix_pre), w_in_b, l, tm=1024, tn=1280, out_dtype=F32)
        prep = nsa_compress(z, cmp_lo, cmp_hi, pe_lo, pe_hi, l, B, S)
        o_nsa = nsa_attention(z, prep, *nsa_tabs, B, S)
        o_ret = retention(z, ret_tabs, row(ret_norm_g), l, B, S)
        o_rwkv = rwkv7(z, mu, w2, a2, g2, vec, rwkv_consts, l, B, S)
        xf = merge_mix(xf, z, o_nsa, o_ret, o_rwkv, conv_p, w_branch_b, w_mix_out_b,
                       row(ln_mix_post), l, B, S, tm=256)
        kvm = norm_matmul(memf, row(ln_mem), xa_wkv_b, l, tm=M, tn=1024, out_dtype=BF16)
        xf = cross_attention(xf, kvm, row(ln_xa_pre), xa_wq_b, xa_wo_b, row(ln_xa_post), l, B, S, M, tm=512)
        xf = mlp(xf, row(ln_mlp_pre), mlp_w1_b, mlp_w2_b, row(ln_mlp_post), l, tm=512)
    return xf.reshape(B, S, D)
```

```python
import functools
import math

import numpy as np
import jax
import jax.numpy as jnp
from jax import lax
from jax.experimental import pallas as pl
from jax.experimental.pallas import tpu as pltpu

F32 = jnp.float32
BF16 = jnp.bfloat16

D_MODEL = 1024
N_BRANCHES = 4
MIX_WIDTH = 256
HEAD_DIM = 64
MIX_HEADS = 4

NSA_CMP_LEN = 32
NSA_CMP_STRIDE = 16
NSA_SLC_LEN = 64
NSA_TOP_N = 8
NSA_WINDOW = 512
Q_BLOCK = 128
FORCE_BONUS = 1e4
N_BUCKETS = 32
MAX_DISTANCE = 128

RET_CHUNK = 128
ROPE_BASE = 10000.0
RET_NORM_EPS = 1e-5

RWKV_DECAY_LORA = 32
RWKV_AAA_LORA = 32
RWKV_GATE_LORA = 64
RWKV_GN_EPS = 64e-5
RWKV_COLS = 3 * MIX_WIDTH + RWKV_DECAY_LORA + RWKV_AAA_LORA + RWKV_GATE_LORA
RWKV_CHUNK = 64

CONV_WIDTH = 3
XA_HEADS = 4
XA_HEAD_DIM = D_MODEL // XA_HEADS
D_FF = 4 * D_MODEL

RMS_EPS = 1e-6
NEG_INF = -1e30
NEG_BIG = -3e38

OFF_GATE = 0
OFF_RET = 4096
OFF_RWKV = 5120
OFF_CONV = 6144
OFF_NSAQ = 6912
OFF_NSAKV = 7168
OFF_NSAG = 7552
Z_COLS = 7680

VMEM_LIMIT = 56 * 1024 * 1024


def _cparams(sem):
    return pltpu.CompilerParams(dimension_semantics=sem, vmem_limit_bytes=VMEM_LIMIT)


def _dot(a, b):
    return jnp.dot(a, b, preferred_element_type=F32)


def _dot_nt(a, b):
    return lax.dot_general(a, b, (((1,), (1,)), ((), ())), preferred_element_type=F32)


def _dot_tn(a, b):
    return lax.dot_general(a, b, (((0,), (0,)), ((), ())), preferred_element_type=F32)


def _split2(x):
    hi = x.astype(BF16)
    lo = (x - hi.astype(F32)).astype(BF16)
    return hi, lo


def _dot_x2(x, w_bf16):
    hi, lo = _split2(x)
    return _dot(hi, w_bf16) + _dot(lo, w_bf16)


def _rms(x, g):
    ms = jnp.mean(x * x, axis=-1, keepdims=True)
    return x * lax.rsqrt(ms + RMS_EPS) * g


def _norm_matmul_kernel(x_ref, g_ref, w_ref, o_ref, h_ref):
    @pl.when(pl.program_id(1) == 0)
    def _():
        h_ref[...] = _rms(x_ref[...], g_ref[...]).astype(BF16)

    o_ref[...] = _dot(h_ref[...], w_ref[...]).astype(o_ref.dtype)


def norm_matmul(x, g, w, layer, tm, tn, out_dtype):
    M, D = x.shape
    N = w.shape[2]
    return pl.pallas_call(
        _norm_matmul_kernel,
        grid=(M // tm, N // tn),
        in_specs=[
            pl.BlockSpec((tm, D), lambda i, j: (i, 0)),
            pl.BlockSpec((None, 1, D), lambda i, j: (layer, 0, 0)),
            pl.BlockSpec((None, D, tn), lambda i, j: (layer, 0, j)),
        ],
        out_specs=pl.BlockSpec((tm, tn), lambda i, j: (i, j)),
        out_shape=jax.ShapeDtypeStruct((M, N), out_dtype),
        scratch_shapes=[pltpu.VMEM((tm, D), BF16)],
        compiler_params=_cparams(("parallel", "arbitrary")),
        name="norm_matmul",
    )(x, g, w)


def _nsa_compress_kernel(kvc_ref, kvs_ref, kvw_ref, wlo_ref, whi_ref, pelo_ref, pehi_ref,
                         kc_ref, vct_ref, ks_ref, vst_ref, kw_ref, vwt_ref):
    nblk = kc_ref.shape[0]
    n_kt = vst_ref.shape[0]
    ylo = jnp.zeros((nblk, 128), F32)
    yhi = jnp.zeros((nblk, 128), F32)
    for r in range(NSA_CMP_STRIDE):
        xr = kvc_ref[pl.ds(r, nblk, stride=NSA_CMP_STRIDE), :]
        ylo = ylo + _dot((xr + pelo_ref[r]).astype(BF16), wlo_ref[r])
        yhi = yhi + _dot((xr + pehi_ref[r]).astype(BF16), whi_ref[r])
    y = ylo + pltpu.roll(yhi, nblk - 1, axis=0)
    kc_ref[...] = y.astype(BF16)
    vct_ref[...] = y.T[HEAD_DIM:, :].astype(BF16)
    ks_ref[...] = kvs_ref[...].astype(BF16)
    kw_ref[...] = kvw_ref[...].astype(BF16)
    for kt in range(n_kt):
        rs = slice(kt * 128, (kt + 1) * 128)
        vst_ref[kt] = kvs_ref[rs, :].T[HEAD_DIM:, :].astype(BF16)
        vwt_ref[kt] = kvw_ref[rs, :].T[HEAD_DIM:, :].astype(BF16)


def nsa_compress(z, wlo, whi, pelo, pehi, layer, B, S):
    nblk = S // NSA_CMP_STRIDE
    n_kt = S // 128
    c0 = OFF_NSAKV // 128
    wmap = lambda b: (layer, 0, 0, 0)
    b3 = lambda b: (b, 0, 0)
    b4 = lambda b: (b, 0, 0, 0)
    return pl.pallas_call(
        _nsa_compress_kernel,
        grid=(B,),
        in_specs=[
            pl.BlockSpec((S, 128), lambda b: (b, c0)),
            pl.BlockSpec((S, 128), lambda b: (b, c0 + 1)),
            pl.BlockSpec((S, 128), lambda b: (b, c0 + 2)),
            pl.BlockSpec((None, NSA_CMP_STRIDE, 128, 128), wmap),
            pl.BlockSpec((None, NSA_CMP_STRIDE, 128, 128), wmap),
            pl.BlockSpec((None, NSA_CMP_STRIDE, 1, 128), wmap),
            pl.BlockSpec((None, NSA_CMP_STRIDE, 1, 128), wmap),
        ],
        out_specs=[
            pl.BlockSpec((None, nblk, 128), b3),
            pl.BlockSpec((None, HEAD_DIM, nblk), b3),
            pl.BlockSpec((None, S, 128), b3),
            pl.BlockSpec((None, n_kt, HEAD_DIM, 128), b4),
            pl.BlockSpec((None, S, 128), b3),
            pl.BlockSpec((None, n_kt, HEAD_DIM, 128), b4),
        ],
        out_shape=[
            jax.ShapeDtypeStruct((B, nblk, 128), BF16),
            jax.ShapeDtypeStruct((B, HEAD_DIM, nblk), BF16),
            jax.ShapeDtypeStruct((B, S, 128), BF16),
            jax.ShapeDtypeStruct((B, n_kt, HEAD_DIM, 128), BF16),
            jax.ShapeDtypeStruct((B, S, 128), BF16),
            jax.ShapeDtypeStruct((B, n_kt, HEAD_DIM, 128), BF16),
        ],
        compiler_params=_cparams(("parallel",)),
        name="nsa_compress",
    )(z, z, z, wlo, whi, pelo, pehi)


def _rowmax(s):
    return jnp.max(s, axis=-1, keepdims=True)


def _rowsum(s):
    return jnp.sum(s, axis=-1, keepdims=True)


def _colmax(s):
    return jnp.max(s, axis=0, keepdims=True)


def _colsum(s):
    return jnp.sum(s, axis=0, keepdims=True)


def _nsa_kernel(q_ref, gl_ref, kc_ref, vct_ref, ks_ref, vst_ref, kw_ref, vwt_ref, bc_ref, tab_ref,
                ovt_ref, ext_ref, o_ref, selx_ref):
    bi = pl.program_id(1)
    G = q_ref.shape[0]
    n_kt = selx_ref.shape[1]
    n_blk = ovt_ref.shape[0]
    n_cmp = kc_ref.shape[1]
    QB = Q_BLOCK
    R = MIX_HEADS * QB
    n_off = NSA_WINDOW // 128

    key = lax.broadcasted_iota(jnp.int32, (128, R), 0)
    i4 = lax.broadcasted_iota(jnp.int32, (128, R), 1) & (QB - 1)
    causal = key <= i4
    lo = jnp.maximum(bi - n_off, 0)
    q4 = [None] * G
    o_c = [None] * G
    carry_s = [None] * G
    carry_w = [None] * G

    def tile_qk(g, kref, kt):
        start = pl.multiple_of(kt * 128, 128)
        return _dot(kref[g, pl.ds(start, 128), :HEAD_DIM], q4[g])

    def tile_bias(kt):
        return tab_ref[jnp.where(kt == bi - 1, 1, 2)]

    def sel_tile(g, kt, thr):
        m1 = selx_ref[g, kt]
        return jnp.concatenate([m1, m1, m1, m1], axis=1) > thr

    def win_mask(kt, valid):
        shift = jnp.where(valid, jnp.where(bi - kt < n_off, -4096, 0), 4096)
        return key > i4 + shift

    def first(s, vt, out, g):
        m = _colmax(s)
        p = jnp.exp(s - m)
        yield
        out[g] = (m, _colsum(p), _dot(vt, p.astype(BF16)))

    def head_part(g):
        qt = (q_ref[g] * (HEAD_DIM ** -0.5)).T
        q4[g] = jnp.concatenate([qt[h * HEAD_DIM:(h + 1) * HEAD_DIM] for h in range(MIX_HEADS)],
                                axis=1).astype(BF16)
        yield
        s = jnp.where(causal, tile_qk(g, kw_ref, bi) + tab_ref[0], NEG_INF)
        yield
        yield from first(s, vwt_ref[g, bi], carry_w, g)
        yield
        g0 = pl.multiple_of((n_kt - 1 - bi) * 8, 8)
        s = _dot(kc_ref[g, :, :HEAD_DIM], q4[g]) + bc_ref[pl.ds(g0, n_cmp), :]
        nrow = lax.broadcasted_iota(jnp.int32, (n_cmp, R), 0)
        tq = bi * QB + (lax.broadcasted_iota(jnp.int32, (n_cmp, R), 1) & (QB - 1))
        s = jnp.where(tq - (nrow * NSA_CMP_STRIDE + (NSA_CMP_LEN - 1)) >= 0, s, NEG_INF)
        yield
        e = jnp.exp(s - _colmax(s))
        p_c = e / _colsum(e)
        p_c = p_c * (tq >= NSA_CMP_LEN - 1).astype(F32)
        yield
        o_c[g] = _dot(vct_ref[g], p_c.astype(BF16))
        p_sum = p_c[:, 0:QB] + p_c[:, QB:2 * QB] + p_c[:, 2 * QB:3 * QB] + p_c[:, 3 * QB:4 * QB]
        p_hi, p_lo = _split2(p_sum)
        imp = _dot(ovt_ref[...], p_hi) + _dot(ovt_ref[...], p_lo)
        yield
        blk = lax.broadcasted_iota(jnp.int32, (n_blk, QB), 0)
        cur = (bi * QB + lax.broadcasted_iota(jnp.int32, (n_blk, QB), 1)) >> 6
        forced = (blk == 0) | (blk == cur) | (blk == cur - 1)
        imp = jnp.where(forced, imp + FORCE_BONUS, imp)
        imp = jnp.where(blk <= cur, imp, NEG_INF)
        blk_f = blk.astype(F32)
        sel = jnp.zeros((n_blk, QB), F32)
        for _ in range(NSA_TOP_N):
            mx = _colmax(imp)
            idx = jnp.min(jnp.where(imp == mx, blk_f, 4096.0), axis=0, keepdims=True)
            pick = blk_f == idx
            sel = jnp.where(pick, 1.0, sel)
            imp = jnp.where(pick, NEG_BIG, imp)
            yield
        selx = _dot(ext_ref[...], sel.astype(BF16))
        for kt in range(n_kt):
            selx_ref[g, kt] = selx[kt * 128:(kt + 1) * 128, :]
        yield
        s = jnp.where(sel_tile(g, bi, 0.5) & causal, tile_qk(g, ks_ref, bi) + tab_ref[0], NEG_INF)
        yield
        yield from first(s, vst_ref[g, bi], carry_s, g)

    _round_robin([head_part(g) for g in range(G)])

    def pair_step(g, carry, s0, vt0, s1, vt1, out):
        m, l, acc = carry
        m_new = jnp.maximum(m, jnp.maximum(_colmax(s0), _colmax(s1)))
        alpha = jnp.exp(m - m_new)
        p0 = jnp.exp(s0 - m_new)
        p1 = jnp.exp(s1 - m_new)
        yield
        l = alpha * l + _colsum(p0) + _colsum(p1)
        acc = alpha * acc + _dot(vt0, p0.astype(BF16)) + _dot(vt1, p1.astype(BF16))
        out[g] = (m_new, l, acc)

    def sel_step(g, j, carry, out):
        k0 = 2 * j
        k1 = jnp.minimum(k0 + 1, bi)
        thr1 = jnp.where(k0 + 1 < bi, 0.5, 2.0)
        s0 = jnp.where(sel_tile(g, k0, 0.5), tile_qk(g, ks_ref, k0) + tile_bias(k0), NEG_INF)
        s1 = jnp.where(sel_tile(g, k1, thr1), tile_qk(g, ks_ref, k1) + tile_bias(k1), NEG_INF)
        yield
        yield from pair_step(g, carry, s0, vst_ref[g, k0], s1, vst_ref[g, k1], out)

    def win_step(g, j, carry, out):
        k0 = lo + 2 * j
        k1 = jnp.minimum(k0 + 1, bi)
        s0 = jnp.where(win_mask(k0, True), tile_qk(g, kw_ref, k0) + tile_bias(k0), NEG_INF)
        s1 = jnp.where(win_mask(k1, k0 + 1 < bi), tile_qk(g, kw_ref, k1) + tile_bias(k1), NEG_INF)
        yield
        yield from pair_step(g, carry, s0, vwt_ref[g, k0], s1, vwt_ref[g, k1], out)

    def joint(step):
        def body(j, carries):
            out = [None] * G
            _round_robin([step(g, j, carries[g], out) for g in range(G)])
            return tuple(out)
        return body

    res_s = lax.fori_loop(0, (bi + 1) // 2, joint(sel_step), tuple(carry_s))
    res_w = lax.fori_loop(0, (bi - lo + 1) // 2, joint(win_step), tuple(carry_w))

    def tail_part(g):
        _, l_s, acc_s = res_s[g]
        _, l_w, acc_w = res_w[g]
        o_s = acc_s * (1.0 / l_s)
        o_w = acc_w * (1.0 / l_w)
        gate = jax.nn.sigmoid(gl_ref[g].T)
        yield
        outs = []
        for h in range(MIX_HEADS):
            cs = slice(h * QB, (h + 1) * QB)
            outs.append(gate[3 * h:3 * h + 1] * o_c[g][:, cs] + gate[3 * h + 1:3 * h + 2] * o_s[:, cs]
                        + gate[3 * h + 2:3 * h + 3] * o_w[:, cs])
        o_ref[g] = jnp.concatenate(outs, axis=0).T

    _round_robin([tail_part(g) for g in range(G)])


def nsa_attention(z, prep, bias_c, tab, ovt, ext, B, S, G):
    n_qb = S // Q_BLOCK
    n_kt = S // 128
    n_cmp = S // NSA_CMP_STRIDE
    R = MIX_HEADS * Q_BLOCK
    kc, vct, ks, vst, kw, vwt = prep
    b3 = lambda b, i: (b, 0, 0)
    b4 = lambda b, i: (b, 0, 0, 0)
    return pl.pallas_call(
        _nsa_kernel,
        grid=(B // G, n_qb),
        in_specs=[
            pl.BlockSpec((G, Q_BLOCK, MIX_WIDTH), lambda b, i: (b, i, OFF_NSAQ // MIX_WIDTH)),
            pl.BlockSpec((G, Q_BLOCK, 128), lambda b, i: (b, i, OFF_NSAG // 128)),
            pl.BlockSpec((G, n_cmp, 128), b3),
            pl.BlockSpec((G, HEAD_DIM, n_cmp), b3),
            pl.BlockSpec((G, S, 128), b3),
            pl.BlockSpec((G, n_kt, HEAD_DIM, 128), b4),
            pl.BlockSpec((G, S, 128), b3),
            pl.BlockSpec((G, n_kt, HEAD_DIM, 128), b4),
            pl.BlockSpec(bias_c.shape, lambda b, i: (0, 0)),
            pl.BlockSpec((3, 128, R), lambda b, i: (0, 0, 0)),
            pl.BlockSpec(ovt.shape, lambda b, i: (0, 0)),
            pl.BlockSpec(ext.shape, lambda b, i: (0, 0)),
        ],
        out_specs=pl.BlockSpec((G, Q_BLOCK, MIX_WIDTH), lambda b, i: (b, i, 0)),
        out_shape=jax.ShapeDtypeStruct((B, S, MIX_WIDTH), F32),
        scratch_shapes=[pltpu.VMEM((G, n_kt, 128, Q_BLOCK), F32)],
        compiler_params=_cparams(("parallel", "arbitrary")),
        name="nsa_attention",
    )(z, z, kc, vct, ks, vst, kw, vwt, bias_c, tab, ovt, ext)


def _stack_heads(x, head_of_lane):
    return jnp.concatenate([jnp.where(head_of_lane == h, x, jnp.zeros_like(x))
                            for h in range(MIX_HEADS)], axis=0)


def _head_norm(y, seg_mean, eps):
    mu = _dot_x2(y, seg_mean)
    d = y - mu
    var = _dot_x2(d * d, seg_mean)
    return d * lax.rsqrt(var + eps)


def _round_robin(chains):
    while chains:
        chains = [c for c in chains if next(c, StopIteration) is not StopIteration]


def _retention_kernel(z_ref, cos_ref, sin_ref, indec_ref, qdec_ref, kdec_ref, cdec_ref, bd_ref,
                      segm_ref, ng_ref, o_ref, st_ref):
    @pl.when(pl.program_id(1) == 0)
    def _():
        st_ref[...] = jnp.zeros_like(st_ref)

    _round_robin([_retention_chunk(z_ref.at[g], cos_ref, sin_ref, indec_ref, qdec_ref, kdec_ref, cdec_ref,
                                   bd_ref, segm_ref, ng_ref, o_ref.at[g], st_ref.at[g])
                  for g in range(z_ref.shape[0])])


def _retention_chunk(z_ref, cos_ref, sin_ref, indec_ref, qdec_ref, kdec_ref, cdec_ref, bd_ref,
                     segm_ref, ng_ref, o_ref, st_ref):
    L = RET_CHUNK
    z = z_ref[...]
    q = z[:, 0:256]
    k = z[:, 256:512]
    v = z[:, 512:768]
    g = z[:, 768:1024]
    cos = cos_ref[...]
    sin = sin_ref[...]

    def rot(u):
        u1 = u[:, :128]
        u2 = u[:, 128:]
        return jnp.concatenate([u1 * cos - u2 * sin, u2 * cos + u1 * sin], axis=1)

    qr = rot(q) * (HEAD_DIM ** -0.5)
    kr = rot(k)
    lane = lax.broadcasted_iota(jnp.int32, (L, 256), 1)
    head_qk = (lane & 127) >> 5
    head_v = lane >> 6
    qb = qr.astype(BF16)
    kb = kr.astype(BF16)
    vb = v.astype(BF16)
    yield

    att = _dot_nt(_stack_heads(qb, head_qk), kb) * indec_ref[...]
    yield
    o_st = _dot(att.astype(BF16), vb)
    o = jnp.zeros((L, 256), F32)
    for h in range(MIX_HEADS):
        o = o + jnp.where(head_v == h, o_st[h * L:(h + 1) * L], 0.0)
    yield
    state = st_ref[...]
    o = o + _dot(qb, state.astype(BF16)) * qdec_ref[...]
    st_ref[...] = state * cdec_ref[...] + _dot_tn((kr * kdec_ref[...]).astype(BF16), vb) * bd_ref[...]
    yield

    mu = _dot_x2(o, segm_ref[...])
    yield
    d = o - mu
    var = _dot_x2(d * d, segm_ref[...])
    yield
    o_ref[...] = g * jax.nn.sigmoid(g) * (d * lax.rsqrt(var + RET_NORM_EPS) * ng_ref[...])


def retention(z, tabs, ng, layer, B, S, G):
    L = RET_CHUNK
    nc = S // L
    cos, sin, indec, qdec, kdec, cdec, bd, segm = tabs
    const2 = lambda b, c: (0, 0)
    return pl.pallas_call(
        _retention_kernel,
        grid=(B // G, nc),
        in_specs=[
            pl.BlockSpec((G, L, 1024), lambda b, c: (b, c, OFF_RET // 1024)),
            pl.BlockSpec((L, 128), lambda b, c: (c, 0)),
            pl.BlockSpec((L, 128), lambda b, c: (c, 0)),
            pl.BlockSpec((MIX_HEADS * L, L), const2),
            pl.BlockSpec((L, 256), const2),
            pl.BlockSpec((L, 256), const2),
            pl.BlockSpec((1, 256), const2),
            pl.BlockSpec((256, 256), const2),
            pl.BlockSpec((256, 256), const2),
            pl.BlockSpec((None, 1, 256), lambda b, c: (layer, 0, 0)),
        ],
        out_specs=pl.BlockSpec((G, L, 256), lambda b, c: (b, c, 0)),
        out_shape=jax.ShapeDtypeStruct((B, S, 256), F32),
        scratch_shapes=[pltpu.VMEM((G, 256, 256), F32)],
        compiler_params=_cparams(("parallel", "arbitrary")),
        name="retention",
    )(z, cos, sin, indec, qdec, kdec, cdec, bd, segm, ng)


def _rwkv_kernel(z_ref, mu_ref, w2_ref, a2_ref, g2_ref, vec_ref, tri_ref, bdl_ref, wide_ref,
                 segs_ref, segm_ref, o_ref, st_ref, prev_ref):
    @pl.when(pl.program_id(1) == 0)
    def _():
        st_ref[...] = jnp.zeros_like(st_ref)
        prev_ref[...] = jnp.zeros_like(prev_ref)

    _round_robin([_rwkv_chunk(z_ref.at[g], mu_ref, w2_ref, a2_ref, g2_ref, vec_ref, tri_ref, bdl_ref,
                              wide_ref, segs_ref, segm_ref, o_ref.at[g], st_ref.at[g], prev_ref.at[g])
                  for g in range(z_ref.shape[0])])


def _rwkv_chunk(z_ref, mu_ref, w2_ref, a2_ref, g2_ref, vec_ref, tri_ref, bdl_ref, wide_ref,
                segs_ref, segm_ref, o_ref, st_ref, prev_ref):
    C = RWKV_CHUNK
    z = z_ref[...]
    rows = lax.broadcasted_iota(jnp.int32, (C, 1024), 0)
    zs = jnp.where(rows == 0, prev_ref[0:1, :], pltpu.roll(z, 1, axis=0))
    prev_ref[...] = jnp.broadcast_to(z[C - 1:C, :], prev_ref.shape)
    zf = z + (zs - z) * mu_ref[...]
    r = zf[:, 0:256]
    k = zf[:, 256:512]
    v = zf[:, 512:768]
    lora = zf[:, 768:896]

    vec = vec_ref[...]
    w0, a0, k_k, k_a, r_k, ln_g, ln_b = (vec[i:i + 1] for i in range(7))

    def lora_dot(x, w_ref):
        hi, lo = _split2(x)
        return _dot(hi, w_ref[0]) + _dot(lo, w_ref[0]) + _dot(hi, w_ref[1])

    wpre = w0 + lora_dot(jnp.tanh(lora), w2_ref)
    y = -wpre
    softplus = jnp.maximum(y, 0.0) + jnp.log(1.0 + jnp.exp(-jnp.abs(y)))
    w_log = -softplus - 0.5
    ld = -jnp.exp(w_log)
    a = jax.nn.sigmoid(a0 + lora_dot(lora, a2_ref))
    gate = lora_dot(jax.nn.sigmoid(lora), g2_ref)
    kk = k * k_k
    kk = kk / jnp.maximum(jnp.sqrt(_dot_x2(kk * kk, segs_ref[...])), 1e-12)
    k2 = k * (1.0 + (a - 1.0) * k_a)
    yield

    tri = tri_ref[...]
    l1 = ld.astype(BF16)
    r1 = ld - l1.astype(F32)
    l2 = r1.astype(BF16)
    l3 = (r1 - l2.astype(F32)).astype(BF16)
    cs = _dot(tri, l1) + _dot(tri, l2) + _dot(tri, l3)
    cs_end = cs[C - 1:C, :]
    yield
    e_neg = jnp.exp(-cs)
    e_end = jnp.exp(cs_end - cs)
    kka = kk * a
    a_t = -kk * jnp.exp(cs - ld)
    r_t = r * jnp.exp(cs)
    b_t = kka * e_neg
    k_t = k2 * e_neg
    b_g = kka * e_end
    k_g = k2 * e_end

    lane = lax.broadcasted_iota(jnp.int32, (C, 256), 1)
    head = lane >> 6
    stack = lambda x: _stack_heads(x.astype(BF16), head)
    a_st, b_st, k_st, v_st = stack(a_t), stack(b_t), stack(k_t), stack(v)

    n_pow = _dot_nt(a_st, b_st) * bdl_ref[...]
    ri = lax.broadcasted_iota(jnp.int32, (4 * C, 4 * C), 0)
    ci = lax.broadcasted_iota(jnp.int32, (4 * C, 4 * C), 1)
    t_inv = jnp.where(ri == ci, 1.0, 0.0) + n_pow
    yield

    ar = jnp.concatenate([a_t, r_t], axis=0).astype(BF16)
    bk = jnp.concatenate([b_st, k_st], axis=0)
    wide = _dot_nt(ar, bk)
    strict = wide_ref[0]
    incl = wide_ref[1]
    a_ak = (wide[0:C, 4 * C:] * strict).astype(BF16)
    a_rb = (wide[C:, 0:4 * C] * incl).astype(BF16)
    a_rk = (wide[C:, 4 * C:] * incl).astype(BF16)
    yield

    sq = 2
    while sq < C:
        nb = n_pow.astype(BF16)
        n_pow = _dot(nb, nb)
        yield
        t_inv = t_inv + _dot(t_inv.astype(BF16), n_pow.astype(BF16))
        sq *= 2
    yield

    state = st_ref[...]
    sb = state.astype(BF16)
    x = _dot_nt(a_t.astype(BF16), sb) + _dot(a_ak, v_st)
    yield
    u_st = _dot(t_inv.astype(BF16), stack(x)).astype(BF16)
    yield
    yv = _dot_nt(r_t.astype(BF16), sb) + _dot(a_rb, u_st) + _dot(a_rk, v_st)
    uv = jnp.concatenate([u_st, v_st], axis=0)
    bkg = jnp.concatenate([stack(b_g), stack(k_g)], axis=0)
    st_ref[...] = state * jnp.exp(cs_end) + _dot_tn(uv, bkg)
    yield

    yn = _head_norm(yv, segm_ref[...], RWKV_GN_EPS) * ln_g + ln_b
    yn = yn + _dot_x2(r * k2 * r_k, segs_ref[...]) * v
    o_ref[...] = yn * gate


def rwkv7(z, mu, w2, a2, g2, vec, consts, layer, B, S, G):
    C = RWKV_CHUNK
    nc = S // C
    tri, bdl, wide, segs, segm = consts
    const2 = lambda b, c: (0, 0)
    lw = lambda b, c: (layer, 0, 0, 0)
    return pl.pallas_call(
        _rwkv_kernel,
        grid=(B // G, nc),
        in_specs=[
            pl.BlockSpec((G, C, 1024), lambda b, c: (b, c, OFF_RWKV // 1024)),
            pl.BlockSpec((None, 1, 1024), lambda b, c: (layer, 0, 0)),
            pl.BlockSpec((None, 2, 128, 256), lw),
            pl.BlockSpec((None, 2, 128, 256), lw),
            pl.BlockSpec((None, 2, 128, 256), lw),
            pl.BlockSpec((None, 8, 256), lambda b, c: (layer, 0, 0)),
            pl.BlockSpec((C, C), const2),
            pl.BlockSpec((4 * C, 4 * C), const2),
            pl.BlockSpec((2, C, 4 * C), lambda b, c: (0, 0, 0)),
            pl.BlockSpec((256, 256), const2),
            pl.BlockSpec((256, 256), const2),
        ],
        out_specs=pl.BlockSpec((G, C, 256), lambda b, c: (b, c, 0)),
        out_shape=jax.ShapeDtypeStruct((B, S, 256), F32),
        scratch_shapes=[pltpu.VMEM((G, 256, 256), F32), pltpu.VMEM((G, 8, 1024), F32)],
        compiler_params=_cparams(("parallel", "arbitrary")),
        name="rwkv7",
    )(z, mu, w2, a2, g2, vec, tri, bdl, wide, segs, segm)


def _merge_kernel(x_ref, zc_ref, zg_ref, on_ref, or_ref, ow_ref, cw_ref, wb_ref, wo_ref, g_ref,
                  o_ref, carry_ref):
    tm = x_ref.shape[0]

    @pl.when(pl.program_id(1) == 0)
    def _():
        carry_ref[...] = jnp.zeros_like(carry_ref)

    zc = zc_ref[...]
    b_g = zc[:, 0:256]
    u = zc[:, 256:512] * zc[:, 512:768]
    prev = carry_ref[...]
    rows = lax.broadcasted_iota(jnp.int32, (tm, 256), 0)
    u1 = jnp.where(rows == 0, prev[7:8], pltpu.roll(u, 1, axis=0))
    u2 = jnp.where(rows == 0, prev[6:7], jnp.where(rows == 1, prev[7:8], pltpu.roll(u, 2, axis=0)))
    carry_ref[...] = u[tm - 8:tm]
    cw = cw_ref[...]
    o_conv = b_g * (cw[0:1] * u2 + cw[1:2] * u1 + cw[2:3] * u)

    branches = (on_ref[...], or_ref[...], ow_ref[...], o_conv)
    merged = jnp.zeros((tm, D_MODEL), F32)
    for m in range(N_BRANCHES):
        gm = jax.nn.sigmoid(zg_ref[:, m * D_MODEL:(m + 1) * D_MODEL])
        merged = merged + gm * _dot(branches[m].astype(BF16), wb_ref[m])
    y = _dot(merged.astype(BF16), wo_ref[...])
    o_ref[...] = x_ref[...] + _rms(y, g_ref[...])


def merge_mix(x, z, o_nsa, o_ret, o_rwkv, conv_w, w_branch, w_out, g_post, layer, B, S, tm):
    nt = S // tm
    rowmap = lambda b, i: (b * nt + i, 0)
    return pl.pallas_call(
        _merge_kernel,
        grid=(B, nt),
        in_specs=[
            pl.BlockSpec((tm, D_MODEL), rowmap),
            pl.BlockSpec((tm, 768), lambda b, i: (b * nt + i, OFF_CONV // 768)),
            pl.BlockSpec((tm, 4096), lambda b, i: (b * nt + i, 0)),
            pl.BlockSpec((tm, 256), rowmap),
            pl.BlockSpec((tm, 256), rowmap),
            pl.BlockSpec((tm, 256), rowmap),
            pl.BlockSpec((None, 8, 256), lambda b, i: (layer, 0, 0)),
            pl.BlockSpec((None, N_BRANCHES, MIX_WIDTH, D_MODEL), lambda b, i: (layer, 0, 0, 0)),
            pl.BlockSpec((None, D_MODEL, D_MODEL), lambda b, i: (layer, 0, 0)),
            pl.BlockSpec((None, 1, D_MODEL), lambda b, i: (layer, 0, 0)),
        ],
        out_specs=pl.BlockSpec((tm, D_MODEL), rowmap),
        out_shape=jax.ShapeDtypeStruct((B * S, D_MODEL), F32),
        scratch_shapes=[pltpu.VMEM((8, 256), F32)],
        compiler_params=_cparams(("parallel", "arbitrary")),
        name="merge_mix",
    )(x, z, z, o_nsa, o_ret, o_rwkv, conv_w, w_branch, w_out, g_post)


def _xattn_kernel(x_ref, kv_ref, gpre_ref, wq_ref, wo_ref, gpost_ref, o_ref):
    x = x_ref[...]
    h = _rms(x, gpre_ref[...]).astype(BF16)
    q = (_dot(h, wq_ref[...]) * (XA_HEAD_DIM ** -0.5)).astype(BF16)
    outs = []
    for hd in range(XA_HEADS):
        cs = slice(hd * XA_HEAD_DIM, (hd + 1) * XA_HEAD_DIM)
        kh = kv_ref[:, hd * XA_HEAD_DIM:(hd + 1) * XA_HEAD_DIM]
        vh = kv_ref[:, D_MODEL + hd * XA_HEAD_DIM:D_MODEL + (hd + 1) * XA_HEAD_DIM]
        s = _dot_nt(q[:, cs], kh)
        e = jnp.exp(s - _rowmax(s))
        p = e / _rowsum(e)
        outs.append(_dot(p.astype(BF16), vh).astype(BF16))
    o = jnp.concatenate(outs, axis=1)
    y = _dot(o, wo_ref[...])
    o_ref[...] = x + _rms(y, gpost_ref[...])


def cross_attention(x, kv, g_pre, wq, wo, g_post, layer, B, S, M, tm):
    nt = S // tm
    rowmap = lambda b, i: (b * nt + i, 0)
    lmap = lambda b, i: (layer, 0, 0)
    return pl.pallas_call(
        _xattn_kernel,
        grid=(B, nt),
        in_specs=[
            pl.BlockSpec((tm, D_MODEL), rowmap),
            pl.BlockSpec((M, 2 * D_MODEL), lambda b, i: (b, 0)),
            pl.BlockSpec((None, 1, D_MODEL), lmap),
            pl.BlockSpec((None, D_MODEL, D_MODEL), lmap),
            pl.BlockSpec((None, D_MODEL, D_MODEL), lmap),
            pl.BlockSpec((None, 1, D_MODEL), lmap),
        ],
        out_specs=pl.BlockSpec((tm, D_MODEL), rowmap),
        out_shape=jax.ShapeDtypeStruct((B * S, D_MODEL), F32),
        compiler_params=_cparams(("parallel", "arbitrary")),
        name="cross_attention",
    )(x, kv, g_pre, wq, wo, g_post)


def _mlp_kernel(x_ref, gpre_ref, w1_ref, w2_ref, gpost_ref, o_ref):
    x = x_ref[...]
    h = _rms(x, gpre_ref[...]).astype(BF16)
    tf = 1024
    acc = jnp.zeros(x.shape, F32)
    for f in range(D_FF // tf):
        a = jnp.maximum(_dot(h, w1_ref[:, f * tf:(f + 1) * tf]), 0.0)
        acc = acc + _dot((a * a).astype(BF16), w2_ref[f * tf:(f + 1) * tf, :])
    o_ref[...] = x + _rms(acc, gpost_ref[...])


def mlp(x, g_pre, w1, w2, g_post, layer, tm):
    M = x.shape[0]
    lmap = lambda i: (layer, 0, 0)
    return pl.pallas_call(
        _mlp_kernel,
        grid=(M // tm,),
        in_specs=[
            pl.BlockSpec((tm, D_MODEL), lambda i: (i, 0)),
            pl.BlockSpec((None, 1, D_MODEL), lmap),
            pl.BlockSpec((None, D_MODEL, D_FF), lmap, pipeline_mode=pl.Buffered(1)),
            pl.BlockSpec((None, D_FF, D_MODEL), lmap, pipeline_mode=pl.Buffered(1)),
            pl.BlockSpec((None, 1, D_MODEL), lmap),
        ],
        out_specs=pl.BlockSpec((tm, D_MODEL), lambda i: (i, 0)),
        out_shape=jax.ShapeDtypeStruct((M, D_MODEL), F32),
        compiler_params=_cparams(("parallel",)),
        name="mlp",
    )(x, g_pre, w1, w2, g_post)


def _t5_bucket(dist):
    n = jnp.maximum(dist, 0)
    max_exact = N_BUCKETS // 2
    nf = jnp.maximum(n, 1).astype(F32)
    large = max_exact + (jnp.log(nf / max_exact) / math.log(MAX_DISTANCE / max_exact)
                         * (N_BUCKETS - max_exact)).astype(jnp.int32)
    large = jnp.minimum(large, N_BUCKETS - 1)
    return jnp.where(n < max_exact, n, large)


def _nsa_tables(rel_bias, S):
    n_qb = S // Q_BLOCK
    n_cmp = S // NSA_CMP_STRIDE
    n_blk = S // NSA_SLC_LEN
    bias_f = rel_bias.astype(F32)
    i = jnp.arange(Q_BLOCK)
    per_qb = Q_BLOCK // NSA_CMP_STRIDE
    g = jnp.arange(n_cmp + per_qb * (n_qb - 1)) - per_qb * (n_qb - 1)
    d_c = i[None, :] - (g[:, None] * NSA_CMP_STRIDE + NSA_CMP_LEN - 1)
    bias_c = bias_f[_t5_bucket(d_c)].transpose(0, 2, 1).reshape(g.shape[0], MIX_HEADS * Q_BLOCK)
    d0 = i[:, None] - i[None, :]
    tab = jnp.stack([bias_f[_t5_bucket(d0)], bias_f[_t5_bucket(d0 + Q_BLOCK)],
                     jnp.broadcast_to(bias_f[N_BUCKETS - 1], (Q_BLOCK, Q_BLOCK, MIX_HEADS))])
    tab = tab.transpose(0, 2, 3, 1).reshape(3, Q_BLOCK, MIX_HEADS * Q_BLOCK)
    cs = np.arange(n_cmp)[None, :] * NSA_CMP_STRIDE
    ss = np.arange(n_blk)[:, None] * NSA_SLC_LEN
    ovt = np.clip(np.minimum(cs + NSA_CMP_LEN, ss + NSA_SLC_LEN) - np.maximum(cs, ss), 0, None)
    ovt = ovt.astype(np.float32) / NSA_CMP_LEN
    ovt[:, (S - NSA_CMP_LEN) // NSA_CMP_STRIDE + 1:] = 0.0
    ext = ((np.arange(S)[:, None] // NSA_SLC_LEN) == np.arange(n_blk)[None, :]).astype(np.float32)
    return bias_c, tab, jnp.asarray(ovt, BF16), jnp.asarray(ext, BF16)


def _ret_tables(S):
    L = RET_CHUNK
    H = MIX_HEADS
    half = HEAD_DIM // 2
    pos = jnp.arange(S)
    inv_freq = ROPE_BASE ** (-jnp.arange(half, dtype=F32) / half)
    ang = pos.astype(F32)[:, None] * inv_freq[None, :]
    cos = jnp.tile(jnp.cos(ang), (1, H))
    sin = jnp.tile(jnp.sin(ang), (1, H))
    lg = jnp.log(1.0 - 2.0 ** (-5.0 - jnp.arange(H, dtype=F32)))
    n = jnp.arange(L, dtype=F32)
    diff = n[:, None] - n[None, :]
    inner = jnp.where(diff >= 0, jnp.exp(jnp.maximum(diff, 0.0)[None] * lg[:, None, None]), 0.0)
    indec = inner.reshape(H * L, L)
    q_decay = jnp.exp((n + 1.0)[None, :] * lg[:, None])
    k_decay = jnp.exp((L - 1.0 - n)[None, :] * lg[:, None])
    chunk_decay = jnp.exp(L * lg)
    lane = np.arange(256)
    head_v = lane // HEAD_DIM
    head_qk = (lane % 128) // half
    qdec = q_decay.T[:, head_v]
    kdec = k_decay.T[:, head_qk]
    cdec = chunk_decay[head_v][None, :]
    bd = jnp.asarray((head_qk[:, None] == head_v[None, :]).astype(np.float32))
    segm = jnp.asarray((head_v[:, None] == head_v[None, :]).astype(np.float32) / HEAD_DIM, BF16)
    return cos, sin, indec, qdec, kdec, cdec, bd, segm


def _rwkv_consts():
    C = RWKV_CHUNK
    t = np.arange(C)
    tri = (t[:, None] >= t[None, :]).astype(np.float32)
    r = np.arange(4 * C)
    bdl = ((r[:, None] // C == r[None, :] // C) & (r[:, None] % C > r[None, :] % C)).astype(np.float32)
    strict = (t[:, None] > (r[None, :] % C)).astype(np.float32)
    incl = (t[:, None] >= (r[None, :] % C)).astype(np.float32)
    lane = np.arange(256) // HEAD_DIM
    seg = (lane[:, None] == lane[None, :]).astype(np.float32)
    return (jnp.asarray(tri, BF16), jnp.asarray(bdl), jnp.asarray(np.stack([strict, incl])),
            jnp.asarray(seg, BF16), jnp.asarray(seg / HEAD_DIM, BF16))


def _hi_lo(w):
    hi = w.astype(BF16)
    lo = (w - hi.astype(F32)).astype(BF16)
    return jnp.stack([hi, lo], axis=1)


def _pad_rows(w, top, total):
    return jnp.pad(w, ((0, 0), (top, total - top - w.shape[1]), (0, 0)))


def _layout_w_in(w_in):
    L = w_in.shape[0]
    o = 0
    nsa_q = w_in[:, :, o:o + 256]; o += 256
    nsa_kv = w_in[:, :, o:o + 384]; o += 384
    nsa_g = w_in[:, :, o:o + 12]; o += 12
    ret = w_in[:, :, o:o + 1024]; o += 1024
    rwkv = w_in[:, :, o:o + RWKV_COLS]; o += RWKV_COLS
    conv = w_in[:, :, o:o + 768]; o += 768
    gate = w_in[:, :, o:o + 4096]

    def rot_perm(w):
        return w.reshape(L, D_MODEL, MIX_HEADS, 2, HEAD_DIM // 2).transpose(0, 1, 3, 2, 4).reshape(L, D_MODEL, 256)

    ret = jnp.concatenate([rot_perm(ret[:, :, 0:256]), rot_perm(ret[:, :, 256:512]), ret[:, :, 512:]], axis=2)
    zeros = lambda n: jnp.zeros((L, D_MODEL, n), w_in.dtype)
    out = jnp.concatenate([gate, ret, rwkv, zeros(1024 - RWKV_COLS), conv, nsa_q, nsa_kv, nsa_g,
                           zeros(128 - 12)], axis=2)
    assert out.shape[2] == Z_COLS
    return out.astype(BF16)


def _layout_cmp(cmp_w, cmp_pe):
    L = cmp_w.shape[0]
    wk = cmp_w[:, 0]
    wv = cmp_w[:, 1]
    zero = jnp.zeros_like(wk)
    blk = jnp.concatenate([jnp.concatenate([wk, zero], axis=3), jnp.concatenate([zero, wv], axis=3)], axis=2)
    pe2 = jnp.concatenate([cmp_pe, cmp_pe], axis=2)[:, :, None, :]
    s = NSA_CMP_STRIDE
    return blk[:, :s].astype(BF16), blk[:, s:].astype(BF16), pe2[:, :s], pe2[:, s:]


def kernel(x, mem, ln_mix_pre, w_in, nsa_cmp_w, nsa_cmp_pe, rel_bias, ret_norm_g, rwkv_mu, rwkv_w0, rwkv_w2, rwkv_a0, rwkv_a2, rwkv_g2, rwkv_k_k, rwkv_k_a, rwkv_r_k, rwkv_ln_g, rwkv_ln_b, conv_w, w_branch, w_mix_out, ln_mix_post, ln_xa_pre, ln_mem, xa_wq, xa_wkv, xa_wo, ln_xa_post, ln_mlp_pre, mlp_w1, mlp_w2, ln_mlp_post):
    B, S, D = x.shape
    M = mem.shape[1]
    depth = w_in.shape[0]
    row = lambda g: g[:, None, :]

    w_in_b = _layout_w_in(w_in)
    cmp_lo, cmp_hi, pe_lo, pe_hi = _layout_cmp(nsa_cmp_w, nsa_cmp_pe)
    nsa_tabs = _nsa_tables(rel_bias, S)
    ret_tabs = _ret_tables(S)
    rwkv_consts = _rwkv_consts()
    mu = jnp.pad(rwkv_mu, ((0, 0), (0, 1024 - RWKV_COLS)))[:, None, :]
    w2 = _hi_lo(_pad_rows(rwkv_w2, 0, 128))
    a2 = _hi_lo(_pad_rows(rwkv_a2, RWKV_DECAY_LORA, 128))
    g2 = _hi_lo(_pad_rows(rwkv_g2, RWKV_DECAY_LORA + RWKV_AAA_LORA, 128))
    vec = jnp.stack([rwkv_w0, rwkv_a0, rwkv_k_k, rwkv_k_a, rwkv_r_k, rwkv_ln_g, rwkv_ln_b,
                     jnp.zeros_like(rwkv_w0)], axis=1)
    conv_p = jnp.pad(conv_w, ((0, 0), (0, 8 - CONV_WIDTH), (0, 0)))
    w_branch_b = w_branch.astype(BF16)
    w_mix_out_b = w_mix_out.astype(BF16)
    xa_wq_b = xa_wq.astype(BF16)
    xa_wkv_b = xa_wkv.astype(BF16)
    xa_wo_b = xa_wo.astype(BF16)
    mlp_w1_b = mlp_w1.astype(BF16)
    mlp_w2_b = mlp_w2.astype(BF16)

    xf = x.reshape(B * S, D)
    memf = mem.reshape(B * M, D)
    for l in range(depth):
        z = norm_matmul(xf, row(ln_mix_pre), w_in_b, l, tm=1024, tn=1280, out_dtype=F32)
        prep = nsa_compress(z, cmp_lo, cmp_hi, pe_lo, pe_hi, l, B, S)
        z3 = z.reshape(B, S, Z_COLS)
        o_nsa = nsa_attention(z3, prep, *nsa_tabs, B, S, G=4).reshape(B * S, MIX_WIDTH)
        o_ret = retention(z3, ret_tabs, row(ret_norm_g), l, B, S, G=4).reshape(B * S, MIX_WIDTH)
        o_rwkv = rwkv7(z3, mu, w2, a2, g2, vec, rwkv_consts, l, B, S, G=4).reshape(B * S, MIX_WIDTH)
        xf = merge_mix(xf, z, o_nsa, o_ret, o_rwkv, conv_p, w_branch_b, w_mix_out_b,
                       row(ln_mix_post), l, B, S, tm=256)
        kvm = norm_matmul(memf, row(ln_mem), xa_wkv_b, l, tm=M, tn=1024, out_dtype=BF16)
        xf = cross_attention(xf, kvm, row(ln_xa_pre), xa_wq_b, xa_wo_b, row(ln_xa_post), l, B, S, M, tm=512)
        xf = mlp(xf, row(ln_mlp_pre), mlp_w1_b, mlp_w2_b, row(ln_mlp_post), l, tm=512)
    return xf.reshape(B, S, D)
```

```python
import functools
import math

import numpy as np
import jax
import jax.numpy as jnp
from jax import lax
from jax.experimental import pallas as pl
from jax.experimental.pallas import tpu as pltpu

F32 = jnp.float32
BF16 = jnp.bfloat16

D_MODEL = 1024
N_BRANCHES = 4
MIX_WIDTH = 256
HEAD_DIM = 64
MIX_HEADS = 4

NSA_CMP_LEN = 32
NSA_CMP_STRIDE = 16
NSA_SLC_LEN = 64
NSA_TOP_N = 8
NSA_WINDOW = 512
Q_BLOCK = 128
FORCE_BONUS = 1e4
N_BUCKETS = 32
MAX_DISTANCE = 128

RET_CHUNK = 128
ROPE_BASE = 10000.0
RET_NORM_EPS = 1e-5

RWKV_DECAY_LORA = 32
RWKV_AAA_LORA = 32
RWKV_GATE_LORA = 64
RWKV_GN_EPS = 64e-5
RWKV_COLS = 3 * MIX_WIDTH + RWKV_DECAY_LORA + RWKV_AAA_LORA + RWKV_GATE_LORA
RWKV_CHUNK = 64

CONV_WIDTH = 3
XA_HEADS = 4
XA_HEAD_DIM = D_MODEL // XA_HEADS
D_FF = 4 * D_MODEL

RMS_EPS = 1e-6
LOG2E = math.log2(math.e)
NEG_INF = -1e30
NEG_BIG = -3e38

OFF_CONV = 0
OFF_NSAQ = 768
OFF_RET = 1024
OFF_RWKV = 2048
OFF_NSAKV = 3072
OFF_NSAG = 3456
Z_COLS = 3584

VMEM_LIMIT = 56 * 1024 * 1024


def _cparams(sem):
    return pltpu.CompilerParams(dimension_semantics=sem, vmem_limit_bytes=VMEM_LIMIT)


def _dot(a, b):
    return jnp.dot(a, b, preferred_element_type=F32)


def _dot_nt(a, b):
    return lax.dot_general(a, b, (((1,), (1,)), ((), ())), preferred_element_type=F32)


def _dot_tn(a, b):
    return lax.dot_general(a, b, (((0,), (0,)), ((), ())), preferred_element_type=F32)


def _split2(x):
    hi = x.astype(BF16)
    lo = (x - hi.astype(F32)).astype(BF16)
    return hi, lo


def _dot_x2(x, w_bf16):
    hi, lo = _split2(x)
    return _dot(hi, w_bf16) + _dot(lo, w_bf16)


def _rms(x, g):
    ms = jnp.mean(x * x, axis=-1, keepdims=True)
    return x * lax.rsqrt(ms + RMS_EPS) * g


def _norm_matmul_kernel(x_ref, g_ref, w_ref, o_ref, h_ref):
    @pl.when(pl.program_id(1) == 0)
    def _():
        h_ref[...] = _rms(x_ref[...], g_ref[...]).astype(BF16)

    o_ref[...] = _dot(h_ref[...], w_ref[...]).astype(o_ref.dtype)


def norm_matmul(x, g, w, layer, tm, tn, out_dtype):
    M, D = x.shape
    N = w.shape[2]
    return pl.pallas_call(
        _norm_matmul_kernel,
        grid=(M // tm, N // tn),
        in_specs=[
            pl.BlockSpec((tm, D), lambda i, j: (i, 0)),
            pl.BlockSpec((None, 1, D), lambda i, j: (layer, 0, 0)),
            pl.BlockSpec((None, D, tn), lambda i, j: (layer, 0, j)),
        ],
        out_specs=pl.BlockSpec((tm, tn), lambda i, j: (i, j)),
        out_shape=jax.ShapeDtypeStruct((M, N), out_dtype),
        scratch_shapes=[pltpu.VMEM((tm, D), BF16)],
        compiler_params=_cparams(("parallel", "arbitrary")),
        name="norm_matmul",
    )(x, g, w)


def _nsa_compress_kernel(kvc_ref, kvs_ref, kvw_ref, wlo_ref, whi_ref, pelo_ref, pehi_ref,
                         kc_ref, vct_ref, ks_ref, vst_ref, kw_ref, vwt_ref):
    nblk = kc_ref.shape[0]
    n_kt = vst_ref.shape[0]
    ylo = jnp.zeros((nblk, 128), F32)
    yhi = jnp.zeros((nblk, 128), F32)
    for r in range(NSA_CMP_STRIDE):
        xr = kvc_ref[pl.ds(r, nblk, stride=NSA_CMP_STRIDE), :]
        ylo = ylo + _dot((xr + pelo_ref[r]).astype(BF16), wlo_ref[r])
        yhi = yhi + _dot((xr + pehi_ref[r]).astype(BF16), whi_ref[r])
    y = ylo + pltpu.roll(yhi, nblk - 1, axis=0)
    kc_ref[...] = y.astype(BF16)
    vct_ref[...] = y.T[HEAD_DIM:, :].astype(BF16)
    ks_ref[...] = kvs_ref[...].astype(BF16)
    kw_ref[...] = kvw_ref[...].astype(BF16)
    for kt in range(n_kt):
        rs = slice(kt * 128, (kt + 1) * 128)
        vst_ref[kt] = kvs_ref[rs, :].T[HEAD_DIM:, :].astype(BF16)
        vwt_ref[kt] = kvw_ref[rs, :].T[HEAD_DIM:, :].astype(BF16)


def nsa_compress(z, wlo, whi, pelo, pehi, layer, B, S):
    nblk = S // NSA_CMP_STRIDE
    n_kt = S // 128
    c0 = OFF_NSAKV // 128
    wmap = lambda b: (layer, 0, 0, 0)
    b3 = lambda b: (b, 0, 0)
    b4 = lambda b: (b, 0, 0, 0)
    return pl.pallas_call(
        _nsa_compress_kernel,
        grid=(B,),
        in_specs=[
            pl.BlockSpec((S, 128), lambda b: (b, c0)),
            pl.BlockSpec((S, 128), lambda b: (b, c0 + 1)),
            pl.BlockSpec((S, 128), lambda b: (b, c0 + 2)),
            pl.BlockSpec((None, NSA_CMP_STRIDE, 128, 128), wmap),
            pl.BlockSpec((None, NSA_CMP_STRIDE, 128, 128), wmap),
            pl.BlockSpec((None, NSA_CMP_STRIDE, 1, 128), wmap),
            pl.BlockSpec((None, NSA_CMP_STRIDE, 1, 128), wmap),
        ],
        out_specs=[
            pl.BlockSpec((None, nblk, 128), b3),
            pl.BlockSpec((None, HEAD_DIM, nblk), b3),
            pl.BlockSpec((None, S, 128), b3),
            pl.BlockSpec((None, n_kt, HEAD_DIM, 128), b4),
            pl.BlockSpec((None, S, 128), b3),
            pl.BlockSpec((None, n_kt, HEAD_DIM, 128), b4),
        ],
        out_shape=[
            jax.ShapeDtypeStruct((B, nblk, 128), BF16),
            jax.ShapeDtypeStruct((B, HEAD_DIM, nblk), BF16),
            jax.ShapeDtypeStruct((B, S, 128), BF16),
            jax.ShapeDtypeStruct((B, n_kt, HEAD_DIM, 128), BF16),
            jax.ShapeDtypeStruct((B, S, 128), BF16),
            jax.ShapeDtypeStruct((B, n_kt, HEAD_DIM, 128), BF16),
        ],
        compiler_params=_cparams(("parallel",)),
        name="nsa_compress",
    )(z, z, z, wlo, whi, pelo, pehi)


def _rowmax(s):
    return jnp.max(s, axis=-1, keepdims=True)


def _rowsum(s):
    return jnp.sum(s, axis=-1, keepdims=True)


def _colmax(s):
    return jnp.max(s, axis=0, keepdims=True)


def _colsum(s):
    return jnp.sum(s, axis=0, keepdims=True)


def _nsa_kernel(q_ref, gl_ref, kc_ref, vct_ref, ks_ref, vst_ref, kw_ref, vwt_ref, bc_ref, tab_ref,
                ovt_ref, ext_ref, o_ref, selx_ref):
    bi = pl.program_id(1)
    G = q_ref.shape[0]
    n_kt = selx_ref.shape[1]
    n_blk = ovt_ref.shape[0]
    n_cmp = kc_ref.shape[1]
    QB = Q_BLOCK
    R = MIX_HEADS * QB
    n_off = NSA_WINDOW // 128

    key = lax.broadcasted_iota(jnp.int32, (128, R), 0)
    i4 = lax.broadcasted_iota(jnp.int32, (128, R), 1) & (QB - 1)
    causal = key <= i4
    lo = jnp.maximum(bi - n_off, 0)
    q4 = [None] * G
    o_c = [None] * G
    carry_s = [None] * G
    carry_w = [None] * G

    def tile_qk(g, kref, kt):
        start = pl.multiple_of(kt * 128, 128)
        return _dot(kref[g, pl.ds(start, 128), :HEAD_DIM], q4[g])

    key1 = lax.broadcasted_iota(jnp.int32, (128, QB), 0)
    i1 = lax.broadcasted_iota(jnp.int32, (128, QB), 1)
    kp = jnp.maximum(bi - 1, 0)
    pen_prev = jnp.where(bi >= 1, 0.0, NEG_INF)

    def lanes4(m1):
        return jnp.concatenate([m1, m1, m1, m1], axis=1)

    def sel_add(g, kt, pen):
        return lanes4(selx_ref[g, kt] + pen)

    def win_add(kt, valid):
        shift = jnp.where(valid, jnp.where(bi - kt < n_off, -4096, 0), 4096)
        return lanes4(jnp.where(key1 > i1 + shift, 0.0, NEG_INF))

    def first(s_d, vt_d, s_p, vt_p, out, g):
        m = jnp.maximum(_colmax(s_d), _colmax(s_p))
        p_d = jnp.exp2(s_d - m)
        p_p = jnp.exp2(s_p - m)
        yield
        out[g] = (m, _colsum(p_d) + _colsum(p_p),
                  _dot(vt_d, p_d.astype(BF16)) + _dot(vt_p, p_p.astype(BF16)))

    def head_part(g):
        qt = (q_ref[g] * (HEAD_DIM ** -0.5 * LOG2E)).T
        q4[g] = jnp.concatenate([qt[h * HEAD_DIM:(h + 1) * HEAD_DIM] for h in range(MIX_HEADS)],
                                axis=1).astype(BF16)
        yield
        s_d = jnp.where(causal, tile_qk(g, kw_ref, bi) + tab_ref[0], NEG_INF)
        s_p = tile_qk(g, kw_ref, kp) + (tab_ref[1] + pen_prev)
        yield
        yield from first(s_d, vwt_ref[g, bi], s_p, vwt_ref[g, kp], carry_w, g)
        yield
        g0 = pl.multiple_of((n_kt - 1 - bi) * 8, 8)
        s = _dot(kc_ref[g, :, :HEAD_DIM], q4[g]) + bc_ref[pl.ds(g0, n_cmp), :]
        nrow = lax.broadcasted_iota(jnp.int32, (n_cmp, R), 0)
        tq = bi * QB + (lax.broadcasted_iota(jnp.int32, (n_cmp, R), 1) & (QB - 1))
        s = jnp.where(tq - (nrow * NSA_CMP_STRIDE + (NSA_CMP_LEN - 1)) >= 0, s, NEG_INF)
        yield
        e = jnp.exp2(s - _colmax(s))
        p_c = e / _colsum(e)
        p_c = p_c * (tq >= NSA_CMP_LEN - 1).astype(F32)
        yield
        o_c[g] = _dot(vct_ref[g], p_c.astype(BF16))
        p_sum = p_c[:, 0:QB] + p_c[:, QB:2 * QB] + p_c[:, 2 * QB:3 * QB] + p_c[:, 3 * QB:4 * QB]
        p_hi, p_lo = _split2(p_sum)
        imp = _dot(ovt_ref[...], p_hi) + _dot(ovt_ref[...], p_lo)
        yield
        blk = lax.broadcasted_iota(jnp.int32, (n_blk, QB), 0)
        cur = (bi * QB + lax.broadcasted_iota(jnp.int32, (n_blk, QB), 1)) >> 6
        forced = (blk == 0) | (blk == cur) | (blk == cur - 1)
        imp = jnp.where(forced, imp + FORCE_BONUS, imp)
        imp = jnp.where(blk <= cur, imp, NEG_INF)
        blk_f = blk.astype(F32)
        sel = jnp.zeros((n_blk, QB), F32)
        for _ in range(NSA_TOP_N):
            mx = _colmax(imp)
            idx = jnp.min(jnp.where(imp == mx, blk_f, 4096.0), axis=0, keepdims=True)
            pick = blk_f == idx
            sel = jnp.where(pick, 1.0, sel)
            imp = jnp.where(pick, NEG_BIG, imp)
            yield
        selx = _dot(ext_ref[...], sel.astype(BF16))
        for kt in range(n_kt):
            selx_ref[g, kt] = (selx[kt * 128:(kt + 1) * 128, :] - 1.0) * (-NEG_INF)
        yield
        s_d = jnp.where(causal, tile_qk(g, ks_ref, bi) + tab_ref[0] + sel_add(g, bi, 0.0), NEG_INF)
        s_p = tile_qk(g, ks_ref, kp) + tab_ref[1] + sel_add(g, kp, pen_prev)
        yield
        yield from first(s_d, vst_ref[g, bi], s_p, vst_ref[g, kp], carry_s, g)

    _round_robin([head_part(g) for g in range(G)])

    def pair_step(g, carry, s0, vt0, s1, vt1, out):
        m, l, acc = carry
        m_new = jnp.maximum(m, jnp.maximum(_colmax(s0), _colmax(s1)))
        alpha = jnp.exp2(m - m_new)
        p0 = jnp.exp2(s0 - m_new)
        p1 = jnp.exp2(s1 - m_new)
        yield
        l = alpha * l + _colsum(p0) + _colsum(p1)
        acc = alpha * acc + _dot(vt0, p0.astype(BF16)) + _dot(vt1, p1.astype(BF16))
        out[g] = (m_new, l, acc)

    n_old = jnp.maximum(bi - 1, 0)

    def sel_step(g, j, carry, out):
        k0 = 2 * j
        k1 = jnp.minimum(k0 + 1, bi)
        pen1 = jnp.where(k0 + 1 < n_old, 0.0, NEG_INF)
        s0 = tile_qk(g, ks_ref, k0) + sel_add(g, k0, 0.0)
        s1 = tile_qk(g, ks_ref, k1) + sel_add(g, k1, pen1)
        yield
        yield from pair_step(g, carry, s0, vst_ref[g, k0], s1, vst_ref[g, k1], out)

    def win_step(g, j, carry, out):
        k0 = lo + 2 * j
        k1 = jnp.minimum(k0 + 1, bi)
        s0 = tile_qk(g, kw_ref, k0) + win_add(k0, True)
        s1 = tile_qk(g, kw_ref, k1) + win_add(k1, k0 + 1 < n_old)
        yield
        yield from pair_step(g, carry, s0, vwt_ref[g, k0], s1, vwt_ref[g, k1], out)

    def joint(step):
        def body(j, carries):
            out = [None] * G
            _round_robin([step(g, j, carries[g], out) for g in range(G)])
            return tuple(out)
        return body

    res_s = lax.fori_loop(0, (n_old + 1) // 2, joint(sel_step), tuple(carry_s))
    res_w = lax.fori_loop(0, (n_old - lo + 1) // 2, joint(win_step), tuple(carry_w))

    def tail_part(g):
        _, l_s, acc_s = res_s[g]
        _, l_w, acc_w = res_w[g]
        o_s = acc_s * (1.0 / l_s)
        o_w = acc_w * (1.0 / l_w)
        gate = jax.nn.sigmoid(gl_ref[g].T)
        yield
        outs = []
        for h in range(MIX_HEADS):
            cs = slice(h * QB, (h + 1) * QB)
            outs.append(gate[3 * h:3 * h + 1] * o_c[g][:, cs] + gate[3 * h + 1:3 * h + 2] * o_s[:, cs]
                        + gate[3 * h + 2:3 * h + 3] * o_w[:, cs])
        o_ref[g] = jnp.concatenate(outs, axis=0).T

    _round_robin([tail_part(g) for g in range(G)])


def nsa_attention(z, prep, bias_c, tab, ovt, ext, B, S, G):
    n_qb = S // Q_BLOCK
    n_kt = S // 128
    n_cmp = S // NSA_CMP_STRIDE
    R = MIX_HEADS * Q_BLOCK
    kc, vct, ks, vst, kw, vwt = prep
    b3 = lambda b, i: (b, 0, 0)
    b4 = lambda b, i: (b, 0, 0, 0)
    return pl.pallas_call(
        _nsa_kernel,
        grid=(B // G, n_qb),
        in_specs=[
            pl.BlockSpec((G, Q_BLOCK, MIX_WIDTH), lambda b, i: (b, i, OFF_NSAQ // MIX_WIDTH)),
            pl.BlockSpec((G, Q_BLOCK, 128), lambda b, i: (b, i, OFF_NSAG // 128)),
            pl.BlockSpec((G, n_cmp, 128), b3),
            pl.BlockSpec((G, HEAD_DIM, n_cmp), b3),
            pl.BlockSpec((G, S, 128), b3),
            pl.BlockSpec((G, n_kt, HEAD_DIM, 128), b4),
            pl.BlockSpec((G, S, 128), b3),
            pl.BlockSpec((G, n_kt, HEAD_DIM, 128), b4),
            pl.BlockSpec(bias_c.shape, lambda b, i: (0, 0)),
            pl.BlockSpec((2, 128, R), lambda b, i: (0, 0, 0)),
            pl.BlockSpec(ovt.shape, lambda b, i: (0, 0)),
            pl.BlockSpec(ext.shape, lambda b, i: (0, 0)),
        ],
        out_specs=pl.BlockSpec((G, Q_BLOCK, MIX_WIDTH), lambda b, i: (b, i, 0)),
        out_shape=jax.ShapeDtypeStruct((B, S, MIX_WIDTH), F32),
        scratch_shapes=[pltpu.VMEM((G, n_kt, 128, Q_BLOCK), F32)],
        compiler_params=_cparams(("parallel", "arbitrary")),
        name="nsa_attention",
    )(z, z, kc, vct, ks, vst, kw, vwt, bias_c, tab, ovt, ext)


def _stack_heads(x, head_of_lane):
    return jnp.concatenate([jnp.where(head_of_lane == h, x, jnp.zeros_like(x))
                            for h in range(MIX_HEADS)], axis=0)


def _head_norm(y, seg_mean, eps):
    mu = _dot_x2(y, seg_mean)
    d = y - mu
    var = _dot_x2(d * d, seg_mean)
    return d * lax.rsqrt(var + eps)


def _round_robin(chains):
    while chains:
        chains = [c for c in chains if next(c, StopIteration) is not StopIteration]


def _retention_kernel(z_ref, cos_ref, sin_ref, indec_ref, qdec_ref, kdec_ref, cdec_ref, bd_ref,
                      segm_ref, ng_ref, o_ref, st_ref):
    @pl.when(pl.program_id(1) == 0)
    def _():
        st_ref[...] = jnp.zeros_like(st_ref)

    _round_robin([_retention_chunk(z_ref.at[g], cos_ref, sin_ref, indec_ref, qdec_ref, kdec_ref, cdec_ref,
                                   bd_ref, segm_ref, ng_ref, o_ref.at[g], st_ref.at[g])
                  for g in range(z_ref.shape[0])])


def _retention_chunk(z_ref, cos_ref, sin_ref, indec_ref, qdec_ref, kdec_ref, cdec_ref, bd_ref,
                     segm_ref, ng_ref, o_ref, st_ref):
    L = RET_CHUNK
    z = z_ref[...]
    q = z[:, 0:256]
    k = z[:, 256:512]
    v = z[:, 512:768]
    g = z[:, 768:1024]
    cos = cos_ref[...]
    sin = sin_ref[...]

    def rot(u):
        u1 = u[:, :128]
        u2 = u[:, 128:]
        return jnp.concatenate([u1 * cos - u2 * sin, u2 * cos + u1 * sin], axis=1)

    qr = rot(q) * (HEAD_DIM ** -0.5)
    kr = rot(k)
    lane = lax.broadcasted_iota(jnp.int32, (L, 256), 1)
    head_qk = (lane & 127) >> 5
    head_v = lane >> 6
    qb = qr.astype(BF16)
    kb = kr.astype(BF16)
    vb = v.astype(BF16)
    yield

    att = _dot_nt(_stack_heads(qb, head_qk), kb) * indec_ref[...]
    yield
    o_st = _dot(att.astype(BF16), vb)
    o = jnp.zeros((L, 256), F32)
    for h in range(MIX_HEADS):
        o = o + jnp.where(head_v == h, o_st[h * L:(h + 1) * L], 0.0)
    yield
    state = st_ref[...]
    o = o + _dot(qb, state.astype(BF16)) * qdec_ref[...]
    st_ref[...] = state * cdec_ref[...] + _dot_tn((kr * kdec_ref[...]).astype(BF16), vb) * bd_ref[...]
    yield

    mu = _dot_x2(o, segm_ref[...])
    yield
    d = o - mu
    var = _dot_x2(d * d, segm_ref[...])
    yield
    o_ref[...] = g * jax.nn.sigmoid(g) * (d * lax.rsqrt(var + RET_NORM_EPS) * ng_ref[...])


def retention(z, tabs, ng, layer, B, S, G):
    L = RET_CHUNK
    nc = S // L
    cos, sin, indec, qdec, kdec, cdec, bd, segm = tabs
    const2 = lambda b, c: (0, 0)
    return pl.pallas_call(
        _retention_kernel,
        grid=(B // G, nc),
        in_specs=[
            pl.BlockSpec((G, L, 1024), lambda b, c: (b, c, OFF_RET // 1024)),
            pl.BlockSpec((L, 128), lambda b, c: (c, 0)),
            pl.BlockSpec((L, 128), lambda b, c: (c, 0)),
            pl.BlockSpec((MIX_HEADS * L, L), const2),
            pl.BlockSpec((L, 256), const2),
            pl.BlockSpec((L, 256), const2),
            pl.BlockSpec((1, 256), const2),
            pl.BlockSpec((256, 256), const2),
            pl.BlockSpec((256, 256), const2),
            pl.BlockSpec((None, 1, 256), lambda b, c: (layer, 0, 0)),
        ],
        out_specs=pl.BlockSpec((G, L, 256), lambda b, c: (b, c, 0)),
        out_shape=jax.ShapeDtypeStruct((B, S, 256), F32),
        scratch_shapes=[pltpu.VMEM((G, 256, 256), F32)],
        compiler_params=_cparams(("parallel", "arbitrary")),
        name="retention",
    )(z, cos, sin, indec, qdec, kdec, cdec, bd, segm, ng)


def _rwkv_kernel(z_ref, mu_ref, w2_ref, a2_ref, g2_ref, vec_ref, tri_ref, bdl_ref, wide_ref,
                 segs_ref, segm_ref, o_ref, st_ref, prev_ref):
    @pl.when(pl.program_id(1) == 0)
    def _():
        st_ref[...] = jnp.zeros_like(st_ref)
        prev_ref[...] = jnp.zeros_like(prev_ref)

    _round_robin([_rwkv_chunk(z_ref.at[g], mu_ref, w2_ref, a2_ref, g2_ref, vec_ref, tri_ref, bdl_ref,
                              wide_ref, segs_ref, segm_ref, o_ref.at[g], st_ref.at[g], prev_ref.at[g])
                  for g in range(z_ref.shape[0])])


def _rwkv_chunk(z_ref, mu_ref, w2_ref, a2_ref, g2_ref, vec_ref, tri_ref, bdl_ref, wide_ref,
                segs_ref, segm_ref, o_ref, st_ref, prev_ref):
    C = RWKV_CHUNK
    z = z_ref[...]
    rows = lax.broadcasted_iota(jnp.int32, (C, 1024), 0)
    zs = jnp.where(rows == 0, prev_ref[0:1, :], pltpu.roll(z, 1, axis=0))
    prev_ref[...] = jnp.broadcast_to(z[C - 1:C, :], prev_ref.shape)
    zf = z + (zs - z) * mu_ref[...]
    r = zf[:, 0:256]
    k = zf[:, 256:512]
    v = zf[:, 512:768]
    lora = zf[:, 768:896]

    vec = vec_ref[...]
    w0, a0, k_k, k_a, r_k, ln_g, ln_b = (vec[i:i + 1] for i in range(7))

    def lora_dot(x, w_ref):
        hi, lo = _split2(x)
        return _dot(hi, w_ref[0]) + _dot(lo, w_ref[0]) + _dot(hi, w_ref[1])

    wpre = w0 + lora_dot(jnp.tanh(lora), w2_ref)
    y = -wpre
    softplus = jnp.maximum(y, 0.0) + jnp.log(1.0 + jnp.exp(-jnp.abs(y)))
    w_log = -softplus - 0.5
    ld = -jnp.exp(w_log)
    a = jax.nn.sigmoid(a0 + lora_dot(lora, a2_ref))
    gate = lora_dot(jax.nn.sigmoid(lora), g2_ref)
    kk = k * k_k
    kk = kk / jnp.maximum(jnp.sqrt(_dot_x2(kk * kk, segs_ref[...])), 1e-12)
    k2 = k * (1.0 + (a - 1.0) * k_a)
    yield

    tri = tri_ref[...]
    l1 = ld.astype(BF16)
    r1 = ld - l1.astype(F32)
    l2 = r1.astype(BF16)
    l3 = (r1 - l2.astype(F32)).astype(BF16)
    cs = _dot(tri, l1) + _dot(tri, l2) + _dot(tri, l3)
    cs_end = cs[C - 1:C, :]
    yield
    e_neg = jnp.exp(-cs)
    e_end = jnp.exp(cs_end - cs)
    kka = kk * a
    a_t = -kk * jnp.exp(cs - ld)
    r_t = r * jnp.exp(cs)
    b_t = kka * e_neg
    k_t = k2 * e_neg
    b_g = kka * e_end
    k_g = k2 * e_end

    lane = lax.broadcasted_iota(jnp.int32, (C, 256), 1)
    head = lane >> 6
    stack = lambda x: _stack_heads(x.astype(BF16), head)
    a_st, b_st, k_st, v_st = stack(a_t), stack(b_t), stack(k_t), stack(v)

    n_pow = _dot_nt(a_st, b_st) * bdl_ref[...]
    ri = lax.broadcasted_iota(jnp.int32, (4 * C, 4 * C), 0)
    ci = lax.broadcasted_iota(jnp.int32, (4 * C, 4 * C), 1)
    t_inv = jnp.where(ri == ci, 1.0, 0.0) + n_pow
    yield

    ar = jnp.concatenate([a_t, r_t], axis=0).astype(BF16)
    bk = jnp.concatenate([b_st, k_st], axis=0)
    wide = _dot_nt(ar, bk)
    strict = wide_ref[0]
    incl = wide_ref[1]
    a_ak = (wide[0:C, 4 * C:] * strict).astype(BF16)
    a_rb = (wide[C:, 0:4 * C] * incl).astype(BF16)
    a_rk = (wide[C:, 4 * C:] * incl).astype(BF16)
    yield

    sq = 2
    while sq < C:
        nb = n_pow.astype(BF16)
        n_pow = _dot(nb, nb)
        yield
        t_inv = t_inv + _dot(t_inv.astype(BF16), n_pow.astype(BF16))
        sq *= 2
    yield

    state = st_ref[...]
    sb = state.astype(BF16)
    x = _dot_nt(a_t.astype(BF16), sb) + _dot(a_ak, v_st)
    yield
    u_st = _dot(t_inv.astype(BF16), stack(x)).astype(BF16)
    yield
    yv = _dot_nt(r_t.astype(BF16), sb) + _dot(a_rb, u_st) + _dot(a_rk, v_st)
    uv = jnp.concatenate([u_st, v_st], axis=0)
    bkg = jnp.concatenate([stack(b_g), stack(k_g)], axis=0)
    st_ref[...] = state * jnp.exp(cs_end) + _dot_tn(uv, bkg)
    yield

    yn = _head_norm(yv, segm_ref[...], RWKV_GN_EPS) * ln_g + ln_b
    yn = yn + _dot_x2(r * k2 * r_k, segs_ref[...]) * v
    o_ref[...] = yn * gate


def rwkv7(z, mu, w2, a2, g2, vec, consts, layer, B, S, G):
    C = RWKV_CHUNK
    nc = S // C
    tri, bdl, wide, segs, segm = consts
    const2 = lambda b, c: (0, 0)
    lw = lambda b, c: (layer, 0, 0, 0)
    return pl.pallas_call(
        _rwkv_kernel,
        grid=(B // G, nc),
        in_specs=[
            pl.BlockSpec((G, C, 1024), lambda b, c: (b, c, OFF_RWKV // 1024)),
            pl.BlockSpec((None, 1, 1024), lambda b, c: (layer, 0, 0)),
            pl.BlockSpec((None, 2, 128, 256), lw),
            pl.BlockSpec((None, 2, 128, 256), lw),
            pl.BlockSpec((None, 2, 128, 256), lw),
            pl.BlockSpec((None, 8, 256), lambda b, c: (layer, 0, 0)),
            pl.BlockSpec((C, C), const2),
            pl.BlockSpec((4 * C, 4 * C), const2),
            pl.BlockSpec((2, C, 4 * C), lambda b, c: (0, 0, 0)),
            pl.BlockSpec((256, 256), const2),
            pl.BlockSpec((256, 256), const2),
        ],
        out_specs=pl.BlockSpec((G, C, 256), lambda b, c: (b, c, 0)),
        out_shape=jax.ShapeDtypeStruct((B, S, 256), F32),
        scratch_shapes=[pltpu.VMEM((G, 256, 256), F32), pltpu.VMEM((G, 8, 1024), F32)],
        compiler_params=_cparams(("parallel", "arbitrary")),
        name="rwkv7",
    )(z, mu, w2, a2, g2, vec, tri, bdl, wide, segs, segm)


def _merge_kernel(x_ref, zc_ref, on_ref, or_ref, ow_ref, cw_ref, gpre_ref, wg_ref, wb_ref, wo_ref, g_ref,
                  o_ref, carry_ref):
    tm = x_ref.shape[0]
    x = x_ref[...]
    h = _rms(x, gpre_ref[...]).astype(BF16)

    @pl.when(pl.program_id(1) == 0)
    def _():
        carry_ref[...] = jnp.zeros_like(carry_ref)

    zc = zc_ref[...]
    b_g = zc[:, 0:256]
    u = zc[:, 256:512] * zc[:, 512:768]
    prev = carry_ref[...]
    rows = lax.broadcasted_iota(jnp.int32, (tm, 256), 0)
    u1 = jnp.where(rows == 0, prev[7:8], pltpu.roll(u, 1, axis=0))
    u2 = jnp.where(rows == 0, prev[6:7], jnp.where(rows == 1, prev[7:8], pltpu.roll(u, 2, axis=0)))
    carry_ref[...] = u[tm - 8:tm]
    cw = cw_ref[...]
    o_conv = b_g * (cw[0:1] * u2 + cw[1:2] * u1 + cw[2:3] * u)

    branches = (on_ref[...], or_ref[...], ow_ref[...], o_conv)
    merged = jnp.zeros((tm, D_MODEL), F32)
    for m in range(N_BRANCHES):
        gm = jax.nn.sigmoid(_dot(h, wg_ref[:, m * D_MODEL:(m + 1) * D_MODEL]))
        merged = merged + gm * _dot(branches[m].astype(BF16), wb_ref[m])
    y = _dot(merged.astype(BF16), wo_ref[...])
    o_ref[...] = x + _rms(y, g_ref[...])


def merge_mix(x, z, o_nsa, o_ret, o_rwkv, conv_w, g_pre, w_gate, w_branch, w_out, g_post, layer, B, S, tm):
    nt = S // tm
    rowmap = lambda b, i: (b * nt + i, 0)
    lmap = lambda b, i: (layer, 0, 0)
    return pl.pallas_call(
        _merge_kernel,
        grid=(B, nt),
        in_specs=[
            pl.BlockSpec((tm, D_MODEL), rowmap),
            pl.BlockSpec((tm, 768), lambda b, i: (b * nt + i, OFF_CONV // 768)),
            pl.BlockSpec((tm, 256), rowmap),
            pl.BlockSpec((tm, 256), rowmap),
            pl.BlockSpec((tm, 256), rowmap),
            pl.BlockSpec((None, 8, 256), lmap),
            pl.BlockSpec((None, 1, D_MODEL), lmap),
            pl.BlockSpec((None, D_MODEL, N_BRANCHES * D_MODEL), lmap, pipeline_mode=pl.Buffered(1)),
            pl.BlockSpec((None, N_BRANCHES, MIX_WIDTH, D_MODEL), lambda b, i: (layer, 0, 0, 0),
                         pipeline_mode=pl.Buffered(1)),
            pl.BlockSpec((None, D_MODEL, D_MODEL), lmap, pipeline_mode=pl.Buffered(1)),
            pl.BlockSpec((None, 1, D_MODEL), lmap),
        ],
        out_specs=pl.BlockSpec((tm, D_MODEL), rowmap),
        out_shape=jax.ShapeDtypeStruct((B * S, D_MODEL), F32),
        scratch_shapes=[pltpu.VMEM((8, 256), F32)],
        compiler_params=_cparams(("parallel", "arbitrary")),
        name="merge_mix",
    )(x, z, o_nsa, o_ret, o_rwkv, conv_w, g_pre, w_gate, w_branch, w_out, g_post)


def _xattn_kernel(x_ref, kv_ref, gpre_ref, wq_ref, wo_ref, gpost_ref, o_ref):
    x = x_ref[...]
    h = _rms(x, gpre_ref[...]).astype(BF16)
    q = (_dot(h, wq_ref[...]) * (XA_HEAD_DIM ** -0.5)).astype(BF16)
    outs = []
    for hd in range(XA_HEADS):
        cs = slice(hd * XA_HEAD_DIM, (hd + 1) * XA_HEAD_DIM)
        kh = kv_ref[:, hd * XA_HEAD_DIM:(hd + 1) * XA_HEAD_DIM]
        vh = kv_ref[:, D_MODEL + hd * XA_HEAD_DIM:D_MODEL + (hd + 1) * XA_HEAD_DIM]
        s = _dot_nt(q[:, cs], kh)
        e = jnp.exp(s - _rowmax(s))
        p = e / _rowsum(e)
        outs.append(_dot(p.astype(BF16), vh).astype(BF16))
    o = jnp.concatenate(outs, axis=1)
    y = _dot(o, wo_ref[...])
    o_ref[...] = x + _rms(y, gpost_ref[...])


def cross_attention(x, kv, g_pre, wq, wo, g_post, layer, B, S, M, tm):
    nt = S // tm
    rowmap = lambda b, i: (b * nt + i, 0)
    lmap = lambda b, i: (layer, 0, 0)
    return pl.pallas_call(
        _xattn_kernel,
        grid=(B, nt),
        in_specs=[
            pl.BlockSpec((tm, D_MODEL), rowmap),
            pl.BlockSpec((M, 2 * D_MODEL), lambda b, i: (b, 0)),
            pl.BlockSpec((None, 1, D_MODEL), lmap),
            pl.BlockSpec((None, D_MODEL, D_MODEL), lmap),
            pl.BlockSpec((None, D_MODEL, D_MODEL), lmap),
            pl.BlockSpec((None, 1, D_MODEL), lmap),
        ],
        out_specs=pl.BlockSpec((tm, D_MODEL), rowmap),
        out_shape=jax.ShapeDtypeStruct((B * S, D_MODEL), F32),
        compiler_params=_cparams(("parallel", "arbitrary")),
        name="cross_attention",
    )(x, kv, g_pre, wq, wo, g_post)


def _mlp_kernel(x_ref, gpre_ref, w1_ref, w2_ref, gpost_ref, o_ref):
    x = x_ref[...]
    h = _rms(x, gpre_ref[...]).astype(BF16)
    tf = 1024
    acc = jnp.zeros(x.shape, F32)
    for f in range(D_FF // tf):
        a = jnp.maximum(_dot(h, w1_ref[:, f * tf:(f + 1) * tf]), 0.0)
        acc = acc + _dot((a * a).astype(BF16), w2_ref[f * tf:(f + 1) * tf, :])
    o_ref[...] = x + _rms(acc, gpost_ref[...])


def mlp(x, g_pre, w1, w2, g_post, layer, tm):
    M = x.shape[0]
    lmap = lambda i: (layer, 0, 0)
    return pl.pallas_call(
        _mlp_kernel,
        grid=(M // tm,),
        in_specs=[
            pl.BlockSpec((tm, D_MODEL), lambda i: (i, 0)),
            pl.BlockSpec((None, 1, D_MODEL), lmap),
            pl.BlockSpec((None, D_MODEL, D_FF), lmap, pipeline_mode=pl.Buffered(1)),
            pl.BlockSpec((None, D_FF, D_MODEL), lmap, pipeline_mode=pl.Buffered(1)),
            pl.BlockSpec((None, 1, D_MODEL), lmap),
        ],
        out_specs=pl.BlockSpec((tm, D_MODEL), lambda i: (i, 0)),
        out_shape=jax.ShapeDtypeStruct((M, D_MODEL), F32),
        compiler_params=_cparams(("parallel",)),
        name="mlp",
    )(x, g_pre, w1, w2, g_post)


def _t5_bucket(dist):
    n = jnp.maximum(dist, 0)
    max_exact = N_BUCKETS // 2
    nf = jnp.maximum(n, 1).astype(F32)
    large = max_exact + (jnp.log(nf / max_exact) / math.log(MAX_DISTANCE / max_exact)
                         * (N_BUCKETS - max_exact)).astype(jnp.int32)
    large = jnp.minimum(large, N_BUCKETS - 1)
    return jnp.where(n < max_exact, n, large)


def _nsa_tables(rel_bias, S):
    n_qb = S // Q_BLOCK
    n_cmp = S // NSA_CMP_STRIDE
    n_blk = S // NSA_SLC_LEN
    bias_f = rel_bias.astype(F32)
    i = jnp.arange(Q_BLOCK)
    per_qb = Q_BLOCK // NSA_CMP_STRIDE
    g = jnp.arange(n_cmp + per_qb * (n_qb - 1)) - per_qb * (n_qb - 1)
    d_c = i[None, :] - (g[:, None] * NSA_CMP_STRIDE + NSA_CMP_LEN - 1)
    def lookup(dist):
        hit = _t5_bucket(dist)[..., None, None] == jnp.arange(N_BUCKETS)[:, None]
        return jnp.sum(jnp.where(hit, bias_f, 0.0), axis=-2)

    bias_c = lookup(d_c).transpose(0, 2, 1).reshape(g.shape[0], MIX_HEADS * Q_BLOCK)
    d0 = i[:, None] - i[None, :]
    tab = jnp.stack([lookup(d0), lookup(d0 + Q_BLOCK)]) - bias_f[N_BUCKETS - 1]
    tab = tab.transpose(0, 2, 3, 1).reshape(2, Q_BLOCK, MIX_HEADS * Q_BLOCK)
    cs = np.arange(n_cmp)[None, :] * NSA_CMP_STRIDE
    ss = np.arange(n_blk)[:, None] * NSA_SLC_LEN
    ovt = np.clip(np.minimum(cs + NSA_CMP_LEN, ss + NSA_SLC_LEN) - np.maximum(cs, ss), 0, None)
    ovt = ovt.astype(np.float32) / NSA_CMP_LEN
    ovt[:, (S - NSA_CMP_LEN) // NSA_CMP_STRIDE + 1:] = 0.0
    ext = ((np.arange(S)[:, None] // NSA_SLC_LEN) == np.arange(n_blk)[None, :]).astype(np.float32)
    return bias_c * LOG2E, tab * LOG2E, jnp.asarray(ovt, BF16), jnp.asarray(ext, BF16)


def _ret_tables(S):
    L = RET_CHUNK
    H = MIX_HEADS
    half = HEAD_DIM // 2
    pos = jnp.arange(S)
    inv_freq = ROPE_BASE ** (-jnp.arange(half, dtype=F32) / half)
    ang = pos.astype(F32)[:, None] * inv_freq[None, :]
    cos = jnp.tile(jnp.cos(ang), (1, H))
    sin = jnp.tile(jnp.sin(ang), (1, H))
    lg = jnp.log(1.0 - 2.0 ** (-5.0 - jnp.arange(H, dtype=F32)))
    n = jnp.arange(L, dtype=F32)
    diff = n[:, None] - n[None, :]
    inner = jnp.where(diff >= 0, jnp.exp(jnp.maximum(diff, 0.0)[None] * lg[:, None, None]), 0.0)
    indec = inner.reshape(H * L, L)
    q_decay = jnp.exp((n + 1.0)[None, :] * lg[:, None])
    k_decay = jnp.exp((L - 1.0 - n)[None, :] * lg[:, None])
    chunk_decay = jnp.exp(L * lg)
    lane = np.arange(256)
    head_v = lane // HEAD_DIM
    head_qk = (lane % 128) // half
    qdec = q_decay.T[:, head_v]
    kdec = k_decay.T[:, head_qk]
    cdec = chunk_decay[head_v][None, :]
    bd = jnp.asarray((head_qk[:, None] == head_v[None, :]).astype(np.float32))
    segm = jnp.asarray((head_v[:, None] == head_v[None, :]).astype(np.float32) / HEAD_DIM, BF16)
    return cos, sin, indec, qdec, kdec, cdec, bd, segm


def _rwkv_consts():
    C = RWKV_CHUNK
    t = np.arange(C)
    tri = (t[:, None] >= t[None, :]).astype(np.float32)
    r = np.arange(4 * C)
    bdl = ((r[:, None] // C == r[None, :] // C) & (r[:, None] % C > r[None, :] % C)).astype(np.float32)
    strict = (t[:, None] > (r[None, :] % C)).astype(np.float32)
    incl = (t[:, None] >= (r[None, :] % C)).astype(np.float32)
    lane = np.arange(256) // HEAD_DIM
    seg = (lane[:, None] == lane[None, :]).astype(np.float32)
    return (jnp.asarray(tri, BF16), jnp.asarray(bdl), jnp.asarray(np.stack([strict, incl])),
            jnp.asarray(seg, BF16), jnp.asarray(seg / HEAD_DIM, BF16))


def _hi_lo(w):
    hi = w.astype(BF16)
    lo = (w - hi.astype(F32)).astype(BF16)
    return jnp.stack([hi, lo], axis=1)


def _pad_rows(w, top, total):
    return jnp.pad(w, ((0, 0), (top, total - top - w.shape[1]), (0, 0)))


def _layout_w_in(w_in):
    L = w_in.shape[0]
    o = 0
    nsa_q = w_in[:, :, o:o + 256]; o += 256
    nsa_kv = w_in[:, :, o:o + 384]; o += 384
    nsa_g = w_in[:, :, o:o + 12]; o += 12
    ret = w_in[:, :, o:o + 1024]; o += 1024
    rwkv = w_in[:, :, o:o + RWKV_COLS]; o += RWKV_COLS
    conv = w_in[:, :, o:o + 768]; o += 768
    gate = w_in[:, :, o:o + 4096]

    def rot_perm(w):
        return w.reshape(L, D_MODEL, MIX_HEADS, 2, HEAD_DIM // 2).transpose(0, 1, 3, 2, 4).reshape(L, D_MODEL, 256)

    ret = jnp.concatenate([rot_perm(ret[:, :, 0:256]), rot_perm(ret[:, :, 256:512]), ret[:, :, 512:]], axis=2)
    zeros = lambda n: jnp.zeros((L, D_MODEL, n), w_in.dtype)
    out = jnp.concatenate([conv, nsa_q, ret, rwkv, zeros(1024 - RWKV_COLS), nsa_kv, nsa_g,
                           zeros(128 - 12)], axis=2)
    assert out.shape[2] == Z_COLS
    return out.astype(BF16), gate.astype(BF16)


def _layout_cmp(cmp_w, cmp_pe):
    L = cmp_w.shape[0]
    wk = cmp_w[:, 0]
    wv = cmp_w[:, 1]
    zero = jnp.zeros_like(wk)
    blk = jnp.concatenate([jnp.concatenate([wk, zero], axis=3), jnp.concatenate([zero, wv], axis=3)], axis=2)
    pe2 = jnp.concatenate([cmp_pe, cmp_pe], axis=2)[:, :, None, :]
    s = NSA_CMP_STRIDE
    return blk[:, :s].astype(BF16), blk[:, s:].astype(BF16), pe2[:, :s], pe2[:, s:]


def kernel(x, mem, ln_mix_pre, w_in, nsa_cmp_w, nsa_cmp_pe, rel_bias, ret_norm_g, rwkv_mu, rwkv_w0, rwkv_w2, rwkv_a0, rwkv_a2, rwkv_g2, rwkv_k_k, rwkv_k_a, rwkv_r_k, rwkv_ln_g, rwkv_ln_b, conv_w, w_branch, w_mix_out, ln_mix_post, ln_xa_pre, ln_mem, xa_wq, xa_wkv, xa_wo, ln_xa_post, ln_mlp_pre, mlp_w1, mlp_w2, ln_mlp_post):
    B, S, D = x.shape
    M = mem.shape[1]
    depth = w_in.shape[0]
    row = lambda g: g[:, None, :]

    w_in_b, w_gate_b = _layout_w_in(w_in)
    cmp_lo, cmp_hi, pe_lo, pe_hi = _layout_cmp(nsa_cmp_w, nsa_cmp_pe)
    nsa_tabs = _nsa_tables(rel_bias, S)
    ret_tabs = _ret_tables(S)
    rwkv_consts = _rwkv_consts()
    mu = jnp.pad(rwkv_mu, ((0, 0), (0, 1024 - RWKV_COLS)))[:, None, :]
    w2 = _hi_lo(_pad_rows(rwkv_w2, 0, 128))
    a2 = _hi_lo(_pad_rows(rwkv_a2, RWKV_DECAY_LORA, 128))
    g2 = _hi_lo(_pad_rows(rwkv_g2, RWKV_DECAY_LORA + RWKV_AAA_LORA, 128))
    vec = jnp.stack([rwkv_w0, rwkv_a0, rwkv_k_k, rwkv_k_a, rwkv_r_k, rwkv_ln_g, rwkv_ln_b,
                     jnp.zeros_like(rwkv_w0)], axis=1)
    conv_p = jnp.pad(conv_w, ((0, 0), (0, 8 - CONV_WIDTH), (0, 0)))
    w_branch_b = w_branch.astype(BF16)
    w_mix_out_b = w_mix_out.astype(BF16)
    xa_wq_b = xa_wq.astype(BF16)
    xa_wkv_b = xa_wkv.astype(BF16)
    xa_wo_b = xa_wo.astype(BF16)
    mlp_w1_b = mlp_w1.astype(BF16)
    mlp_w2_b = mlp_w2.astype(BF16)

    xf = x.reshape(B * S, D)
    memf = mem.reshape(B * M, D)
    for l in range(depth):
        z = norm_matmul(xf, row(ln_mix_pre), w_in_b, l, tm=1024, tn=896, out_dtype=F32)
        prep = nsa_compress(z, cmp_lo, cmp_hi, pe_lo, pe_hi, l, B, S)
        z3 = z.reshape(B, S, Z_COLS)
        o_nsa = nsa_attention(z3, prep, *nsa_tabs, B, S, G=4).reshape(B * S, MIX_WIDTH)
        o_ret = retention(z3, ret_tabs, row(ret_norm_g), l, B, S, G=4).reshape(B * S, MIX_WIDTH)
        o_rwkv = rwkv7(z3, mu, w2, a2, g2, vec, rwkv_consts, l, B, S, G=4).reshape(B * S, MIX_WIDTH)
        xf = merge_mix(xf, z, o_nsa, o_ret, o_rwkv, conv_p, row(ln_mix_pre), w_gate_b, w_branch_b,
                       w_mix_out_b, row(ln_mix_post), l, B, S, tm=512)
        kvm = norm_matmul(memf, row(ln_mem), xa_wkv_b, l, tm=M, tn=1024, out_dtype=BF16)
        xf = cross_attention(xf, kvm, row(ln_xa_pre), xa_wq_b, xa_wo_b, row(ln_xa_post), l, B, S, M, tm=512)
        xf = mlp(xf, row(ln_mlp_pre), mlp_w1_b, mlp_w2_b, row(ln_mlp_post), l, tm=512)
    return xf.reshape(B, S, D)
```

```python
import functools
import math

import numpy as np
import jax
import jax.numpy as jnp
from jax import lax
from jax.experimental import pallas as pl
from jax.experimental.pallas import tpu as pltpu

F32 = jnp.float32
BF16 = jnp.bfloat16

D_MODEL = 1024
N_BRANCHES = 4
MIX_WIDTH = 256
HEAD_DIM = 64
MIX_HEADS = 4

NSA_CMP_LEN = 32
NSA_CMP_STRIDE = 16
NSA_SLC_LEN = 64
NSA_TOP_N = 8
NSA_WINDOW = 512
Q_BLOCK = 128
FORCE_BONUS = 1e4
N_BUCKETS = 32
MAX_DISTANCE = 128

RET_CHUNK = 128
ROPE_BASE = 10000.0
RET_NORM_EPS = 1e-5

RWKV_DECAY_LORA = 32
RWKV_AAA_LORA = 32
RWKV_GATE_LORA = 64
RWKV_GN_EPS = 64e-5
RWKV_COLS = 3 * MIX_WIDTH + RWKV_DECAY_LORA + RWKV_AAA_LORA + RWKV_GATE_LORA
RWKV_CHUNK = 64

CONV_WIDTH = 3
XA_HEADS = 4
XA_HEAD_DIM = D_MODEL // XA_HEADS
D_FF = 4 * D_MODEL

RMS_EPS = 1e-6
LOG2E = math.log2(math.e)
NEG_INF = -1e30
NEG_BIG = -3e38

OFF_CONV = 0
OFF_NSAQ = 768
OFF_RET = 1024
OFF_RWKV = 2048
OFF_NSAKV = 3072
OFF_NSAG = 3456
Z_COLS = 3584

VMEM_LIMIT = 56 * 1024 * 1024


def _cparams(sem):
    return pltpu.CompilerParams(dimension_semantics=sem, vmem_limit_bytes=VMEM_LIMIT)


def _dot(a, b):
    return jnp.dot(a, b, preferred_element_type=F32)


def _dot_nt(a, b):
    return lax.dot_general(a, b, (((1,), (1,)), ((), ())), preferred_element_type=F32)


def _dot_tn(a, b):
    return lax.dot_general(a, b, (((0,), (0,)), ((), ())), preferred_element_type=F32)


def _split2(x):
    hi = x.astype(BF16)
    lo = (x - hi.astype(F32)).astype(BF16)
    return hi, lo


def _dot_x2(x, w_bf16, *more):
    xs = (x,) + more
    rows = x.shape[0]
    parts = [p for xi in xs for p in _split2(xi)]
    out = _dot(jnp.concatenate(parts, axis=0), w_bf16)
    res = [out[2 * i * rows:(2 * i + 1) * rows] + out[(2 * i + 1) * rows:(2 * i + 2) * rows]
           for i in range(len(xs))]
    return res[0] if not more else res


def _rms(x, g):
    ms = jnp.mean(x * x, axis=-1, keepdims=True)
    return x * lax.rsqrt(ms + RMS_EPS) * g


def _norm_matmul_kernel(x_ref, g_ref, w_ref, o_ref, h_ref):
    @pl.when(pl.program_id(1) == 0)
    def _():
        h_ref[...] = _rms(x_ref[...], g_ref[...]).astype(BF16)

    o_ref[...] = _dot(h_ref[...], w_ref[...]).astype(o_ref.dtype)


def norm_matmul(x, g, w, layer, tm, tn, out_dtype):
    M, D = x.shape
    N = w.shape[2]
    w_mode = dict(pipeline_mode=pl.Buffered(1)) if tn == N else {}
    return pl.pallas_call(
        _norm_matmul_kernel,
        grid=(M // tm, N // tn),
        in_specs=[
            pl.BlockSpec((tm, D), lambda i, j: (i, 0)),
            pl.BlockSpec((None, 1, D), lambda i, j: (layer, 0, 0)),
            pl.BlockSpec((None, D, tn), lambda i, j: (layer, 0, j), **w_mode),
        ],
        out_specs=pl.BlockSpec((tm, tn), lambda i, j: (i, j)),
        out_shape=jax.ShapeDtypeStruct((M, N), out_dtype),
        scratch_shapes=[pltpu.VMEM((tm, D), BF16)],
        compiler_params=_cparams(("parallel", "arbitrary")),
        name="norm_matmul",
    )(x, g, w)


def _nsa_compress_kernel(kvc_ref, kvs_ref, kvw_ref, wlo_ref, whi_ref, pelo_ref, pehi_ref,
                         kc_ref, vct_ref, ks_ref, vst_ref, kw_ref, vwt_ref):
    nblk = kc_ref.shape[0]
    n_kt = vst_ref.shape[0]
    ylo = jnp.zeros((nblk, 128), F32)
    yhi = jnp.zeros((nblk, 128), F32)
    for r in range(NSA_CMP_STRIDE):
        xr = kvc_ref[pl.ds(r, nblk, stride=NSA_CMP_STRIDE), :]
        ylo = ylo + _dot((xr + pelo_ref[r]).astype(BF16), wlo_ref[r])
        yhi = yhi + _dot((xr + pehi_ref[r]).astype(BF16), whi_ref[r])
    y = ylo + pltpu.roll(yhi, nblk - 1, axis=0)
    kc_ref[...] = y.astype(BF16)
    vct_ref[...] = y.T[HEAD_DIM:, :].astype(BF16)
    ks_ref[...] = kvs_ref[...].astype(BF16)
    kw_ref[...] = kvw_ref[...].astype(BF16)
    for kt in range(n_kt):
        rs = slice(kt * 128, (kt + 1) * 128)
        vst_ref[kt] = kvs_ref[rs, :].T[HEAD_DIM:, :].astype(BF16)
        vwt_ref[kt] = kvw_ref[rs, :].T[HEAD_DIM:, :].astype(BF16)


def nsa_compress(z, wlo, whi, pelo, pehi, layer, B, S):
    nblk = S // NSA_CMP_STRIDE
    n_kt = S // 128
    c0 = OFF_NSAKV // 128
    wmap = lambda b: (layer, 0, 0, 0)
    b3 = lambda b: (b, 0, 0)
    b4 = lambda b: (b, 0, 0, 0)
    return pl.pallas_call(
        _nsa_compress_kernel,
        grid=(B,),
        in_specs=[
            pl.BlockSpec((S, 128), lambda b: (b, c0)),
            pl.BlockSpec((S, 128), lambda b: (b, c0 + 1)),
            pl.BlockSpec((S, 128), lambda b: (b, c0 + 2)),
            pl.BlockSpec((None, NSA_CMP_STRIDE, 128, 128), wmap),
            pl.BlockSpec((None, NSA_CMP_STRIDE, 128, 128), wmap),
            pl.BlockSpec((None, NSA_CMP_STRIDE, 1, 128), wmap),
            pl.BlockSpec((None, NSA_CMP_STRIDE, 1, 128), wmap),
        ],
        out_specs=[
            pl.BlockSpec((None, nblk, 128), b3),
            pl.BlockSpec((None, HEAD_DIM, nblk), b3),
            pl.BlockSpec((None, S, 128), b3),
            pl.BlockSpec((None, n_kt, HEAD_DIM, 128), b4),
            pl.BlockSpec((None, S, 128), b3),
            pl.BlockSpec((None, n_kt, HEAD_DIM, 128), b4),
        ],
        out_shape=[
            jax.ShapeDtypeStruct((B, nblk, 128), BF16),
            jax.ShapeDtypeStruct((B, HEAD_DIM, nblk), BF16),
            jax.ShapeDtypeStruct((B, S, 128), BF16),
            jax.ShapeDtypeStruct((B, n_kt, HEAD_DIM, 128), BF16),
            jax.ShapeDtypeStruct((B, S, 128), BF16),
            jax.ShapeDtypeStruct((B, n_kt, HEAD_DIM, 128), BF16),
        ],
        compiler_params=_cparams(("parallel",)),
        name="nsa_compress",
    )(z, z, z, wlo, whi, pelo, pehi)


def _rowmax(s):
    return jnp.max(s, axis=-1, keepdims=True)


def _rowsum(s):
    return jnp.sum(s, axis=-1, keepdims=True)


def _colmax(s):
    return jnp.max(s, axis=0, keepdims=True)


def _colsum(s):
    return jnp.sum(s, axis=0, keepdims=True)


def _nsa_kernel(q_ref, gl_ref, kc_ref, vct_ref, ks_ref, vst_ref, kw_ref, vwt_ref, bc_ref, tab_ref,
                ovt_ref, ext_ref, o_ref, selx_ref):
    bi = pl.program_id(1)
    G = q_ref.shape[0]
    n_kt = selx_ref.shape[1]
    n_blk = ovt_ref.shape[0]
    n_cmp = kc_ref.shape[1]
    QB = Q_BLOCK
    R = MIX_HEADS * QB
    n_off = NSA_WINDOW // 128

    lo = jnp.maximum(bi - n_off, 0)
    q4 = [None] * G
    o_c = [None] * G
    carry_s = [None] * G
    carry_w = [None] * G

    def tile_qk(g, kref, kt):
        start = pl.multiple_of(kt * 128, 128)
        return _dot(kref[g, pl.ds(start, 128), :HEAD_DIM], q4[g])

    def head_cols(x, h):
        return x[:, h * QB:(h + 1) * QB]

    key1 = lax.broadcasted_iota(jnp.int32, (128, QB), 0)
    i1 = lax.broadcasted_iota(jnp.int32, (128, QB), 1)
    causal1 = key1 <= i1
    kp = jnp.maximum(bi - 1, 0)
    pen_prev = jnp.where(bi >= 1, 0.0, NEG_INF)

    def win_add(kt, valid):
        shift = jnp.where(valid, jnp.where(bi - kt < n_off, -4096, 0), 4096)
        return jnp.where(key1 > i1 + shift, 0.0, NEG_INF)

    def diag_add(extra):
        return [jnp.where(causal1, head_cols(tab_ref[0], h) + extra, NEG_INF) for h in range(MIX_HEADS)]

    def prev_add(extra):
        return [head_cols(tab_ref[1], h) + extra for h in range(MIX_HEADS)]

    def pair_update(carry, qk0, add0, vt0, qk1, add1, vt1):
        p0s, p1s, ms, ls, alphas = [], [], [], [], []
        for h in range(MIX_HEADS):
            a0 = head_cols(qk0, h) + add0[h]
            a1 = head_cols(qk1, h) + add1[h]
            m_new = jnp.maximum(_colmax(a0), _colmax(a1))
            if carry is not None:
                m_old = head_cols(carry[0], h)
                m_new = jnp.maximum(m_old, m_new)
                alpha = jnp.exp2(m_old - m_new)
                alphas.append(alpha)
            p0 = jnp.exp2(a0 - m_new)
            p1 = jnp.exp2(a1 - m_new)
            l_new = _colsum(p0) + _colsum(p1)
            if carry is not None:
                l_new = alpha * head_cols(carry[1], h) + l_new
            ms.append(m_new)
            ls.append(l_new)
            p0s.append(p0.astype(BF16))
            p1s.append(p1.astype(BF16))
        yield
        acc = _dot(vt0, jnp.concatenate(p0s, axis=1)) + _dot(vt1, jnp.concatenate(p1s, axis=1))
        if carry is not None:
            acc = jnp.concatenate(alphas, axis=1) * carry[2] + acc
        return jnp.concatenate(ms, axis=1), jnp.concatenate(ls, axis=1), acc

    def head_part(g):
        qt = (q_ref[g] * (HEAD_DIM ** -0.5 * LOG2E)).T
        q4[g] = jnp.concatenate([qt[h * HEAD_DIM:(h + 1) * HEAD_DIM] for h in range(MIX_HEADS)],
                                axis=1).astype(BF16)
        yield
        carry_w[g] = yield from pair_update(None, tile_qk(g, kw_ref, bi), diag_add(0.0), vwt_ref[g, bi],
                                            tile_qk(g, kw_ref, kp), prev_add(pen_prev), vwt_ref[g, kp])
        yield
        g0 = pl.multiple_of((n_kt - 1 - bi) * 8, 8)
        qk = _dot(kc_ref[g, :, :HEAD_DIM], q4[g])
        nrow = lax.broadcasted_iota(jnp.int32, (n_cmp, QB), 0)
        tq = bi * QB + lax.broadcasted_iota(jnp.int32, (n_cmp, QB), 1)
        vis = tq - (nrow * NSA_CMP_STRIDE + (NSA_CMP_LEN - 1)) >= 0
        live = (tq >= NSA_CMP_LEN - 1).astype(F32)
        yield
        p_cs = []
        for h in range(MIX_HEADS):
            s = jnp.where(vis, head_cols(qk, h) + bc_ref[pl.ds(g0, n_cmp), h * QB:(h + 1) * QB], NEG_INF)
            e = jnp.exp2(s - _colmax(s))
            p_cs.append(e / _colsum(e) * live)
        yield
        o_c[g] = _dot(vct_ref[g], jnp.concatenate(p_cs, axis=1).astype(BF16))
        p_sum = p_cs[0] + p_cs[1] + p_cs[2] + p_cs[3]
        p_hi, p_lo = _split2(p_sum)
        imp = _dot(ovt_ref[...], p_hi) + _dot(ovt_ref[...], p_lo)
        yield
        blk = lax.broadcasted_iota(jnp.int32, (n_blk, QB), 0)
        cur = (bi * QB + lax.broadcasted_iota(jnp.int32, (n_blk, QB), 1)) >> 6
        forced = (blk == 0) | (blk == cur) | (blk == cur - 1)
        imp = jnp.where(forced, imp + FORCE_BONUS, imp)
        imp = jnp.where(blk <= cur, imp, NEG_INF)
        blk_f = blk.astype(F32)
        sel = jnp.zeros((n_blk, QB), F32)
        for _ in range(NSA_TOP_N):
            mx = _colmax(imp)
            idx = jnp.min(jnp.where(imp == mx, blk_f, 4096.0), axis=0, keepdims=True)
            pick = blk_f == idx
            sel = jnp.where(pick, 1.0, sel)
            imp = jnp.where(pick, NEG_BIG, imp)
            yield
        selx = _dot(ext_ref[...], sel.astype(BF16))
        for kt in range(n_kt):
            selx_ref[g, kt] = (selx[kt * 128:(kt + 1) * 128, :] - 1.0) * (-NEG_INF)
        yield
        carry_s[g] = yield from pair_update(
            None, tile_qk(g, ks_ref, bi), diag_add(selx_ref[g, bi]), vst_ref[g, bi],
            tile_qk(g, ks_ref, kp), prev_add(selx_ref[g, kp] + pen_prev), vst_ref[g, kp])

    _round_robin([head_part(g) for g in range(G)])

    n_old = jnp.maximum(bi - 1, 0)

    def sel_step(g, j, carry, out):
        k0 = 2 * j
        k1 = jnp.minimum(k0 + 1, bi)
        pen1 = jnp.where(k0 + 1 < n_old, 0.0, NEG_INF)
        qk0 = tile_qk(g, ks_ref, k0)
        qk1 = tile_qk(g, ks_ref, k1)
        yield
        out[g] = yield from pair_update(carry, qk0, [selx_ref[g, k0]] * MIX_HEADS, vst_ref[g, k0],
                                        qk1, [selx_ref[g, k1] + pen1] * MIX_HEADS, vst_ref[g, k1])

    def win_step(g, j, carry, out):
        k0 = lo + 2 * j
        k1 = jnp.minimum(k0 + 1, bi)
        qk0 = tile_qk(g, kw_ref, k0)
        qk1 = tile_qk(g, kw_ref, k1)
        yield
        out[g] = yield from pair_update(carry, qk0, [win_add(k0, True)] * MIX_HEADS, vwt_ref[g, k0],
                                        qk1, [win_add(k1, k0 + 1 < n_old)] * MIX_HEADS, vwt_ref[g, k1])

    def joint(step):
        def body(j, carries):
            out = [None] * G
            _round_robin([step(g, j, carries[g], out) for g in range(G)])
            return tuple(out)
        return body

    res_s = lax.fori_loop(0, (n_old + 1) // 2, joint(sel_step), tuple(carry_s))
    res_w = lax.fori_loop(0, (n_old - lo + 1) // 2, joint(win_step), tuple(carry_w))

    def tail_part(g):
        _, l_s, acc_s = res_s[g]
        _, l_w, acc_w = res_w[g]
        o_s = acc_s * (1.0 / l_s)
        o_w = acc_w * (1.0 / l_w)
        gate = jax.nn.sigmoid(gl_ref[g].T)
        yield
        outs = []
        for h in range(MIX_HEADS):
            cs = slice(h * QB, (h + 1) * QB)
            outs.append(gate[3 * h:3 * h + 1] * o_c[g][:, cs] + gate[3 * h + 1:3 * h + 2] * o_s[:, cs]
                        + gate[3 * h + 2:3 * h + 3] * o_w[:, cs])
        o_ref[g] = jnp.concatenate(outs, axis=0).T

    _round_robin([tail_part(g) for g in range(G)])


def nsa_attention(z, prep, bias_c, tab, ovt, ext, B, S, G):
    n_qb = S // Q_BLOCK
    n_kt = S // 128
    n_cmp = S // NSA_CMP_STRIDE
    R = MIX_HEADS * Q_BLOCK
    kc, vct, ks, vst, kw, vwt = prep
    b3 = lambda b, i: (b, 0, 0)
    b4 = lambda b, i: (b, 0, 0, 0)
    return pl.pallas_call(
        _nsa_kernel,
        grid=(B // G, n_qb),
        in_specs=[
            pl.BlockSpec((G, Q_BLOCK, MIX_WIDTH), lambda b, i: (b, i, OFF_NSAQ // MIX_WIDTH)),
            pl.BlockSpec((G, Q_BLOCK, 128), lambda b, i: (b, i, OFF_NSAG // 128)),
            pl.BlockSpec((G, n_cmp, 128), b3),
            pl.BlockSpec((G, HEAD_DIM, n_cmp), b3),
            pl.BlockSpec((G, S, 128), b3),
            pl.BlockSpec((G, n_kt, HEAD_DIM, 128), b4),
            pl.BlockSpec((G, S, 128), b3),
            pl.BlockSpec((G, n_kt, HEAD_DIM, 128), b4),
            pl.BlockSpec(bias_c.shape, lambda b, i: (0, 0)),
            pl.BlockSpec((2, 128, R), lambda b, i: (0, 0, 0)),
            pl.BlockSpec(ovt.shape, lambda b, i: (0, 0)),
            pl.BlockSpec(ext.shape, lambda b, i: (0, 0)),
        ],
        out_specs=pl.BlockSpec((G, Q_BLOCK, MIX_WIDTH), lambda b, i: (b, i, 0)),
        out_shape=jax.ShapeDtypeStruct((B, S, MIX_WIDTH), F32),
        scratch_shapes=[pltpu.VMEM((G, n_kt, 128, Q_BLOCK), F32)],
        compiler_params=_cparams(("parallel", "arbitrary")),
        name="nsa_attention",
    )(z, z, kc, vct, ks, vst, kw, vwt, bias_c, tab, ovt, ext)


def _stack_heads(x, head_of_lane):
    return jnp.concatenate([jnp.where(head_of_lane == h, x, jnp.zeros_like(x))
                            for h in range(MIX_HEADS)], axis=0)


def _head_norm(y, seg_mean, eps):
    mu = _dot_x2(y, seg_mean)
    d = y - mu
    var = _dot_x2(d * d, seg_mean)
    return d * lax.rsqrt(var + eps)


def _round_robin(chains):
    while chains:
        chains = [c for c in chains if next(c, StopIteration) is not StopIteration]


def _retention_kernel(z_ref, cos_ref, sin_ref, indec_ref, qdec_ref, kdec_ref, cdec_ref, bd_ref,
                      segm_ref, ng_ref, o_ref, st_ref):
    @pl.when(pl.program_id(1) == 0)
    def _():
        st_ref[...] = jnp.zeros_like(st_ref)

    _round_robin([_retention_chunk(z_ref.at[g], cos_ref, sin_ref, indec_ref, qdec_ref, kdec_ref, cdec_ref,
                                   bd_ref, segm_ref, ng_ref, o_ref.at[g], st_ref.at[g])
                  for g in range(z_ref.shape[0])])


def _retention_chunk(z_ref, cos_ref, sin_ref, indec_ref, qdec_ref, kdec_ref, cdec_ref, bd_ref,
                     segm_ref, ng_ref, o_ref, st_ref):
    L = RET_CHUNK
    z = z_ref[...]
    q = z[:, 0:256]
    k = z[:, 256:512]
    v = z[:, 512:768]
    g = z[:, 768:1024]
    cos = cos_ref[...]
    sin = sin_ref[...]

    def rot(u):
        u1 = u[:, :128]
        u2 = u[:, 128:]
        return jnp.concatenate([u1 * cos - u2 * sin, u2 * cos + u1 * sin], axis=1)

    qr = rot(q) * (HEAD_DIM ** -0.5)
    kr = rot(k)
    lane = lax.broadcasted_iota(jnp.int32, (L, 256), 1)
    head_qk = (lane & 127) >> 5
    head_v = lane >> 6
    qb = qr.astype(BF16)
    kb = kr.astype(BF16)
    vb = v.astype(BF16)
    yield

    att = _dot_nt(_stack_heads(qb, head_qk), kb) * indec_ref[...]
    yield
    o_st = _dot(att.astype(BF16), vb)
    o = jnp.zeros((L, 256), F32)
    for h in range(MIX_HEADS):
        o = o + jnp.where(head_v == h, o_st[h * L:(h + 1) * L], 0.0)
    yield
    state = st_ref[...]
    o = o + _dot(qb, state.astype(BF16)) * qdec_ref[...]
    st_ref[...] = state * cdec_ref[...] + _dot_tn((kr * kdec_ref[...]).astype(BF16), vb) * bd_ref[...]
    yield

    mu = _dot_x2(o, segm_ref[...])
    yield
    d = o - mu
    var = _dot_x2(d * d, segm_ref[...])
    yield
    o_ref[...] = g * jax.nn.sigmoid(g) * (d * lax.rsqrt(var + RET_NORM_EPS) * ng_ref[...])


def retention(z, tabs, ng, layer, B, S, G):
    L = RET_CHUNK
    nc = S // L
    cos, sin, indec, qdec, kdec, cdec, bd, segm = tabs
    const2 = lambda b, c: (0, 0)
    return pl.pallas_call(
        _retention_kernel,
        grid=(B // G, nc),
        in_specs=[
            pl.BlockSpec((G, L, 1024), lambda b, c: (b, c, OFF_RET // 1024)),
            pl.BlockSpec((L, 128), lambda b, c: (c, 0)),
            pl.BlockSpec((L, 128), lambda b, c: (c, 0)),
            pl.BlockSpec((MIX_HEADS * L, L), const2),
            pl.BlockSpec((L, 256), const2),
            pl.BlockSpec((L, 256), const2),
            pl.BlockSpec((1, 256), const2),
            pl.BlockSpec((256, 256), const2),
            pl.BlockSpec((256, 256), const2),
            pl.BlockSpec((None, 1, 256), lambda b, c: (layer, 0, 0)),
        ],
        out_specs=pl.BlockSpec((G, L, 256), lambda b, c: (b, c, 0)),
        out_shape=jax.ShapeDtypeStruct((B, S, 256), F32),
        scratch_shapes=[pltpu.VMEM((G, 256, 256), F32)],
        compiler_params=_cparams(("parallel", "arbitrary")),
        name="retention",
    )(z, cos, sin, indec, qdec, kdec, cdec, bd, segm, ng)


def _rwkv_kernel(z_ref, mu_ref, lw_ref, vec_ref, tri_ref, bdl_ref, wide_ref,
                 segs_ref, segm_ref, o_ref, st_ref, prev_ref):
    @pl.when(pl.program_id(1) == 0)
    def _():
        st_ref[...] = jnp.zeros_like(st_ref)
        prev_ref[...] = jnp.zeros_like(prev_ref)

    _round_robin([_rwkv_chunk(z_ref.at[g], mu_ref, lw_ref, vec_ref, tri_ref, bdl_ref,
                              wide_ref, segs_ref, segm_ref, o_ref.at[g], st_ref.at[g], prev_ref.at[g])
                  for g in range(z_ref.shape[0])])


def _rwkv_chunk(z_ref, mu_ref, lw_ref, vec_ref, tri_ref, bdl_ref, wide_ref,
                segs_ref, segm_ref, o_ref, st_ref, prev_ref):
    C = RWKV_CHUNK
    z = z_ref[...]
    rows = lax.broadcasted_iota(jnp.int32, (C, 1024), 0)
    zs = jnp.where(rows == 0, prev_ref[0:1, :], pltpu.roll(z, 1, axis=0))
    prev_ref[...] = jnp.broadcast_to(z[C - 1:C, :], prev_ref.shape)
    zf = z + (zs - z) * mu_ref[...]
    r = zf[:, 0:256]
    k = zf[:, 256:512]
    v = zf[:, 512:768]
    lora = zf[:, 768:896]

    vec = vec_ref[...]
    w0, a0, k_k, k_a, r_k, ln_g, ln_b = (vec[i:i + 1] for i in range(7))

    lane = lax.broadcasted_iota(jnp.int32, lora.shape, 1)
    lora_in = jnp.where(lane < RWKV_DECAY_LORA, jnp.tanh(lora),
                        jnp.where(lane < RWKV_DECAY_LORA + RWKV_AAA_LORA, lora, jax.nn.sigmoid(lora)))
    hi, lo = _split2(lora_in)
    both = _dot(jnp.concatenate([hi, lo], axis=0), lw_ref[0])
    lora_out = both[:C] + both[C:] + _dot(hi, lw_ref[1])
    wpre = w0 + lora_out[:, 0:256]
    y = -wpre
    softplus = jnp.maximum(y, 0.0) + jnp.log(1.0 + jnp.exp(-jnp.abs(y)))
    w_log = -softplus - 0.5
    ld = -jnp.exp(w_log)
    a = jax.nn.sigmoid(a0 + lora_out[:, 256:512])
    gate = lora_out[:, 512:768]
    kk = k * k_k
    k2 = k * (1.0 + (a - 1.0) * k_a)
    kk_sq, bonus = _dot_x2(kk * kk, segs_ref[...], r * k2 * r_k)
    kk = kk / jnp.maximum(jnp.sqrt(kk_sq), 1e-12)
    yield

    tri = tri_ref[...]
    l1 = ld.astype(BF16)
    r1 = ld - l1.astype(F32)
    l2 = r1.astype(BF16)
    l3 = (r1 - l2.astype(F32)).astype(BF16)
    cs = _dot(tri, l1) + _dot(tri, l2) + _dot(tri, l3)
    cs_end = cs[C - 1:C, :]
    yield
    e_neg = jnp.exp(-cs)
    e_end = jnp.exp(cs_end - cs)
    kka = kk * a
    a_t = -kk * jnp.exp(cs - ld)
    r_t = r * jnp.exp(cs)
    b_t = kka * e_neg
    k_t = k2 * e_neg
    b_g = kka * e_end
    k_g = k2 * e_end

    lane = lax.broadcasted_iota(jnp.int32, (C, 256), 1)
    head = lane >> 6
    stack = lambda x: _stack_heads(x.astype(BF16), head)
    a_st, b_st, k_st, v_st = stack(a_t), stack(b_t), stack(k_t), stack(v)

    n_pow = _dot_nt(a_st, b_st) * bdl_ref[...]
    ri = lax.broadcasted_iota(jnp.int32, (4 * C, 4 * C), 0)
    ci = lax.broadcasted_iota(jnp.int32, (4 * C, 4 * C), 1)
    t_inv = jnp.where(ri == ci, 1.0, 0.0) + n_pow
    yield

    ar = jnp.concatenate([a_t, r_t], axis=0).astype(BF16)
    bk = jnp.concatenate([b_st, k_st], axis=0)
    wide = _dot_nt(ar, bk)
    strict = wide_ref[0]
    incl = wide_ref[1]
    a_ak = (wide[0:C, 4 * C:] * strict).astype(BF16)
    a_rb = (wide[C:, 0:4 * C] * incl).astype(BF16)
    a_rk = (wide[C:, 4 * C:] * incl).astype(BF16)
    yield

    sq = 2
    while sq < C:
        nb = n_pow.astype(BF16)
        n_pow = _dot(nb, nb)
        yield
        t_inv = t_inv + _dot(t_inv.astype(BF16), n_pow.astype(BF16))
        sq *= 2
    yield

    state = st_ref[...]
    sb = state.astype(BF16)
    x = _dot_nt(a_t.astype(BF16), sb) + _dot(a_ak, v_st)
    yield
    u_st = _dot(t_inv.astype(BF16), stack(x)).astype(BF16)
    yield
    yv = _dot_nt(r_t.astype(BF16), sb) + _dot(a_rb, u_st) + _dot(a_rk, v_st)
    uv = jnp.concatenate([u_st, v_st], axis=0)
    bkg = jnp.concatenate([stack(b_g), stack(k_g)], axis=0)
    st_ref[...] = state * jnp.exp(cs_end) + _dot_tn(uv, bkg)
    yield

    yn = _head_norm(yv, segm_ref[...], RWKV_GN_EPS) * ln_g + ln_b
    yn = yn + bonus * v
    o_ref[...] = yn * gate


def rwkv7(z, mu, lora_w, vec, consts, layer, B, S, G):
    C = RWKV_CHUNK
    nc = S // C
    tri, bdl, wide, segs, segm = consts
    const2 = lambda b, c: (0, 0)
    return pl.pallas_call(
        _rwkv_kernel,
        grid=(B // G, nc),
        in_specs=[
            pl.BlockSpec((G, C, 1024), lambda b, c: (b, c, OFF_RWKV // 1024)),
            pl.BlockSpec((None, 1, 1024), lambda b, c: (layer, 0, 0)),
            pl.BlockSpec((None, 2, 128, 768), lambda b, c: (layer, 0, 0, 0)),
            pl.BlockSpec((None, 8, 256), lambda b, c: (layer, 0, 0)),
            pl.BlockSpec((C, C), const2),
            pl.BlockSpec((4 * C, 4 * C), const2),
            pl.BlockSpec((2, C, 4 * C), lambda b, c: (0, 0, 0)),
            pl.BlockSpec((256, 256), const2),
            pl.BlockSpec((256, 256), const2),
        ],
        out_specs=pl.BlockSpec((G, C, 256), lambda b, c: (b, c, 0)),
        out_shape=jax.ShapeDtypeStruct((B, S, 256), F32),
        scratch_shapes=[pltpu.VMEM((G, 256, 256), F32), pltpu.VMEM((G, 8, 1024), F32)],
        compiler_params=_cparams(("parallel", "arbitrary")),
        name="rwkv7",
    )(z, mu, lora_w, vec, tri, bdl, wide, segs, segm)


def _merge_kernel(x_ref, zc_ref, on_ref, or_ref, ow_ref, cw_ref, gpre_ref, wg_ref, wb_ref, wo_ref, g_ref,
                  o_ref, carry_ref):
    tm = x_ref.shape[0]
    x = x_ref[...]
    h = _rms(x, gpre_ref[...]).astype(BF16)

    @pl.when(pl.program_id(1) == 0)
    def _():
        carry_ref[...] = jnp.zeros_like(carry_ref)

    zc = zc_ref[...]
    b_g = zc[:, 0:256]
    u = zc[:, 256:512] * zc[:, 512:768]
    prev = carry_ref[...]
    rows = lax.broadcasted_iota(jnp.int32, (tm, 256), 0)
    u1 = jnp.where(rows == 0, prev[7:8], pltpu.roll(u, 1, axis=0))
    u2 = jnp.where(rows == 0, prev[6:7], jnp.where(rows == 1, prev[7:8], pltpu.roll(u, 2, axis=0)))
    carry_ref[...] = u[tm - 8:tm]
    cw = cw_ref[...]
    o_conv = b_g * (cw[0:1] * u2 + cw[1:2] * u1 + cw[2:3] * u)

    branches = (on_ref[...], or_ref[...], ow_ref[...], o_conv)
    merged = jnp.zeros((tm, D_MODEL), F32)
    for m in range(N_BRANCHES):
        gm = jax.nn.sigmoid(_dot(h, wg_ref[:, m * D_MODEL:(m + 1) * D_MODEL]))
        merged = merged + gm * _dot(branches[m].astype(BF16), wb_ref[m])
    y = _dot(merged.astype(BF16), wo_ref[...])
    o_ref[...] = x + _rms(y, g_ref[...])


def merge_mix(x, z, o_nsa, o_ret, o_rwkv, conv_w, g_pre, w_gate, w_branch, w_out, g_post, layer, B, S, tm):
    nt = S // tm
    rowmap = lambda b, i: (b * nt + i, 0)
    lmap = lambda b, i: (layer, 0, 0)
    return pl.pallas_call(
        _merge_kernel,
        grid=(B, nt),
        in_specs=[
            pl.BlockSpec((tm, D_MODEL), rowmap),
            pl.BlockSpec((tm, 768), lambda b, i: (b * nt + i, OFF_CONV // 768)),
            pl.BlockSpec((tm, 256), rowmap),
            pl.BlockSpec((tm, 256), rowmap),
            pl.BlockSpec((tm, 256), rowmap),
            pl.BlockSpec((None, 8, 256), lmap),
            pl.BlockSpec((None, 1, D_MODEL), lmap),
            pl.BlockSpec((None, D_MODEL, N_BRANCHES * D_MODEL), lmap, pipeline_mode=pl.Buffered(1)),
            pl.BlockSpec((None, N_BRANCHES, MIX_WIDTH, D_MODEL), lambda b, i: (layer, 0, 0, 0),
                         pipeline_mode=pl.Buffered(1)),
            pl.BlockSpec((None, D_MODEL, D_MODEL), lmap, pipeline_mode=pl.Buffered(1)),
            pl.BlockSpec((None, 1, D_MODEL), lmap),
        ],
        out_specs=pl.BlockSpec((tm, D_MODEL), rowmap),
        out_shape=jax.ShapeDtypeStruct((B * S, D_MODEL), F32),
        scratch_shapes=[pltpu.VMEM((8, 256), F32)],
        compiler_params=_cparams(("parallel", "arbitrary")),
        name="merge_mix",
    )(x, z, o_nsa, o_ret, o_rwkv, conv_w, g_pre, w_gate, w_branch, w_out, g_post)


def _xattn_kernel(x_ref, kv_ref, gpre_ref, wq_ref, wo_ref, gpost_ref, o_ref):
    n_sub = 1
    rows = x_ref.shape[0] // n_sub

    def part(r0):
        rs = pl.ds(r0, rows)
        x = x_ref[rs, :]
        h = _rms(x, gpre_ref[...]).astype(BF16)
        yield
        q = (_dot(h, wq_ref[...]) * (XA_HEAD_DIM ** -0.5)).astype(BF16)
        yield
        outs = []
        for hd in range(XA_HEADS):
            cs = slice(hd * XA_HEAD_DIM, (hd + 1) * XA_HEAD_DIM)
            kh = kv_ref[:, hd * XA_HEAD_DIM:(hd + 1) * XA_HEAD_DIM]
            vh = kv_ref[:, D_MODEL + hd * XA_HEAD_DIM:D_MODEL + (hd + 1) * XA_HEAD_DIM]
            s = _dot_nt(q[:, cs], kh)
            yield
            e = jnp.exp(s - _rowmax(s))
            p = e / _rowsum(e)
            outs.append(_dot(p.astype(BF16), vh).astype(BF16))
            yield
        o = jnp.concatenate(outs, axis=1)
        y = _dot(o, wo_ref[...])
        yield
        o_ref[rs, :] = x + _rms(y, gpost_ref[...])

    _round_robin([part(i * rows) for i in range(n_sub)])


def cross_attention(x, kv, g_pre, wq, wo, g_post, layer, B, S, M, tm):
    nt = S // tm
    rowmap = lambda b, i: (b * nt + i, 0)
    lmap = lambda b, i: (layer, 0, 0)
    return pl.pallas_call(
        _xattn_kernel,
        grid=(B, nt),
        in_specs=[
            pl.BlockSpec((tm, D_MODEL), rowmap),
            pl.BlockSpec((M, 2 * D_MODEL), lambda b, i: (b, 0)),
            pl.BlockSpec((None, 1, D_MODEL), lmap),
            pl.BlockSpec((None, D_MODEL, D_MODEL), lmap),
            pl.BlockSpec((None, D_MODEL, D_MODEL), lmap),
            pl.BlockSpec((None, 1, D_MODEL), lmap),
        ],
        out_specs=pl.BlockSpec((tm, D_MODEL), rowmap),
        out_shape=jax.ShapeDtypeStruct((B * S, D_MODEL), F32),
        compiler_params=_cparams(("parallel", "arbitrary")),
        name="cross_attention",
    )(x, kv, g_pre, wq, wo, g_post)


def _mlp_kernel(x_ref, gpre_ref, w1_ref, w2_ref, gpost_ref, o_ref):
    x = x_ref[...]
    h = _rms(x, gpre_ref[...]).astype(BF16)
    tf = 1024
    acc = jnp.zeros(x.shape, F32)
    for f in range(D_FF // tf):
        a = jnp.maximum(_dot(h, w1_ref[:, f * tf:(f + 1) * tf]), 0.0)
        acc = acc + _dot((a * a).astype(BF16), w2_ref[f * tf:(f + 1) * tf, :])
    o_ref[...] = x + _rms(acc, gpost_ref[...])


def mlp(x, g_pre, w1, w2, g_post, layer, tm):
    M = x.shape[0]
    lmap = lambda i: (layer, 0, 0)
    return pl.pallas_call(
        _mlp_kernel,
        grid=(M // tm,),
        in_specs=[
            pl.BlockSpec((tm, D_MODEL), lambda i: (i, 0)),
            pl.BlockSpec((None, 1, D_MODEL), lmap),
            pl.BlockSpec((None, D_MODEL, D_FF), lmap, pipeline_mode=pl.Buffered(1)),
            pl.BlockSpec((None, D_FF, D_MODEL), lmap, pipeline_mode=pl.Buffered(1)),
            pl.BlockSpec((None, 1, D_MODEL), lmap),
        ],
        out_specs=pl.BlockSpec((tm, D_MODEL), lambda i: (i, 0)),
        out_shape=jax.ShapeDtypeStruct((M, D_MODEL), F32),
        compiler_params=_cparams(("parallel",)),
        name="mlp",
    )(x, g_pre, w1, w2, g_post)


def _t5_bucket(dist):
    n = jnp.maximum(dist, 0)
    max_exact = N_BUCKETS // 2
    nf = jnp.maximum(n, 1).astype(F32)
    large = max_exact + (jnp.log(nf / max_exact) / math.log(MAX_DISTANCE / max_exact)
                         * (N_BUCKETS - max_exact)).astype(jnp.int32)
    large = jnp.minimum(large, N_BUCKETS - 1)
    return jnp.where(n < max_exact, n, large)


def _nsa_tables(rel_bias, S):
    n_qb = S // Q_BLOCK
    n_cmp = S // NSA_CMP_STRIDE
    n_blk = S // NSA_SLC_LEN
    bias_f = rel_bias.astype(F32)
    i = jnp.arange(Q_BLOCK)
    per_qb = Q_BLOCK // NSA_CMP_STRIDE
    g = jnp.arange(n_cmp + per_qb * (n_qb - 1)) - per_qb * (n_qb - 1)
    d_c = i[None, :] - (g[:, None] * NSA_CMP_STRIDE + NSA_CMP_LEN - 1)
    def lookup(dist):
        hit = _t5_bucket(dist)[..., None, None] == jnp.arange(N_BUCKETS)[:, None]
        return jnp.sum(jnp.where(hit, bias_f, 0.0), axis=-2)

    bias_c = lookup(d_c).transpose(0, 2, 1).reshape(g.shape[0], MIX_HEADS * Q_BLOCK)
    d0 = i[:, None] - i[None, :]
    tab = jnp.stack([lookup(d0), lookup(d0 + Q_BLOCK)]) - bias_f[N_BUCKETS - 1]
    tab = tab.transpose(0, 2, 3, 1).reshape(2, Q_BLOCK, MIX_HEADS * Q_BLOCK)
    cs = np.arange(n_cmp)[None, :] * NSA_CMP_STRIDE
    ss = np.arange(n_blk)[:, None] * NSA_SLC_LEN
    ovt = np.clip(np.minimum(cs + NSA_CMP_LEN, ss + NSA_SLC_LEN) - np.maximum(cs, ss), 0, None)
    ovt = ovt.astype(np.float32) / NSA_CMP_LEN
    ovt[:, (S - NSA_CMP_LEN) // NSA_CMP_STRIDE + 1:] = 0.0
    ext = ((np.arange(S)[:, None] // NSA_SLC_LEN) == np.arange(n_blk)[None, :]).astype(np.float32)
    return bias_c * LOG2E, tab * LOG2E, jnp.asarray(ovt, BF16), jnp.asarray(ext, BF16)


def _ret_tables(S):
    L = RET_CHUNK
    H = MIX_HEADS
    half = HEAD_DIM // 2
    pos = jnp.arange(S)
    inv_freq = ROPE_BASE ** (-jnp.arange(half, dtype=F32) / half)
    ang = pos.astype(F32)[:, None] * inv_freq[None, :]
    cos = jnp.tile(jnp.cos(ang), (1, H))
    sin = jnp.tile(jnp.sin(ang), (1, H))
    lg = jnp.log(1.0 - 2.0 ** (-5.0 - jnp.arange(H, dtype=F32)))
    n = jnp.arange(L, dtype=F32)
    diff = n[:, None] - n[None, :]
    inner = jnp.where(diff >= 0, jnp.exp(jnp.maximum(diff, 0.0)[None] * lg[:, None, None]), 0.0)
    indec = inner.reshape(H * L, L)
    q_decay = jnp.exp((n + 1.0)[None, :] * lg[:, None])
    k_decay = jnp.exp((L - 1.0 - n)[None, :] * lg[:, None])
    chunk_decay = jnp.exp(L * lg)
    lane = np.arange(256)
    head_v = lane // HEAD_DIM
    head_qk = (lane % 128) // half
    qdec = q_decay.T[:, head_v]
    kdec = k_decay.T[:, head_qk]
    cdec = chunk_decay[head_v][None, :]
    bd = jnp.asarray((head_qk[:, None] == head_v[None, :]).astype(np.float32))
    segm = jnp.asarray((head_v[:, None] == head_v[None, :]).astype(np.float32) / HEAD_DIM, BF16)
    return cos, sin, indec, qdec, kdec, cdec, bd, segm


def _rwkv_consts():
    C = RWKV_CHUNK
    t = np.arange(C)
    tri = (t[:, None] >= t[None, :]).astype(np.float32)
    r = np.arange(4 * C)
    bdl = ((r[:, None] // C == r[None, :] // C) & (r[:, None] % C > r[None, :] % C)).astype(np.float32)
    strict = (t[:, None] > (r[None, :] % C)).astype(np.float32)
    incl = (t[:, None] >= (r[None, :] % C)).astype(np.float32)
    lane = np.arange(256) // HEAD_DIM
    seg = (lane[:, None] == lane[None, :]).astype(np.float32)
    return (jnp.asarray(tri, BF16), jnp.asarray(bdl), jnp.asarray(np.stack([strict, incl])),
            jnp.asarray(seg, BF16), jnp.asarray(seg / HEAD_DIM, BF16))


def _hi_lo(w):
    hi = w.astype(BF16)
    lo = (w - hi.astype(F32)).astype(BF16)
    return jnp.stack([hi, lo], axis=1)


def _pad_rows(w, top, total):
    return jnp.pad(w, ((0, 0), (top, total - top - w.shape[1]), (0, 0)))


def _layout_w_in(w_in):
    L = w_in.shape[0]
    o = 0
    nsa_q = w_in[:, :, o:o + 256]; o += 256
    nsa_kv = w_in[:, :, o:o + 384]; o += 384
    nsa_g = w_in[:, :, o:o + 12]; o += 12
    ret = w_in[:, :, o:o + 1024]; o += 1024
    rwkv = w_in[:, :, o:o + RWKV_COLS]; o += RWKV_COLS
    conv = w_in[:, :, o:o + 768]; o += 768
    gate = w_in[:, :, o:o + 4096]

    def rot_perm(w):
        return w.reshape(L, D_MODEL, MIX_HEADS, 2, HEAD_DIM // 2).transpose(0, 1, 3, 2, 4).reshape(L, D_MODEL, 256)

    ret = jnp.concatenate([rot_perm(ret[:, :, 0:256]), rot_perm(ret[:, :, 256:512]), ret[:, :, 512:]], axis=2)
    zeros = lambda n: jnp.zeros((L, D_MODEL, n), w_in.dtype)
    out = jnp.concatenate([conv, nsa_q, ret, rwkv, zeros(1024 - RWKV_COLS), nsa_kv, nsa_g,
                           zeros(128 - 12)], axis=2)
    assert out.shape[2] == Z_COLS
    return out.astype(BF16), gate.astype(BF16)


def _layout_cmp(cmp_w, cmp_pe):
    L = cmp_w.shape[0]
    wk = cmp_w[:, 0]
    wv = cmp_w[:, 1]
    zero = jnp.zeros_like(wk)
    blk = jnp.concatenate([jnp.concatenate([wk, zero], axis=3), jnp.concatenate([zero, wv], axis=3)], axis=2)
    pe2 = jnp.concatenate([cmp_pe, cmp_pe], axis=2)[:, :, None, :]
    s = NSA_CMP_STRIDE
    return blk[:, :s].astype(BF16), blk[:, s:].astype(BF16), pe2[:, :s], pe2[:, s:]


def kernel(x, mem, ln_mix_pre, w_in, nsa_cmp_w, nsa_cmp_pe, rel_bias, ret_norm_g, rwkv_mu, rwkv_w0, rwkv_w2, rwkv_a0, rwkv_a2, rwkv_g2, rwkv_k_k, rwkv_k_a, rwkv_r_k, rwkv_ln_g, rwkv_ln_b, conv_w, w_branch, w_mix_out, ln_mix_post, ln_xa_pre, ln_mem, xa_wq, xa_wkv, xa_wo, ln_xa_post, ln_mlp_pre, mlp_w1, mlp_w2, ln_mlp_post):
    B, S, D = x.shape
    M = mem.shape[1]
    depth = w_in.shape[0]
    row = lambda g: g[:, None, :]

    w_in_b, w_gate_b = _layout_w_in(w_in)
    cmp_lo, cmp_hi, pe_lo, pe_hi = _layout_cmp(nsa_cmp_w, nsa_cmp_pe)
    nsa_tabs = _nsa_tables(rel_bias, S)
    ret_tabs = _ret_tables(S)
    rwkv_consts = _rwkv_consts()
    mu = jnp.pad(rwkv_mu, ((0, 0), (0, 1024 - RWKV_COLS)))[:, None, :]
    lora_w = _hi_lo(jnp.concatenate([_pad_rows(rwkv_w2, 0, 128), _pad_rows(rwkv_a2, RWKV_DECAY_LORA, 128),
                                     _pad_rows(rwkv_g2, RWKV_DECAY_LORA + RWKV_AAA_LORA, 128)], axis=2))
    vec = jnp.stack([rwkv_w0, rwkv_a0, rwkv_k_k, rwkv_k_a, rwkv_r_k, rwkv_ln_g, rwkv_ln_b,
                     jnp.zeros_like(rwkv_w0)], axis=1)
    conv_p = jnp.pad(conv_w, ((0, 0), (0, 8 - CONV_WIDTH), (0, 0)))
    w_branch_b = w_branch.astype(BF16)
    w_mix_out_b = w_mix_out.astype(BF16)
    xa_wq_b = xa_wq.astype(BF16)
    xa_wkv_b = xa_wkv.astype(BF16)
    xa_wo_b = xa_wo.astype(BF16)
    mlp_w1_b = mlp_w1.astype(BF16)
    mlp_w2_b = mlp_w2.astype(BF16)

    xf = x.reshape(B * S, D)
    memf = mem.reshape(B * M, D)
    for l in range(depth):
        z = norm_matmul(xf, row(ln_mix_pre), w_in_b, l, tm=512, tn=Z_COLS, out_dtype=F32)
        prep = nsa_compress(z, cmp_lo, cmp_hi, pe_lo, pe_hi, l, B, S)
        z3 = z.reshape(B, S, Z_COLS)
        o_nsa = nsa_attention(z3, prep, *nsa_tabs, B, S, G=4).reshape(B * S, MIX_WIDTH)
        o_ret = retention(z3, ret_tabs, row(ret_norm_g), l, B, S, G=4).reshape(B * S, MIX_WIDTH)
        o_rwkv = rwkv7(z3, mu, lora_w, vec, rwkv_consts, l, B, S, G=4).reshape(B * S, MIX_WIDTH)
        xf = merge_mix(xf, z, o_nsa, o_ret, o_rwkv, conv_p, row(ln_mix_pre), w_gate_b, w_branch_b,
                       w_mix_out_b, row(ln_mix_post), l, B, S, tm=512)
        kvm = norm_matmul(memf, row(ln_mem), xa_wkv_b, l, tm=2 * M, tn=2 * D_MODEL, out_dtype=BF16)
        xf = cross_attention(xf, kvm, row(ln_xa_pre), xa_wq_b, xa_wo_b, row(ln_xa_post), l, B, S, M, tm=1024)
        xf = mlp(xf, row(ln_mlp_pre), mlp_w1_b, mlp_w2_b, row(ln_mlp_post), l, tm=512)
    return xf.reshape(B, S, D)
```

```python
import functools
import math

import numpy as np
import jax
import jax.numpy as jnp
from jax import lax
from jax.experimental import pallas as pl
from jax.experimental.pallas import tpu as pltpu

F32 = jnp.float32
BF16 = jnp.bfloat16

D_MODEL = 1024
N_BRANCHES = 4
MIX_WIDTH = 256
HEAD_DIM = 64
MIX_HEADS = 4

NSA_CMP_LEN = 32
NSA_CMP_STRIDE = 16
NSA_SLC_LEN = 64
NSA_TOP_N = 8
NSA_WINDOW = 512
Q_BLOCK = 128
FORCE_BONUS = 1e4
N_BUCKETS = 32
MAX_DISTANCE = 128

RET_CHUNK = 128
ROPE_BASE = 10000.0
RET_NORM_EPS = 1e-5

RWKV_DECAY_LORA = 32
RWKV_AAA_LORA = 32
RWKV_GATE_LORA = 64
RWKV_GN_EPS = 64e-5
RWKV_COLS = 3 * MIX_WIDTH + RWKV_DECAY_LORA + RWKV_AAA_LORA + RWKV_GATE_LORA
RWKV_CHUNK = 64

CONV_WIDTH = 3
XA_HEADS = 4
XA_HEAD_DIM = D_MODEL // XA_HEADS
D_FF = 4 * D_MODEL

RMS_EPS = 1e-6
LOG2E = math.log2(math.e)
NEG_INF = -1e30
NEG_BIG = -3e38

OFF_CONV = 0
OFF_NSAQ = 768
OFF_RET = 1024
OFF_RWKV = 2048
OFF_NSAKV = 3072
OFF_NSAG = 3456
Z_COLS = 3584

VMEM_LIMIT = 56 * 1024 * 1024


def _cparams(sem):
    return pltpu.CompilerParams(dimension_semantics=sem, vmem_limit_bytes=VMEM_LIMIT)


def _dot(a, b):
    return jnp.dot(a, b, preferred_element_type=F32)


def _dot_nt(a, b):
    return lax.dot_general(a, b, (((1,), (1,)), ((), ())), preferred_element_type=F32)


def _dot_tn(a, b):
    return lax.dot_general(a, b, (((0,), (0,)), ((), ())), preferred_element_type=F32)


def _split2(x):
    hi = x.astype(BF16)
    lo = (x - hi.astype(F32)).astype(BF16)
    return hi, lo


def _dot_x2(x, w_bf16, *more):
    xs = (x,) + more
    rows = x.shape[0]
    parts = [p for xi in xs for p in _split2(xi)]
    out = _dot(jnp.concatenate(parts, axis=0), w_bf16)
    res = [out[2 * i * rows:(2 * i + 1) * rows] + out[(2 * i + 1) * rows:(2 * i + 2) * rows]
           for i in range(len(xs))]
    return res[0] if not more else res


def _rms(x, g):
    ms = jnp.mean(x * x, axis=-1, keepdims=True)
    return x * lax.rsqrt(ms + RMS_EPS) * g


def _norm_matmul_kernel(x_ref, g_ref, w_ref, o_ref, h_ref):
    @pl.when(pl.program_id(1) == 0)
    def _():
        h_ref[...] = _rms(x_ref[...], g_ref[...]).astype(BF16)

    o_ref[...] = _dot(h_ref[...], w_ref[...]).astype(o_ref.dtype)


def norm_matmul(x, g, w, layer, tm, tn, out_dtype):
    M, D = x.shape
    N = w.shape[2]
    w_mode = dict(pipeline_mode=pl.Buffered(1)) if tn == N else {}
    return pl.pallas_call(
        _norm_matmul_kernel,
        grid=(M // tm, N // tn),
        in_specs=[
            pl.BlockSpec((tm, D), lambda i, j: (i, 0)),
            pl.BlockSpec((None, 1, D), lambda i, j: (layer, 0, 0)),
            pl.BlockSpec((None, D, tn), lambda i, j: (layer, 0, j), **w_mode),
        ],
        out_specs=pl.BlockSpec((tm, tn), lambda i, j: (i, j)),
        out_shape=jax.ShapeDtypeStruct((M, N), out_dtype),
        scratch_shapes=[pltpu.VMEM((tm, D), BF16)],
        compiler_params=_cparams(("parallel", "arbitrary")),
        name="norm_matmul",
    )(x, g, w)


def _nsa_compress_kernel(kvc_ref, kvs_ref, kvw_ref, wlo_ref, whi_ref, pelo_ref, pehi_ref,
                         kc_ref, vct_ref, ks_ref, vst_ref, kw_ref, vwt_ref):
    nblk = kc_ref.shape[0]
    n_kt = vst_ref.shape[0]
    ylo = jnp.zeros((nblk, 128), F32)
    yhi = jnp.zeros((nblk, 128), F32)
    for r in range(NSA_CMP_STRIDE):
        xr = kvc_ref[pl.ds(r, nblk, stride=NSA_CMP_STRIDE), :]
        ylo = ylo + _dot((xr + pelo_ref[r]).astype(BF16), wlo_ref[r])
        yhi = yhi + _dot((xr + pehi_ref[r]).astype(BF16), whi_ref[r])
    y = ylo + pltpu.roll(yhi, nblk - 1, axis=0)
    kc_ref[...] = y.astype(BF16)
    vct_ref[...] = y.T[HEAD_DIM:, :].astype(BF16)
    ks_ref[...] = kvs_ref[...].astype(BF16)
    kw_ref[...] = kvw_ref[...].astype(BF16)
    for kt in range(n_kt):
        rs = slice(kt * 128, (kt + 1) * 128)
        vst_ref[kt] = kvs_ref[rs, :].T[HEAD_DIM:, :].astype(BF16)
        vwt_ref[kt] = kvw_ref[rs, :].T[HEAD_DIM:, :].astype(BF16)


def nsa_compress(z, wlo, whi, pelo, pehi, layer, B, S):
    nblk = S // NSA_CMP_STRIDE
    n_kt = S // 128
    c0 = OFF_NSAKV // 128
    wmap = lambda b: (layer, 0, 0, 0)
    b3 = lambda b: (b, 0, 0)
    b4 = lambda b: (b, 0, 0, 0)
    return pl.pallas_call(
        _nsa_compress_kernel,
        grid=(B,),
        in_specs=[
            pl.BlockSpec((S, 128), lambda b: (b, c0)),
            pl.BlockSpec((S, 128), lambda b: (b, c0 + 1)),
            pl.BlockSpec((S, 128), lambda b: (b, c0 + 2)),
            pl.BlockSpec((None, NSA_CMP_STRIDE, 128, 128), wmap),
            pl.BlockSpec((None, NSA_CMP_STRIDE, 128, 128), wmap),
            pl.BlockSpec((None, NSA_CMP_STRIDE, 1, 128), wmap),
            pl.BlockSpec((None, NSA_CMP_STRIDE, 1, 128), wmap),
        ],
        out_specs=[
            pl.BlockSpec((None, nblk, 128), b3),
            pl.BlockSpec((None, HEAD_DIM, nblk), b3),
            pl.BlockSpec((None, S, 128), b3),
            pl.BlockSpec((None, n_kt, HEAD_DIM, 128), b4),
            pl.BlockSpec((None, S, 128), b3),
            pl.BlockSpec((None, n_kt, HEAD_DIM, 128), b4),
        ],
        out_shape=[
            jax.ShapeDtypeStruct((B, nblk, 128), BF16),
            jax.ShapeDtypeStruct((B, HEAD_DIM, nblk), BF16),
            jax.ShapeDtypeStruct((B, S, 128), BF16),
            jax.ShapeDtypeStruct((B, n_kt, HEAD_DIM, 128), BF16),
            jax.ShapeDtypeStruct((B, S, 128), BF16),
            jax.ShapeDtypeStruct((B, n_kt, HEAD_DIM, 128), BF16),
        ],
        compiler_params=_cparams(("parallel",)),
        name="nsa_compress",
    )(z, z, z, wlo, whi, pelo, pehi)


def _rowmax(s):
    return jnp.max(s, axis=-1, keepdims=True)


def _rowsum(s):
    return jnp.sum(s, axis=-1, keepdims=True)


def _colmax(s):
    return jnp.max(s, axis=0, keepdims=True)


def _colsum(s):
    return jnp.sum(s, axis=0, keepdims=True)


def _nsa_kernel(q_ref, gl_ref, kc_ref, vct_ref, ks_ref, vst_ref, kw_ref, vwt_ref, bc_ref, tab_ref,
                ovt_ref, ext_ref, o_ref, selx_ref):
    bi = pl.program_id(1)
    G = q_ref.shape[0]
    n_kt = selx_ref.shape[1]
    n_blk = ovt_ref.shape[0]
    n_cmp = kc_ref.shape[1]
    QB = Q_BLOCK
    R = MIX_HEADS * QB
    n_off = NSA_WINDOW // 128

    lo = jnp.maximum(bi - n_off, 0)
    q4 = [None] * G
    o_c = [None] * G
    carry_s = [None] * G
    carry_w = [None] * G

    def tile_qk(g, kref, kt):
        start = pl.multiple_of(kt * 128, 128)
        return _dot(kref[g, pl.ds(start, 128), :HEAD_DIM], q4[g])

    def head_cols(x, h):
        return x[:, h * QB:(h + 1) * QB]

    key1 = lax.broadcasted_iota(jnp.int32, (128, QB), 0)
    i1 = lax.broadcasted_iota(jnp.int32, (128, QB), 1)
    causal1 = key1 <= i1
    kp = jnp.maximum(bi - 1, 0)
    pen_prev = jnp.where(bi >= 1, 0.0, NEG_INF)

    def win_add(kt, valid):
        shift = jnp.where(valid, jnp.where(bi - kt < n_off, -4096, 0), 4096)
        return jnp.where(key1 > i1 + shift, 0.0, NEG_INF)

    def diag_add(extra):
        return [jnp.where(causal1, head_cols(tab_ref[0], h) + extra, NEG_INF) for h in range(MIX_HEADS)]

    def prev_add(extra):
        return [head_cols(tab_ref[1], h) + extra for h in range(MIX_HEADS)]

    def pair_update(carry, qk0, add0, vt0, qk1, add1, vt1):
        p0s, p1s, ms, ls, alphas = [], [], [], [], []
        for h in range(MIX_HEADS):
            a0 = head_cols(qk0, h) + add0[h]
            a1 = head_cols(qk1, h) + add1[h]
            m_new = jnp.maximum(_colmax(a0), _colmax(a1))
            if carry is not None:
                m_old = head_cols(carry[0], h)
                m_new = jnp.maximum(m_old, m_new)
                alpha = jnp.exp2(m_old - m_new)
                alphas.append(alpha)
            p0 = jnp.exp2(a0 - m_new)
            p1 = jnp.exp2(a1 - m_new)
            l_new = _colsum(p0) + _colsum(p1)
            if carry is not None:
                l_new = alpha * head_cols(carry[1], h) + l_new
            ms.append(m_new)
            ls.append(l_new)
            p0s.append(p0.astype(BF16))
            p1s.append(p1.astype(BF16))
        yield
        acc = _dot(vt0, jnp.concatenate(p0s, axis=1)) + _dot(vt1, jnp.concatenate(p1s, axis=1))
        if carry is not None:
            acc = jnp.concatenate(alphas, axis=1) * carry[2] + acc
        return jnp.concatenate(ms, axis=1), jnp.concatenate(ls, axis=1), acc

    def head_part(g):
        qt = (q_ref[g] * (HEAD_DIM ** -0.5 * LOG2E)).T
        q4[g] = jnp.concatenate([qt[h * HEAD_DIM:(h + 1) * HEAD_DIM] for h in range(MIX_HEADS)],
                                axis=1).astype(BF16)
        yield
        carry_w[g] = yield from pair_update(None, tile_qk(g, kw_ref, bi), diag_add(0.0), vwt_ref[g, bi],
                                            tile_qk(g, kw_ref, kp), prev_add(pen_prev), vwt_ref[g, kp])
        yield
        g0 = pl.multiple_of((n_kt - 1 - bi) * 8, 8)
        qk = _dot(kc_ref[g, :, :HEAD_DIM], q4[g])
        nrow = lax.broadcasted_iota(jnp.int32, (n_cmp, QB), 0)
        tq = bi * QB + lax.broadcasted_iota(jnp.int32, (n_cmp, QB), 1)
        vis = tq - (nrow * NSA_CMP_STRIDE + (NSA_CMP_LEN - 1)) >= 0
        live = (tq >= NSA_CMP_LEN - 1).astype(F32)
        yield
        p_cs = []
        for h in range(MIX_HEADS):
            s = jnp.where(vis, head_cols(qk, h) + bc_ref[pl.ds(g0, n_cmp), h * QB:(h + 1) * QB], NEG_INF)
            e = jnp.exp2(s - _colmax(s))
            p_cs.append(e / _colsum(e) * live)
        yield
        o_c[g] = _dot(vct_ref[g], jnp.concatenate(p_cs, axis=1).astype(BF16))
        p_sum = p_cs[0] + p_cs[1] + p_cs[2] + p_cs[3]
        p_hi, p_lo = _split2(p_sum)
        imp = _dot(ovt_ref[...], p_hi) + _dot(ovt_ref[...], p_lo)
        yield
        blk = lax.broadcasted_iota(jnp.int32, (n_blk, QB), 0)
        cur = (bi * QB + lax.broadcasted_iota(jnp.int32, (n_blk, QB), 1)) >> 6
        forced = (blk == 0) | (blk == cur) | (blk == cur - 1)
        imp = jnp.where(forced, imp + FORCE_BONUS, imp)
        imp = jnp.where(blk <= cur, imp, NEG_INF)
        blk_f = blk.astype(F32)
        sel = jnp.zeros((n_blk, QB), F32)
        for _ in range(NSA_TOP_N):
            mx = _colmax(imp)
            idx = jnp.min(jnp.where(imp == mx, blk_f, 4096.0), axis=0, keepdims=True)
            pick = blk_f == idx
            sel = jnp.where(pick, 1.0, sel)
            imp = jnp.where(pick, NEG_BIG, imp)
            yield
        selx = _dot(ext_ref[...], sel.astype(BF16))
        for kt in range(n_kt):
            selx_ref[g, kt] = (selx[kt * 128:(kt + 1) * 128, :] - 1.0) * (-NEG_INF)
        yield
        carry_s[g] = yield from pair_update(
            None, tile_qk(g, ks_ref, bi), diag_add(selx_ref[g, bi]), vst_ref[g, bi],
            tile_qk(g, ks_ref, kp), prev_add(selx_ref[g, kp] + pen_prev), vst_ref[g, kp])

    _round_robin([head_part(g) for g in range(G)])

    n_old = jnp.maximum(bi - 1, 0)

    def sel_step(g, j, carry, out):
        k0 = 2 * j
        k1 = jnp.minimum(k0 + 1, bi)
        pen1 = jnp.where(k0 + 1 < n_old, 0.0, NEG_INF)
        qk0 = tile_qk(g, ks_ref, k0)
        qk1 = tile_qk(g, ks_ref, k1)
        yield
        out[g] = yield from pair_update(carry, qk0, [selx_ref[g, k0]] * MIX_HEADS, vst_ref[g, k0],
                                        qk1, [selx_ref[g, k1] + pen1] * MIX_HEADS, vst_ref[g, k1])

    def win_step(g, j, carry, out):
        k0 = lo + 2 * j
        k1 = jnp.minimum(k0 + 1, bi)
        qk0 = tile_qk(g, kw_ref, k0)
        qk1 = tile_qk(g, kw_ref, k1)
        yield
        out[g] = yield from pair_update(carry, qk0, [win_add(k0, True)] * MIX_HEADS, vwt_ref[g, k0],
                                        qk1, [win_add(k1, k0 + 1 < n_old)] * MIX_HEADS, vwt_ref[g, k1])

    def joint(step):
        def body(j, carries):
            out = [None] * G
            _round_robin([step(g, j, carries[g], out) for g in range(G)])
            return tuple(out)
        return body

    res_s = lax.fori_loop(0, (n_old + 1) // 2, joint(sel_step), tuple(carry_s))
    res_w = lax.fori_loop(0, (n_old - lo + 1) // 2, joint(win_step), tuple(carry_w))

    def tail_part(g):
        _, l_s, acc_s = res_s[g]
        _, l_w, acc_w = res_w[g]
        o_s = acc_s * (1.0 / l_s)
        o_w = acc_w * (1.0 / l_w)
        gate = jax.nn.sigmoid(gl_ref[g].T)
        yield
        outs = []
        for h in range(MIX_HEADS):
            cs = slice(h * QB, (h + 1) * QB)
            outs.append(gate[3 * h:3 * h + 1] * o_c[g][:, cs] + gate[3 * h + 1:3 * h + 2] * o_s[:, cs]
                        + gate[3 * h + 2:3 * h + 3] * o_w[:, cs])
        o_ref[g] = jnp.concatenate(outs, axis=0).T

    _round_robin([tail_part(g) for g in range(G)])


def nsa_attention(z, prep, bias_c, tab, ovt, ext, B, S, G):
    n_qb = S // Q_BLOCK
    n_kt = S // 128
    n_cmp = S // NSA_CMP_STRIDE
    R = MIX_HEADS * Q_BLOCK
    kc, vct, ks, vst, kw, vwt = prep
    b3 = lambda b, i: (b, 0, 0)
    b4 = lambda b, i: (b, 0, 0, 0)
    return pl.pallas_call(
        _nsa_kernel,
        grid=(B // G, n_qb),
        in_specs=[
            pl.BlockSpec((G, Q_BLOCK, MIX_WIDTH), lambda b, i: (b, i, OFF_NSAQ // MIX_WIDTH)),
            pl.BlockSpec((G, Q_BLOCK, 128), lambda b, i: (b, i, OFF_NSAG // 128)),
            pl.BlockSpec((G, n_cmp, 128), b3),
            pl.BlockSpec((G, HEAD_DIM, n_cmp), b3),
            pl.BlockSpec((G, S, 128), b3),
            pl.BlockSpec((G, n_kt, HEAD_DIM, 128), b4),
            pl.BlockSpec((G, S, 128), b3),
            pl.BlockSpec((G, n_kt, HEAD_DIM, 128), b4),
            pl.BlockSpec(bias_c.shape, lambda b, i: (0, 0)),
            pl.BlockSpec((2, 128, R), lambda b, i: (0, 0, 0)),
            pl.BlockSpec(ovt.shape, lambda b, i: (0, 0)),
            pl.BlockSpec(ext.shape, lambda b, i: (0, 0)),
        ],
        out_specs=pl.BlockSpec((G, Q_BLOCK, MIX_WIDTH), lambda b, i: (b, i, 0)),
        out_shape=jax.ShapeDtypeStruct((B, S, MIX_WIDTH), F32),
        scratch_shapes=[pltpu.VMEM((G, n_kt, 128, Q_BLOCK), F32)],
        compiler_params=_cparams(("parallel", "arbitrary")),
        name="nsa_attention",
    )(z, z, kc, vct, ks, vst, kw, vwt, bias_c, tab, ovt, ext)


def _stack_heads(x, head_of_lane):
    return jnp.concatenate([jnp.where(head_of_lane == h, x, jnp.zeros_like(x))
                            for h in range(MIX_HEADS)], axis=0)


def _head_norm(y, seg_mean, eps):
    mu = _dot_x2(y, seg_mean)
    d = y - mu
    var = _dot_x2(d * d, seg_mean)
    return d * lax.rsqrt(var + eps)


def _round_robin(chains):
    while chains:
        chains = [c for c in chains if next(c, StopIteration) is not StopIteration]


def _retention_kernel(z_ref, cos_ref, sin_ref, indec_ref, qdec_ref, kdec_ref, cdec_ref, bd_ref,
                      segm_ref, ng_ref, o_ref, st_ref):
    @pl.when(pl.program_id(1) == 0)
    def _():
        st_ref[...] = jnp.zeros_like(st_ref)

    _round_robin([_retention_chunk(z_ref.at[g], cos_ref, sin_ref, indec_ref, qdec_ref, kdec_ref, cdec_ref,
                                   bd_ref, segm_ref, ng_ref, o_ref.at[g], st_ref.at[g])
                  for g in range(z_ref.shape[0])])


def _retention_chunk(z_ref, cos_ref, sin_ref, indec_ref, qdec_ref, kdec_ref, cdec_ref, bd_ref,
                     segm_ref, ng_ref, o_ref, st_ref):
    L = RET_CHUNK
    z = z_ref[...]
    q = z[:, 0:256]
    k = z[:, 256:512]
    v = z[:, 512:768]
    g = z[:, 768:1024]
    cos = cos_ref[...]
    sin = sin_ref[...]

    def rot(u):
        u1 = u[:, :128]
        u2 = u[:, 128:]
        return jnp.concatenate([u1 * cos - u2 * sin, u2 * cos + u1 * sin], axis=1)

    qr = rot(q) * (HEAD_DIM ** -0.5)
    kr = rot(k)
    lane = lax.broadcasted_iota(jnp.int32, (L, 256), 1)
    head_qk = (lane & 127) >> 5
    head_v = lane >> 6
    qb = qr.astype(BF16)
    kb = kr.astype(BF16)
    vb = v.astype(BF16)
    yield

    att = _dot_nt(_stack_heads(qb, head_qk), kb) * indec_ref[...]
    yield
    o_st = _dot(att.astype(BF16), vb)
    o = jnp.zeros((L, 256), F32)
    for h in range(MIX_HEADS):
        o = o + jnp.where(head_v == h, o_st[h * L:(h + 1) * L], 0.0)
    yield
    state = st_ref[...]
    o = o + _dot(qb, state.astype(BF16)) * qdec_ref[...]
    st_ref[...] = state * cdec_ref[...] + _dot_tn((kr * kdec_ref[...]).astype(BF16), vb) * bd_ref[...]
    yield

    mu = _dot_x2(o, segm_ref[...])
    yield
    d = o - mu
    var = _dot_x2(d * d, segm_ref[...])
    yield
    o_ref[...] = g * jax.nn.sigmoid(g) * (d * lax.rsqrt(var + RET_NORM_EPS) * ng_ref[...])


def retention(z, tabs, ng, layer, B, S, G):
    L = RET_CHUNK
    nc = S // L
    cos, sin, indec, qdec, kdec, cdec, bd, segm = tabs
    const2 = lambda b, c: (0, 0)
    return pl.pallas_call(
        _retention_kernel,
        grid=(B // G, nc),
        in_specs=[
            pl.BlockSpec((G, L, 1024), lambda b, c: (b, c, OFF_RET // 1024)),
            pl.BlockSpec((L, 128), lambda b, c: (c, 0)),
            pl.BlockSpec((L, 128), lambda b, c: (c, 0)),
            pl.BlockSpec((MIX_HEADS * L, L), const2),
            pl.BlockSpec((L, 256), const2),
            pl.BlockSpec((L, 256), const2),
            pl.BlockSpec((1, 256), const2),
            pl.BlockSpec((256, 256), const2),
            pl.BlockSpec((256, 256), const2),
            pl.BlockSpec((None, 1, 256), lambda b, c: (layer, 0, 0)),
        ],
        out_specs=pl.BlockSpec((G, L, 256), lambda b, c: (b, c, 0)),
        out_shape=jax.ShapeDtypeStruct((B, S, 256), F32),
        scratch_shapes=[pltpu.VMEM((G, 256, 256), F32)],
        compiler_params=_cparams(("parallel", "arbitrary")),
        name="retention",
    )(z, cos, sin, indec, qdec, kdec, cdec, bd, segm, ng)


def _rwkv_kernel(z_ref, mu_ref, lw_ref, vec_ref, tri_ref, bdl_ref, wide_ref,
                 segs_ref, segm_ref, o_ref, st_ref, prev_ref):
    @pl.when(pl.program_id(1) == 0)
    def _():
        st_ref[...] = jnp.zeros_like(st_ref)
        prev_ref[...] = jnp.zeros_like(prev_ref)

    _round_robin([_rwkv_chunk(z_ref.at[g], mu_ref, lw_ref, vec_ref, tri_ref, bdl_ref,
                              wide_ref, segs_ref, segm_ref, o_ref.at[g], st_ref.at[g], prev_ref.at[g])
                  for g in range(z_ref.shape[0])])


def _rwkv_chunk(z_ref, mu_ref, lw_ref, vec_ref, tri_ref, bdl_ref, wide_ref,
                segs_ref, segm_ref, o_ref, st_ref, prev_ref):
    C = RWKV_CHUNK
    z = z_ref[...]
    rows = lax.broadcasted_iota(jnp.int32, (C, 1024), 0)
    zs = jnp.where(rows == 0, prev_ref[0:1, :], pltpu.roll(z, 1, axis=0))
    prev_ref[...] = jnp.broadcast_to(z[C - 1:C, :], prev_ref.shape)
    zf = z + (zs - z) * mu_ref[...]
    r = zf[:, 0:256]
    k = zf[:, 256:512]
    v = zf[:, 512:768]
    lora = zf[:, 768:896]

    vec = vec_ref[...]
    w0, a0, k_k, k_a, r_k, ln_g, ln_b = (vec[i:i + 1] for i in range(7))

    lane = lax.broadcasted_iota(jnp.int32, lora.shape, 1)
    lora_in = jnp.where(lane < RWKV_DECAY_LORA, jnp.tanh(lora),
                        jnp.where(lane < RWKV_DECAY_LORA + RWKV_AAA_LORA, lora, jax.nn.sigmoid(lora)))
    hi, lo = _split2(lora_in)
    both = _dot(jnp.concatenate([hi, lo], axis=0), lw_ref[0])
    lora_out = both[:C] + both[C:] + _dot(hi, lw_ref[1])
    wpre = w0 + lora_out[:, 0:256]
    y = -wpre
    softplus = jnp.maximum(y, 0.0) + jnp.log(1.0 + jnp.exp(-jnp.abs(y)))
    w_log = -softplus - 0.5
    ld = -jnp.exp(w_log)
    a = jax.nn.sigmoid(a0 + lora_out[:, 256:512])
    gate = lora_out[:, 512:768]
    kk = k * k_k
    k2 = k * (1.0 + (a - 1.0) * k_a)
    kk_sq, bonus = _dot_x2(kk * kk, segs_ref[...], r * k2 * r_k)
    kk = kk / jnp.maximum(jnp.sqrt(kk_sq), 1e-12)
    yield

    tri = tri_ref[...]
    l1 = ld.astype(BF16)
    r1 = ld - l1.astype(F32)
    l2 = r1.astype(BF16)
    l3 = (r1 - l2.astype(F32)).astype(BF16)
    cs = _dot(tri, l1) + _dot(tri, l2) + _dot(tri, l3)
    cs_end = cs[C - 1:C, :]
    yield
    e_neg = jnp.exp(-cs)
    e_end = jnp.exp(cs_end - cs)
    kka = kk * a
    a_t = -kk * jnp.exp(cs - ld)
    r_t = r * jnp.exp(cs)
    b_t = kka * e_neg
    k_t = k2 * e_neg
    b_g = kka * e_end
    k_g = k2 * e_end

    lane = lax.broadcasted_iota(jnp.int32, (C, 256), 1)
    head = lane >> 6
    stack = lambda x: _stack_heads(x.astype(BF16), head)
    a_st, b_st, k_st, v_st = stack(a_t), stack(b_t), stack(k_t), stack(v)

    n_pow = _dot_nt(a_st, b_st) * bdl_ref[...]
    ri = lax.broadcasted_iota(jnp.int32, (4 * C, 4 * C), 0)
    ci = lax.broadcasted_iota(jnp.int32, (4 * C, 4 * C), 1)
    t_inv = jnp.where(ri == ci, 1.0, 0.0) + n_pow
    yield

    ar = jnp.concatenate([a_t, r_t], axis=0).astype(BF16)
    bk = jnp.concatenate([b_st, k_st], axis=0)
    wide = _dot_nt(ar, bk)
    strict = wide_ref[0]
    incl = wide_ref[1]
    a_ak = (wide[0:C, 4 * C:] * strict).astype(BF16)
    a_rb = (wide[C:, 0:4 * C] * incl).astype(BF16)
    a_rk = (wide[C:, 4 * C:] * incl).astype(BF16)
    yield

    sq = 2
    while sq < C:
        nb = n_pow.astype(BF16)
        n_pow = _dot(nb, nb)
        yield
        t_inv = t_inv + _dot(t_inv.astype(BF16), n_pow.astype(BF16))
        sq *= 2
    yield

    state = st_ref[...]
    sb = state.astype(BF16)
    x = _dot_nt(a_t.astype(BF16), sb) + _dot(a_ak, v_st)
    yield
    u_st = _dot(t_inv.astype(BF16), stack(x)).astype(BF16)
    yield
    yv = _dot_nt(r_t.astype(BF16), sb) + _dot(a_rb, u_st) + _dot(a_rk, v_st)
    uv = jnp.concatenate([u_st, v_st], axis=0)
    bkg = jnp.concatenate([stack(b_g), stack(k_g)], axis=0)
    st_ref[...] = state * jnp.exp(cs_end) + _dot_tn(uv, bkg)
    yield

    yn = _head_norm(yv, segm_ref[...], RWKV_GN_EPS) * ln_g + ln_b
    yn = yn + bonus * v
    o_ref[...] = yn * gate


def rwkv7(z, mu, lora_w, vec, consts, layer, B, S, G):
    C = RWKV_CHUNK
    nc = S // C
    tri, bdl, wide, segs, segm = consts
    const2 = lambda b, c: (0, 0)
    return pl.pallas_call(
        _rwkv_kernel,
        grid=(B // G, nc),
        in_specs=[
            pl.BlockSpec((G, C, 1024), lambda b, c: (b, c, OFF_RWKV // 1024)),
            pl.BlockSpec((None, 1, 1024), lambda b, c: (layer, 0, 0)),
            pl.BlockSpec((None, 2, 128, 768), lambda b, c: (layer, 0, 0, 0)),
            pl.BlockSpec((None, 8, 256), lambda b, c: (layer, 0, 0)),
            pl.BlockSpec((C, C), const2),
            pl.BlockSpec((4 * C, 4 * C), const2),
            pl.BlockSpec((2, C, 4 * C), lambda b, c: (0, 0, 0)),
            pl.BlockSpec((256, 256), const2),
            pl.BlockSpec((256, 256), const2),
        ],
        out_specs=pl.BlockSpec((G, C, 256), lambda b, c: (b, c, 0)),
        out_shape=jax.ShapeDtypeStruct((B, S, 256), F32),
        scratch_shapes=[pltpu.VMEM((G, 256, 256), F32), pltpu.VMEM((G, 8, 1024), F32)],
        compiler_params=_cparams(("parallel", "arbitrary")),
        name="rwkv7",
    )(z, mu, lora_w, vec, tri, bdl, wide, segs, segm)


def _merge_kernel(x_ref, zc_ref, on_ref, or_ref, ow_ref, cw_ref, gpre_ref, wg_ref, wb_ref, wo_ref, g_ref,
                  o_ref, carry_ref):
    tm = x_ref.shape[0]
    x = x_ref[...]
    h = _rms(x, gpre_ref[...]).astype(BF16)

    @pl.when(pl.program_id(1) == 0)
    def _():
        carry_ref[...] = jnp.zeros_like(carry_ref)

    zc = zc_ref[...]
    b_g = zc[:, 0:256]
    u = zc[:, 256:512] * zc[:, 512:768]
    prev = carry_ref[...]
    rows = lax.broadcasted_iota(jnp.int32, (tm, 256), 0)
    u1 = jnp.where(rows == 0, prev[7:8], pltpu.roll(u, 1, axis=0))
    u2 = jnp.where(rows == 0, prev[6:7], jnp.where(rows == 1, prev[7:8], pltpu.roll(u, 2, axis=0)))
    carry_ref[...] = u[tm - 8:tm]
    cw = cw_ref[...]
    o_conv = b_g * (cw[0:1] * u2 + cw[1:2] * u1 + cw[2:3] * u)

    branches = (on_ref[...], or_ref[...], ow_ref[...], o_conv)
    merged = jnp.zeros((tm, D_MODEL), F32)
    for m in range(N_BRANCHES):
        gm = jax.nn.sigmoid(_dot(h, wg_ref[:, m * D_MODEL:(m + 1) * D_MODEL]))
        merged = merged + gm * _dot(branches[m].astype(BF16), wb_ref[m])
    y = _dot(merged.astype(BF16), wo_ref[...])
    o_ref[...] = x + _rms(y, g_ref[...])


def merge_mix(x, z, o_nsa, o_ret, o_rwkv, conv_w, g_pre, w_gate, w_branch, w_out, g_post, layer, B, S, tm):
    nt = S // tm
    rowmap = lambda b, i: (b * nt + i, 0)
    lmap = lambda b, i: (layer, 0, 0)
    return pl.pallas_call(
        _merge_kernel,
        grid=(B, nt),
        in_specs=[
            pl.BlockSpec((tm, D_MODEL), rowmap),
            pl.BlockSpec((tm, 768), lambda b, i: (b * nt + i, OFF_CONV // 768)),
            pl.BlockSpec((tm, 256), rowmap),
            pl.BlockSpec((tm, 256), rowmap),
            pl.BlockSpec((tm, 256), rowmap),
            pl.BlockSpec((None, 8, 256), lmap),
            pl.BlockSpec((None, 1, D_MODEL), lmap),
            pl.BlockSpec((None, D_MODEL, N_BRANCHES * D_MODEL), lmap, pipeline_mode=pl.Buffered(1)),
            pl.BlockSpec((None, N_BRANCHES, MIX_WIDTH, D_MODEL), lambda b, i: (layer, 0, 0, 0),
                         pipeline_mode=pl.Buffered(1)),
            pl.BlockSpec((None, D_MODEL, D_MODEL), lmap, pipeline_mode=pl.Buffered(1)),
            pl.BlockSpec((None, 1, D_MODEL), lmap),
        ],
        out_specs=pl.BlockSpec((tm, D_MODEL), rowmap),
        out_shape=jax.ShapeDtypeStruct((B * S, D_MODEL), F32),
        scratch_shapes=[pltpu.VMEM((8, 256), F32)],
        compiler_params=_cparams(("parallel", "arbitrary")),
        name="merge_mix",
    )(x, z, o_nsa, o_ret, o_rwkv, conv_w, g_pre, w_gate, w_branch, w_out, g_post)


def _xattn_kernel(x_ref, kv_ref, gpre_ref, wq_ref, wo_ref, gpost_ref, o_ref):
    n_sub = 1
    rows = x_ref.shape[0] // n_sub

    def part(r0):
        rs = pl.ds(r0, rows)
        x = x_ref[rs, :]
        h = _rms(x, gpre_ref[...]).astype(BF16)
        yield
        q = (_dot(h, wq_ref[...]) * (XA_HEAD_DIM ** -0.5)).astype(BF16)
        yield
        outs = []
        for hd in range(XA_HEADS):
            cs = slice(hd * XA_HEAD_DIM, (hd + 1) * XA_HEAD_DIM)
            kh = kv_ref[:, hd * XA_HEAD_DIM:(hd + 1) * XA_HEAD_DIM]
            vh = kv_ref[:, D_MODEL + hd * XA_HEAD_DIM:D_MODEL + (hd + 1) * XA_HEAD_DIM]
            s = _dot_nt(q[:, cs], kh)
            yield
            e = jnp.exp(s - _rowmax(s))
            p = e / _rowsum(e)
            outs.append(_dot(p.astype(BF16), vh).astype(BF16))
            yield
        o = jnp.concatenate(outs, axis=1)
        y = _dot(o, wo_ref[...])
        yield
        o_ref[rs, :] = x + _rms(y, gpost_ref[...])

    _round_robin([part(i * rows) for i in range(n_sub)])


def cross_attention(x, kv, g_pre, wq, wo, g_post, layer, B, S, M, tm):
    nt = S // tm
    rowmap = lambda b, i: (b * nt + i, 0)
    lmap = lambda b, i: (layer, 0, 0)
    return pl.pallas_call(
        _xattn_kernel,
        grid=(B, nt),
        in_specs=[
            pl.BlockSpec((tm, D_MODEL), rowmap),
            pl.BlockSpec((M, 2 * D_MODEL), lambda b, i: (b, 0)),
            pl.BlockSpec((None, 1, D_MODEL), lmap),
            pl.BlockSpec((None, D_MODEL, D_MODEL), lmap),
            pl.BlockSpec((None, D_MODEL, D_MODEL), lmap),
            pl.BlockSpec((None, 1, D_MODEL), lmap),
        ],
        out_specs=pl.BlockSpec((tm, D_MODEL), rowmap),
        out_shape=jax.ShapeDtypeStruct((B * S, D_MODEL), F32),
        compiler_params=_cparams(("parallel", "arbitrary")),
        name="cross_attention",
    )(x, kv, g_pre, wq, wo, g_post)


def _mlp_kernel(x_ref, gpre_ref, w1_ref, w2_ref, gpost_ref, o_ref):
    x = x_ref[...]
    h = _rms(x, gpre_ref[...]).astype(BF16)
    tf = 1024
    acc = jnp.zeros(x.shape, F32)
    for f in range(D_FF // tf):
        a = jnp.maximum(_dot(h, w1_ref[:, f * tf:(f + 1) * tf]), 0.0)
        acc = acc + _dot((a * a).astype(BF16), w2_ref[f * tf:(f + 1) * tf, :])
    o_ref[...] = x + _rms(acc, gpost_ref[...])


def mlp(x, g_pre, w1, w2, g_post, layer, tm):
    M = x.shape[0]
    lmap = lambda i: (layer, 0, 0)
    return pl.pallas_call(
        _mlp_kernel,
        grid=(M // tm,),
        in_specs=[
            pl.BlockSpec((tm, D_MODEL), lambda i: (i, 0)),
            pl.BlockSpec((None, 1, D_MODEL), lmap),
            pl.BlockSpec((None, D_MODEL, D_FF), lmap, pipeline_mode=pl.Buffered(1)),
            pl.BlockSpec((None, D_FF, D_MODEL), lmap, pipeline_mode=pl.Buffered(1)),
            pl.BlockSpec((None, 1, D_MODEL), lmap),
        ],
        out_specs=pl.BlockSpec((tm, D_MODEL), lambda i: (i, 0)),
        out_shape=jax.ShapeDtypeStruct((M, D_MODEL), F32),
        compiler_params=_cparams(("parallel",)),
        name="mlp",
    )(x, g_pre, w1, w2, g_post)


def _t5_bucket(dist):
    n = jnp.maximum(dist, 0)
    max_exact = N_BUCKETS // 2
    nf = jnp.maximum(n, 1).astype(F32)
    large = max_exact + (jnp.log(nf / max_exact) / math.log(MAX_DISTANCE / max_exact)
                         * (N_BUCKETS - max_exact)).astype(jnp.int32)
    large = jnp.minimum(large, N_BUCKETS - 1)
    return jnp.where(n < max_exact, n, large)


def _nsa_tables(rel_bias, S):
    n_qb = S // Q_BLOCK
    n_cmp = S // NSA_CMP_STRIDE
    n_blk = S // NSA_SLC_LEN
    bias_f = rel_bias.astype(F32)
    i = jnp.arange(Q_BLOCK)
    per_qb = Q_BLOCK // NSA_CMP_STRIDE
    g = jnp.arange(n_cmp + per_qb * (n_qb - 1)) - per_qb * (n_qb - 1)
    d_c = i[None, :] - (g[:, None] * NSA_CMP_STRIDE + NSA_CMP_LEN - 1)
    def lookup(dist):
        hit = _t5_bucket(dist)[..., None, None] == jnp.arange(N_BUCKETS)[:, None]
        return jnp.sum(jnp.where(hit, bias_f, 0.0), axis=-2)

    bias_c = lookup(d_c).transpose(0, 2, 1).reshape(g.shape[0], MIX_HEADS * Q_BLOCK)
    d0 = i[:, None] - i[None, :]
    tab = jnp.stack([lookup(d0), lookup(d0 + Q_BLOCK)]) - bias_f[N_BUCKETS - 1]
    tab = tab.transpose(0, 2, 3, 1).reshape(2, Q_BLOCK, MIX_HEADS * Q_BLOCK)
    cs = np.arange(n_cmp)[None, :] * NSA_CMP_STRIDE
    ss = np.arange(n_blk)[:, None] * NSA_SLC_LEN
    ovt = np.clip(np.minimum(cs + NSA_CMP_LEN, ss + NSA_SLC_LEN) - np.maximum(cs, ss), 0, None)
    ovt = ovt.astype(np.float32) / NSA_CMP_LEN
    ovt[:, (S - NSA_CMP_LEN) // NSA_CMP_STRIDE + 1:] = 0.0
    ext = ((np.arange(S)[:, None] // NSA_SLC_LEN) == np.arange(n_blk)[None, :]).astype(np.float32)
    return bias_c * LOG2E, tab * LOG2E, jnp.asarray(ovt, BF16), jnp.asarray(ext, BF16)


def _ret_tables(S):
    L = RET_CHUNK
    H = MIX_HEADS
    half = HEAD_DIM // 2
    pos = jnp.arange(S)
    inv_freq = ROPE_BASE ** (-jnp.arange(half, dtype=F32) / half)
    ang = pos.astype(F32)[:, None] * inv_freq[None, :]
    cos = jnp.tile(jnp.cos(ang), (1, H))
    sin = jnp.tile(jnp.sin(ang), (1, H))
    lg = jnp.log(1.0 - 2.0 ** (-5.0 - jnp.arange(H, dtype=F32)))
    n = jnp.arange(L, dtype=F32)
    diff = n[:, None] - n[None, :]
    inner = jnp.where(diff >= 0, jnp.exp(jnp.maximum(diff, 0.0)[None] * lg[:, None, None]), 0.0)
    indec = inner.reshape(H * L, L)
    q_decay = jnp.exp((n + 1.0)[None, :] * lg[:, None])
    k_decay = jnp.exp((L - 1.0 - n)[None, :] * lg[:, None])
    chunk_decay = jnp.exp(L * lg)
    lane = np.arange(256)
    head_v = lane // HEAD_DIM
    head_qk = (lane % 128) // half
    qdec = q_decay.T[:, head_v]
    kdec = k_decay.T[:, head_qk]
    cdec = chunk_decay[head_v][None, :]
    bd = jnp.asarray((head_qk[:, None] == head_v[None, :]).astype(np.float32))
    segm = jnp.asarray((head_v[:, None] == head_v[None, :]).astype(np.float32) / HEAD_DIM, BF16)
    return cos, sin, indec, qdec, kdec, cdec, bd, segm


def _rwkv_consts():
    C = RWKV_CHUNK
    t = np.arange(C)
    tri = (t[:, None] >= t[None, :]).astype(np.float32)
    r = np.arange(4 * C)
    bdl = ((r[:, None] // C == r[None, :] // C) & (r[:, None] % C > r[None, :] % C)).astype(np.float32)
    strict = (t[:, None] > (r[None, :] % C)).astype(np.float32)
    incl = (t[:, None] >= (r[None, :] % C)).astype(np.float32)
    lane = np.arange(256) // HEAD_DIM
    seg = (lane[:, None] == lane[None, :]).astype(np.float32)
    return (jnp.asarray(tri, BF16), jnp.asarray(bdl), jnp.asarray(np.stack([strict, incl])),
            jnp.asarray(seg, BF16), jnp.asarray(seg / HEAD_DIM, BF16))


def _hi_lo(w):
    hi = w.astype(BF16)
    lo = (w - hi.astype(F32)).astype(BF16)
    return jnp.stack([hi, lo], axis=1)


def _pad_rows(w, top, total):
    return jnp.pad(w, ((0, 0), (top, total - top - w.shape[1]), (0, 0)))


def _layout_w_in(w_in):
    L = w_in.shape[0]
    o = 0
    nsa_q = w_in[:, :, o:o + 256]; o += 256
    nsa_kv = w_in[:, :, o:o + 384]; o += 384
    nsa_g = w_in[:, :, o:o + 12]; o += 12
    ret = w_in[:, :, o:o + 1024]; o += 1024
    rwkv = w_in[:, :, o:o + RWKV_COLS]; o += RWKV_COLS
    conv = w_in[:, :, o:o + 768]; o += 768
    gate = w_in[:, :, o:o + 4096]

    def rot_perm(w):
        return w.reshape(L, D_MODEL, MIX_HEADS, 2, HEAD_DIM // 2).transpose(0, 1, 3, 2, 4).reshape(L, D_MODEL, 256)

    ret = jnp.concatenate([rot_perm(ret[:, :, 0:256]), rot_perm(ret[:, :, 256:512]), ret[:, :, 512:]], axis=2)
    zeros = lambda n: jnp.zeros((L, D_MODEL, n), w_in.dtype)
    out = jnp.concatenate([conv, nsa_q, ret, rwkv, zeros(1024 - RWKV_COLS), nsa_kv, nsa_g,
                           zeros(128 - 12)], axis=2)
    assert out.shape[2] == Z_COLS
    gate = jnp.concatenate([gate, zeros(128)], axis=2)
    return out.astype(BF16), gate.astype(BF16)


def _layout_cmp(cmp_w, cmp_pe):
    L = cmp_w.shape[0]
    wk = cmp_w[:, 0]
    wv = cmp_w[:, 1]
    zero = jnp.zeros_like(wk)
    blk = jnp.concatenate([jnp.concatenate([wk, zero], axis=3), jnp.concatenate([zero, wv], axis=3)], axis=2)
    pe2 = jnp.concatenate([cmp_pe, cmp_pe], axis=2)[:, :, None, :]
    s = NSA_CMP_STRIDE
    return blk[:, :s].astype(BF16), blk[:, s:].astype(BF16), pe2[:, :s], pe2[:, s:]


def kernel(x, mem, ln_mix_pre, w_in, nsa_cmp_w, nsa_cmp_pe, rel_bias, ret_norm_g, rwkv_mu, rwkv_w0, rwkv_w2, rwkv_a0, rwkv_a2, rwkv_g2, rwkv_k_k, rwkv_k_a, rwkv_r_k, rwkv_ln_g, rwkv_ln_b, conv_w, w_branch, w_mix_out, ln_mix_post, ln_xa_pre, ln_mem, xa_wq, xa_wkv, xa_wo, ln_xa_post, ln_mlp_pre, mlp_w1, mlp_w2, ln_mlp_post):
    B, S, D = x.shape
    M = mem.shape[1]
    depth = w_in.shape[0]
    row = lambda g: g[:, None, :]

    w_in_b, w_gate_b = _layout_w_in(w_in)
    cmp_lo, cmp_hi, pe_lo, pe_hi = _layout_cmp(nsa_cmp_w, nsa_cmp_pe)
    nsa_tabs = _nsa_tables(rel_bias, S)
    ret_tabs = _ret_tables(S)
    rwkv_consts = _rwkv_consts()
    mu = jnp.pad(rwkv_mu, ((0, 0), (0, 1024 - RWKV_COLS)))[:, None, :]
    lora_w = _hi_lo(jnp.concatenate([_pad_rows(rwkv_w2, 0, 128), _pad_rows(rwkv_a2, RWKV_DECAY_LORA, 128),
                                     _pad_rows(rwkv_g2, RWKV_DECAY_LORA + RWKV_AAA_LORA, 128)], axis=2))
    vec = jnp.stack([rwkv_w0, rwkv_a0, rwkv_k_k, rwkv_k_a, rwkv_r_k, rwkv_ln_g, rwkv_ln_b,
                     jnp.zeros_like(rwkv_w0)], axis=1)
    conv_p = jnp.pad(conv_w, ((0, 0), (0, 8 - CONV_WIDTH), (0, 0)))
    w_branch_b = w_branch.astype(BF16)
    w_mix_out_b = w_mix_out.astype(BF16)
    xa_wq_b = xa_wq.astype(BF16)
    xa_wkv_b = xa_wkv.astype(BF16)
    xa_wo_b = xa_wo.astype(BF16)
    mlp_w1_b = mlp_w1.astype(BF16)
    mlp_w2_b = mlp_w2.astype(BF16)

    xf = x.reshape(B * S, D)
    memf = mem.reshape(B * M, D)
    for l in range(depth):
        z = norm_matmul(xf, row(ln_mix_pre), w_in_b, l, tm=512, tn=Z_COLS, out_dtype=F32)
        prep = nsa_compress(z, cmp_lo, cmp_hi, pe_lo, pe_hi, l, B, S)
        z3 = z.reshape(B, S, Z_COLS)
        o_nsa = nsa_attention(z3, prep, *nsa_tabs, B, S, G=8).reshape(B * S, MIX_WIDTH)
        o_ret = retention(z3, ret_tabs, row(ret_norm_g), l, B, S, G=4).reshape(B * S, MIX_WIDTH)
        o_rwkv = rwkv7(z3, mu, lora_w, vec, rwkv_consts, l, B, S, G=4).reshape(B * S, MIX_WIDTH)
        xf = merge_mix(xf, z, o_nsa, o_ret, o_rwkv, conv_p, row(ln_mix_pre), w_gate_b, w_branch_b,
                       w_mix_out_b, row(ln_mix_post), l, B, S, tm=512)
        kvm = norm_matmul(memf, row(ln_mem), xa_wkv_b, l, tm=2 * M, tn=2 * D_MODEL, out_dtype=BF16)
        xf = cross_attention(xf, kvm, row(ln_xa_pre), xa_wq_b, xa_wo_b, row(ln_xa_post), l, B, S, M, tm=1024)
        xf = mlp(xf, row(ln_mlp_pre), mlp_w1_b, mlp_w2_b, row(ln_mlp_post), l, tm=512)
    return xf.reshape(B, S, D)
```

```python
import functools
import math

import numpy as np
import jax
import jax.numpy as jnp
from jax import lax
from jax.experimental import pallas as pl
from jax.experimental.pallas import tpu as pltpu

F32 = jnp.float32
BF16 = jnp.bfloat16

D_MODEL = 1024
N_BRANCHES = 4
MIX_WIDTH = 256
HEAD_DIM = 64
MIX_HEADS = 4

NSA_CMP_LEN = 32
NSA_CMP_STRIDE = 16
NSA_SLC_LEN = 64
NSA_TOP_N = 8
NSA_WINDOW = 512
Q_BLOCK = 128
FORCE_BONUS = 1e4
N_BUCKETS = 32
MAX_DISTANCE = 128

RET_CHUNK = 128
ROPE_BASE = 10000.0
RET_NORM_EPS = 1e-5

RWKV_DECAY_LORA = 32
RWKV_AAA_LORA = 32
RWKV_GATE_LORA = 64
RWKV_GN_EPS = 64e-5
RWKV_COLS = 3 * MIX_WIDTH + RWKV_DECAY_LORA + RWKV_AAA_LORA + RWKV_GATE_LORA
RWKV_CHUNK = 64

CONV_WIDTH = 3
XA_HEADS = 4
XA_HEAD_DIM = D_MODEL // XA_HEADS
D_FF = 4 * D_MODEL

RMS_EPS = 1e-6
LOG2E = math.log2(math.e)
NEG_INF = -1e30
NEG_BIG = -3e38

OFF_CONV = 0
OFF_NSAQ = 768
OFF_RET = 1024
OFF_RWKV = 2048
OFF_NSAKV = 3072
OFF_NSAG = 3456
Z_COLS = 3584

VMEM_LIMIT = 56 * 1024 * 1024


def _cparams(sem):
    return pltpu.CompilerParams(dimension_semantics=sem, vmem_limit_bytes=VMEM_LIMIT)


def _dot(a, b):
    return jnp.dot(a, b, preferred_element_type=F32)


def _dot_nt(a, b):
    return lax.dot_general(a, b, (((1,), (1,)), ((), ())), preferred_element_type=F32)


def _dot_tn(a, b):
    return lax.dot_general(a, b, (((0,), (0,)), ((), ())), preferred_element_type=F32)


def _split2(x):
    hi = x.astype(BF16)
    lo = (x - hi.astype(F32)).astype(BF16)
    return hi, lo


def _dot_x2(x, w_bf16, *more):
    xs = (x,) + more
    rows = x.shape[0]
    parts = [p for xi in xs for p in _split2(xi)]
    out = _dot(jnp.concatenate(parts, axis=0), w_bf16)
    res = [out[2 * i * rows:(2 * i + 1) * rows] + out[(2 * i + 1) * rows:(2 * i + 2) * rows]
           for i in range(len(xs))]
    return res[0] if not more else res


def _rms(x, g):
    ms = jnp.mean(x * x, axis=-1, keepdims=True)
    return x * lax.rsqrt(ms + RMS_EPS) * g


def _norm_matmul_kernel(x_ref, g_ref, w_ref, o_ref, h_ref):
    @pl.when(pl.program_id(1) == 0)
    def _():
        h_ref[...] = _rms(x_ref[...], g_ref[...]).astype(BF16)

    o_ref[...] = _dot(h_ref[...], w_ref[...]).astype(o_ref.dtype)


def norm_matmul(x, g, w, layer, tm, tn, out_dtype):
    M, D = x.shape
    N = w.shape[2]
    w_mode = dict(pipeline_mode=pl.Buffered(1)) if tn == N else {}
    return pl.pallas_call(
        _norm_matmul_kernel,
        grid=(M // tm, N // tn),
        in_specs=[
            pl.BlockSpec((tm, D), lambda i, j: (i, 0)),
            pl.BlockSpec((None, 1, D), lambda i, j: (layer, 0, 0)),
            pl.BlockSpec((None, D, tn), lambda i, j: (layer, 0, j), **w_mode),
        ],
        out_specs=pl.BlockSpec((tm, tn), lambda i, j: (i, j)),
        out_shape=jax.ShapeDtypeStruct((M, N), out_dtype),
        scratch_shapes=[pltpu.VMEM((tm, D), BF16)],
        compiler_params=_cparams(("parallel", "arbitrary")),
        name="norm_matmul",
    )(x, g, w)


def _nsa_compress_kernel(kvc_ref, kvs_ref, kvw_ref, wlo_ref, whi_ref, pelo_ref, pehi_ref,
                         kc_ref, vct_ref, ks_ref, vst_ref, kw_ref, vwt_ref):
    nblk = kc_ref.shape[0]
    n_kt = vst_ref.shape[0]
    ylo = jnp.zeros((nblk, 128), F32)
    yhi = jnp.zeros((nblk, 128), F32)
    for r in range(NSA_CMP_STRIDE):
        xr = kvc_ref[pl.ds(r, nblk, stride=NSA_CMP_STRIDE), :]
        ylo = ylo + _dot((xr + pelo_ref[r]).astype(BF16), wlo_ref[r])
        yhi = yhi + _dot((xr + pehi_ref[r]).astype(BF16), whi_ref[r])
    y = ylo + pltpu.roll(yhi, nblk - 1, axis=0)
    kc_ref[...] = y.astype(BF16)
    vct_ref[...] = y.T[HEAD_DIM:, :].astype(BF16)
    ks_ref[...] = kvs_ref[...].astype(BF16)
    kw_ref[...] = kvw_ref[...].astype(BF16)
    for kt in range(n_kt):
        rs = slice(kt * 128, (kt + 1) * 128)
        vst_ref[kt] = kvs_ref[rs, :].T[HEAD_DIM:, :].astype(BF16)
        vwt_ref[kt] = kvw_ref[rs, :].T[HEAD_DIM:, :].astype(BF16)


def nsa_compress(z, wlo, whi, pelo, pehi, layer, B, S):
    nblk = S // NSA_CMP_STRIDE
    n_kt = S // 128
    c0 = OFF_NSAKV // 128
    wmap = lambda b: (layer, 0, 0, 0)
    b3 = lambda b: (b, 0, 0)
    b4 = lambda b: (b, 0, 0, 0)
    return pl.pallas_call(
        _nsa_compress_kernel,
        grid=(B,),
        in_specs=[
            pl.BlockSpec((S, 128), lambda b: (b, c0)),
            pl.BlockSpec((S, 128), lambda b: (b, c0 + 1)),
            pl.BlockSpec((S, 128), lambda b: (b, c0 + 2)),
            pl.BlockSpec((None, NSA_CMP_STRIDE, 128, 128), wmap),
            pl.BlockSpec((None, NSA_CMP_STRIDE, 128, 128), wmap),
            pl.BlockSpec((None, NSA_CMP_STRIDE, 1, 128), wmap),
            pl.BlockSpec((None, NSA_CMP_STRIDE, 1, 128), wmap),
        ],
        out_specs=[
            pl.BlockSpec((None, nblk, 128), b3),
            pl.BlockSpec((None, HEAD_DIM, nblk), b3),
            pl.BlockSpec((None, S, 128), b3),
            pl.BlockSpec((None, n_kt, HEAD_DIM, 128), b4),
            pl.BlockSpec((None, S, 128), b3),
            pl.BlockSpec((None, n_kt, HEAD_DIM, 128), b4),
        ],
        out_shape=[
            jax.ShapeDtypeStruct((B, nblk, 128), BF16),
            jax.ShapeDtypeStruct((B, HEAD_DIM, nblk), BF16),
            jax.ShapeDtypeStruct((B, S, 128), BF16),
            jax.ShapeDtypeStruct((B, n_kt, HEAD_DIM, 128), BF16),
            jax.ShapeDtypeStruct((B, S, 128), BF16),
            jax.ShapeDtypeStruct((B, n_kt, HEAD_DIM, 128), BF16),
        ],
        compiler_params=_cparams(("parallel",)),
        name="nsa_compress",
    )(z, z, z, wlo, whi, pelo, pehi)


def _rowmax(s):
    return jnp.max(s, axis=-1, keepdims=True)


def _rowsum(s):
    return jnp.sum(s, axis=-1, keepdims=True)


def _colmax(s):
    return jnp.max(s, axis=0, keepdims=True)


def _colsum(s):
    return jnp.sum(s, axis=0, keepdims=True)


def _nsa_kernel(q_ref, gl_ref, kc_ref, vct_ref, ks_ref, vst_ref, kw_ref, vwt_ref, bc_ref, tab_ref,
                ovt_ref, ext_ref, o_ref, selx_ref):
    bi = pl.program_id(1)
    G = q_ref.shape[0]
    n_kt = selx_ref.shape[1]
    n_blk = ovt_ref.shape[0]
    n_cmp = kc_ref.shape[1]
    QB = Q_BLOCK
    R = MIX_HEADS * QB
    n_off = NSA_WINDOW // 128

    lo = jnp.maximum(bi - n_off, 0)
    q4 = [None] * G
    o_c = [None] * G
    carry_s = [None] * G
    carry_w = [None] * G

    def tile_qk(g, kref, kt):
        start = pl.multiple_of(kt * 128, 128)
        return _dot(kref[g, pl.ds(start, 128), :HEAD_DIM], q4[g])

    def head_cols(x, h):
        return x[:, h * QB:(h + 1) * QB]

    key1 = lax.broadcasted_iota(jnp.int32, (128, QB), 0)
    i1 = lax.broadcasted_iota(jnp.int32, (128, QB), 1)
    causal1 = key1 <= i1
    kp = jnp.maximum(bi - 1, 0)
    pen_prev = jnp.where(bi >= 1, 0.0, NEG_INF)

    def win_add(kt, valid):
        shift = jnp.where(valid, jnp.where(bi - kt < n_off, -4096, 0), 4096)
        return jnp.where(key1 > i1 + shift, 0.0, NEG_INF)

    def diag_add(extra):
        return [jnp.where(causal1, head_cols(tab_ref[0], h) + extra, NEG_INF) for h in range(MIX_HEADS)]

    def prev_add(extra):
        return [head_cols(tab_ref[1], h) + extra for h in range(MIX_HEADS)]

    def pair_update(carry, qk0, add0, vt0, qk1, add1, vt1):
        p0s, p1s, ms, ls, alphas = [], [], [], [], []
        for h in range(MIX_HEADS):
            a0 = head_cols(qk0, h) + add0[h]
            a1 = head_cols(qk1, h) + add1[h]
            m_new = jnp.maximum(_colmax(a0), _colmax(a1))
            if carry is not None:
                m_old = head_cols(carry[0], h)
                m_new = jnp.maximum(m_old, m_new)
                alpha = jnp.exp2(m_old - m_new)
                alphas.append(alpha)
            p0 = jnp.exp2(a0 - m_new)
            p1 = jnp.exp2(a1 - m_new)
            l_new = _colsum(p0) + _colsum(p1)
            if carry is not None:
                l_new = alpha * head_cols(carry[1], h) + l_new
            ms.append(m_new)
            ls.append(l_new)
            p0s.append(p0.astype(BF16))
            p1s.append(p1.astype(BF16))
        yield
        acc = _dot(vt0, jnp.concatenate(p0s, axis=1)) + _dot(vt1, jnp.concatenate(p1s, axis=1))
        if carry is not None:
            acc = jnp.concatenate(alphas, axis=1) * carry[2] + acc
        return jnp.concatenate(ms, axis=1), jnp.concatenate(ls, axis=1), acc

    def head_part(g):
        qt = (q_ref[g] * (HEAD_DIM ** -0.5 * LOG2E)).T
        q4[g] = jnp.concatenate([qt[h * HEAD_DIM:(h + 1) * HEAD_DIM] for h in range(MIX_HEADS)],
                                axis=1).astype(BF16)
        yield
        carry_w[g] = yield from pair_update(None, tile_qk(g, kw_ref, bi), diag_add(0.0), vwt_ref[g, bi],
                                            tile_qk(g, kw_ref, kp), prev_add(pen_prev), vwt_ref[g, kp])
        yield
        g0 = pl.multiple_of((n_kt - 1 - bi) * 8, 8)
        qk = _dot(kc_ref[g, :, :HEAD_DIM], q4[g])
        nrow = lax.broadcasted_iota(jnp.int32, (n_cmp, QB), 0)
        tq = bi * QB + lax.broadcasted_iota(jnp.int32, (n_cmp, QB), 1)
        vis = tq - (nrow * NSA_CMP_STRIDE + (NSA_CMP_LEN - 1)) >= 0
        live = (tq >= NSA_CMP_LEN - 1).astype(F32)
        yield
        p_cs = []
        for h in range(MIX_HEADS):
            s = jnp.where(vis, head_cols(qk, h) + bc_ref[pl.ds(g0, n_cmp), h * QB:(h + 1) * QB], NEG_INF)
            e = jnp.exp2(s - _colmax(s))
            p_cs.append(e / _colsum(e) * live)
        yield
        o_c[g] = _dot(vct_ref[g], jnp.concatenate(p_cs, axis=1).astype(BF16))
        p_sum = p_cs[0] + p_cs[1] + p_cs[2] + p_cs[3]
        p_hi, p_lo = _split2(p_sum)
        imp = _dot(ovt_ref[...], p_hi) + _dot(ovt_ref[...], p_lo)
        yield
        blk = lax.broadcasted_iota(jnp.int32, (n_blk, QB), 0)
        cur = (bi * QB + lax.broadcasted_iota(jnp.int32, (n_blk, QB), 1)) >> 6
        forced = (blk == 0) | (blk == cur) | (blk == cur - 1)
        imp = jnp.where(forced, imp + FORCE_BONUS, imp)
        imp = jnp.where(blk <= cur, imp, NEG_INF)
        blk_f = blk.astype(F32)
        sel = jnp.zeros((n_blk, QB), F32)
        for _ in range(NSA_TOP_N):
            mx = _colmax(imp)
            idx = jnp.min(jnp.where(imp == mx, blk_f, 4096.0), axis=0, keepdims=True)
            pick = blk_f == idx
            sel = jnp.where(pick, 1.0, sel)
            imp = jnp.where(pick, NEG_BIG, imp)
            yield
        selx = _dot(ext_ref[...], sel.astype(BF16))
        for kt in range(n_kt):
            selx_ref[g, kt] = (selx[kt * 128:(kt + 1) * 128, :] - 1.0) * (-NEG_INF)
        yield
        carry_s[g] = yield from pair_update(
            None, tile_qk(g, ks_ref, bi), diag_add(selx_ref[g, bi]), vst_ref[g, bi],
            tile_qk(g, ks_ref, kp), prev_add(selx_ref[g, kp] + pen_prev), vst_ref[g, kp])

    _round_robin([head_part(g) for g in range(G)])

    n_old = jnp.maximum(bi - 1, 0)

    def sel_step(g, j, carry, out):
        k0 = 2 * j
        k1 = jnp.minimum(k0 + 1, bi)
        pen1 = jnp.where(k0 + 1 < n_old, 0.0, NEG_INF)
        qk0 = tile_qk(g, ks_ref, k0)
        qk1 = tile_qk(g, ks_ref, k1)
        yield
        out[g] = yield from pair_update(carry, qk0, [selx_ref[g, k0]] * MIX_HEADS, vst_ref[g, k0],
                                        qk1, [selx_ref[g, k1] + pen1] * MIX_HEADS, vst_ref[g, k1])

    def win_step(g, j, carry, out):
        k0 = lo + 2 * j
        k1 = jnp.minimum(k0 + 1, bi)
        qk0 = tile_qk(g, kw_ref, k0)
        qk1 = tile_qk(g, kw_ref, k1)
        yield
        out[g] = yield from pair_update(carry, qk0, [win_add(k0, True)] * MIX_HEADS, vwt_ref[g, k0],
                                        qk1, [win_add(k1, k0 + 1 < n_old)] * MIX_HEADS, vwt_ref[g, k1])

    def joint(step):
        def body(j, carries):
            out = [None] * G
            _round_robin([step(g, j, carries[g], out) for g in range(G)])
            return tuple(out)
        return body

    res_s = lax.fori_loop(0, (n_old + 1) // 2, joint(sel_step), tuple(carry_s))
    res_w = lax.fori_loop(0, (n_old - lo + 1) // 2, joint(win_step), tuple(carry_w))

    def tail_part(g):
        _, l_s, acc_s = res_s[g]
        _, l_w, acc_w = res_w[g]
        o_s = acc_s * (1.0 / l_s)
        o_w = acc_w * (1.0 / l_w)
        gate = jax.nn.sigmoid(gl_ref[g].T)
        yield
        outs = []
        for h in range(MIX_HEADS):
            cs = slice(h * QB, (h + 1) * QB)
            outs.append(gate[3 * h:3 * h + 1] * o_c[g][:, cs] + gate[3 * h + 1:3 * h + 2] * o_s[:, cs]
                        + gate[3 * h + 2:3 * h + 3] * o_w[:, cs])
        o_ref[g] = jnp.concatenate(outs, axis=0).T

    _round_robin([tail_part(g) for g in range(G)])


def nsa_attention(z, prep, bias_c, tab, ovt, ext, B, S, G):
    n_qb = S // Q_BLOCK
    n_kt = S // 128
    n_cmp = S // NSA_CMP_STRIDE
    R = MIX_HEADS * Q_BLOCK
    kc, vct, ks, vst, kw, vwt = prep
    b3 = lambda b, i: (b, 0, 0)
    b4 = lambda b, i: (b, 0, 0, 0)
    return pl.pallas_call(
        _nsa_kernel,
        grid=(B // G, n_qb),
        in_specs=[
            pl.BlockSpec((G, Q_BLOCK, MIX_WIDTH), lambda b, i: (b, i, OFF_NSAQ // MIX_WIDTH)),
            pl.BlockSpec((G, Q_BLOCK, 128), lambda b, i: (b, i, OFF_NSAG // 128)),
            pl.BlockSpec((G, n_cmp, 128), b3),
            pl.BlockSpec((G, HEAD_DIM, n_cmp), b3),
            pl.BlockSpec((G, S, 128), b3),
            pl.BlockSpec((G, n_kt, HEAD_DIM, 128), b4),
            pl.BlockSpec((G, S, 128), b3),
            pl.BlockSpec((G, n_kt, HEAD_DIM, 128), b4),
            pl.BlockSpec(bias_c.shape, lambda b, i: (0, 0)),
            pl.BlockSpec((2, 128, R), lambda b, i: (0, 0, 0)),
            pl.BlockSpec(ovt.shape, lambda b, i: (0, 0)),
            pl.BlockSpec(ext.shape, lambda b, i: (0, 0)),
        ],
        out_specs=pl.BlockSpec((G, Q_BLOCK, MIX_WIDTH), lambda b, i: (b, i, 0)),
        out_shape=jax.ShapeDtypeStruct((B, S, MIX_WIDTH), F32),
        scratch_shapes=[pltpu.VMEM((G, n_kt, 128, Q_BLOCK), F32)],
        compiler_params=_cparams(("parallel", "arbitrary")),
        name="nsa_attention",
    )(z, z, kc, vct, ks, vst, kw, vwt, bias_c, tab, ovt, ext)


def _stack_heads(x, head_of_lane):
    return jnp.concatenate([jnp.where(head_of_lane == h, x, jnp.zeros_like(x))
                            for h in range(MIX_HEADS)], axis=0)


def _head_norm(y, seg_mean, eps):
    mu = _dot_x2(y, seg_mean)
    d = y - mu
    var = _dot_x2(d * d, seg_mean)
    return d * lax.rsqrt(var + eps)


def _round_robin(chains):
    while chains:
        chains = [c for c in chains if next(c, StopIteration) is not StopIteration]


def _retention_kernel(z_ref, cos_ref, sin_ref, indec_ref, qdec_ref, kdec_ref, cdec_ref, bd_ref,
                      segm_ref, ng_ref, o_ref, st_ref):
    @pl.when(pl.program_id(1) == 0)
    def _():
        st_ref[...] = jnp.zeros_like(st_ref)

    _round_robin([_retention_chunk(z_ref.at[g], cos_ref, sin_ref, indec_ref, qdec_ref, kdec_ref, cdec_ref,
                                   bd_ref, segm_ref, ng_ref, o_ref.at[g], st_ref.at[g])
                  for g in range(z_ref.shape[0])])


def _retention_chunk(z_ref, cos_ref, sin_ref, indec_ref, qdec_ref, kdec_ref, cdec_ref, bd_ref,
                     segm_ref, ng_ref, o_ref, st_ref):
    L = RET_CHUNK
    z = z_ref[...]
    q = z[:, 0:256]
    k = z[:, 256:512]
    v = z[:, 512:768]
    g = z[:, 768:1024]
    cos = cos_ref[...]
    sin = sin_ref[...]

    def rot(u):
        u1 = u[:, :128]
        u2 = u[:, 128:]
        return jnp.concatenate([u1 * cos - u2 * sin, u2 * cos + u1 * sin], axis=1)

    qr = rot(q) * (HEAD_DIM ** -0.5)
    kr = rot(k)
    lane = lax.broadcasted_iota(jnp.int32, (L, 256), 1)
    head_qk = (lane & 127) >> 5
    head_v = lane >> 6
    qb = qr.astype(BF16)
    kb = kr.astype(BF16)
    vb = v.astype(BF16)
    yield

    att = _dot_nt(_stack_heads(qb, head_qk), kb) * indec_ref[...]
    yield
    o_st = _dot(att.astype(BF16), vb)
    o = jnp.zeros((L, 256), F32)
    for h in range(MIX_HEADS):
        o = o + jnp.where(head_v == h, o_st[h * L:(h + 1) * L], 0.0)
    yield
    state = st_ref[...]
    o = o + _dot(qb, state.astype(BF16)) * qdec_ref[...]
    st_ref[...] = state * cdec_ref[...] + _dot_tn((kr * kdec_ref[...]).astype(BF16), vb) * bd_ref[...]
    yield

    mu = _dot_x2(o, segm_ref[...])
    yield
    d = o - mu
    var = _dot_x2(d * d, segm_ref[...])
    yield
    o_ref[...] = g * jax.nn.sigmoid(g) * (d * lax.rsqrt(var + RET_NORM_EPS) * ng_ref[...])


def retention(z, tabs, ng, layer, B, S, G):
    L = RET_CHUNK
    nc = S // L
    cos, sin, indec, qdec, kdec, cdec, bd, segm = tabs
    const2 = lambda b, c: (0, 0)
    return pl.pallas_call(
        _retention_kernel,
        grid=(B // G, nc),
        in_specs=[
            pl.BlockSpec((G, L, 1024), lambda b, c: (b, c, OFF_RET // 1024)),
            pl.BlockSpec((L, 128), lambda b, c: (c, 0)),
            pl.BlockSpec((L, 128), lambda b, c: (c, 0)),
            pl.BlockSpec((MIX_HEADS * L, L), const2),
            pl.BlockSpec((L, 256), const2),
            pl.BlockSpec((L, 256), const2),
            pl.BlockSpec((1, 256), const2),
            pl.BlockSpec((256, 256), const2),
            pl.BlockSpec((256, 256), const2),
            pl.BlockSpec((None, 1, 256), lambda b, c: (layer, 0, 0)),
        ],
        out_specs=pl.BlockSpec((G, L, 256), lambda b, c: (b, c, 0)),
        out_shape=jax.ShapeDtypeStruct((B, S, 256), F32),
        scratch_shapes=[pltpu.VMEM((G, 256, 256), F32)],
        compiler_params=_cparams(("parallel", "arbitrary")),
        name="retention",
    )(z, cos, sin, indec, qdec, kdec, cdec, bd, segm, ng)


def _rwkv_kernel(z_ref, mu_ref, lw_ref, vec_ref, tri_ref, bdl_ref, wide_ref,
                 segs_ref, segm_ref, o_ref, st_ref, prev_ref):
    @pl.when(pl.program_id(1) == 0)
    def _():
        st_ref[...] = jnp.zeros_like(st_ref)
        prev_ref[...] = jnp.zeros_like(prev_ref)

    _round_robin([_rwkv_chunk(z_ref.at[g], mu_ref, lw_ref, vec_ref, tri_ref, bdl_ref,
                              wide_ref, segs_ref, segm_ref, o_ref.at[g], st_ref.at[g], prev_ref.at[g])
                  for g in range(z_ref.shape[0])])


def _rwkv_chunk(z_ref, mu_ref, lw_ref, vec_ref, tri_ref, bdl_ref, wide_ref,
                segs_ref, segm_ref, o_ref, st_ref, prev_ref):
    C = RWKV_CHUNK
    z = z_ref[...]
    rows = lax.broadcasted_iota(jnp.int32, (C, 1024), 0)
    zs = jnp.where(rows == 0, prev_ref[0:1, :], pltpu.roll(z, 1, axis=0))
    prev_ref[...] = jnp.broadcast_to(z[C - 1:C, :], prev_ref.shape)
    zf = z + (zs - z) * mu_ref[...]
    r = zf[:, 0:256]
    k = zf[:, 256:512]
    v = zf[:, 512:768]
    lora = zf[:, 768:896]

    vec = vec_ref[...]
    w0, a0, k_k, k_a, r_k, ln_g, ln_b = (vec[i:i + 1] for i in range(7))

    lane = lax.broadcasted_iota(jnp.int32, lora.shape, 1)
    lora_in = jnp.where(lane < RWKV_DECAY_LORA, jnp.tanh(lora),
                        jnp.where(lane < RWKV_DECAY_LORA + RWKV_AAA_LORA, lora, jax.nn.sigmoid(lora)))
    hi, lo = _split2(lora_in)
    both = _dot(jnp.concatenate([hi, lo], axis=0), lw_ref[0])
    lora_out = both[:C] + both[C:] + _dot(hi, lw_ref[1])
    wpre = w0 + lora_out[:, 0:256]
    y = -wpre
    softplus = jnp.maximum(y, 0.0) + jnp.log(1.0 + jnp.exp(-jnp.abs(y)))
    w_log = -softplus - 0.5
    ld = -jnp.exp(w_log)
    a = jax.nn.sigmoid(a0 + lora_out[:, 256:512])
    gate = lora_out[:, 512:768]
    kk = k * k_k
    k2 = k * (1.0 + (a - 1.0) * k_a)
    kk_sq, bonus = _dot_x2(kk * kk, segs_ref[...], r * k2 * r_k)
    kk = kk / jnp.maximum(jnp.sqrt(kk_sq), 1e-12)
    yield

    tri = tri_ref[...]
    l1 = ld.astype(BF16)
    r1 = ld - l1.astype(F32)
    l2 = r1.astype(BF16)
    l3 = (r1 - l2.astype(F32)).astype(BF16)
    cs = _dot(tri, l1) + _dot(tri, l2) + _dot(tri, l3)
    cs_end = cs[C - 1:C, :]
    yield
    e_neg = jnp.exp(-cs)
    e_end = jnp.exp(cs_end - cs)
    kka = kk * a
    a_t = -kk * jnp.exp(cs - ld)
    r_t = r * jnp.exp(cs)
    b_t = kka * e_neg
    k_t = k2 * e_neg
    b_g = kka * e_end
    k_g = k2 * e_end

    lane = lax.broadcasted_iota(jnp.int32, (C, 256), 1)
    head = lane >> 6
    stack = lambda x: _stack_heads(x.astype(BF16), head)
    a_st, b_st, k_st, v_st = stack(a_t), stack(b_t), stack(k_t), stack(v)

    n_pow = _dot_nt(a_st, b_st) * bdl_ref[...]
    ri = lax.broadcasted_iota(jnp.int32, (4 * C, 4 * C), 0)
    ci = lax.broadcasted_iota(jnp.int32, (4 * C, 4 * C), 1)
    t_inv = jnp.where(ri == ci, 1.0, 0.0) + n_pow
    yield

    ar = jnp.concatenate([a_t, r_t], axis=0).astype(BF16)
    bk = jnp.concatenate([b_st, k_st], axis=0)
    wide = _dot_nt(ar, bk)
    strict = wide_ref[0]
    incl = wide_ref[1]
    a_ak = (wide[0:C, 4 * C:] * strict).astype(BF16)
    a_rb = (wide[C:, 0:4 * C] * incl).astype(BF16)
    a_rk = (wide[C:, 4 * C:] * incl).astype(BF16)
    yield

    sq = 2
    while sq < C:
        nb = n_pow.astype(BF16)
        n_pow = _dot(nb, nb)
        yield
        t_inv = t_inv + _dot(t_inv.astype(BF16), n_pow.astype(BF16))
        sq *= 2
    yield

    state = st_ref[...]
    sb = state.astype(BF16)
    x = _dot_nt(a_t.astype(BF16), sb) + _dot(a_ak, v_st)
    yield
    u_st = _dot(t_inv.astype(BF16), stack(x)).astype(BF16)
    yield
    yv = _dot_nt(r_t.astype(BF16), sb) + _dot(a_rb, u_st) + _dot(a_rk, v_st)
    uv = jnp.concatenate([u_st, v_st], axis=0)
    bkg = jnp.concatenate([stack(b_g), stack(k_g)], axis=0)
    st_ref[...] = state * jnp.exp(cs_end) + _dot_tn(uv, bkg)
    yield

    yn = _head_norm(yv, segm_ref[...], RWKV_GN_EPS) * ln_g + ln_b
    yn = yn + bonus * v
    o_ref[...] = yn * gate


def rwkv7(z, mu, lora_w, vec, consts, layer, B, S, G):
    C = RWKV_CHUNK
    nc = S // C
    tri, bdl, wide, segs, segm = consts
    const2 = lambda b, c: (0, 0)
    return pl.pallas_call(
        _rwkv_kernel,
        grid=(B // G, nc),
        in_specs=[
            pl.BlockSpec((G, C, 1024), lambda b, c: (b, c, OFF_RWKV // 1024)),
            pl.BlockSpec((None, 1, 1024), lambda b, c: (layer, 0, 0)),
            pl.BlockSpec((None, 2, 128, 768), lambda b, c: (layer, 0, 0, 0)),
            pl.BlockSpec((None, 8, 256), lambda b, c: (layer, 0, 0)),
            pl.BlockSpec((C, C), const2),
            pl.BlockSpec((4 * C, 4 * C), const2),
            pl.BlockSpec((2, C, 4 * C), lambda b, c: (0, 0, 0)),
            pl.BlockSpec((256, 256), const2),
            pl.BlockSpec((256, 256), const2),
        ],
        out_specs=pl.BlockSpec((G, C, 256), lambda b, c: (b, c, 0)),
        out_shape=jax.ShapeDtypeStruct((B, S, 256), F32),
        scratch_shapes=[pltpu.VMEM((G, 256, 256), F32), pltpu.VMEM((G, 8, 1024), F32)],
        compiler_params=_cparams(("parallel", "arbitrary")),
        name="rwkv7",
    )(z, mu, lora_w, vec, tri, bdl, wide, segs, segm)


def _merge_kernel(x_ref, zc_ref, on_ref, or_ref, ow_ref, cw_ref, gpre_ref, wg_ref, wb_ref, wo_ref, g_ref,
                  o_ref, carry_ref):
    tm = x_ref.shape[0]
    x = x_ref[...]
    h = _rms(x, gpre_ref[...]).astype(BF16)

    @pl.when(pl.program_id(1) == 0)
    def _():
        carry_ref[...] = jnp.zeros_like(carry_ref)

    zc = zc_ref[...]
    b_g = zc[:, 0:256]
    u = zc[:, 256:512] * zc[:, 512:768]
    prev = carry_ref[...]
    rows = lax.broadcasted_iota(jnp.int32, (tm, 256), 0)
    u1 = jnp.where(rows == 0, prev[7:8], pltpu.roll(u, 1, axis=0))
    u2 = jnp.where(rows == 0, prev[6:7], jnp.where(rows == 1, prev[7:8], pltpu.roll(u, 2, axis=0)))
    carry_ref[...] = u[tm - 8:tm]
    cw = cw_ref[...]
    o_conv = b_g * (cw[0:1] * u2 + cw[1:2] * u1 + cw[2:3] * u)

    branches = (on_ref[...], or_ref[...], ow_ref[...], o_conv)
    merged = jnp.zeros((tm, D_MODEL), F32)
    for m in range(N_BRANCHES):
        gm = jax.nn.sigmoid(_dot(h, wg_ref[:, m * D_MODEL:(m + 1) * D_MODEL]))
        merged = merged + gm * _dot(branches[m].astype(BF16), wb_ref[m])
    y = _dot(merged.astype(BF16), wo_ref[...])
    o_ref[...] = x + _rms(y, g_ref[...])


def merge_mix(x, z, o_nsa, o_ret, o_rwkv, conv_w, g_pre, w_gate, w_branch, w_out, g_post, layer, B, S, tm):
    nt = S // tm
    rowmap = lambda b, i: (b * nt + i, 0)
    lmap = lambda b, i: (layer, 0, 0)
    return pl.pallas_call(
        _merge_kernel,
        grid=(B, nt),
        in_specs=[
            pl.BlockSpec((tm, D_MODEL), rowmap),
            pl.BlockSpec((tm, 768), lambda b, i: (b * nt + i, OFF_CONV // 768)),
            pl.BlockSpec((tm, 256), rowmap),
            pl.BlockSpec((tm, 256), rowmap),
            pl.BlockSpec((tm, 256), rowmap),
            pl.BlockSpec((None, 8, 256), lmap),
            pl.BlockSpec((None, 1, D_MODEL), lmap),
            pl.BlockSpec((None, D_MODEL, N_BRANCHES * D_MODEL), lmap, pipeline_mode=pl.Buffered(1)),
            pl.BlockSpec((None, N_BRANCHES, MIX_WIDTH, D_MODEL), lambda b, i: (layer, 0, 0, 0),
                         pipeline_mode=pl.Buffered(1)),
            pl.BlockSpec((None, D_MODEL, D_MODEL), lmap, pipeline_mode=pl.Buffered(1)),
            pl.BlockSpec((None, 1, D_MODEL), lmap),
        ],
        out_specs=pl.BlockSpec((tm, D_MODEL), rowmap),
        out_shape=jax.ShapeDtypeStruct((B * S, D_MODEL), F32),
        scratch_shapes=[pltpu.VMEM((8, 256), F32)],
        compiler_params=_cparams(("parallel", "arbitrary")),
        name="merge_mix",
    )(x, z, o_nsa, o_ret, o_rwkv, conv_w, g_pre, w_gate, w_branch, w_out, g_post)


def _xattn_kernel(x_ref, kv_ref, gpre_ref, wq_ref, wo_ref, gpost_ref, o_ref):
    n_sub = 1
    rows = x_ref.shape[0] // n_sub

    def part(r0):
        rs = pl.ds(r0, rows)
        x = x_ref[rs, :]
        h = _rms(x, gpre_ref[...]).astype(BF16)
        yield
        q = (_dot(h, wq_ref[...]) * (XA_HEAD_DIM ** -0.5)).astype(BF16)
        yield
        outs = []
        for hd in range(XA_HEADS):
            cs = slice(hd * XA_HEAD_DIM, (hd + 1) * XA_HEAD_DIM)
            kh = kv_ref[:, hd * XA_HEAD_DIM:(hd + 1) * XA_HEAD_DIM]
            vh = kv_ref[:, D_MODEL + hd * XA_HEAD_DIM:D_MODEL + (hd + 1) * XA_HEAD_DIM]
            s = _dot_nt(q[:, cs], kh)
            yield
            e = jnp.exp(s - _rowmax(s))
            p = e / _rowsum(e)
            outs.append(_dot(p.astype(BF16), vh).astype(BF16))
            yield
        o = jnp.concatenate(outs, axis=1)
        y = _dot(o, wo_ref[...])
        yield
        o_ref[rs, :] = x + _rms(y, gpost_ref[...])

    _round_robin([part(i * rows) for i in range(n_sub)])


def cross_attention(x, kv, g_pre, wq, wo, g_post, layer, B, S, M, tm):
    nt = S // tm
    rowmap = lambda b, i: (b * nt + i, 0)
    lmap = lambda b, i: (layer, 0, 0)
    return pl.pallas_call(
        _xattn_kernel,
        grid=(B, nt),
        in_specs=[
            pl.BlockSpec((tm, D_MODEL), rowmap),
            pl.BlockSpec((M, 2 * D_MODEL), lambda b, i: (b, 0)),
            pl.BlockSpec((None, 1, D_MODEL), lmap),
            pl.BlockSpec((None, D_MODEL, D_MODEL), lmap),
            pl.BlockSpec((None, D_MODEL, D_MODEL), lmap),
            pl.BlockSpec((None, 1, D_MODEL), lmap),
        ],
        out_specs=pl.BlockSpec((tm, D_MODEL), rowmap),
        out_shape=jax.ShapeDtypeStruct((B * S, D_MODEL), F32),
        compiler_params=_cparams(("parallel", "arbitrary")),
        name="cross_attention",
    )(x, kv, g_pre, wq, wo, g_post)


def _mlp_kernel(x_ref, gpre_ref, w1_ref, w2_ref, gpost_ref, o_ref):
    x = x_ref[...]
    h = _rms(x, gpre_ref[...]).astype(BF16)
    tf = 1024
    acc = jnp.zeros(x.shape, F32)
    for f in range(D_FF // tf):
        a = jnp.maximum(_dot(h, w1_ref[:, f * tf:(f + 1) * tf]), 0.0)
        acc = acc + _dot((a * a).astype(BF16), w2_ref[f * tf:(f + 1) * tf, :])
    o_ref[...] = x + _rms(acc, gpost_ref[...])


def mlp(x, g_pre, w1, w2, g_post, layer, tm):
    M = x.shape[0]
    lmap = lambda i: (layer, 0, 0)
    return pl.pallas_call(
        _mlp_kernel,
        grid=(M // tm,),
        in_specs=[
            pl.BlockSpec((tm, D_MODEL), lambda i: (i, 0)),
            pl.BlockSpec((None, 1, D_MODEL), lmap),
            pl.BlockSpec((None, D_MODEL, D_FF), lmap, pipeline_mode=pl.Buffered(1)),
            pl.BlockSpec((None, D_FF, D_MODEL), lmap, pipeline_mode=pl.Buffered(1)),
            pl.BlockSpec((None, 1, D_MODEL), lmap),
        ],
        out_specs=pl.BlockSpec((tm, D_MODEL), lambda i: (i, 0)),
        out_shape=jax.ShapeDtypeStruct((M, D_MODEL), F32),
        compiler_params=_cparams(("parallel",)),
        name="mlp",
    )(x, g_pre, w1, w2, g_post)


def _t5_bucket(dist):
    n = jnp.maximum(dist, 0)
    max_exact = N_BUCKETS // 2
    nf = jnp.maximum(n, 1).astype(F32)
    large = max_exact + (jnp.log(nf / max_exact) / math.log(MAX_DISTANCE / max_exact)
                         * (N_BUCKETS - max_exact)).astype(jnp.int32)
    large = jnp.minimum(large, N_BUCKETS - 1)
    return jnp.where(n < max_exact, n, large)


def _nsa_tables(rel_bias, S):
    n_qb = S // Q_BLOCK
    n_cmp = S // NSA_CMP_STRIDE
    n_blk = S // NSA_SLC_LEN
    bias_f = rel_bias.astype(F32)
    i = jnp.arange(Q_BLOCK)
    per_qb = Q_BLOCK // NSA_CMP_STRIDE
    g = jnp.arange(n_cmp + per_qb * (n_qb - 1)) - per_qb * (n_qb - 1)
    d_c = i[None, :] - (g[:, None] * NSA_CMP_STRIDE + NSA_CMP_LEN - 1)
    def lookup(dist):
        hit = _t5_bucket(dist)[..., None, None] == jnp.arange(N_BUCKETS)[:, None]
        return jnp.sum(jnp.where(hit, bias_f, 0.0), axis=-2)

    bias_c = lookup(d_c).transpose(0, 2, 1).reshape(g.shape[0], MIX_HEADS * Q_BLOCK)
    d0 = i[:, None] - i[None, :]
    tab = jnp.stack([lookup(d0), lookup(d0 + Q_BLOCK)]) - bias_f[N_BUCKETS - 1]
    tab = tab.transpose(0, 2, 3, 1).reshape(2, Q_BLOCK, MIX_HEADS * Q_BLOCK)
    cs = np.arange(n_cmp)[None, :] * NSA_CMP_STRIDE
    ss = np.arange(n_blk)[:, None] * NSA_SLC_LEN
    ovt = np.clip(np.minimum(cs + NSA_CMP_LEN, ss + NSA_SLC_LEN) - np.maximum(cs, ss), 0, None)
    ovt = ovt.astype(np.float32) / NSA_CMP_LEN
    ovt[:, (S - NSA_CMP_LEN) // NSA_CMP_STRIDE + 1:] = 0.0
    ext = ((np.arange(S)[:, None] // NSA_SLC_LEN) == np.arange(n_blk)[None, :]).astype(np.float32)
    return bias_c * LOG2E, tab * LOG2E, jnp.asarray(ovt, BF16), jnp.asarray(ext, BF16)


def _ret_tables(S):
    L = RET_CHUNK
    H = MIX_HEADS
    half = HEAD_DIM // 2
    pos = jnp.arange(S)
    inv_freq = ROPE_BASE ** (-jnp.arange(half, dtype=F32) / half)
    ang = pos.astype(F32)[:, None] * inv_freq[None, :]
    cos = jnp.tile(jnp.cos(ang), (1, H))
    sin = jnp.tile(jnp.sin(ang), (1, H))
    lg = jnp.log(1.0 - 2.0 ** (-5.0 - jnp.arange(H, dtype=F32)))
    n = jnp.arange(L, dtype=F32)
    diff = n[:, None] - n[None, :]
    inner = jnp.where(diff >= 0, jnp.exp(jnp.maximum(diff, 0.0)[None] * lg[:, None, None]), 0.0)
    indec = inner.reshape(H * L, L)
    q_decay = jnp.exp((n + 1.0)[None, :] * lg[:, None])
    k_decay = jnp.exp((L - 1.0 - n)[None, :] * lg[:, None])
    chunk_decay = jnp.exp(L * lg)
    lane = np.arange(256)
    head_v = lane // HEAD_DIM
    head_qk = (lane % 128) // half
    qdec = q_decay.T[:, head_v]
    kdec = k_decay.T[:, head_qk]
    cdec = chunk_decay[head_v][None, :]
    bd = jnp.asarray((head_qk[:, None] == head_v[None, :]).astype(np.float32))
    segm = jnp.asarray((head_v[:, None] == head_v[None, :]).astype(np.float32) / HEAD_DIM, BF16)
    return cos, sin, indec, qdec, kdec, cdec, bd, segm


def _rwkv_consts():
    C = RWKV_CHUNK
    t = np.arange(C)
    tri = (t[:, None] >= t[None, :]).astype(np.float32)
    r = np.arange(4 * C)
    bdl = ((r[:, None] // C == r[None, :] // C) & (r[:, None] % C > r[None, :] % C)).astype(np.float32)
    strict = (t[:, None] > (r[None, :] % C)).astype(np.float32)
    incl = (t[:, None] >= (r[None, :] % C)).astype(np.float32)
    lane = np.arange(256) // HEAD_DIM
    seg = (lane[:, None] == lane[None, :]).astype(np.float32)
    return (jnp.asarray(tri, BF16), jnp.asarray(bdl), jnp.asarray(np.stack([strict, incl])),
            jnp.asarray(seg, BF16), jnp.asarray(seg / HEAD_DIM, BF16))


def _hi_lo(w):
    hi = w.astype(BF16)
    lo = (w - hi.astype(F32)).astype(BF16)
    return jnp.stack([hi, lo], axis=1)


def _pad_rows(w, top, total):
    return jnp.pad(w, ((0, 0), (top, total - top - w.shape[1]), (0, 0)))


def _layout_w_in(w_in):
    L = w_in.shape[0]
    o = 0
    nsa_q = w_in[:, :, o:o + 256]; o += 256
    nsa_kv = w_in[:, :, o:o + 384]; o += 384
    nsa_g = w_in[:, :, o:o + 12]; o += 12
    ret = w_in[:, :, o:o + 1024]; o += 1024
    rwkv = w_in[:, :, o:o + RWKV_COLS]; o += RWKV_COLS
    conv = w_in[:, :, o:o + 768]; o += 768
    gate = w_in[:, :, o:o + 4096]

    def rot_perm(w):
        half = HEAD_DIM // 2
        return [w[:, :, h * HEAD_DIM + p * half:h * HEAD_DIM + (p + 1) * half]
                for p in range(2) for h in range(MIX_HEADS)]

    ret = rot_perm(ret[:, :, 0:256]) + rot_perm(ret[:, :, 256:512]) + [ret[:, :, 512:]]
    zeros = lambda n: jnp.zeros((L, D_MODEL, n), w_in.dtype)
    out = jnp.concatenate([conv, nsa_q] + ret + [rwkv, zeros(1024 - RWKV_COLS), nsa_kv, nsa_g,
                                                 zeros(128 - 12)], axis=2)
    assert out.shape[2] == Z_COLS
    return out.astype(BF16), gate.astype(BF16)


def _layout_cmp(cmp_w, cmp_pe):
    L = cmp_w.shape[0]
    wk = cmp_w[:, 0]
    wv = cmp_w[:, 1]
    zero = jnp.zeros_like(wk)
    blk = jnp.concatenate([jnp.concatenate([wk, zero], axis=3), jnp.concatenate([zero, wv], axis=3)], axis=2)
    pe2 = jnp.concatenate([cmp_pe, cmp_pe], axis=2)[:, :, None, :]
    s = NSA_CMP_STRIDE
    return blk[:, :s].astype(BF16), blk[:, s:].astype(BF16), pe2[:, :s], pe2[:, s:]


def kernel(x, mem, ln_mix_pre, w_in, nsa_cmp_w, nsa_cmp_pe, rel_bias, ret_norm_g, rwkv_mu, rwkv_w0, rwkv_w2, rwkv_a0, rwkv_a2, rwkv_g2, rwkv_k_k, rwkv_k_a, rwkv_r_k, rwkv_ln_g, rwkv_ln_b, conv_w, w_branch, w_mix_out, ln_mix_post, ln_xa_pre, ln_mem, xa_wq, xa_wkv, xa_wo, ln_xa_post, ln_mlp_pre, mlp_w1, mlp_w2, ln_mlp_post):
    B, S, D = x.shape
    M = mem.shape[1]
    depth = w_in.shape[0]
    row = lambda g: g[:, None, :]

    w_in_b, w_gate_b = _layout_w_in(w_in)
    cmp_lo, cmp_hi, pe_lo, pe_hi = _layout_cmp(nsa_cmp_w, nsa_cmp_pe)
    nsa_tabs = _nsa_tables(rel_bias, S)
    ret_tabs = _ret_tables(S)
    rwkv_consts = _rwkv_consts()
    mu = jnp.pad(rwkv_mu, ((0, 0), (0, 1024 - RWKV_COLS)))[:, None, :]
    lora_w = _hi_lo(jnp.concatenate([_pad_rows(rwkv_w2, 0, 128), _pad_rows(rwkv_a2, RWKV_DECAY_LORA, 128),
                                     _pad_rows(rwkv_g2, RWKV_DECAY_LORA + RWKV_AAA_LORA, 128)], axis=2))
    vec = jnp.stack([rwkv_w0, rwkv_a0, rwkv_k_k, rwkv_k_a, rwkv_r_k, rwkv_ln_g, rwkv_ln_b,
                     jnp.zeros_like(rwkv_w0)], axis=1)
    conv_p = jnp.pad(conv_w, ((0, 0), (0, 8 - CONV_WIDTH), (0, 0)))
    w_branch_b = w_branch.astype(BF16)
    w_mix_out_b = w_mix_out.astype(BF16)
    xa_wq_b = xa_wq.astype(BF16)
    xa_wkv_b = xa_wkv.astype(BF16)
    xa_wo_b = xa_wo.astype(BF16)
    mlp_w1_b = mlp_w1.astype(BF16)
    mlp_w2_b = mlp_w2.astype(BF16)

    xf = x.reshape(B * S, D)
    memf = mem.reshape(B * M, D)
    for l in range(depth):
        z = norm_matmul(xf, row(ln_mix_pre), w_in_b, l, tm=512, tn=Z_COLS, out_dtype=F32)
        prep = nsa_compress(z, cmp_lo, cmp_hi, pe_lo, pe_hi, l, B, S)
        z3 = z.reshape(B, S, Z_COLS)
        o_nsa = nsa_attention(z3, prep, *nsa_tabs, B, S, G=8).reshape(B * S, MIX_WIDTH)
        o_ret = retention(z3, ret_tabs, row(ret_norm_g), l, B, S, G=4).reshape(B * S, MIX_WIDTH)
        o_rwkv = rwkv7(z3, mu, lora_w, vec, rwkv_consts, l, B, S, G=4).reshape(B * S, MIX_WIDTH)
        xf = merge_mix(xf, z, o_nsa, o_ret, o_rwkv, conv_p, row(ln_mix_pre), w_gate_b, w_branch_b,
                       w_mix_out_b, row(ln_mix_post), l, B, S, tm=512)
        kvm = norm_matmul(memf, row(ln_mem), xa_wkv_b, l, tm=2 * M, tn=2 * D_MODEL, out_dtype=BF16)
        xf = cross_attention(xf, kvm, row(ln_xa_pre), xa_wq_b, xa_wo_b, row(ln_xa_post), l, B, S, M, tm=1024)
        xf = mlp(xf, row(ln_mlp_pre), mlp_w1_b, mlp_w2_b, row(ln_mlp_post), l, tm=512)
    return xf.reshape(B, S, D)
```

```python
import functools
import math

import numpy as np
import jax
import jax.numpy as jnp
from jax import lax
from jax.experimental import pallas as pl
from jax.experimental.pallas import tpu as pltpu

F32 = jnp.float32
BF16 = jnp.bfloat16

D_MODEL = 1024
N_BRANCHES = 4
MIX_WIDTH = 256
HEAD_DIM = 64
MIX_HEADS = 4

NSA_CMP_LEN = 32
NSA_CMP_STRIDE = 16
NSA_SLC_LEN = 64
NSA_TOP_N = 8
NSA_WINDOW = 512
Q_BLOCK = 128
NSA_V_ROWS = HEAD_DIM + 16
FORCE_BONUS = 1e4
N_BUCKETS = 32
MAX_DISTANCE = 128

RET_CHUNK = 128
ROPE_BASE = 10000.0
RET_NORM_EPS = 1e-5

RWKV_DECAY_LORA = 32
RWKV_AAA_LORA = 32
RWKV_GATE_LORA = 64
RWKV_GN_EPS = 64e-5
RWKV_COLS = 3 * MIX_WIDTH + RWKV_DECAY_LORA + RWKV_AAA_LORA + RWKV_GATE_LORA
RWKV_CHUNK = 64

CONV_WIDTH = 3
XA_HEADS = 4
XA_HEAD_DIM = D_MODEL // XA_HEADS
D_FF = 4 * D_MODEL

RMS_EPS = 1e-6
LOG2E = math.log2(math.e)
NEG_INF = -1e30
NEG_BIG = -3e38

OFF_CONV = 0
OFF_NSAQ = 768
OFF_RET = 1024
OFF_RWKV = 2048
OFF_NSAKV = 3072
OFF_NSAG = 3456
Z_COLS = 3584

VMEM_LIMIT = 56 * 1024 * 1024


def _cparams(sem):
    return pltpu.CompilerParams(dimension_semantics=sem, vmem_limit_bytes=VMEM_LIMIT)


def _dot(a, b):
    return jnp.dot(a, b, preferred_element_type=F32)


def _dot_nt(a, b):
    return lax.dot_general(a, b, (((1,), (1,)), ((), ())), preferred_element_type=F32)


def _dot_tn(a, b):
    return lax.dot_general(a, b, (((0,), (0,)), ((), ())), preferred_element_type=F32)


def _split2(x):
    hi = x.astype(BF16)
    lo = (x - hi.astype(F32)).astype(BF16)
    return hi, lo


def _dot_x2(x, w_bf16, *more):
    xs = (x,) + more
    rows = x.shape[0]
    parts = [p for xi in xs for p in _split2(xi)]
    out = _dot(jnp.concatenate(parts, axis=0), w_bf16)
    res = [out[2 * i * rows:(2 * i + 1) * rows] + out[(2 * i + 1) * rows:(2 * i + 2) * rows]
           for i in range(len(xs))]
    return res[0] if not more else res


def _rms(x, g):
    ms = jnp.mean(x * x, axis=-1, keepdims=True)
    return x * lax.rsqrt(ms + RMS_EPS) * g


def _norm_matmul_kernel(x_ref, g_ref, w_ref, o_ref, h_ref):
    @pl.when(pl.program_id(1) == 0)
    def _():
        h_ref[...] = _rms(x_ref[...], g_ref[...]).astype(BF16)

    o_ref[...] = _dot(h_ref[...], w_ref[...]).astype(o_ref.dtype)


def norm_matmul(x, g, w, layer, tm, tn, out_dtype):
    M, D = x.shape
    N = w.shape[2]
    w_mode = dict(pipeline_mode=pl.Buffered(1)) if tn == N else {}
    return pl.pallas_call(
        _norm_matmul_kernel,
        grid=(M // tm, N // tn),
        in_specs=[
            pl.BlockSpec((tm, D), lambda i, j: (i, 0)),
            pl.BlockSpec((None, 1, D), lambda i, j: (layer, 0, 0)),
            pl.BlockSpec((None, D, tn), lambda i, j: (layer, 0, j), **w_mode),
        ],
        out_specs=pl.BlockSpec((tm, tn), lambda i, j: (i, j)),
        out_shape=jax.ShapeDtypeStruct((M, N), out_dtype),
        scratch_shapes=[pltpu.VMEM((tm, D), BF16)],
        compiler_params=_cparams(("parallel", "arbitrary")),
        name="norm_matmul",
    )(x, g, w)


def _nsa_compress_kernel(kvc_ref, kvs_ref, kvw_ref, wlo_ref, whi_ref, pelo_ref, pehi_ref,
                         kc_ref, vct_ref, ks_ref, vst_ref, kw_ref, vwt_ref):
    nblk = kc_ref.shape[0]
    n_kt = vst_ref.shape[0]
    ylo = jnp.zeros((nblk, 128), F32)
    yhi = jnp.zeros((nblk, 128), F32)
    for r in range(NSA_CMP_STRIDE):
        xr = kvc_ref[pl.ds(r, nblk, stride=NSA_CMP_STRIDE), :]
        ylo = ylo + _dot((xr + pelo_ref[r]).astype(BF16), wlo_ref[r])
        yhi = yhi + _dot((xr + pehi_ref[r]).astype(BF16), whi_ref[r])
    y = ylo + pltpu.roll(yhi, nblk - 1, axis=0)
    kc_ref[...] = y.astype(BF16)
    vct_ref[...] = y.T[HEAD_DIM:, :].astype(BF16)
    ks_ref[...] = kvs_ref[...].astype(BF16)
    kw_ref[...] = kvw_ref[...].astype(BF16)
    ones = jnp.ones((NSA_V_ROWS - HEAD_DIM, 128), BF16)
    for kt in range(n_kt):
        rs = slice(kt * 128, (kt + 1) * 128)
        vst_ref[kt] = jnp.concatenate([kvs_ref[rs, :].T[HEAD_DIM:, :].astype(BF16), ones], axis=0)
        vwt_ref[kt] = jnp.concatenate([kvw_ref[rs, :].T[HEAD_DIM:, :].astype(BF16), ones], axis=0)


def nsa_compress(z, wlo, whi, pelo, pehi, layer, B, S):
    nblk = S // NSA_CMP_STRIDE
    n_kt = S // 128
    c0 = OFF_NSAKV // 128
    wmap = lambda b: (layer, 0, 0, 0)
    b3 = lambda b: (b, 0, 0)
    b4 = lambda b: (b, 0, 0, 0)
    return pl.pallas_call(
        _nsa_compress_kernel,
        grid=(B,),
        in_specs=[
            pl.BlockSpec((S, 128), lambda b: (b, c0)),
            pl.BlockSpec((S, 128), lambda b: (b, c0 + 1)),
            pl.BlockSpec((S, 128), lambda b: (b, c0 + 2)),
            pl.BlockSpec((None, NSA_CMP_STRIDE, 128, 128), wmap),
            pl.BlockSpec((None, NSA_CMP_STRIDE, 128, 128), wmap),
            pl.BlockSpec((None, NSA_CMP_STRIDE, 1, 128), wmap),
            pl.BlockSpec((None, NSA_CMP_STRIDE, 1, 128), wmap),
        ],
        out_specs=[
            pl.BlockSpec((None, nblk, 128), b3),
            pl.BlockSpec((None, HEAD_DIM, nblk), b3),
            pl.BlockSpec((None, S, 128), b3),
            pl.BlockSpec((None, n_kt, NSA_V_ROWS, 128), b4),
            pl.BlockSpec((None, S, 128), b3),
            pl.BlockSpec((None, n_kt, NSA_V_ROWS, 128), b4),
        ],
        out_shape=[
            jax.ShapeDtypeStruct((B, nblk, 128), BF16),
            jax.ShapeDtypeStruct((B, HEAD_DIM, nblk), BF16),
            jax.ShapeDtypeStruct((B, S, 128), BF16),
            jax.ShapeDtypeStruct((B, n_kt, NSA_V_ROWS, 128), BF16),
            jax.ShapeDtypeStruct((B, S, 128), BF16),
            jax.ShapeDtypeStruct((B, n_kt, NSA_V_ROWS, 128), BF16),
        ],
        compiler_params=_cparams(("parallel",)),
        name="nsa_compress",
    )(z, z, z, wlo, whi, pelo, pehi)


def _rowmax(s):
    return jnp.max(s, axis=-1, keepdims=True)


def _rowsum(s):
    return jnp.sum(s, axis=-1, keepdims=True)


def _colmax(s):
    return jnp.max(s, axis=0, keepdims=True)


def _colsum(s):
    return jnp.sum(s, axis=0, keepdims=True)


def _nsa_kernel(q_ref, gl_ref, kc_ref, vct_ref, ks_ref, vst_ref, kw_ref, vwt_ref, bc_ref, tab_ref,
                ovt_ref, o_ref, selx_ref):
    bi = pl.program_id(1)
    G = q_ref.shape[0]
    n_kt = selx_ref.shape[1]
    n_blk = ovt_ref.shape[0]
    n_cmp = kc_ref.shape[1]
    QB = Q_BLOCK
    R = MIX_HEADS * QB
    n_off = NSA_WINDOW // 128

    lo = jnp.maximum(bi - n_off, 0)
    q4 = [None] * G
    o_c = [None] * G
    carry_s = [None] * G
    carry_w = [None] * G

    def tile_qk(g, kref, kt):
        start = pl.multiple_of(kt * 128, 128)
        return _dot(kref[g, pl.ds(start, 128), :HEAD_DIM], q4[g])

    def head_cols(x, h):
        return x[:, h * QB:(h + 1) * QB]

    key1 = lax.broadcasted_iota(jnp.int32, (128, QB), 0)
    i1 = lax.broadcasted_iota(jnp.int32, (128, QB), 1)
    causal1 = key1 <= i1
    kp = jnp.maximum(bi - 1, 0)
    pen_prev = jnp.where(bi >= 1, 0.0, NEG_INF)

    def win_add(kt, valid):
        shift = jnp.where(valid, jnp.where(bi - kt < n_off, -4096, 0), 4096)
        return jnp.where(key1 > i1 + shift, 0.0, NEG_INF)

    def diag_add(extra):
        return [jnp.where(causal1, head_cols(tab_ref[0], h) + extra, NEG_INF) for h in range(MIX_HEADS)]

    def prev_add(extra):
        return [head_cols(tab_ref[1], h) + extra for h in range(MIX_HEADS)]

    def pair_update(carry, qk0, add0, vt0, qk1, add1, vt1):
        p0s, p1s, ms, alphas = [], [], [], []
        for h in range(MIX_HEADS):
            a0 = head_cols(qk0, h) + add0[h]
            a1 = head_cols(qk1, h) + add1[h]
            m_new = jnp.maximum(_colmax(a0), _colmax(a1))
            if carry is not None:
                m_old = head_cols(carry[0], h)
                m_new = jnp.maximum(m_old, m_new)
                alphas.append(jnp.exp2(m_old - m_new))
            ms.append(m_new)
            p0s.append(jnp.exp2(a0 - m_new).astype(BF16))
            p1s.append(jnp.exp2(a1 - m_new).astype(BF16))
        yield
        acc = _dot(vt0, jnp.concatenate(p0s, axis=1)) + _dot(vt1, jnp.concatenate(p1s, axis=1))
        if carry is not None:
            acc = jnp.concatenate(alphas, axis=1) * carry[1] + acc
        return jnp.concatenate(ms, axis=1), acc

    def head_part(g):
        qt = (q_ref[g] * (HEAD_DIM ** -0.5 * LOG2E)).T
        q4[g] = jnp.concatenate([qt[h * HEAD_DIM:(h + 1) * HEAD_DIM] for h in range(MIX_HEADS)],
                                axis=1).astype(BF16)
        yield
        carry_w[g] = yield from pair_update(None, tile_qk(g, kw_ref, bi), diag_add(0.0), vwt_ref[g, bi],
                                            tile_qk(g, kw_ref, kp), prev_add(pen_prev), vwt_ref[g, kp])
        yield
        g0 = pl.multiple_of((n_kt - 1 - bi) * 8, 8)
        qk = _dot(kc_ref[g, :, :HEAD_DIM], q4[g])
        nrow = lax.broadcasted_iota(jnp.int32, (n_cmp, QB), 0)
        tq = bi * QB + lax.broadcasted_iota(jnp.int32, (n_cmp, QB), 1)
        vis = tq - (nrow * NSA_CMP_STRIDE + (NSA_CMP_LEN - 1)) >= 0
        live = (tq >= NSA_CMP_LEN - 1).astype(F32)
        yield
        p_cs = []
        for h in range(MIX_HEADS):
            s = jnp.where(vis, head_cols(qk, h) + bc_ref[pl.ds(g0, n_cmp), h * QB:(h + 1) * QB], NEG_INF)
            e = jnp.exp2(s - _colmax(s))
            p_cs.append(e / _colsum(e) * live)
        yield
        o_c[g] = _dot(vct_ref[g], jnp.concatenate(p_cs, axis=1).astype(BF16))
        p_sum = p_cs[0] + p_cs[1] + p_cs[2] + p_cs[3]
        p_hi, p_lo = _split2(p_sum)
        imp = _dot(ovt_ref[...], p_hi) + _dot(ovt_ref[...], p_lo)
        yield
        blk = lax.broadcasted_iota(jnp.int32, (n_blk, QB), 0)
        cur = (bi * QB + lax.broadcasted_iota(jnp.int32, (n_blk, QB), 1)) >> 6
        forced = (blk == 0) | (blk == cur) | (blk == cur - 1)
        imp = jnp.where(forced, imp + FORCE_BONUS, imp)
        imp = jnp.where(blk <= cur, imp, NEG_INF)
        blk_f = blk.astype(F32)
        sel = jnp.zeros((n_blk, QB), F32)
        for _ in range(NSA_TOP_N):
            mx = _colmax(imp)
            idx = jnp.min(jnp.where(imp == mx, blk_f, 4096.0), axis=0, keepdims=True)
            pick = blk_f == idx
            sel = jnp.where(pick, 1.0, sel)
            imp = jnp.where(pick, NEG_BIG, imp)
            yield
        sel_add = (sel - 1.0) * (-NEG_INF)
        per_tile = 128 // NSA_SLC_LEN
        for kt in range(n_kt):
            selx_ref[g, kt] = jnp.concatenate(
                [jnp.broadcast_to(sel_add[kt * per_tile + r:kt * per_tile + r + 1], (NSA_SLC_LEN, QB))
                 for r in range(per_tile)], axis=0)
        yield
        carry_s[g] = yield from pair_update(
            None, tile_qk(g, ks_ref, bi), diag_add(selx_ref[g, bi]), vst_ref[g, bi],
            tile_qk(g, ks_ref, kp), prev_add(selx_ref[g, kp] + pen_prev), vst_ref[g, kp])

    _round_robin([head_part(g) for g in range(G)])

    n_old = jnp.maximum(bi - 1, 0)

    def sel_step(g, j, carry, out):
        k0 = 2 * j
        k1 = jnp.minimum(k0 + 1, bi)
        pen1 = jnp.where(k0 + 1 < n_old, 0.0, NEG_INF)
        qk0 = tile_qk(g, ks_ref, k0)
        qk1 = tile_qk(g, ks_ref, k1)
        yield
        out[g] = yield from pair_update(carry, qk0, [selx_ref[g, k0]] * MIX_HEADS, vst_ref[g, k0],
                                        qk1, [selx_ref[g, k1] + pen1] * MIX_HEADS, vst_ref[g, k1])

    def win_step(g, j, carry, out):
        k0 = lo + 2 * j
        k1 = jnp.minimum(k0 + 1, bi)
        qk0 = tile_qk(g, kw_ref, k0)
        qk1 = tile_qk(g, kw_ref, k1)
        yield
        out[g] = yield from pair_update(carry, qk0, [win_add(k0, True)] * MIX_HEADS, vwt_ref[g, k0],
                                        qk1, [win_add(k1, k0 + 1 < n_old)] * MIX_HEADS, vwt_ref[g, k1])

    def joint(step):
        def body(j, carries):
            out = [None] * G
            _round_robin([step(g, j, carries[g], out) for g in range(G)])
            return tuple(out)
        return body

    res_s = lax.fori_loop(0, (n_old + 1) // 2, joint(sel_step), tuple(carry_s))
    res_w = lax.fori_loop(0, (n_old - lo + 1) // 2, joint(win_step), tuple(carry_w))

    def tail_part(g):
        acc_s = res_s[g][1]
        acc_w = res_w[g][1]
        o_s = acc_s[:HEAD_DIM] * (1.0 / acc_s[HEAD_DIM:HEAD_DIM + 1])
        o_w = acc_w[:HEAD_DIM] * (1.0 / acc_w[HEAD_DIM:HEAD_DIM + 1])
        gate = jax.nn.sigmoid(gl_ref[g].T)
        yield
        outs = []
        for h in range(MIX_HEADS):
            cs = slice(h * QB, (h + 1) * QB)
            outs.append(gate[3 * h:3 * h + 1] * o_c[g][:, cs] + gate[3 * h + 1:3 * h + 2] * o_s[:, cs]
                        + gate[3 * h + 2:3 * h + 3] * o_w[:, cs])
        o_ref[g] = jnp.concatenate(outs, axis=0).T

    _round_robin([tail_part(g) for g in range(G)])


def nsa_attention(z, prep, bias_c, tab, ovt, B, S, G):
    n_qb = S // Q_BLOCK
    n_kt = S // 128
    n_cmp = S // NSA_CMP_STRIDE
    R = MIX_HEADS * Q_BLOCK
    kc, vct, ks, vst, kw, vwt = prep
    b3 = lambda b, i: (b, 0, 0)
    b4 = lambda b, i: (b, 0, 0, 0)
    return pl.pallas_call(
        _nsa_kernel,
        grid=(B // G, n_qb),
        in_specs=[
            pl.BlockSpec((G, Q_BLOCK, MIX_WIDTH), lambda b, i: (b, i, OFF_NSAQ // MIX_WIDTH)),
            pl.BlockSpec((G, Q_BLOCK, 128), lambda b, i: (b, i, OFF_NSAG // 128)),
            pl.BlockSpec((G, n_cmp, 128), b3),
            pl.BlockSpec((G, HEAD_DIM, n_cmp), b3),
            pl.BlockSpec((G, S, 128), b3),
            pl.BlockSpec((G, n_kt, NSA_V_ROWS, 128), b4),
            pl.BlockSpec((G, S, 128), b3),
            pl.BlockSpec((G, n_kt, NSA_V_ROWS, 128), b4),
            pl.BlockSpec(bias_c.shape, lambda b, i: (0, 0)),
            pl.BlockSpec((2, 128, R), lambda b, i: (0, 0, 0)),
            pl.BlockSpec(ovt.shape, lambda b, i: (0, 0)),
        ],
        out_specs=pl.BlockSpec((G, Q_BLOCK, MIX_WIDTH), lambda b, i: (b, i, 0)),
        out_shape=jax.ShapeDtypeStruct((B, S, MIX_WIDTH), F32),
        scratch_shapes=[pltpu.VMEM((G, n_kt, 128, Q_BLOCK), F32)],
        compiler_params=_cparams(("parallel", "arbitrary")),
        name="nsa_attention",
    )(z, z, kc, vct, ks, vst, kw, vwt, bias_c, tab, ovt)


def _stack_heads(x, head_of_lane):
    return jnp.concatenate([jnp.where(head_of_lane == h, x, jnp.zeros_like(x))
                            for h in range(MIX_HEADS)], axis=0)


def _head_norm(y, seg_mean, eps):
    mu = _dot_x2(y, seg_mean)
    d = y - mu
    var = _dot_x2(d * d, seg_mean)
    return d * lax.rsqrt(var + eps)


def _round_robin(chains):
    while chains:
        chains = [c for c in chains if next(c, StopIteration) is not StopIteration]


def _retention_kernel(z_ref, cos_ref, sin_ref, indec_ref, qdec_ref, kdec_ref, cdec_ref, bd_ref,
                      segm_ref, ng_ref, o_ref, st_ref):
    @pl.when(pl.program_id(1) == 0)
    def _():
        st_ref[...] = jnp.zeros_like(st_ref)

    _round_robin([_retention_chunk(z_ref.at[g], cos_ref, sin_ref, indec_ref, qdec_ref, kdec_ref, cdec_ref,
                                   bd_ref, segm_ref, ng_ref, o_ref.at[g], st_ref.at[g])
                  for g in range(z_ref.shape[0])])


def _retention_chunk(z_ref, cos_ref, sin_ref, indec_ref, qdec_ref, kdec_ref, cdec_ref, bd_ref,
                     segm_ref, ng_ref, o_ref, st_ref):
    L = RET_CHUNK
    z = z_ref[...]
    q = z[:, 0:256]
    k = z[:, 256:512]
    v = z[:, 512:768]
    g = z[:, 768:1024]
    cos = cos_ref[...]
    sin = sin_ref[...]

    def rot(u):
        u1 = u[:, :128]
        u2 = u[:, 128:]
        return jnp.concatenate([u1 * cos - u2 * sin, u2 * cos + u1 * sin], axis=1)

    qr = rot(q) * (HEAD_DIM ** -0.5)
    kr = rot(k)
    lane = lax.broadcasted_iota(jnp.int32, (L, 256), 1)
    head_qk = (lane & 127) >> 5
    head_v = lane >> 6
    qb = qr.astype(BF16)
    kb = kr.astype(BF16)
    vb = v.astype(BF16)
    yield

    att = _dot_nt(_stack_heads(qb, head_qk), kb) * indec_ref[...]
    yield
    o_st = _dot(att.astype(BF16), vb)
    o = jnp.zeros((L, 256), F32)
    for h in range(MIX_HEADS):
        o = o + jnp.where(head_v == h, o_st[h * L:(h + 1) * L], 0.0)
    yield
    state = st_ref[...]
    o = o + _dot(qb, state.astype(BF16)) * qdec_ref[...]
    st_ref[...] = state * cdec_ref[...] + _dot_tn((kr * kdec_ref[...]).astype(BF16), vb) * bd_ref[...]
    yield

    mu = _dot_x2(o, segm_ref[...])
    yield
    d = o - mu
    var = _dot_x2(d * d, segm_ref[...])
    yield
    o_ref[...] = g * jax.nn.sigmoid(g) * (d * lax.rsqrt(var + RET_NORM_EPS) * ng_ref[...])


def retention(z, tabs, ng, layer, B, S, G):
    L = RET_CHUNK
    nc = S // L
    cos, sin, indec, qdec, kdec, cdec, bd, segm = tabs
    const2 = lambda b, c: (0, 0)
    return pl.pallas_call(
        _retention_kernel,
        grid=(B // G, nc),
        in_specs=[
            pl.BlockSpec((G, L, 1024), lambda b, c: (b, c, OFF_RET // 1024)),
            pl.BlockSpec((L, 128), lambda b, c: (c, 0)),
            pl.BlockSpec((L, 128), lambda b, c: (c, 0)),
            pl.BlockSpec((MIX_HEADS * L, L), const2),
            pl.BlockSpec((L, 256), const2),
            pl.BlockSpec((L, 256), const2),
            pl.BlockSpec((1, 256), const2),
            pl.BlockSpec((256, 256), const2),
            pl.BlockSpec((256, 256), const2),
            pl.BlockSpec((None, 1, 256), lambda b, c: (layer, 0, 0)),
        ],
        out_specs=pl.BlockSpec((G, L, 256), lambda b, c: (b, c, 0)),
        out_shape=jax.ShapeDtypeStruct((B, S, 256), F32),
        scratch_shapes=[pltpu.VMEM((G, 256, 256), F32)],
        compiler_params=_cparams(("parallel", "arbitrary")),
        name="retention",
    )(z, cos, sin, indec, qdec, kdec, cdec, bd, segm, ng)


def _rwkv_kernel(z_ref, mu_ref, lw_ref, vec_ref, tri_ref, bdl_ref, wide_ref,
                 segs_ref, segm_ref, o_ref, st_ref, prev_ref):
    @pl.when(pl.program_id(1) == 0)
    def _():
        st_ref[...] = jnp.zeros_like(st_ref)
        prev_ref[...] = jnp.zeros_like(prev_ref)

    _round_robin([_rwkv_chunk(z_ref.at[g], mu_ref, lw_ref, vec_ref, tri_ref, bdl_ref,
                              wide_ref, segs_ref, segm_ref, o_ref.at[g], st_ref.at[g], prev_ref.at[g])
                  for g in range(z_ref.shape[0])])


def _rwkv_chunk(z_ref, mu_ref, lw_ref, vec_ref, tri_ref, bdl_ref, wide_ref,
                segs_ref, segm_ref, o_ref, st_ref, prev_ref):
    C = RWKV_CHUNK
    z = z_ref[...]
    rows = lax.broadcasted_iota(jnp.int32, (C, 1024), 0)
    zs = jnp.where(rows == 0, prev_ref[0:1, :], pltpu.roll(z, 1, axis=0))
    prev_ref[...] = jnp.broadcast_to(z[C - 1:C, :], prev_ref.shape)
    zf = z + (zs - z) * mu_ref[...]
    r = zf[:, 0:256]
    k = zf[:, 256:512]
    v = zf[:, 512:768]
    lora = zf[:, 768:896]

    vec = vec_ref[...]
    w0, a0, k_k, k_a, r_k, ln_g, ln_b = (vec[i:i + 1] for i in range(7))

    lane = lax.broadcasted_iota(jnp.int32, lora.shape, 1)
    lora_in = jnp.where(lane < RWKV_DECAY_LORA, jnp.tanh(lora),
                        jnp.where(lane < RWKV_DECAY_LORA + RWKV_AAA_LORA, lora, jax.nn.sigmoid(lora)))
    hi, lo = _split2(lora_in)
    both = _dot(jnp.concatenate([hi, lo], axis=0), lw_ref[0])
    lora_out = both[:C] + both[C:] + _dot(hi, lw_ref[1])
    wpre = w0 + lora_out[:, 0:256]
    y = -wpre
    softplus = jnp.maximum(y, 0.0) + jnp.log(1.0 + jnp.exp(-jnp.abs(y)))
    w_log = -softplus - 0.5
    ld = -jnp.exp(w_log)
    a = jax.nn.sigmoid(a0 + lora_out[:, 256:512])
    gate = lora_out[:, 512:768]
    kk = k * k_k
    k2 = k * (1.0 + (a - 1.0) * k_a)
    kk_sq, bonus = _dot_x2(kk * kk, segs_ref[...], r * k2 * r_k)
    kk = kk / jnp.maximum(jnp.sqrt(kk_sq), 1e-12)
    yield

    tri = tri_ref[...]
    l1 = ld.astype(BF16)
    r1 = ld - l1.astype(F32)
    l2 = r1.astype(BF16)
    l3 = (r1 - l2.astype(F32)).astype(BF16)
    cs = _dot(tri, l1) + _dot(tri, l2) + _dot(tri, l3)
    cs_end = cs[C - 1:C, :]
    yield
    e_neg = jnp.exp(-cs)
    e_end = jnp.exp(cs_end - cs)
    kka = kk * a
    a_t = -kk * jnp.exp(cs - ld)
    r_t = r * jnp.exp(cs)
    b_t = kka * e_neg
    k_t = k2 * e_neg
    b_g = kka * e_end
    k_g = k2 * e_end

    lane = lax.broadcasted_iota(jnp.int32, (C, 256), 1)
    head = lane >> 6
    stack = lambda x: _stack_heads(x.astype(BF16), head)
    a_st, b_st, k_st, v_st = stack(a_t), stack(b_t), stack(k_t), stack(v)

    n_pow = _dot_nt(a_st, b_st) * bdl_ref[...]
    ri = lax.broadcasted_iota(jnp.int32, (4 * C, 4 * C), 0)
    ci = lax.broadcasted_iota(jnp.int32, (4 * C, 4 * C), 1)
    t_inv = jnp.where(ri == ci, 1.0, 0.0) + n_pow
    yield

    ar = jnp.concatenate([a_t, r_t], axis=0).astype(BF16)
    bk = jnp.concatenate([b_st, k_st], axis=0)
    wide = _dot_nt(ar, bk)
    strict = wide_ref[0]
    incl = wide_ref[1]
    a_ak = (wide[0:C, 4 * C:] * strict).astype(BF16)
    a_rb = (wide[C:, 0:4 * C] * incl).astype(BF16)
    a_rk = (wide[C:, 4 * C:] * incl).astype(BF16)
    yield

    sq = 2
    while sq < C:
        nb = n_pow.astype(BF16)
        n_pow = _dot(nb, nb)
        yield
        t_inv = t_inv + _dot(t_inv.astype(BF16), n_pow.astype(BF16))
        sq *= 2
    yield

    state = st_ref[...]
    sb = state.astype(BF16)
    x = _dot_nt(a_t.astype(BF16), sb) + _dot(a_ak, v_st)
    yield
    u_st = _dot(t_inv.astype(BF16), stack(x)).astype(BF16)
    yield
    yv = _dot_nt(r_t.astype(BF16), sb) + _dot(a_rb, u_st) + _dot(a_rk, v_st)
    uv = jnp.concatenate([u_st, v_st], axis=0)
    bkg = jnp.concatenate([stack(b_g), stack(k_g)], axis=0)
    st_ref[...] = state * jnp.exp(cs_end) + _dot_tn(uv, bkg)
    yield

    yn = _head_norm(yv, segm_ref[...], RWKV_GN_EPS) * ln_g + ln_b
    yn = yn + bonus * v
    o_ref[...] = yn * gate


def rwkv7(z, mu, lora_w, vec, consts, layer, B, S, G):
    C = RWKV_CHUNK
    nc = S // C
    tri, bdl, wide, segs, segm = consts
    const2 = lambda b, c: (0, 0)
    return pl.pallas_call(
        _rwkv_kernel,
        grid=(B // G, nc),
        in_specs=[
            pl.BlockSpec((G, C, 1024), lambda b, c: (b, c, OFF_RWKV // 1024)),
            pl.BlockSpec((None, 1, 1024), lambda b, c: (layer, 0, 0)),
            pl.BlockSpec((None, 2, 128, 768), lambda b, c: (layer, 0, 0, 0)),
            pl.BlockSpec((None, 8, 256), lambda b, c: (layer, 0, 0)),
            pl.BlockSpec((C, C), const2),
            pl.BlockSpec((4 * C, 4 * C), const2),
            pl.BlockSpec((2, C, 4 * C), lambda b, c: (0, 0, 0)),
            pl.BlockSpec((256, 256), const2),
            pl.BlockSpec((256, 256), const2),
        ],
        out_specs=pl.BlockSpec((G, C, 256), lambda b, c: (b, c, 0)),
        out_shape=jax.ShapeDtypeStruct((B, S, 256), F32),
        scratch_shapes=[pltpu.VMEM((G, 256, 256), F32), pltpu.VMEM((G, 8, 1024), F32)],
        compiler_params=_cparams(("parallel", "arbitrary")),
        name="rwkv7",
    )(z, mu, lora_w, vec, tri, bdl, wide, segs, segm)


def _merge_kernel(x_ref, zc_ref, on_ref, or_ref, ow_ref, cw_ref, gpre_ref, wg_ref, wb_ref, wo_ref, g_ref,
                  o_ref, carry_ref):
    tm = x_ref.shape[0]
    x = x_ref[...]
    h = _rms(x, gpre_ref[...]).astype(BF16)

    @pl.when(pl.program_id(1) == 0)
    def _():
        carry_ref[...] = jnp.zeros_like(carry_ref)

    zc = zc_ref[...]
    b_g = zc[:, 0:256]
    u = zc[:, 256:512] * zc[:, 512:768]
    prev = carry_ref[...]
    rows = lax.broadcasted_iota(jnp.int32, (tm, 256), 0)
    u1 = jnp.where(rows == 0, prev[7:8], pltpu.roll(u, 1, axis=0))
    u2 = jnp.where(rows == 0, prev[6:7], jnp.where(rows == 1, prev[7:8], pltpu.roll(u, 2, axis=0)))
    carry_ref[...] = u[tm - 8:tm]
    cw = cw_ref[...]
    o_conv = b_g * (cw[0:1] * u2 + cw[1:2] * u1 + cw[2:3] * u)

    branches = (on_ref[...], or_ref[...], ow_ref[...], o_conv)
    merged = jnp.zeros((tm, D_MODEL), F32)
    for m in range(N_BRANCHES):
        gm = jax.nn.sigmoid(_dot(h, wg_ref[:, m * D_MODEL:(m + 1) * D_MODEL]))
        merged = merged + gm * _dot(branches[m].astype(BF16), wb_ref[m])
    y = _dot(merged.astype(BF16), wo_ref[...])
    o_ref[...] = x + _rms(y, g_ref[...])


def merge_mix(x, z, o_nsa, o_ret, o_rwkv, conv_w, g_pre, w_gate, w_branch, w_out, g_post, layer, B, S, tm):
    nt = S // tm
    rowmap = lambda b, i: (b * nt + i, 0)
    lmap = lambda b, i: (layer, 0, 0)
    return pl.pallas_call(
        _merge_kernel,
        grid=(B, nt),
        in_specs=[
            pl.BlockSpec((tm, D_MODEL), rowmap),
            pl.BlockSpec((tm, 768), lambda b, i: (b * nt + i, OFF_CONV // 768)),
            pl.BlockSpec((tm, 256), rowmap),
            pl.BlockSpec((tm, 256), rowmap),
            pl.BlockSpec((tm, 256), rowmap),
            pl.BlockSpec((None, 8, 256), lmap),
            pl.BlockSpec((None, 1, D_MODEL), lmap),
            pl.BlockSpec((None, D_MODEL, N_BRANCHES * D_MODEL), lmap, pipeline_mode=pl.Buffered(1)),
            pl.BlockSpec((None, N_BRANCHES, MIX_WIDTH, D_MODEL), lambda b, i: (layer, 0, 0, 0),
                         pipeline_mode=pl.Buffered(1)),
            pl.BlockSpec((None, D_MODEL, D_MODEL), lmap, pipeline_mode=pl.Buffered(1)),
            pl.BlockSpec((None, 1, D_MODEL), lmap),
        ],
        out_specs=pl.BlockSpec((tm, D_MODEL), rowmap),
        out_shape=jax.ShapeDtypeStruct((B * S, D_MODEL), F32),
        scratch_shapes=[pltpu.VMEM((8, 256), F32)],
        compiler_params=_cparams(("parallel", "arbitrary")),
        name="merge_mix",
    )(x, z, o_nsa, o_ret, o_rwkv, conv_w, g_pre, w_gate, w_branch, w_out, g_post)


def _xattn_kernel(x_ref, kv_ref, gpre_ref, wq_ref, wo_ref, gpost_ref, o_ref):
    n_sub = 1
    rows = x_ref.shape[0] // n_sub

    def part(r0):
        rs = pl.ds(r0, rows)
        x = x_ref[rs, :]
        h = _rms(x, gpre_ref[...]).astype(BF16)
        yield
        q = (_dot(h, wq_ref[...]) * (XA_HEAD_DIM ** -0.5)).astype(BF16)
        yield
        outs = []
        for hd in range(XA_HEADS):
            cs = slice(hd * XA_HEAD_DIM, (hd + 1) * XA_HEAD_DIM)
            kh = kv_ref[:, hd * XA_HEAD_DIM:(hd + 1) * XA_HEAD_DIM]
            vh = kv_ref[:, D_MODEL + hd * XA_HEAD_DIM:D_MODEL + (hd + 1) * XA_HEAD_DIM]
            s = _dot_nt(q[:, cs], kh)
            yield
            e = jnp.exp(s - _rowmax(s))
            p = e / _rowsum(e)
            outs.append(_dot(p.astype(BF16), vh).astype(BF16))
            yield
        o = jnp.concatenate(outs, axis=1)
        y = _dot(o, wo_ref[...])
        yield
        o_ref[rs, :] = x + _rms(y, gpost_ref[...])

    _round_robin([part(i * rows) for i in range(n_sub)])


def cross_attention(x, kv, g_pre, wq, wo, g_post, layer, B, S, M, tm):
    nt = S // tm
    rowmap = lambda b, i: (b * nt + i, 0)
    lmap = lambda b, i: (layer, 0, 0)
    return pl.pallas_call(
        _xattn_kernel,
        grid=(B, nt),
        in_specs=[
            pl.BlockSpec((tm, D_MODEL), rowmap),
            pl.BlockSpec((M, 2 * D_MODEL), lambda b, i: (b, 0)),
            pl.BlockSpec((None, 1, D_MODEL), lmap),
            pl.BlockSpec((None, D_MODEL, D_MODEL), lmap),
            pl.BlockSpec((None, D_MODEL, D_MODEL), lmap),
            pl.BlockSpec((None, 1, D_MODEL), lmap),
        ],
        out_specs=pl.BlockSpec((tm, D_MODEL), rowmap),
        out_shape=jax.ShapeDtypeStruct((B * S, D_MODEL), F32),
        compiler_params=_cparams(("parallel", "arbitrary")),
        name="cross_attention",
    )(x, kv, g_pre, wq, wo, g_post)


def _mlp_kernel(x_ref, gpre_ref, w1_ref, w2_ref, gpost_ref, o_ref):
    x = x_ref[...]
    h = _rms(x, gpre_ref[...]).astype(BF16)
    tf = 1024
    acc = jnp.zeros(x.shape, F32)
    for f in range(D_FF // tf):
        a = jnp.maximum(_dot(h, w1_ref[:, f * tf:(f + 1) * tf]), 0.0)
        acc = acc + _dot((a * a).astype(BF16), w2_ref[f * tf:(f + 1) * tf, :])
    o_ref[...] = x + _rms(acc, gpost_ref[...])


def mlp(x, g_pre, w1, w2, g_post, layer, tm):
    M = x.shape[0]
    lmap = lambda i: (layer, 0, 0)
    return pl.pallas_call(
        _mlp_kernel,
        grid=(M // tm,),
        in_specs=[
            pl.BlockSpec((tm, D_MODEL), lambda i: (i, 0)),
            pl.BlockSpec((None, 1, D_MODEL), lmap),
            pl.BlockSpec((None, D_MODEL, D_FF), lmap, pipeline_mode=pl.Buffered(1)),
            pl.BlockSpec((None, D_FF, D_MODEL), lmap, pipeline_mode=pl.Buffered(1)),
            pl.BlockSpec((None, 1, D_MODEL), lmap),
        ],
        out_specs=pl.BlockSpec((tm, D_MODEL), lambda i: (i, 0)),
        out_shape=jax.ShapeDtypeStruct((M, D_MODEL), F32),
        compiler_params=_cparams(("parallel",)),
        name="mlp",
    )(x, g_pre, w1, w2, g_post)


def _t5_bucket(dist):
    n = jnp.maximum(dist, 0)
    max_exact = N_BUCKETS // 2
    nf = jnp.maximum(n, 1).astype(F32)
    large = max_exact + (jnp.log(nf / max_exact) / math.log(MAX_DISTANCE / max_exact)
                         * (N_BUCKETS - max_exact)).astype(jnp.int32)
    large = jnp.minimum(large, N_BUCKETS - 1)
    return jnp.where(n < max_exact, n, large)


def _nsa_tables(rel_bias, S):
    n_qb = S // Q_BLOCK
    n_cmp = S // NSA_CMP_STRIDE
    n_blk = S // NSA_SLC_LEN
    bias_f = rel_bias.astype(F32)
    i = jnp.arange(Q_BLOCK)
    per_qb = Q_BLOCK // NSA_CMP_STRIDE
    g = jnp.arange(n_cmp + per_qb * (n_qb - 1)) - per_qb * (n_qb - 1)
    d_c = i[None, :] - (g[:, None] * NSA_CMP_STRIDE + NSA_CMP_LEN - 1)
    def lookup(dist):
        hit = _t5_bucket(dist)[..., None, None] == jnp.arange(N_BUCKETS)[:, None]
        return jnp.sum(jnp.where(hit, bias_f, 0.0), axis=-2)

    bias_c = lookup(d_c).transpose(0, 2, 1).reshape(g.shape[0], MIX_HEADS * Q_BLOCK)
    d0 = i[:, None] - i[None, :]
    tab = jnp.stack([lookup(d0), lookup(d0 + Q_BLOCK)]) - bias_f[N_BUCKETS - 1]
    tab = tab.transpose(0, 2, 3, 1).reshape(2, Q_BLOCK, MIX_HEADS * Q_BLOCK)
    cs = np.arange(n_cmp)[None, :] * NSA_CMP_STRIDE
    ss = np.arange(n_blk)[:, None] * NSA_SLC_LEN
    ovt = np.clip(np.minimum(cs + NSA_CMP_LEN, ss + NSA_SLC_LEN) - np.maximum(cs, ss), 0, None)
    ovt = ovt.astype(np.float32) / NSA_CMP_LEN
    ovt[:, (S - NSA_CMP_LEN) // NSA_CMP_STRIDE + 1:] = 0.0
    return bias_c * LOG2E, tab * LOG2E, jnp.asarray(ovt, BF16)


def _ret_tables(S):
    L = RET_CHUNK
    H = MIX_HEADS
    half = HEAD_DIM // 2
    pos = jnp.arange(S)
    inv_freq = ROPE_BASE ** (-jnp.arange(half, dtype=F32) / half)
    ang = pos.astype(F32)[:, None] * inv_freq[None, :]
    cos = jnp.tile(jnp.cos(ang), (1, H))
    sin = jnp.tile(jnp.sin(ang), (1, H))
    lg = jnp.log(1.0 - 2.0 ** (-5.0 - jnp.arange(H, dtype=F32)))
    n = jnp.arange(L, dtype=F32)
    diff = n[:, None] - n[None, :]
    inner = jnp.where(diff >= 0, jnp.exp(jnp.maximum(diff, 0.0)[None] * lg[:, None, None]), 0.0)
    indec = inner.reshape(H * L, L)
    q_decay = jnp.exp((n + 1.0)[None, :] * lg[:, None])
    k_decay = jnp.exp((L - 1.0 - n)[None, :] * lg[:, None])
    chunk_decay = jnp.exp(L * lg)
    lane = np.arange(256)
    head_v = lane // HEAD_DIM
    head_qk = (lane % 128) // half
    qdec = q_decay.T[:, head_v]
    kdec = k_decay.T[:, head_qk]
    cdec = chunk_decay[head_v][None, :]
    bd = jnp.asarray((head_qk[:, None] == head_v[None, :]).astype(np.float32))
    segm = jnp.asarray((head_v[:, None] == head_v[None, :]).astype(np.float32) / HEAD_DIM, BF16)
    return cos, sin, indec, qdec, kdec, cdec, bd, segm


def _rwkv_consts():
    C = RWKV_CHUNK
    t = np.arange(C)
    tri = (t[:, None] >= t[None, :]).astype(np.float32)
    r = np.arange(4 * C)
    bdl = ((r[:, None] // C == r[None, :] // C) & (r[:, None] % C > r[None, :] % C)).astype(np.float32)
    strict = (t[:, None] > (r[None, :] % C)).astype(np.float32)
    incl = (t[:, None] >= (r[None, :] % C)).astype(np.float32)
    lane = np.arange(256) // HEAD_DIM
    seg = (lane[:, None] == lane[None, :]).astype(np.float32)
    return (jnp.asarray(tri, BF16), jnp.asarray(bdl), jnp.asarray(np.stack([strict, incl])),
            jnp.asarray(seg, BF16), jnp.asarray(seg / HEAD_DIM, BF16))


def _hi_lo(w):
    hi = w.astype(BF16)
    lo = (w - hi.astype(F32)).astype(BF16)
    return jnp.stack([hi, lo], axis=1)


def _pad_rows(w, top, total):
    return jnp.pad(w, ((0, 0), (top, total - top - w.shape[1]), (0, 0)))


def _layout_w_in(w_in):
    L = w_in.shape[0]
    o = 0
    nsa_q = w_in[:, :, o:o + 256]; o += 256
    nsa_kv = w_in[:, :, o:o + 384]; o += 384
    nsa_g = w_in[:, :, o:o + 12]; o += 12
    ret = w_in[:, :, o:o + 1024]; o += 1024
    rwkv = w_in[:, :, o:o + RWKV_COLS]; o += RWKV_COLS
    conv = w_in[:, :, o:o + 768]; o += 768
    gate = w_in[:, :, o:o + 4096]

    def rot_perm(w):
        half = HEAD_DIM // 2
        return [w[:, :, h * HEAD_DIM + p * half:h * HEAD_DIM + (p + 1) * half]
                for p in range(2) for h in range(MIX_HEADS)]

    ret = rot_perm(ret[:, :, 0:256]) + rot_perm(ret[:, :, 256:512]) + [ret[:, :, 512:]]
    zeros = lambda n: jnp.zeros((L, D_MODEL, n), w_in.dtype)
    out = jnp.concatenate([conv, nsa_q] + ret + [rwkv, zeros(1024 - RWKV_COLS), nsa_kv, nsa_g,
                                                 zeros(128 - 12)], axis=2)
    assert out.shape[2] == Z_COLS
    return out.astype(BF16), gate.astype(BF16)


def _layout_cmp(cmp_w, cmp_pe):
    L = cmp_w.shape[0]
    wk = cmp_w[:, 0]
    wv = cmp_w[:, 1]
    zero = jnp.zeros_like(wk)
    blk = jnp.concatenate([jnp.concatenate([wk, zero], axis=3), jnp.concatenate([zero, wv], axis=3)], axis=2)
    pe2 = jnp.concatenate([cmp_pe, cmp_pe], axis=2)[:, :, None, :]
    s = NSA_CMP_STRIDE
    return blk[:, :s].astype(BF16), blk[:, s:].astype(BF16), pe2[:, :s], pe2[:, s:]


def kernel(x, mem, ln_mix_pre, w_in, nsa_cmp_w, nsa_cmp_pe, rel_bias, ret_norm_g, rwkv_mu, rwkv_w0, rwkv_w2, rwkv_a0, rwkv_a2, rwkv_g2, rwkv_k_k, rwkv_k_a, rwkv_r_k, rwkv_ln_g, rwkv_ln_b, conv_w, w_branch, w_mix_out, ln_mix_post, ln_xa_pre, ln_mem, xa_wq, xa_wkv, xa_wo, ln_xa_post, ln_mlp_pre, mlp_w1, mlp_w2, ln_mlp_post):
    B, S, D = x.shape
    M = mem.shape[1]
    depth = w_in.shape[0]
    row = lambda g: g[:, None, :]

    w_in_b, w_gate_b = _layout_w_in(w_in)
    cmp_lo, cmp_hi, pe_lo, pe_hi = _layout_cmp(nsa_cmp_w, nsa_cmp_pe)
    nsa_tabs = _nsa_tables(rel_bias, S)
    ret_tabs = _ret_tables(S)
    rwkv_consts = _rwkv_consts()
    mu = jnp.pad(rwkv_mu, ((0, 0), (0, 1024 - RWKV_COLS)))[:, None, :]
    lora_w = _hi_lo(jnp.concatenate([_pad_rows(rwkv_w2, 0, 128), _pad_rows(rwkv_a2, RWKV_DECAY_LORA, 128),
                                     _pad_rows(rwkv_g2, RWKV_DECAY_LORA + RWKV_AAA_LORA, 128)], axis=2))
    vec = jnp.stack([rwkv_w0, rwkv_a0, rwkv_k_k, rwkv_k_a, rwkv_r_k, rwkv_ln_g, rwkv_ln_b,
                     jnp.zeros_like(rwkv_w0)], axis=1)
    conv_p = jnp.pad(conv_w, ((0, 0), (0, 8 - CONV_WIDTH), (0, 0)))
    w_branch_b = w_branch.astype(BF16)
    w_mix_out_b = w_mix_out.astype(BF16)
    xa_wq_b = xa_wq.astype(BF16)
    xa_wkv_b = xa_wkv.astype(BF16)
    xa_wo_b = xa_wo.astype(BF16)
    mlp_w1_b = mlp_w1.astype(BF16)
    mlp_w2_b = mlp_w2.astype(BF16)

    xf = x.reshape(B * S, D)
    memf = mem.reshape(B * M, D)
    for l in range(depth):
        z = norm_matmul(xf, row(ln_mix_pre), w_in_b, l, tm=512, tn=Z_COLS, out_dtype=F32)
        prep = nsa_compress(z, cmp_lo, cmp_hi, pe_lo, pe_hi, l, B, S)
        z3 = z.reshape(B, S, Z_COLS)
        o_nsa = nsa_attention(z3, prep, *nsa_tabs, B, S, G=8).reshape(B * S, MIX_WIDTH)
        o_ret = retention(z3, ret_tabs, row(ret_norm_g), l, B, S, G=4).reshape(B * S, MIX_WIDTH)
        o_rwkv = rwkv7(z3, mu, lora_w, vec, rwkv_consts, l, B, S, G=4).reshape(B * S, MIX_WIDTH)
        xf = merge_mix(xf, z, o_nsa, o_ret, o_rwkv, conv_p, row(ln_mix_pre), w_gate_b, w_branch_b,
                       w_mix_out_b, row(ln_mix_post), l, B, S, tm=512)
        kvm = norm_matmul(memf, row(ln_mem), xa_wkv_b, l, tm=2 * M, tn=2 * D_MODEL, out_dtype=BF16)
        xf = cross_attention(xf, kvm, row(ln_xa_pre), xa_wq_b, xa_wo_b, row(ln_xa_post), l, B, S, M, tm=1024)
        xf = mlp(xf, row(ln_mlp_pre), mlp_w1_b, mlp_w2_b, row(ln_mlp_post), l, tm=512)
    return xf.reshape(B, S, D)
```

```python
import functools
import math

import numpy as np
import jax
import jax.numpy as jnp
from jax import lax
from jax.experimental import pallas as pl
from jax.experimental.pallas import tpu as pltpu

F32 = jnp.float32
BF16 = jnp.bfloat16

D_MODEL = 1024
N_BRANCHES = 4
MIX_WIDTH = 256
HEAD_DIM = 64
MIX_HEADS = 4

NSA_CMP_LEN = 32
NSA_CMP_STRIDE = 16
NSA_SLC_LEN = 64
NSA_TOP_N = 8
NSA_WINDOW = 512
Q_BLOCK = 128
NSA_V_ROWS = HEAD_DIM + 16
FORCE_BONUS = 1e4
N_BUCKETS = 32
MAX_DISTANCE = 128

RET_CHUNK = 128
ROPE_BASE = 10000.0
RET_NORM_EPS = 1e-5

RWKV_DECAY_LORA = 32
RWKV_AAA_LORA = 32
RWKV_GATE_LORA = 64
RWKV_GN_EPS = 64e-5
RWKV_COLS = 3 * MIX_WIDTH + RWKV_DECAY_LORA + RWKV_AAA_LORA + RWKV_GATE_LORA
RWKV_CHUNK = 64

CONV_WIDTH = 3
XA_HEADS = 4
XA_HEAD_DIM = D_MODEL // XA_HEADS
D_FF = 4 * D_MODEL

RMS_EPS = 1e-6
LOG2E = math.log2(math.e)
NEG_INF = -1e30
NEG_BIG = -3e38

OFF_CONV = 0
OFF_NSAQ = 768
OFF_RET = 1024
OFF_RWKV = 2048
OFF_NSAKV = 3072
OFF_NSAG = 3456
Z_COLS = 3584

VMEM_LIMIT = 56 * 1024 * 1024


def _cparams(sem):
    return pltpu.CompilerParams(dimension_semantics=sem, vmem_limit_bytes=VMEM_LIMIT)


def _dot(a, b):
    return jnp.dot(a, b, preferred_element_type=F32)


def _dot_nt(a, b):
    return lax.dot_general(a, b, (((1,), (1,)), ((), ())), preferred_element_type=F32)


def _dot_tn(a, b):
    return lax.dot_general(a, b, (((0,), (0,)), ((), ())), preferred_element_type=F32)


def _split2(x):
    hi = x.astype(BF16)
    lo = (x - hi.astype(F32)).astype(BF16)
    return hi, lo


def _dot_x2(x, w_bf16, *more):
    xs = (x,) + more
    rows = x.shape[0]
    parts = [p for xi in xs for p in _split2(xi)]
    out = _dot(jnp.concatenate(parts, axis=0), w_bf16)
    res = [out[2 * i * rows:(2 * i + 1) * rows] + out[(2 * i + 1) * rows:(2 * i + 2) * rows]
           for i in range(len(xs))]
    return res[0] if not more else res


def _rms(x, g):
    ms = jnp.mean(x * x, axis=-1, keepdims=True)
    return x * lax.rsqrt(ms + RMS_EPS) * g


def _norm_matmul_kernel(x_ref, g_ref, w_ref, o_ref, h_ref):
    @pl.when(pl.program_id(1) == 0)
    def _():
        h_ref[...] = _rms(x_ref[...], g_ref[...]).astype(BF16)

    o_ref[...] = _dot(h_ref[...], w_ref[...]).astype(o_ref.dtype)


def norm_matmul(x, g, w, layer, tm, tn, out_dtype):
    M, D = x.shape
    N = w.shape[2]
    w_mode = dict(pipeline_mode=pl.Buffered(1)) if tn == N else {}
    return pl.pallas_call(
        _norm_matmul_kernel,
        grid=(M // tm, N // tn),
        in_specs=[
            pl.BlockSpec((tm, D), lambda i, j: (i, 0)),
            pl.BlockSpec((None, 1, D), lambda i, j: (layer, 0, 0)),
            pl.BlockSpec((None, D, tn), lambda i, j: (layer, 0, j), **w_mode),
        ],
        out_specs=pl.BlockSpec((tm, tn), lambda i, j: (i, j)),
        out_shape=jax.ShapeDtypeStruct((M, N), out_dtype),
        scratch_shapes=[pltpu.VMEM((tm, D), BF16)],
        compiler_params=_cparams(("parallel", "arbitrary")),
        name="norm_matmul",
    )(x, g, w)


def _nsa_compress_kernel(kvc_ref, kvs_ref, kvw_ref, wlo_ref, whi_ref, pelo_ref, pehi_ref,
                         kc_ref, vct_ref, ks_ref, vst_ref, kw_ref, vwt_ref):
    nblk = kc_ref.shape[0]
    n_kt = vwt_ref.shape[0]
    ylo = jnp.zeros((nblk, 128), F32)
    yhi = jnp.zeros((nblk, 128), F32)
    for r in range(NSA_CMP_STRIDE):
        xr = kvc_ref[pl.ds(r, nblk, stride=NSA_CMP_STRIDE), :]
        ylo = ylo + _dot((xr + pelo_ref[r]).astype(BF16), wlo_ref[r])
        yhi = yhi + _dot((xr + pehi_ref[r]).astype(BF16), whi_ref[r])
    y = ylo + pltpu.roll(yhi, nblk - 1, axis=0)
    kc_ref[...] = y.astype(BF16)
    vct_ref[...] = y.T[HEAD_DIM:, :].astype(BF16)
    kw_ref[...] = kvw_ref[...].astype(BF16)
    ones = jnp.ones((NSA_V_ROWS - HEAD_DIM, 128), BF16)
    for kt in range(n_kt):
        rs = slice(kt * 128, (kt + 1) * 128)
        vst_ref[kt] = jnp.concatenate([kvs_ref[rs, :].T[HEAD_DIM:, :].astype(BF16), ones], axis=0)
        vwt_ref[kt] = jnp.concatenate([kvw_ref[rs, :].T[HEAD_DIM:, :].astype(BF16), ones], axis=0)
    S = kvs_ref.shape[0]
    key = lax.broadcasted_iota(jnp.int32, (S, 128), 0)
    lane = lax.broadcasted_iota(jnp.int32, (S, 128), 1)
    onehot = (lane == HEAD_DIM + key // NSA_SLC_LEN).astype(F32)
    ks_ref[0:S, :] = jnp.where(lane < HEAD_DIM, kvs_ref[...], onehot).astype(BF16)
    pad_lane = lax.broadcasted_iota(jnp.int32, (128, 128), 1)
    ks_ref[S:S + 128, :] = (pad_lane == HEAD_DIM + S // NSA_SLC_LEN).astype(BF16)
    vst_ref[n_kt] = jnp.zeros((NSA_V_ROWS, 128), BF16)


def nsa_compress(z, wlo, whi, pelo, pehi, layer, B, S):
    nblk = S // NSA_CMP_STRIDE
    n_kt = S // 128
    c0 = OFF_NSAKV // 128
    wmap = lambda b: (layer, 0, 0, 0)
    b3 = lambda b: (b, 0, 0)
    b4 = lambda b: (b, 0, 0, 0)
    return pl.pallas_call(
        _nsa_compress_kernel,
        grid=(B,),
        in_specs=[
            pl.BlockSpec((S, 128), lambda b: (b, c0)),
            pl.BlockSpec((S, 128), lambda b: (b, c0 + 1)),
            pl.BlockSpec((S, 128), lambda b: (b, c0 + 2)),
            pl.BlockSpec((None, NSA_CMP_STRIDE, 128, 128), wmap),
            pl.BlockSpec((None, NSA_CMP_STRIDE, 128, 128), wmap),
            pl.BlockSpec((None, NSA_CMP_STRIDE, 1, 128), wmap),
            pl.BlockSpec((None, NSA_CMP_STRIDE, 1, 128), wmap),
        ],
        out_specs=[
            pl.BlockSpec((None, nblk, 128), b3),
            pl.BlockSpec((None, HEAD_DIM, nblk), b3),
            pl.BlockSpec((None, S + 128, 128), b3),
            pl.BlockSpec((None, n_kt + 1, NSA_V_ROWS, 128), b4),
            pl.BlockSpec((None, S, 128), b3),
            pl.BlockSpec((None, n_kt, NSA_V_ROWS, 128), b4),
        ],
        out_shape=[
            jax.ShapeDtypeStruct((B, nblk, 128), BF16),
            jax.ShapeDtypeStruct((B, HEAD_DIM, nblk), BF16),
            jax.ShapeDtypeStruct((B, S + 128, 128), BF16),
            jax.ShapeDtypeStruct((B, n_kt + 1, NSA_V_ROWS, 128), BF16),
            jax.ShapeDtypeStruct((B, S, 128), BF16),
            jax.ShapeDtypeStruct((B, n_kt, NSA_V_ROWS, 128), BF16),
        ],
        compiler_params=_cparams(("parallel",)),
        name="nsa_compress",
    )(z, z, z, wlo, whi, pelo, pehi)


def _rowmax(s):
    return jnp.max(s, axis=-1, keepdims=True)


def _rowsum(s):
    return jnp.sum(s, axis=-1, keepdims=True)


def _colmax(s):
    return jnp.max(s, axis=0, keepdims=True)


def _colsum(s):
    return jnp.sum(s, axis=0, keepdims=True)


def _nsa_kernel(q_ref, gl_ref, kc_ref, vct_ref, ks_ref, vst_ref, kw_ref, vwt_ref, bc_ref, tab_ref,
                ovt_ref, o_ref):
    bi = pl.program_id(1)
    G = q_ref.shape[0]
    n_kt = vwt_ref.shape[1]
    n_blk = ovt_ref.shape[0]
    n_cmp = kc_ref.shape[1]
    QB = Q_BLOCK
    R = MIX_HEADS * QB
    n_off = NSA_WINDOW // 128

    lo = jnp.maximum(bi - n_off, 0)
    q4 = [None] * G
    q4_sel = [None] * G
    o_c = [None] * G
    carry_s = [None] * G
    carry_w = [None] * G

    def tile_qk(g, kref, kt):
        start = pl.multiple_of(kt * 128, 128)
        return _dot(kref[g, pl.ds(start, 128), :HEAD_DIM], q4[g])

    def tile_qk_sel(g, kt):
        start = pl.multiple_of(kt * 128, 128)
        return _dot(ks_ref[g, pl.ds(start, 128), :], q4_sel[g])

    def head_cols(x, h):
        return x[:, h * QB:(h + 1) * QB]

    key1 = lax.broadcasted_iota(jnp.int32, (128, QB), 0)
    i1 = lax.broadcasted_iota(jnp.int32, (128, QB), 1)
    causal1 = key1 <= i1
    kp = jnp.maximum(bi - 1, 0)
    pen_prev = jnp.where(bi >= 1, 0.0, NEG_INF)

    def win_add(kt, valid):
        shift = jnp.where(valid, jnp.where(bi - kt < n_off, -4096, 0), 4096)
        return jnp.where(key1 > i1 + shift, 0.0, NEG_INF)

    def diag_add(extra):
        return [jnp.where(causal1, head_cols(tab_ref[0], h) + extra, NEG_INF) for h in range(MIX_HEADS)]

    def prev_add(extra):
        return [head_cols(tab_ref[1], h) + extra for h in range(MIX_HEADS)]

    def pair_update(carry, qk0, add0, vt0, qk1, add1, vt1):
        p0s, p1s, ms, alphas = [], [], [], []
        for h in range(MIX_HEADS):
            a0 = head_cols(qk0, h) if add0 is None else head_cols(qk0, h) + add0[h]
            a1 = head_cols(qk1, h) if add1 is None else head_cols(qk1, h) + add1[h]
            m_new = jnp.maximum(_colmax(a0), _colmax(a1))
            if carry is not None:
                m_old = head_cols(carry[0], h)
                m_new = jnp.maximum(m_old, m_new)
                alphas.append(jnp.exp2(m_old - m_new))
            ms.append(m_new)
            p0s.append(jnp.exp2(a0 - m_new).astype(BF16))
            p1s.append(jnp.exp2(a1 - m_new).astype(BF16))
        yield
        acc = _dot(vt0, jnp.concatenate(p0s, axis=1)) + _dot(vt1, jnp.concatenate(p1s, axis=1))
        if carry is not None:
            acc = jnp.concatenate(alphas, axis=1) * carry[1] + acc
        return jnp.concatenate(ms, axis=1), acc

    def head_part(g):
        qt = (q_ref[g] * (HEAD_DIM ** -0.5 * LOG2E)).T
        q4[g] = jnp.concatenate([qt[h * HEAD_DIM:(h + 1) * HEAD_DIM] for h in range(MIX_HEADS)],
                                axis=1).astype(BF16)
        yield
        carry_w[g] = yield from pair_update(None, tile_qk(g, kw_ref, bi), diag_add(0.0), vwt_ref[g, bi],
                                            tile_qk(g, kw_ref, kp), prev_add(pen_prev), vwt_ref[g, kp])
        yield
        g0 = pl.multiple_of((n_kt - 1 - bi) * 8, 8)
        qk = _dot(kc_ref[g, :, :HEAD_DIM], q4[g])
        nrow = lax.broadcasted_iota(jnp.int32, (n_cmp, QB), 0)
        tq = bi * QB + lax.broadcasted_iota(jnp.int32, (n_cmp, QB), 1)
        vis = tq - (nrow * NSA_CMP_STRIDE + (NSA_CMP_LEN - 1)) >= 0
        live = (tq >= NSA_CMP_LEN - 1).astype(F32)
        yield
        p_cs = []
        for h in range(MIX_HEADS):
            s = jnp.where(vis, head_cols(qk, h) + bc_ref[pl.ds(g0, n_cmp), h * QB:(h + 1) * QB], NEG_INF)
            e = jnp.exp2(s - _colmax(s))
            p_cs.append(e / _colsum(e) * live)
        yield
        o_c[g] = _dot(vct_ref[g], jnp.concatenate(p_cs, axis=1).astype(BF16))
        p_sum = p_cs[0] + p_cs[1] + p_cs[2] + p_cs[3]
        p_hi, p_lo = _split2(p_sum)
        imp = _dot(ovt_ref[...], p_hi) + _dot(ovt_ref[...], p_lo)
        yield
        blk = lax.broadcasted_iota(jnp.int32, (n_blk, QB), 0)
        cur = (bi * QB + lax.broadcasted_iota(jnp.int32, (n_blk, QB), 1)) >> 6
        forced = (blk == 0) | (blk == cur) | (blk == cur - 1)
        imp = jnp.where(forced, imp + FORCE_BONUS, imp)
        imp = jnp.where(blk <= cur, imp, NEG_INF)
        blk_f = blk.astype(F32)
        sel = jnp.zeros((n_blk, QB), F32)
        for _ in range(NSA_TOP_N):
            mx = _colmax(imp)
            idx = jnp.min(jnp.where(imp == mx, blk_f, 4096.0), axis=0, keepdims=True)
            pick = blk_f == idx
            sel = jnp.where(pick, 1.0, sel)
            imp = jnp.where(pick, NEG_BIG, imp)
            yield
        sel_add = (sel - 1.0) * (-NEG_INF)
        rows_left = 128 - HEAD_DIM - n_blk
        pad_rows = jnp.where(lax.broadcasted_iota(jnp.int32, (rows_left, QB), 0) == 0, NEG_INF, 0.0)
        mask_rows = jnp.concatenate([sel_add, pad_rows], axis=0).astype(BF16)
        q4_sel[g] = jnp.concatenate([q4[g], jnp.concatenate([mask_rows] * MIX_HEADS, axis=1)], axis=0)
        yield
        carry_s[g] = yield from pair_update(
            None, tile_qk_sel(g, bi), diag_add(0.0), vst_ref[g, bi],
            tile_qk_sel(g, kp), prev_add(pen_prev), vst_ref[g, kp])

    _round_robin([head_part(g) for g in range(G)])

    n_old = jnp.maximum(bi - 1, 0)

    def sel_step(g, j, carry, out):
        k0 = 2 * j
        k1 = jnp.where(k0 + 1 < n_old, k0 + 1, n_kt)
        qk0 = tile_qk_sel(g, k0)
        qk1 = tile_qk_sel(g, k1)
        yield
        out[g] = yield from pair_update(carry, qk0, None, vst_ref[g, k0], qk1, None, vst_ref[g, k1])

    def win_step(g, j, carry, out):
        k0 = lo + 2 * j
        k1 = jnp.minimum(k0 + 1, bi)
        qk0 = tile_qk(g, kw_ref, k0)
        qk1 = tile_qk(g, kw_ref, k1)
        yield
        out[g] = yield from pair_update(carry, qk0, [win_add(k0, True)] * MIX_HEADS, vwt_ref[g, k0],
                                        qk1, [win_add(k1, k0 + 1 < n_old)] * MIX_HEADS, vwt_ref[g, k1])

    def joint(step):
        def body(j, carries):
            out = [None] * G
            _round_robin([step(g, j, carries[g], out) for g in range(G)])
            return tuple(out)
        return body

    res_s = lax.fori_loop(0, (n_old + 1) // 2, joint(sel_step), tuple(carry_s))
    res_w = lax.fori_loop(0, (n_old - lo + 1) // 2, joint(win_step), tuple(carry_w))

    def tail_part(g):
        acc_s = res_s[g][1]
        acc_w = res_w[g][1]
        o_s = acc_s[:HEAD_DIM] * (1.0 / acc_s[HEAD_DIM:HEAD_DIM + 1])
        o_w = acc_w[:HEAD_DIM] * (1.0 / acc_w[HEAD_DIM:HEAD_DIM + 1])
        gate = jax.nn.sigmoid(gl_ref[g].T)
        yield
        outs = []
        for h in range(MIX_HEADS):
            cs = slice(h * QB, (h + 1) * QB)
            outs.append(gate[3 * h:3 * h + 1] * o_c[g][:, cs] + gate[3 * h + 1:3 * h + 2] * o_s[:, cs]
                        + gate[3 * h + 2:3 * h + 3] * o_w[:, cs])
        o_ref[g] = jnp.concatenate(outs, axis=0).T

    _round_robin([tail_part(g) for g in range(G)])


def nsa_attention(z, prep, bias_c, tab, ovt, B, S, G):
    n_qb = S // Q_BLOCK
    n_kt = S // 128
    n_cmp = S // NSA_CMP_STRIDE
    R = MIX_HEADS * Q_BLOCK
    kc, vct, ks, vst, kw, vwt = prep
    b3 = lambda b, i: (b, 0, 0)
    b4 = lambda b, i: (b, 0, 0, 0)
    return pl.pallas_call(
        _nsa_kernel,
        grid=(B // G, n_qb),
        in_specs=[
            pl.BlockSpec((G, Q_BLOCK, MIX_WIDTH), lambda b, i: (b, i, OFF_NSAQ // MIX_WIDTH)),
            pl.BlockSpec((G, Q_BLOCK, 128), lambda b, i: (b, i, OFF_NSAG // 128)),
            pl.BlockSpec((G, n_cmp, 128), b3),
            pl.BlockSpec((G, HEAD_DIM, n_cmp), b3),
            pl.BlockSpec((G, S + 128, 128), b3),
            pl.BlockSpec((G, n_kt + 1, NSA_V_ROWS, 128), b4),
            pl.BlockSpec((G, S, 128), b3),
            pl.BlockSpec((G, n_kt, NSA_V_ROWS, 128), b4),
            pl.BlockSpec(bias_c.shape, lambda b, i: (0, 0)),
            pl.BlockSpec((2, 128, R), lambda b, i: (0, 0, 0)),
            pl.BlockSpec(ovt.shape, lambda b, i: (0, 0)),
        ],
        out_specs=pl.BlockSpec((G, Q_BLOCK, MIX_WIDTH), lambda b, i: (b, i, 0)),
        out_shape=jax.ShapeDtypeStruct((B, S, MIX_WIDTH), F32),
        compiler_params=_cparams(("parallel", "arbitrary")),
        name="nsa_attention",
    )(z, z, kc, vct, ks, vst, kw, vwt, bias_c, tab, ovt)


def _stack_heads(x, head_of_lane):
    return jnp.concatenate([jnp.where(head_of_lane == h, x, jnp.zeros_like(x))
                            for h in range(MIX_HEADS)], axis=0)


def _head_norm(y, seg_mean, eps):
    mu = _dot_x2(y, seg_mean)
    d = y - mu
    var = _dot_x2(d * d, seg_mean)
    return d * lax.rsqrt(var + eps)


def _round_robin(chains):
    while chains:
        chains = [c for c in chains if next(c, StopIteration) is not StopIteration]


def _retention_kernel(z_ref, cos_ref, sin_ref, indec_ref, qdec_ref, kdec_ref, cdec_ref, bd_ref,
                      segm_ref, ng_ref, o_ref, st_ref):
    @pl.when(pl.program_id(1) == 0)
    def _():
        st_ref[...] = jnp.zeros_like(st_ref)

    _round_robin([_retention_chunk(z_ref.at[g], cos_ref, sin_ref, indec_ref, qdec_ref, kdec_ref, cdec_ref,
                                   bd_ref, segm_ref, ng_ref, o_ref.at[g], st_ref.at[g])
                  for g in range(z_ref.shape[0])])


def _retention_chunk(z_ref, cos_ref, sin_ref, indec_ref, qdec_ref, kdec_ref, cdec_ref, bd_ref,
                     segm_ref, ng_ref, o_ref, st_ref):
    L = RET_CHUNK
    z = z_ref[...]
    q = z[:, 0:256]
    k = z[:, 256:512]
    v = z[:, 512:768]
    g = z[:, 768:1024]
    cos = cos_ref[...]
    sin = sin_ref[...]

    def rot(u):
        u1 = u[:, :128]
        u2 = u[:, 128:]
        return jnp.concatenate([u1 * cos - u2 * sin, u2 * cos + u1 * sin], axis=1)

    qr = rot(q) * (HEAD_DIM ** -0.5)
    kr = rot(k)
    lane = lax.broadcasted_iota(jnp.int32, (L, 256), 1)
    head_qk = (lane & 127) >> 5
    head_v = lane >> 6
    qb = qr.astype(BF16)
    kb = kr.astype(BF16)
    vb = v.astype(BF16)
    yield

    att = _dot_nt(_stack_heads(qb, head_qk), kb) * indec_ref[...]
    yield
    o_st = _dot(att.astype(BF16), vb)
    o = jnp.zeros((L, 256), F32)
    for h in range(MIX_HEADS):
        o = o + jnp.where(head_v == h, o_st[h * L:(h + 1) * L], 0.0)
    yield
    state = st_ref[...]
    o = o + _dot(qb, state.astype(BF16)) * qdec_ref[...]
    st_ref[...] = state * cdec_ref[...] + _dot_tn((kr * kdec_ref[...]).astype(BF16), vb) * bd_ref[...]
    yield

    mu = _dot_x2(o, segm_ref[...])
    yield
    d = o - mu
    var = _dot_x2(d * d, segm_ref[...])
    yield
    o_ref[...] = g * jax.nn.sigmoid(g) * (d * lax.rsqrt(var + RET_NORM_EPS) * ng_ref[...])


def retention(z, tabs, ng, layer, B, S, G):
    L = RET_CHUNK
    nc = S // L
    cos, sin, indec, qdec, kdec, cdec, bd, segm = tabs
    const2 = lambda b, c: (0, 0)
    return pl.pallas_call(
        _retention_kernel,
        grid=(B // G, nc),
        in_specs=[
            pl.BlockSpec((G, L, 1024), lambda b, c: (b, c, OFF_RET // 1024)),
            pl.BlockSpec((L, 128), lambda b, c: (c, 0)),
            pl.BlockSpec((L, 128), lambda b, c: (c, 0)),
            pl.BlockSpec((MIX_HEADS * L, L), const2),
            pl.BlockSpec((L, 256), const2),
            pl.BlockSpec((L, 256), const2),
            pl.BlockSpec((1, 256), const2),
            pl.BlockSpec((256, 256), const2),
            pl.BlockSpec((256, 256), const2),
            pl.BlockSpec((None, 1, 256), lambda b, c: (layer, 0, 0)),
        ],
        out_specs=pl.BlockSpec((G, L, 256), lambda b, c: (b, c, 0)),
        out_shape=jax.ShapeDtypeStruct((B, S, 256), F32),
        scratch_shapes=[pltpu.VMEM((G, 256, 256), F32)],
        compiler_params=_cparams(("parallel", "arbitrary")),
        name="retention",
    )(z, cos, sin, indec, qdec, kdec, cdec, bd, segm, ng)


def _rwkv_kernel(z_ref, mu_ref, lw_ref, vec_ref, tri_ref, bdl_ref, wide_ref,
                 segs_ref, segm_ref, o_ref, st_ref, prev_ref):
    @pl.when(pl.program_id(1) == 0)
    def _():
        st_ref[...] = jnp.zeros_like(st_ref)
        prev_ref[...] = jnp.zeros_like(prev_ref)

    _round_robin([_rwkv_chunk(z_ref.at[g], mu_ref, lw_ref, vec_ref, tri_ref, bdl_ref,
                              wide_ref, segs_ref, segm_ref, o_ref.at[g], st_ref.at[g], prev_ref.at[g])
                  for g in range(z_ref.shape[0])])


def _rwkv_chunk(z_ref, mu_ref, lw_ref, vec_ref, tri_ref, bdl_ref, wide_ref,
                segs_ref, segm_ref, o_ref, st_ref, prev_ref):
    C = RWKV_CHUNK
    z = z_ref[...]
    rows = lax.broadcasted_iota(jnp.int32, (C, 1024), 0)
    zs = jnp.where(rows == 0, prev_ref[0:1, :], pltpu.roll(z, 1, axis=0))
    prev_ref[...] = jnp.broadcast_to(z[C - 1:C, :], prev_ref.shape)
    zf = z + (zs - z) * mu_ref[...]
    r = zf[:, 0:256]
    k = zf[:, 256:512]
    v = zf[:, 512:768]
    lora = zf[:, 768:896]

    vec = vec_ref[...]
    w0, a0, k_k, k_a, r_k, ln_g, ln_b = (vec[i:i + 1] for i in range(7))

    lane = lax.broadcasted_iota(jnp.int32, lora.shape, 1)
    lora_in = jnp.where(lane < RWKV_DECAY_LORA, jnp.tanh(lora),
                        jnp.where(lane < RWKV_DECAY_LORA + RWKV_AAA_LORA, lora, jax.nn.sigmoid(lora)))
    hi, lo = _split2(lora_in)
    both = _dot(jnp.concatenate([hi, lo], axis=0), lw_ref[0])
    lora_out = both[:C] + both[C:] + _dot(hi, lw_ref[1])
    wpre = w0 + lora_out[:, 0:256]
    y = -wpre
    softplus = jnp.maximum(y, 0.0) + jnp.log(1.0 + jnp.exp(-jnp.abs(y)))
    w_log = -softplus - 0.5
    ld = -jnp.exp(w_log)
    a = jax.nn.sigmoid(a0 + lora_out[:, 256:512])
    gate = lora_out[:, 512:768]
    kk = k * k_k
    k2 = k * (1.0 + (a - 1.0) * k_a)
    kk_sq, bonus = _dot_x2(kk * kk, segs_ref[...], r * k2 * r_k)
    kk = kk / jnp.maximum(jnp.sqrt(kk_sq), 1e-12)
    yield

    tri = tri_ref[...]
    l1 = ld.astype(BF16)
    r1 = ld - l1.astype(F32)
    l2 = r1.astype(BF16)
    l3 = (r1 - l2.astype(F32)).astype(BF16)
    cs = _dot(tri, l1) + _dot(tri, l2) + _dot(tri, l3)
    cs_end = cs[C - 1:C, :]
    yield
    e_neg = jnp.exp(-cs)
    e_end = jnp.exp(cs_end - cs)
    kka = kk * a
    a_t = -kk * jnp.exp(cs - ld)
    r_t = r * jnp.exp(cs)
    b_t = kka * e_neg
    k_t = k2 * e_neg
    b_g = kka * e_end
    k_g = k2 * e_end

    lane = lax.broadcasted_iota(jnp.int32, (C, 256), 1)
    head = lane >> 6
    stack = lambda x: _stack_heads(x.astype(BF16), head)
    a_st, b_st, k_st, v_st = stack(a_t), stack(b_t), stack(k_t), stack(v)

    n_pow = _dot_nt(a_st, b_st) * bdl_ref[...]
    ri = lax.broadcasted_iota(jnp.int32, (4 * C, 4 * C), 0)
    ci = lax.broadcasted_iota(jnp.int32, (4 * C, 4 * C), 1)
    t_inv = jnp.where(ri == ci, 1.0, 0.0) + n_pow
    yield

    ar = jnp.concatenate([a_t, r_t], axis=0).astype(BF16)
    bk = jnp.concatenate([b_st, k_st], axis=0)
    wide = _dot_nt(ar, bk)
    strict = wide_ref[0]
    incl = wide_ref[1]
    a_ak = (wide[0:C, 4 * C:] * strict).astype(BF16)
    a_rb = (wide[C:, 0:4 * C] * incl).astype(BF16)
    a_rk = (wide[C:, 4 * C:] * incl).astype(BF16)
    yield

    sq = 2
    while sq < C:
        nb = n_pow.astype(BF16)
        n_pow = _dot(nb, nb)
        yield
        t_inv = t_inv + _dot(t_inv.astype(BF16), n_pow.astype(BF16))
        sq *= 2
    yield

    state = st_ref[...]
    sb = state.astype(BF16)
    x = _dot_nt(a_t.astype(BF16), sb) + _dot(a_ak, v_st)
    yield
    u_st = _dot(t_inv.astype(BF16), stack(x)).astype(BF16)
    yield
    yv = _dot_nt(r_t.astype(BF16), sb) + _dot(a_rb, u_st) + _dot(a_rk, v_st)
    uv = jnp.concatenate([u_st, v_st], axis=0)
    bkg = jnp.concatenate([stack(b_g), stack(k_g)], axis=0)
    st_ref[...] = state * jnp.exp(cs_end) + _dot_tn(uv, bkg)
    yield

    yn = _head_norm(yv, segm_ref[...], RWKV_GN_EPS) * ln_g + ln_b
    yn = yn + bonus * v
    o_ref[...] = yn * gate


def rwkv7(z, mu, lora_w, vec, consts, layer, B, S, G):
    C = RWKV_CHUNK
    nc = S // C
    tri, bdl, wide, segs, segm = consts
    const2 = lambda b, c: (0, 0)
    return pl.pallas_call(
        _rwkv_kernel,
        grid=(B // G, nc),
        in_specs=[
            pl.BlockSpec((G, C, 1024), lambda b, c: (b, c, OFF_RWKV // 1024)),
            pl.BlockSpec((None, 1, 1024), lambda b, c: (layer, 0, 0)),
            pl.BlockSpec((None, 2, 128, 768), lambda b, c: (layer, 0, 0, 0)),
            pl.BlockSpec((None, 8, 256), lambda b, c: (layer, 0, 0)),
            pl.BlockSpec((C, C), const2),
            pl.BlockSpec((4 * C, 4 * C), const2),
            pl.BlockSpec((2, C, 4 * C), lambda b, c: (0, 0, 0)),
            pl.BlockSpec((256, 256), const2),
            pl.BlockSpec((256, 256), const2),
        ],
        out_specs=pl.BlockSpec((G, C, 256), lambda b, c: (b, c, 0)),
        out_shape=jax.ShapeDtypeStruct((B, S, 256), F32),
        scratch_shapes=[pltpu.VMEM((G, 256, 256), F32), pltpu.VMEM((G, 8, 1024), F32)],
        compiler_params=_cparams(("parallel", "arbitrary")),
        name="rwkv7",
    )(z, mu, lora_w, vec, tri, bdl, wide, segs, segm)


def _merge_kernel(x_ref, zc_ref, on_ref, or_ref, ow_ref, cw_ref, gpre_ref, wg_ref, wb_ref, wo_ref, g_ref,
                  o_ref, carry_ref):
    tm = x_ref.shape[0]
    x = x_ref[...]
    h = _rms(x, gpre_ref[...]).astype(BF16)

    @pl.when(pl.program_id(1) == 0)
    def _():
        carry_ref[...] = jnp.zeros_like(carry_ref)

    zc = zc_ref[...]
    b_g = zc[:, 0:256]
    u = zc[:, 256:512] * zc[:, 512:768]
    prev = carry_ref[...]
    rows = lax.broadcasted_iota(jnp.int32, (tm, 256), 0)
    u1 = jnp.where(rows == 0, prev[7:8], pltpu.roll(u, 1, axis=0))
    u2 = jnp.where(rows == 0, prev[6:7], jnp.where(rows == 1, prev[7:8], pltpu.roll(u, 2, axis=0)))
    carry_ref[...] = u[tm - 8:tm]
    cw = cw_ref[...]
    o_conv = b_g * (cw[0:1] * u2 + cw[1:2] * u1 + cw[2:3] * u)

    branches = (on_ref[...], or_ref[...], ow_ref[...], o_conv)
    merged = jnp.zeros((tm, D_MODEL), F32)
    for m in range(N_BRANCHES):
        gm = jax.nn.sigmoid(_dot(h, wg_ref[:, m * D_MODEL:(m + 1) * D_MODEL]))
        merged = merged + gm * _dot(branches[m].astype(BF16), wb_ref[m])
    y = _dot(merged.astype(BF16), wo_ref[...])
    o_ref[...] = x + _rms(y, g_ref[...])


def merge_mix(x, z, o_nsa, o_ret, o_rwkv, conv_w, g_pre, w_gate, w_branch, w_out, g_post, layer, B, S, tm):
    nt = S // tm
    rowmap = lambda b, i: (b * nt + i, 0)
    lmap = lambda b, i: (layer, 0, 0)
    return pl.pallas_call(
        _merge_kernel,
        grid=(B, nt),
        in_specs=[
            pl.BlockSpec((tm, D_MODEL), rowmap),
            pl.BlockSpec((tm, 768), lambda b, i: (b * nt + i, OFF_CONV // 768)),
            pl.BlockSpec((tm, 256), rowmap),
            pl.BlockSpec((tm, 256), rowmap),
            pl.BlockSpec((tm, 256), rowmap),
            pl.BlockSpec((None, 8, 256), lmap),
            pl.BlockSpec((None, 1, D_MODEL), lmap),
            pl.BlockSpec((None, D_MODEL, N_BRANCHES * D_MODEL), lmap, pipeline_mode=pl.Buffered(1)),
            pl.BlockSpec((None, N_BRANCHES, MIX_WIDTH, D_MODEL), lambda b, i: (layer, 0, 0, 0),
                         pipeline_mode=pl.Buffered(1)),
            pl.BlockSpec((None, D_MODEL, D_MODEL), lmap, pipeline_mode=pl.Buffered(1)),
            pl.BlockSpec((None, 1, D_MODEL), lmap),
        ],
        out_specs=pl.BlockSpec((tm, D_MODEL), rowmap),
        out_shape=jax.ShapeDtypeStruct((B * S, D_MODEL), F32),
        scratch_shapes=[pltpu.VMEM((8, 256), F32)],
        compiler_params=_cparams(("parallel", "arbitrary")),
        name="merge_mix",
    )(x, z, o_nsa, o_ret, o_rwkv, conv_w, g_pre, w_gate, w_branch, w_out, g_post)


def _xattn_kernel(x_ref, kv_ref, gpre_ref, wq_ref, wo_ref, gpost_ref, o_ref):
    n_sub = 1
    rows = x_ref.shape[0] // n_sub

    def part(r0):
        rs = pl.ds(r0, rows)
        x = x_ref[rs, :]
        h = _rms(x, gpre_ref[...]).astype(BF16)
        yield
        q = (_dot(h, wq_ref[...]) * (XA_HEAD_DIM ** -0.5)).astype(BF16)
        yield
        outs = []
        for hd in range(XA_HEADS):
            cs = slice(hd * XA_HEAD_DIM, (hd + 1) * XA_HEAD_DIM)
            kh = kv_ref[:, hd * XA_HEAD_DIM:(hd + 1) * XA_HEAD_DIM]
            vh = kv_ref[:, D_MODEL + hd * XA_HEAD_DIM:D_MODEL + (hd + 1) * XA_HEAD_DIM]
            s = _dot_nt(q[:, cs], kh)
            yield
            e = jnp.exp(s - _rowmax(s))
            p = e / _rowsum(e)
            outs.append(_dot(p.astype(BF16), vh).astype(BF16))
            yield
        o = jnp.concatenate(outs, axis=1)
        y = _dot(o, wo_ref[...])
        yield
        o_ref[rs, :] = x + _rms(y, gpost_ref[...])

    _round_robin([part(i * rows) for i in range(n_sub)])


def cross_attention(x, kv, g_pre, wq, wo, g_post, layer, B, S, M, tm):
    nt = S // tm
    rowmap = lambda b, i: (b * nt + i, 0)
    lmap = lambda b, i: (layer, 0, 0)
    return pl.pallas_call(
        _xattn_kernel,
        grid=(B, nt),
        in_specs=[
            pl.BlockSpec((tm, D_MODEL), rowmap),
            pl.BlockSpec((M, 2 * D_MODEL), lambda b, i: (b, 0)),
            pl.BlockSpec((None, 1, D_MODEL), lmap),
            pl.BlockSpec((None, D_MODEL, D_MODEL), lmap),
            pl.BlockSpec((None, D_MODEL, D_MODEL), lmap),
            pl.BlockSpec((None, 1, D_MODEL), lmap),
        ],
        out_specs=pl.BlockSpec((tm, D_MODEL), rowmap),
        out_shape=jax.ShapeDtypeStruct((B * S, D_MODEL), F32),
        compiler_params=_cparams(("parallel", "arbitrary")),
        name="cross_attention",
    )(x, kv, g_pre, wq, wo, g_post)


def _mlp_kernel(x_ref, gpre_ref, w1_ref, w2_ref, gpost_ref, o_ref):
    x = x_ref[...]
    h = _rms(x, gpre_ref[...]).astype(BF16)
    tf = 1024
    acc = jnp.zeros(x.shape, F32)
    for f in range(D_FF // tf):
        a = jnp.maximum(_dot(h, w1_ref[:, f * tf:(f + 1) * tf]), 0.0)
        acc = acc + _dot((a * a).astype(BF16), w2_ref[f * tf:(f + 1) * tf, :])
    o_ref[...] = x + _rms(acc, gpost_ref[...])


def mlp(x, g_pre, w1, w2, g_post, layer, tm):
    M = x.shape[0]
    lmap = lambda i: (layer, 0, 0)
    return pl.pallas_call(
        _mlp_kernel,
        grid=(M // tm,),
        in_specs=[
            pl.BlockSpec((tm, D_MODEL), lambda i: (i, 0)),
            pl.BlockSpec((None, 1, D_MODEL), lmap),
            pl.BlockSpec((None, D_MODEL, D_FF), lmap, pipeline_mode=pl.Buffered(1)),
            pl.BlockSpec((None, D_FF, D_MODEL), lmap, pipeline_mode=pl.Buffered(1)),
            pl.BlockSpec((None, 1, D_MODEL), lmap),
        ],
        out_specs=pl.BlockSpec((tm, D_MODEL), lambda i: (i, 0)),
        out_shape=jax.ShapeDtypeStruct((M, D_MODEL), F32),
        compiler_params=_cparams(("parallel",)),
        name="mlp",
    )(x, g_pre, w1, w2, g_post)


def _t5_bucket(dist):
    n = jnp.maximum(dist, 0)
    max_exact = N_BUCKETS // 2
    nf = jnp.maximum(n, 1).astype(F32)
    large = max_exact + (jnp.log(nf / max_exact) / math.log(MAX_DISTANCE / max_exact)
                         * (N_BUCKETS - max_exact)).astype(jnp.int32)
    large = jnp.minimum(large, N_BUCKETS - 1)
    return jnp.where(n < max_exact, n, large)


def _nsa_tables(rel_bias, S):
    n_qb = S // Q_BLOCK
    n_cmp = S // NSA_CMP_STRIDE
    n_blk = S // NSA_SLC_LEN
    bias_f = rel_bias.astype(F32)
    i = jnp.arange(Q_BLOCK)
    per_qb = Q_BLOCK // NSA_CMP_STRIDE
    g = jnp.arange(n_cmp + per_qb * (n_qb - 1)) - per_qb * (n_qb - 1)
    d_c = i[None, :] - (g[:, None] * NSA_CMP_STRIDE + NSA_CMP_LEN - 1)
    def lookup(dist):
        hit = _t5_bucket(dist)[..., None, None] == jnp.arange(N_BUCKETS)[:, None]
        return jnp.sum(jnp.where(hit, bias_f, 0.0), axis=-2)

    bias_c = lookup(d_c).transpose(0, 2, 1).reshape(g.shape[0], MIX_HEADS * Q_BLOCK)
    d0 = i[:, None] - i[None, :]
    tab = jnp.stack([lookup(d0), lookup(d0 + Q_BLOCK)]) - bias_f[N_BUCKETS - 1]
    tab = tab.transpose(0, 2, 3, 1).reshape(2, Q_BLOCK, MIX_HEADS * Q_BLOCK)
    cs = np.arange(n_cmp)[None, :] * NSA_CMP_STRIDE
    ss = np.arange(n_blk)[:, None] * NSA_SLC_LEN
    ovt = np.clip(np.minimum(cs + NSA_CMP_LEN, ss + NSA_SLC_LEN) - np.maximum(cs, ss), 0, None)
    ovt = ovt.astype(np.float32) / NSA_CMP_LEN
    ovt[:, (S - NSA_CMP_LEN) // NSA_CMP_STRIDE + 1:] = 0.0
    return bias_c * LOG2E, tab * LOG2E, jnp.asarray(ovt, BF16)


def _ret_tables(S):
    L = RET_CHUNK
    H = MIX_HEADS
    half = HEAD_DIM // 2
    pos = jnp.arange(S)
    inv_freq = ROPE_BASE ** (-jnp.arange(half, dtype=F32) / half)
    ang = pos.astype(F32)[:, None] * inv_freq[None, :]
    cos = jnp.tile(jnp.cos(ang), (1, H))
    sin = jnp.tile(jnp.sin(ang), (1, H))
    lg = jnp.log(1.0 - 2.0 ** (-5.0 - jnp.arange(H, dtype=F32)))
    n = jnp.arange(L, dtype=F32)
    diff = n[:, None] - n[None, :]
    inner = jnp.where(diff >= 0, jnp.exp(jnp.maximum(diff, 0.0)[None] * lg[:, None, None]), 0.0)
    indec = inner.reshape(H * L, L)
    q_decay = jnp.exp((n + 1.0)[None, :] * lg[:, None])
    k_decay = jnp.exp((L - 1.0 - n)[None, :] * lg[:, None])
    chunk_decay = jnp.exp(L * lg)
    lane = np.arange(256)
    head_v = lane // HEAD_DIM
    head_qk = (lane % 128) // half
    qdec = q_decay.T[:, head_v]
    kdec = k_decay.T[:, head_qk]
    cdec = chunk_decay[head_v][None, :]
    bd = jnp.asarray((head_qk[:, None] == head_v[None, :]).astype(np.float32))
    segm = jnp.asarray((head_v[:, None] == head_v[None, :]).astype(np.float32) / HEAD_DIM, BF16)
    return cos, sin, indec, qdec, kdec, cdec, bd, segm


def _rwkv_consts():
    C = RWKV_CHUNK
    t = np.arange(C)
    tri = (t[:, None] >= t[None, :]).astype(np.float32)
    r = np.arange(4 * C)
    bdl = ((r[:, None] // C == r[None, :] // C) & (r[:, None] % C > r[None, :] % C)).astype(np.float32)
    strict = (t[:, None] > (r[None, :] % C)).astype(np.float32)
    incl = (t[:, None] >= (r[None, :] % C)).astype(np.float32)
    lane = np.arange(256) // HEAD_DIM
    seg = (lane[:, None] == lane[None, :]).astype(np.float32)
    return (jnp.asarray(tri, BF16), jnp.asarray(bdl), jnp.asarray(np.stack([strict, incl])),
            jnp.asarray(seg, BF16), jnp.asarray(seg / HEAD_DIM, BF16))


def _hi_lo(w):
    hi = w.astype(BF16)
    lo = (w - hi.astype(F32)).astype(BF16)
    return jnp.stack([hi, lo], axis=1)


def _pad_rows(w, top, total):
    return jnp.pad(w, ((0, 0), (top, total - top - w.shape[1]), (0, 0)))


def _layout_w_in(w_in):
    L = w_in.shape[0]
    o = 0
    nsa_q = w_in[:, :, o:o + 256]; o += 256
    nsa_kv = w_in[:, :, o:o + 384]; o += 384
    nsa_g = w_in[:, :, o:o + 12]; o += 12
    ret = w_in[:, :, o:o + 1024]; o += 1024
    rwkv = w_in[:, :, o:o + RWKV_COLS]; o += RWKV_COLS
    conv = w_in[:, :, o:o + 768]; o += 768
    gate = w_in[:, :, o:o + 4096]

    def rot_perm(w):
        half = HEAD_DIM // 2
        return [w[:, :, h * HEAD_DIM + p * half:h * HEAD_DIM + (p + 1) * half]
                for p in range(2) for h in range(MIX_HEADS)]

    ret = rot_perm(ret[:, :, 0:256]) + rot_perm(ret[:, :, 256:512]) + [ret[:, :, 512:]]
    zeros = lambda n: jnp.zeros((L, D_MODEL, n), w_in.dtype)
    out = jnp.concatenate([conv, nsa_q] + ret + [rwkv, zeros(1024 - RWKV_COLS), nsa_kv, nsa_g,
                                                 zeros(128 - 12)], axis=2)
    assert out.shape[2] == Z_COLS
    return out.astype(BF16), gate.astype(BF16)


def _layout_cmp(cmp_w, cmp_pe):
    L = cmp_w.shape[0]
    wk = cmp_w[:, 0]
    wv = cmp_w[:, 1]
    zero = jnp.zeros_like(wk)
    blk = jnp.concatenate([jnp.concatenate([wk, zero], axis=3), jnp.concatenate([zero, wv], axis=3)], axis=2)
    pe2 = jnp.concatenate([cmp_pe, cmp_pe], axis=2)[:, :, None, :]
    s = NSA_CMP_STRIDE
    return blk[:, :s].astype(BF16), blk[:, s:].astype(BF16), pe2[:, :s], pe2[:, s:]


def kernel(x, mem, ln_mix_pre, w_in, nsa_cmp_w, nsa_cmp_pe, rel_bias, ret_norm_g, rwkv_mu, rwkv_w0, rwkv_w2, rwkv_a0, rwkv_a2, rwkv_g2, rwkv_k_k, rwkv_k_a, rwkv_r_k, rwkv_ln_g, rwkv_ln_b, conv_w, w_branch, w_mix_out, ln_mix_post, ln_xa_pre, ln_mem, xa_wq, xa_wkv, xa_wo, ln_xa_post, ln_mlp_pre, mlp_w1, mlp_w2, ln_mlp_post):
    B, S, D = x.shape
    M = mem.shape[1]
    depth = w_in.shape[0]
    row = lambda g: g[:, None, :]

    w_in_b, w_gate_b = _layout_w_in(w_in)
    cmp_lo, cmp_hi, pe_lo, pe_hi = _layout_cmp(nsa_cmp_w, nsa_cmp_pe)
    nsa_tabs = _nsa_tables(rel_bias, S)
    ret_tabs = _ret_tables(S)
    rwkv_consts = _rwkv_consts()
    mu = jnp.pad(rwkv_mu, ((0, 0), (0, 1024 - RWKV_COLS)))[:, None, :]
    lora_w = _hi_lo(jnp.concatenate([_pad_rows(rwkv_w2, 0, 128), _pad_rows(rwkv_a2, RWKV_DECAY_LORA, 128),
                                     _pad_rows(rwkv_g2, RWKV_DECAY_LORA + RWKV_AAA_LORA, 128)], axis=2))
    vec = jnp.stack([rwkv_w0, rwkv_a0, rwkv_k_k, rwkv_k_a, rwkv_r_k, rwkv_ln_g, rwkv_ln_b,
                     jnp.zeros_like(rwkv_w0)], axis=1)
    conv_p = jnp.pad(conv_w, ((0, 0), (0, 8 - CONV_WIDTH), (0, 0)))
    w_branch_b = w_branch.astype(BF16)
    w_mix_out_b = w_mix_out.astype(BF16)
    xa_wq_b = xa_wq.astype(BF16)
    xa_wkv_b = xa_wkv.astype(BF16)
    xa_wo_b = xa_wo.astype(BF16)
    mlp_w1_b = mlp_w1.astype(BF16)
    mlp_w2_b = mlp_w2.astype(BF16)

    xf = x.reshape(B * S, D)
    memf = mem.reshape(B * M, D)
    for l in range(depth):
        z = norm_matmul(xf, row(ln_mix_pre), w_in_b, l, tm=512, tn=Z_COLS, out_dtype=F32)
        prep = nsa_compress(z, cmp_lo, cmp_hi, pe_lo, pe_hi, l, B, S)
        z3 = z.reshape(B, S, Z_COLS)
        o_nsa = nsa_attention(z3, prep, *nsa_tabs, B, S, G=8).reshape(B * S, MIX_WIDTH)
        o_ret = retention(z3, ret_tabs, row(ret_norm_g), l, B, S, G=4).reshape(B * S, MIX_WIDTH)
        o_rwkv = rwkv7(z3, mu, lora_w, vec, rwkv_consts, l, B, S, G=4).reshape(B * S, MIX_WIDTH)
        xf = merge_mix(xf, z, o_nsa, o_ret, o_rwkv, conv_p, row(ln_mix_pre), w_gate_b, w_branch_b,
                       w_mix_out_b, row(ln_mix_post), l, B, S, tm=512)
        kvm = norm_matmul(memf, row(ln_mem), xa_wkv_b, l, tm=2 * M, tn=2 * D_MODEL, out_dtype=BF16)
        xf = cross_attention(xf, kvm, row(ln_xa_pre), xa_wq_b, xa_wo_b, row(ln_xa_post), l, B, S, M, tm=1024)
        xf = mlp(xf, row(ln_mlp_pre), mlp_w1_b, mlp_w2_b, row(ln_mlp_post), l, tm=512)
    return xf.reshape(B, S, D)
```

```python
import functools
import math

import numpy as np
import jax
import jax.numpy as jnp
from jax import lax
from jax.experimental import pallas as pl
from jax.experimental.pallas import tpu as pltpu

F32 = jnp.float32
BF16 = jnp.bfloat16

D_MODEL = 1024
N_BRANCHES = 4
MIX_WIDTH = 256
HEAD_DIM = 64
MIX_HEADS = 4

NSA_CMP_LEN = 32
NSA_CMP_STRIDE = 16
NSA_SLC_LEN = 64
NSA_TOP_N = 8
NSA_WINDOW = 512
Q_BLOCK = 128
NSA_V_ROWS = HEAD_DIM + 16
FORCE_BONUS = 1e4
N_BUCKETS = 32
MAX_DISTANCE = 128

RET_CHUNK = 128
ROPE_BASE = 10000.0
RET_NORM_EPS = 1e-5

RWKV_DECAY_LORA = 32
RWKV_AAA_LORA = 32
RWKV_GATE_LORA = 64
RWKV_GN_EPS = 64e-5
RWKV_COLS = 3 * MIX_WIDTH + RWKV_DECAY_LORA + RWKV_AAA_LORA + RWKV_GATE_LORA
RWKV_CHUNK = 64

CONV_WIDTH = 3
XA_HEADS = 4
XA_HEAD_DIM = D_MODEL // XA_HEADS
D_FF = 4 * D_MODEL

RMS_EPS = 1e-6
LOG2E = math.log2(math.e)
NEG_INF = -1e30
NEG_BIG = -3e38

OFF_CONV = 0
OFF_NSAQ = 768
OFF_RET = 1024
OFF_RWKV = 2048
OFF_NSAKV = 3072
OFF_NSAG = 3456
Z_COLS = 3584

VMEM_LIMIT = 56 * 1024 * 1024


def _cparams(sem):
    return pltpu.CompilerParams(dimension_semantics=sem, vmem_limit_bytes=VMEM_LIMIT)


def _dot(a, b):
    return jnp.dot(a, b, preferred_element_type=F32)


def _dot_nt(a, b):
    return lax.dot_general(a, b, (((1,), (1,)), ((), ())), preferred_element_type=F32)


def _dot_tn(a, b):
    return lax.dot_general(a, b, (((0,), (0,)), ((), ())), preferred_element_type=F32)


def _split2(x):
    hi = x.astype(BF16)
    lo = (x - hi.astype(F32)).astype(BF16)
    return hi, lo


def _dot_x2(x, w_bf16, *more):
    xs = (x,) + more
    rows = x.shape[0]
    parts = [p for xi in xs for p in _split2(xi)]
    out = _dot(jnp.concatenate(parts, axis=0), w_bf16)
    res = [out[2 * i * rows:(2 * i + 1) * rows] + out[(2 * i + 1) * rows:(2 * i + 2) * rows]
           for i in range(len(xs))]
    return res[0] if not more else res


def _rms(x, g):
    ms = jnp.mean(x * x, axis=-1, keepdims=True)
    return x * lax.rsqrt(ms + RMS_EPS) * g


def _norm_matmul_kernel(x_ref, g_ref, w_ref, o_ref, h_ref):
    @pl.when(pl.program_id(1) == 0)
    def _():
        h_ref[...] = _rms(x_ref[...], g_ref[...]).astype(BF16)

    o_ref[...] = _dot(h_ref[...], w_ref[...]).astype(o_ref.dtype)


def norm_matmul(x, g, w, layer, tm, tn, out_dtype):
    M, D = x.shape
    N = w.shape[2]
    w_mode = dict(pipeline_mode=pl.Buffered(1)) if tn == N else {}
    return pl.pallas_call(
        _norm_matmul_kernel,
        grid=(M // tm, N // tn),
        in_specs=[
            pl.BlockSpec((tm, D), lambda i, j: (i, 0)),
            pl.BlockSpec((None, 1, D), lambda i, j: (layer, 0, 0)),
            pl.BlockSpec((None, D, tn), lambda i, j: (layer, 0, j), **w_mode),
        ],
        out_specs=pl.BlockSpec((tm, tn), lambda i, j: (i, j)),
        out_shape=jax.ShapeDtypeStruct((M, N), out_dtype),
        scratch_shapes=[pltpu.VMEM((tm, D), BF16)],
        compiler_params=_cparams(("parallel", "arbitrary")),
        name="norm_matmul",
    )(x, g, w)


def _nsa_compress_kernel(kvc_ref, kvs_ref, kvw_ref, wlo_ref, whi_ref, pelo_ref, pehi_ref,
                         kc_ref, vct_ref, ks_ref, vst_ref, kw_ref, vwt_ref):
    nblk = kc_ref.shape[0]
    n_kt = vwt_ref.shape[0]
    ylo = jnp.zeros((nblk, 128), F32)
    yhi = jnp.zeros((nblk, 128), F32)
    for r in range(NSA_CMP_STRIDE):
        xr = kvc_ref[pl.ds(r, nblk, stride=NSA_CMP_STRIDE), :]
        ylo = ylo + _dot((xr + pelo_ref[r]).astype(BF16), wlo_ref[r])
        yhi = yhi + _dot((xr + pehi_ref[r]).astype(BF16), whi_ref[r])
    y = ylo + pltpu.roll(yhi, nblk - 1, axis=0)
    kc_ref[...] = y.astype(BF16)
    vct_ref[...] = y.T[HEAD_DIM:, :].astype(BF16)
    kw_ref[...] = kvw_ref[...].astype(BF16)
    ones = jnp.ones((NSA_V_ROWS - HEAD_DIM, 128), BF16)
    for kt in range(n_kt):
        rs = slice(kt * 128, (kt + 1) * 128)
        vst_ref[kt] = jnp.concatenate([kvs_ref[rs, :].T[HEAD_DIM:, :].astype(BF16), ones], axis=0)
        vwt_ref[kt] = jnp.concatenate([kvw_ref[rs, :].T[HEAD_DIM:, :].astype(BF16), ones], axis=0)
    S = kvs_ref.shape[0]
    key = lax.broadcasted_iota(jnp.int32, (S, 128), 0)
    lane = lax.broadcasted_iota(jnp.int32, (S, 128), 1)
    onehot = (lane == HEAD_DIM + key // NSA_SLC_LEN).astype(F32)
    ks_ref[0:S, :] = jnp.where(lane < HEAD_DIM, kvs_ref[...], onehot).astype(BF16)
    pad_lane = lax.broadcasted_iota(jnp.int32, (128, 128), 1)
    ks_ref[S:S + 128, :] = (pad_lane == HEAD_DIM + S // NSA_SLC_LEN).astype(BF16)
    vst_ref[n_kt] = jnp.zeros((NSA_V_ROWS, 128), BF16)


def nsa_compress(z, wlo, whi, pelo, pehi, layer, B, S):
    nblk = S // NSA_CMP_STRIDE
    n_kt = S // 128
    c0 = OFF_NSAKV // 128
    wmap = lambda b: (layer, 0, 0, 0)
    b3 = lambda b: (b, 0, 0)
    b4 = lambda b: (b, 0, 0, 0)
    return pl.pallas_call(
        _nsa_compress_kernel,
        grid=(B,),
        in_specs=[
            pl.BlockSpec((S, 128), lambda b: (b, c0)),
            pl.BlockSpec((S, 128), lambda b: (b, c0 + 1)),
            pl.BlockSpec((S, 128), lambda b: (b, c0 + 2)),
            pl.BlockSpec((None, NSA_CMP_STRIDE, 128, 128), wmap),
            pl.BlockSpec((None, NSA_CMP_STRIDE, 128, 128), wmap),
            pl.BlockSpec((None, NSA_CMP_STRIDE, 1, 128), wmap),
            pl.BlockSpec((None, NSA_CMP_STRIDE, 1, 128), wmap),
        ],
        out_specs=[
            pl.BlockSpec((None, nblk, 128), b3),
            pl.BlockSpec((None, HEAD_DIM, nblk), b3),
            pl.BlockSpec((None, S + 128, 128), b3),
            pl.BlockSpec((None, n_kt + 1, NSA_V_ROWS, 128), b4),
            pl.BlockSpec((None, S, 128), b3),
            pl.BlockSpec((None, n_kt, NSA_V_ROWS, 128), b4),
        ],
        out_shape=[
            jax.ShapeDtypeStruct((B, nblk, 128), BF16),
            jax.ShapeDtypeStruct((B, HEAD_DIM, nblk), BF16),
            jax.ShapeDtypeStruct((B, S + 128, 128), BF16),
            jax.ShapeDtypeStruct((B, n_kt + 1, NSA_V_ROWS, 128), BF16),
            jax.ShapeDtypeStruct((B, S, 128), BF16),
            jax.ShapeDtypeStruct((B, n_kt, NSA_V_ROWS, 128), BF16),
        ],
        compiler_params=_cparams(("parallel",)),
        name="nsa_compress",
    )(z, z, z, wlo, whi, pelo, pehi)


def _rowmax(s):
    return jnp.max(s, axis=-1, keepdims=True)


def _rowsum(s):
    return jnp.sum(s, axis=-1, keepdims=True)


def _colmax(s):
    return jnp.max(s, axis=0, keepdims=True)


def _colsum(s):
    return jnp.sum(s, axis=0, keepdims=True)


def _nsa_kernel(q_ref, gl_ref, kc_ref, vct_ref, ks_ref, vst_ref, kw_ref, vwt_ref, bc_ref, tab_ref,
                ovt_ref, o_ref):
    bi = pl.program_id(1)
    G = q_ref.shape[0]
    n_kt = vwt_ref.shape[1]
    n_blk = ovt_ref.shape[0]
    n_cmp = kc_ref.shape[1]
    QB = Q_BLOCK
    R = MIX_HEADS * QB
    n_off = NSA_WINDOW // 128

    lo = jnp.maximum(bi - n_off, 0)
    q4 = [None] * G
    q4_sel = [None] * G
    o_c = [None] * G
    carry_s = [None] * G
    carry_w = [None] * G

    def key_tile(kref, g, kt, lanes):
        return kref[g, pl.ds(pl.multiple_of(kt * 128, 128), 128), lanes]

    def pair_qk(g, k0, k1):
        keys = jnp.concatenate([key_tile(kw_ref, g, k0, slice(0, HEAD_DIM)),
                                key_tile(kw_ref, g, k1, slice(0, HEAD_DIM))], axis=0)
        qk = _dot(keys, q4[g])
        return qk[:128], qk[128:]

    def pair_qk_sel(g, k0, k1):
        keys = jnp.concatenate([key_tile(ks_ref, g, k0, slice(None)), key_tile(ks_ref, g, k1, slice(None))],
                               axis=0)
        qk = _dot(keys, q4_sel[g])
        return qk[:128], qk[128:]

    def head_cols(x, h):
        return x[:, h * QB:(h + 1) * QB]

    key1 = lax.broadcasted_iota(jnp.int32, (128, QB), 0)
    i1 = lax.broadcasted_iota(jnp.int32, (128, QB), 1)
    causal1 = key1 <= i1
    kp = jnp.maximum(bi - 1, 0)
    pen_prev = jnp.where(bi >= 1, 0.0, NEG_INF)

    def win_add(kt, valid):
        shift = jnp.where(valid, jnp.where(bi - kt < n_off, -4096, 0), 4096)
        return jnp.where(key1 > i1 + shift, 0.0, NEG_INF)

    def diag_add(extra):
        return [jnp.where(causal1, head_cols(tab_ref[0], h) + extra, NEG_INF) for h in range(MIX_HEADS)]

    def prev_add(extra):
        return [head_cols(tab_ref[1], h) + extra for h in range(MIX_HEADS)]

    def pair_update(carry, qk0, add0, vt0, qk1, add1, vt1):
        p0s, p1s, ms, alphas = [], [], [], []
        for h in range(MIX_HEADS):
            a0 = head_cols(qk0, h) if add0 is None else head_cols(qk0, h) + add0[h]
            a1 = head_cols(qk1, h) if add1 is None else head_cols(qk1, h) + add1[h]
            m_new = jnp.maximum(_colmax(a0), _colmax(a1))
            if carry is not None:
                m_old = head_cols(carry[0], h)
                m_new = jnp.maximum(m_old, m_new)
                alphas.append(jnp.exp2(m_old - m_new))
            ms.append(m_new)
            p0s.append(jnp.exp2(a0 - m_new).astype(BF16))
            p1s.append(jnp.exp2(a1 - m_new).astype(BF16))
        yield
        probs = jnp.concatenate([jnp.concatenate(p0s, axis=1), jnp.concatenate(p1s, axis=1)], axis=0)
        acc = _dot(jnp.concatenate([vt0, vt1], axis=1), probs)
        if carry is not None:
            acc = jnp.concatenate(alphas, axis=1) * carry[1] + acc
        return jnp.concatenate(ms, axis=1), acc

    def head_part(g):
        qt = (q_ref[g] * (HEAD_DIM ** -0.5 * LOG2E)).T
        q4[g] = jnp.concatenate([qt[h * HEAD_DIM:(h + 1) * HEAD_DIM] for h in range(MIX_HEADS)],
                                axis=1).astype(BF16)
        yield
        qk_d, qk_p = pair_qk(g, bi, kp)
        carry_w[g] = yield from pair_update(None, qk_d, diag_add(0.0), vwt_ref[g, bi],
                                            qk_p, prev_add(pen_prev), vwt_ref[g, kp])
        yield
        g0 = pl.multiple_of((n_kt - 1 - bi) * 8, 8)
        qk = _dot(kc_ref[g, :, :HEAD_DIM], q4[g])
        nrow = lax.broadcasted_iota(jnp.int32, (n_cmp, QB), 0)
        tq = bi * QB + lax.broadcasted_iota(jnp.int32, (n_cmp, QB), 1)
        vis = tq - (nrow * NSA_CMP_STRIDE + (NSA_CMP_LEN - 1)) >= 0
        live = (tq >= NSA_CMP_LEN - 1).astype(F32)
        yield
        p_cs = []
        for h in range(MIX_HEADS):
            s = jnp.where(vis, head_cols(qk, h) + bc_ref[pl.ds(g0, n_cmp), h * QB:(h + 1) * QB], NEG_INF)
            e = jnp.exp2(s - _colmax(s))
            p_cs.append(e / _colsum(e) * live)
        yield
        o_c[g] = _dot(vct_ref[g], jnp.concatenate(p_cs, axis=1).astype(BF16))
        p_sum = p_cs[0] + p_cs[1] + p_cs[2] + p_cs[3]
        p_hi, p_lo = _split2(p_sum)
        imp = _dot(ovt_ref[...], p_hi) + _dot(ovt_ref[...], p_lo)
        yield
        blk = lax.broadcasted_iota(jnp.int32, (n_blk, QB), 0)
        cur = (bi * QB + lax.broadcasted_iota(jnp.int32, (n_blk, QB), 1)) >> 6
        forced = (blk == 0) | (blk == cur) | (blk == cur - 1)
        imp = jnp.where(forced, imp + FORCE_BONUS, imp)
        imp = jnp.where(blk <= cur, imp, NEG_INF)
        blk_f = blk.astype(F32)
        sel = jnp.zeros((n_blk, QB), F32)
        for _ in range(NSA_TOP_N):
            mx = _colmax(imp)
            idx = jnp.min(jnp.where(imp == mx, blk_f, 4096.0), axis=0, keepdims=True)
            pick = blk_f == idx
            sel = jnp.where(pick, 1.0, sel)
            imp = jnp.where(pick, NEG_BIG, imp)
            yield
        sel_add = (sel - 1.0) * (-NEG_INF)
        rows_left = 128 - HEAD_DIM - n_blk
        pad_rows = jnp.where(lax.broadcasted_iota(jnp.int32, (rows_left, QB), 0) == 0, NEG_INF, 0.0)
        mask_rows = jnp.concatenate([sel_add, pad_rows], axis=0).astype(BF16)
        q4_sel[g] = jnp.concatenate([q4[g], jnp.concatenate([mask_rows] * MIX_HEADS, axis=1)], axis=0)
        yield
        qk_d, qk_p = pair_qk_sel(g, bi, kp)
        carry_s[g] = yield from pair_update(None, qk_d, diag_add(0.0), vst_ref[g, bi],
                                            qk_p, prev_add(pen_prev), vst_ref[g, kp])

    _round_robin([head_part(g) for g in range(G)])

    n_old = jnp.maximum(bi - 1, 0)

    def sel_step(g, j, carry, out):
        k0 = 2 * j
        k1 = jnp.where(k0 + 1 < n_old, k0 + 1, n_kt)
        qk0, qk1 = pair_qk_sel(g, k0, k1)
        yield
        out[g] = yield from pair_update(carry, qk0, None, vst_ref[g, k0], qk1, None, vst_ref[g, k1])

    def win_step(g, j, carry, out):
        k0 = lo + 2 * j
        k1 = jnp.minimum(k0 + 1, bi)
        qk0, qk1 = pair_qk(g, k0, k1)
        yield
        out[g] = yield from pair_update(carry, qk0, [win_add(k0, True)] * MIX_HEADS, vwt_ref[g, k0],
                                        qk1, [win_add(k1, k0 + 1 < n_old)] * MIX_HEADS, vwt_ref[g, k1])

    def joint(step):
        def body(j, carries):
            out = [None] * G
            _round_robin([step(g, j, carries[g], out) for g in range(G)])
            return tuple(out)
        return body

    res_s = lax.fori_loop(0, (n_old + 1) // 2, joint(sel_step), tuple(carry_s))
    res_w = lax.fori_loop(0, (n_old - lo + 1) // 2, joint(win_step), tuple(carry_w))

    def tail_part(g):
        acc_s = res_s[g][1]
        acc_w = res_w[g][1]
        o_s = acc_s[:HEAD_DIM] * (1.0 / acc_s[HEAD_DIM:HEAD_DIM + 1])
        o_w = acc_w[:HEAD_DIM] * (1.0 / acc_w[HEAD_DIM:HEAD_DIM + 1])
        gate = jax.nn.sigmoid(gl_ref[g].T)
        yield
        outs = []
        for h in range(MIX_HEADS):
            cs = slice(h * QB, (h + 1) * QB)
            outs.append(gate[3 * h:3 * h + 1] * o_c[g][:, cs] + gate[3 * h + 1:3 * h + 2] * o_s[:, cs]
                        + gate[3 * h + 2:3 * h + 3] * o_w[:, cs])
        o_ref[g] = jnp.concatenate(outs, axis=0).T

    _round_robin([tail_part(g) for g in range(G)])


def nsa_attention(z, prep, bias_c, tab, ovt, B, S, G):
    n_qb = S // Q_BLOCK
    n_kt = S // 128
    n_cmp = S // NSA_CMP_STRIDE
    R = MIX_HEADS * Q_BLOCK
    kc, vct, ks, vst, kw, vwt = prep
    b3 = lambda b, i: (b, 0, 0)
    b4 = lambda b, i: (b, 0, 0, 0)
    return pl.pallas_call(
        _nsa_kernel,
        grid=(B // G, n_qb),
        in_specs=[
            pl.BlockSpec((G, Q_BLOCK, MIX_WIDTH), lambda b, i: (b, i, OFF_NSAQ // MIX_WIDTH)),
            pl.BlockSpec((G, Q_BLOCK, 128), lambda b, i: (b, i, OFF_NSAG // 128)),
            pl.BlockSpec((G, n_cmp, 128), b3),
            pl.BlockSpec((G, HEAD_DIM, n_cmp), b3),
            pl.BlockSpec((G, S + 128, 128), b3),
            pl.BlockSpec((G, n_kt + 1, NSA_V_ROWS, 128), b4),
            pl.BlockSpec((G, S, 128), b3),
            pl.BlockSpec((G, n_kt, NSA_V_ROWS, 128), b4),
            pl.BlockSpec(bias_c.shape, lambda b, i: (0, 0)),
            pl.BlockSpec((2, 128, R), lambda b, i: (0, 0, 0)),
            pl.BlockSpec(ovt.shape, lambda b, i: (0, 0)),
        ],
        out_specs=pl.BlockSpec((G, Q_BLOCK, MIX_WIDTH), lambda b, i: (b, i, 0)),
        out_shape=jax.ShapeDtypeStruct((B, S, MIX_WIDTH), F32),
        compiler_params=_cparams(("parallel", "arbitrary")),
        name="nsa_attention",
    )(z, z, kc, vct, ks, vst, kw, vwt, bias_c, tab, ovt)


def _stack_heads(x, head_of_lane):
    return jnp.concatenate([jnp.where(head_of_lane == h, x, jnp.zeros_like(x))
                            for h in range(MIX_HEADS)], axis=0)


def _head_norm(y, seg_mean, eps):
    mu = _dot_x2(y, seg_mean)
    d = y - mu
    var = _dot_x2(d * d, seg_mean)
    return d * lax.rsqrt(var + eps)


def _round_robin(chains):
    while chains:
        chains = [c for c in chains if next(c, StopIteration) is not StopIteration]


def _retention_kernel(z_ref, cos_ref, sin_ref, indec_ref, qdec_ref, kdec_ref, cdec_ref, bd_ref,
                      segm_ref, ng_ref, o_ref, st_ref):
    @pl.when(pl.program_id(1) == 0)
    def _():
        st_ref[...] = jnp.zeros_like(st_ref)

    _round_robin([_retention_chunk(z_ref.at[g], cos_ref, sin_ref, indec_ref, qdec_ref, kdec_ref, cdec_ref,
                                   bd_ref, segm_ref, ng_ref, o_ref.at[g], st_ref.at[g])
                  for g in range(z_ref.shape[0])])


def _retention_chunk(z_ref, cos_ref, sin_ref, indec_ref, qdec_ref, kdec_ref, cdec_ref, bd_ref,
                     segm_ref, ng_ref, o_ref, st_ref):
    L = RET_CHUNK
    z = z_ref[...]
    q = z[:, 0:256]
    k = z[:, 256:512]
    v = z[:, 512:768]
    g = z[:, 768:1024]
    cos = cos_ref[...]
    sin = sin_ref[...]

    def rot(u):
        u1 = u[:, :128]
        u2 = u[:, 128:]
        return jnp.concatenate([u1 * cos - u2 * sin, u2 * cos + u1 * sin], axis=1)

    qr = rot(q) * (HEAD_DIM ** -0.5)
    kr = rot(k)
    lane = lax.broadcasted_iota(jnp.int32, (L, 256), 1)
    head_qk = (lane & 127) >> 5
    head_v = lane >> 6
    qb = qr.astype(BF16)
    kb = kr.astype(BF16)
    vb = v.astype(BF16)
    yield

    att = _dot_nt(_stack_heads(qb, head_qk), kb) * indec_ref[...]
    yield
    o_st = _dot(att.astype(BF16), vb)
    o = jnp.zeros((L, 256), F32)
    for h in range(MIX_HEADS):
        o = o + jnp.where(head_v == h, o_st[h * L:(h + 1) * L], 0.0)
    yield
    state = st_ref[...]
    o = o + _dot(qb, state.astype(BF16)) * qdec_ref[...]
    st_ref[...] = state * cdec_ref[...] + _dot_tn((kr * kdec_ref[...]).astype(BF16), vb) * bd_ref[...]
    yield

    mu = _dot_x2(o, segm_ref[...])
    yield
    d = o - mu
    var = _dot_x2(d * d, segm_ref[...])
    yield
    o_ref[...] = g * jax.nn.sigmoid(g) * (d * lax.rsqrt(var + RET_NORM_EPS) * ng_ref[...])


def retention(z, tabs, ng, layer, B, S, G):
    L = RET_CHUNK
    nc = S // L
    cos, sin, indec, qdec, kdec, cdec, bd, segm = tabs
    const2 = lambda b, c: (0, 0)
    return pl.pallas_call(
        _retention_kernel,
        grid=(B // G, nc),
        in_specs=[
            pl.BlockSpec((G, L, 1024), lambda b, c: (b, c, OFF_RET // 1024)),
            pl.BlockSpec((L, 128), lambda b, c: (c, 0)),
            pl.BlockSpec((L, 128), lambda b, c: (c, 0)),
            pl.BlockSpec((MIX_HEADS * L, L), const2),
            pl.BlockSpec((L, 256), const2),
            pl.BlockSpec((L, 256), const2),
            pl.BlockSpec((1, 256), const2),
            pl.BlockSpec((256, 256), const2),
            pl.BlockSpec((256, 256), const2),
            pl.BlockSpec((None, 1, 256), lambda b, c: (layer, 0, 0)),
        ],
        out_specs=pl.BlockSpec((G, L, 256), lambda b, c: (b, c, 0)),
        out_shape=jax.ShapeDtypeStruct((B, S, 256), F32),
        scratch_shapes=[pltpu.VMEM((G, 256, 256), F32)],
        compiler_params=_cparams(("parallel", "arbitrary")),
        name="retention",
    )(z, cos, sin, indec, qdec, kdec, cdec, bd, segm, ng)


def _rwkv_kernel(z_ref, mu_ref, lw_ref, vec_ref, tri_ref, bdl_ref, wide_ref,
                 segs_ref, segm_ref, o_ref, st_ref, prev_ref):
    @pl.when(pl.program_id(1) == 0)
    def _():
        st_ref[...] = jnp.zeros_like(st_ref)
        prev_ref[...] = jnp.zeros_like(prev_ref)

    _round_robin([_rwkv_chunk(z_ref.at[g], mu_ref, lw_ref, vec_ref, tri_ref, bdl_ref,
                              wide_ref, segs_ref, segm_ref, o_ref.at[g], st_ref.at[g], prev_ref.at[g])
                  for g in range(z_ref.shape[0])])


def _rwkv_chunk(z_ref, mu_ref, lw_ref, vec_ref, tri_ref, bdl_ref, wide_ref,
                segs_ref, segm_ref, o_ref, st_ref, prev_ref):
    C = RWKV_CHUNK
    z = z_ref[...]
    rows = lax.broadcasted_iota(jnp.int32, (C, 1024), 0)
    zs = jnp.where(rows == 0, prev_ref[0:1, :], pltpu.roll(z, 1, axis=0))
    prev_ref[...] = jnp.broadcast_to(z[C - 1:C, :], prev_ref.shape)
    zf = z + (zs - z) * mu_ref[...]
    r = zf[:, 0:256]
    k = zf[:, 256:512]
    v = zf[:, 512:768]
    lora = zf[:, 768:896]

    vec = vec_ref[...]
    w0, a0, k_k, k_a, r_k, ln_g, ln_b = (vec[i:i + 1] for i in range(7))

    lane = lax.broadcasted_iota(jnp.int32, lora.shape, 1)
    lora_in = jnp.where(lane < RWKV_DECAY_LORA, jnp.tanh(lora),
                        jnp.where(lane < RWKV_DECAY_LORA + RWKV_AAA_LORA, lora, jax.nn.sigmoid(lora)))
    hi, lo = _split2(lora_in)
    both = _dot(jnp.concatenate([hi, lo], axis=0), lw_ref[0])
    lora_out = both[:C] + both[C:] + _dot(hi, lw_ref[1])
    wpre = w0 + lora_out[:, 0:256]
    y = -wpre
    softplus = jnp.maximum(y, 0.0) + jnp.log(1.0 + jnp.exp(-jnp.abs(y)))
    w_log = -softplus - 0.5
    ld = -jnp.exp(w_log)
    a = jax.nn.sigmoid(a0 + lora_out[:, 256:512])
    gate = lora_out[:, 512:768]
    kk = k * k_k
    k2 = k * (1.0 + (a - 1.0) * k_a)
    kk_sq, bonus = _dot_x2(kk * kk, segs_ref[...], r * k2 * r_k)
    kk = kk / jnp.maximum(jnp.sqrt(kk_sq), 1e-12)
    yield

    tri = tri_ref[...]
    l1 = ld.astype(BF16)
    r1 = ld - l1.astype(F32)
    l2 = r1.astype(BF16)
    l3 = (r1 - l2.astype(F32)).astype(BF16)
    cs = _dot(tri, l1) + _dot(tri, l2) + _dot(tri, l3)
    cs_end = cs[C - 1:C, :]
    yield
    e_neg = jnp.exp(-cs)
    e_end = jnp.exp(cs_end - cs)
    kka = kk * a
    a_t = -kk * jnp.exp(cs - ld)
    r_t = r * jnp.exp(cs)
    b_t = kka * e_neg
    k_t = k2 * e_neg
    b_g = kka * e_end
    k_g = k2 * e_end

    lane = lax.broadcasted_iota(jnp.int32, (C, 256), 1)
    head = lane >> 6
    stack = lambda x: _stack_heads(x.astype(BF16), head)
    a_st, b_st, k_st, v_st = stack(a_t), stack(b_t), stack(k_t), stack(v)

    n_pow = _dot_nt(a_st, b_st) * bdl_ref[...]
    ri = lax.broadcasted_iota(jnp.int32, (4 * C, 4 * C), 0)
    ci = lax.broadcasted_iota(jnp.int32, (4 * C, 4 * C), 1)
    t_inv = jnp.where(ri == ci, 1.0, 0.0) + n_pow
    yield

    ar = jnp.concatenate([a_t, r_t], axis=0).astype(BF16)
    bk = jnp.concatenate([b_st, k_st], axis=0)
    wide = _dot_nt(ar, bk)
    strict = wide_ref[0]
    incl = wide_ref[1]
    a_ak = (wide[0:C, 4 * C:] * strict).astype(BF16)
    a_rb = (wide[C:, 0:4 * C] * incl).astype(BF16)
    a_rk = (wide[C:, 4 * C:] * incl).astype(BF16)
    yield

    sq = 2
    while sq < C:
        nb = n_pow.astype(BF16)
        n_pow = _dot(nb, nb)
        yield
        t_inv = t_inv + _dot(t_inv.astype(BF16), n_pow.astype(BF16))
        sq *= 2
    yield

    state = st_ref[...]
    sb = state.astype(BF16)
    x = _dot_nt(a_t.astype(BF16), sb) + _dot(a_ak, v_st)
    yield
    u_st = _dot(t_inv.astype(BF16), stack(x)).astype(BF16)
    yield
    yv = _dot_nt(r_t.astype(BF16), sb) + _dot(a_rb, u_st) + _dot(a_rk, v_st)
    uv = jnp.concatenate([u_st, v_st], axis=0)
    bkg = jnp.concatenate([stack(b_g), stack(k_g)], axis=0)
    st_ref[...] = state * jnp.exp(cs_end) + _dot_tn(uv, bkg)
    yield

    yn = _head_norm(yv, segm_ref[...], RWKV_GN_EPS) * ln_g + ln_b
    yn = yn + bonus * v
    o_ref[...] = yn * gate


def rwkv7(z, mu, lora_w, vec, consts, layer, B, S, G):
    C = RWKV_CHUNK
    nc = S // C
    tri, bdl, wide, segs, segm = consts
    const2 = lambda b, c: (0, 0)
    return pl.pallas_call(
        _rwkv_kernel,
        grid=(B // G, nc),
        in_specs=[
            pl.BlockSpec((G, C, 1024), lambda b, c: (b, c, OFF_RWKV // 1024)),
            pl.BlockSpec((None, 1, 1024), lambda b, c: (layer, 0, 0)),
            pl.BlockSpec((None, 2, 128, 768), lambda b, c: (layer, 0, 0, 0)),
            pl.BlockSpec((None, 8, 256), lambda b, c: (layer, 0, 0)),
            pl.BlockSpec((C, C), const2),
            pl.BlockSpec((4 * C, 4 * C), const2),
            pl.BlockSpec((2, C, 4 * C), lambda b, c: (0, 0, 0)),
            pl.BlockSpec((256, 256), const2),
            pl.BlockSpec((256, 256), const2),
        ],
        out_specs=pl.BlockSpec((G, C, 256), lambda b, c: (b, c, 0)),
        out_shape=jax.ShapeDtypeStruct((B, S, 256), F32),
        scratch_shapes=[pltpu.VMEM((G, 256, 256), F32), pltpu.VMEM((G, 8, 1024), F32)],
        compiler_params=_cparams(("parallel", "arbitrary")),
        name="rwkv7",
    )(z, mu, lora_w, vec, tri, bdl, wide, segs, segm)


def _merge_kernel(x_ref, zc_ref, on_ref, or_ref, ow_ref, cw_ref, gpre_ref, wg_ref, wb_ref, wo_ref, g_ref,
                  o_ref, carry_ref):
    tm = x_ref.shape[0]
    x = x_ref[...]
    h = _rms(x, gpre_ref[...]).astype(BF16)

    @pl.when(pl.program_id(1) == 0)
    def _():
        carry_ref[...] = jnp.zeros_like(carry_ref)

    zc = zc_ref[...]
    b_g = zc[:, 0:256]
    u = zc[:, 256:512] * zc[:, 512:768]
    prev = carry_ref[...]
    rows = lax.broadcasted_iota(jnp.int32, (tm, 256), 0)
    u1 = jnp.where(rows == 0, prev[7:8], pltpu.roll(u, 1, axis=0))
    u2 = jnp.where(rows == 0, prev[6:7], jnp.where(rows == 1, prev[7:8], pltpu.roll(u, 2, axis=0)))
    carry_ref[...] = u[tm - 8:tm]
    cw = cw_ref[...]
    o_conv = b_g * (cw[0:1] * u2 + cw[1:2] * u1 + cw[2:3] * u)

    branches = (on_ref[...], or_ref[...], ow_ref[...], o_conv)
    merged = jnp.zeros((tm, D_MODEL), F32)
    for m in range(N_BRANCHES):
        gm = jax.nn.sigmoid(_dot(h, wg_ref[:, m * D_MODEL:(m + 1) * D_MODEL]))
        merged = merged + gm * _dot(branches[m].astype(BF16), wb_ref[m])
    y = _dot(merged.astype(BF16), wo_ref[...])
    o_ref[...] = x + _rms(y, g_ref[...])


def merge_mix(x, z, o_nsa, o_ret, o_rwkv, conv_w, g_pre, w_gate, w_branch, w_out, g_post, layer, B, S, tm):
    nt = S // tm
    rowmap = lambda b, i: (b * nt + i, 0)
    lmap = lambda b, i: (layer, 0, 0)
    return pl.pallas_call(
        _merge_kernel,
        grid=(B, nt),
        in_specs=[
            pl.BlockSpec((tm, D_MODEL), rowmap),
            pl.BlockSpec((tm, 768), lambda b, i: (b * nt + i, OFF_CONV // 768)),
            pl.BlockSpec((tm, 256), rowmap),
            pl.BlockSpec((tm, 256), rowmap),
            pl.BlockSpec((tm, 256), rowmap),
            pl.BlockSpec((None, 8, 256), lmap),
            pl.BlockSpec((None, 1, D_MODEL), lmap),
            pl.BlockSpec((None, D_MODEL, N_BRANCHES * D_MODEL), lmap, pipeline_mode=pl.Buffered(1)),
            pl.BlockSpec((None, N_BRANCHES, MIX_WIDTH, D_MODEL), lambda b, i: (layer, 0, 0, 0),
                         pipeline_mode=pl.Buffered(1)),
            pl.BlockSpec((None, D_MODEL, D_MODEL), lmap, pipeline_mode=pl.Buffered(1)),
            pl.BlockSpec((None, 1, D_MODEL), lmap),
        ],
        out_specs=pl.BlockSpec((tm, D_MODEL), rowmap),
        out_shape=jax.ShapeDtypeStruct((B * S, D_MODEL), F32),
        scratch_shapes=[pltpu.VMEM((8, 256), F32)],
        compiler_params=_cparams(("parallel", "arbitrary")),
        name="merge_mix",
    )(x, z, o_nsa, o_ret, o_rwkv, conv_w, g_pre, w_gate, w_branch, w_out, g_post)


def _xattn_kernel(x_ref, kv_ref, gpre_ref, wq_ref, wo_ref, gpost_ref, o_ref):
    n_sub = 1
    rows = x_ref.shape[0] // n_sub

    def part(r0):
        rs = pl.ds(r0, rows)
        x = x_ref[rs, :]
        h = _rms(x, gpre_ref[...]).astype(BF16)
        yield
        q = (_dot(h, wq_ref[...]) * (XA_HEAD_DIM ** -0.5)).astype(BF16)
        yield
        outs = []
        for hd in range(XA_HEADS):
            cs = slice(hd * XA_HEAD_DIM, (hd + 1) * XA_HEAD_DIM)
            kh = kv_ref[:, hd * XA_HEAD_DIM:(hd + 1) * XA_HEAD_DIM]
            vh = kv_ref[:, D_MODEL + hd * XA_HEAD_DIM:D_MODEL + (hd + 1) * XA_HEAD_DIM]
            s = _dot_nt(q[:, cs], kh)
            yield
            e = jnp.exp(s - _rowmax(s))
            p = e / _rowsum(e)
            outs.append(_dot(p.astype(BF16), vh).astype(BF16))
            yield
        o = jnp.concatenate(outs, axis=1)
        y = _dot(o, wo_ref[...])
        yield
        o_ref[rs, :] = x + _rms(y, gpost_ref[...])

    _round_robin([part(i * rows) for i in range(n_sub)])


def cross_attention(x, kv, g_pre, wq, wo, g_post, layer, B, S, M, tm):
    nt = S // tm
    rowmap = lambda b, i: (b * nt + i, 0)
    lmap = lambda b, i: (layer, 0, 0)
    return pl.pallas_call(
        _xattn_kernel,
        grid=(B, nt),
        in_specs=[
            pl.BlockSpec((tm, D_MODEL), rowmap),
            pl.BlockSpec((M, 2 * D_MODEL), lambda b, i: (b, 0)),
            pl.BlockSpec((None, 1, D_MODEL), lmap),
            pl.BlockSpec((None, D_MODEL, D_MODEL), lmap),
            pl.BlockSpec((None, D_MODEL, D_MODEL), lmap),
            pl.BlockSpec((None, 1, D_MODEL), lmap),
        ],
        out_specs=pl.BlockSpec((tm, D_MODEL), rowmap),
        out_shape=jax.ShapeDtypeStruct((B * S, D_MODEL), F32),
        compiler_params=_cparams(("parallel", "arbitrary")),
        name="cross_attention",
    )(x, kv, g_pre, wq, wo, g_post)


def _mlp_kernel(x_ref, gpre_ref, w1_ref, w2_ref, gpost_ref, o_ref):
    x = x_ref[...]
    h = _rms(x, gpre_ref[...]).astype(BF16)
    tf = 1024
    acc = jnp.zeros(x.shape, F32)
    for f in range(D_FF // tf):
        a = jnp.maximum(_dot(h, w1_ref[:, f * tf:(f + 1) * tf]), 0.0)
        acc = acc + _dot((a * a).astype(BF16), w2_ref[f * tf:(f + 1) * tf, :])
    o_ref[...] = x + _rms(acc, gpost_ref[...])


def mlp(x, g_pre, w1, w2, g_post, layer, tm):
    M = x.shape[0]
    lmap = lambda i: (layer, 0, 0)
    return pl.pallas_call(
        _mlp_kernel,
        grid=(M // tm,),
        in_specs=[
            pl.BlockSpec((tm, D_MODEL), lambda i: (i, 0)),
            pl.BlockSpec((None, 1, D_MODEL), lmap),
            pl.BlockSpec((None, D_MODEL, D_FF), lmap, pipeline_mode=pl.Buffered(1)),
            pl.BlockSpec((None, D_FF, D_MODEL), lmap, pipeline_mode=pl.Buffered(1)),
            pl.BlockSpec((None, 1, D_MODEL), lmap),
        ],
        out_specs=pl.BlockSpec((tm, D_MODEL), lambda i: (i, 0)),
        out_shape=jax.ShapeDtypeStruct((M, D_MODEL), F32),
        compiler_params=_cparams(("parallel",)),
        name="mlp",
    )(x, g_pre, w1, w2, g_post)


def _t5_bucket(dist):
    n = jnp.maximum(dist, 0)
    max_exact = N_BUCKETS // 2
    nf = jnp.maximum(n, 1).astype(F32)
    large = max_exact + (jnp.log(nf / max_exact) / math.log(MAX_DISTANCE / max_exact)
                         * (N_BUCKETS - max_exact)).astype(jnp.int32)
    large = jnp.minimum(large, N_BUCKETS - 1)
    return jnp.where(n < max_exact, n, large)


def _nsa_tables(rel_bias, S):
    n_qb = S // Q_BLOCK
    n_cmp = S // NSA_CMP_STRIDE
    n_blk = S // NSA_SLC_LEN
    bias_f = rel_bias.astype(F32)
    i = jnp.arange(Q_BLOCK)
    per_qb = Q_BLOCK // NSA_CMP_STRIDE
    g = jnp.arange(n_cmp + per_qb * (n_qb - 1)) - per_qb * (n_qb - 1)
    d_c = i[None, :] - (g[:, None] * NSA_CMP_STRIDE + NSA_CMP_LEN - 1)
    def lookup(dist):
        hit = _t5_bucket(dist)[..., None, None] == jnp.arange(N_BUCKETS)[:, None]
        return jnp.sum(jnp.where(hit, bias_f, 0.0), axis=-2)

    bias_c = lookup(d_c).transpose(0, 2, 1).reshape(g.shape[0], MIX_HEADS * Q_BLOCK)
    d0 = i[:, None] - i[None, :]
    tab = jnp.stack([lookup(d0), lookup(d0 + Q_BLOCK)]) - bias_f[N_BUCKETS - 1]
    tab = tab.transpose(0, 2, 3, 1).reshape(2, Q_BLOCK, MIX_HEADS * Q_BLOCK)
    cs = np.arange(n_cmp)[None, :] * NSA_CMP_STRIDE
    ss = np.arange(n_blk)[:, None] * NSA_SLC_LEN
    ovt = np.clip(np.minimum(cs + NSA_CMP_LEN, ss + NSA_SLC_LEN) - np.maximum(cs, ss), 0, None)
    ovt = ovt.astype(np.float32) / NSA_CMP_LEN
    ovt[:, (S - NSA_CMP_LEN) // NSA_CMP_STRIDE + 1:] = 0.0
    return bias_c * LOG2E, tab * LOG2E, jnp.asarray(ovt, BF16)


def _ret_tables(S):
    L = RET_CHUNK
    H = MIX_HEADS
    half = HEAD_DIM // 2
    pos = jnp.arange(S)
    inv_freq = ROPE_BASE ** (-jnp.arange(half, dtype=F32) / half)
    ang = pos.astype(F32)[:, None] * inv_freq[None, :]
    cos = jnp.tile(jnp.cos(ang), (1, H))
    sin = jnp.tile(jnp.sin(ang), (1, H))
    lg = jnp.log(1.0 - 2.0 ** (-5.0 - jnp.arange(H, dtype=F32)))
    n = jnp.arange(L, dtype=F32)
    diff = n[:, None] - n[None, :]
    inner = jnp.where(diff >= 0, jnp.exp(jnp.maximum(diff, 0.0)[None] * lg[:, None, None]), 0.0)
    indec = inner.reshape(H * L, L)
    q_decay = jnp.exp((n + 1.0)[None, :] * lg[:, None])
    k_decay = jnp.exp((L - 1.0 - n)[None, :] * lg[:, None])
    chunk_decay = jnp.exp(L * lg)
    lane = np.arange(256)
    head_v = lane // HEAD_DIM
    head_qk = (lane % 128) // half
    qdec = q_decay.T[:, head_v]
    kdec = k_decay.T[:, head_qk]
    cdec = chunk_decay[head_v][None, :]
    bd = jnp.asarray((head_qk[:, None] == head_v[None, :]).astype(np.float32))
    segm = jnp.asarray((head_v[:, None] == head_v[None, :]).astype(np.float32) / HEAD_DIM, BF16)
    return cos, sin, indec, qdec, kdec, cdec, bd, segm


def _rwkv_consts():
    C = RWKV_CHUNK
    t = np.arange(C)
    tri = (t[:, None] >= t[None, :]).astype(np.float32)
    r = np.arange(4 * C)
    bdl = ((r[:, None] // C == r[None, :] // C) & (r[:, None] % C > r[None, :] % C)).astype(np.float32)
    strict = (t[:, None] > (r[None, :] % C)).astype(np.float32)
    incl = (t[:, None] >= (r[None, :] % C)).astype(np.float32)
    lane = np.arange(256) // HEAD_DIM
    seg = (lane[:, None] == lane[None, :]).astype(np.float32)
    return (jnp.asarray(tri, BF16), jnp.asarray(bdl), jnp.asarray(np.stack([strict, incl])),
            jnp.asarray(seg, BF16), jnp.asarray(seg / HEAD_DIM, BF16))


def _hi_lo(w):
    hi = w.astype(BF16)
    lo = (w - hi.astype(F32)).astype(BF16)
    return jnp.stack([hi, lo], axis=1)


def _pad_rows(w, top, total):
    return jnp.pad(w, ((0, 0), (top, total - top - w.shape[1]), (0, 0)))


def _layout_w_in(w_in):
    L = w_in.shape[0]
    o = 0
    nsa_q = w_in[:, :, o:o + 256]; o += 256
    nsa_kv = w_in[:, :, o:o + 384]; o += 384
    nsa_g = w_in[:, :, o:o + 12]; o += 12
    ret = w_in[:, :, o:o + 1024]; o += 1024
    rwkv = w_in[:, :, o:o + RWKV_COLS]; o += RWKV_COLS
    conv = w_in[:, :, o:o + 768]; o += 768
    gate = w_in[:, :, o:o + 4096]

    def rot_perm(w):
        half = HEAD_DIM // 2
        return [w[:, :, h * HEAD_DIM + p * half:h * HEAD_DIM + (p + 1) * half]
                for p in range(2) for h in range(MIX_HEADS)]

    ret = rot_perm(ret[:, :, 0:256]) + rot_perm(ret[:, :, 256:512]) + [ret[:, :, 512:]]
    zeros = lambda n: jnp.zeros((L, D_MODEL, n), w_in.dtype)
    out = jnp.concatenate([conv, nsa_q] + ret + [rwkv, zeros(1024 - RWKV_COLS), nsa_kv, nsa_g,
                                                 zeros(128 - 12)], axis=2)
    assert out.shape[2] == Z_COLS
    return out.astype(BF16), gate.astype(BF16)


def _layout_cmp(cmp_w, cmp_pe):
    L = cmp_w.shape[0]
    wk = cmp_w[:, 0]
    wv = cmp_w[:, 1]
    zero = jnp.zeros_like(wk)
    blk = jnp.concatenate([jnp.concatenate([wk, zero], axis=3), jnp.concatenate([zero, wv], axis=3)], axis=2)
    pe2 = jnp.concatenate([cmp_pe, cmp_pe], axis=2)[:, :, None, :]
    s = NSA_CMP_STRIDE
    return blk[:, :s].astype(BF16), blk[:, s:].astype(BF16), pe2[:, :s], pe2[:, s:]


def kernel(x, mem, ln_mix_pre, w_in, nsa_cmp_w, nsa_cmp_pe, rel_bias, ret_norm_g, rwkv_mu, rwkv_w0, rwkv_w2, rwkv_a0, rwkv_a2, rwkv_g2, rwkv_k_k, rwkv_k_a, rwkv_r_k, rwkv_ln_g, rwkv_ln_b, conv_w, w_branch, w_mix_out, ln_mix_post, ln_xa_pre, ln_mem, xa_wq, xa_wkv, xa_wo, ln_xa_post, ln_mlp_pre, mlp_w1, mlp_w2, ln_mlp_post):
    B, S, D = x.shape
    M = mem.shape[1]
    depth = w_in.shape[0]
    row = lambda g: g[:, None, :]

    w_in_b, w_gate_b = _layout_w_in(w_in)
    cmp_lo, cmp_hi, pe_lo, pe_hi = _layout_cmp(nsa_cmp_w, nsa_cmp_pe)
    nsa_tabs = _nsa_tables(rel_bias, S)
    ret_tabs = _ret_tables(S)
    rwkv_consts = _rwkv_consts()
    mu = jnp.pad(rwkv_mu, ((0, 0), (0, 1024 - RWKV_COLS)))[:, None, :]
    lora_w = _hi_lo(jnp.concatenate([_pad_rows(rwkv_w2, 0, 128), _pad_rows(rwkv_a2, RWKV_DECAY_LORA, 128),
                                     _pad_rows(rwkv_g2, RWKV_DECAY_LORA + RWKV_AAA_LORA, 128)], axis=2))
    vec = jnp.stack([rwkv_w0, rwkv_a0, rwkv_k_k, rwkv_k_a, rwkv_r_k, rwkv_ln_g, rwkv_ln_b,
                     jnp.zeros_like(rwkv_w0)], axis=1)
    conv_p = jnp.pad(conv_w, ((0, 0), (0, 8 - CONV_WIDTH), (0, 0)))
    w_branch_b = w_branch.astype(BF16)
    w_mix_out_b = w_mix_out.astype(BF16)
    xa_wq_b = xa_wq.astype(BF16)
    xa_wkv_b = xa_wkv.astype(BF16)
    xa_wo_b = xa_wo.astype(BF16)
    mlp_w1_b = mlp_w1.astype(BF16)
    mlp_w2_b = mlp_w2.astype(BF16)

    xf = x.reshape(B * S, D)
    memf = mem.reshape(B * M, D)
    for l in range(depth):
        z = norm_matmul(xf, row(ln_mix_pre), w_in_b, l, tm=512, tn=Z_COLS, out_dtype=F32)
        prep = nsa_compress(z, cmp_lo, cmp_hi, pe_lo, pe_hi, l, B, S)
        z3 = z.reshape(B, S, Z_COLS)
        o_nsa = nsa_attention(z3, prep, *nsa_tabs, B, S, G=8).reshape(B * S, MIX_WIDTH)
        o_ret = retention(z3, ret_tabs, row(ret_norm_g), l, B, S, G=4).reshape(B * S, MIX_WIDTH)
        o_rwkv = rwkv7(z3, mu, lora_w, vec, rwkv_consts, l, B, S, G=4).reshape(B * S, MIX_WIDTH)
        xf = merge_mix(xf, z, o_nsa, o_ret, o_rwkv, conv_p, row(ln_mix_pre), w_gate_b, w_branch_b,
                       w_mix_out_b, row(ln_mix_post), l, B, S, tm=512)
        kvm = norm_matmul(memf, row(ln_mem), xa_wkv_b, l, tm=2 * M, tn=2 * D_MODEL, out_dtype=BF16)
        xf = cross_attention(xf, kvm, row(ln_xa_pre), xa_wq_b, xa_wo_b, row(ln_xa_post), l, B, S, M, tm=1024)
        xf = mlp(xf, row(ln_mlp_pre), mlp_w1_b, mlp_w2_b, row(ln_mlp_post), l, tm=512)
    return xf.reshape(B, S, D)
```

```python
import functools
import math

import numpy as np
import jax
import jax.numpy as jnp
from jax import lax
from jax.experimental import pallas as pl
from jax.experimental.pallas import tpu as pltpu

F32 = jnp.float32
BF16 = jnp.bfloat16

D_MODEL = 1024
N_BRANCHES = 4
MIX_WIDTH = 256
HEAD_DIM = 64
MIX_HEADS = 4

NSA_CMP_LEN = 32
NSA_CMP_STRIDE = 16
NSA_SLC_LEN = 64
NSA_TOP_N = 8
NSA_WINDOW = 512
Q_BLOCK = 128
NSA_V_ROWS = HEAD_DIM + 16
FORCE_BONUS = 1e4
N_BUCKETS = 32
MAX_DISTANCE = 128

RET_CHUNK = 128
ROPE_BASE = 10000.0
RET_NORM_EPS = 1e-5

RWKV_DECAY_LORA = 32
RWKV_AAA_LORA = 32
RWKV_GATE_LORA = 64
RWKV_GN_EPS = 64e-5
RWKV_COLS = 3 * MIX_WIDTH + RWKV_DECAY_LORA + RWKV_AAA_LORA + RWKV_GATE_LORA
RWKV_CHUNK = 64

CONV_WIDTH = 3
XA_HEADS = 4
XA_HEAD_DIM = D_MODEL // XA_HEADS
D_FF = 4 * D_MODEL

RMS_EPS = 1e-6
LOG2E = math.log2(math.e)
NEG_INF = -1e30
NEG_BIG = -3e38

OFF_CONV = 0
OFF_NSAQ = 768
OFF_RET = 1024
OFF_RWKV = 2048
OFF_NSAKV = 3072
OFF_NSAG = 3456
Z_COLS = 3584

VMEM_LIMIT = 56 * 1024 * 1024


def _cparams(sem):
    return pltpu.CompilerParams(dimension_semantics=sem, vmem_limit_bytes=VMEM_LIMIT)


def _dot(a, b):
    return jnp.dot(a, b, preferred_element_type=F32)


def _dot_nt(a, b):
    return lax.dot_general(a, b, (((1,), (1,)), ((), ())), preferred_element_type=F32)


def _dot_tn(a, b):
    return lax.dot_general(a, b, (((0,), (0,)), ((), ())), preferred_element_type=F32)


def _split2(x):
    hi = x.astype(BF16)
    lo = (x - hi.astype(F32)).astype(BF16)
    return hi, lo


def _dot_x2(x, w_bf16, *more):
    xs = (x,) + more
    rows = x.shape[0]
    parts = [p for xi in xs for p in _split2(xi)]
    out = _dot(jnp.concatenate(parts, axis=0), w_bf16)
    res = [out[2 * i * rows:(2 * i + 1) * rows] + out[(2 * i + 1) * rows:(2 * i + 2) * rows]
           for i in range(len(xs))]
    return res[0] if not more else res


def _rms(x, g):
    ms = jnp.mean(x * x, axis=-1, keepdims=True)
    return x * lax.rsqrt(ms + RMS_EPS) * g


def _norm_matmul_kernel(x_ref, g_ref, w_ref, o_ref, h_ref):
    @pl.when(pl.program_id(1) == 0)
    def _():
        h_ref[...] = _rms(x_ref[...], g_ref[...]).astype(BF16)

    o_ref[...] = _dot(h_ref[...], w_ref[...]).astype(o_ref.dtype)


def norm_matmul(x, g, w, layer, tm, tn, out_dtype):
    M, D = x.shape
    N = w.shape[2]
    w_mode = dict(pipeline_mode=pl.Buffered(1)) if tn == N else {}
    return pl.pallas_call(
        _norm_matmul_kernel,
        grid=(M // tm, N // tn),
        in_specs=[
            pl.BlockSpec((tm, D), lambda i, j: (i, 0)),
            pl.BlockSpec((None, 1, D), lambda i, j: (layer, 0, 0)),
            pl.BlockSpec((None, D, tn), lambda i, j: (layer, 0, j), **w_mode),
        ],
        out_specs=pl.BlockSpec((tm, tn), lambda i, j: (i, j)),
        out_shape=jax.ShapeDtypeStruct((M, N), out_dtype),
        scratch_shapes=[pltpu.VMEM((tm, D), BF16)],
        compiler_params=_cparams(("parallel", "arbitrary")),
        name="norm_matmul",
    )(x, g, w)


def _nsa_compress_kernel(kvc_ref, kvs_ref, kvw_ref, wlo_ref, whi_ref, pelo_ref, pehi_ref,
                         kc_ref, vct_ref, ks_ref, vst_ref, kw_ref, vwt_ref):
    nblk = kc_ref.shape[0]
    n_kt = vwt_ref.shape[0]
    ylo = jnp.zeros((nblk, 128), F32)
    yhi = jnp.zeros((nblk, 128), F32)
    for r in range(NSA_CMP_STRIDE):
        xr = kvc_ref[pl.ds(r, nblk, stride=NSA_CMP_STRIDE), :]
        ylo = ylo + _dot((xr + pelo_ref[r]).astype(BF16), wlo_ref[r])
        yhi = yhi + _dot((xr + pehi_ref[r]).astype(BF16), whi_ref[r])
    y = ylo + pltpu.roll(yhi, nblk - 1, axis=0)
    kc_ref[...] = y.astype(BF16)
    vct_ref[...] = y.T[HEAD_DIM:, :].astype(BF16)
    kw_ref[...] = kvw_ref[...].astype(BF16)
    ones = jnp.ones((NSA_V_ROWS - HEAD_DIM, 128), BF16)
    for kt in range(n_kt):
        rs = slice(kt * 128, (kt + 1) * 128)
        vst_ref[kt] = jnp.concatenate([kvs_ref[rs, :].T[HEAD_DIM:, :].astype(BF16), ones], axis=0)
        vwt_ref[kt] = jnp.concatenate([kvw_ref[rs, :].T[HEAD_DIM:, :].astype(BF16), ones], axis=0)
    S = kvs_ref.shape[0]
    key = lax.broadcasted_iota(jnp.int32, (S, 128), 0)
    lane = lax.broadcasted_iota(jnp.int32, (S, 128), 1)
    onehot = (lane == HEAD_DIM + key // NSA_SLC_LEN).astype(F32)
    ks_ref[0:S, :] = jnp.where(lane < HEAD_DIM, kvs_ref[...], onehot).astype(BF16)
    pad_lane = lax.broadcasted_iota(jnp.int32, (128, 128), 1)
    ks_ref[S:S + 128, :] = (pad_lane == HEAD_DIM + S // NSA_SLC_LEN).astype(BF16)
    vst_ref[n_kt] = jnp.zeros((NSA_V_ROWS, 128), BF16)


def nsa_compress(z, wlo, whi, pelo, pehi, layer, B, S):
    nblk = S // NSA_CMP_STRIDE
    n_kt = S // 128
    c0 = OFF_NSAKV // 128
    wmap = lambda b: (layer, 0, 0, 0)
    b3 = lambda b: (b, 0, 0)
    b4 = lambda b: (b, 0, 0, 0)
    return pl.pallas_call(
        _nsa_compress_kernel,
        grid=(B,),
        in_specs=[
            pl.BlockSpec((S, 128), lambda b: (b, c0)),
            pl.BlockSpec((S, 128), lambda b: (b, c0 + 1)),
            pl.BlockSpec((S, 128), lambda b: (b, c0 + 2)),
            pl.BlockSpec((None, NSA_CMP_STRIDE, 128, 128), wmap),
            pl.BlockSpec((None, NSA_CMP_STRIDE, 128, 128), wmap),
            pl.BlockSpec((None, NSA_CMP_STRIDE, 1, 128), wmap),
            pl.BlockSpec((None, NSA_CMP_STRIDE, 1, 128), wmap),
        ],
        out_specs=[
            pl.BlockSpec((None, nblk, 128), b3),
            pl.BlockSpec((None, HEAD_DIM, nblk), b3),
            pl.BlockSpec((None, S + 128, 128), b3),
            pl.BlockSpec((None, n_kt + 1, NSA_V_ROWS, 128), b4),
            pl.BlockSpec((None, S, 128), b3),
            pl.BlockSpec((None, n_kt, NSA_V_ROWS, 128), b4),
        ],
        out_shape=[
            jax.ShapeDtypeStruct((B, nblk, 128), BF16),
            jax.ShapeDtypeStruct((B, HEAD_DIM, nblk), BF16),
            jax.ShapeDtypeStruct((B, S + 128, 128), BF16),
            jax.ShapeDtypeStruct((B, n_kt + 1, NSA_V_ROWS, 128), BF16),
            jax.ShapeDtypeStruct((B, S, 128), BF16),
            jax.ShapeDtypeStruct((B, n_kt, NSA_V_ROWS, 128), BF16),
        ],
        compiler_params=_cparams(("parallel",)),
        name="nsa_compress",
    )(z, z, z, wlo, whi, pelo, pehi)


def _rowmax(s):
    return jnp.max(s, axis=-1, keepdims=True)


def _rowsum(s):
    return jnp.sum(s, axis=-1, keepdims=True)


def _colmax(s):
    return jnp.max(s, axis=0, keepdims=True)


def _colsum(s):
    return jnp.sum(s, axis=0, keepdims=True)


def _nsa_kernel(q_ref, gl_ref, kc_ref, vct_ref, ks_ref, vst_ref, kw_ref, vwt_ref, bc_ref, tab_ref,
                ovt_ref, o_ref):
    bi = pl.program_id(1)
    G = q_ref.shape[0]
    n_kt = vwt_ref.shape[1]
    n_blk = ovt_ref.shape[0]
    n_cmp = kc_ref.shape[1]
    QB = Q_BLOCK
    R = MIX_HEADS * QB
    n_off = NSA_WINDOW // 128

    lo = jnp.maximum(bi - n_off, 0)
    q4 = [None] * G
    q4_sel = [None] * G
    o_c = [None] * G
    carry_s = [None] * G
    carry_w = [None] * G

    def key_tile(kref, g, kt, lanes):
        return kref[g, pl.ds(pl.multiple_of(kt * 128, 128), 128), lanes]

    def pair_qk(g, k0, k1):
        keys = jnp.concatenate([key_tile(kw_ref, g, k0, slice(0, HEAD_DIM)),
                                key_tile(kw_ref, g, k1, slice(0, HEAD_DIM))], axis=0)
        qk = _dot(keys, q4[g])
        return qk[:128], qk[128:]

    def pair_qk_sel(g, k0, k1):
        keys = jnp.concatenate([key_tile(ks_ref, g, k0, slice(None)), key_tile(ks_ref, g, k1, slice(None))],
                               axis=0)
        qk = _dot(keys, q4_sel[g])
        return qk[:128], qk[128:]

    def head_cols(x, h):
        return x[:, h * QB:(h + 1) * QB]

    key1 = lax.broadcasted_iota(jnp.int32, (128, QB), 0)
    i1 = lax.broadcasted_iota(jnp.int32, (128, QB), 1)
    causal1 = key1 <= i1
    kp = jnp.maximum(bi - 1, 0)
    pen_prev = jnp.where(bi >= 1, 0.0, NEG_INF)

    def win_add(kt, valid):
        shift = jnp.where(valid, jnp.where(bi - kt < n_off, -4096, 0), 4096)
        return jnp.where(key1 > i1 + shift, 0.0, NEG_INF)

    def diag_add(extra):
        return [jnp.where(causal1, head_cols(tab_ref[0], h) + extra, NEG_INF) for h in range(MIX_HEADS)]

    def prev_add(extra):
        return [head_cols(tab_ref[1], h) + extra for h in range(MIX_HEADS)]

    def pair_update(carry, qk0, add0, vt0, qk1, add1, vt1):
        p0s, p1s, ms, alphas = [], [], [], []
        for h in range(MIX_HEADS):
            a0 = head_cols(qk0, h) if add0 is None else head_cols(qk0, h) + add0[h]
            a1 = head_cols(qk1, h) if add1 is None else head_cols(qk1, h) + add1[h]
            m_new = jnp.maximum(_colmax(a0), _colmax(a1))
            if carry is not None:
                m_old = head_cols(carry[0], h)
                m_new = jnp.maximum(m_old, m_new)
                alphas.append(jnp.exp2(m_old - m_new))
            ms.append(m_new)
            p0s.append(jnp.exp2(a0 - m_new).astype(BF16))
            p1s.append(jnp.exp2(a1 - m_new).astype(BF16))
        yield
        probs = jnp.concatenate([jnp.concatenate(p0s, axis=1), jnp.concatenate(p1s, axis=1)], axis=0)
        acc = _dot(jnp.concatenate([vt0, vt1], axis=1), probs)
        if carry is not None:
            acc = jnp.concatenate(alphas, axis=1) * carry[1] + acc
        return jnp.concatenate(ms, axis=1), acc

    def head_part(g):
        qt = (q_ref[g] * (HEAD_DIM ** -0.5 * LOG2E)).T
        q4[g] = jnp.concatenate([qt[h * HEAD_DIM:(h + 1) * HEAD_DIM] for h in range(MIX_HEADS)],
                                axis=1).astype(BF16)
        yield
        qk_d, qk_p = pair_qk(g, bi, kp)
        carry_w[g] = yield from pair_update(None, qk_d, diag_add(0.0), vwt_ref[g, bi],
                                            qk_p, prev_add(pen_prev), vwt_ref[g, kp])
        yield
        g0 = pl.multiple_of((n_kt - 1 - bi) * 8, 8)
        qk = _dot(kc_ref[g, :, :HEAD_DIM], q4[g])
        nrow = lax.broadcasted_iota(jnp.int32, (n_cmp, QB), 0)
        tq = bi * QB + lax.broadcasted_iota(jnp.int32, (n_cmp, QB), 1)
        vis = tq - (nrow * NSA_CMP_STRIDE + (NSA_CMP_LEN - 1)) >= 0
        live = (tq >= NSA_CMP_LEN - 1).astype(F32)
        yield
        p_cs = []
        for h in range(MIX_HEADS):
            s = jnp.where(vis, head_cols(qk, h) + bc_ref[pl.ds(g0, n_cmp), h * QB:(h + 1) * QB], NEG_INF)
            e = jnp.exp2(s - _colmax(s))
            p_cs.append(e / _colsum(e) * live)
        yield
        o_c[g] = _dot(vct_ref[g], jnp.concatenate(p_cs, axis=1).astype(BF16))
        p_sum = p_cs[0] + p_cs[1] + p_cs[2] + p_cs[3]
        p_hi, p_lo = _split2(p_sum)
        imp = _dot(ovt_ref[...], p_hi) + _dot(ovt_ref[...], p_lo)
        yield
        blk = lax.broadcasted_iota(jnp.int32, (n_blk, QB), 0)
        cur = (bi * QB + lax.broadcasted_iota(jnp.int32, (n_blk, QB), 1)) >> 6
        forced = (blk == 0) | (blk == cur) | (blk == cur - 1)
        imp = jnp.where(forced, imp + FORCE_BONUS, imp)
        imp = jnp.where(blk <= cur, imp, NEG_INF)
        blk_f = blk.astype(F32)
        sel = jnp.zeros((n_blk, QB), F32)
        for _ in range(NSA_TOP_N):
            mx = _colmax(imp)
            idx = jnp.min(jnp.where(imp == mx, blk_f, 4096.0), axis=0, keepdims=True)
            pick = blk_f == idx
            sel = jnp.where(pick, 1.0, sel)
            imp = jnp.where(pick, NEG_BIG, imp)
            yield
        sel_add = (sel - 1.0) * (-NEG_INF)
        rows_left = 128 - HEAD_DIM - n_blk
        pad_rows = jnp.where(lax.broadcasted_iota(jnp.int32, (rows_left, QB), 0) == 0, NEG_INF, 0.0)
        mask_rows = jnp.concatenate([sel_add, pad_rows], axis=0).astype(BF16)
        q4_sel[g] = jnp.concatenate([q4[g], jnp.concatenate([mask_rows] * MIX_HEADS, axis=1)], axis=0)
        yield
        qk_d, qk_p = pair_qk_sel(g, bi, kp)
        carry_s[g] = yield from pair_update(None, qk_d, diag_add(0.0), vst_ref[g, bi],
                                            qk_p, prev_add(pen_prev), vst_ref[g, kp])

    _round_robin([head_part(g) for g in range(G)])

    n_old = jnp.maximum(bi - 1, 0)

    def sel_step(g, j, carry, out):
        k0 = 2 * j
        k1 = jnp.where(k0 + 1 < n_old, k0 + 1, n_kt)
        qk0, qk1 = pair_qk_sel(g, k0, k1)
        yield
        out[g] = yield from pair_update(carry, qk0, None, vst_ref[g, k0], qk1, None, vst_ref[g, k1])

    def win_step(g, j, carry, out):
        k0 = lo + 2 * j
        k1 = jnp.minimum(k0 + 1, bi)
        qk0, qk1 = pair_qk(g, k0, k1)
        yield
        out[g] = yield from pair_update(carry, qk0, [win_add(k0, True)] * MIX_HEADS, vwt_ref[g, k0],
                                        qk1, [win_add(k1, k0 + 1 < n_old)] * MIX_HEADS, vwt_ref[g, k1])

    def joint(step):
        def body(j, carries):
            out = [None] * G
            _round_robin([step(g, j, carries[g], out) for g in range(G)])
            return tuple(out)
        return body

    res_s = lax.fori_loop(0, (n_old + 1) // 2, joint(sel_step), tuple(carry_s))
    res_w = lax.fori_loop(0, (n_old - lo + 1) // 2, joint(win_step), tuple(carry_w))

    def tail_part(g):
        acc_s = res_s[g][1]
        acc_w = res_w[g][1]
        o_s = acc_s[:HEAD_DIM] * (1.0 / acc_s[HEAD_DIM:HEAD_DIM + 1])
        o_w = acc_w[:HEAD_DIM] * (1.0 / acc_w[HEAD_DIM:HEAD_DIM + 1])
        gate = jax.nn.sigmoid(gl_ref[g].T)
        yield
        outs = []
        for h in range(MIX_HEADS):
            cs = slice(h * QB, (h + 1) * QB)
            outs.append(gate[3 * h:3 * h + 1] * o_c[g][:, cs] + gate[3 * h + 1:3 * h + 2] * o_s[:, cs]
                        + gate[3 * h + 2:3 * h + 3] * o_w[:, cs])
        o_ref[g] = jnp.concatenate(outs, axis=0).T

    _round_robin([tail_part(g) for g in range(G)])


def nsa_attention(z, prep, bias_c, tab, ovt, B, S, G):
    n_qb = S // Q_BLOCK
    n_kt = S // 128
    n_cmp = S // NSA_CMP_STRIDE
    R = MIX_HEADS * Q_BLOCK
    kc, vct, ks, vst, kw, vwt = prep
    b3 = lambda b, i: (b, 0, 0)
    b4 = lambda b, i: (b, 0, 0, 0)
    return pl.pallas_call(
        _nsa_kernel,
        grid=(B // G, n_qb),
        in_specs=[
            pl.BlockSpec((G, Q_BLOCK, MIX_WIDTH), lambda b, i: (b, i, OFF_NSAQ // MIX_WIDTH)),
            pl.BlockSpec((G, Q_BLOCK, 128), lambda b, i: (b, i, OFF_NSAG // 128)),
            pl.BlockSpec((G, n_cmp, 128), b3),
            pl.BlockSpec((G, HEAD_DIM, n_cmp), b3),
            pl.BlockSpec((G, S + 128, 128), b3),
            pl.BlockSpec((G, n_kt + 1, NSA_V_ROWS, 128), b4),
            pl.BlockSpec((G, S, 128), b3),
            pl.BlockSpec((G, n_kt, NSA_V_ROWS, 128), b4),
            pl.BlockSpec(bias_c.shape, lambda b, i: (0, 0)),
            pl.BlockSpec((2, 128, R), lambda b, i: (0, 0, 0)),
            pl.BlockSpec(ovt.shape, lambda b, i: (0, 0)),
        ],
        out_specs=pl.BlockSpec((G, Q_BLOCK, MIX_WIDTH), lambda b, i: (b, i, 0)),
        out_shape=jax.ShapeDtypeStruct((B, S, MIX_WIDTH), F32),
        compiler_params=_cparams(("parallel", "arbitrary")),
        name="nsa_attention",
    )(z, z, kc, vct, ks, vst, kw, vwt, bias_c, tab, ovt)


def _stack_heads(x, head_of_lane):
    return jnp.concatenate([jnp.where(head_of_lane == h, x, jnp.zeros_like(x))
                            for h in range(MIX_HEADS)], axis=0)


def _head_norm(y, seg_mean, eps):
    mu = _dot_x2(y, seg_mean)
    d = y - mu
    var = _dot_x2(d * d, seg_mean)
    return d * lax.rsqrt(var + eps)


def _round_robin(chains):
    while chains:
        chains = [c for c in chains if next(c, StopIteration) is not StopIteration]


def _retention_kernel(z_ref, cos_ref, sin_ref, indec_ref, qdec_ref, kdec_ref, cdec_ref, bd_ref,
                      segm_ref, ng_ref, o_ref, st_ref):
    @pl.when(pl.program_id(1) == 0)
    def _():
        st_ref[...] = jnp.zeros_like(st_ref)

    _round_robin([_retention_chunk(z_ref.at[g], cos_ref, sin_ref, indec_ref, qdec_ref, kdec_ref, cdec_ref,
                                   bd_ref, segm_ref, ng_ref, o_ref.at[g], st_ref.at[g])
                  for g in range(z_ref.shape[0])])


def _retention_chunk(z_ref, cos_ref, sin_ref, indec_ref, qdec_ref, kdec_ref, cdec_ref, bd_ref,
                     segm_ref, ng_ref, o_ref, st_ref):
    L = RET_CHUNK
    z = z_ref[...]
    q = z[:, 0:256]
    k = z[:, 256:512]
    v = z[:, 512:768]
    g = z[:, 768:1024]
    cos = cos_ref[...]
    sin = sin_ref[...]

    def rot(u):
        u1 = u[:, :128]
        u2 = u[:, 128:]
        return jnp.concatenate([u1 * cos - u2 * sin, u2 * cos + u1 * sin], axis=1)

    qr = rot(q) * (HEAD_DIM ** -0.5)
    kr = rot(k)
    lane = lax.broadcasted_iota(jnp.int32, (L, 256), 1)
    head_qk = (lane & 127) >> 5
    head_v = lane >> 6
    qb = qr.astype(BF16)
    kb = kr.astype(BF16)
    vb = v.astype(BF16)
    yield

    att = _dot_nt(_stack_heads(qb, head_qk), kb) * indec_ref[...]
    yield
    o_st = _dot(att.astype(BF16), vb)
    o = jnp.zeros((L, 256), F32)
    for h in range(MIX_HEADS):
        o = o + jnp.where(head_v == h, o_st[h * L:(h + 1) * L], 0.0)
    yield
    state = st_ref[...]
    o = o + _dot(qb, state.astype(BF16)) * qdec_ref[...]
    st_ref[...] = state * cdec_ref[...] + _dot_tn((kr * kdec_ref[...]).astype(BF16), vb) * bd_ref[...]
    yield

    mu = _dot_x2(o, segm_ref[...])
    yield
    d = o - mu
    var = _dot_x2(d * d, segm_ref[...])
    yield
    o_ref[...] = g * jax.nn.sigmoid(g) * (d * lax.rsqrt(var + RET_NORM_EPS) * ng_ref[...])


def retention(z, tabs, ng, layer, B, S, G):
    L = RET_CHUNK
    nc = S // L
    cos, sin, indec, qdec, kdec, cdec, bd, segm = tabs
    const2 = lambda b, c: (0, 0)
    return pl.pallas_call(
        _retention_kernel,
        grid=(B // G, nc),
        in_specs=[
            pl.BlockSpec((G, L, 1024), lambda b, c: (b, c, OFF_RET // 1024)),
            pl.BlockSpec((L, 128), lambda b, c: (c, 0)),
            pl.BlockSpec((L, 128), lambda b, c: (c, 0)),
            pl.BlockSpec((MIX_HEADS * L, L), const2),
            pl.BlockSpec((L, 256), const2),
            pl.BlockSpec((L, 256), const2),
            pl.BlockSpec((1, 256), const2),
            pl.BlockSpec((256, 256), const2),
            pl.BlockSpec((256, 256), const2),
            pl.BlockSpec((None, 1, 256), lambda b, c: (layer, 0, 0)),
        ],
        out_specs=pl.BlockSpec((G, L, 256), lambda b, c: (b, c, 0)),
        out_shape=jax.ShapeDtypeStruct((B, S, 256), F32),
        scratch_shapes=[pltpu.VMEM((G, 256, 256), F32)],
        compiler_params=_cparams(("parallel", "arbitrary")),
        name="retention",
    )(z, cos, sin, indec, qdec, kdec, cdec, bd, segm, ng)


def _rwkv_kernel(z_ref, mu_ref, lw_ref, vec_ref, tri_ref, bdl_ref, wide_ref,
                 segs_ref, segm_ref, o_ref, st_ref, prev_ref):
    @pl.when(pl.program_id(1) == 0)
    def _():
        st_ref[...] = jnp.zeros_like(st_ref)
        prev_ref[...] = jnp.zeros_like(prev_ref)

    _round_robin([_rwkv_chunk(z_ref.at[g], mu_ref, lw_ref, vec_ref, tri_ref, bdl_ref,
                              wide_ref, segs_ref, segm_ref, o_ref.at[g], st_ref.at[g], prev_ref.at[g])
                  for g in range(z_ref.shape[0])])


def _rwkv_chunk(z_ref, mu_ref, lw_ref, vec_ref, tri_ref, bdl_ref, wide_ref,
                segs_ref, segm_ref, o_ref, st_ref, prev_ref):
    C = RWKV_CHUNK
    z = z_ref[...]
    rows = lax.broadcasted_iota(jnp.int32, (C, 1024), 0)
    zs = jnp.where(rows == 0, prev_ref[0:1, :], pltpu.roll(z, 1, axis=0))
    prev_ref[...] = jnp.broadcast_to(z[C - 1:C, :], prev_ref.shape)
    zf = z + (zs - z) * mu_ref[...]
    r = zf[:, 0:256]
    k = zf[:, 256:512]
    v = zf[:, 512:768]
    lora = zf[:, 768:896]

    vec = vec_ref[...]
    w0, a0, k_k, k_a, r_k, ln_g, ln_b = (vec[i:i + 1] for i in range(7))

    lane = lax.broadcasted_iota(jnp.int32, lora.shape, 1)
    lora_in = jnp.where(lane < RWKV_DECAY_LORA, jnp.tanh(lora),
                        jnp.where(lane < RWKV_DECAY_LORA + RWKV_AAA_LORA, lora, jax.nn.sigmoid(lora)))
    hi, lo = _split2(lora_in)
    lhs = jnp.concatenate([jnp.concatenate([hi, hi], axis=1),
                           jnp.concatenate([lo, jnp.zeros_like(lo)], axis=1)], axis=0)
    both = _dot(lhs, lw_ref[...])
    lora_out = both[:C] + both[C:]
    wpre = w0 + lora_out[:, 0:256]
    y = -wpre
    softplus = jnp.maximum(y, 0.0) + jnp.log(1.0 + jnp.exp(-jnp.abs(y)))
    w_log = -softplus - 0.5
    ld = -jnp.exp(w_log)
    a = jax.nn.sigmoid(a0 + lora_out[:, 256:512])
    gate = lora_out[:, 512:768]
    kk = k * k_k
    k2 = k * (1.0 + (a - 1.0) * k_a)
    kk_sq, bonus = _dot_x2(kk * kk, segs_ref[...], r * k2 * r_k)
    kk = kk / jnp.maximum(jnp.sqrt(kk_sq), 1e-12)
    yield

    tri = tri_ref[...]
    l1 = ld.astype(BF16)
    r1 = ld - l1.astype(F32)
    l2 = r1.astype(BF16)
    l3 = (r1 - l2.astype(F32)).astype(BF16)
    cs = _dot(jnp.concatenate([tri, tri, tri], axis=1), jnp.concatenate([l1, l2, l3], axis=0))
    cs_end = cs[C - 1:C, :]
    yield
    e_neg = jnp.exp(-cs)
    e_end = jnp.exp(cs_end - cs)
    kka = kk * a
    a_t = -kk * jnp.exp(cs - ld)
    r_t = r * jnp.exp(cs)
    b_t = kka * e_neg
    k_t = k2 * e_neg
    b_g = kka * e_end
    k_g = k2 * e_end

    lane = lax.broadcasted_iota(jnp.int32, (C, 256), 1)
    head = lane >> 6
    stack = lambda x: _stack_heads(x.astype(BF16), head)
    a_st, b_st, k_st, v_st = stack(a_t), stack(b_t), stack(k_t), stack(v)

    n_pow = _dot_nt(a_st, b_st) * bdl_ref[...]
    ri = lax.broadcasted_iota(jnp.int32, (4 * C, 4 * C), 0)
    ci = lax.broadcasted_iota(jnp.int32, (4 * C, 4 * C), 1)
    t_inv = jnp.where(ri == ci, 1.0, 0.0) + n_pow
    yield

    ar = jnp.concatenate([a_t, r_t], axis=0).astype(BF16)
    bk = jnp.concatenate([b_st, k_st], axis=0)
    wide = _dot_nt(ar, bk)
    strict = wide_ref[0]
    incl = wide_ref[1]
    a_ak = (wide[0:C, 4 * C:] * strict).astype(BF16)
    a_rb = (wide[C:, 0:4 * C] * incl).astype(BF16)
    a_rk = (wide[C:, 4 * C:] * incl).astype(BF16)
    yield

    sq = 2
    while sq < C:
        nb = n_pow.astype(BF16)
        n_pow = _dot(nb, nb)
        yield
        t_inv = t_inv + _dot(t_inv.astype(BF16), n_pow.astype(BF16))
        sq *= 2
    yield

    state = st_ref[...]
    sb = state.astype(BF16)
    from_state = _dot_nt(ar, sb)
    x = from_state[0:C] + _dot(a_ak, v_st)
    yield
    u_st = _dot(t_inv.astype(BF16), stack(x)).astype(BF16)
    yield
    uv = jnp.concatenate([u_st, v_st], axis=0)
    yv = from_state[C:] + _dot(jnp.concatenate([a_rb, a_rk], axis=1), uv)
    bkg = jnp.concatenate([stack(b_g), stack(k_g)], axis=0)
    st_ref[...] = state * jnp.exp(cs_end) + _dot_tn(uv, bkg)
    yield

    yn = _head_norm(yv, segm_ref[...], RWKV_GN_EPS) * ln_g + ln_b
    yn = yn + bonus * v
    o_ref[...] = yn * gate


def rwkv7(z, mu, lora_w, vec, consts, layer, B, S, G):
    C = RWKV_CHUNK
    nc = S // C
    tri, bdl, wide, segs, segm = consts
    const2 = lambda b, c: (0, 0)
    return pl.pallas_call(
        _rwkv_kernel,
        grid=(B // G, nc),
        in_specs=[
            pl.BlockSpec((G, C, 1024), lambda b, c: (b, c, OFF_RWKV // 1024)),
            pl.BlockSpec((None, 1, 1024), lambda b, c: (layer, 0, 0)),
            pl.BlockSpec((None, 256, 768), lambda b, c: (layer, 0, 0)),
            pl.BlockSpec((None, 8, 256), lambda b, c: (layer, 0, 0)),
            pl.BlockSpec((C, C), const2),
            pl.BlockSpec((4 * C, 4 * C), const2),
            pl.BlockSpec((2, C, 4 * C), lambda b, c: (0, 0, 0)),
            pl.BlockSpec((256, 256), const2),
            pl.BlockSpec((256, 256), const2),
        ],
        out_specs=pl.BlockSpec((G, C, 256), lambda b, c: (b, c, 0)),
        out_shape=jax.ShapeDtypeStruct((B, S, 256), F32),
        scratch_shapes=[pltpu.VMEM((G, 256, 256), F32), pltpu.VMEM((G, 8, 1024), F32)],
        compiler_params=_cparams(("parallel", "arbitrary")),
        name="rwkv7",
    )(z, mu, lora_w, vec, tri, bdl, wide, segs, segm)


def _merge_kernel(x_ref, zc_ref, on_ref, or_ref, ow_ref, cw_ref, gpre_ref, wg_ref, wb_ref, wo_ref, g_ref,
                  o_ref, carry_ref):
    tm = x_ref.shape[0]
    x = x_ref[...]
    h = _rms(x, gpre_ref[...]).astype(BF16)

    @pl.when(pl.program_id(1) == 0)
    def _():
        carry_ref[...] = jnp.zeros_like(carry_ref)

    zc = zc_ref[...]
    b_g = zc[:, 0:256]
    u = zc[:, 256:512] * zc[:, 512:768]
    prev = carry_ref[...]
    rows = lax.broadcasted_iota(jnp.int32, (tm, 256), 0)
    u1 = jnp.where(rows == 0, prev[7:8], pltpu.roll(u, 1, axis=0))
    u2 = jnp.where(rows == 0, prev[6:7], jnp.where(rows == 1, prev[7:8], pltpu.roll(u, 2, axis=0)))
    carry_ref[...] = u[tm - 8:tm]
    cw = cw_ref[...]
    o_conv = b_g * (cw[0:1] * u2 + cw[1:2] * u1 + cw[2:3] * u)

    branches = (on_ref[...], or_ref[...], ow_ref[...], o_conv)
    merged = jnp.zeros((tm, D_MODEL), F32)
    for m in range(N_BRANCHES):
        gm = jax.nn.sigmoid(_dot(h, wg_ref[:, m * D_MODEL:(m + 1) * D_MODEL]))
        merged = merged + gm * _dot(branches[m].astype(BF16), wb_ref[m])
    y = _dot(merged.astype(BF16), wo_ref[...])
    o_ref[...] = x + _rms(y, g_ref[...])


def merge_mix(x, z, o_nsa, o_ret, o_rwkv, conv_w, g_pre, w_gate, w_branch, w_out, g_post, layer, B, S, tm):
    nt = S // tm
    rowmap = lambda b, i: (b * nt + i, 0)
    lmap = lambda b, i: (layer, 0, 0)
    return pl.pallas_call(
        _merge_kernel,
        grid=(B, nt),
        in_specs=[
            pl.BlockSpec((tm, D_MODEL), rowmap),
            pl.BlockSpec((tm, 768), lambda b, i: (b * nt + i, OFF_CONV // 768)),
            pl.BlockSpec((tm, 256), rowmap),
            pl.BlockSpec((tm, 256), rowmap),
            pl.BlockSpec((tm, 256), rowmap),
            pl.BlockSpec((None, 8, 256), lmap),
            pl.BlockSpec((None, 1, D_MODEL), lmap),
            pl.BlockSpec((None, D_MODEL, N_BRANCHES * D_MODEL), lmap, pipeline_mode=pl.Buffered(1)),
            pl.BlockSpec((None, N_BRANCHES, MIX_WIDTH, D_MODEL), lambda b, i: (layer, 0, 0, 0),
                         pipeline_mode=pl.Buffered(1)),
            pl.BlockSpec((None, D_MODEL, D_MODEL), lmap, pipeline_mode=pl.Buffered(1)),
            pl.BlockSpec((None, 1, D_MODEL), lmap),
        ],
        out_specs=pl.BlockSpec((tm, D_MODEL), rowmap),
        out_shape=jax.ShapeDtypeStruct((B * S, D_MODEL), F32),
        scratch_shapes=[pltpu.VMEM((8, 256), F32)],
        compiler_params=_cparams(("parallel", "arbitrary")),
        name="merge_mix",
    )(x, z, o_nsa, o_ret, o_rwkv, conv_w, g_pre, w_gate, w_branch, w_out, g_post)


def _xattn_kernel(x_ref, kv_ref, gpre_ref, wq_ref, wo_ref, gpost_ref, o_ref):
    n_sub = 1
    rows = x_ref.shape[0] // n_sub

    def part(r0):
        rs = pl.ds(r0, rows)
        x = x_ref[rs, :]
        h = _rms(x, gpre_ref[...]).astype(BF16)
        yield
        q = (_dot(h, wq_ref[...]) * (XA_HEAD_DIM ** -0.5)).astype(BF16)
        yield
        outs = []
        for hd in range(XA_HEADS):
            cs = slice(hd * XA_HEAD_DIM, (hd + 1) * XA_HEAD_DIM)
            kh = kv_ref[:, hd * XA_HEAD_DIM:(hd + 1) * XA_HEAD_DIM]
            vh = kv_ref[:, D_MODEL + hd * XA_HEAD_DIM:D_MODEL + (hd + 1) * XA_HEAD_DIM]
            s = _dot_nt(q[:, cs], kh)
            yield
            e = jnp.exp(s - _rowmax(s))
            p = e / _rowsum(e)
            outs.append(_dot(p.astype(BF16), vh).astype(BF16))
            yield
        o = jnp.concatenate(outs, axis=1)
        y = _dot(o, wo_ref[...])
        yield
        o_ref[rs, :] = x + _rms(y, gpost_ref[...])

    _round_robin([part(i * rows) for i in range(n_sub)])


def cross_attention(x, kv, g_pre, wq, wo, g_post, layer, B, S, M, tm):
    nt = S // tm
    rowmap = lambda b, i: (b * nt + i, 0)
    lmap = lambda b, i: (layer, 0, 0)
    return pl.pallas_call(
        _xattn_kernel,
        grid=(B, nt),
        in_specs=[
            pl.BlockSpec((tm, D_MODEL), rowmap),
            pl.BlockSpec((M, 2 * D_MODEL), lambda b, i: (b, 0)),
            pl.BlockSpec((None, 1, D_MODEL), lmap),
            pl.BlockSpec((None, D_MODEL, D_MODEL), lmap),
            pl.BlockSpec((None, D_MODEL, D_MODEL), lmap),
            pl.BlockSpec((None, 1, D_MODEL), lmap),
        ],
        out_specs=pl.BlockSpec((tm, D_MODEL), rowmap),
        out_shape=jax.ShapeDtypeStruct((B * S, D_MODEL), F32),
        compiler_params=_cparams(("parallel", "arbitrary")),
        name="cross_attention",
    )(x, kv, g_pre, wq, wo, g_post)


def _mlp_kernel(x_ref, gpre_ref, w1_ref, w2_ref, gpost_ref, o_ref):
    x = x_ref[...]
    h = _rms(x, gpre_ref[...]).astype(BF16)
    tf = 1024
    acc = jnp.zeros(x.shape, F32)
    for f in range(D_FF // tf):
        a = jnp.maximum(_dot(h, w1_ref[:, f * tf:(f + 1) * tf]), 0.0)
        acc = acc + _dot((a * a).astype(BF16), w2_ref[f * tf:(f + 1) * tf, :])
    o_ref[...] = x + _rms(acc, gpost_ref[...])


def mlp(x, g_pre, w1, w2, g_post, layer, tm):
    M = x.shape[0]
    lmap = lambda i: (layer, 0, 0)
    return pl.pallas_call(
        _mlp_kernel,
        grid=(M // tm,),
        in_specs=[
            pl.BlockSpec((tm, D_MODEL), lambda i: (i, 0)),
            pl.BlockSpec((None, 1, D_MODEL), lmap),
            pl.BlockSpec((None, D_MODEL, D_FF), lmap, pipeline_mode=pl.Buffered(1)),
            pl.BlockSpec((None, D_FF, D_MODEL), lmap, pipeline_mode=pl.Buffered(1)),
            pl.BlockSpec((None, 1, D_MODEL), lmap),
        ],
        out_specs=pl.BlockSpec((tm, D_MODEL), lambda i: (i, 0)),
        out_shape=jax.ShapeDtypeStruct((M, D_MODEL), F32),
        compiler_params=_cparams(("parallel",)),
        name="mlp",
    )(x, g_pre, w1, w2, g_post)


def _t5_bucket(dist):
    n = jnp.maximum(dist, 0)
    max_exact = N_BUCKETS // 2
    nf = jnp.maximum(n, 1).astype(F32)
    large = max_exact + (jnp.log(nf / max_exact) / math.log(MAX_DISTANCE / max_exact)
                         * (N_BUCKETS - max_exact)).astype(jnp.int32)
    large = jnp.minimum(large, N_BUCKETS - 1)
    return jnp.where(n < max_exact, n, large)


def _nsa_tables(rel_bias, S):
    n_qb = S // Q_BLOCK
    n_cmp = S // NSA_CMP_STRIDE
    n_blk = S // NSA_SLC_LEN
    bias_f = rel_bias.astype(F32)
    i = jnp.arange(Q_BLOCK)
    per_qb = Q_BLOCK // NSA_CMP_STRIDE
    g = jnp.arange(n_cmp + per_qb * (n_qb - 1)) - per_qb * (n_qb - 1)
    d_c = i[None, :] - (g[:, None] * NSA_CMP_STRIDE + NSA_CMP_LEN - 1)
    def lookup(dist):
        hit = _t5_bucket(dist)[..., None, None] == jnp.arange(N_BUCKETS)[:, None]
        return jnp.sum(jnp.where(hit, bias_f, 0.0), axis=-2)

    bias_c = lookup(d_c).transpose(0, 2, 1).reshape(g.shape[0], MIX_HEADS * Q_BLOCK)
    d0 = i[:, None] - i[None, :]
    tab = jnp.stack([lookup(d0), lookup(d0 + Q_BLOCK)]) - bias_f[N_BUCKETS - 1]
    tab = tab.transpose(0, 2, 3, 1).reshape(2, Q_BLOCK, MIX_HEADS * Q_BLOCK)
    cs = np.arange(n_cmp)[None, :] * NSA_CMP_STRIDE
    ss = np.arange(n_blk)[:, None] * NSA_SLC_LEN
    ovt = np.clip(np.minimum(cs + NSA_CMP_LEN, ss + NSA_SLC_LEN) - np.maximum(cs, ss), 0, None)
    ovt = ovt.astype(np.float32) / NSA_CMP_LEN
    ovt[:, (S - NSA_CMP_LEN) // NSA_CMP_STRIDE + 1:] = 0.0
    return bias_c * LOG2E, tab * LOG2E, jnp.asarray(ovt, BF16)


def _ret_tables(S):
    L = RET_CHUNK
    H = MIX_HEADS
    half = HEAD_DIM // 2
    pos = jnp.arange(S)
    inv_freq = ROPE_BASE ** (-jnp.arange(half, dtype=F32) / half)
    ang = pos.astype(F32)[:, None] * inv_freq[None, :]
    cos = jnp.tile(jnp.cos(ang), (1, H))
    sin = jnp.tile(jnp.sin(ang), (1, H))
    lg = jnp.log(1.0 - 2.0 ** (-5.0 - jnp.arange(H, dtype=F32)))
    n = jnp.arange(L, dtype=F32)
    diff = n[:, None] - n[None, :]
    inner = jnp.where(diff >= 0, jnp.exp(jnp.maximum(diff, 0.0)[None] * lg[:, None, None]), 0.0)
    indec = inner.reshape(H * L, L)
    q_decay = jnp.exp((n + 1.0)[None, :] * lg[:, None])
    k_decay = jnp.exp((L - 1.0 - n)[None, :] * lg[:, None])
    chunk_decay = jnp.exp(L * lg)
    lane = np.arange(256)
    head_v = lane // HEAD_DIM
    head_qk = (lane % 128) // half
    qdec = q_decay.T[:, head_v]
    kdec = k_decay.T[:, head_qk]
    cdec = chunk_decay[head_v][None, :]
    bd = jnp.asarray((head_qk[:, None] == head_v[None, :]).astype(np.float32))
    segm = jnp.asarray((head_v[:, None] == head_v[None, :]).astype(np.float32) / HEAD_DIM, BF16)
    return cos, sin, indec, qdec, kdec, cdec, bd, segm


def _rwkv_consts():
    C = RWKV_CHUNK
    t = np.arange(C)
    tri = (t[:, None] >= t[None, :]).astype(np.float32)
    r = np.arange(4 * C)
    bdl = ((r[:, None] // C == r[None, :] // C) & (r[:, None] % C > r[None, :] % C)).astype(np.float32)
    strict = (t[:, None] > (r[None, :] % C)).astype(np.float32)
    incl = (t[:, None] >= (r[None, :] % C)).astype(np.float32)
    lane = np.arange(256) // HEAD_DIM
    seg = (lane[:, None] == lane[None, :]).astype(np.float32)
    return (jnp.asarray(tri, BF16), jnp.asarray(bdl), jnp.asarray(np.stack([strict, incl])),
            jnp.asarray(seg, BF16), jnp.asarray(seg / HEAD_DIM, BF16))


def _hi_lo(w):
    hi = w.astype(BF16)
    lo = (w - hi.astype(F32)).astype(BF16)
    return jnp.stack([hi, lo], axis=1)


def _pad_rows(w, top, total):
    return jnp.pad(w, ((0, 0), (top, total - top - w.shape[1]), (0, 0)))


def _layout_w_in(w_in):
    L = w_in.shape[0]
    o = 0
    nsa_q = w_in[:, :, o:o + 256]; o += 256
    nsa_kv = w_in[:, :, o:o + 384]; o += 384
    nsa_g = w_in[:, :, o:o + 12]; o += 12
    ret = w_in[:, :, o:o + 1024]; o += 1024
    rwkv = w_in[:, :, o:o + RWKV_COLS]; o += RWKV_COLS
    conv = w_in[:, :, o:o + 768]; o += 768
    gate = w_in[:, :, o:o + 4096]

    def rot_perm(w):
        half = HEAD_DIM // 2
        return [w[:, :, h * HEAD_DIM + p * half:h * HEAD_DIM + (p + 1) * half]
                for p in range(2) for h in range(MIX_HEADS)]

    ret = rot_perm(ret[:, :, 0:256]) + rot_perm(ret[:, :, 256:512]) + [ret[:, :, 512:]]
    zeros = lambda n: jnp.zeros((L, D_MODEL, n), w_in.dtype)
    out = jnp.concatenate([conv, nsa_q] + ret + [rwkv, zeros(1024 - RWKV_COLS), nsa_kv, nsa_g,
                                                 zeros(128 - 12)], axis=2)
    assert out.shape[2] == Z_COLS
    return out.astype(BF16), gate.astype(BF16)


def _layout_cmp(cmp_w, cmp_pe):
    L = cmp_w.shape[0]
    wk = cmp_w[:, 0]
    wv = cmp_w[:, 1]
    zero = jnp.zeros_like(wk)
    blk = jnp.concatenate([jnp.concatenate([wk, zero], axis=3), jnp.concatenate([zero, wv], axis=3)], axis=2)
    pe2 = jnp.concatenate([cmp_pe, cmp_pe], axis=2)[:, :, None, :]
    s = NSA_CMP_STRIDE
    return blk[:, :s].astype(BF16), blk[:, s:].astype(BF16), pe2[:, :s], pe2[:, s:]


def kernel(x, mem, ln_mix_pre, w_in, nsa_cmp_w, nsa_cmp_pe, rel_bias, ret_norm_g, rwkv_mu, rwkv_w0, rwkv_w2, rwkv_a0, rwkv_a2, rwkv_g2, rwkv_k_k, rwkv_k_a, rwkv_r_k, rwkv_ln_g, rwkv_ln_b, conv_w, w_branch, w_mix_out, ln_mix_post, ln_xa_pre, ln_mem, xa_wq, xa_wkv, xa_wo, ln_xa_post, ln_mlp_pre, mlp_w1, mlp_w2, ln_mlp_post):
    B, S, D = x.shape
    M = mem.shape[1]
    depth = w_in.shape[0]
    row = lambda g: g[:, None, :]

    w_in_b, w_gate_b = _layout_w_in(w_in)
    cmp_lo, cmp_hi, pe_lo, pe_hi = _layout_cmp(nsa_cmp_w, nsa_cmp_pe)
    nsa_tabs = _nsa_tables(rel_bias, S)
    ret_tabs = _ret_tables(S)
    rwkv_consts = _rwkv_consts()
    mu = jnp.pad(rwkv_mu, ((0, 0), (0, 1024 - RWKV_COLS)))[:, None, :]
    lora_w = _hi_lo(jnp.concatenate([_pad_rows(rwkv_w2, 0, 128), _pad_rows(rwkv_a2, RWKV_DECAY_LORA, 128),
                                     _pad_rows(rwkv_g2, RWKV_DECAY_LORA + RWKV_AAA_LORA, 128)], axis=2))
    lora_w = lora_w.reshape(depth, 256, 3 * MIX_WIDTH)
    vec = jnp.stack([rwkv_w0, rwkv_a0, rwkv_k_k, rwkv_k_a, rwkv_r_k, rwkv_ln_g, rwkv_ln_b,
                     jnp.zeros_like(rwkv_w0)], axis=1)
    conv_p = jnp.pad(conv_w, ((0, 0), (0, 8 - CONV_WIDTH), (0, 0)))
    w_branch_b = w_branch.astype(BF16)
    w_mix_out_b = w_mix_out.astype(BF16)
    xa_wq_b = xa_wq.astype(BF16)
    xa_wkv_b = xa_wkv.astype(BF16)
    xa_wo_b = xa_wo.astype(BF16)
    mlp_w1_b = mlp_w1.astype(BF16)
    mlp_w2_b = mlp_w2.astype(BF16)

    xf = x.reshape(B * S, D)
    memf = mem.reshape(B * M, D)
    for l in range(depth):
        z = norm_matmul(xf, row(ln_mix_pre), w_in_b, l, tm=512, tn=Z_COLS, out_dtype=F32)
        prep = nsa_compress(z, cmp_lo, cmp_hi, pe_lo, pe_hi, l, B, S)
        z3 = z.reshape(B, S, Z_COLS)
        o_nsa = nsa_attention(z3, prep, *nsa_tabs, B, S, G=8).reshape(B * S, MIX_WIDTH)
        o_ret = retention(z3, ret_tabs, row(ret_norm_g), l, B, S, G=8).reshape(B * S, MIX_WIDTH)
        o_rwkv = rwkv7(z3, mu, lora_w, vec, rwkv_consts, l, B, S, G=8).reshape(B * S, MIX_WIDTH)
        xf = merge_mix(xf, z, o_nsa, o_ret, o_rwkv, conv_p, row(ln_mix_pre), w_gate_b, w_branch_b,
                       w_mix_out_b, row(ln_mix_post), l, B, S, tm=512)
        kvm = norm_matmul(memf, row(ln_mem), xa_wkv_b, l, tm=2 * M, tn=2 * D_MODEL, out_dtype=BF16)
        xf = cross_attention(xf, kvm, row(ln_xa_pre), xa_wq_b, xa_wo_b, row(ln_xa_post), l, B, S, M, tm=1024)
        xf = mlp(xf, row(ln_mlp_pre), mlp_w1_b, mlp_w2_b, row(ln_mlp_post), l, tm=512)
    return xf.reshape(B, S, D)
```

```python
import functools
import math

import numpy as np
import jax
import jax.numpy as jnp
from jax import lax
from jax.experimental import pallas as pl
from jax.experimental.pallas import tpu as pltpu

F32 = jnp.float32
BF16 = jnp.bfloat16

D_MODEL = 1024
N_BRANCHES = 4
MIX_WIDTH = 256
HEAD_DIM = 64
MIX_HEADS = 4

NSA_CMP_LEN = 32
NSA_CMP_STRIDE = 16
NSA_SLC_LEN = 64
NSA_TOP_N = 8
NSA_WINDOW = 512
Q_BLOCK = 128
NSA_V_ROWS = HEAD_DIM + 16
FORCE_BONUS = 1e4
N_BUCKETS = 32
MAX_DISTANCE = 128

RET_CHUNK = 128
ROPE_BASE = 10000.0
RET_NORM_EPS = 1e-5

RWKV_DECAY_LORA = 32
RWKV_AAA_LORA = 32
RWKV_GATE_LORA = 64
RWKV_GN_EPS = 64e-5
RWKV_COLS = 3 * MIX_WIDTH + RWKV_DECAY_LORA + RWKV_AAA_LORA + RWKV_GATE_LORA
RWKV_CHUNK = 64

CONV_WIDTH = 3
XA_HEADS = 4
XA_HEAD_DIM = D_MODEL // XA_HEADS
D_FF = 4 * D_MODEL

RMS_EPS = 1e-6
LOG2E = math.log2(math.e)
NEG_INF = -1e30
NEG_BIG = -3e38

OFF_CONV = 0
OFF_NSAQ = 768
OFF_RET = 1024
OFF_RWKV = 2048
OFF_NSAKV = 3072
OFF_NSAG = 3456
Z_COLS = 3584

VMEM_LIMIT = 56 * 1024 * 1024


def _cparams(sem):
    return pltpu.CompilerParams(dimension_semantics=sem, vmem_limit_bytes=VMEM_LIMIT)


def _dot(a, b):
    return jnp.dot(a, b, preferred_element_type=F32)


def _dot_nt(a, b):
    return lax.dot_general(a, b, (((1,), (1,)), ((), ())), preferred_element_type=F32)


def _dot_tn(a, b):
    return lax.dot_general(a, b, (((0,), (0,)), ((), ())), preferred_element_type=F32)


def _split2(x):
    hi = x.astype(BF16)
    lo = (x - hi.astype(F32)).astype(BF16)
    return hi, lo


def _dot_x2(x, w_bf16, *more):
    xs = (x,) + more
    rows = x.shape[0]
    parts = [p for xi in xs for p in _split2(xi)]
    out = _dot(jnp.concatenate(parts, axis=0), w_bf16)
    res = [out[2 * i * rows:(2 * i + 1) * rows] + out[(2 * i + 1) * rows:(2 * i + 2) * rows]
           for i in range(len(xs))]
    return res[0] if not more else res


def _rms(x, g):
    ms = jnp.mean(x * x, axis=-1, keepdims=True)
    return x * lax.rsqrt(ms + RMS_EPS) * g


def _norm_matmul_kernel(x_ref, g_ref, w_ref, o_ref, h_ref):
    @pl.when(pl.program_id(1) == 0)
    def _():
        h_ref[...] = _rms(x_ref[...], g_ref[...]).astype(BF16)

    o_ref[...] = _dot(h_ref[...], w_ref[...]).astype(o_ref.dtype)


def norm_matmul(x, g, w, layer, tm, tn, out_dtype):
    M, D = x.shape
    N = w.shape[2]
    w_mode = dict(pipeline_mode=pl.Buffered(1)) if tn == N else {}
    return pl.pallas_call(
        _norm_matmul_kernel,
        grid=(M // tm, N // tn),
        in_specs=[
            pl.BlockSpec((tm, D), lambda i, j: (i, 0)),
            pl.BlockSpec((None, 1, D), lambda i, j: (layer, 0, 0)),
            pl.BlockSpec((None, D, tn), lambda i, j: (layer, 0, j), **w_mode),
        ],
        out_specs=pl.BlockSpec((tm, tn), lambda i, j: (i, j)),
        out_shape=jax.ShapeDtypeStruct((M, N), out_dtype),
        scratch_shapes=[pltpu.VMEM((tm, D), BF16)],
        compiler_params=_cparams(("parallel", "arbitrary")),
        name="norm_matmul",
    )(x, g, w)


def _nsa_compress_kernel(kvc_ref, kvs_ref, kvw_ref, wlo_ref, whi_ref, pelo_ref, pehi_ref,
                         kc_ref, vct_ref, ks_ref, vst_ref, kw_ref, vwt_ref):
    nblk = kc_ref.shape[0]
    n_kt = vwt_ref.shape[0]
    ylo = jnp.zeros((nblk, 128), F32)
    yhi = jnp.zeros((nblk, 128), F32)
    for r in range(NSA_CMP_STRIDE):
        xr = kvc_ref[pl.ds(r, nblk, stride=NSA_CMP_STRIDE), :]
        ylo = ylo + _dot((xr + pelo_ref[r]).astype(BF16), wlo_ref[r])
        yhi = yhi + _dot((xr + pehi_ref[r]).astype(BF16), whi_ref[r])
    y = ylo + pltpu.roll(yhi, nblk - 1, axis=0)
    kc_ref[...] = y.astype(BF16)
    vct_ref[...] = y.T[HEAD_DIM:, :].astype(BF16)
    kw_ref[...] = kvw_ref[...].astype(BF16)
    ones = jnp.ones((NSA_V_ROWS - HEAD_DIM, 128), BF16)
    for kt in range(n_kt):
        rs = slice(kt * 128, (kt + 1) * 128)
        vst_ref[kt] = jnp.concatenate([kvs_ref[rs, :].T[HEAD_DIM:, :].astype(BF16), ones], axis=0)
        vwt_ref[kt] = jnp.concatenate([kvw_ref[rs, :].T[HEAD_DIM:, :].astype(BF16), ones], axis=0)
    S = kvs_ref.shape[0]
    key = lax.broadcasted_iota(jnp.int32, (S, 128), 0)
    lane = lax.broadcasted_iota(jnp.int32, (S, 128), 1)
    onehot = (lane == HEAD_DIM + key // NSA_SLC_LEN).astype(F32)
    ks_ref[0:S, :] = jnp.where(lane < HEAD_DIM, kvs_ref[...], onehot).astype(BF16)
    pad_lane = lax.broadcasted_iota(jnp.int32, (128, 128), 1)
    ks_ref[S:S + 128, :] = (pad_lane == HEAD_DIM + S // NSA_SLC_LEN).astype(BF16)
    vst_ref[n_kt] = jnp.zeros((NSA_V_ROWS, 128), BF16)


def nsa_compress(z, wlo, whi, pelo, pehi, layer, B, S):
    nblk = S // NSA_CMP_STRIDE
    n_kt = S // 128
    c0 = OFF_NSAKV // 128
    wmap = lambda b: (layer, 0, 0, 0)
    b3 = lambda b: (b, 0, 0)
    b4 = lambda b: (b, 0, 0, 0)
    return pl.pallas_call(
        _nsa_compress_kernel,
        grid=(B,),
        in_specs=[
            pl.BlockSpec((S, 128), lambda b: (b, c0)),
            pl.BlockSpec((S, 128), lambda b: (b, c0 + 1)),
            pl.BlockSpec((S, 128), lambda b: (b, c0 + 2)),
            pl.BlockSpec((None, NSA_CMP_STRIDE, 128, 128), wmap),
            pl.BlockSpec((None, NSA_CMP_STRIDE, 128, 128), wmap),
            pl.BlockSpec((None, NSA_CMP_STRIDE, 1, 128), wmap),
            pl.BlockSpec((None, NSA_CMP_STRIDE, 1, 128), wmap),
        ],
        out_specs=[
            pl.BlockSpec((None, nblk, 128), b3),
            pl.BlockSpec((None, HEAD_DIM, nblk), b3),
            pl.BlockSpec((None, S + 128, 128), b3),
            pl.BlockSpec((None, n_kt + 1, NSA_V_ROWS, 128), b4),
            pl.BlockSpec((None, S, 128), b3),
            pl.BlockSpec((None, n_kt, NSA_V_ROWS, 128), b4),
        ],
        out_shape=[
            jax.ShapeDtypeStruct((B, nblk, 128), BF16),
            jax.ShapeDtypeStruct((B, HEAD_DIM, nblk), BF16),
            jax.ShapeDtypeStruct((B, S + 128, 128), BF16),
            jax.ShapeDtypeStruct((B, n_kt + 1, NSA_V_ROWS, 128), BF16),
            jax.ShapeDtypeStruct((B, S, 128), BF16),
            jax.ShapeDtypeStruct((B, n_kt, NSA_V_ROWS, 128), BF16),
        ],
        compiler_params=_cparams(("parallel",)),
        name="nsa_compress",
    )(z, z, z, wlo, whi, pelo, pehi)


def _rowmax(s):
    return jnp.max(s, axis=-1, keepdims=True)


def _rowsum(s):
    return jnp.sum(s, axis=-1, keepdims=True)


def _colmax(s):
    return jnp.max(s, axis=0, keepdims=True)


def _colsum(s):
    return jnp.sum(s, axis=0, keepdims=True)


def _nsa_kernel(q_ref, gl_ref, kc_ref, vct_ref, ks_ref, vst_ref, kw_ref, vwt_ref, bc_ref, tab_ref,
                ovt_ref, o_ref):
    bi = pl.program_id(1)
    G = q_ref.shape[0]
    n_kt = vwt_ref.shape[1]
    n_blk = ovt_ref.shape[0]
    n_cmp = kc_ref.shape[1]
    QB = Q_BLOCK
    R = MIX_HEADS * QB
    n_off = NSA_WINDOW // 128

    lo = jnp.maximum(bi - n_off, 0)
    q4 = [None] * G
    q4_sel = [None] * G
    o_c = [None] * G
    carry_s = [None] * G
    carry_w = [None] * G

    def key_tile(kref, g, kt, lanes):
        return kref[g, pl.ds(pl.multiple_of(kt * 128, 128), 128), lanes]

    def pair_qk(g, k0, k1):
        keys = jnp.concatenate([key_tile(kw_ref, g, k0, slice(0, HEAD_DIM)),
                                key_tile(kw_ref, g, k1, slice(0, HEAD_DIM))], axis=0)
        qk = _dot(keys, q4[g])
        return qk[:128], qk[128:]

    def pair_qk_sel(g, k0, k1):
        keys = jnp.concatenate([key_tile(ks_ref, g, k0, slice(None)), key_tile(ks_ref, g, k1, slice(None))],
                               axis=0)
        qk = _dot(keys, q4_sel[g])
        return qk[:128], qk[128:]

    def head_cols(x, h):
        return x[:, h * QB:(h + 1) * QB]

    key1 = lax.broadcasted_iota(jnp.int32, (128, QB), 0)
    i1 = lax.broadcasted_iota(jnp.int32, (128, QB), 1)
    causal1 = key1 <= i1
    kp = jnp.maximum(bi - 1, 0)
    pen_prev = jnp.where(bi >= 1, 0.0, NEG_INF)

    def win_add(kt, valid):
        shift = jnp.where(valid, jnp.where(bi - kt < n_off, -4096, 0), 4096)
        return jnp.where(key1 > i1 + shift, 0.0, NEG_INF)

    def diag_add(extra):
        return [jnp.where(causal1, head_cols(tab_ref[0], h) + extra, NEG_INF) for h in range(MIX_HEADS)]

    def prev_add(extra):
        return [head_cols(tab_ref[1], h) + extra for h in range(MIX_HEADS)]

    def pair_update(carry, qk0, add0, vt0, qk1, add1, vt1):
        p0s, p1s, ms, alphas = [], [], [], []
        for h in range(MIX_HEADS):
            a0 = head_cols(qk0, h) if add0 is None else head_cols(qk0, h) + add0[h]
            a1 = head_cols(qk1, h) if add1 is None else head_cols(qk1, h) + add1[h]
            m_new = jnp.maximum(_colmax(a0), _colmax(a1))
            if carry is not None:
                m_old = head_cols(carry[0], h)
                m_new = jnp.maximum(m_old, m_new)
                alphas.append(jnp.exp2(m_old - m_new))
            ms.append(m_new)
            p0s.append(jnp.exp2(a0 - m_new).astype(BF16))
            p1s.append(jnp.exp2(a1 - m_new).astype(BF16))
        yield
        probs = jnp.concatenate([jnp.concatenate(p0s, axis=1), jnp.concatenate(p1s, axis=1)], axis=0)
        acc = _dot(jnp.concatenate([vt0, vt1], axis=1), probs)
        if carry is not None:
            acc = jnp.concatenate(alphas, axis=1) * carry[1] + acc
        return jnp.concatenate(ms, axis=1), acc

    def head_part(g):
        qt = (q_ref[g] * (HEAD_DIM ** -0.5 * LOG2E)).T
        q4[g] = jnp.concatenate([qt[h * HEAD_DIM:(h + 1) * HEAD_DIM] for h in range(MIX_HEADS)],
                                axis=1).astype(BF16)
        yield
        qk_d, qk_p = pair_qk(g, bi, kp)
        carry_w[g] = yield from pair_update(None, qk_d, diag_add(0.0), vwt_ref[g, bi],
                                            qk_p, prev_add(pen_prev), vwt_ref[g, kp])
        yield
        g0 = pl.multiple_of((n_kt - 1 - bi) * 8, 8)
        qk = _dot(kc_ref[g, :, :HEAD_DIM], q4[g])
        nrow = lax.broadcasted_iota(jnp.int32, (n_cmp, QB), 0)
        tq = bi * QB + lax.broadcasted_iota(jnp.int32, (n_cmp, QB), 1)
        vis = tq - (nrow * NSA_CMP_STRIDE + (NSA_CMP_LEN - 1)) >= 0
        live = (tq >= NSA_CMP_LEN - 1).astype(F32)
        yield
        p_cs = []
        for h in range(MIX_HEADS):
            s = jnp.where(vis, head_cols(qk, h) + bc_ref[pl.ds(g0, n_cmp), h * QB:(h + 1) * QB], NEG_INF)
            e = jnp.exp2(s - _colmax(s))
            p_cs.append(e / _colsum(e) * live)
        yield
        o_c[g] = _dot(vct_ref[g], jnp.concatenate(p_cs, axis=1).astype(BF16))
        p_sum = p_cs[0] + p_cs[1] + p_cs[2] + p_cs[3]
        p_hi, p_lo = _split2(p_sum)
        imp = _dot(ovt_ref[...], p_hi) + _dot(ovt_ref[...], p_lo)
        yield
        blk = lax.broadcasted_iota(jnp.int32, (n_blk, QB), 0)
        cur = (bi * QB + lax.broadcasted_iota(jnp.int32, (n_blk, QB), 1)) >> 6
        forced = (blk == 0) | (blk == cur) | (blk == cur - 1)
        imp = jnp.where(forced, imp + FORCE_BONUS, imp)
        imp = jnp.where(blk <= cur, imp, NEG_INF)
        blk_f = blk.astype(F32)
        sel = jnp.zeros((n_blk, QB), F32)
        for _ in range(NSA_TOP_N):
            mx = _colmax(imp)
            idx = jnp.min(jnp.where(imp == mx, blk_f, 4096.0), axis=0, keepdims=True)
            pick = blk_f == idx
            sel = jnp.where(pick, 1.0, sel)
            imp = jnp.where(pick, NEG_BIG, imp)
            yield
        sel_add = (sel - 1.0) * (-NEG_INF)
        rows_left = 128 - HEAD_DIM - n_blk
        pad_rows = jnp.where(lax.broadcasted_iota(jnp.int32, (rows_left, QB), 0) == 0, NEG_INF, 0.0)
        mask_rows = jnp.concatenate([sel_add, pad_rows], axis=0).astype(BF16)
        q4_sel[g] = jnp.concatenate([q4[g], jnp.concatenate([mask_rows] * MIX_HEADS, axis=1)], axis=0)
        yield
        qk_d, qk_p = pair_qk_sel(g, bi, kp)
        carry_s[g] = yield from pair_update(None, qk_d, diag_add(0.0), vst_ref[g, bi],
                                            qk_p, prev_add(pen_prev), vst_ref[g, kp])

    _round_robin([head_part(g) for g in range(G)])

    n_old = jnp.maximum(bi - 1, 0)

    def sel_step(g, j, carry, out):
        k0 = 2 * j
        k1 = jnp.where(k0 + 1 < n_old, k0 + 1, n_kt)
        qk0, qk1 = pair_qk_sel(g, k0, k1)
        yield
        out[g] = yield from pair_update(carry, qk0, None, vst_ref[g, k0], qk1, None, vst_ref[g, k1])

    def win_step(g, j, carry, out):
        k0 = lo + 2 * j
        k1 = jnp.minimum(k0 + 1, bi)
        qk0, qk1 = pair_qk(g, k0, k1)
        yield
        out[g] = yield from pair_update(carry, qk0, [win_add(k0, True)] * MIX_HEADS, vwt_ref[g, k0],
                                        qk1, [win_add(k1, k0 + 1 < n_old)] * MIX_HEADS, vwt_ref[g, k1])

    def joint(step):
        def body(j, carries):
            out = [None] * G
            _round_robin([step(g, j, carries[g], out) for g in range(G)])
            return tuple(out)
        return body

    res_s = lax.fori_loop(0, (n_old + 1) // 2, joint(sel_step), tuple(carry_s))
    res_w = lax.fori_loop(0, (n_old - lo + 1) // 2, joint(win_step), tuple(carry_w))

    def tail_part(g):
        acc_s = res_s[g][1]
        acc_w = res_w[g][1]
        o_s = acc_s[:HEAD_DIM] * (1.0 / acc_s[HEAD_DIM:HEAD_DIM + 1])
        o_w = acc_w[:HEAD_DIM] * (1.0 / acc_w[HEAD_DIM:HEAD_DIM + 1])
        gate = jax.nn.sigmoid(gl_ref[g].T)
        yield
        outs = []
        for h in range(MIX_HEADS):
            cs = slice(h * QB, (h + 1) * QB)
            outs.append(gate[3 * h:3 * h + 1] * o_c[g][:, cs] + gate[3 * h + 1:3 * h + 2] * o_s[:, cs]
                        + gate[3 * h + 2:3 * h + 3] * o_w[:, cs])
        o_ref[g] = jnp.concatenate(outs, axis=0).T

    _round_robin([tail_part(g) for g in range(G)])


def nsa_attention(z, prep, bias_c, tab, ovt, B, S, G):
    n_qb = S // Q_BLOCK
    n_kt = S // 128
    n_cmp = S // NSA_CMP_STRIDE
    R = MIX_HEADS * Q_BLOCK
    kc, vct, ks, vst, kw, vwt = prep
    b3 = lambda b, i: (b, 0, 0)
    b4 = lambda b, i: (b, 0, 0, 0)
    return pl.pallas_call(
        _nsa_kernel,
        grid=(B // G, n_qb),
        in_specs=[
            pl.BlockSpec((G, Q_BLOCK, MIX_WIDTH), lambda b, i: (b, i, OFF_NSAQ // MIX_WIDTH)),
            pl.BlockSpec((G, Q_BLOCK, 128), lambda b, i: (b, i, OFF_NSAG // 128)),
            pl.BlockSpec((G, n_cmp, 128), b3),
            pl.BlockSpec((G, HEAD_DIM, n_cmp), b3),
            pl.BlockSpec((G, S + 128, 128), b3),
            pl.BlockSpec((G, n_kt + 1, NSA_V_ROWS, 128), b4),
            pl.BlockSpec((G, S, 128), b3),
            pl.BlockSpec((G, n_kt, NSA_V_ROWS, 128), b4),
            pl.BlockSpec(bias_c.shape, lambda b, i: (0, 0)),
            pl.BlockSpec((2, 128, R), lambda b, i: (0, 0, 0)),
            pl.BlockSpec(ovt.shape, lambda b, i: (0, 0)),
        ],
        out_specs=pl.BlockSpec((G, Q_BLOCK, MIX_WIDTH), lambda b, i: (b, i, 0)),
        out_shape=jax.ShapeDtypeStruct((B, S, MIX_WIDTH), F32),
        compiler_params=_cparams(("parallel", "arbitrary")),
        name="nsa_attention",
    )(z, z, kc, vct, ks, vst, kw, vwt, bias_c, tab, ovt)


def _stack_heads(x, head_of_lane):
    return jnp.concatenate([jnp.where(head_of_lane == h, x, jnp.zeros_like(x))
                            for h in range(MIX_HEADS)], axis=0)


def _head_norm(y, seg_mean, eps):
    mu = _dot_x2(y, seg_mean)
    d = y - mu
    var = _dot_x2(d * d, seg_mean)
    return d * lax.rsqrt(var + eps)


def _round_robin(chains):
    while chains:
        chains = [c for c in chains if next(c, StopIteration) is not StopIteration]


def _retention_kernel(z_ref, cos_ref, sin_ref, indec_ref, qdec_ref, kdec_ref, cdec_ref, bd_ref,
                      segm_ref, ng_ref, o_ref, st_ref):
    @pl.when(pl.program_id(1) == 0)
    def _():
        st_ref[...] = jnp.zeros_like(st_ref)

    _round_robin([_retention_chunk(z_ref.at[g], cos_ref, sin_ref, indec_ref, qdec_ref, kdec_ref, cdec_ref,
                                   bd_ref, segm_ref, ng_ref, o_ref.at[g], st_ref.at[g])
                  for g in range(z_ref.shape[0])])


def _retention_chunk(z_ref, cos_ref, sin_ref, indec_ref, qdec_ref, kdec_ref, cdec_ref, bd_ref,
                     segm_ref, ng_ref, o_ref, st_ref):
    L = RET_CHUNK
    z = z_ref[...]
    q = z[:, 0:256]
    k = z[:, 256:512]
    v = z[:, 512:768]
    g = z[:, 768:1024]
    cos = cos_ref[...]
    sin = sin_ref[...]

    def rot(u):
        u1 = u[:, :128]
        u2 = u[:, 128:]
        return jnp.concatenate([u1 * cos - u2 * sin, u2 * cos + u1 * sin], axis=1)

    qr = rot(q) * (HEAD_DIM ** -0.5)
    kr = rot(k)
    lane = lax.broadcasted_iota(jnp.int32, (L, 256), 1)
    head_qk = (lane & 127) >> 5
    head_v = lane >> 6
    qb = qr.astype(BF16)
    kb = kr.astype(BF16)
    vb = v.astype(BF16)
    yield

    att = _dot_nt(_stack_heads(qb, head_qk), kb) * indec_ref[...]
    yield
    o_st = _dot(att.astype(BF16), vb)
    o = jnp.zeros((L, 256), F32)
    for h in range(MIX_HEADS):
        o = o + jnp.where(head_v == h, o_st[h * L:(h + 1) * L], 0.0)
    yield
    state = st_ref[...]
    o = o + _dot(qb, state.astype(BF16)) * qdec_ref[...]
    st_ref[...] = state * cdec_ref[...] + _dot_tn((kr * kdec_ref[...]).astype(BF16), vb) * bd_ref[...]
    yield

    mu = _dot_x2(o, segm_ref[...])
    yield
    d = o - mu
    var = _dot_x2(d * d, segm_ref[...])
    yield
    o_ref[...] = g * jax.nn.sigmoid(g) * (d * lax.rsqrt(var + RET_NORM_EPS) * ng_ref[...])


def retention(z, tabs, ng, layer, B, S, G):
    L = RET_CHUNK
    nc = S // L
    cos, sin, indec, qdec, kdec, cdec, bd, segm = tabs
    const2 = lambda b, c: (0, 0)
    return pl.pallas_call(
        _retention_kernel,
        grid=(B // G, nc),
        in_specs=[
            pl.BlockSpec((G, L, 1024), lambda b, c: (b, c, OFF_RET // 1024)),
            pl.BlockSpec((L, 128), lambda b, c: (c, 0)),
            pl.BlockSpec((L, 128), lambda b, c: (c, 0)),
            pl.BlockSpec((MIX_HEADS * L, L), const2),
            pl.BlockSpec((L, 256), const2),
            pl.BlockSpec((L, 256), const2),
            pl.BlockSpec((1, 256), const2),
            pl.BlockSpec((256, 256), const2),
            pl.BlockSpec((256, 256), const2),
            pl.BlockSpec((None, 1, 256), lambda b, c: (layer, 0, 0)),
        ],
        out_specs=pl.BlockSpec((G, L, 256), lambda b, c: (b, c, 0)),
        out_shape=jax.ShapeDtypeStruct((B, S, 256), F32),
        scratch_shapes=[pltpu.VMEM((G, 256, 256), F32)],
        compiler_params=_cparams(("parallel", "arbitrary")),
        name="retention",
    )(z, cos, sin, indec, qdec, kdec, cdec, bd, segm, ng)


def _rwkv_kernel(z_ref, mu_ref, lw_ref, vec_ref, tri_ref, bdl_ref, wide_ref,
                 segs_ref, segm_ref, o_ref, st_ref, prev_ref):
    @pl.when(pl.program_id(1) == 0)
    def _():
        st_ref[...] = jnp.zeros_like(st_ref)
        prev_ref[...] = jnp.zeros_like(prev_ref)

    _round_robin([_rwkv_chunk(z_ref.at[g], mu_ref, lw_ref, vec_ref, tri_ref, bdl_ref,
                              wide_ref, segs_ref, segm_ref, o_ref.at[g], st_ref.at[g], prev_ref.at[g])
                  for g in range(z_ref.shape[0])])


def _rwkv_chunk(z_ref, mu_ref, lw_ref, vec_ref, tri_ref, bdl_ref, wide_ref,
                segs_ref, segm_ref, o_ref, st_ref, prev_ref):
    C = RWKV_CHUNK
    z = z_ref[...]
    rows = lax.broadcasted_iota(jnp.int32, (C, 1024), 0)
    zs = jnp.where(rows == 0, prev_ref[0:1, :], pltpu.roll(z, 1, axis=0))
    prev_ref[...] = jnp.broadcast_to(z[C - 1:C, :], prev_ref.shape)
    zf = z + (zs - z) * mu_ref[...]
    r = zf[:, 0:256]
    k = zf[:, 256:512]
    v = zf[:, 512:768]
    lora = zf[:, 768:896]

    vec = vec_ref[...]
    w0, a0, k_k, k_a, r_k, ln_g, ln_b = (vec[i:i + 1] for i in range(7))

    lane = lax.broadcasted_iota(jnp.int32, lora.shape, 1)
    lora_in = jnp.where(lane < RWKV_DECAY_LORA, jnp.tanh(lora),
                        jnp.where(lane < RWKV_DECAY_LORA + RWKV_AAA_LORA, lora, jax.nn.sigmoid(lora)))
    hi, lo = _split2(lora_in)
    lhs = jnp.concatenate([jnp.concatenate([hi, hi], axis=1),
                           jnp.concatenate([lo, jnp.zeros_like(lo)], axis=1)], axis=0)
    both = _dot(lhs, lw_ref[...])
    lora_out = both[:C] + both[C:]
    wpre = w0 + lora_out[:, 0:256]
    y = -wpre
    softplus = jnp.maximum(y, 0.0) + jnp.log(1.0 + jnp.exp(-jnp.abs(y)))
    w_log = -softplus - 0.5
    ld = -jnp.exp(w_log)
    a = jax.nn.sigmoid(a0 + lora_out[:, 256:512])
    gate = lora_out[:, 512:768]
    kk = k * k_k
    k2 = k * (1.0 + (a - 1.0) * k_a)
    kk_sq, bonus = _dot_x2(kk * kk, segs_ref[...], r * k2 * r_k)
    kk = kk / jnp.maximum(jnp.sqrt(kk_sq), 1e-12)
    yield

    tri = tri_ref[...]
    l1 = ld.astype(BF16)
    r1 = ld - l1.astype(F32)
    l2 = r1.astype(BF16)
    l3 = (r1 - l2.astype(F32)).astype(BF16)
    cs = _dot(jnp.concatenate([tri, tri, tri], axis=1), jnp.concatenate([l1, l2, l3], axis=0))
    cs_end = cs[C - 1:C, :]
    yield
    e_neg = jnp.exp(-cs)
    e_end = jnp.exp(cs_end - cs)
    kka = kk * a
    a_t = -kk * jnp.exp(cs - ld)
    r_t = r * jnp.exp(cs)
    b_t = kka * e_neg
    k_t = k2 * e_neg
    b_g = kka * e_end
    k_g = k2 * e_end

    lane = lax.broadcasted_iota(jnp.int32, (C, 256), 1)
    head = lane >> 6
    stack = lambda x: _stack_heads(x.astype(BF16), head)
    a_st, b_st, k_st, v_st = stack(a_t), stack(b_t), stack(k_t), stack(v)

    n_pow = _dot_nt(a_st, b_st) * bdl_ref[...]
    ri = lax.broadcasted_iota(jnp.int32, (4 * C, 4 * C), 0)
    ci = lax.broadcasted_iota(jnp.int32, (4 * C, 4 * C), 1)
    t_inv = jnp.where(ri == ci, 1.0, 0.0) + n_pow
    yield

    ar = jnp.concatenate([a_t, r_t], axis=0).astype(BF16)
    bk = jnp.concatenate([b_st, k_st], axis=0)
    wide = _dot_nt(ar, bk)
    strict = wide_ref[0]
    incl = wide_ref[1]
    a_ak = (wide[0:C, 4 * C:] * strict).astype(BF16)
    a_rb = (wide[C:, 0:4 * C] * incl).astype(BF16)
    a_rk = (wide[C:, 4 * C:] * incl).astype(BF16)
    yield

    sq = 2
    while sq < C:
        nb = n_pow.astype(BF16)
        n_pow = _dot(nb, nb)
        yield
        t_inv = t_inv + _dot(t_inv.astype(BF16), n_pow.astype(BF16))
        sq *= 2
    yield

    state = st_ref[...]
    sb = state.astype(BF16)
    from_state = _dot_nt(ar, sb)
    x = from_state[0:C] + _dot(a_ak, v_st)
    yield
    u_st = _dot(t_inv.astype(BF16), stack(x)).astype(BF16)
    yield
    uv = jnp.concatenate([u_st, v_st], axis=0)
    yv = from_state[C:] + _dot(jnp.concatenate([a_rb, a_rk], axis=1), uv)
    bkg = jnp.concatenate([stack(b_g), stack(k_g)], axis=0)
    st_ref[...] = state * jnp.exp(cs_end) + _dot_tn(uv, bkg)
    yield

    yn = _head_norm(yv, segm_ref[...], RWKV_GN_EPS) * ln_g + ln_b
    yn = yn + bonus * v
    o_ref[...] = yn * gate


def rwkv7(z, mu, lora_w, vec, consts, layer, B, S, G):
    C = RWKV_CHUNK
    nc = S // C
    tri, bdl, wide, segs, segm = consts
    const2 = lambda b, c: (0, 0)
    return pl.pallas_call(
        _rwkv_kernel,
        grid=(B // G, nc),
        in_specs=[
            pl.BlockSpec((G, C, 1024), lambda b, c: (b, c, OFF_RWKV // 1024)),
            pl.BlockSpec((None, 1, 1024), lambda b, c: (layer, 0, 0)),
            pl.BlockSpec((None, 256, 768), lambda b, c: (layer, 0, 0)),
            pl.BlockSpec((None, 8, 256), lambda b, c: (layer, 0, 0)),
            pl.BlockSpec((C, C), const2),
            pl.BlockSpec((4 * C, 4 * C), const2),
            pl.BlockSpec((2, C, 4 * C), lambda b, c: (0, 0, 0)),
            pl.BlockSpec((256, 256), const2),
            pl.BlockSpec((256, 256), const2),
        ],
        out_specs=pl.BlockSpec((G, C, 256), lambda b, c: (b, c, 0)),
        out_shape=jax.ShapeDtypeStruct((B, S, 256), F32),
        scratch_shapes=[pltpu.VMEM((G, 256, 256), F32), pltpu.VMEM((G, 8, 1024), F32)],
        compiler_params=_cparams(("parallel", "arbitrary")),
        name="rwkv7",
    )(z, mu, lora_w, vec, tri, bdl, wide, segs, segm)


def _merge_kernel(x_ref, zc_ref, on_ref, or_ref, ow_ref, cw_ref, gpre_ref, wg_ref, wb_ref, wo_ref, g_ref,
                  o_ref, carry_ref):
    tm = x_ref.shape[0]
    x = x_ref[...]
    h = _rms(x, gpre_ref[...]).astype(BF16)

    @pl.when(pl.program_id(1) == 0)
    def _():
        carry_ref[...] = jnp.zeros_like(carry_ref)

    zc = zc_ref[...]
    b_g = zc[:, 0:256]
    u = zc[:, 256:512] * zc[:, 512:768]
    prev = carry_ref[...]
    rows = lax.broadcasted_iota(jnp.int32, (tm, 256), 0)
    u1 = jnp.where(rows == 0, prev[7:8], pltpu.roll(u, 1, axis=0))
    u2 = jnp.where(rows == 0, prev[6:7], jnp.where(rows == 1, prev[7:8], pltpu.roll(u, 2, axis=0)))
    carry_ref[...] = u[tm - 8:tm]
    cw = cw_ref[...]
    o_conv = b_g * (cw[0:1] * u2 + cw[1:2] * u1 + cw[2:3] * u)

    branches = (on_ref[...], or_ref[...], ow_ref[...], o_conv)
    merged = jnp.zeros((tm, D_MODEL), F32)
    for m in range(N_BRANCHES):
        gm = jax.nn.sigmoid(_dot(h, wg_ref[:, m * D_MODEL:(m + 1) * D_MODEL]))
        merged = merged + gm * _dot(branches[m].astype(BF16), wb_ref[m])
    y = _dot(merged.astype(BF16), wo_ref[...])
    o_ref[...] = x + _rms(y, g_ref[...])


def merge_mix(x, z, o_nsa, o_ret, o_rwkv, conv_w, g_pre, w_gate, w_branch, w_out, g_post, layer, B, S, tm):
    nt = S // tm
    rowmap = lambda b, i: (b * nt + i, 0)
    lmap = lambda b, i: (layer, 0, 0)
    return pl.pallas_call(
        _merge_kernel,
        grid=(B, nt),
        in_specs=[
            pl.BlockSpec((tm, D_MODEL), rowmap),
            pl.BlockSpec((tm, 768), lambda b, i: (b * nt + i, OFF_CONV // 768)),
            pl.BlockSpec((tm, 256), rowmap),
            pl.BlockSpec((tm, 256), rowmap),
            pl.BlockSpec((tm, 256), rowmap),
            pl.BlockSpec((None, 8, 256), lmap),
            pl.BlockSpec((None, 1, D_MODEL), lmap),
            pl.BlockSpec((None, D_MODEL, N_BRANCHES * D_MODEL), lmap, pipeline_mode=pl.Buffered(1)),
            pl.BlockSpec((None, N_BRANCHES, MIX_WIDTH, D_MODEL), lambda b, i: (layer, 0, 0, 0),
                         pipeline_mode=pl.Buffered(1)),
            pl.BlockSpec((None, D_MODEL, D_MODEL), lmap, pipeline_mode=pl.Buffered(1)),
            pl.BlockSpec((None, 1, D_MODEL), lmap),
        ],
        out_specs=pl.BlockSpec((tm, D_MODEL), rowmap),
        out_shape=jax.ShapeDtypeStruct((B * S, D_MODEL), F32),
        scratch_shapes=[pltpu.VMEM((8, 256), F32)],
        compiler_params=_cparams(("parallel", "arbitrary")),
        name="merge_mix",
    )(x, z, o_nsa, o_ret, o_rwkv, conv_w, g_pre, w_gate, w_branch, w_out, g_post)


def _xattn_kernel(x_ref, kv_ref, gpre_ref, wq_ref, wo_ref, gpost_ref, o_ref):
    n_sub = 1
    rows = x_ref.shape[0] // n_sub

    def part(r0):
        rs = pl.ds(r0, rows)
        x = x_ref[rs, :]
        h = _rms(x, gpre_ref[...]).astype(BF16)
        yield
        q = (_dot(h, wq_ref[...]) * (XA_HEAD_DIM ** -0.5)).astype(BF16)
        yield
        outs = []
        for hd in range(XA_HEADS):
            cs = slice(hd * XA_HEAD_DIM, (hd + 1) * XA_HEAD_DIM)
            kh = kv_ref[:, hd * XA_HEAD_DIM:(hd + 1) * XA_HEAD_DIM]
            vh = kv_ref[:, D_MODEL + hd * XA_HEAD_DIM:D_MODEL + (hd + 1) * XA_HEAD_DIM]
            s = _dot_nt(q[:, cs], kh)
            yield
            e = jnp.exp(s - _rowmax(s))
            p = e / _rowsum(e)
            outs.append(_dot(p.astype(BF16), vh).astype(BF16))
            yield
        o = jnp.concatenate(outs, axis=1)
        y = _dot(o, wo_ref[...])
        yield
        o_ref[rs, :] = x + _rms(y, gpost_ref[...])

    _round_robin([part(i * rows) for i in range(n_sub)])


def cross_attention(x, kv, g_pre, wq, wo, g_post, layer, B, S, M, tm):
    nt = S // tm
    rowmap = lambda b, i: (b * nt + i, 0)
    lmap = lambda b, i: (layer, 0, 0)
    return pl.pallas_call(
        _xattn_kernel,
        grid=(B, nt),
        in_specs=[
            pl.BlockSpec((tm, D_MODEL), rowmap),
            pl.BlockSpec((M, 2 * D_MODEL), lambda b, i: (b, 0)),
            pl.BlockSpec((None, 1, D_MODEL), lmap),
            pl.BlockSpec((None, D_MODEL, D_MODEL), lmap),
            pl.BlockSpec((None, D_MODEL, D_MODEL), lmap),
            pl.BlockSpec((None, 1, D_MODEL), lmap),
        ],
        out_specs=pl.BlockSpec((tm, D_MODEL), rowmap),
        out_shape=jax.ShapeDtypeStruct((B * S, D_MODEL), F32),
        compiler_params=_cparams(("parallel", "arbitrary")),
        name="cross_attention",
    )(x, kv, g_pre, wq, wo, g_post)


def _mlp_kernel(x_ref, gpre_ref, w1_ref, w2_ref, gpost_ref, o_ref):
    x = x_ref[...]
    h = _rms(x, gpre_ref[...]).astype(BF16)
    tf = 1024
    acc = jnp.zeros(x.shape, F32)
    for f in range(D_FF // tf):
        a = jnp.maximum(_dot(h, w1_ref[:, f * tf:(f + 1) * tf]), 0.0)
        acc = acc + _dot((a * a).astype(BF16), w2_ref[f * tf:(f + 1) * tf, :])
    o_ref[...] = x + _rms(acc, gpost_ref[...])


def mlp(x, g_pre, w1, w2, g_post, layer, tm):
    M = x.shape[0]
    lmap = lambda i: (layer, 0, 0)
    return pl.pallas_call(
        _mlp_kernel,
        grid=(M // tm,),
        in_specs=[
            pl.BlockSpec((tm, D_MODEL), lambda i: (i, 0)),
            pl.BlockSpec((None, 1, D_MODEL), lmap),
            pl.BlockSpec((None, D_MODEL, D_FF), lmap, pipeline_mode=pl.Buffered(1)),
            pl.BlockSpec((None, D_FF, D_MODEL), lmap, pipeline_mode=pl.Buffered(1)),
            pl.BlockSpec((None, 1, D_MODEL), lmap),
        ],
        out_specs=pl.BlockSpec((tm, D_MODEL), lambda i: (i, 0)),
        out_shape=jax.ShapeDtypeStruct((M, D_MODEL), F32),
        compiler_params=_cparams(("parallel",)),
        name="mlp",
    )(x, g_pre, w1, w2, g_post)


def _t5_bucket(dist):
    n = jnp.maximum(dist, 0)
    max_exact = N_BUCKETS // 2
    nf = jnp.maximum(n, 1).astype(F32)
    large = max_exact + (jnp.log(nf / max_exact) / math.log(MAX_DISTANCE / max_exact)
                         * (N_BUCKETS - max_exact)).astype(jnp.int32)
    large = jnp.minimum(large, N_BUCKETS - 1)
    return jnp.where(n < max_exact, n, large)


def _nsa_tables(rel_bias, S):
    n_qb = S // Q_BLOCK
    n_cmp = S // NSA_CMP_STRIDE
    n_blk = S // NSA_SLC_LEN
    bias_f = rel_bias.astype(F32)
    i = jnp.arange(Q_BLOCK)
    per_qb = Q_BLOCK // NSA_CMP_STRIDE
    g = jnp.arange(n_cmp + per_qb * (n_qb - 1)) - per_qb * (n_qb - 1)
    d_c = i[None, :] - (g[:, None] * NSA_CMP_STRIDE + NSA_CMP_LEN - 1)
    def lookup(dist):
        hit = _t5_bucket(dist)[..., None, None] == jnp.arange(N_BUCKETS)[:, None]
        return jnp.sum(jnp.where(hit, bias_f, 0.0), axis=-2)

    bias_c = lookup(d_c).transpose(0, 2, 1).reshape(g.shape[0], MIX_HEADS * Q_BLOCK)
    d0 = i[:, None] - i[None, :]
    tab = jnp.stack([lookup(d0), lookup(d0 + Q_BLOCK)]) - bias_f[N_BUCKETS - 1]
    tab = tab.transpose(0, 2, 3, 1).reshape(2, Q_BLOCK, MIX_HEADS * Q_BLOCK)
    cs = np.arange(n_cmp)[None, :] * NSA_CMP_STRIDE
    ss = np.arange(n_blk)[:, None] * NSA_SLC_LEN
    ovt = np.clip(np.minimum(cs + NSA_CMP_LEN, ss + NSA_SLC_LEN) - np.maximum(cs, ss), 0, None)
    ovt = ovt.astype(np.float32) / NSA_CMP_LEN
    ovt[:, (S - NSA_CMP_LEN) // NSA_CMP_STRIDE + 1:] = 0.0
    return bias_c * LOG2E, tab * LOG2E, jnp.asarray(ovt, BF16)


def _ret_tables(S):
    L = RET_CHUNK
    H = MIX_HEADS
    half = HEAD_DIM // 2
    pos = jnp.arange(S)
    inv_freq = ROPE_BASE ** (-jnp.arange(half, dtype=F32) / half)
    ang = pos.astype(F32)[:, None] * inv_freq[None, :]
    cos = jnp.tile(jnp.cos(ang), (1, H))
    sin = jnp.tile(jnp.sin(ang), (1, H))
    lg = jnp.log(1.0 - 2.0 ** (-5.0 - jnp.arange(H, dtype=F32)))
    n = jnp.arange(L, dtype=F32)
    diff = n[:, None] - n[None, :]
    inner = jnp.where(diff >= 0, jnp.exp(jnp.maximum(diff, 0.0)[None] * lg[:, None, None]), 0.0)
    indec = inner.reshape(H * L, L)
    q_decay = jnp.exp((n + 1.0)[None, :] * lg[:, None])
    k_decay = jnp.exp((L - 1.0 - n)[None, :] * lg[:, None])
    chunk_decay = jnp.exp(L * lg)
    lane = np.arange(256)
    head_v = lane // HEAD_DIM
    head_qk = (lane % 128) // half
    qdec = q_decay.T[:, head_v]
    kdec = k_decay.T[:, head_qk]
    cdec = chunk_decay[head_v][None, :]
    bd = jnp.asarray((head_qk[:, None] == head_v[None, :]).astype(np.float32))
    segm = jnp.asarray((head_v[:, None] == head_v[None, :]).astype(np.float32) / HEAD_DIM, BF16)
    return cos, sin, indec, qdec, kdec, cdec, bd, segm


def _rwkv_consts():
    C = RWKV_CHUNK
    t = np.arange(C)
    tri = (t[:, None] >= t[None, :]).astype(np.float32)
    r = np.arange(4 * C)
    bdl = ((r[:, None] // C == r[None, :] // C) & (r[:, None] % C > r[None, :] % C)).astype(np.float32)
    strict = (t[:, None] > (r[None, :] % C)).astype(np.float32)
    incl = (t[:, None] >= (r[None, :] % C)).astype(np.float32)
    lane = np.arange(256) // HEAD_DIM
    seg = (lane[:, None] == lane[None, :]).astype(np.float32)
    return (jnp.asarray(tri, BF16), jnp.asarray(bdl), jnp.asarray(np.stack([strict, incl])),
            jnp.asarray(seg, BF16), jnp.asarray(seg / HEAD_DIM, BF16))


def _hi_lo(w):
    hi = w.astype(BF16)
    lo = (w - hi.astype(F32)).astype(BF16)
    return jnp.stack([hi, lo], axis=1)


def _pad_rows(w, top, total):
    return jnp.pad(w, ((0, 0), (top, total - top - w.shape[1]), (0, 0)))


def _layout_w_in_kernel(w_ref, oz_ref, og_ref):
    w = w_ref[...]
    o = 0
    nsa_q = w[:, o:o + 256]; o += 256
    nsa_kv = w[:, o:o + 384]; o += 384
    nsa_g = w[:, o:o + 3 * MIX_HEADS]; o += 3 * MIX_HEADS
    ret = w[:, o:o + 1024]; o += 1024
    rwkv = w[:, o:o + RWKV_COLS]; o += RWKV_COLS
    conv = w[:, o:o + 768]; o += 768
    gate = w[:, o:o + N_BRANCHES * D_MODEL]

    def rot_perm(x):
        half = HEAD_DIM // 2
        return [x[:, h * HEAD_DIM + p * half:h * HEAD_DIM + (p + 1) * half]
                for p in range(2) for h in range(MIX_HEADS)]

    ret = rot_perm(ret[:, 0:256]) + rot_perm(ret[:, 256:512]) + [ret[:, 512:]]
    zeros = lambda n: jnp.zeros((w.shape[0], n), w.dtype)
    out = jnp.concatenate([conv, nsa_q] + ret + [rwkv, zeros(1024 - RWKV_COLS), nsa_kv, nsa_g,
                                                 zeros(128 - 3 * MIX_HEADS)], axis=1)
    oz_ref[...] = out.astype(BF16)
    og_ref[...] = gate.astype(BF16)


def _layout_w_in(w_in, tr=256):
    L, D, n_in = w_in.shape
    n_gate = N_BRANCHES * D_MODEL
    oz, og = pl.pallas_call(
        _layout_w_in_kernel,
        grid=(L * D // tr,),
        in_specs=[pl.BlockSpec((tr, n_in), lambda i: (i, 0))],
        out_specs=[pl.BlockSpec((tr, Z_COLS), lambda i: (i, 0)), pl.BlockSpec((tr, n_gate), lambda i: (i, 0))],
        out_shape=[jax.ShapeDtypeStruct((L * D, Z_COLS), BF16), jax.ShapeDtypeStruct((L * D, n_gate), BF16)],
        compiler_params=_cparams(("parallel",)),
        name="layout_w_in",
    )(w_in.reshape(L * D, n_in))
    return oz.reshape(L, D, Z_COLS), og.reshape(L, D, n_gate)


def _layout_cmp(cmp_w, cmp_pe):
    L = cmp_w.shape[0]
    wk = cmp_w[:, 0]
    wv = cmp_w[:, 1]
    zero = jnp.zeros_like(wk)
    blk = jnp.concatenate([jnp.concatenate([wk, zero], axis=3), jnp.concatenate([zero, wv], axis=3)], axis=2)
    pe2 = jnp.concatenate([cmp_pe, cmp_pe], axis=2)[:, :, None, :]
    s = NSA_CMP_STRIDE
    return blk[:, :s].astype(BF16), blk[:, s:].astype(BF16), pe2[:, :s], pe2[:, s:]


def kernel(x, mem, ln_mix_pre, w_in, nsa_cmp_w, nsa_cmp_pe, rel_bias, ret_norm_g, rwkv_mu, rwkv_w0, rwkv_w2, rwkv_a0, rwkv_a2, rwkv_g2, rwkv_k_k, rwkv_k_a, rwkv_r_k, rwkv_ln_g, rwkv_ln_b, conv_w, w_branch, w_mix_out, ln_mix_post, ln_xa_pre, ln_mem, xa_wq, xa_wkv, xa_wo, ln_xa_post, ln_mlp_pre, mlp_w1, mlp_w2, ln_mlp_post):
    B, S, D = x.shape
    M = mem.shape[1]
    depth = w_in.shape[0]
    row = lambda g: g[:, None, :]

    w_in_b, w_gate_b = _layout_w_in(w_in)
    cmp_lo, cmp_hi, pe_lo, pe_hi = _layout_cmp(nsa_cmp_w, nsa_cmp_pe)
    nsa_tabs = _nsa_tables(rel_bias, S)
    ret_tabs = _ret_tables(S)
    rwkv_consts = _rwkv_consts()
    mu = jnp.pad(rwkv_mu, ((0, 0), (0, 1024 - RWKV_COLS)))[:, None, :]
    lora_w = _hi_lo(jnp.concatenate([_pad_rows(rwkv_w2, 0, 128), _pad_rows(rwkv_a2, RWKV_DECAY_LORA, 128),
                                     _pad_rows(rwkv_g2, RWKV_DECAY_LORA + RWKV_AAA_LORA, 128)], axis=2))
    lora_w = lora_w.reshape(depth, 256, 3 * MIX_WIDTH)
    vec = jnp.stack([rwkv_w0, rwkv_a0, rwkv_k_k, rwkv_k_a, rwkv_r_k, rwkv_ln_g, rwkv_ln_b,
                     jnp.zeros_like(rwkv_w0)], axis=1)
    conv_p = jnp.pad(conv_w, ((0, 0), (0, 8 - CONV_WIDTH), (0, 0)))
    w_branch_b = w_branch.astype(BF16)
    w_mix_out_b = w_mix_out.astype(BF16)
    xa_wq_b = xa_wq.astype(BF16)
    xa_wkv_b = xa_wkv.astype(BF16)
    xa_wo_b = xa_wo.astype(BF16)
    mlp_w1_b = mlp_w1.astype(BF16)
    mlp_w2_b = mlp_w2.astype(BF16)

    xf = x.reshape(B * S, D)
    memf = mem.reshape(B * M, D)
    for l in range(depth):
        z = norm_matmul(xf, row(ln_mix_pre), w_in_b, l, tm=512, tn=Z_COLS, out_dtype=F32)
        prep = nsa_compress(z, cmp_lo, cmp_hi, pe_lo, pe_hi, l, B, S)
        z3 = z.reshape(B, S, Z_COLS)
        o_nsa = nsa_attention(z3, prep, *nsa_tabs, B, S, G=8).reshape(B * S, MIX_WIDTH)
        o_ret = retention(z3, ret_tabs, row(ret_norm_g), l, B, S, G=8).reshape(B * S, MIX_WIDTH)
        o_rwkv = rwkv7(z3, mu, lora_w, vec, rwkv_consts, l, B, S, G=8).reshape(B * S, MIX_WIDTH)
        xf = merge_mix(xf, z, o_nsa, o_ret, o_rwkv, conv_p, row(ln_mix_pre), w_gate_b, w_branch_b,
                       w_mix_out_b, row(ln_mix_post), l, B, S, tm=512)
        kvm = norm_matmul(memf, row(ln_mem), xa_wkv_b, l, tm=2 * M, tn=2 * D_MODEL, out_dtype=BF16)
        xf = cross_attention(xf, kvm, row(ln_xa_pre), xa_wq_b, xa_wo_b, row(ln_xa_post), l, B, S, M, tm=1024)
        xf = mlp(xf, row(ln_mlp_pre), mlp_w1_b, mlp_w2_b, row(ln_mlp_post), l, tm=512)
    return xf.reshape(B, S, D)
```

```python
import functools
import math

import numpy as np
import jax
import jax.numpy as jnp
from jax import lax
from jax.experimental import pallas as pl
from jax.experimental.pallas import tpu as pltpu

F32 = jnp.float32
BF16 = jnp.bfloat16

D_MODEL = 1024
N_BRANCHES = 4
MIX_WIDTH = 256
HEAD_DIM = 64
MIX_HEADS = 4

NSA_CMP_LEN = 32
NSA_CMP_STRIDE = 16
NSA_SLC_LEN = 64
NSA_TOP_N = 8
NSA_WINDOW = 512
Q_BLOCK = 128
NSA_V_ROWS = HEAD_DIM + 16
FORCE_BONUS = 1e4
N_BUCKETS = 32
MAX_DISTANCE = 128

RET_CHUNK = 128
ROPE_BASE = 10000.0
RET_NORM_EPS = 1e-5

RWKV_DECAY_LORA = 32
RWKV_AAA_LORA = 32
RWKV_GATE_LORA = 64
RWKV_GN_EPS = 64e-5
RWKV_COLS = 3 * MIX_WIDTH + RWKV_DECAY_LORA + RWKV_AAA_LORA + RWKV_GATE_LORA
RWKV_CHUNK = 64

CONV_WIDTH = 3
XA_HEADS = 4
XA_HEAD_DIM = D_MODEL // XA_HEADS
D_FF = 4 * D_MODEL

RMS_EPS = 1e-6
LOG2E = math.log2(math.e)
NEG_INF = -1e30
NEG_BIG = -3e38

OFF_CONV = 0
OFF_NSAQ = 768
OFF_RET = 1024
OFF_RWKV = 2048
OFF_NSAKV = 3072
OFF_NSAG = 3456
Z_COLS = 3584

VMEM_LIMIT = 56 * 1024 * 1024


def _cparams(sem):
    return pltpu.CompilerParams(dimension_semantics=sem, vmem_limit_bytes=VMEM_LIMIT)


def _dot(a, b):
    return jnp.dot(a, b, preferred_element_type=F32)


def _dot_nt(a, b):
    return lax.dot_general(a, b, (((1,), (1,)), ((), ())), preferred_element_type=F32)


def _dot_tn(a, b):
    return lax.dot_general(a, b, (((0,), (0,)), ((), ())), preferred_element_type=F32)


def _split2(x):
    hi = x.astype(BF16)
    lo = (x - hi.astype(F32)).astype(BF16)
    return hi, lo


def _dot_x2(x, w_bf16, *more):
    xs = (x,) + more
    rows = x.shape[0]
    parts = [p for xi in xs for p in _split2(xi)]
    out = _dot(jnp.concatenate(parts, axis=0), w_bf16)
    res = [out[2 * i * rows:(2 * i + 1) * rows] + out[(2 * i + 1) * rows:(2 * i + 2) * rows]
           for i in range(len(xs))]
    return res[0] if not more else res


def _rms(x, g):
    ms = jnp.mean(x * x, axis=-1, keepdims=True)
    return x * lax.rsqrt(ms + RMS_EPS) * g


def _norm_matmul_kernel(x_ref, g_ref, w_ref, o_ref, h_ref):
    @pl.when(pl.program_id(1) == 0)
    def _():
        h_ref[...] = _rms(x_ref[...], g_ref[...]).astype(BF16)

    o_ref[...] = _dot(h_ref[...], w_ref[...]).astype(o_ref.dtype)


def norm_matmul(x, g, w, layer, tm, tn, out_dtype):
    M, D = x.shape
    N = w.shape[2]
    w_mode = dict(pipeline_mode=pl.Buffered(1)) if tn == N else {}
    return pl.pallas_call(
        _norm_matmul_kernel,
        grid=(M // tm, N // tn),
        in_specs=[
            pl.BlockSpec((tm, D), lambda i, j: (i, 0)),
            pl.BlockSpec((None, 1, D), lambda i, j: (layer, 0, 0)),
            pl.BlockSpec((None, D, tn), lambda i, j: (layer, 0, j), **w_mode),
        ],
        out_specs=pl.BlockSpec((tm, tn), lambda i, j: (i, j)),
        out_shape=jax.ShapeDtypeStruct((M, N), out_dtype),
        scratch_shapes=[pltpu.VMEM((tm, D), BF16)],
        compiler_params=_cparams(("parallel", "arbitrary")),
        name="norm_matmul",
    )(x, g, w)


def _nsa_compress_kernel(kvc_ref, kvs_ref, kvw_ref, wlo_ref, whi_ref, pelo_ref, pehi_ref,
                         kc_ref, vct_ref, ks_ref, vst_ref, kw_ref, vwt_ref):
    nblk = kc_ref.shape[0]
    n_kt = vwt_ref.shape[0]
    ylo = jnp.zeros((nblk, 128), F32)
    yhi = jnp.zeros((nblk, 128), F32)
    for r in range(NSA_CMP_STRIDE):
        xr = kvc_ref[pl.ds(r, nblk, stride=NSA_CMP_STRIDE), :]
        ylo = ylo + _dot((xr + pelo_ref[r]).astype(BF16), wlo_ref[r])
        yhi = yhi + _dot((xr + pehi_ref[r]).astype(BF16), whi_ref[r])
    y = ylo + pltpu.roll(yhi, nblk - 1, axis=0)
    kc_ref[...] = y.astype(BF16)
    vct_ref[...] = y.T[HEAD_DIM:, :].astype(BF16)
    kw_ref[...] = kvw_ref[...].astype(BF16)
    ones = jnp.ones((NSA_V_ROWS - HEAD_DIM, 128), BF16)
    for kt in range(n_kt):
        rs = slice(kt * 128, (kt + 1) * 128)
        vst_ref[kt] = jnp.concatenate([kvs_ref[rs, :].T[HEAD_DIM:, :].astype(BF16), ones], axis=0)
        vwt_ref[kt] = jnp.concatenate([kvw_ref[rs, :].T[HEAD_DIM:, :].astype(BF16), ones], axis=0)
    S = kvs_ref.shape[0]
    key = lax.broadcasted_iota(jnp.int32, (S, 128), 0)
    lane = lax.broadcasted_iota(jnp.int32, (S, 128), 1)
    onehot = (lane == HEAD_DIM + key // NSA_SLC_LEN).astype(F32)
    ks_ref[0:S, :] = jnp.where(lane < HEAD_DIM, kvs_ref[...], onehot).astype(BF16)
    pad_lane = lax.broadcasted_iota(jnp.int32, (128, 128), 1)
    ks_ref[S:S + 128, :] = (pad_lane == HEAD_DIM + S // NSA_SLC_LEN).astype(BF16)
    vst_ref[n_kt] = jnp.zeros((NSA_V_ROWS, 128), BF16)


def nsa_compress(z, wlo, whi, pelo, pehi, layer, B, S):
    nblk = S // NSA_CMP_STRIDE
    n_kt = S // 128
    c0 = OFF_NSAKV // 128
    wmap = lambda b: (layer, 0, 0, 0)
    b3 = lambda b: (b, 0, 0)
    b4 = lambda b: (b, 0, 0, 0)
    return pl.pallas_call(
        _nsa_compress_kernel,
        grid=(B,),
        in_specs=[
            pl.BlockSpec((S, 128), lambda b: (b, c0)),
            pl.BlockSpec((S, 128), lambda b: (b, c0 + 1)),
            pl.BlockSpec((S, 128), lambda b: (b, c0 + 2)),
            pl.BlockSpec((None, NSA_CMP_STRIDE, 128, 128), wmap),
            pl.BlockSpec((None, NSA_CMP_STRIDE, 128, 128), wmap),
            pl.BlockSpec((None, NSA_CMP_STRIDE, 1, 128), wmap),
            pl.BlockSpec((None, NSA_CMP_STRIDE, 1, 128), wmap),
        ],
        out_specs=[
            pl.BlockSpec((None, nblk, 128), b3),
            pl.BlockSpec((None, HEAD_DIM, nblk), b3),
            pl.BlockSpec((None, S + 128, 128), b3),
            pl.BlockSpec((None, n_kt + 1, NSA_V_ROWS, 128), b4),
            pl.BlockSpec((None, S, 128), b3),
            pl.BlockSpec((None, n_kt, NSA_V_ROWS, 128), b4),
        ],
        out_shape=[
            jax.ShapeDtypeStruct((B, nblk, 128), BF16),
            jax.ShapeDtypeStruct((B, HEAD_DIM, nblk), BF16),
            jax.ShapeDtypeStruct((B, S + 128, 128), BF16),
            jax.ShapeDtypeStruct((B, n_kt + 1, NSA_V_ROWS, 128), BF16),
            jax.ShapeDtypeStruct((B, S, 128), BF16),
            jax.ShapeDtypeStruct((B, n_kt, NSA_V_ROWS, 128), BF16),
        ],
        compiler_params=_cparams(("parallel",)),
        name="nsa_compress",
    )(z, z, z, wlo, whi, pelo, pehi)


def _rowmax(s):
    return jnp.max(s, axis=-1, keepdims=True)


def _rowsum(s):
    return jnp.sum(s, axis=-1, keepdims=True)


def _colmax(s):
    return jnp.max(s, axis=0, keepdims=True)


def _colsum(s):
    return jnp.sum(s, axis=0, keepdims=True)


def _nsa_kernel(q_ref, gl_ref, kc_ref, vct_ref, ks_ref, vst_ref, kw_ref, vwt_ref, bc_ref, tab_ref,
                ovt_ref, o_ref):
    bi = pl.program_id(1)
    G = q_ref.shape[0]
    n_kt = vwt_ref.shape[1]
    n_blk = ovt_ref.shape[0]
    n_cmp = kc_ref.shape[1]
    QB = Q_BLOCK
    R = MIX_HEADS * QB
    n_off = NSA_WINDOW // 128

    lo = jnp.maximum(bi - n_off, 0)
    q4 = [None] * G
    q4_sel = [None] * G
    o_c = [None] * G
    carry_s = [None] * G
    carry_w = [None] * G

    def key_tile(kref, g, kt, lanes):
        return kref[g, pl.ds(pl.multiple_of(kt * 128, 128), 128), lanes]

    def pair_qk(g, k0, k1):
        keys = jnp.concatenate([key_tile(kw_ref, g, k0, slice(0, HEAD_DIM)),
                                key_tile(kw_ref, g, k1, slice(0, HEAD_DIM))], axis=0)
        qk = _dot(keys, q4[g])
        return qk[:128], qk[128:]

    def pair_qk_sel(g, k0, k1):
        keys = jnp.concatenate([key_tile(ks_ref, g, k0, slice(None)), key_tile(ks_ref, g, k1, slice(None))],
                               axis=0)
        qk = _dot(keys, q4_sel[g])
        return qk[:128], qk[128:]

    def head_cols(x, h):
        return x[:, h * QB:(h + 1) * QB]

    key1 = lax.broadcasted_iota(jnp.int32, (128, QB), 0)
    i1 = lax.broadcasted_iota(jnp.int32, (128, QB), 1)
    causal1 = key1 <= i1
    kp = jnp.maximum(bi - 1, 0)
    pen_prev = jnp.where(bi >= 1, 0.0, NEG_INF)

    def win_add(kt, valid):
        shift = jnp.where(valid, jnp.where(bi - kt < n_off, -4096, 0), 4096)
        return jnp.where(key1 > i1 + shift, 0.0, NEG_INF)

    def diag_add(extra):
        return [jnp.where(causal1, head_cols(tab_ref[0], h) + extra, NEG_INF) for h in range(MIX_HEADS)]

    def prev_add(extra):
        return [head_cols(tab_ref[1], h) + extra for h in range(MIX_HEADS)]

    def pair_update(carry, qk0, add0, vt0, qk1, add1, vt1):
        p0s, p1s, ms, alphas = [], [], [], []
        for h in range(MIX_HEADS):
            a0 = head_cols(qk0, h) if add0 is None else head_cols(qk0, h) + add0[h]
            a1 = head_cols(qk1, h) if add1 is None else head_cols(qk1, h) + add1[h]
            m_new = jnp.maximum(_colmax(a0), _colmax(a1))
            if carry is not None:
                m_old = head_cols(carry[0], h)
                m_new = jnp.maximum(m_old, m_new)
                alphas.append(jnp.exp2(m_old - m_new))
            ms.append(m_new)
            p0s.append(jnp.exp2(a0 - m_new).astype(BF16))
            p1s.append(jnp.exp2(a1 - m_new).astype(BF16))
        yield
        probs = jnp.concatenate([jnp.concatenate(p0s, axis=1), jnp.concatenate(p1s, axis=1)], axis=0)
        acc = _dot(jnp.concatenate([vt0, vt1], axis=1), probs)
        if carry is not None:
            acc = jnp.concatenate(alphas, axis=1) * carry[1] + acc
        return jnp.concatenate(ms, axis=1), acc

    def head_part(g):
        qt = (q_ref[g] * (HEAD_DIM ** -0.5 * LOG2E)).T
        q4[g] = jnp.concatenate([qt[h * HEAD_DIM:(h + 1) * HEAD_DIM] for h in range(MIX_HEADS)],
                                axis=1).astype(BF16)
        yield
        qk_d, qk_p = pair_qk(g, bi, kp)
        carry_w[g] = yield from pair_update(None, qk_d, diag_add(0.0), vwt_ref[g, bi],
                                            qk_p, prev_add(pen_prev), vwt_ref[g, kp])
        yield
        g0 = pl.multiple_of((n_kt - 1 - bi) * 8, 8)
        qk = _dot(kc_ref[g, :, :HEAD_DIM], q4[g])
        nrow = lax.broadcasted_iota(jnp.int32, (n_cmp, QB), 0)
        tq = bi * QB + lax.broadcasted_iota(jnp.int32, (n_cmp, QB), 1)
        vis = tq - (nrow * NSA_CMP_STRIDE + (NSA_CMP_LEN - 1)) >= 0
        live = (tq >= NSA_CMP_LEN - 1).astype(F32)
        yield
        p_cs = []
        for h in range(MIX_HEADS):
            s = jnp.where(vis, head_cols(qk, h) + bc_ref[pl.ds(g0, n_cmp), h * QB:(h + 1) * QB], NEG_INF)
            e = jnp.exp2(s - _colmax(s))
            p_cs.append(e / _colsum(e) * live)
        yield
        o_c[g] = _dot(vct_ref[g], jnp.concatenate(p_cs, axis=1).astype(BF16))
        p_sum = p_cs[0] + p_cs[1] + p_cs[2] + p_cs[3]
        p_hi, p_lo = _split2(p_sum)
        imp = _dot(ovt_ref[...], p_hi) + _dot(ovt_ref[...], p_lo)
        yield
        blk = lax.broadcasted_iota(jnp.int32, (n_blk, QB), 0)
        cur = (bi * QB + lax.broadcasted_iota(jnp.int32, (n_blk, QB), 1)) >> 6
        forced = (blk == 0) | (blk == cur) | (blk == cur - 1)
        imp = jnp.where(forced, imp + FORCE_BONUS, imp)
        imp = jnp.where(blk <= cur, imp, NEG_INF)
        blk_f = blk.astype(F32)
        sel = jnp.zeros((n_blk, QB), F32)
        for _ in range(NSA_TOP_N):
            mx = _colmax(imp)
            idx = jnp.min(jnp.where(imp == mx, blk_f, 4096.0), axis=0, keepdims=True)
            pick = blk_f == idx
            sel = jnp.where(pick, 1.0, sel)
            imp = jnp.where(pick, NEG_BIG, imp)
            yield
        sel_add = (sel - 1.0) * (-NEG_INF)
        rows_left = 128 - HEAD_DIM - n_blk
        pad_rows = jnp.where(lax.broadcasted_iota(jnp.int32, (rows_left, QB), 0) == 0, NEG_INF, 0.0)
        mask_rows = jnp.concatenate([sel_add, pad_rows], axis=0).astype(BF16)
        q4_sel[g] = jnp.concatenate([q4[g], jnp.concatenate([mask_rows] * MIX_HEADS, axis=1)], axis=0)
        yield
        qk_d, qk_p = pair_qk_sel(g, bi, kp)
        carry_s[g] = yield from pair_update(None, qk_d, diag_add(0.0), vst_ref[g, bi],
                                            qk_p, prev_add(pen_prev), vst_ref[g, kp])

    _round_robin([head_part(g) for g in range(G)])

    n_old = jnp.maximum(bi - 1, 0)

    def sel_step(g, j, carry, out):
        k0 = 2 * j
        k1 = jnp.where(k0 + 1 < n_old, k0 + 1, n_kt)
        qk0, qk1 = pair_qk_sel(g, k0, k1)
        yield
        out[g] = yield from pair_update(carry, qk0, None, vst_ref[g, k0], qk1, None, vst_ref[g, k1])

    def win_step(g, j, carry, out):
        k0 = lo + 2 * j
        k1 = jnp.minimum(k0 + 1, bi)
        qk0, qk1 = pair_qk(g, k0, k1)
        yield
        out[g] = yield from pair_update(carry, qk0, [win_add(k0, True)] * MIX_HEADS, vwt_ref[g, k0],
                                        qk1, [win_add(k1, k0 + 1 < n_old)] * MIX_HEADS, vwt_ref[g, k1])

    def joint(step):
        def body(j, carries):
            out = [None] * G
            _round_robin([step(g, j, carries[g], out) for g in range(G)])
            return tuple(out)
        return body

    res_s = lax.fori_loop(0, (n_old + 1) // 2, joint(sel_step), tuple(carry_s))
    res_w = lax.fori_loop(0, (n_old - lo + 1) // 2, joint(win_step), tuple(carry_w))

    def tail_part(g):
        acc_s = res_s[g][1]
        acc_w = res_w[g][1]
        o_s = acc_s[:HEAD_DIM] * (1.0 / acc_s[HEAD_DIM:HEAD_DIM + 1])
        o_w = acc_w[:HEAD_DIM] * (1.0 / acc_w[HEAD_DIM:HEAD_DIM + 1])
        gate = jax.nn.sigmoid(gl_ref[g].T)
        yield
        outs = []
        for h in range(MIX_HEADS):
            cs = slice(h * QB, (h + 1) * QB)
            outs.append(gate[3 * h:3 * h + 1] * o_c[g][:, cs] + gate[3 * h + 1:3 * h + 2] * o_s[:, cs]
                        + gate[3 * h + 2:3 * h + 3] * o_w[:, cs])
        o_ref[g] = jnp.concatenate(outs, axis=0).T

    _round_robin([tail_part(g) for g in range(G)])


def nsa_attention(z, prep, bias_c, tab, ovt, B, S, G):
    n_qb = S // Q_BLOCK
    n_kt = S // 128
    n_cmp = S // NSA_CMP_STRIDE
    R = MIX_HEADS * Q_BLOCK
    kc, vct, ks, vst, kw, vwt = prep
    b3 = lambda b, i: (b, 0, 0)
    b4 = lambda b, i: (b, 0, 0, 0)
    return pl.pallas_call(
        _nsa_kernel,
        grid=(B // G, n_qb),
        in_specs=[
            pl.BlockSpec((G, Q_BLOCK, MIX_WIDTH), lambda b, i: (b, i, OFF_NSAQ // MIX_WIDTH)),
            pl.BlockSpec((G, Q_BLOCK, 128), lambda b, i: (b, i, OFF_NSAG // 128)),
            pl.BlockSpec((G, n_cmp, 128), b3),
            pl.BlockSpec((G, HEAD_DIM, n_cmp), b3),
            pl.BlockSpec((G, S + 128, 128), b3),
            pl.BlockSpec((G, n_kt + 1, NSA_V_ROWS, 128), b4),
            pl.BlockSpec((G, S, 128), b3),
            pl.BlockSpec((G, n_kt, NSA_V_ROWS, 128), b4),
            pl.BlockSpec(bias_c.shape, lambda b, i: (0, 0)),
            pl.BlockSpec((2, 128, R), lambda b, i: (0, 0, 0)),
            pl.BlockSpec(ovt.shape, lambda b, i: (0, 0)),
        ],
        out_specs=pl.BlockSpec((G, Q_BLOCK, MIX_WIDTH), lambda b, i: (b, i, 0)),
        out_shape=jax.ShapeDtypeStruct((B, S, MIX_WIDTH), F32),
        compiler_params=_cparams(("parallel", "arbitrary")),
        name="nsa_attention",
    )(z, z, kc, vct, ks, vst, kw, vwt, bias_c, tab, ovt)


def _stack_heads(x, head_of_lane):
    return jnp.concatenate([jnp.where(head_of_lane == h, x, jnp.zeros_like(x))
                            for h in range(MIX_HEADS)], axis=0)


def _head_norm(y, seg_mean, eps):
    mu = _dot_x2(y, seg_mean)
    d = y - mu
    var = _dot_x2(d * d, seg_mean)
    return d * lax.rsqrt(var + eps)


def _round_robin(chains):
    while chains:
        chains = [c for c in chains if next(c, StopIteration) is not StopIteration]


def _retention_kernel(z_ref, cos_ref, sin_ref, indec_ref, qdec_ref, kdec_ref, cdec_ref, bd_ref,
                      segm_ref, ng_ref, o_ref, st_ref):
    @pl.when(pl.program_id(1) == 0)
    def _():
        st_ref[...] = jnp.zeros_like(st_ref)

    _round_robin([_retention_chunk(z_ref.at[g], cos_ref, sin_ref, indec_ref, qdec_ref, kdec_ref, cdec_ref,
                                   bd_ref, segm_ref, ng_ref, o_ref.at[g], st_ref.at[g])
                  for g in range(z_ref.shape[0])])


def _retention_chunk(z_ref, cos_ref, sin_ref, indec_ref, qdec_ref, kdec_ref, cdec_ref, bd_ref,
                     segm_ref, ng_ref, o_ref, st_ref):
    L = RET_CHUNK
    z = z_ref[...]
    q = z[:, 0:256]
    k = z[:, 256:512]
    v = z[:, 512:768]
    g = z[:, 768:1024]
    cos = cos_ref[...]
    sin = sin_ref[...]

    def rot(u):
        u1 = u[:, :128]
        u2 = u[:, 128:]
        return jnp.concatenate([u1 * cos - u2 * sin, u2 * cos + u1 * sin], axis=1)

    qr = rot(q) * (HEAD_DIM ** -0.5)
    kr = rot(k)
    lane = lax.broadcasted_iota(jnp.int32, (L, 256), 1)
    head_qk = (lane & 127) >> 5
    head_v = lane >> 6
    qb = qr.astype(BF16)
    kb = kr.astype(BF16)
    vb = v.astype(BF16)
    yield

    att = _dot_nt(_stack_heads(qb, head_qk), kb) * indec_ref[...]
    yield
    o_st = _dot(att.astype(BF16), vb)
    o = jnp.zeros((L, 256), F32)
    for h in range(MIX_HEADS):
        o = o + jnp.where(head_v == h, o_st[h * L:(h + 1) * L], 0.0)
    yield
    state = st_ref[...]
    o = o + _dot(qb, state.astype(BF16)) * qdec_ref[...]
    st_ref[...] = state * cdec_ref[...] + _dot_tn((kr * kdec_ref[...]).astype(BF16), vb) * bd_ref[...]
    yield

    mu = _dot_x2(o, segm_ref[...])
    yield
    d = o - mu
    var = _dot_x2(d * d, segm_ref[...])
    yield
    o_ref[...] = g * jax.nn.sigmoid(g) * (d * lax.rsqrt(var + RET_NORM_EPS) * ng_ref[...])


def retention(z, tabs, ng, layer, B, S, G):
    L = RET_CHUNK
    nc = S // L
    cos, sin, indec, qdec, kdec, cdec, bd, segm = tabs
    const2 = lambda b, c: (0, 0)
    return pl.pallas_call(
        _retention_kernel,
        grid=(B // G, nc),
        in_specs=[
            pl.BlockSpec((G, L, 1024), lambda b, c: (b, c, OFF_RET // 1024)),
            pl.BlockSpec((L, 128), lambda b, c: (c, 0)),
            pl.BlockSpec((L, 128), lambda b, c: (c, 0)),
            pl.BlockSpec((MIX_HEADS * L, L), const2),
            pl.BlockSpec((L, 256), const2),
            pl.BlockSpec((L, 256), const2),
            pl.BlockSpec((1, 256), const2),
            pl.BlockSpec((256, 256), const2),
            pl.BlockSpec((256, 256), const2),
            pl.BlockSpec((None, 1, 256), lambda b, c: (layer, 0, 0)),
        ],
        out_specs=pl.BlockSpec((G, L, 256), lambda b, c: (b, c, 0)),
        out_shape=jax.ShapeDtypeStruct((B, S, 256), F32),
        scratch_shapes=[pltpu.VMEM((G, 256, 256), F32)],
        compiler_params=_cparams(("parallel", "arbitrary")),
        name="retention",
    )(z, cos, sin, indec, qdec, kdec, cdec, bd, segm, ng)


def _rwkv_kernel(z_ref, mu_ref, lw_ref, vec_ref, tri_ref, bdl_ref, wide_ref,
                 segs_ref, segm_ref, o_ref, st_ref, prev_ref):
    @pl.when(pl.program_id(1) == 0)
    def _():
        st_ref[...] = jnp.zeros_like(st_ref)
        prev_ref[...] = jnp.zeros_like(prev_ref)

    _round_robin([_rwkv_chunk(z_ref.at[g], mu_ref, lw_ref, vec_ref, tri_ref, bdl_ref,
                              wide_ref, segs_ref, segm_ref, o_ref.at[g], st_ref.at[g], prev_ref.at[g])
                  for g in range(z_ref.shape[0])])


def _rwkv_chunk(z_ref, mu_ref, lw_ref, vec_ref, tri_ref, bdl_ref, wide_ref,
                segs_ref, segm_ref, o_ref, st_ref, prev_ref):
    C = RWKV_CHUNK
    z = z_ref[...]
    rows = lax.broadcasted_iota(jnp.int32, (C, 1024), 0)
    zs = jnp.where(rows == 0, prev_ref[0:1, :], pltpu.roll(z, 1, axis=0))
    prev_ref[...] = jnp.broadcast_to(z[C - 1:C, :], prev_ref.shape)
    zf = z + (zs - z) * mu_ref[...]
    r = zf[:, 0:256]
    k = zf[:, 256:512]
    v = zf[:, 512:768]
    lora = zf[:, 768:896]

    vec = vec_ref[...]
    w0, a0, k_k, k_a, r_k, ln_g, ln_b = (vec[i:i + 1] for i in range(7))

    lane = lax.broadcasted_iota(jnp.int32, lora.shape, 1)
    lora_in = jnp.where(lane < RWKV_DECAY_LORA, jnp.tanh(lora),
                        jnp.where(lane < RWKV_DECAY_LORA + RWKV_AAA_LORA, lora, jax.nn.sigmoid(lora)))
    hi, lo = _split2(lora_in)
    lhs = jnp.concatenate([jnp.concatenate([hi, hi], axis=1),
                           jnp.concatenate([lo, jnp.zeros_like(lo)], axis=1)], axis=0)
    both = _dot(lhs, lw_ref[...])
    lora_out = both[:C] + both[C:]
    wpre = w0 + lora_out[:, 0:256]
    y = -wpre
    softplus = jnp.maximum(y, 0.0) + jnp.log(1.0 + jnp.exp(-jnp.abs(y)))
    w_log = -softplus - 0.5
    ld = -jnp.exp(w_log)
    a = jax.nn.sigmoid(a0 + lora_out[:, 256:512])
    gate = lora_out[:, 512:768]
    kk = k * k_k
    k2 = k * (1.0 + (a - 1.0) * k_a)
    kk_sq, bonus = _dot_x2(kk * kk, segs_ref[...], r * k2 * r_k)
    kk = kk / jnp.maximum(jnp.sqrt(kk_sq), 1e-12)
    yield

    tri = tri_ref[...]
    l1 = ld.astype(BF16)
    r1 = ld - l1.astype(F32)
    l2 = r1.astype(BF16)
    l3 = (r1 - l2.astype(F32)).astype(BF16)
    cs = _dot(jnp.concatenate([tri, tri, tri], axis=1), jnp.concatenate([l1, l2, l3], axis=0))
    cs_end = cs[C - 1:C, :]
    yield
    e_neg = jnp.exp(-cs)
    e_end = jnp.exp(cs_end - cs)
    kka = kk * a
    a_t = -kk * jnp.exp(cs - ld)
    r_t = r * jnp.exp(cs)
    b_t = kka * e_neg
    k_t = k2 * e_neg
    b_g = kka * e_end
    k_g = k2 * e_end

    lane = lax.broadcasted_iota(jnp.int32, (C, 256), 1)
    head = lane >> 6
    stack = lambda x: _stack_heads(x.astype(BF16), head)
    a_st, b_st, k_st, v_st = stack(a_t), stack(b_t), stack(k_t), stack(v)

    n_pow = _dot_nt(a_st, b_st) * bdl_ref[...]
    ri = lax.broadcasted_iota(jnp.int32, (4 * C, 4 * C), 0)
    ci = lax.broadcasted_iota(jnp.int32, (4 * C, 4 * C), 1)
    t_inv = jnp.where(ri == ci, 1.0, 0.0) + n_pow
    yield

    ar = jnp.concatenate([a_t, r_t], axis=0).astype(BF16)
    bk = jnp.concatenate([b_st, k_st], axis=0)
    wide = _dot_nt(ar, bk)
    strict = wide_ref[0]
    incl = wide_ref[1]
    a_ak = (wide[0:C, 4 * C:] * strict).astype(BF16)
    a_rb = (wide[C:, 0:4 * C] * incl).astype(BF16)
    a_rk = (wide[C:, 4 * C:] * incl).astype(BF16)
    yield

    sq = 2
    while sq < C:
        nb = n_pow.astype(BF16)
        n_pow = _dot(nb, nb)
        yield
        t_inv = t_inv + _dot(t_inv.astype(BF16), n_pow.astype(BF16))
        sq *= 2
    yield

    state = st_ref[...]
    sb = state.astype(BF16)
    from_state = _dot_nt(ar, sb)
    x = from_state[0:C] + _dot(a_ak, v_st)
    yield
    u_st = _dot(t_inv.astype(BF16), stack(x)).astype(BF16)
    yield
    uv = jnp.concatenate([u_st, v_st], axis=0)
    yv = from_state[C:] + _dot(jnp.concatenate([a_rb, a_rk], axis=1), uv)
    bkg = jnp.concatenate([stack(b_g), stack(k_g)], axis=0)
    st_ref[...] = state * jnp.exp(cs_end) + _dot_tn(uv, bkg)
    yield

    yn = _head_norm(yv, segm_ref[...], RWKV_GN_EPS) * ln_g + ln_b
    yn = yn + bonus * v
    o_ref[...] = yn * gate


def rwkv7(z, mu, lora_w, vec, consts, layer, B, S, G):
    C = RWKV_CHUNK
    nc = S // C
    tri, bdl, wide, segs, segm = consts
    const2 = lambda b, c: (0, 0)
    return pl.pallas_call(
        _rwkv_kernel,
        grid=(B // G, nc),
        in_specs=[
            pl.BlockSpec((G, C, 1024), lambda b, c: (b, c, OFF_RWKV // 1024)),
            pl.BlockSpec((None, 1, 1024), lambda b, c: (layer, 0, 0)),
            pl.BlockSpec((None, 256, 768), lambda b, c: (layer, 0, 0)),
            pl.BlockSpec((None, 8, 256), lambda b, c: (layer, 0, 0)),
            pl.BlockSpec((C, C), const2),
            pl.BlockSpec((4 * C, 4 * C), const2),
            pl.BlockSpec((2, C, 4 * C), lambda b, c: (0, 0, 0)),
            pl.BlockSpec((256, 256), const2),
            pl.BlockSpec((256, 256), const2),
        ],
        out_specs=pl.BlockSpec((G, C, 256), lambda b, c: (b, c, 0)),
        out_shape=jax.ShapeDtypeStruct((B, S, 256), F32),
        scratch_shapes=[pltpu.VMEM((G, 256, 256), F32), pltpu.VMEM((G, 8, 1024), F32)],
        compiler_params=_cparams(("parallel", "arbitrary")),
        name="rwkv7",
    )(z, mu, lora_w, vec, tri, bdl, wide, segs, segm)


def _merge_kernel(x_ref, zc_ref, on_ref, or_ref, ow_ref, cw_ref, gpre_ref, wg_ref, wb_ref, wo_ref, g_ref,
                  o_ref, carry_ref):
    tm = x_ref.shape[0]
    x = x_ref[...]
    h = _rms(x, gpre_ref[...]).astype(BF16)

    @pl.when(pl.program_id(1) == 0)
    def _():
        carry_ref[...] = jnp.zeros_like(carry_ref)

    zc = zc_ref[...]
    b_g = zc[:, 0:256]
    u = zc[:, 256:512] * zc[:, 512:768]
    prev = carry_ref[...]
    rows = lax.broadcasted_iota(jnp.int32, (tm, 256), 0)
    u1 = jnp.where(rows == 0, prev[7:8], pltpu.roll(u, 1, axis=0))
    u2 = jnp.where(rows == 0, prev[6:7], jnp.where(rows == 1, prev[7:8], pltpu.roll(u, 2, axis=0)))
    carry_ref[...] = u[tm - 8:tm]
    cw = cw_ref[...]
    o_conv = b_g * (cw[0:1] * u2 + cw[1:2] * u1 + cw[2:3] * u)

    branches = (on_ref[...], or_ref[...], ow_ref[...], o_conv)
    merged = jnp.zeros((tm, D_MODEL), F32)
    for m in range(N_BRANCHES):
        gm = jax.nn.sigmoid(_dot(h, wg_ref[:, m * D_MODEL:(m + 1) * D_MODEL]))
        merged = merged + gm * _dot(branches[m].astype(BF16), wb_ref[m])
    y = _dot(merged.astype(BF16), wo_ref[...])
    o_ref[...] = x + _rms(y, g_ref[...])


def merge_mix(x, z, o_nsa, o_ret, o_rwkv, conv_w, g_pre, w_gate, w_branch, w_out, g_post, layer, B, S, tm):
    nt = S // tm
    rowmap = lambda b, i: (b * nt + i, 0)
    lmap = lambda b, i: (layer, 0, 0)
    return pl.pallas_call(
        _merge_kernel,
        grid=(B, nt),
        in_specs=[
            pl.BlockSpec((tm, D_MODEL), rowmap),
            pl.BlockSpec((tm, 768), lambda b, i: (b * nt + i, OFF_CONV // 768)),
            pl.BlockSpec((tm, 256), rowmap),
            pl.BlockSpec((tm, 256), rowmap),
            pl.BlockSpec((tm, 256), rowmap),
            pl.BlockSpec((None, 8, 256), lmap),
            pl.BlockSpec((None, 1, D_MODEL), lmap),
            pl.BlockSpec((None, D_MODEL, N_BRANCHES * D_MODEL), lmap, pipeline_mode=pl.Buffered(1)),
            pl.BlockSpec((None, N_BRANCHES, MIX_WIDTH, D_MODEL), lambda b, i: (layer, 0, 0, 0),
                         pipeline_mode=pl.Buffered(1)),
            pl.BlockSpec((None, D_MODEL, D_MODEL), lmap, pipeline_mode=pl.Buffered(1)),
            pl.BlockSpec((None, 1, D_MODEL), lmap),
        ],
        out_specs=pl.BlockSpec((tm, D_MODEL), rowmap),
        out_shape=jax.ShapeDtypeStruct((B * S, D_MODEL), F32),
        scratch_shapes=[pltpu.VMEM((8, 256), F32)],
        compiler_params=_cparams(("parallel", "arbitrary")),
        name="merge_mix",
    )(x, z, o_nsa, o_ret, o_rwkv, conv_w, g_pre, w_gate, w_branch, w_out, g_post)


def _xattn_kernel(x_ref, kv_ref, gpre_ref, wq_ref, wo_ref, gpost_ref, o_ref):
    n_sub = 1
    rows = x_ref.shape[0] // n_sub

    def part(r0):
        rs = pl.ds(r0, rows)
        x = x_ref[rs, :]
        h = _rms(x, gpre_ref[...]).astype(BF16)
        yield
        q = (_dot(h, wq_ref[...]) * (XA_HEAD_DIM ** -0.5)).astype(BF16)
        yield
        outs = []
        for hd in range(XA_HEADS):
            cs = slice(hd * XA_HEAD_DIM, (hd + 1) * XA_HEAD_DIM)
            kh = kv_ref[:, hd * XA_HEAD_DIM:(hd + 1) * XA_HEAD_DIM]
            vh = kv_ref[:, D_MODEL + hd * XA_HEAD_DIM:D_MODEL + (hd + 1) * XA_HEAD_DIM]
            s = _dot_nt(q[:, cs], kh)
            yield
            e = jnp.exp(s - _rowmax(s))
            p = e / _rowsum(e)
            outs.append(_dot(p.astype(BF16), vh).astype(BF16))
            yield
        o = jnp.concatenate(outs, axis=1)
        y = _dot(o, wo_ref[...])
        yield
        o_ref[rs, :] = x + _rms(y, gpost_ref[...])

    _round_robin([part(i * rows) for i in range(n_sub)])


def cross_attention(x, kv, g_pre, wq, wo, g_post, layer, B, S, M, tm):
    nt = S // tm
    rowmap = lambda b, i: (b * nt + i, 0)
    lmap = lambda b, i: (layer, 0, 0)
    return pl.pallas_call(
        _xattn_kernel,
        grid=(B, nt),
        in_specs=[
            pl.BlockSpec((tm, D_MODEL), rowmap),
            pl.BlockSpec((M, 2 * D_MODEL), lambda b, i: (b, 0)),
            pl.BlockSpec((None, 1, D_MODEL), lmap),
            pl.BlockSpec((None, D_MODEL, D_MODEL), lmap),
            pl.BlockSpec((None, D_MODEL, D_MODEL), lmap),
            pl.BlockSpec((None, 1, D_MODEL), lmap),
        ],
        out_specs=pl.BlockSpec((tm, D_MODEL), rowmap),
        out_shape=jax.ShapeDtypeStruct((B * S, D_MODEL), F32),
        compiler_params=_cparams(("parallel", "arbitrary")),
        name="cross_attention",
    )(x, kv, g_pre, wq, wo, g_post)


def _mlp_kernel(x_ref, gpre_ref, w1_ref, w2_ref, gpost_ref, o_ref):
    x = x_ref[...]
    h = _rms(x, gpre_ref[...]).astype(BF16)
    tf = 1024
    acc = jnp.zeros(x.shape, F32)
    for f in range(D_FF // tf):
        a = jnp.maximum(_dot(h, w1_ref[:, f * tf:(f + 1) * tf]), 0.0)
        acc = acc + _dot((a * a).astype(BF16), w2_ref[f * tf:(f + 1) * tf, :])
    o_ref[...] = x + _rms(acc, gpost_ref[...])


def mlp(x, g_pre, w1, w2, g_post, layer, tm):
    M = x.shape[0]
    lmap = lambda i: (layer, 0, 0)
    return pl.pallas_call(
        _mlp_kernel,
        grid=(M // tm,),
        in_specs=[
            pl.BlockSpec((tm, D_MODEL), lambda i: (i, 0)),
            pl.BlockSpec((None, 1, D_MODEL), lmap),
            pl.BlockSpec((None, D_MODEL, D_FF), lmap, pipeline_mode=pl.Buffered(1)),
            pl.BlockSpec((None, D_FF, D_MODEL), lmap, pipeline_mode=pl.Buffered(1)),
            pl.BlockSpec((None, 1, D_MODEL), lmap),
        ],
        out_specs=pl.BlockSpec((tm, D_MODEL), lambda i: (i, 0)),
        out_shape=jax.ShapeDtypeStruct((M, D_MODEL), F32),
        compiler_params=_cparams(("parallel",)),
        name="mlp",
    )(x, g_pre, w1, w2, g_post)


def _t5_bucket(dist):
    n = jnp.maximum(dist, 0)
    max_exact = N_BUCKETS // 2
    nf = jnp.maximum(n, 1).astype(F32)
    large = max_exact + (jnp.log(nf / max_exact) / math.log(MAX_DISTANCE / max_exact)
                         * (N_BUCKETS - max_exact)).astype(jnp.int32)
    large = jnp.minimum(large, N_BUCKETS - 1)
    return jnp.where(n < max_exact, n, large)


def _nsa_tables(rel_bias, S):
    n_qb = S // Q_BLOCK
    n_cmp = S // NSA_CMP_STRIDE
    n_blk = S // NSA_SLC_LEN
    bias_f = rel_bias.astype(F32)
    i = jnp.arange(Q_BLOCK)
    per_qb = Q_BLOCK // NSA_CMP_STRIDE
    g = jnp.arange(n_cmp + per_qb * (n_qb - 1)) - per_qb * (n_qb - 1)
    d_c = i[None, :] - (g[:, None] * NSA_CMP_STRIDE + NSA_CMP_LEN - 1)
    def lookup(dist):
        hit = _t5_bucket(dist)[..., None, None] == jnp.arange(N_BUCKETS)[:, None]
        return jnp.sum(jnp.where(hit, bias_f, 0.0), axis=-2)

    bias_c = lookup(d_c).transpose(0, 2, 1).reshape(g.shape[0], MIX_HEADS * Q_BLOCK)
    d0 = i[:, None] - i[None, :]
    tab = jnp.stack([lookup(d0), lookup(d0 + Q_BLOCK)]) - bias_f[N_BUCKETS - 1]
    tab = tab.transpose(0, 2, 3, 1).reshape(2, Q_BLOCK, MIX_HEADS * Q_BLOCK)
    cs = np.arange(n_cmp)[None, :] * NSA_CMP_STRIDE
    ss = np.arange(n_blk)[:, None] * NSA_SLC_LEN
    ovt = np.clip(np.minimum(cs + NSA_CMP_LEN, ss + NSA_SLC_LEN) - np.maximum(cs, ss), 0, None)
    ovt = ovt.astype(np.float32) / NSA_CMP_LEN
    ovt[:, (S - NSA_CMP_LEN) // NSA_CMP_STRIDE + 1:] = 0.0
    return bias_c * LOG2E, tab * LOG2E, jnp.asarray(ovt, BF16)


def _ret_tables(S):
    L = RET_CHUNK
    H = MIX_HEADS
    half = HEAD_DIM // 2
    pos = jnp.arange(S)
    inv_freq = ROPE_BASE ** (-jnp.arange(half, dtype=F32) / half)
    ang = pos.astype(F32)[:, None] * inv_freq[None, :]
    cos = jnp.tile(jnp.cos(ang), (1, H))
    sin = jnp.tile(jnp.sin(ang), (1, H))
    lg = jnp.log(1.0 - 2.0 ** (-5.0 - jnp.arange(H, dtype=F32)))
    n = jnp.arange(L, dtype=F32)
    diff = n[:, None] - n[None, :]
    inner = jnp.where(diff >= 0, jnp.exp(jnp.maximum(diff, 0.0)[None] * lg[:, None, None]), 0.0)
    indec = inner.reshape(H * L, L)
    q_decay = jnp.exp((n + 1.0)[None, :] * lg[:, None])
    k_decay = jnp.exp((L - 1.0 - n)[None, :] * lg[:, None])
    chunk_decay = jnp.exp(L * lg)
    lane = np.arange(256)
    head_v = lane // HEAD_DIM
    head_qk = (lane % 128) // half
    qdec = q_decay.T[:, head_v]
    kdec = k_decay.T[:, head_qk]
    cdec = chunk_decay[head_v][None, :]
    bd = jnp.asarray((head_qk[:, None] == head_v[None, :]).astype(np.float32))
    segm = jnp.asarray((head_v[:, None] == head_v[None, :]).astype(np.float32) / HEAD_DIM, BF16)
    return cos, sin, indec, qdec, kdec, cdec, bd, segm


def _rwkv_consts():
    C = RWKV_CHUNK
    t = np.arange(C)
    tri = (t[:, None] >= t[None, :]).astype(np.float32)
    r = np.arange(4 * C)
    bdl = ((r[:, None] // C == r[None, :] // C) & (r[:, None] % C > r[None, :] % C)).astype(np.float32)
    strict = (t[:, None] > (r[None, :] % C)).astype(np.float32)
    incl = (t[:, None] >= (r[None, :] % C)).astype(np.float32)
    lane = np.arange(256) // HEAD_DIM
    seg = (lane[:, None] == lane[None, :]).astype(np.float32)
    return (jnp.asarray(tri, BF16), jnp.asarray(bdl), jnp.asarray(np.stack([strict, incl])),
            jnp.asarray(seg, BF16), jnp.asarray(seg / HEAD_DIM, BF16))


def _hi_lo(w):
    hi = w.astype(BF16)
    lo = (w - hi.astype(F32)).astype(BF16)
    return jnp.stack([hi, lo], axis=1)


def _pad_rows(w, top, total):
    return jnp.pad(w, ((0, 0), (top, total - top - w.shape[1]), (0, 0)))


def _layout_w_in_kernel(w_ref, oz_ref, og_ref):
    w = w_ref[...]
    o = 0
    nsa_q = w[:, o:o + 256]; o += 256
    nsa_kv = w[:, o:o + 384]; o += 384
    nsa_g = w[:, o:o + 3 * MIX_HEADS]; o += 3 * MIX_HEADS
    ret = w[:, o:o + 1024]; o += 1024
    rwkv = w[:, o:o + RWKV_COLS]; o += RWKV_COLS
    conv = w[:, o:o + 768]; o += 768
    gate = w[:, o:o + N_BRANCHES * D_MODEL]

    def rot_perm(x):
        half = HEAD_DIM // 2
        return [x[:, h * HEAD_DIM + p * half:h * HEAD_DIM + (p + 1) * half]
                for p in range(2) for h in range(MIX_HEADS)]

    ret = rot_perm(ret[:, 0:256]) + rot_perm(ret[:, 256:512]) + [ret[:, 512:]]
    zeros = lambda n: jnp.zeros((w.shape[0], n), w.dtype)
    out = jnp.concatenate([conv, nsa_q] + ret + [rwkv, zeros(1024 - RWKV_COLS), nsa_kv, nsa_g,
                                                 zeros(128 - 3 * MIX_HEADS)], axis=1)
    oz_ref[...] = out.astype(BF16)
    og_ref[...] = gate.astype(BF16)


def _layout_w_in(w_in, tr=256):
    L, D, n_in = w_in.shape
    n_gate = N_BRANCHES * D_MODEL
    blk = lambda n: pl.BlockSpec((None, tr, n), lambda l, i: (l, i, 0))
    return pl.pallas_call(
        _layout_w_in_kernel,
        grid=(L, D // tr),
        in_specs=[blk(n_in)],
        out_specs=[blk(Z_COLS), blk(n_gate)],
        out_shape=[jax.ShapeDtypeStruct((L, D, Z_COLS), BF16), jax.ShapeDtypeStruct((L, D, n_gate), BF16)],
        compiler_params=_cparams(("parallel", "parallel")),
        name="layout_w_in",
    )(w_in)


def _layout_cmp(cmp_w, cmp_pe):
    L = cmp_w.shape[0]
    wk = cmp_w[:, 0]
    wv = cmp_w[:, 1]
    zero = jnp.zeros_like(wk)
    blk = jnp.concatenate([jnp.concatenate([wk, zero], axis=3), jnp.concatenate([zero, wv], axis=3)], axis=2)
    pe2 = jnp.concatenate([cmp_pe, cmp_pe], axis=2)[:, :, None, :]
    s = NSA_CMP_STRIDE
    return blk[:, :s].astype(BF16), blk[:, s:].astype(BF16), pe2[:, :s], pe2[:, s:]


def kernel(x, mem, ln_mix_pre, w_in, nsa_cmp_w, nsa_cmp_pe, rel_bias, ret_norm_g, rwkv_mu, rwkv_w0, rwkv_w2, rwkv_a0, rwkv_a2, rwkv_g2, rwkv_k_k, rwkv_k_a, rwkv_r_k, rwkv_ln_g, rwkv_ln_b, conv_w, w_branch, w_mix_out, ln_mix_post, ln_xa_pre, ln_mem, xa_wq, xa_wkv, xa_wo, ln_xa_post, ln_mlp_pre, mlp_w1, mlp_w2, ln_mlp_post):
    B, S, D = x.shape
    M = mem.shape[1]
    depth = w_in.shape[0]
    row = lambda g: g[:, None, :]

    w_in_b, w_gate_b = _layout_w_in(w_in.astype(BF16))
    cmp_lo, cmp_hi, pe_lo, pe_hi = _layout_cmp(nsa_cmp_w, nsa_cmp_pe)
    nsa_tabs = _nsa_tables(rel_bias, S)
    ret_tabs = _ret_tables(S)
    rwkv_consts = _rwkv_consts()
    mu = jnp.pad(rwkv_mu, ((0, 0), (0, 1024 - RWKV_COLS)))[:, None, :]
    lora_w = _hi_lo(jnp.concatenate([_pad_rows(rwkv_w2, 0, 128), _pad_rows(rwkv_a2, RWKV_DECAY_LORA, 128),
                                     _pad_rows(rwkv_g2, RWKV_DECAY_LORA + RWKV_AAA_LORA, 128)], axis=2))
    lora_w = lora_w.reshape(depth, 256, 3 * MIX_WIDTH)
    vec = jnp.stack([rwkv_w0, rwkv_a0, rwkv_k_k, rwkv_k_a, rwkv_r_k, rwkv_ln_g, rwkv_ln_b,
                     jnp.zeros_like(rwkv_w0)], axis=1)
    conv_p = jnp.pad(conv_w, ((0, 0), (0, 8 - CONV_WIDTH), (0, 0)))
    w_branch_b = w_branch.astype(BF16)
    w_mix_out_b = w_mix_out.astype(BF16)
    xa_wq_b = xa_wq.astype(BF16)
    xa_wkv_b = xa_wkv.astype(BF16)
    xa_wo_b = xa_wo.astype(BF16)
    mlp_w1_b = mlp_w1.astype(BF16)
    mlp_w2_b = mlp_w2.astype(BF16)

    xf = x.reshape(B * S, D)
    memf = mem.reshape(B * M, D)
    for l in range(depth):
        z = norm_matmul(xf, row(ln_mix_pre), w_in_b, l, tm=512, tn=Z_COLS, out_dtype=F32)
        prep = nsa_compress(z, cmp_lo, cmp_hi, pe_lo, pe_hi, l, B, S)
        z3 = z.reshape(B, S, Z_COLS)
        o_nsa = nsa_attention(z3, prep, *nsa_tabs, B, S, G=8).reshape(B * S, MIX_WIDTH)
        o_ret = retention(z3, ret_tabs, row(ret_norm_g), l, B, S, G=8).reshape(B * S, MIX_WIDTH)
        o_rwkv = rwkv7(z3, mu, lora_w, vec, rwkv_consts, l, B, S, G=8).reshape(B * S, MIX_WIDTH)
        xf = merge_mix(xf, z, o_nsa, o_ret, o_rwkv, conv_p, row(ln_mix_pre), w_gate_b, w_branch_b,
                       w_mix_out_b, row(ln_mix_post), l, B, S, tm=512)
        kvm = norm_matmul(memf, row(ln_mem), xa_wkv_b, l, tm=2 * M, tn=2 * D_MODEL, out_dtype=BF16)
        xf = cross_attention(xf, kvm, row(ln_xa_pre), xa_wq_b, xa_wo_b, row(ln_xa_post), l, B, S, M, tm=1024)
        xf = mlp(xf, row(ln_mlp_pre), mlp_w1_b, mlp_w2_b, row(ln_mlp_post), l, tm=512)
    return xf.reshape(B, S, D)
```

```python
import functools
import math

import numpy as np
import jax
import jax.numpy as jnp
from jax import lax
from jax.experimental import pallas as pl
from jax.experimental.pallas import tpu as pltpu

F32 = jnp.float32
BF16 = jnp.bfloat16

D_MODEL = 1024
N_BRANCHES = 4
MIX_WIDTH = 256
HEAD_DIM = 64
MIX_HEADS = 4

NSA_CMP_LEN = 32
NSA_CMP_STRIDE = 16
NSA_SLC_LEN = 64
NSA_TOP_N = 8
NSA_WINDOW = 512
Q_BLOCK = 128
NSA_V_ROWS = HEAD_DIM + 16
FORCE_BONUS = 1e4
N_BUCKETS = 32
MAX_DISTANCE = 128

RET_CHUNK = 128
ROPE_BASE = 10000.0
RET_NORM_EPS = 1e-5

RWKV_DECAY_LORA = 32
RWKV_AAA_LORA = 32
RWKV_GATE_LORA = 64
RWKV_GN_EPS = 64e-5
RWKV_COLS = 3 * MIX_WIDTH + RWKV_DECAY_LORA + RWKV_AAA_LORA + RWKV_GATE_LORA
RWKV_CHUNK = 64

CONV_WIDTH = 3
XA_HEADS = 4
XA_HEAD_DIM = D_MODEL // XA_HEADS
D_FF = 4 * D_MODEL

RMS_EPS = 1e-6
LOG2E = math.log2(math.e)
NEG_INF = -1e30
NEG_BIG = -3e38

OFF_CONV = 0
OFF_NSAQ = 768
OFF_RET = 1024
OFF_RWKV = 2048
OFF_NSAKV = 3072
OFF_NSAG = 3456
Z_COLS = 3584

VMEM_LIMIT = 56 * 1024 * 1024


def _cparams(sem):
    return pltpu.CompilerParams(dimension_semantics=sem, vmem_limit_bytes=VMEM_LIMIT)


def _dot(a, b):
    return jnp.dot(a, b, preferred_element_type=F32)


def _dot_nt(a, b):
    return lax.dot_general(a, b, (((1,), (1,)), ((), ())), preferred_element_type=F32)


def _dot_tn(a, b):
    return lax.dot_general(a, b, (((0,), (0,)), ((), ())), preferred_element_type=F32)


def _split2(x):
    hi = x.astype(BF16)
    lo = (x - hi.astype(F32)).astype(BF16)
    return hi, lo


def _dot_x2(x, w_bf16, *more):
    xs = (x,) + more
    rows = x.shape[0]
    parts = [p for xi in xs for p in _split2(xi)]
    out = _dot(jnp.concatenate(parts, axis=0), w_bf16)
    res = [out[2 * i * rows:(2 * i + 1) * rows] + out[(2 * i + 1) * rows:(2 * i + 2) * rows]
           for i in range(len(xs))]
    return res[0] if not more else res


def _rms(x, g):
    ms = jnp.mean(x * x, axis=-1, keepdims=True)
    return x * lax.rsqrt(ms + RMS_EPS) * g


def _norm_matmul_kernel(x_ref, g_ref, w_ref, o_ref, h_ref):
    @pl.when(pl.program_id(1) == 0)
    def _():
        h_ref[...] = _rms(x_ref[...], g_ref[...]).astype(BF16)

    o_ref[...] = _dot(h_ref[...], w_ref[...]).astype(o_ref.dtype)


def norm_matmul(x, g, w, layer, tm, tn, out_dtype):
    M, D = x.shape
    N = w.shape[2]
    w_mode = dict(pipeline_mode=pl.Buffered(1)) if tn == N else {}
    return pl.pallas_call(
        _norm_matmul_kernel,
        grid=(M // tm, N // tn),
        in_specs=[
            pl.BlockSpec((tm, D), lambda i, j: (i, 0)),
            pl.BlockSpec((None, 1, D), lambda i, j: (layer, 0, 0)),
            pl.BlockSpec((None, D, tn), lambda i, j: (layer, 0, j), **w_mode),
        ],
        out_specs=pl.BlockSpec((tm, tn), lambda i, j: (i, j)),
        out_shape=jax.ShapeDtypeStruct((M, N), out_dtype),
        scratch_shapes=[pltpu.VMEM((tm, D), BF16)],
        compiler_params=_cparams(("parallel", "arbitrary")),
        name="norm_matmul",
    )(x, g, w)


def _nsa_compress_kernel(kvc_ref, kvs_ref, kvw_ref, wlo_ref, whi_ref, pelo_ref, pehi_ref,
                         kc_ref, vct_ref, ks_ref, vst_ref, kw_ref, vwt_ref):
    nblk = kc_ref.shape[0]
    n_kt = vwt_ref.shape[0]
    ylo = jnp.zeros((nblk, 128), F32)
    yhi = jnp.zeros((nblk, 128), F32)
    for r in range(NSA_CMP_STRIDE):
        xr = kvc_ref[pl.ds(r, nblk, stride=NSA_CMP_STRIDE), :]
        ylo = ylo + _dot((xr + pelo_ref[r]).astype(BF16), wlo_ref[r])
        yhi = yhi + _dot((xr + pehi_ref[r]).astype(BF16), whi_ref[r])
    y = ylo + pltpu.roll(yhi, nblk - 1, axis=0)
    kc_ref[...] = y.astype(BF16)
    vct_ref[...] = y.T[HEAD_DIM:, :].astype(BF16)
    kw_ref[...] = kvw_ref[...].astype(BF16)
    ones = jnp.ones((NSA_V_ROWS - HEAD_DIM, 128), BF16)
    for kt in range(n_kt):
        rs = slice(kt * 128, (kt + 1) * 128)
        vst_ref[kt] = jnp.concatenate([kvs_ref[rs, :].T[HEAD_DIM:, :].astype(BF16), ones], axis=0)
        vwt_ref[kt] = jnp.concatenate([kvw_ref[rs, :].T[HEAD_DIM:, :].astype(BF16), ones], axis=0)
    S = kvs_ref.shape[0]
    key = lax.broadcasted_iota(jnp.int32, (S, 128), 0)
    lane = lax.broadcasted_iota(jnp.int32, (S, 128), 1)
    onehot = (lane == HEAD_DIM + key // NSA_SLC_LEN).astype(F32)
    ks_ref[0:S, :] = jnp.where(lane < HEAD_DIM, kvs_ref[...], onehot).astype(BF16)
    pad_lane = lax.broadcasted_iota(jnp.int32, (128, 128), 1)
    ks_ref[S:S + 128, :] = (pad_lane == HEAD_DIM + S // NSA_SLC_LEN).astype(BF16)
    vst_ref[n_kt] = jnp.zeros((NSA_V_ROWS, 128), BF16)


def nsa_compress(z, wlo, whi, pelo, pehi, layer, B, S):
    nblk = S // NSA_CMP_STRIDE
    n_kt = S // 128
    c0 = OFF_NSAKV // 128
    wmap = lambda b: (layer, 0, 0, 0)
    b3 = lambda b: (b, 0, 0)
    b4 = lambda b: (b, 0, 0, 0)
    return pl.pallas_call(
        _nsa_compress_kernel,
        grid=(B,),
        in_specs=[
            pl.BlockSpec((S, 128), lambda b: (b, c0)),
            pl.BlockSpec((S, 128), lambda b: (b, c0 + 1)),
            pl.BlockSpec((S, 128), lambda b: (b, c0 + 2)),
            pl.BlockSpec((None, NSA_CMP_STRIDE, 128, 128), wmap),
            pl.BlockSpec((None, NSA_CMP_STRIDE, 128, 128), wmap),
            pl.BlockSpec((None, NSA_CMP_STRIDE, 1, 128), wmap),
            pl.BlockSpec((None, NSA_CMP_STRIDE, 1, 128), wmap),
        ],
        out_specs=[
            pl.BlockSpec((None, nblk, 128), b3),
            pl.BlockSpec((None, HEAD_DIM, nblk), b3),
            pl.BlockSpec((None, S + 128, 128), b3),
            pl.BlockSpec((None, n_kt + 1, NSA_V_ROWS, 128), b4),
            pl.BlockSpec((None, S, 128), b3),
            pl.BlockSpec((None, n_kt, NSA_V_ROWS, 128), b4),
        ],
        out_shape=[
            jax.ShapeDtypeStruct((B, nblk, 128), BF16),
            jax.ShapeDtypeStruct((B, HEAD_DIM, nblk), BF16),
            jax.ShapeDtypeStruct((B, S + 128, 128), BF16),
            jax.ShapeDtypeStruct((B, n_kt + 1, NSA_V_ROWS, 128), BF16),
            jax.ShapeDtypeStruct((B, S, 128), BF16),
            jax.ShapeDtypeStruct((B, n_kt, NSA_V_ROWS, 128), BF16),
        ],
        compiler_params=_cparams(("parallel",)),
        name="nsa_compress",
    )(z, z, z, wlo, whi, pelo, pehi)


def _rowmax(s):
    return jnp.max(s, axis=-1, keepdims=True)


def _rowsum(s):
    return jnp.sum(s, axis=-1, keepdims=True)


def _colmax(s):
    return jnp.max(s, axis=0, keepdims=True)


def _colsum(s):
    return jnp.sum(s, axis=0, keepdims=True)


def _nsa_kernel(q_ref, gl_ref, kc_ref, vct_ref, ks_ref, vst_ref, kw_ref, vwt_ref, bc_ref, tab_ref,
                ovt_ref, o_ref):
    bi = pl.program_id(1)
    G = q_ref.shape[0]
    n_kt = vwt_ref.shape[1]
    n_blk = ovt_ref.shape[0]
    n_cmp = kc_ref.shape[1]
    QB = Q_BLOCK
    R = MIX_HEADS * QB
    n_off = NSA_WINDOW // 128

    lo = jnp.maximum(bi - n_off, 0)
    q4 = [None] * G
    q4_sel = [None] * G
    o_c = [None] * G
    carry_s = [None] * G
    carry_w = [None] * G

    def key_tile(kref, g, kt, lanes):
        return kref[g, pl.ds(pl.multiple_of(kt * 128, 128), 128), lanes]

    def pair_qk(g, k0, k1):
        keys = jnp.concatenate([key_tile(kw_ref, g, k0, slice(0, HEAD_DIM)),
                                key_tile(kw_ref, g, k1, slice(0, HEAD_DIM))], axis=0)
        qk = _dot(keys, q4[g])
        return qk[:128], qk[128:]

    def pair_qk_sel(g, k0, k1):
        keys = jnp.concatenate([key_tile(ks_ref, g, k0, slice(None)), key_tile(ks_ref, g, k1, slice(None))],
                               axis=0)
        qk = _dot(keys, q4_sel[g])
        return qk[:128], qk[128:]

    def head_cols(x, h):
        return x[:, h * QB:(h + 1) * QB]

    key1 = lax.broadcasted_iota(jnp.int32, (128, QB), 0)
    i1 = lax.broadcasted_iota(jnp.int32, (128, QB), 1)
    causal1 = key1 <= i1
    kp = jnp.maximum(bi - 1, 0)
    pen_prev = jnp.where(bi >= 1, 0.0, NEG_INF)

    def win_add(kt, valid):
        shift = jnp.where(valid, jnp.where(bi - kt < n_off, -4096, 0), 4096)
        return jnp.where(key1 > i1 + shift, 0.0, NEG_INF)

    def diag_add(extra):
        return [jnp.where(causal1, head_cols(tab_ref[0], h) + extra, NEG_INF) for h in range(MIX_HEADS)]

    def prev_add(extra):
        return [head_cols(tab_ref[1], h) + extra for h in range(MIX_HEADS)]

    def pair_update(carry, qk0, add0, vt0, qk1, add1, vt1):
        p0s, p1s, ms, alphas = [], [], [], []
        for h in range(MIX_HEADS):
            a0 = head_cols(qk0, h) if add0 is None else head_cols(qk0, h) + add0[h]
            a1 = head_cols(qk1, h) if add1 is None else head_cols(qk1, h) + add1[h]
            m_new = jnp.maximum(_colmax(a0), _colmax(a1))
            if carry is not None:
                m_old = head_cols(carry[0], h)
                m_new = jnp.maximum(m_old, m_new)
                alphas.append(jnp.exp2(m_old - m_new))
            ms.append(m_new)
            p0s.append(jnp.exp2(a0 - m_new).astype(BF16))
            p1s.append(jnp.exp2(a1 - m_new).astype(BF16))
        yield
        probs = jnp.concatenate([jnp.concatenate(p0s, axis=1), jnp.concatenate(p1s, axis=1)], axis=0)
        acc = _dot(jnp.concatenate([vt0, vt1], axis=1), probs)
        if carry is not None:
            acc = jnp.concatenate(alphas, axis=1) * carry[1] + acc
        return jnp.concatenate(ms, axis=1), acc

    def head_part(g):
        qt = (q_ref[g] * (HEAD_DIM ** -0.5 * LOG2E)).T
        q4[g] = jnp.concatenate([qt[h * HEAD_DIM:(h + 1) * HEAD_DIM] for h in range(MIX_HEADS)],
                                axis=1).astype(BF16)
        yield
        qk_d, qk_p = pair_qk(g, bi, kp)
        carry_w[g] = yield from pair_update(None, qk_d, diag_add(0.0), vwt_ref[g, bi],
                                            qk_p, prev_add(pen_prev), vwt_ref[g, kp])
        yield
        g0 = pl.multiple_of((n_kt - 1 - bi) * 8, 8)
        qk = _dot(kc_ref[g, :, :HEAD_DIM], q4[g])
        nrow = lax.broadcasted_iota(jnp.int32, (n_cmp, QB), 0)
        tq = bi * QB + lax.broadcasted_iota(jnp.int32, (n_cmp, QB), 1)
        vis = tq - (nrow * NSA_CMP_STRIDE + (NSA_CMP_LEN - 1)) >= 0
        live = (tq >= NSA_CMP_LEN - 1).astype(F32)
        yield
        p_cs = []
        for h in range(MIX_HEADS):
            s = jnp.where(vis, head_cols(qk, h) + bc_ref[pl.ds(g0, n_cmp), h * QB:(h + 1) * QB], NEG_INF)
            e = jnp.exp2(s - _colmax(s))
            p_cs.append(e / _colsum(e) * live)
        yield
        o_c[g] = _dot(vct_ref[g], jnp.concatenate(p_cs, axis=1).astype(BF16))
        p_sum = p_cs[0] + p_cs[1] + p_cs[2] + p_cs[3]
        p_hi, p_lo = _split2(p_sum)
        imp = _dot(ovt_ref[...], p_hi) + _dot(ovt_ref[...], p_lo)
        yield
        blk = lax.broadcasted_iota(jnp.int32, (n_blk, QB), 0)
        cur = (bi * QB + lax.broadcasted_iota(jnp.int32, (n_blk, QB), 1)) >> 6
        forced = (blk == 0) | (blk == cur) | (blk == cur - 1)
        imp = jnp.where(forced, imp + FORCE_BONUS, imp)
        imp = jnp.where(blk <= cur, imp, NEG_INF)
        blk_f = blk.astype(F32)
        sel = jnp.zeros((n_blk, QB), F32)
        for _ in range(NSA_TOP_N):
            mx = _colmax(imp)
            idx = jnp.min(jnp.where(imp == mx, blk_f, 4096.0), axis=0, keepdims=True)
            pick = blk_f == idx
            sel = jnp.where(pick, 1.0, sel)
            imp = jnp.where(pick, NEG_BIG, imp)
            yield
        sel_add = (sel - 1.0) * (-NEG_INF)
        rows_left = 128 - HEAD_DIM - n_blk
        pad_rows = jnp.where(lax.broadcasted_iota(jnp.int32, (rows_left, QB), 0) == 0, NEG_INF, 0.0)
        mask_rows = jnp.concatenate([sel_add, pad_rows], axis=0).astype(BF16)
        q4_sel[g] = jnp.concatenate([q4[g], jnp.concatenate([mask_rows] * MIX_HEADS, axis=1)], axis=0)
        yield
        qk_d, qk_p = pair_qk_sel(g, bi, kp)
        carry_s[g] = yield from pair_update(None, qk_d, diag_add(0.0), vst_ref[g, bi],
                                            qk_p, prev_add(pen_prev), vst_ref[g, kp])

    _round_robin([head_part(g) for g in range(G)])

    n_old = jnp.maximum(bi - 1, 0)

    def sel_step(g, j, carry, out):
        k0 = 2 * j
        k1 = jnp.where(k0 + 1 < n_old, k0 + 1, n_kt)
        qk0, qk1 = pair_qk_sel(g, k0, k1)
        yield
        out[g] = yield from pair_update(carry, qk0, None, vst_ref[g, k0], qk1, None, vst_ref[g, k1])

    def win_step(g, j, carry, out):
        k0 = lo + 2 * j
        k1 = jnp.minimum(k0 + 1, bi)
        qk0, qk1 = pair_qk(g, k0, k1)
        yield
        out[g] = yield from pair_update(carry, qk0, [win_add(k0, True)] * MIX_HEADS, vwt_ref[g, k0],
                                        qk1, [win_add(k1, k0 + 1 < n_old)] * MIX_HEADS, vwt_ref[g, k1])

    def joint(step):
        def body(j, carries):
            out = [None] * G
            _round_robin([step(g, j, carries[g], out) for g in range(G)])
            return tuple(out)
        return body

    res_s = lax.fori_loop(0, (n_old + 1) // 2, joint(sel_step), tuple(carry_s))
    res_w = lax.fori_loop(0, (n_old - lo + 1) // 2, joint(win_step), tuple(carry_w))

    def tail_part(g):
        acc_s = res_s[g][1]
        acc_w = res_w[g][1]
        o_s = acc_s[:HEAD_DIM] * (1.0 / acc_s[HEAD_DIM:HEAD_DIM + 1])
        o_w = acc_w[:HEAD_DIM] * (1.0 / acc_w[HEAD_DIM:HEAD_DIM + 1])
        gate = jax.nn.sigmoid(gl_ref[g].T)
        yield
        outs = []
        for h in range(MIX_HEADS):
            cs = slice(h * QB, (h + 1) * QB)
            outs.append(gate[3 * h:3 * h + 1] * o_c[g][:, cs] + gate[3 * h + 1:3 * h + 2] * o_s[:, cs]
                        + gate[3 * h + 2:3 * h + 3] * o_w[:, cs])
        o_ref[g] = jnp.concatenate(outs, axis=0).T

    _round_robin([tail_part(g) for g in range(G)])


def nsa_attention(z, prep, bias_c, tab, ovt, B, S, G):
    n_qb = S // Q_BLOCK
    n_kt = S // 128
    n_cmp = S // NSA_CMP_STRIDE
    R = MIX_HEADS * Q_BLOCK
    kc, vct, ks, vst, kw, vwt = prep
    b3 = lambda b, i: (b, 0, 0)
    b4 = lambda b, i: (b, 0, 0, 0)
    return pl.pallas_call(
        _nsa_kernel,
        grid=(B // G, n_qb),
        in_specs=[
            pl.BlockSpec((G, Q_BLOCK, MIX_WIDTH), lambda b, i: (b, i, OFF_NSAQ // MIX_WIDTH)),
            pl.BlockSpec((G, Q_BLOCK, 128), lambda b, i: (b, i, OFF_NSAG // 128)),
            pl.BlockSpec((G, n_cmp, 128), b3),
            pl.BlockSpec((G, HEAD_DIM, n_cmp), b3),
            pl.BlockSpec((G, S + 128, 128), b3),
            pl.BlockSpec((G, n_kt + 1, NSA_V_ROWS, 128), b4),
            pl.BlockSpec((G, S, 128), b3),
            pl.BlockSpec((G, n_kt, NSA_V_ROWS, 128), b4),
            pl.BlockSpec(bias_c.shape, lambda b, i: (0, 0)),
            pl.BlockSpec((2, 128, R), lambda b, i: (0, 0, 0)),
            pl.BlockSpec(ovt.shape, lambda b, i: (0, 0)),
        ],
        out_specs=pl.BlockSpec((G, Q_BLOCK, MIX_WIDTH), lambda b, i: (b, i, 0)),
        out_shape=jax.ShapeDtypeStruct((B, S, MIX_WIDTH), F32),
        compiler_params=_cparams(("parallel", "arbitrary")),
        name="nsa_attention",
    )(z, z, kc, vct, ks, vst, kw, vwt, bias_c, tab, ovt)


def _stack_heads(x, head_of_lane):
    return jnp.concatenate([jnp.where(head_of_lane == h, x, jnp.zeros_like(x))
                            for h in range(MIX_HEADS)], axis=0)


def _head_norm(y, seg_mean, eps):
    mu = _dot_x2(y, seg_mean)
    d = y - mu
    var = _dot_x2(d * d, seg_mean)
    return d * lax.rsqrt(var + eps)


def _round_robin(chains):
    while chains:
        chains = [c for c in chains if next(c, StopIteration) is not StopIteration]


def _retention_kernel(z_ref, cos_ref, sin_ref, indec_ref, qdec_ref, kdec_ref, cdec_ref, bd_ref,
                      segm_ref, ng_ref, o_ref, st_ref):
    @pl.when(pl.program_id(1) == 0)
    def _():
        st_ref[...] = jnp.zeros_like(st_ref)

    _round_robin([_retention_chunk(z_ref.at[g], cos_ref, sin_ref, indec_ref, qdec_ref, kdec_ref, cdec_ref,
                                   bd_ref, segm_ref, ng_ref, o_ref.at[g], st_ref.at[g])
                  for g in range(z_ref.shape[0])])


def _retention_chunk(z_ref, cos_ref, sin_ref, indec_ref, qdec_ref, kdec_ref, cdec_ref, bd_ref,
                     segm_ref, ng_ref, o_ref, st_ref):
    L = RET_CHUNK
    z = z_ref[...]
    q = z[:, 0:256]
    k = z[:, 256:512]
    v = z[:, 512:768]
    g = z[:, 768:1024]
    cos = cos_ref[...]
    sin = sin_ref[...]

    def rot(u):
        u1 = u[:, :128]
        u2 = u[:, 128:]
        return jnp.concatenate([u1 * cos - u2 * sin, u2 * cos + u1 * sin], axis=1)

    qr = rot(q) * (HEAD_DIM ** -0.5)
    kr = rot(k)
    lane = lax.broadcasted_iota(jnp.int32, (L, 256), 1)
    head_qk = (lane & 127) >> 5
    head_v = lane >> 6
    qb = qr.astype(BF16)
    kb = kr.astype(BF16)
    vb = v.astype(BF16)
    yield

    att = _dot_nt(_stack_heads(qb, head_qk), kb) * indec_ref[...]
    yield
    o_st = _dot(att.astype(BF16), vb)
    o = jnp.zeros((L, 256), F32)
    for h in range(MIX_HEADS):
        o = o + jnp.where(head_v == h, o_st[h * L:(h + 1) * L], 0.0)
    yield
    state = st_ref[...]
    o = o + _dot(qb, state.astype(BF16)) * qdec_ref[...]
    st_ref[...] = state * cdec_ref[...] + _dot_tn((kr * kdec_ref[...]).astype(BF16), vb) * bd_ref[...]
    yield

    mu = _dot_x2(o, segm_ref[...])
    yield
    d = o - mu
    var = _dot_x2(d * d, segm_ref[...])
    yield
    o_ref[...] = g * jax.nn.sigmoid(g) * (d * lax.rsqrt(var + RET_NORM_EPS) * ng_ref[...])


def retention(z, tabs, ng, layer, B, S, G):
    L = RET_CHUNK
    nc = S // L
    cos, sin, indec, qdec, kdec, cdec, bd, segm = tabs
    const2 = lambda b, c: (0, 0)
    return pl.pallas_call(
        _retention_kernel,
        grid=(B // G, nc),
        in_specs=[
            pl.BlockSpec((G, L, 1024), lambda b, c: (b, c, OFF_RET // 1024)),
            pl.BlockSpec((L, 128), lambda b, c: (c, 0)),
            pl.BlockSpec((L, 128), lambda b, c: (c, 0)),
            pl.BlockSpec((MIX_HEADS * L, L), const2),
            pl.BlockSpec((L, 256), const2),
            pl.BlockSpec((L, 256), const2),
            pl.BlockSpec((1, 256), const2),
            pl.BlockSpec((256, 256), const2),
            pl.BlockSpec((256, 256), const2),
            pl.BlockSpec((None, 1, 256), lambda b, c: (layer, 0, 0)),
        ],
        out_specs=pl.BlockSpec((G, L, 256), lambda b, c: (b, c, 0)),
        out_shape=jax.ShapeDtypeStruct((B, S, 256), F32),
        scratch_shapes=[pltpu.VMEM((G, 256, 256), F32)],
        compiler_params=_cparams(("parallel", "arbitrary")),
        name="retention",
    )(z, cos, sin, indec, qdec, kdec, cdec, bd, segm, ng)


def _rwkv_kernel(z_ref, mu_ref, lw_ref, vec_ref, tri_ref, bdl_ref, wide_ref,
                 segs_ref, segm_ref, o_ref, st_ref, prev_ref):
    @pl.when(pl.program_id(1) == 0)
    def _():
        st_ref[...] = jnp.zeros_like(st_ref)
        prev_ref[...] = jnp.zeros_like(prev_ref)

    _round_robin([_rwkv_chunk(z_ref.at[g], mu_ref, lw_ref, vec_ref, tri_ref, bdl_ref,
                              wide_ref, segs_ref, segm_ref, o_ref.at[g], st_ref.at[g], prev_ref.at[g])
                  for g in range(z_ref.shape[0])])


def _rwkv_chunk(z_ref, mu_ref, lw_ref, vec_ref, tri_ref, bdl_ref, wide_ref,
                segs_ref, segm_ref, o_ref, st_ref, prev_ref):
    C = RWKV_CHUNK
    z = z_ref[...]
    rows = lax.broadcasted_iota(jnp.int32, (C, 1024), 0)
    zs = jnp.where(rows == 0, prev_ref[0:1, :], pltpu.roll(z, 1, axis=0))
    prev_ref[...] = jnp.broadcast_to(z[C - 1:C, :], prev_ref.shape)
    zf = z + (zs - z) * mu_ref[...]
    r = zf[:, 0:256]
    k = zf[:, 256:512]
    v = zf[:, 512:768]
    lora = zf[:, 768:896]

    vec = vec_ref[...]
    w0, a0, k_k, k_a, r_k, ln_g, ln_b = (vec[i:i + 1] for i in range(7))

    lane = lax.broadcasted_iota(jnp.int32, lora.shape, 1)
    lora_in = jnp.where(lane < RWKV_DECAY_LORA, jnp.tanh(lora),
                        jnp.where(lane < RWKV_DECAY_LORA + RWKV_AAA_LORA, lora, jax.nn.sigmoid(lora)))
    hi, lo = _split2(lora_in)
    lhs = jnp.concatenate([jnp.concatenate([hi, hi], axis=1),
                           jnp.concatenate([lo, jnp.zeros_like(lo)], axis=1)], axis=0)
    both = _dot(lhs, lw_ref[...])
    lora_out = both[:C] + both[C:]
    wpre = w0 + lora_out[:, 0:256]
    y = -wpre
    softplus = jnp.maximum(y, 0.0) + jnp.log(1.0 + jnp.exp(-jnp.abs(y)))
    w_log = -softplus - 0.5
    ld = -jnp.exp(w_log)
    a = jax.nn.sigmoid(a0 + lora_out[:, 256:512])
    gate = lora_out[:, 512:768]
    kk = k * k_k
    k2 = k * (1.0 + (a - 1.0) * k_a)
    kk_sq, bonus = _dot_x2(kk * kk, segs_ref[...], r * k2 * r_k)
    kk = kk / jnp.maximum(jnp.sqrt(kk_sq), 1e-12)
    yield

    tri = tri_ref[...]
    l1 = ld.astype(BF16)
    r1 = ld - l1.astype(F32)
    l2 = r1.astype(BF16)
    l3 = (r1 - l2.astype(F32)).astype(BF16)
    cs = _dot(jnp.concatenate([tri, tri, tri], axis=1), jnp.concatenate([l1, l2, l3], axis=0))
    cs_end = cs[C - 1:C, :]
    yield
    e_neg = jnp.exp(-cs)
    e_end = jnp.exp(cs_end - cs)
    kka = kk * a
    a_t = -kk * jnp.exp(cs - ld)
    r_t = r * jnp.exp(cs)
    b_t = kka * e_neg
    k_t = k2 * e_neg
    b_g = kka * e_end
    k_g = k2 * e_end

    lane = lax.broadcasted_iota(jnp.int32, (C, 256), 1)
    head = lane >> 6
    stack = lambda x: _stack_heads(x.astype(BF16), head)
    a_st, b_st, k_st, v_st = stack(a_t), stack(b_t), stack(k_t), stack(v)

    n_pow = _dot_nt(a_st, b_st) * bdl_ref[...]
    ri = lax.broadcasted_iota(jnp.int32, (4 * C, 4 * C), 0)
    ci = lax.broadcasted_iota(jnp.int32, (4 * C, 4 * C), 1)
    t_inv = jnp.where(ri == ci, 1.0, 0.0) + n_pow
    yield

    ar = jnp.concatenate([a_t, r_t], axis=0).astype(BF16)
    bk = jnp.concatenate([b_st, k_st], axis=0)
    wide = _dot_nt(ar, bk)
    strict = wide_ref[0]
    incl = wide_ref[1]
    a_ak = (wide[0:C, 4 * C:] * strict).astype(BF16)
    a_rb = (wide[C:, 0:4 * C] * incl).astype(BF16)
    a_rk = (wide[C:, 4 * C:] * incl).astype(BF16)
    yield

    sq = 2
    while sq < C:
        nb = n_pow.astype(BF16)
        n_pow = _dot(nb, nb)
        yield
        t_inv = t_inv + _dot(t_inv.astype(BF16), n_pow.astype(BF16))
        sq *= 2
    yield

    state = st_ref[...]
    sb = state.astype(BF16)
    from_state = _dot_nt(ar, sb)
    x = from_state[0:C] + _dot(a_ak, v_st)
    yield
    u_st = _dot(t_inv.astype(BF16), stack(x)).astype(BF16)
    yield
    uv = jnp.concatenate([u_st, v_st], axis=0)
    yv = from_state[C:] + _dot(jnp.concatenate([a_rb, a_rk], axis=1), uv)
    bkg = jnp.concatenate([stack(b_g), stack(k_g)], axis=0)
    st_ref[...] = state * jnp.exp(cs_end) + _dot_tn(uv, bkg)
    yield

    yn = _head_norm(yv, segm_ref[...], RWKV_GN_EPS) * ln_g + ln_b
    yn = yn + bonus * v
    o_ref[...] = yn * gate


def rwkv7(z, mu, lora_w, vec, consts, layer, B, S, G):
    C = RWKV_CHUNK
    nc = S // C
    tri, bdl, wide, segs, segm = consts
    const2 = lambda b, c: (0, 0)
    return pl.pallas_call(
        _rwkv_kernel,
        grid=(B // G, nc),
        in_specs=[
            pl.BlockSpec((G, C, 1024), lambda b, c: (b, c, OFF_RWKV // 1024)),
            pl.BlockSpec((None, 1, 1024), lambda b, c: (layer, 0, 0)),
            pl.BlockSpec((None, 256, 768), lambda b, c: (layer, 0, 0)),
            pl.BlockSpec((None, 8, 256), lambda b, c: (layer, 0, 0)),
            pl.BlockSpec((C, C), const2),
            pl.BlockSpec((4 * C, 4 * C), const2),
            pl.BlockSpec((2, C, 4 * C), lambda b, c: (0, 0, 0)),
            pl.BlockSpec((256, 256), const2),
            pl.BlockSpec((256, 256), const2),
        ],
        out_specs=pl.BlockSpec((G, C, 256), lambda b, c: (b, c, 0)),
        out_shape=jax.ShapeDtypeStruct((B, S, 256), F32),
        scratch_shapes=[pltpu.VMEM((G, 256, 256), F32), pltpu.VMEM((G, 8, 1024), F32)],
        compiler_params=_cparams(("parallel", "arbitrary")),
        name="rwkv7",
    )(z, mu, lora_w, vec, tri, bdl, wide, segs, segm)


def _merge_kernel(x_ref, zc_ref, on_ref, or_ref, ow_ref, cw_ref, gpre_ref, wg_ref, wb_ref, wo_ref, g_ref,
                  o_ref, carry_ref):
    tm = x_ref.shape[0]
    x = x_ref[...]
    h = _rms(x, gpre_ref[...]).astype(BF16)

    @pl.when(pl.program_id(1) == 0)
    def _():
        carry_ref[...] = jnp.zeros_like(carry_ref)

    zc = zc_ref[...]
    b_g = zc[:, 0:256]
    u = zc[:, 256:512] * zc[:, 512:768]
    prev = carry_ref[...]
    rows = lax.broadcasted_iota(jnp.int32, (tm, 256), 0)
    u1 = jnp.where(rows == 0, prev[7:8], pltpu.roll(u, 1, axis=0))
    u2 = jnp.where(rows == 0, prev[6:7], jnp.where(rows == 1, prev[7:8], pltpu.roll(u, 2, axis=0)))
    carry_ref[...] = u[tm - 8:tm]
    cw = cw_ref[...]
    o_conv = b_g * (cw[0:1] * u2 + cw[1:2] * u1 + cw[2:3] * u)

    branches = (on_ref[...], or_ref[...], ow_ref[...], o_conv)
    merged = jnp.zeros((tm, D_MODEL), F32)
    for m in range(N_BRANCHES):
        gm = jax.nn.sigmoid(_dot(h, wg_ref[:, m * D_MODEL:(m + 1) * D_MODEL]))
        merged = merged + gm * _dot(branches[m].astype(BF16), wb_ref[m])
    y = _dot(merged.astype(BF16), wo_ref[...])
    o_ref[...] = x + _rms(y, g_ref[...])


def merge_mix(x, z, o_nsa, o_ret, o_rwkv, conv_w, g_pre, w_gate, w_branch, w_out, g_post, layer, B, S, tm):
    nt = S // tm
    rowmap = lambda b, i: (b * nt + i, 0)
    lmap = lambda b, i: (layer, 0, 0)
    return pl.pallas_call(
        _merge_kernel,
        grid=(B, nt),
        in_specs=[
            pl.BlockSpec((tm, D_MODEL), rowmap),
            pl.BlockSpec((tm, 768), lambda b, i: (b * nt + i, OFF_CONV // 768)),
            pl.BlockSpec((tm, 256), rowmap),
            pl.BlockSpec((tm, 256), rowmap),
            pl.BlockSpec((tm, 256), rowmap),
            pl.BlockSpec((None, 8, 256), lmap),
            pl.BlockSpec((None, 1, D_MODEL), lmap),
            pl.BlockSpec((None, D_MODEL, N_BRANCHES * D_MODEL), lmap, pipeline_mode=pl.Buffered(1)),
            pl.BlockSpec((None, N_BRANCHES, MIX_WIDTH, D_MODEL), lambda b, i: (layer, 0, 0, 0),
                         pipeline_mode=pl.Buffered(1)),
            pl.BlockSpec((None, D_MODEL, D_MODEL), lmap, pipeline_mode=pl.Buffered(1)),
            pl.BlockSpec((None, 1, D_MODEL), lmap),
        ],
        out_specs=pl.BlockSpec((tm, D_MODEL), rowmap),
        out_shape=jax.ShapeDtypeStruct((B * S, D_MODEL), F32),
        scratch_shapes=[pltpu.VMEM((8, 256), F32)],
        compiler_params=_cparams(("parallel", "arbitrary")),
        name="merge_mix",
    )(x, z, o_nsa, o_ret, o_rwkv, conv_w, g_pre, w_gate, w_branch, w_out, g_post)


def _xattn_kernel(x_ref, kv_ref, gpre_ref, wq_ref, wo_ref, gpost_ref, o_ref):
    n_sub = 1
    rows = x_ref.shape[0] // n_sub

    def part(r0):
        rs = pl.ds(r0, rows)
        x = x_ref[rs, :]
        h = _rms(x, gpre_ref[...]).astype(BF16)
        yield
        q = (_dot(h, wq_ref[...]) * (XA_HEAD_DIM ** -0.5)).astype(BF16)
        yield
        outs = []
        for hd in range(XA_HEADS):
            cs = slice(hd * XA_HEAD_DIM, (hd + 1) * XA_HEAD_DIM)
            kh = kv_ref[:, hd * XA_HEAD_DIM:(hd + 1) * XA_HEAD_DIM]
            vh = kv_ref[:, D_MODEL + hd * XA_HEAD_DIM:D_MODEL + (hd + 1) * XA_HEAD_DIM]
            s = _dot_nt(q[:, cs], kh)
            yield
            e = jnp.exp(s - _rowmax(s))
            p = e / _rowsum(e)
            outs.append(_dot(p.astype(BF16), vh).astype(BF16))
            yield
        o = jnp.concatenate(outs, axis=1)
        y = _dot(o, wo_ref[...])
        yield
        o_ref[rs, :] = x + _rms(y, gpost_ref[...])

    _round_robin([part(i * rows) for i in range(n_sub)])


def cross_attention(x, kv, g_pre, wq, wo, g_post, layer, B, S, M, tm):
    nt = S // tm
    rowmap = lambda b, i: (b * nt + i, 0)
    lmap = lambda b, i: (layer, 0, 0)
    return pl.pallas_call(
        _xattn_kernel,
        grid=(B, nt),
        in_specs=[
            pl.BlockSpec((tm, D_MODEL), rowmap),
            pl.BlockSpec((M, 2 * D_MODEL), lambda b, i: (b, 0)),
            pl.BlockSpec((None, 1, D_MODEL), lmap),
            pl.BlockSpec((None, D_MODEL, D_MODEL), lmap),
            pl.BlockSpec((None, D_MODEL, D_MODEL), lmap),
            pl.BlockSpec((None, 1, D_MODEL), lmap),
        ],
        out_specs=pl.BlockSpec((tm, D_MODEL), rowmap),
        out_shape=jax.ShapeDtypeStruct((B * S, D_MODEL), F32),
        compiler_params=_cparams(("parallel", "arbitrary")),
        name="cross_attention",
    )(x, kv, g_pre, wq, wo, g_post)


def _mlp_kernel(x_ref, gpre_ref, w1_ref, w2_ref, gpost_ref, o_ref):
    x = x_ref[...]
    h = _rms(x, gpre_ref[...]).astype(BF16)
    tf = 1024
    acc = jnp.zeros(x.shape, F32)
    for f in range(D_FF // tf):
        a = jnp.maximum(_dot(h, w1_ref[:, f * tf:(f + 1) * tf]), 0.0)
        acc = acc + _dot((a * a).astype(BF16), w2_ref[f * tf:(f + 1) * tf, :])
    o_ref[...] = x + _rms(acc, gpost_ref[...])


def mlp(x, g_pre, w1, w2, g_post, layer, tm):
    M = x.shape[0]
    lmap = lambda i: (layer, 0, 0)
    return pl.pallas_call(
        _mlp_kernel,
        grid=(M // tm,),
        in_specs=[
            pl.BlockSpec((tm, D_MODEL), lambda i: (i, 0)),
            pl.BlockSpec((None, 1, D_MODEL), lmap),
            pl.BlockSpec((None, D_MODEL, D_FF), lmap, pipeline_mode=pl.Buffered(1)),
            pl.BlockSpec((None, D_FF, D_MODEL), lmap, pipeline_mode=pl.Buffered(1)),
            pl.BlockSpec((None, 1, D_MODEL), lmap),
        ],
        out_specs=pl.BlockSpec((tm, D_MODEL), lambda i: (i, 0)),
        out_shape=jax.ShapeDtypeStruct((M, D_MODEL), F32),
        compiler_params=_cparams(("parallel",)),
        name="mlp",
    )(x, g_pre, w1, w2, g_post)


def _t5_bucket(dist):
    n = jnp.maximum(dist, 0)
    max_exact = N_BUCKETS // 2
    nf = jnp.maximum(n, 1).astype(F32)
    large = max_exact + (jnp.log(nf / max_exact) / math.log(MAX_DISTANCE / max_exact)
                         * (N_BUCKETS - max_exact)).astype(jnp.int32)
    large = jnp.minimum(large, N_BUCKETS - 1)
    return jnp.where(n < max_exact, n, large)


def _nsa_tables(rel_bias, S):
    n_qb = S // Q_BLOCK
    n_cmp = S // NSA_CMP_STRIDE
    n_blk = S // NSA_SLC_LEN
    bias_f = rel_bias.astype(F32)
    i = jnp.arange(Q_BLOCK)
    per_qb = Q_BLOCK // NSA_CMP_STRIDE
    g = jnp.arange(n_cmp + per_qb * (n_qb - 1)) - per_qb * (n_qb - 1)
    d_c = i[None, :] - (g[:, None] * NSA_CMP_STRIDE + NSA_CMP_LEN - 1)
    def lookup(dist):
        hit = _t5_bucket(dist)[..., None, None] == jnp.arange(N_BUCKETS)[:, None]
        return jnp.sum(jnp.where(hit, bias_f, 0.0), axis=-2)

    bias_c = lookup(d_c).transpose(0, 2, 1).reshape(g.shape[0], MIX_HEADS * Q_BLOCK)
    d0 = i[:, None] - i[None, :]
    tab = jnp.stack([lookup(d0), lookup(d0 + Q_BLOCK)]) - bias_f[N_BUCKETS - 1]
    tab = tab.transpose(0, 2, 3, 1).reshape(2, Q_BLOCK, MIX_HEADS * Q_BLOCK)
    cs = np.arange(n_cmp)[None, :] * NSA_CMP_STRIDE
    ss = np.arange(n_blk)[:, None] * NSA_SLC_LEN
    ovt = np.clip(np.minimum(cs + NSA_CMP_LEN, ss + NSA_SLC_LEN) - np.maximum(cs, ss), 0, None)
    ovt = ovt.astype(np.float32) / NSA_CMP_LEN
    ovt[:, (S - NSA_CMP_LEN) // NSA_CMP_STRIDE + 1:] = 0.0
    return bias_c * LOG2E, tab * LOG2E, jnp.asarray(ovt, BF16)


def _ret_tables(S):
    L = RET_CHUNK
    H = MIX_HEADS
    half = HEAD_DIM // 2
    pos = jnp.arange(S)
    inv_freq = ROPE_BASE ** (-jnp.arange(half, dtype=F32) / half)
    ang = pos.astype(F32)[:, None] * inv_freq[None, :]
    cos = jnp.tile(jnp.cos(ang), (1, H))
    sin = jnp.tile(jnp.sin(ang), (1, H))
    lg = jnp.log(1.0 - 2.0 ** (-5.0 - jnp.arange(H, dtype=F32)))
    n = jnp.arange(L, dtype=F32)
    diff = n[:, None] - n[None, :]
    inner = jnp.where(diff >= 0, jnp.exp(jnp.maximum(diff, 0.0)[None] * lg[:, None, None]), 0.0)
    indec = inner.reshape(H * L, L)
    q_decay = jnp.exp((n + 1.0)[None, :] * lg[:, None])
    k_decay = jnp.exp((L - 1.0 - n)[None, :] * lg[:, None])
    chunk_decay = jnp.exp(L * lg)
    lane = np.arange(256)
    head_v = lane // HEAD_DIM
    head_qk = (lane % 128) // half
    qdec = q_decay.T[:, head_v]
    kdec = k_decay.T[:, head_qk]
    cdec = chunk_decay[head_v][None, :]
    bd = jnp.asarray((head_qk[:, None] == head_v[None, :]).astype(np.float32))
    segm = jnp.asarray((head_v[:, None] == head_v[None, :]).astype(np.float32) / HEAD_DIM, BF16)
    return cos, sin, indec, qdec, kdec, cdec, bd, segm


def _rwkv_consts():
    C = RWKV_CHUNK
    t = np.arange(C)
    tri = (t[:, None] >= t[None, :]).astype(np.float32)
    r = np.arange(4 * C)
    bdl = ((r[:, None] // C == r[None, :] // C) & (r[:, None] % C > r[None, :] % C)).astype(np.float32)
    strict = (t[:, None] > (r[None, :] % C)).astype(np.float32)
    incl = (t[:, None] >= (r[None, :] % C)).astype(np.float32)
    lane = np.arange(256) // HEAD_DIM
    seg = (lane[:, None] == lane[None, :]).astype(np.float32)
    return (jnp.asarray(tri, BF16), jnp.asarray(bdl), jnp.asarray(np.stack([strict, incl])),
            jnp.asarray(seg, BF16), jnp.asarray(seg / HEAD_DIM, BF16))


def _hi_lo(w):
    hi = w.astype(BF16)
    lo = (w - hi.astype(F32)).astype(BF16)
    return jnp.stack([hi, lo], axis=1)


def _pad_rows(w, top, total):
    return jnp.pad(w, ((0, 0), (top, total - top - w.shape[1]), (0, 0)))


def _layout_w_in_kernel(w_ref, oz_ref, og_ref):
    w = w_ref[...]
    o = 0
    nsa_q = w[:, o:o + 256]; o += 256
    nsa_kv = w[:, o:o + 384]; o += 384
    nsa_g = w[:, o:o + 3 * MIX_HEADS]; o += 3 * MIX_HEADS
    ret = w[:, o:o + 1024]; o += 1024
    rwkv = w[:, o:o + RWKV_COLS]; o += RWKV_COLS
    conv = w[:, o:o + 768]; o += 768
    gate = w[:, o:o + N_BRANCHES * D_MODEL]

    def rot_perm(x):
        half = HEAD_DIM // 2
        return [x[:, h * HEAD_DIM + p * half:h * HEAD_DIM + (p + 1) * half]
                for p in range(2) for h in range(MIX_HEADS)]

    ret = rot_perm(ret[:, 0:256]) + rot_perm(ret[:, 256:512]) + [ret[:, 512:]]
    zeros = lambda n: jnp.zeros((w.shape[0], n), w.dtype)
    out = jnp.concatenate([conv, nsa_q] + ret + [rwkv, zeros(1024 - RWKV_COLS), nsa_kv, nsa_g,
                                                 zeros(128 - 3 * MIX_HEADS)], axis=1)
    oz_ref[...] = out.astype(BF16)
    og_ref[...] = gate.astype(BF16)


def _rowmajor_w_in_kernel(w_ref, o_ref, *, n_cols):
    bc = w_ref.shape[0]
    col = pl.program_id(0) * bc + lax.broadcasted_iota(jnp.int32, (bc, w_ref.shape[2]), 0)
    for l in range(w_ref.shape[1]):
        o_ref[l] = jnp.where(col < n_cols, w_ref[:, l, :], 0.0).T.astype(BF16)


def _layout_w_in(w_in, tr=256, bc=256):
    L, D, n_in = w_in.shape
    n_pad = pl.cdiv(n_in, bc) * bc
    w_rows = pl.pallas_call(
        functools.partial(_rowmajor_w_in_kernel, n_cols=n_in),
        grid=(n_pad // bc,),
        in_specs=[pl.BlockSpec((bc, L, D), lambda i: (i, 0, 0))],
        out_specs=pl.BlockSpec((L, D, bc), lambda i: (0, 0, i)),
        out_shape=jax.ShapeDtypeStruct((L, D, n_pad), BF16),
        compiler_params=_cparams(("parallel",)),
        name="rowmajor_w_in",
    )(jnp.transpose(w_in, (2, 0, 1)))
    n_gate = N_BRANCHES * D_MODEL
    blk = lambda n: pl.BlockSpec((None, tr, n), lambda l, i: (l, i, 0))
    return pl.pallas_call(
        _layout_w_in_kernel,
        grid=(L, D // tr),
        in_specs=[blk(n_pad)],
        out_specs=[blk(Z_COLS), blk(n_gate)],
        out_shape=[jax.ShapeDtypeStruct((L, D, Z_COLS), BF16), jax.ShapeDtypeStruct((L, D, n_gate), BF16)],
        compiler_params=_cparams(("parallel", "parallel")),
        name="layout_w_in",
    )(w_rows)


def _layout_cmp(cmp_w, cmp_pe):
    L = cmp_w.shape[0]
    wk = cmp_w[:, 0]
    wv = cmp_w[:, 1]
    zero = jnp.zeros_like(wk)
    blk = jnp.concatenate([jnp.concatenate([wk, zero], axis=3), jnp.concatenate([zero, wv], axis=3)], axis=2)
    pe2 = jnp.concatenate([cmp_pe, cmp_pe], axis=2)[:, :, None, :]
    s = NSA_CMP_STRIDE
    return blk[:, :s].astype(BF16), blk[:, s:].astype(BF16), pe2[:, :s], pe2[:, s:]


def kernel(x, mem, ln_mix_pre, w_in, nsa_cmp_w, nsa_cmp_pe, rel_bias, ret_norm_g, rwkv_mu, rwkv_w0, rwkv_w2, rwkv_a0, rwkv_a2, rwkv_g2, rwkv_k_k, rwkv_k_a, rwkv_r_k, rwkv_ln_g, rwkv_ln_b, conv_w, w_branch, w_mix_out, ln_mix_post, ln_xa_pre, ln_mem, xa_wq, xa_wkv, xa_wo, ln_xa_post, ln_mlp_pre, mlp_w1, mlp_w2, ln_mlp_post):
    B, S, D = x.shape
    M = mem.shape[1]
    depth = w_in.shape[0]
    row = lambda g: g[:, None, :]

    w_in_b, w_gate_b = _layout_w_in(w_in)
    cmp_lo, cmp_hi, pe_lo, pe_hi = _layout_cmp(nsa_cmp_w, nsa_cmp_pe)
    nsa_tabs = _nsa_tables(rel_bias, S)
    ret_tabs = _ret_tables(S)
    rwkv_consts = _rwkv_consts()
    mu = jnp.pad(rwkv_mu, ((0, 0), (0, 1024 - RWKV_COLS)))[:, None, :]
    lora_w = _hi_lo(jnp.concatenate([_pad_rows(rwkv_w2, 0, 128), _pad_rows(rwkv_a2, RWKV_DECAY_LORA, 128),
                                     _pad_rows(rwkv_g2, RWKV_DECAY_LORA + RWKV_AAA_LORA, 128)], axis=2))
    lora_w = lora_w.reshape(depth, 256, 3 * MIX_WIDTH)
    vec = jnp.stack([rwkv_w0, rwkv_a0, rwkv_k_k, rwkv_k_a, rwkv_r_k, rwkv_ln_g, rwkv_ln_b,
                     jnp.zeros_like(rwkv_w0)], axis=1)
    conv_p = jnp.pad(conv_w, ((0, 0), (0, 8 - CONV_WIDTH), (0, 0)))
    w_branch_b = w_branch.astype(BF16)
    w_mix_out_b = w_mix_out.astype(BF16)
    xa_wq_b = xa_wq.astype(BF16)
    xa_wkv_b = xa_wkv.astype(BF16)
    xa_wo_b = xa_wo.astype(BF16)
    mlp_w1_b = mlp_w1.astype(BF16)
    mlp_w2_b = mlp_w2.astype(BF16)

    xf = x.reshape(B * S, D)
    memf = mem.reshape(B * M, D)
    for l in range(depth):
        z = norm_matmul(xf, row(ln_mix_pre), w_in_b, l, tm=512, tn=Z_COLS, out_dtype=F32)
        prep = nsa_compress(z, cmp_lo, cmp_hi, pe_lo, pe_hi, l, B, S)
        z3 = z.reshape(B, S, Z_COLS)
        o_nsa = nsa_attention(z3, prep, *nsa_tabs, B, S, G=8).reshape(B * S, MIX_WIDTH)
        o_ret = retention(z3, ret_tabs, row(ret_norm_g), l, B, S, G=8).reshape(B * S, MIX_WIDTH)
        o_rwkv = rwkv7(z3, mu, lora_w, vec, rwkv_consts, l, B, S, G=8).reshape(B * S, MIX_WIDTH)
        xf = merge_mix(xf, z, o_nsa, o_ret, o_rwkv, conv_p, row(ln_mix_pre), w_gate_b, w_branch_b,
                       w_mix_out_b, row(ln_mix_post), l, B, S, tm=512)
        kvm = norm_matmul(memf, row(ln_mem), xa_wkv_b, l, tm=2 * M, tn=2 * D_MODEL, out_dtype=BF16)
        xf = cross_attention(xf, kvm, row(ln_xa_pre), xa_wq_b, xa_wo_b, row(ln_xa_post), l, B, S, M, tm=1024)
        xf = mlp(xf, row(ln_mlp_pre), mlp_w1_b, mlp_w2_b, row(ln_mlp_post), l, tm=512)
    return xf.reshape(B, S, D)
```

```python
import functools
import math

import numpy as np
import jax
import jax.numpy as jnp
from jax import lax
from jax.experimental import pallas as pl
from jax.experimental.pallas import tpu as pltpu

F32 = jnp.float32
BF16 = jnp.bfloat16

D_MODEL = 1024
N_BRANCHES = 4
MIX_WIDTH = 256
HEAD_DIM = 64
MIX_HEADS = 4

NSA_CMP_LEN = 32
NSA_CMP_STRIDE = 16
NSA_SLC_LEN = 64
NSA_TOP_N = 8
NSA_WINDOW = 512
Q_BLOCK = 128
NSA_V_ROWS = HEAD_DIM + 16
FORCE_BONUS = 1e4
N_BUCKETS = 32
MAX_DISTANCE = 128

RET_CHUNK = 128
ROPE_BASE = 10000.0
RET_NORM_EPS = 1e-5

RWKV_DECAY_LORA = 32
RWKV_AAA_LORA = 32
RWKV_GATE_LORA = 64
RWKV_GN_EPS = 64e-5
RWKV_COLS = 3 * MIX_WIDTH + RWKV_DECAY_LORA + RWKV_AAA_LORA + RWKV_GATE_LORA
RWKV_CHUNK = 64

CONV_WIDTH = 3
XA_HEADS = 4
XA_HEAD_DIM = D_MODEL // XA_HEADS
D_FF = 4 * D_MODEL

RMS_EPS = 1e-6
LOG2E = math.log2(math.e)
NEG_INF = -1e30
NEG_BIG = -3e38

OFF_CONV = 0
OFF_NSAQ = 768
OFF_RET = 1024
OFF_RWKV = 2048
OFF_NSAKV = 3072
OFF_NSAG = 3456
Z_COLS = 3584

VMEM_LIMIT = 56 * 1024 * 1024


def _tiles(B, S, M):
    rows = B * S
    return dict(
        in_proj_rows=min(512, rows),
        merge_rows=min(512, S),
        xattn_rows=min(1024, S),
        mlp_rows=min(512, rows),
        mem_rows=min(2 * M, B * M),
        mixer_seqs=min(8, B),
        w_layout_rows=256,
        w_layout_cols=256,
    )


def _cparams(sem):
    return pltpu.CompilerParams(dimension_semantics=sem, vmem_limit_bytes=VMEM_LIMIT)


def _dot(a, b):
    return jnp.dot(a, b, preferred_element_type=F32)


def _dot_nt(a, b):
    return lax.dot_general(a, b, (((1,), (1,)), ((), ())), preferred_element_type=F32)


def _dot_tn(a, b):
    return lax.dot_general(a, b, (((0,), (0,)), ((), ())), preferred_element_type=F32)


def _split2(x):
    hi = x.astype(BF16)
    lo = (x - hi.astype(F32)).astype(BF16)
    return hi, lo


def _dot_x2(x, w_bf16, *more):
    xs = (x,) + more
    rows = x.shape[0]
    parts = [p for xi in xs for p in _split2(xi)]
    out = _dot(jnp.concatenate(parts, axis=0), w_bf16)
    res = [out[2 * i * rows:(2 * i + 1) * rows] + out[(2 * i + 1) * rows:(2 * i + 2) * rows]
           for i in range(len(xs))]
    return res[0] if not more else res


def _rms(x, g):
    ms = jnp.mean(x * x, axis=-1, keepdims=True)
    return x * lax.rsqrt(ms + RMS_EPS) * g


def _norm_matmul_kernel(x_ref, g_ref, w_ref, o_ref, h_ref):
    @pl.when(pl.program_id(1) == 0)
    def _():
        h_ref[...] = _rms(x_ref[...], g_ref[...]).astype(BF16)

    o_ref[...] = _dot(h_ref[...], w_ref[...]).astype(o_ref.dtype)


def norm_matmul(x, g, w, layer, tm, tn, out_dtype):
    M, D = x.shape
    N = w.shape[2]
    w_mode = dict(pipeline_mode=pl.Buffered(1)) if tn == N else {}
    return pl.pallas_call(
        _norm_matmul_kernel,
        grid=(M // tm, N // tn),
        in_specs=[
            pl.BlockSpec((tm, D), lambda i, j: (i, 0)),
            pl.BlockSpec((None, 1, D), lambda i, j: (layer, 0, 0)),
            pl.BlockSpec((None, D, tn), lambda i, j: (layer, 0, j), **w_mode),
        ],
        out_specs=pl.BlockSpec((tm, tn), lambda i, j: (i, j)),
        out_shape=jax.ShapeDtypeStruct((M, N), out_dtype),
        scratch_shapes=[pltpu.VMEM((tm, D), BF16)],
        compiler_params=_cparams(("parallel", "arbitrary")),
        name="norm_matmul",
    )(x, g, w)


def _nsa_compress_kernel(kvc_ref, kvs_ref, kvw_ref, wlo_ref, whi_ref, pelo_ref, pehi_ref,
                         kc_ref, vct_ref, ks_ref, vst_ref, kw_ref, vwt_ref):
    nblk = kc_ref.shape[0]
    n_kt = vwt_ref.shape[0]
    ylo = jnp.zeros((nblk, 128), F32)
    yhi = jnp.zeros((nblk, 128), F32)
    for r in range(NSA_CMP_STRIDE):
        xr = kvc_ref[pl.ds(r, nblk, stride=NSA_CMP_STRIDE), :]
        ylo = ylo + _dot((xr + pelo_ref[r]).astype(BF16), wlo_ref[r])
        yhi = yhi + _dot((xr + pehi_ref[r]).astype(BF16), whi_ref[r])
    y = ylo + pltpu.roll(yhi, nblk - 1, axis=0)
    kc_ref[...] = y.astype(BF16)
    vct_ref[...] = y.T[HEAD_DIM:, :].astype(BF16)
    kw_ref[...] = kvw_ref[...].astype(BF16)
    ones = jnp.ones((NSA_V_ROWS - HEAD_DIM, 128), BF16)
    for kt in range(n_kt):
        rs = slice(kt * 128, (kt + 1) * 128)
        vst_ref[kt] = jnp.concatenate([kvs_ref[rs, :].T[HEAD_DIM:, :].astype(BF16), ones], axis=0)
        vwt_ref[kt] = jnp.concatenate([kvw_ref[rs, :].T[HEAD_DIM:, :].astype(BF16), ones], axis=0)
    S = kvs_ref.shape[0]
    key = lax.broadcasted_iota(jnp.int32, (S, 128), 0)
    lane = lax.broadcasted_iota(jnp.int32, (S, 128), 1)
    onehot = (lane == HEAD_DIM + key // NSA_SLC_LEN).astype(F32)
    ks_ref[0:S, :] = jnp.where(lane < HEAD_DIM, kvs_ref[...], onehot).astype(BF16)
    pad_lane = lax.broadcasted_iota(jnp.int32, (128, 128), 1)
    ks_ref[S:S + 128, :] = (pad_lane == HEAD_DIM + S // NSA_SLC_LEN).astype(BF16)
    vst_ref[n_kt] = jnp.zeros((NSA_V_ROWS, 128), BF16)


def nsa_compress(z, wlo, whi, pelo, pehi, layer, B, S):
    nblk = S // NSA_CMP_STRIDE
    n_kt = S // 128
    c0 = OFF_NSAKV // 128
    wmap = lambda b: (layer, 0, 0, 0)
    b3 = lambda b: (b, 0, 0)
    b4 = lambda b: (b, 0, 0, 0)
    return pl.pallas_call(
        _nsa_compress_kernel,
        grid=(B,),
        in_specs=[
            pl.BlockSpec((S, 128), lambda b: (b, c0)),
            pl.BlockSpec((S, 128), lambda b: (b, c0 + 1)),
            pl.BlockSpec((S, 128), lambda b: (b, c0 + 2)),
            pl.BlockSpec((None, NSA_CMP_STRIDE, 128, 128), wmap),
            pl.BlockSpec((None, NSA_CMP_STRIDE, 128, 128), wmap),
            pl.BlockSpec((None, NSA_CMP_STRIDE, 1, 128), wmap),
            pl.BlockSpec((None, NSA_CMP_STRIDE, 1, 128), wmap),
        ],
        out_specs=[
            pl.BlockSpec((None, nblk, 128), b3),
            pl.BlockSpec((None, HEAD_DIM, nblk), b3),
            pl.BlockSpec((None, S + 128, 128), b3),
            pl.BlockSpec((None, n_kt + 1, NSA_V_ROWS, 128), b4),
            pl.BlockSpec((None, S, 128), b3),
            pl.BlockSpec((None, n_kt, NSA_V_ROWS, 128), b4),
        ],
        out_shape=[
            jax.ShapeDtypeStruct((B, nblk, 128), BF16),
            jax.ShapeDtypeStruct((B, HEAD_DIM, nblk), BF16),
            jax.ShapeDtypeStruct((B, S + 128, 128), BF16),
            jax.ShapeDtypeStruct((B, n_kt + 1, NSA_V_ROWS, 128), BF16),
            jax.ShapeDtypeStruct((B, S, 128), BF16),
            jax.ShapeDtypeStruct((B, n_kt, NSA_V_ROWS, 128), BF16),
        ],
        compiler_params=_cparams(("parallel",)),
        name="nsa_compress",
    )(z, z, z, wlo, whi, pelo, pehi)


def _rowmax(s):
    return jnp.max(s, axis=-1, keepdims=True)


def _rowsum(s):
    return jnp.sum(s, axis=-1, keepdims=True)


def _colmax(s):
    return jnp.max(s, axis=0, keepdims=True)


def _colsum(s):
    return jnp.sum(s, axis=0, keepdims=True)


def _nsa_kernel(q_ref, gl_ref, kc_ref, vct_ref, ks_ref, vst_ref, kw_ref, vwt_ref, bc_ref, tab_ref,
                ovt_ref, o_ref):
    bi = pl.program_id(1)
    G = q_ref.shape[0]
    n_kt = vwt_ref.shape[1]
    n_blk = ovt_ref.shape[0]
    n_cmp = kc_ref.shape[1]
    QB = Q_BLOCK
    R = MIX_HEADS * QB
    n_off = NSA_WINDOW // 128

    lo = jnp.maximum(bi - n_off, 0)
    q4 = [None] * G
    q4_sel = [None] * G
    o_c = [None] * G
    carry_s = [None] * G
    carry_w = [None] * G

    def key_tile(kref, g, kt, lanes):
        return kref[g, pl.ds(pl.multiple_of(kt * 128, 128), 128), lanes]

    def pair_qk(g, k0, k1):
        keys = jnp.concatenate([key_tile(kw_ref, g, k0, slice(0, HEAD_DIM)),
                                key_tile(kw_ref, g, k1, slice(0, HEAD_DIM))], axis=0)
        qk = _dot(keys, q4[g])
        return qk[:128], qk[128:]

    def pair_qk_sel(g, k0, k1):
        keys = jnp.concatenate([key_tile(ks_ref, g, k0, slice(None)), key_tile(ks_ref, g, k1, slice(None))],
                               axis=0)
        qk = _dot(keys, q4_sel[g])
        return qk[:128], qk[128:]

    def head_cols(x, h):
        return x[:, h * QB:(h + 1) * QB]

    key1 = lax.broadcasted_iota(jnp.int32, (128, QB), 0)
    i1 = lax.broadcasted_iota(jnp.int32, (128, QB), 1)
    causal1 = key1 <= i1
    kp = jnp.maximum(bi - 1, 0)
    pen_prev = jnp.where(bi >= 1, 0.0, NEG_INF)

    def win_add(kt, valid):
        shift = jnp.where(valid, jnp.where(bi - kt < n_off, -4096, 0), 4096)
        return jnp.where(key1 > i1 + shift, 0.0, NEG_INF)

    def diag_add(extra):
        return [jnp.where(causal1, head_cols(tab_ref[0], h) + extra, NEG_INF) for h in range(MIX_HEADS)]

    def prev_add(extra):
        return [head_cols(tab_ref[1], h) + extra for h in range(MIX_HEADS)]

    def pair_update(carry, qk0, add0, vt0, qk1, add1, vt1):
        p0s, p1s, ms, alphas = [], [], [], []
        for h in range(MIX_HEADS):
            a0 = head_cols(qk0, h) if add0 is None else head_cols(qk0, h) + add0[h]
            a1 = head_cols(qk1, h) if add1 is None else head_cols(qk1, h) + add1[h]
            m_new = jnp.maximum(_colmax(a0), _colmax(a1))
            if carry is not None:
                m_old = head_cols(carry[0], h)
                m_new = jnp.maximum(m_old, m_new)
                alphas.append(jnp.exp2(m_old - m_new))
            ms.append(m_new)
            p0s.append(jnp.exp2(a0 - m_new).astype(BF16))
            p1s.append(jnp.exp2(a1 - m_new).astype(BF16))
        yield
        probs = jnp.concatenate([jnp.concatenate(p0s, axis=1), jnp.concatenate(p1s, axis=1)], axis=0)
        acc = _dot(jnp.concatenate([vt0, vt1], axis=1), probs)
        if carry is not None:
            acc = jnp.concatenate(alphas, axis=1) * carry[1] + acc
        return jnp.concatenate(ms, axis=1), acc

    def head_part(g):
        qt = (q_ref[g] * (HEAD_DIM ** -0.5 * LOG2E)).T
        q4[g] = jnp.concatenate([qt[h * HEAD_DIM:(h + 1) * HEAD_DIM] for h in range(MIX_HEADS)],
                                axis=1).astype(BF16)
        yield
        qk_d, qk_p = pair_qk(g, bi, kp)
        carry_w[g] = yield from pair_update(None, qk_d, diag_add(0.0), vwt_ref[g, bi],
                                            qk_p, prev_add(pen_prev), vwt_ref[g, kp])
        yield
        g0 = pl.multiple_of((n_kt - 1 - bi) * 8, 8)
        qk = _dot(kc_ref[g, :, :HEAD_DIM], q4[g])
        nrow = lax.broadcasted_iota(jnp.int32, (n_cmp, QB), 0)
        tq = bi * QB + lax.broadcasted_iota(jnp.int32, (n_cmp, QB), 1)
        vis = tq - (nrow * NSA_CMP_STRIDE + (NSA_CMP_LEN - 1)) >= 0
        live = (tq >= NSA_CMP_LEN - 1).astype(F32)
        yield
        p_cs = []
        for h in range(MIX_HEADS):
            s = jnp.where(vis, head_cols(qk, h) + bc_ref[pl.ds(g0, n_cmp), h * QB:(h + 1) * QB], NEG_INF)
            e = jnp.exp2(s - _colmax(s))
            p_cs.append(e / _colsum(e) * live)
        yield
        o_c[g] = _dot(vct_ref[g], jnp.concatenate(p_cs, axis=1).astype(BF16))
        p_sum = p_cs[0] + p_cs[1] + p_cs[2] + p_cs[3]
        p_hi, p_lo = _split2(p_sum)
        imp = _dot(ovt_ref[...], p_hi) + _dot(ovt_ref[...], p_lo)
        yield
        blk = lax.broadcasted_iota(jnp.int32, (n_blk, QB), 0)
        cur = (bi * QB + lax.broadcasted_iota(jnp.int32, (n_blk, QB), 1)) >> 6
        forced = (blk == 0) | (blk == cur) | (blk == cur - 1)
        imp = jnp.where(forced, imp + FORCE_BONUS, imp)
        imp = jnp.where(blk <= cur, imp, NEG_INF)
        blk_f = blk.astype(F32)
        sel = jnp.zeros((n_blk, QB), F32)
        for _ in range(NSA_TOP_N):
            mx = _colmax(imp)
            idx = jnp.min(jnp.where(imp == mx, blk_f, 4096.0), axis=0, keepdims=True)
            pick = blk_f == idx
            sel = jnp.where(pick, 1.0, sel)
            imp = jnp.where(pick, NEG_BIG, imp)
            yield
        sel_add = (sel - 1.0) * (-NEG_INF)
        rows_left = 128 - HEAD_DIM - n_blk
        pad_rows = jnp.where(lax.broadcasted_iota(jnp.int32, (rows_left, QB), 0) == 0, NEG_INF, 0.0)
        mask_rows = jnp.concatenate([sel_add, pad_rows], axis=0).astype(BF16)
        q4_sel[g] = jnp.concatenate([q4[g], jnp.concatenate([mask_rows] * MIX_HEADS, axis=1)], axis=0)
        yield
        qk_d, qk_p = pair_qk_sel(g, bi, kp)
        carry_s[g] = yield from pair_update(None, qk_d, diag_add(0.0), vst_ref[g, bi],
                                            qk_p, prev_add(pen_prev), vst_ref[g, kp])

    _round_robin([head_part(g) for g in range(G)])

    n_old = jnp.maximum(bi - 1, 0)

    def sel_step(g, j, carry, out):
        k0 = 2 * j
        k1 = jnp.where(k0 + 1 < n_old, k0 + 1, n_kt)
        qk0, qk1 = pair_qk_sel(g, k0, k1)
        yield
        out[g] = yield from pair_update(carry, qk0, None, vst_ref[g, k0], qk1, None, vst_ref[g, k1])

    def win_step(g, j, carry, out):
        k0 = lo + 2 * j
        k1 = jnp.minimum(k0 + 1, bi)
        qk0, qk1 = pair_qk(g, k0, k1)
        yield
        out[g] = yield from pair_update(carry, qk0, [win_add(k0, True)] * MIX_HEADS, vwt_ref[g, k0],
                                        qk1, [win_add(k1, k0 + 1 < n_old)] * MIX_HEADS, vwt_ref[g, k1])

    def joint(step):
        def body(j, carries):
            out = [None] * G
            _round_robin([step(g, j, carries[g], out) for g in range(G)])
            return tuple(out)
        return body

    res_s = lax.fori_loop(0, (n_old + 1) // 2, joint(sel_step), tuple(carry_s))
    res_w = lax.fori_loop(0, (n_old - lo + 1) // 2, joint(win_step), tuple(carry_w))

    def tail_part(g):
        acc_s = res_s[g][1]
        acc_w = res_w[g][1]
        o_s = acc_s[:HEAD_DIM] * (1.0 / acc_s[HEAD_DIM:HEAD_DIM + 1])
        o_w = acc_w[:HEAD_DIM] * (1.0 / acc_w[HEAD_DIM:HEAD_DIM + 1])
        gate = jax.nn.sigmoid(gl_ref[g].T)
        yield
        outs = []
        for h in range(MIX_HEADS):
            cs = slice(h * QB, (h + 1) * QB)
            outs.append(gate[3 * h:3 * h + 1] * o_c[g][:, cs] + gate[3 * h + 1:3 * h + 2] * o_s[:, cs]
                        + gate[3 * h + 2:3 * h + 3] * o_w[:, cs])
        o_ref[g] = jnp.concatenate(outs, axis=0).T

    _round_robin([tail_part(g) for g in range(G)])


def nsa_attention(z, prep, bias_c, tab, ovt, B, S, G):
    n_qb = S // Q_BLOCK
    n_kt = S // 128
    n_cmp = S // NSA_CMP_STRIDE
    R = MIX_HEADS * Q_BLOCK
    kc, vct, ks, vst, kw, vwt = prep
    b3 = lambda b, i: (b, 0, 0)
    b4 = lambda b, i: (b, 0, 0, 0)
    return pl.pallas_call(
        _nsa_kernel,
        grid=(B // G, n_qb),
        in_specs=[
            pl.BlockSpec((G, Q_BLOCK, MIX_WIDTH), lambda b, i: (b, i, OFF_NSAQ // MIX_WIDTH)),
            pl.BlockSpec((G, Q_BLOCK, 128), lambda b, i: (b, i, OFF_NSAG // 128)),
            pl.BlockSpec((G, n_cmp, 128), b3),
            pl.BlockSpec((G, HEAD_DIM, n_cmp), b3),
            pl.BlockSpec((G, S + 128, 128), b3),
            pl.BlockSpec((G, n_kt + 1, NSA_V_ROWS, 128), b4),
            pl.BlockSpec((G, S, 128), b3),
            pl.BlockSpec((G, n_kt, NSA_V_ROWS, 128), b4),
            pl.BlockSpec(bias_c.shape, lambda b, i: (0, 0)),
            pl.BlockSpec((2, 128, R), lambda b, i: (0, 0, 0)),
            pl.BlockSpec(ovt.shape, lambda b, i: (0, 0)),
        ],
        out_specs=pl.BlockSpec((G, Q_BLOCK, MIX_WIDTH), lambda b, i: (b, i, 0)),
        out_shape=jax.ShapeDtypeStruct((B, S, MIX_WIDTH), F32),
        compiler_params=_cparams(("parallel", "arbitrary")),
        name="nsa_attention",
    )(z, z, kc, vct, ks, vst, kw, vwt, bias_c, tab, ovt)


def _stack_heads(x, head_of_lane):
    return jnp.concatenate([jnp.where(head_of_lane == h, x, jnp.zeros_like(x))
                            for h in range(MIX_HEADS)], axis=0)


def _head_norm(y, seg_mean, eps):
    mu = _dot_x2(y, seg_mean)
    d = y - mu
    var = _dot_x2(d * d, seg_mean)
    return d * lax.rsqrt(var + eps)


def _round_robin(chains):
    while chains:
        chains = [c for c in chains if next(c, StopIteration) is not StopIteration]


def _retention_kernel(z_ref, cos_ref, sin_ref, indec_ref, qdec_ref, kdec_ref, cdec_ref, bd_ref,
                      segm_ref, ng_ref, o_ref, st_ref):
    @pl.when(pl.program_id(1) == 0)
    def _():
        st_ref[...] = jnp.zeros_like(st_ref)

    _round_robin([_retention_chunk(z_ref.at[g], cos_ref, sin_ref, indec_ref, qdec_ref, kdec_ref, cdec_ref,
                                   bd_ref, segm_ref, ng_ref, o_ref.at[g], st_ref.at[g])
                  for g in range(z_ref.shape[0])])


def _retention_chunk(z_ref, cos_ref, sin_ref, indec_ref, qdec_ref, kdec_ref, cdec_ref, bd_ref,
                     segm_ref, ng_ref, o_ref, st_ref):
    L = RET_CHUNK
    z = z_ref[...]
    q = z[:, 0:256]
    k = z[:, 256:512]
    v = z[:, 512:768]
    g = z[:, 768:1024]
    cos = cos_ref[...]
    sin = sin_ref[...]

    def rot(u):
        u1 = u[:, :128]
        u2 = u[:, 128:]
        return jnp.concatenate([u1 * cos - u2 * sin, u2 * cos + u1 * sin], axis=1)

    qr = rot(q) * (HEAD_DIM ** -0.5)
    kr = rot(k)
    lane = lax.broadcasted_iota(jnp.int32, (L, 256), 1)
    head_qk = (lane & 127) >> 5
    head_v = lane >> 6
    qb = qr.astype(BF16)
    kb = kr.astype(BF16)
    vb = v.astype(BF16)
    yield

    att = _dot_nt(_stack_heads(qb, head_qk), kb) * indec_ref[...]
    yield
    o_st = _dot(att.astype(BF16), vb)
    o = jnp.zeros((L, 256), F32)
    for h in range(MIX_HEADS):
        o = o + jnp.where(head_v == h, o_st[h * L:(h + 1) * L], 0.0)
    yield
    state = st_ref[...]
    o = o + _dot(qb, state.astype(BF16)) * qdec_ref[...]
    st_ref[...] = state * cdec_ref[...] + _dot_tn((kr * kdec_ref[...]).astype(BF16), vb) * bd_ref[...]
    yield

    mu = _dot_x2(o, segm_ref[...])
    yield
    d = o - mu
    var = _dot_x2(d * d, segm_ref[...])
    yield
    o_ref[...] = g * jax.nn.sigmoid(g) * (d * lax.rsqrt(var + RET_NORM_EPS) * ng_ref[...])


def retention(z, tabs, ng, layer, B, S, G):
    L = RET_CHUNK
    nc = S // L
    cos, sin, indec, qdec, kdec, cdec, bd, segm = tabs
    const2 = lambda b, c: (0, 0)
    return pl.pallas_call(
        _retention_kernel,
        grid=(B // G, nc),
        in_specs=[
            pl.BlockSpec((G, L, 1024), lambda b, c: (b, c, OFF_RET // 1024)),
            pl.BlockSpec((L, 128), lambda b, c: (c, 0)),
            pl.BlockSpec((L, 128), lambda b, c: (c, 0)),
            pl.BlockSpec((MIX_HEADS * L, L), const2),
            pl.BlockSpec((L, 256), const2),
            pl.BlockSpec((L, 256), const2),
            pl.BlockSpec((1, 256), const2),
            pl.BlockSpec((256, 256), const2),
            pl.BlockSpec((256, 256), const2),
            pl.BlockSpec((None, 1, 256), lambda b, c: (layer, 0, 0)),
        ],
        out_specs=pl.BlockSpec((G, L, 256), lambda b, c: (b, c, 0)),
        out_shape=jax.ShapeDtypeStruct((B, S, 256), F32),
        scratch_shapes=[pltpu.VMEM((G, 256, 256), F32)],
        compiler_params=_cparams(("parallel", "arbitrary")),
        name="retention",
    )(z, cos, sin, indec, qdec, kdec, cdec, bd, segm, ng)


def _rwkv_kernel(z_ref, mu_ref, lw_ref, vec_ref, tri_ref, bdl_ref, wide_ref,
                 segs_ref, segm_ref, o_ref, st_ref, prev_ref):
    @pl.when(pl.program_id(1) == 0)
    def _():
        st_ref[...] = jnp.zeros_like(st_ref)
        prev_ref[...] = jnp.zeros_like(prev_ref)

    _round_robin([_rwkv_chunk(z_ref.at[g], mu_ref, lw_ref, vec_ref, tri_ref, bdl_ref,
                              wide_ref, segs_ref, segm_ref, o_ref.at[g], st_ref.at[g], prev_ref.at[g])
                  for g in range(z_ref.shape[0])])


def _rwkv_chunk(z_ref, mu_ref, lw_ref, vec_ref, tri_ref, bdl_ref, wide_ref,
                segs_ref, segm_ref, o_ref, st_ref, prev_ref):
    C = RWKV_CHUNK
    z = z_ref[...]
    rows = lax.broadcasted_iota(jnp.int32, (C, 1024), 0)
    zs = jnp.where(rows == 0, prev_ref[0:1, :], pltpu.roll(z, 1, axis=0))
    prev_ref[...] = jnp.broadcast_to(z[C - 1:C, :], prev_ref.shape)
    zf = z + (zs - z) * mu_ref[...]
    r = zf[:, 0:256]
    k = zf[:, 256:512]
    v = zf[:, 512:768]
    lora = zf[:, 768:896]

    vec = vec_ref[...]
    w0, a0, k_k, k_a, r_k, ln_g, ln_b = (vec[i:i + 1] for i in range(7))

    lane = lax.broadcasted_iota(jnp.int32, lora.shape, 1)
    lora_in = jnp.where(lane < RWKV_DECAY_LORA, jnp.tanh(lora),
                        jnp.where(lane < RWKV_DECAY_LORA + RWKV_AAA_LORA, lora, jax.nn.sigmoid(lora)))
    hi, lo = _split2(lora_in)
    lhs = jnp.concatenate([jnp.concatenate([hi, hi], axis=1),
                           jnp.concatenate([lo, jnp.zeros_like(lo)], axis=1)], axis=0)
    both = _dot(lhs, lw_ref[...])
    lora_out = both[:C] + both[C:]
    wpre = w0 + lora_out[:, 0:256]
    y = -wpre
    softplus = jnp.maximum(y, 0.0) + jnp.log(1.0 + jnp.exp(-jnp.abs(y)))
    w_log = -softplus - 0.5
    ld = -jnp.exp(w_log)
    a = jax.nn.sigmoid(a0 + lora_out[:, 256:512])
    gate = lora_out[:, 512:768]
    kk = k * k_k
    k2 = k * (1.0 + (a - 1.0) * k_a)
    kk_sq, bonus = _dot_x2(kk * kk, segs_ref[...], r * k2 * r_k)
    kk = kk / jnp.maximum(jnp.sqrt(kk_sq), 1e-12)
    yield

    tri = tri_ref[...]
    l1 = ld.astype(BF16)
    r1 = ld - l1.astype(F32)
    l2 = r1.astype(BF16)
    l3 = (r1 - l2.astype(F32)).astype(BF16)
    cs = _dot(jnp.concatenate([tri, tri, tri], axis=1), jnp.concatenate([l1, l2, l3], axis=0))
    cs_end = cs[C - 1:C, :]
    yield
    e_neg = jnp.exp(-cs)
    e_end = jnp.exp(cs_end - cs)
    kka = kk * a
    a_t = -kk * jnp.exp(cs - ld)
    r_t = r * jnp.exp(cs)
    b_t = kka * e_neg
    k_t = k2 * e_neg
    b_g = kka * e_end
    k_g = k2 * e_end

    lane = lax.broadcasted_iota(jnp.int32, (C, 256), 1)
    head = lane >> 6
    stack = lambda x: _stack_heads(x.astype(BF16), head)
    a_st, b_st, k_st, v_st = stack(a_t), stack(b_t), stack(k_t), stack(v)

    n_pow = _dot_nt(a_st, b_st) * bdl_ref[...]
    ri = lax.broadcasted_iota(jnp.int32, (4 * C, 4 * C), 0)
    ci = lax.broadcasted_iota(jnp.int32, (4 * C, 4 * C), 1)
    t_inv = jnp.where(ri == ci, 1.0, 0.0) + n_pow
    yield

    ar = jnp.concatenate([a_t, r_t], axis=0).astype(BF16)
    bk = jnp.concatenate([b_st, k_st], axis=0)
    wide = _dot_nt(ar, bk)
    strict = wide_ref[0]
    incl = wide_ref[1]
    a_ak = (wide[0:C, 4 * C:] * strict).astype(BF16)
    a_rb = (wide[C:, 0:4 * C] * incl).astype(BF16)
    a_rk = (wide[C:, 4 * C:] * incl).astype(BF16)
    yield

    sq = 2
    while sq < C:
        nb = n_pow.astype(BF16)
        n_pow = _dot(nb, nb)
        yield
        t_inv = t_inv + _dot(t_inv.astype(BF16), n_pow.astype(BF16))
        sq *= 2
    yield

    state = st_ref[...]
    sb = state.astype(BF16)
    from_state = _dot_nt(ar, sb)
    x = from_state[0:C] + _dot(a_ak, v_st)
    yield
    u_st = _dot(t_inv.astype(BF16), stack(x)).astype(BF16)
    yield
    uv = jnp.concatenate([u_st, v_st], axis=0)
    yv = from_state[C:] + _dot(jnp.concatenate([a_rb, a_rk], axis=1), uv)
    bkg = jnp.concatenate([stack(b_g), stack(k_g)], axis=0)
    st_ref[...] = state * jnp.exp(cs_end) + _dot_tn(uv, bkg)
    yield

    yn = _head_norm(yv, segm_ref[...], RWKV_GN_EPS) * ln_g + ln_b
    yn = yn + bonus * v
    o_ref[...] = yn * gate


def rwkv7(z, mu, lora_w, vec, consts, layer, B, S, G):
    C = RWKV_CHUNK
    nc = S // C
    tri, bdl, wide, segs, segm = consts
    const2 = lambda b, c: (0, 0)
    return pl.pallas_call(
        _rwkv_kernel,
        grid=(B // G, nc),
        in_specs=[
            pl.BlockSpec((G, C, 1024), lambda b, c: (b, c, OFF_RWKV // 1024)),
            pl.BlockSpec((None, 1, 1024), lambda b, c: (layer, 0, 0)),
            pl.BlockSpec((None, 256, 768), lambda b, c: (layer, 0, 0)),
            pl.BlockSpec((None, 8, 256), lambda b, c: (layer, 0, 0)),
            pl.BlockSpec((C, C), const2),
            pl.BlockSpec((4 * C, 4 * C), const2),
            pl.BlockSpec((2, C, 4 * C), lambda b, c: (0, 0, 0)),
            pl.BlockSpec((256, 256), const2),
            pl.BlockSpec((256, 256), const2),
        ],
        out_specs=pl.BlockSpec((G, C, 256), lambda b, c: (b, c, 0)),
        out_shape=jax.ShapeDtypeStruct((B, S, 256), F32),
        scratch_shapes=[pltpu.VMEM((G, 256, 256), F32), pltpu.VMEM((G, 8, 1024), F32)],
        compiler_params=_cparams(("parallel", "arbitrary")),
        name="rwkv7",
    )(z, mu, lora_w, vec, tri, bdl, wide, segs, segm)


def _merge_kernel(x_ref, zc_ref, on_ref, or_ref, ow_ref, cw_ref, gpre_ref, wg_ref, wb_ref, wo_ref, g_ref,
                  o_ref, carry_ref):
    tm = x_ref.shape[0]
    x = x_ref[...]
    h = _rms(x, gpre_ref[...]).astype(BF16)

    @pl.when(pl.program_id(1) == 0)
    def _():
        carry_ref[...] = jnp.zeros_like(carry_ref)

    zc = zc_ref[...]
    b_g = zc[:, 0:256]
    u = zc[:, 256:512] * zc[:, 512:768]
    prev = carry_ref[...]
    rows = lax.broadcasted_iota(jnp.int32, (tm, 256), 0)
    u1 = jnp.where(rows == 0, prev[7:8], pltpu.roll(u, 1, axis=0))
    u2 = jnp.where(rows == 0, prev[6:7], jnp.where(rows == 1, prev[7:8], pltpu.roll(u, 2, axis=0)))
    carry_ref[...] = u[tm - 8:tm]
    cw = cw_ref[...]
    o_conv = b_g * (cw[0:1] * u2 + cw[1:2] * u1 + cw[2:3] * u)

    branches = (on_ref[...], or_ref[...], ow_ref[...], o_conv)
    merged = jnp.zeros((tm, D_MODEL), F32)
    for m in range(N_BRANCHES):
        gm = jax.nn.sigmoid(_dot(h, wg_ref[:, m * D_MODEL:(m + 1) * D_MODEL]))
        merged = merged + gm * _dot(branches[m].astype(BF16), wb_ref[m])
    y = _dot(merged.astype(BF16), wo_ref[...])
    o_ref[...] = x + _rms(y, g_ref[...])


def merge_mix(x, z, o_nsa, o_ret, o_rwkv, conv_w, g_pre, w_gate, w_branch, w_out, g_post, layer, B, S, tm):
    nt = S // tm
    rowmap = lambda b, i: (b * nt + i, 0)
    lmap = lambda b, i: (layer, 0, 0)
    return pl.pallas_call(
        _merge_kernel,
        grid=(B, nt),
        in_specs=[
            pl.BlockSpec((tm, D_MODEL), rowmap),
            pl.BlockSpec((tm, 768), lambda b, i: (b * nt + i, OFF_CONV // 768)),
            pl.BlockSpec((tm, 256), rowmap),
            pl.BlockSpec((tm, 256), rowmap),
            pl.BlockSpec((tm, 256), rowmap),
            pl.BlockSpec((None, 8, 256), lmap),
            pl.BlockSpec((None, 1, D_MODEL), lmap),
            pl.BlockSpec((None, D_MODEL, N_BRANCHES * D_MODEL), lmap, pipeline_mode=pl.Buffered(1)),
            pl.BlockSpec((None, N_BRANCHES, MIX_WIDTH, D_MODEL), lambda b, i: (layer, 0, 0, 0),
                         pipeline_mode=pl.Buffered(1)),
            pl.BlockSpec((None, D_MODEL, D_MODEL), lmap, pipeline_mode=pl.Buffered(1)),
            pl.BlockSpec((None, 1, D_MODEL), lmap),
        ],
        out_specs=pl.BlockSpec((tm, D_MODEL), rowmap),
        out_shape=jax.ShapeDtypeStruct((B * S, D_MODEL), F32),
        scratch_shapes=[pltpu.VMEM((8, 256), F32)],
        compiler_params=_cparams(("parallel", "arbitrary")),
        name="merge_mix",
    )(x, z, o_nsa, o_ret, o_rwkv, conv_w, g_pre, w_gate, w_branch, w_out, g_post)


def _xattn_kernel(x_ref, kv_ref, gpre_ref, wq_ref, wo_ref, gpost_ref, o_ref):
    n_sub = 1
    rows = x_ref.shape[0] // n_sub

    def part(r0):
        rs = pl.ds(r0, rows)
        x = x_ref[rs, :]
        h = _rms(x, gpre_ref[...]).astype(BF16)
        yield
        q = (_dot(h, wq_ref[...]) * (XA_HEAD_DIM ** -0.5)).astype(BF16)
        yield
        outs = []
        for hd in range(XA_HEADS):
            cs = slice(hd * XA_HEAD_DIM, (hd + 1) * XA_HEAD_DIM)
            kh = kv_ref[:, hd * XA_HEAD_DIM:(hd + 1) * XA_HEAD_DIM]
            vh = kv_ref[:, D_MODEL + hd * XA_HEAD_DIM:D_MODEL + (hd + 1) * XA_HEAD_DIM]
            s = _dot_nt(q[:, cs], kh)
            yield
            e = jnp.exp(s - _rowmax(s))
            p = e / _rowsum(e)
            outs.append(_dot(p.astype(BF16), vh).astype(BF16))
            yield
        o = jnp.concatenate(outs, axis=1)
        y = _dot(o, wo_ref[...])
        yield
        o_ref[rs, :] = x + _rms(y, gpost_ref[...])

    _round_robin([part(i * rows) for i in range(n_sub)])


def cross_attention(x, kv, g_pre, wq, wo, g_post, layer, B, S, M, tm):
    nt = S // tm
    rowmap = lambda b, i: (b * nt + i, 0)
    lmap = lambda b, i: (layer, 0, 0)
    return pl.pallas_call(
        _xattn_kernel,
        grid=(B, nt),
        in_specs=[
            pl.BlockSpec((tm, D_MODEL), rowmap),
            pl.BlockSpec((M, 2 * D_MODEL), lambda b, i: (b, 0)),
            pl.BlockSpec((None, 1, D_MODEL), lmap),
            pl.BlockSpec((None, D_MODEL, D_MODEL), lmap),
            pl.BlockSpec((None, D_MODEL, D_MODEL), lmap),
            pl.BlockSpec((None, 1, D_MODEL), lmap),
        ],
        out_specs=pl.BlockSpec((tm, D_MODEL), rowmap),
        out_shape=jax.ShapeDtypeStruct((B * S, D_MODEL), F32),
        compiler_params=_cparams(("parallel", "arbitrary")),
        name="cross_attention",
    )(x, kv, g_pre, wq, wo, g_post)


def _mlp_kernel(x_ref, gpre_ref, w1_ref, w2_ref, gpost_ref, o_ref):
    x = x_ref[...]
    h = _rms(x, gpre_ref[...]).astype(BF16)
    tf = 1024
    acc = jnp.zeros(x.shape, F32)
    for f in range(D_FF // tf):
        a = jnp.maximum(_dot(h, w1_ref[:, f * tf:(f + 1) * tf]), 0.0)
        acc = acc + _dot((a * a).astype(BF16), w2_ref[f * tf:(f + 1) * tf, :])
    o_ref[...] = x + _rms(acc, gpost_ref[...])


def mlp(x, g_pre, w1, w2, g_post, layer, tm):
    M = x.shape[0]
    lmap = lambda i: (layer, 0, 0)
    return pl.pallas_call(
        _mlp_kernel,
        grid=(M // tm,),
        in_specs=[
            pl.BlockSpec((tm, D_MODEL), lambda i: (i, 0)),
            pl.BlockSpec((None, 1, D_MODEL), lmap),
            pl.BlockSpec((None, D_MODEL, D_FF), lmap, pipeline_mode=pl.Buffered(1)),
            pl.BlockSpec((None, D_FF, D_MODEL), lmap, pipeline_mode=pl.Buffered(1)),
            pl.BlockSpec((None, 1, D_MODEL), lmap),
        ],
        out_specs=pl.BlockSpec((tm, D_MODEL), lambda i: (i, 0)),
        out_shape=jax.ShapeDtypeStruct((M, D_MODEL), F32),
        compiler_params=_cparams(("parallel",)),
        name="mlp",
    )(x, g_pre, w1, w2, g_post)


def _t5_bucket(dist):
    n = jnp.maximum(dist, 0)
    max_exact = N_BUCKETS // 2
    nf = jnp.maximum(n, 1).astype(F32)
    large = max_exact + (jnp.log(nf / max_exact) / math.log(MAX_DISTANCE / max_exact)
                         * (N_BUCKETS - max_exact)).astype(jnp.int32)
    large = jnp.minimum(large, N_BUCKETS - 1)
    return jnp.where(n < max_exact, n, large)


def _nsa_tables(rel_bias, S):
    n_qb = S // Q_BLOCK
    n_cmp = S // NSA_CMP_STRIDE
    n_blk = S // NSA_SLC_LEN
    bias_f = rel_bias.astype(F32)
    i = jnp.arange(Q_BLOCK)
    per_qb = Q_BLOCK // NSA_CMP_STRIDE
    g = jnp.arange(n_cmp + per_qb * (n_qb - 1)) - per_qb * (n_qb - 1)
    d_c = i[None, :] - (g[:, None] * NSA_CMP_STRIDE + NSA_CMP_LEN - 1)
    def lookup(dist):
        hit = _t5_bucket(dist)[..., None, None] == jnp.arange(N_BUCKETS)[:, None]
        return jnp.sum(jnp.where(hit, bias_f, 0.0), axis=-2)

    bias_c = lookup(d_c).transpose(0, 2, 1).reshape(g.shape[0], MIX_HEADS * Q_BLOCK)
    d0 = i[:, None] - i[None, :]
    tab = jnp.stack([lookup(d0), lookup(d0 + Q_BLOCK)]) - bias_f[N_BUCKETS - 1]
    tab = tab.transpose(0, 2, 3, 1).reshape(2, Q_BLOCK, MIX_HEADS * Q_BLOCK)
    cs = np.arange(n_cmp)[None, :] * NSA_CMP_STRIDE
    ss = np.arange(n_blk)[:, None] * NSA_SLC_LEN
    ovt = np.clip(np.minimum(cs + NSA_CMP_LEN, ss + NSA_SLC_LEN) - np.maximum(cs, ss), 0, None)
    ovt = ovt.astype(np.float32) / NSA_CMP_LEN
    ovt[:, (S - NSA_CMP_LEN) // NSA_CMP_STRIDE + 1:] = 0.0
    return bias_c * LOG2E, tab * LOG2E, jnp.asarray(ovt, BF16)


def _ret_tables(S):
    L = RET_CHUNK
    H = MIX_HEADS
    half = HEAD_DIM // 2
    pos = jnp.arange(S)
    inv_freq = ROPE_BASE ** (-jnp.arange(half, dtype=F32) / half)
    ang = pos.astype(F32)[:, None] * inv_freq[None, :]
    cos = jnp.tile(jnp.cos(ang), (1, H))
    sin = jnp.tile(jnp.sin(ang), (1, H))
    lg = jnp.log(1.0 - 2.0 ** (-5.0 - jnp.arange(H, dtype=F32)))
    n = jnp.arange(L, dtype=F32)
    diff = n[:, None] - n[None, :]
    inner = jnp.where(diff >= 0, jnp.exp(jnp.maximum(diff, 0.0)[None] * lg[:, None, None]), 0.0)
    indec = inner.reshape(H * L, L)
    q_decay = jnp.exp((n + 1.0)[None, :] * lg[:, None])
    k_decay = jnp.exp((L - 1.0 - n)[None, :] * lg[:, None])
    chunk_decay = jnp.exp(L * lg)
    lane = np.arange(256)
    head_v = lane // HEAD_DIM
    head_qk = (lane % 128) // half
    qdec = q_decay.T[:, head_v]
    kdec = k_decay.T[:, head_qk]
    cdec = chunk_decay[head_v][None, :]
    bd = jnp.asarray((head_qk[:, None] == head_v[None, :]).astype(np.float32))
    segm = jnp.asarray((head_v[:, None] == head_v[None, :]).astype(np.float32) / HEAD_DIM, BF16)
    return cos, sin, indec, qdec, kdec, cdec, bd, segm


def _rwkv_consts():
    C = RWKV_CHUNK
    t = np.arange(C)
    tri = (t[:, None] >= t[None, :]).astype(np.float32)
    r = np.arange(4 * C)
    bdl = ((r[:, None] // C == r[None, :] // C) & (r[:, None] % C > r[None, :] % C)).astype(np.float32)
    strict = (t[:, None] > (r[None, :] % C)).astype(np.float32)
    incl = (t[:, None] >= (r[None, :] % C)).astype(np.float32)
    lane = np.arange(256) // HEAD_DIM
    seg = (lane[:, None] == lane[None, :]).astype(np.float32)
    return (jnp.asarray(tri, BF16), jnp.asarray(bdl), jnp.asarray(np.stack([strict, incl])),
            jnp.asarray(seg, BF16), jnp.asarray(seg / HEAD_DIM, BF16))


def _hi_lo(w):
    hi = w.astype(BF16)
    lo = (w - hi.astype(F32)).astype(BF16)
    return jnp.stack([hi, lo], axis=1)


def _pad_rows(w, top, total):
    return jnp.pad(w, ((0, 0), (top, total - top - w.shape[1]), (0, 0)))


def _layout_w_in_kernel(w_ref, oz_ref, og_ref):
    w = w_ref[...]
    o = 0
    nsa_q = w[:, o:o + 256]; o += 256
    nsa_kv = w[:, o:o + 384]; o += 384
    nsa_g = w[:, o:o + 3 * MIX_HEADS]; o += 3 * MIX_HEADS
    ret = w[:, o:o + 1024]; o += 1024
    rwkv = w[:, o:o + RWKV_COLS]; o += RWKV_COLS
    conv = w[:, o:o + 768]; o += 768
    gate = w[:, o:o + N_BRANCHES * D_MODEL]

    def rot_perm(x):
        half = HEAD_DIM // 2
        return [x[:, h * HEAD_DIM + p * half:h * HEAD_DIM + (p + 1) * half]
                for p in range(2) for h in range(MIX_HEADS)]

    ret = rot_perm(ret[:, 0:256]) + rot_perm(ret[:, 256:512]) + [ret[:, 512:]]
    zeros = lambda n: jnp.zeros((w.shape[0], n), w.dtype)
    out = jnp.concatenate([conv, nsa_q] + ret + [rwkv, zeros(1024 - RWKV_COLS), nsa_kv, nsa_g,
                                                 zeros(128 - 3 * MIX_HEADS)], axis=1)
    oz_ref[...] = out.astype(BF16)
    og_ref[...] = gate.astype(BF16)


def _rowmajor_w_in_kernel(w_ref, o_ref, *, n_cols):
    bc = w_ref.shape[0]
    col = pl.program_id(0) * bc + lax.broadcasted_iota(jnp.int32, (bc, w_ref.shape[2]), 0)
    for l in range(w_ref.shape[1]):
        o_ref[l] = jnp.where(col < n_cols, w_ref[:, l, :], 0.0).T.astype(BF16)


def _layout_w_in(w_in, tr=256, bc=256):
    L, D, n_in = w_in.shape
    n_pad = pl.cdiv(n_in, bc) * bc
    w_rows = pl.pallas_call(
        functools.partial(_rowmajor_w_in_kernel, n_cols=n_in),
        grid=(n_pad // bc,),
        in_specs=[pl.BlockSpec((bc, L, D), lambda i: (i, 0, 0))],
        out_specs=pl.BlockSpec((L, D, bc), lambda i: (0, 0, i)),
        out_shape=jax.ShapeDtypeStruct((L, D, n_pad), BF16),
        compiler_params=_cparams(("parallel",)),
        name="rowmajor_w_in",
    )(jnp.transpose(w_in, (2, 0, 1)))
    n_gate = N_BRANCHES * D_MODEL
    blk = lambda n: pl.BlockSpec((None, tr, n), lambda l, i: (l, i, 0))
    return pl.pallas_call(
        _layout_w_in_kernel,
        grid=(L, D // tr),
        in_specs=[blk(n_pad)],
        out_specs=[blk(Z_COLS), blk(n_gate)],
        out_shape=[jax.ShapeDtypeStruct((L, D, Z_COLS), BF16), jax.ShapeDtypeStruct((L, D, n_gate), BF16)],
        compiler_params=_cparams(("parallel", "parallel")),
        name="layout_w_in",
    )(w_rows)


def _layout_cmp(cmp_w, cmp_pe):
    L = cmp_w.shape[0]
    wk = cmp_w[:, 0]
    wv = cmp_w[:, 1]
    zero = jnp.zeros_like(wk)
    blk = jnp.concatenate([jnp.concatenate([wk, zero], axis=3), jnp.concatenate([zero, wv], axis=3)], axis=2)
    pe2 = jnp.concatenate([cmp_pe, cmp_pe], axis=2)[:, :, None, :]
    s = NSA_CMP_STRIDE
    return blk[:, :s].astype(BF16), blk[:, s:].astype(BF16), pe2[:, :s], pe2[:, s:]


def kernel(x, mem, ln_mix_pre, w_in, nsa_cmp_w, nsa_cmp_pe, rel_bias, ret_norm_g, rwkv_mu, rwkv_w0, rwkv_w2, rwkv_a0, rwkv_a2, rwkv_g2, rwkv_k_k, rwkv_k_a, rwkv_r_k, rwkv_ln_g, rwkv_ln_b, conv_w, w_branch, w_mix_out, ln_mix_post, ln_xa_pre, ln_mem, xa_wq, xa_wkv, xa_wo, ln_xa_post, ln_mlp_pre, mlp_w1, mlp_w2, ln_mlp_post):
    B, S, D = x.shape
    M = mem.shape[1]
    depth = w_in.shape[0]
    row = lambda g: g[:, None, :]
    t = _tiles(B, S, M)

    w_in_b, w_gate_b = _layout_w_in(w_in, tr=t["w_layout_rows"], bc=t["w_layout_cols"])
    cmp_lo, cmp_hi, pe_lo, pe_hi = _layout_cmp(nsa_cmp_w, nsa_cmp_pe)
    nsa_tabs = _nsa_tables(rel_bias, S)
    ret_tabs = _ret_tables(S)
    rwkv_consts = _rwkv_consts()
    mu = jnp.pad(rwkv_mu, ((0, 0), (0, 1024 - RWKV_COLS)))[:, None, :]
    lora_w = _hi_lo(jnp.concatenate([_pad_rows(rwkv_w2, 0, 128), _pad_rows(rwkv_a2, RWKV_DECAY_LORA, 128),
                                     _pad_rows(rwkv_g2, RWKV_DECAY_LORA + RWKV_AAA_LORA, 128)], axis=2))
    lora_w = lora_w.reshape(depth, 256, 3 * MIX_WIDTH)
    vec = jnp.stack([rwkv_w0, rwkv_a0, rwkv_k_k, rwkv_k_a, rwkv_r_k, rwkv_ln_g, rwkv_ln_b,
                     jnp.zeros_like(rwkv_w0)], axis=1)
    conv_p = jnp.pad(conv_w, ((0, 0), (0, 8 - CONV_WIDTH), (0, 0)))
    w_branch_b = w_branch.astype(BF16)
    w_mix_out_b = w_mix_out.astype(BF16)
    xa_wq_b = xa_wq.astype(BF16)
    xa_wkv_b = xa_wkv.astype(BF16)
    xa_wo_b = xa_wo.astype(BF16)
    mlp_w1_b = mlp_w1.astype(BF16)
    mlp_w2_b = mlp_w2.astype(BF16)

    xf = x.reshape(B * S, D)
    memf = mem.reshape(B * M, D)
    for l in range(depth):
        z = norm_matmul(xf, row(ln_mix_pre), w_in_b, l, tm=t["in_proj_rows"], tn=Z_COLS, out_dtype=F32)
        prep = nsa_compress(z, cmp_lo, cmp_hi, pe_lo, pe_hi, l, B, S)
        z3 = z.reshape(B, S, Z_COLS)
        G = t["mixer_seqs"]
        o_nsa = nsa_attention(z3, prep, *nsa_tabs, B, S, G).reshape(B * S, MIX_WIDTH)
        o_ret = retention(z3, ret_tabs, row(ret_norm_g), l, B, S, G).reshape(B * S, MIX_WIDTH)
        o_rwkv = rwkv7(z3, mu, lora_w, vec, rwkv_consts, l, B, S, G).reshape(B * S, MIX_WIDTH)
        xf = merge_mix(xf, z, o_nsa, o_ret, o_rwkv, conv_p, row(ln_mix_pre), w_gate_b, w_branch_b,
                       w_mix_out_b, row(ln_mix_post), l, B, S, tm=t["merge_rows"])
        kvm = norm_matmul(memf, row(ln_mem), xa_wkv_b, l, tm=t["mem_rows"], tn=2 * D_MODEL, out_dtype=BF16)
        xf = cross_attention(xf, kvm, row(ln_xa_pre), xa_wq_b, xa_wo_b, row(ln_xa_post), l, B, S, M,
                             tm=t["xattn_rows"])
        xf = mlp(xf, row(ln_mlp_pre), mlp_w1_b, mlp_w2_b, row(ln_mlp_post), l, tm=t["mlp_rows"])
    return xf.reshape(B, S, D)
```

```python
import functools
import math

import numpy as np
import jax
import jax.numpy as jnp
from jax import lax
from jax.experimental import pallas as pl
from jax.experimental.pallas import tpu as pltpu

F32 = jnp.float32
BF16 = jnp.bfloat16

D_MODEL = 1024
N_BRANCHES = 4
MIX_WIDTH = 256
HEAD_DIM = 64
MIX_HEADS = 4

NSA_CMP_LEN = 32
NSA_CMP_STRIDE = 16
NSA_SLC_LEN = 64
NSA_TOP_N = 8
NSA_WINDOW = 512
Q_BLOCK = 128
NSA_V_ROWS = HEAD_DIM + 16
FORCE_BONUS = 1e4
N_BUCKETS = 32
MAX_DISTANCE = 128

RET_CHUNK = 128
ROPE_BASE = 10000.0
RET_NORM_EPS = 1e-5

RWKV_DECAY_LORA = 32
RWKV_AAA_LORA = 32
RWKV_GATE_LORA = 64
RWKV_GN_EPS = 64e-5
RWKV_COLS = 3 * MIX_WIDTH + RWKV_DECAY_LORA + RWKV_AAA_LORA + RWKV_GATE_LORA
RWKV_CHUNK = 64

CONV_WIDTH = 3
XA_HEADS = 4
XA_HEAD_DIM = D_MODEL // XA_HEADS
D_FF = 4 * D_MODEL

RMS_EPS = 1e-6
LOG2E = math.log2(math.e)
NEG_INF = -1e30
NEG_BIG = -3e38

OFF_CONV = 0
OFF_NSAQ = 768
OFF_RET = 1024
OFF_RWKV = 2048
OFF_NSAKV = 3072
OFF_NSAG = 3456
Z_COLS = 3584

VMEM_LIMIT = 56 * 1024 * 1024
ROW_SUB_TILES = 2


def _tiles(B, S, M):
    rows = B * S
    return dict(
        in_proj_rows=min(1024, rows),
        merge_rows=min(512, S),
        xattn_rows=min(2048, S),
        mlp_rows=min(1024, rows),
        mem_rows=min(2 * M, B * M),
        mixer_seqs=min(8, B),
        w_layout_rows=256,
        w_layout_cols=256,
    )


def _cparams(sem):
    return pltpu.CompilerParams(dimension_semantics=sem, vmem_limit_bytes=VMEM_LIMIT)


def _dot(a, b):
    return jnp.dot(a, b, preferred_element_type=F32)


def _dot_nt(a, b):
    return lax.dot_general(a, b, (((1,), (1,)), ((), ())), preferred_element_type=F32)


def _dot_tn(a, b):
    return lax.dot_general(a, b, (((0,), (0,)), ((), ())), preferred_element_type=F32)


def _split2(x):
    hi = x.astype(BF16)
    lo = (x - hi.astype(F32)).astype(BF16)
    return hi, lo


def _dot_x2(x, w_bf16, *more):
    xs = (x,) + more
    rows = x.shape[0]
    parts = [p for xi in xs for p in _split2(xi)]
    out = _dot(jnp.concatenate(parts, axis=0), w_bf16)
    res = [out[2 * i * rows:(2 * i + 1) * rows] + out[(2 * i + 1) * rows:(2 * i + 2) * rows]
           for i in range(len(xs))]
    return res[0] if not more else res


def _rms(x, g):
    ms = jnp.mean(x * x, axis=-1, keepdims=True)
    return x * lax.rsqrt(ms + RMS_EPS) * g


def _norm_matmul_kernel(x_ref, g_ref, w_ref, o_ref):
    n_sub = ROW_SUB_TILES
    tm = x_ref.shape[0] // n_sub
    hs = [_rms(x_ref[pl.ds(i * tm, tm), :], g_ref[...]).astype(BF16) for i in range(n_sub)]
    for i in range(n_sub):
        o_ref[pl.ds(i * tm, tm), :] = _dot(hs[i], w_ref[...]).astype(o_ref.dtype)


def norm_matmul(x, g, w, layer, tm, out_dtype):
    M, D = x.shape
    N = w.shape[2]
    return pl.pallas_call(
        _norm_matmul_kernel,
        grid=(M // tm,),
        in_specs=[
            pl.BlockSpec((tm, D), lambda i: (i, 0)),
            pl.BlockSpec((None, 1, D), lambda i: (layer, 0, 0)),
            pl.BlockSpec((None, D, N), lambda i: (layer, 0, 0), pipeline_mode=pl.Buffered(1)),
        ],
        out_specs=pl.BlockSpec((tm, N), lambda i: (i, 0)),
        out_shape=jax.ShapeDtypeStruct((M, N), out_dtype),
        compiler_params=_cparams(("parallel",)),
        name="norm_matmul",
    )(x, g, w)


def _nsa_compress_kernel(kvc_ref, kvs_ref, kvw_ref, wlo_ref, whi_ref, pelo_ref, pehi_ref,
                         kc_ref, vct_ref, ks_ref, vst_ref, kw_ref, vwt_ref):
    nblk = kc_ref.shape[0]
    n_kt = vwt_ref.shape[0]
    ylo = jnp.zeros((nblk, 128), F32)
    yhi = jnp.zeros((nblk, 128), F32)
    for r in range(NSA_CMP_STRIDE):
        xr = kvc_ref[pl.ds(r, nblk, stride=NSA_CMP_STRIDE), :]
        ylo = ylo + _dot((xr + pelo_ref[r]).astype(BF16), wlo_ref[r])
        yhi = yhi + _dot((xr + pehi_ref[r]).astype(BF16), whi_ref[r])
    y = ylo + pltpu.roll(yhi, nblk - 1, axis=0)
    kc_ref[...] = y.astype(BF16)
    vct_ref[...] = y.T[HEAD_DIM:, :].astype(BF16)
    kw_ref[...] = kvw_ref[...].astype(BF16)
    ones = jnp.ones((NSA_V_ROWS - HEAD_DIM, 128), BF16)
    for kt in range(n_kt):
        rs = slice(kt * 128, (kt + 1) * 128)
        vst_ref[kt] = jnp.concatenate([kvs_ref[rs, :].T[HEAD_DIM:, :].astype(BF16), ones], axis=0)
        vwt_ref[kt] = jnp.concatenate([kvw_ref[rs, :].T[HEAD_DIM:, :].astype(BF16), ones], axis=0)
    S = kvs_ref.shape[0]
    key = lax.broadcasted_iota(jnp.int32, (S, 128), 0)
    lane = lax.broadcasted_iota(jnp.int32, (S, 128), 1)
    onehot = (lane == HEAD_DIM + key // NSA_SLC_LEN).astype(F32)
    ks_ref[0:S, :] = jnp.where(lane < HEAD_DIM, kvs_ref[...], onehot).astype(BF16)
    pad_lane = lax.broadcasted_iota(jnp.int32, (128, 128), 1)
    ks_ref[S:S + 128, :] = (pad_lane == HEAD_DIM + S // NSA_SLC_LEN).astype(BF16)
    vst_ref[n_kt] = jnp.zeros((NSA_V_ROWS, 128), BF16)


def nsa_compress(z, wlo, whi, pelo, pehi, layer, B, S):
    nblk = S // NSA_CMP_STRIDE
    n_kt = S // 128
    c0 = OFF_NSAKV // 128
    wmap = lambda b: (layer, 0, 0, 0)
    b3 = lambda b: (b, 0, 0)
    b4 = lambda b: (b, 0, 0, 0)
    return pl.pallas_call(
        _nsa_compress_kernel,
        grid=(B,),
        in_specs=[
            pl.BlockSpec((S, 128), lambda b: (b, c0)),
            pl.BlockSpec((S, 128), lambda b: (b, c0 + 1)),
            pl.BlockSpec((S, 128), lambda b: (b, c0 + 2)),
            pl.BlockSpec((None, NSA_CMP_STRIDE, 128, 128), wmap),
            pl.BlockSpec((None, NSA_CMP_STRIDE, 128, 128), wmap),
            pl.BlockSpec((None, NSA_CMP_STRIDE, 1, 128), wmap),
            pl.BlockSpec((None, NSA_CMP_STRIDE, 1, 128), wmap),
        ],
        out_specs=[
            pl.BlockSpec((None, nblk, 128), b3),
            pl.BlockSpec((None, HEAD_DIM, nblk), b3),
            pl.BlockSpec((None, S + 128, 128), b3),
            pl.BlockSpec((None, n_kt + 1, NSA_V_ROWS, 128), b4),
            pl.BlockSpec((None, S, 128), b3),
            pl.BlockSpec((None, n_kt, NSA_V_ROWS, 128), b4),
        ],
        out_shape=[
            jax.ShapeDtypeStruct((B, nblk, 128), BF16),
            jax.ShapeDtypeStruct((B, HEAD_DIM, nblk), BF16),
            jax.ShapeDtypeStruct((B, S + 128, 128), BF16),
            jax.ShapeDtypeStruct((B, n_kt + 1, NSA_V_ROWS, 128), BF16),
            jax.ShapeDtypeStruct((B, S, 128), BF16),
            jax.ShapeDtypeStruct((B, n_kt, NSA_V_ROWS, 128), BF16),
        ],
        compiler_params=_cparams(("parallel",)),
        name="nsa_compress",
    )(z, z, z, wlo, whi, pelo, pehi)


def _rowmax(s):
    return jnp.max(s, axis=-1, keepdims=True)


def _rowsum(s):
    return jnp.sum(s, axis=-1, keepdims=True)


def _colmax(s):
    return jnp.max(s, axis=0, keepdims=True)


def _colsum(s):
    return jnp.sum(s, axis=0, keepdims=True)


def _nsa_kernel(q_ref, gl_ref, kc_ref, vct_ref, ks_ref, vst_ref, kw_ref, vwt_ref, bc_ref, tab_ref,
                ovt_ref, o_ref):
    bi = pl.program_id(1)
    G = q_ref.shape[0]
    n_kt = vwt_ref.shape[1]
    n_blk = ovt_ref.shape[0]
    n_cmp = kc_ref.shape[1]
    QB = Q_BLOCK
    R = MIX_HEADS * QB
    n_off = NSA_WINDOW // 128

    lo = jnp.maximum(bi - n_off, 0)
    q4 = [None] * G
    q4_sel = [None] * G
    o_c = [None] * G
    carry_s = [None] * G
    carry_w = [None] * G

    def key_tile(kref, g, kt, lanes):
        return kref[g, pl.ds(pl.multiple_of(kt * 128, 128), 128), lanes]

    def pair_qk(g, k0, k1):
        keys = jnp.concatenate([key_tile(kw_ref, g, k0, slice(0, HEAD_DIM)),
                                key_tile(kw_ref, g, k1, slice(0, HEAD_DIM))], axis=0)
        qk = _dot(keys, q4[g])
        return qk[:128], qk[128:]

    def pair_qk_sel(g, k0, k1):
        keys = jnp.concatenate([key_tile(ks_ref, g, k0, slice(None)), key_tile(ks_ref, g, k1, slice(None))],
                               axis=0)
        qk = _dot(keys, q4_sel[g])
        return qk[:128], qk[128:]

    def head_cols(x, h):
        return x[:, h * QB:(h + 1) * QB]

    key1 = lax.broadcasted_iota(jnp.int32, (128, QB), 0)
    i1 = lax.broadcasted_iota(jnp.int32, (128, QB), 1)
    causal1 = key1 <= i1
    kp = jnp.maximum(bi - 1, 0)
    pen_prev = jnp.where(bi >= 1, 0.0, NEG_INF)

    def win_add(kt, valid):
        far = 2 * QB
        shift = jnp.where(valid, jnp.where(bi - kt < n_off, -far, 0), far)
        return jnp.where(key1 > i1 + shift, 0.0, NEG_INF)

    def diag_add(extra):
        return [jnp.where(causal1, head_cols(tab_ref[0], h) + extra, NEG_INF) for h in range(MIX_HEADS)]

    def prev_add(extra):
        return [head_cols(tab_ref[1], h) + extra for h in range(MIX_HEADS)]

    def pair_update(carry, qk0, add0, vt0, qk1, add1, vt1):
        p0s, p1s, ms, alphas = [], [], [], []
        for h in range(MIX_HEADS):
            a0 = head_cols(qk0, h) if add0 is None else head_cols(qk0, h) + add0[h]
            a1 = head_cols(qk1, h) if add1 is None else head_cols(qk1, h) + add1[h]
            m_new = jnp.maximum(_colmax(a0), _colmax(a1))
            if carry is not None:
                m_old = head_cols(carry[0], h)
                m_new = jnp.maximum(m_old, m_new)
                alphas.append(jnp.exp2(m_old - m_new))
            ms.append(m_new)
            p0s.append(jnp.exp2(a0 - m_new).astype(BF16))
            p1s.append(jnp.exp2(a1 - m_new).astype(BF16))
        yield
        probs = jnp.concatenate([jnp.concatenate(p0s, axis=1), jnp.concatenate(p1s, axis=1)], axis=0)
        acc = _dot(jnp.concatenate([vt0, vt1], axis=1), probs)
        if carry is not None:
            acc = jnp.concatenate(alphas, axis=1) * carry[1] + acc
        return jnp.concatenate(ms, axis=1), acc

    def head_part(g):
        qt = (q_ref[g] * (HEAD_DIM ** -0.5)).T
        q4f = jnp.concatenate([qt[h * HEAD_DIM:(h + 1) * HEAD_DIM] for h in range(MIX_HEADS)], axis=1)
        q4[g] = (q4f * LOG2E).astype(BF16)
        q4_cmp = q4f.astype(BF16)
        yield
        qk_d, qk_p = pair_qk(g, bi, kp)
        carry_w[g] = yield from pair_update(None, qk_d, diag_add(0.0), vwt_ref[g, bi],
                                            qk_p, prev_add(pen_prev), vwt_ref[g, kp])
        yield
        cmp_per_qb = QB // NSA_CMP_STRIDE
        g0 = pl.multiple_of((n_kt - 1 - bi) * cmp_per_qb, cmp_per_qb)
        qk = _dot(kc_ref[g, :, :HEAD_DIM], q4_cmp) * LOG2E
        nrow = lax.broadcasted_iota(jnp.int32, (n_cmp, QB), 0)
        tq = bi * QB + lax.broadcasted_iota(jnp.int32, (n_cmp, QB), 1)
        vis = tq - (nrow * NSA_CMP_STRIDE + (NSA_CMP_LEN - 1)) >= 0
        live = (tq >= NSA_CMP_LEN - 1).astype(F32)
        yield
        p_cs = []
        for h in range(MIX_HEADS):
            s = jnp.where(vis, head_cols(qk, h) + bc_ref[pl.ds(g0, n_cmp), h * QB:(h + 1) * QB], NEG_INF)
            e = jnp.exp2(s - _colmax(s))
            p_cs.append(e / _colsum(e) * live)
        yield
        o_c[g] = _dot(vct_ref[g], jnp.concatenate(p_cs, axis=1).astype(BF16))
        imp = _dot(ovt_ref[...], p_cs[0].astype(BF16))
        for h in range(1, MIX_HEADS):
            imp = imp + _dot(ovt_ref[...], p_cs[h].astype(BF16))
        yield
        blk = lax.broadcasted_iota(jnp.int32, (n_blk, QB), 0)
        cur = (bi * QB + lax.broadcasted_iota(jnp.int32, (n_blk, QB), 1)) // NSA_SLC_LEN
        forced = (blk == 0) | (blk == cur) | (blk == cur - 1)
        imp = jnp.where(forced, imp + FORCE_BONUS, imp)
        imp = jnp.where(blk <= cur, imp, NEG_INF)
        blk_f = blk.astype(F32)
        sel = jnp.zeros((n_blk, QB), F32)
        for _ in range(NSA_TOP_N):
            mx = _colmax(imp)
            idx = jnp.min(jnp.where(imp == mx, blk_f, 4096.0), axis=0, keepdims=True)
            pick = blk_f == idx
            sel = jnp.where(pick, 1.0, sel)
            imp = jnp.where(pick, NEG_BIG, imp)
            yield
        sel_add = (sel - 1.0) * (-NEG_INF)
        rows_left = 128 - HEAD_DIM - n_blk
        pad_rows = jnp.where(lax.broadcasted_iota(jnp.int32, (rows_left, QB), 0) == 0, NEG_INF, 0.0)
        mask_rows = jnp.concatenate([sel_add, pad_rows], axis=0).astype(BF16)
        q4_sel[g] = jnp.concatenate([q4[g], jnp.concatenate([mask_rows] * MIX_HEADS, axis=1)], axis=0)
        yield
        qk_d, qk_p = pair_qk_sel(g, bi, kp)
        carry_s[g] = yield from pair_update(None, qk_d, diag_add(0.0), vst_ref[g, bi],
                                            qk_p, prev_add(pen_prev), vst_ref[g, kp])

    _round_robin([head_part(g) for g in range(G)])

    n_old = jnp.maximum(bi - 1, 0)

    def sel_step(g, j, carry, out):
        k0 = 2 * j
        k1 = jnp.where(k0 + 1 < n_old, k0 + 1, n_kt)
        qk0, qk1 = pair_qk_sel(g, k0, k1)
        yield
        out[g] = yield from pair_update(carry, qk0, None, vst_ref[g, k0], qk1, None, vst_ref[g, k1])

    def win_step(g, j, carry, out):
        k0 = lo + 2 * j
        k1 = jnp.minimum(k0 + 1, bi)
        qk0, qk1 = pair_qk(g, k0, k1)
        yield
        out[g] = yield from pair_update(carry, qk0, [win_add(k0, True)] * MIX_HEADS, vwt_ref[g, k0],
                                        qk1, [win_add(k1, k0 + 1 < n_old)] * MIX_HEADS, vwt_ref[g, k1])

    def joint(step):
        def body(j, carries):
            out = [None] * G
            _round_robin([step(g, j, carries[g], out) for g in range(G)])
            return tuple(out)
        return body

    res_s = lax.fori_loop(0, (n_old + 1) // 2, joint(sel_step), tuple(carry_s))
    res_w = lax.fori_loop(0, (n_old - lo + 1) // 2, joint(win_step), tuple(carry_w))

    def tail_part(g):
        acc_s = res_s[g][1]
        acc_w = res_w[g][1]
        o_s = acc_s[:HEAD_DIM] * (1.0 / acc_s[HEAD_DIM:HEAD_DIM + 1])
        o_w = acc_w[:HEAD_DIM] * (1.0 / acc_w[HEAD_DIM:HEAD_DIM + 1])
        gate = jax.nn.sigmoid(gl_ref[g].T)
        yield
        outs = []
        for h in range(MIX_HEADS):
            cs = slice(h * QB, (h + 1) * QB)
            outs.append(gate[3 * h:3 * h + 1] * o_c[g][:, cs] + gate[3 * h + 1:3 * h + 2] * o_s[:, cs]
                        + gate[3 * h + 2:3 * h + 3] * o_w[:, cs])
        o_ref[g] = jnp.concatenate(outs, axis=0).T

    _round_robin([tail_part(g) for g in range(G)])


def nsa_attention(z, prep, bias_c, tab, ovt, B, S, G):
    n_qb = S // Q_BLOCK
    n_kt = S // 128
    n_cmp = S // NSA_CMP_STRIDE
    R = MIX_HEADS * Q_BLOCK
    kc, vct, ks, vst, kw, vwt = prep
    b3 = lambda b, i: (b, 0, 0)
    b4 = lambda b, i: (b, 0, 0, 0)
    return pl.pallas_call(
        _nsa_kernel,
        grid=(B // G, n_qb),
        in_specs=[
            pl.BlockSpec((G, Q_BLOCK, MIX_WIDTH), lambda b, i: (b, i, OFF_NSAQ // MIX_WIDTH)),
            pl.BlockSpec((G, Q_BLOCK, 128), lambda b, i: (b, i, OFF_NSAG // 128)),
            pl.BlockSpec((G, n_cmp, 128), b3),
            pl.BlockSpec((G, HEAD_DIM, n_cmp), b3),
            pl.BlockSpec((G, S + 128, 128), b3),
            pl.BlockSpec((G, n_kt + 1, NSA_V_ROWS, 128), b4),
            pl.BlockSpec((G, S, 128), b3),
            pl.BlockSpec((G, n_kt, NSA_V_ROWS, 128), b4),
            pl.BlockSpec(bias_c.shape, lambda b, i: (0, 0)),
            pl.BlockSpec((2, 128, R), lambda b, i: (0, 0, 0)),
            pl.BlockSpec(ovt.shape, lambda b, i: (0, 0)),
        ],
        out_specs=pl.BlockSpec((G, Q_BLOCK, MIX_WIDTH), lambda b, i: (b, i, 0)),
        out_shape=jax.ShapeDtypeStruct((B, S, MIX_WIDTH), F32),
        compiler_params=_cparams(("parallel", "arbitrary")),
        name="nsa_attention",
    )(z, z, kc, vct, ks, vst, kw, vwt, bias_c, tab, ovt)


def _stack_heads(x, head_of_lane):
    return jnp.concatenate([jnp.where(head_of_lane == h, x, jnp.zeros_like(x))
                            for h in range(MIX_HEADS)], axis=0)


def _head_norm(y, seg_mean, eps):
    mu = _dot_x2(y, seg_mean)
    d = y - mu
    var = _dot_x2(d * d, seg_mean)
    return d * lax.rsqrt(var + eps)


def _round_robin(chains):
    while chains:
        chains = [c for c in chains if next(c, StopIteration) is not StopIteration]


def _retention_kernel(z_ref, cos_ref, sin_ref, indec_ref, qdec_ref, kdec_ref, cdec_ref, bd_ref,
                      segm_ref, ng_ref, o_ref, st_ref):
    @pl.when(pl.program_id(1) == 0)
    def _():
        st_ref[...] = jnp.zeros_like(st_ref)

    _round_robin([_retention_chunk(z_ref.at[g], cos_ref, sin_ref, indec_ref, qdec_ref, kdec_ref, cdec_ref,
                                   bd_ref, segm_ref, ng_ref, o_ref.at[g], st_ref.at[g])
                  for g in range(z_ref.shape[0])])


def _retention_chunk(z_ref, cos_ref, sin_ref, indec_ref, qdec_ref, kdec_ref, cdec_ref, bd_ref,
                     segm_ref, ng_ref, o_ref, st_ref):
    L = RET_CHUNK
    z = z_ref[...]
    q = z[:, 0:256]
    k = z[:, 256:512]
    v = z[:, 512:768]
    g = z[:, 768:1024]
    cos = cos_ref[...]
    sin = sin_ref[...]

    def rot(u):
        u1 = u[:, :128]
        u2 = u[:, 128:]
        return jnp.concatenate([u1 * cos - u2 * sin, u2 * cos + u1 * sin], axis=1)

    qr = rot(q) * (HEAD_DIM ** -0.5)
    kr = rot(k)
    lane = lax.broadcasted_iota(jnp.int32, (L, 256), 1)
    head_qk = (lane & 127) >> 5
    head_v = lane >> 6
    qb = qr.astype(BF16)
    kb = kr.astype(BF16)
    vb = v.astype(BF16)
    yield

    att = _dot_nt(_stack_heads(qb, head_qk), kb) * indec_ref[...]
    yield
    o_st = _dot(att.astype(BF16), vb)
    o = jnp.zeros((L, 256), F32)
    for h in range(MIX_HEADS):
        o = o + jnp.where(head_v == h, o_st[h * L:(h + 1) * L], 0.0)
    yield
    state = st_ref[...]
    o = o + _dot(qb, state.astype(BF16)) * qdec_ref[...]
    st_ref[...] = state * cdec_ref[...] + _dot_tn((kr * kdec_ref[...]).astype(BF16), vb) * bd_ref[...]
    yield

    mu = _dot_x2(o, segm_ref[...])
    yield
    d = o - mu
    var = _dot_x2(d * d, segm_ref[...])
    yield
    o_ref[...] = g * jax.nn.sigmoid(g) * (d * lax.rsqrt(var + RET_NORM_EPS) * ng_ref[...])


def retention(z, tabs, ng, layer, B, S, G):
    L = RET_CHUNK
    nc = S // L
    cos, sin, indec, qdec, kdec, cdec, bd, segm = tabs
    const2 = lambda b, c: (0, 0)
    return pl.pallas_call(
        _retention_kernel,
        grid=(B // G, nc),
        in_specs=[
            pl.BlockSpec((G, L, 1024), lambda b, c: (b, c, OFF_RET // 1024)),
            pl.BlockSpec((L, 128), lambda b, c: (c, 0)),
            pl.BlockSpec((L, 128), lambda b, c: (c, 0)),
            pl.BlockSpec((MIX_HEADS * L, L), const2),
            pl.BlockSpec((L, 256), const2),
            pl.BlockSpec((L, 256), const2),
            pl.BlockSpec((1, 256), const2),
            pl.BlockSpec((256, 256), const2),
            pl.BlockSpec((256, 256), const2),
            pl.BlockSpec((None, 1, 256), lambda b, c: (layer, 0, 0)),
        ],
        out_specs=pl.BlockSpec((G, L, 256), lambda b, c: (b, c, 0)),
        out_shape=jax.ShapeDtypeStruct((B, S, 256), F32),
        scratch_shapes=[pltpu.VMEM((G, 256, 256), F32)],
        compiler_params=_cparams(("parallel", "arbitrary")),
        name="retention",
    )(z, cos, sin, indec, qdec, kdec, cdec, bd, segm, ng)


def _rwkv_kernel(z_ref, mu_ref, lw_ref, vec_ref, tri_ref, bdl_ref, wide_ref,
                 segs_ref, segm_ref, o_ref, st_ref, prev_ref):
    @pl.when(pl.program_id(1) == 0)
    def _():
        st_ref[...] = jnp.zeros_like(st_ref)
        prev_ref[...] = jnp.zeros_like(prev_ref)

    _round_robin([_rwkv_chunk(z_ref.at[g], mu_ref, lw_ref, vec_ref, tri_ref, bdl_ref,
                              wide_ref, segs_ref, segm_ref, o_ref.at[g], st_ref.at[g], prev_ref.at[g])
                  for g in range(z_ref.shape[0])])


def _rwkv_chunk(z_ref, mu_ref, lw_ref, vec_ref, tri_ref, bdl_ref, wide_ref,
                segs_ref, segm_ref, o_ref, st_ref, prev_ref):
    C = RWKV_CHUNK
    z = z_ref[...]
    rows = lax.broadcasted_iota(jnp.int32, (C, 1024), 0)
    zs = jnp.where(rows == 0, prev_ref[0:1, :], pltpu.roll(z, 1, axis=0))
    prev_ref[...] = jnp.broadcast_to(z[C - 1:C, :], prev_ref.shape)
    zf = z + (zs - z) * mu_ref[...]
    r = zf[:, 0:256]
    k = zf[:, 256:512]
    v = zf[:, 512:768]
    lora = zf[:, 768:896]

    vec = vec_ref[...]
    w0, a0, k_k, k_a, r_k, ln_g, ln_b = (vec[i:i + 1] for i in range(7))

    lane = lax.broadcasted_iota(jnp.int32, lora.shape, 1)
    lora_in = jnp.where(lane < RWKV_DECAY_LORA, jnp.tanh(lora),
                        jnp.where(lane < RWKV_DECAY_LORA + RWKV_AAA_LORA, lora, jax.nn.sigmoid(lora)))
    hi, lo = _split2(lora_in)
    lhs = jnp.concatenate([jnp.concatenate([hi, hi], axis=1),
                           jnp.concatenate([lo, jnp.zeros_like(lo)], axis=1)], axis=0)
    both = _dot(lhs, lw_ref[...])
    lora_out = both[:C] + both[C:]
    wpre = w0 + lora_out[:, 0:256]
    y = -wpre
    softplus = jnp.maximum(y, 0.0) + jnp.log(1.0 + jnp.exp(-jnp.abs(y)))
    w_log = -softplus - 0.5
    ld = -jnp.exp(w_log)
    a = jax.nn.sigmoid(a0 + lora_out[:, 256:512])
    gate = lora_out[:, 512:768]
    kk = k * k_k
    k2 = k * (1.0 + (a - 1.0) * k_a)
    kk_sq, bonus = _dot_x2(kk * kk, segs_ref[...], r * k2 * r_k)
    kk = kk / jnp.maximum(jnp.sqrt(kk_sq), 1e-12)
    yield

    tri = tri_ref[...]
    l1 = ld.astype(BF16)
    r1 = ld - l1.astype(F32)
    l2 = r1.astype(BF16)
    l3 = (r1 - l2.astype(F32)).astype(BF16)
    cs = _dot(jnp.concatenate([tri, tri, tri], axis=1), jnp.concatenate([l1, l2, l3], axis=0))
    cs_end = cs[C - 1:C, :]
    yield
    e_neg = jnp.exp(-cs)
    e_end = jnp.exp(cs_end - cs)
    kka = kk * a
    a_t = -kk * jnp.exp(cs - ld)
    r_t = r * jnp.exp(cs)
    b_t = kka * e_neg
    k_t = k2 * e_neg
    b_g = kka * e_end
    k_g = k2 * e_end

    lane = lax.broadcasted_iota(jnp.int32, (C, 256), 1)
    head = lane >> 6
    stack = lambda x: _stack_heads(x.astype(BF16), head)
    b_st, k_st, v_st = stack(b_t), stack(k_t), stack(v)

    ar = jnp.concatenate([a_t, r_t], axis=0).astype(BF16)
    bk = jnp.concatenate([b_st, k_st], axis=0)
    wide = _dot_nt(ar, bk)
    strict = wide_ref[0]
    incl = wide_ref[1]
    a_ak = (wide[0:C, 4 * C:] * strict).astype(BF16)
    a_rb = (wide[C:, 0:4 * C] * incl).astype(BF16)
    a_rk = (wide[C:, 4 * C:] * incl).astype(BF16)
    yield

    a_ab = wide[0:C, 0:4 * C]
    n_pow = jnp.concatenate([a_ab] * MIX_HEADS, axis=0) * bdl_ref[...]
    ri = lax.broadcasted_iota(jnp.int32, (4 * C, 4 * C), 0)
    ci = lax.broadcasted_iota(jnp.int32, (4 * C, 4 * C), 1)
    t_inv = jnp.where(ri == ci, 1.0, 0.0) + n_pow
    yield

    sq = 2
    while sq < C:
        nb = n_pow.astype(BF16)
        n_pow = _dot(nb, nb)
        yield
        t_inv = t_inv + _dot(t_inv.astype(BF16), n_pow.astype(BF16))
        sq *= 2
    yield

    state = st_ref[...]
    sb = state.astype(BF16)
    from_state = _dot_nt(ar, sb)
    x = from_state[0:C] + _dot(a_ak, v_st)
    yield
    u_st = _dot(t_inv.astype(BF16), stack(x)).astype(BF16)
    yield
    uv = jnp.concatenate([u_st, v_st], axis=0)
    yv = from_state[C:] + _dot(jnp.concatenate([a_rb, a_rk], axis=1), uv)
    bkg = jnp.concatenate([stack(b_g), stack(k_g)], axis=0)
    st_ref[...] = state * jnp.exp(cs_end) + _dot_tn(uv, bkg)
    yield

    yn = _head_norm(yv, segm_ref[...], RWKV_GN_EPS) * ln_g + ln_b
    yn = yn + bonus * v
    o_ref[...] = yn * gate


def rwkv7(z, mu, lora_w, vec, consts, layer, B, S, G):
    C = RWKV_CHUNK
    nc = S // C
    tri, bdl, wide, segs, segm = consts
    const2 = lambda b, c: (0, 0)
    return pl.pallas_call(
        _rwkv_kernel,
        grid=(B // G, nc),
        in_specs=[
            pl.BlockSpec((G, C, 1024), lambda b, c: (b, c, OFF_RWKV // 1024)),
            pl.BlockSpec((None, 1, 1024), lambda b, c: (layer, 0, 0)),
            pl.BlockSpec((None, 256, 768), lambda b, c: (layer, 0, 0)),
            pl.BlockSpec((None, 8, 256), lambda b, c: (layer, 0, 0)),
            pl.BlockSpec((C, C), const2),
            pl.BlockSpec((4 * C, 4 * C), const2),
            pl.BlockSpec((2, C, 4 * C), lambda b, c: (0, 0, 0)),
            pl.BlockSpec((256, 256), const2),
            pl.BlockSpec((256, 256), const2),
        ],
        out_specs=pl.BlockSpec((G, C, 256), lambda b, c: (b, c, 0)),
        out_shape=jax.ShapeDtypeStruct((B, S, 256), F32),
        scratch_shapes=[pltpu.VMEM((G, 256, 256), F32), pltpu.VMEM((G, 8, 1024), F32)],
        compiler_params=_cparams(("parallel", "arbitrary")),
        name="rwkv7",
    )(z, mu, lora_w, vec, tri, bdl, wide, segs, segm)


def _merge_kernel(x_ref, zc_ref, on_ref, or_ref, ow_ref, cw_ref, gpre_ref, wg_ref, wb_ref, wo_ref, g_ref,
                  o_ref, carry_ref):
    @pl.when(pl.program_id(1) == 0)
    def _():
        carry_ref[...] = jnp.zeros_like(carry_ref)

    n_sub = ROW_SUB_TILES
    tm = x_ref.shape[0] // n_sub

    def prologue(r0):
        rs = pl.ds(r0, tm)
        x = x_ref[rs, :]
        h = _rms(x, gpre_ref[...]).astype(BF16)
        zc = zc_ref[rs, :]
        b_g = zc[:, 0:256]
        u = zc[:, 256:512] * zc[:, 512:768]
        prev = carry_ref[...]
        rows = lax.broadcasted_iota(jnp.int32, (tm, 256), 0)
        u1 = jnp.where(rows == 0, prev[7:8], pltpu.roll(u, 1, axis=0))
        u2 = jnp.where(rows == 0, prev[6:7], jnp.where(rows == 1, prev[7:8], pltpu.roll(u, 2, axis=0)))
        carry_ref[...] = u[tm - 8:tm]
        cw = cw_ref[...]
        o_conv = b_g * (cw[0:1] * u2 + cw[1:2] * u1 + cw[2:3] * u)
        return x, h, (on_ref[rs, :], or_ref[rs, :], ow_ref[rs, :], o_conv)

    def project(r0, x, h, branches):
        merged = jnp.zeros((tm, D_MODEL), F32)
        for m in range(N_BRANCHES):
            gm = jax.nn.sigmoid(_dot(h, wg_ref[:, m * D_MODEL:(m + 1) * D_MODEL]))
            merged = merged + gm * _dot(branches[m].astype(BF16), wb_ref[m])
        y = _dot(merged.astype(BF16), wo_ref[...])
        o_ref[pl.ds(r0, tm), :] = x + _rms(y, g_ref[...])

    staged = [prologue(i * tm) for i in range(n_sub)]
    for i in range(n_sub):
        project(i * tm, *staged[i])


def merge_mix(x, z, o_nsa, o_ret, o_rwkv, conv_w, g_pre, w_gate, w_branch, w_out, g_post, layer, B, S, tm):
    nt = S // tm
    rowmap = lambda b, i: (b * nt + i, 0)
    lmap = lambda b, i: (layer, 0, 0)
    return pl.pallas_call(
        _merge_kernel,
        grid=(B, nt),
        in_specs=[
            pl.BlockSpec((tm, D_MODEL), rowmap),
            pl.BlockSpec((tm, 768), lambda b, i: (b * nt + i, OFF_CONV // 768)),
            pl.BlockSpec((tm, 256), rowmap),
            pl.BlockSpec((tm, 256), rowmap),
            pl.BlockSpec((tm, 256), rowmap),
            pl.BlockSpec((None, 8, 256), lmap),
            pl.BlockSpec((None, 1, D_MODEL), lmap),
            pl.BlockSpec((None, D_MODEL, N_BRANCHES * D_MODEL), lmap, pipeline_mode=pl.Buffered(1)),
            pl.BlockSpec((None, N_BRANCHES, MIX_WIDTH, D_MODEL), lambda b, i: (layer, 0, 0, 0),
                         pipeline_mode=pl.Buffered(1)),
            pl.BlockSpec((None, D_MODEL, D_MODEL), lmap, pipeline_mode=pl.Buffered(1)),
            pl.BlockSpec((None, 1, D_MODEL), lmap),
        ],
        out_specs=pl.BlockSpec((tm, D_MODEL), rowmap),
        out_shape=jax.ShapeDtypeStruct((B * S, D_MODEL), F32),
        scratch_shapes=[pltpu.VMEM((8, 256), F32)],
        compiler_params=_cparams(("parallel", "arbitrary")),
        name="merge_mix",
    )(x, z, o_nsa, o_ret, o_rwkv, conv_w, g_pre, w_gate, w_branch, w_out, g_post)


def _xattn_kernel(x_ref, kv_ref, gpre_ref, wq_ref, wo_ref, gpost_ref, o_ref):
    n_sub = ROW_SUB_TILES
    tm = x_ref.shape[0] // n_sub

    def prologue(r0):
        x = x_ref[pl.ds(r0, tm), :]
        return x, _rms(x, gpre_ref[...]).astype(BF16)

    def attend(r0, x, h):
        q = (_dot(h, wq_ref[...]) * (XA_HEAD_DIM ** -0.5)).astype(BF16)
        outs = []
        for hd in range(XA_HEADS):
            cs = slice(hd * XA_HEAD_DIM, (hd + 1) * XA_HEAD_DIM)
            kh = kv_ref[:, hd * XA_HEAD_DIM:(hd + 1) * XA_HEAD_DIM]
            vh = kv_ref[:, D_MODEL + hd * XA_HEAD_DIM:D_MODEL + (hd + 1) * XA_HEAD_DIM]
            s = _dot_nt(q[:, cs], kh)
            e = jnp.exp(s - _rowmax(s))
            outs.append((_dot(e.astype(BF16), vh) * (1.0 / _rowsum(e))).astype(BF16))
        y = _dot(jnp.concatenate(outs, axis=1), wo_ref[...])
        o_ref[pl.ds(r0, tm), :] = x + _rms(y, gpost_ref[...])

    staged = [prologue(i * tm) for i in range(n_sub)]
    for i in range(n_sub):
        attend(i * tm, *staged[i])


def cross_attention(x, kv, g_pre, wq, wo, g_post, layer, B, S, M, tm):
    nt = S // tm
    rowmap = lambda b, i: (b * nt + i, 0)
    lmap = lambda b, i: (layer, 0, 0)
    return pl.pallas_call(
        _xattn_kernel,
        grid=(B, nt),
        in_specs=[
            pl.BlockSpec((tm, D_MODEL), rowmap),
            pl.BlockSpec((M, 2 * D_MODEL), lambda b, i: (b, 0)),
            pl.BlockSpec((None, 1, D_MODEL), lmap),
            pl.BlockSpec((None, D_MODEL, D_MODEL), lmap),
            pl.BlockSpec((None, D_MODEL, D_MODEL), lmap),
            pl.BlockSpec((None, 1, D_MODEL), lmap),
        ],
        out_specs=pl.BlockSpec((tm, D_MODEL), rowmap),
        out_shape=jax.ShapeDtypeStruct((B * S, D_MODEL), F32),
        compiler_params=_cparams(("parallel", "arbitrary")),
        name="cross_attention",
    )(x, kv, g_pre, wq, wo, g_post)


def _mlp_kernel(x_ref, gpre_ref, w1_ref, w2_ref, gpost_ref, o_ref):
    n_sub = ROW_SUB_TILES
    tm = x_ref.shape[0] // n_sub
    tf = 1024

    def prologue(r0):
        x = x_ref[pl.ds(r0, tm), :]
        return x, _rms(x, gpre_ref[...]).astype(BF16)

    def project(r0, x, h):
        acc = jnp.zeros(x.shape, F32)
        for f in range(D_FF // tf):
            a = jnp.maximum(_dot(h, w1_ref[:, f * tf:(f + 1) * tf]), 0.0)
            acc = acc + _dot((a * a).astype(BF16), w2_ref[f * tf:(f + 1) * tf, :])
        o_ref[pl.ds(r0, tm), :] = x + _rms(acc, gpost_ref[...])

    staged = [prologue(i * tm) for i in range(n_sub)]
    for i in range(n_sub):
        project(i * tm, *staged[i])


def mlp(x, g_pre, w1, w2, g_post, layer, tm):
    M = x.shape[0]
    lmap = lambda i: (layer, 0, 0)
    return pl.pallas_call(
        _mlp_kernel,
        grid=(M // tm,),
        in_specs=[
            pl.BlockSpec((tm, D_MODEL), lambda i: (i, 0)),
            pl.BlockSpec((None, 1, D_MODEL), lmap),
            pl.BlockSpec((None, D_MODEL, D_FF), lmap, pipeline_mode=pl.Buffered(1)),
            pl.BlockSpec((None, D_FF, D_MODEL), lmap, pipeline_mode=pl.Buffered(1)),
            pl.BlockSpec((None, 1, D_MODEL), lmap),
        ],
        out_specs=pl.BlockSpec((tm, D_MODEL), lambda i: (i, 0)),
        out_shape=jax.ShapeDtypeStruct((M, D_MODEL), F32),
        compiler_params=_cparams(("parallel",)),
        name="mlp",
    )(x, g_pre, w1, w2, g_post)


def _t5_bucket(dist):
    n = jnp.maximum(dist, 0)
    max_exact = N_BUCKETS // 2
    nf = jnp.maximum(n, 1).astype(F32)
    large = max_exact + (jnp.log(nf / max_exact) / math.log(MAX_DISTANCE / max_exact)
                         * (N_BUCKETS - max_exact)).astype(jnp.int32)
    large = jnp.minimum(large, N_BUCKETS - 1)
    return jnp.where(n < max_exact, n, large)


def _nsa_tables(rel_bias, S):
    n_qb = S // Q_BLOCK
    n_cmp = S // NSA_CMP_STRIDE
    n_blk = S // NSA_SLC_LEN
    bias_f = rel_bias.astype(F32)
    i = jnp.arange(Q_BLOCK)
    per_qb = Q_BLOCK // NSA_CMP_STRIDE
    g = jnp.arange(n_cmp + per_qb * (n_qb - 1)) - per_qb * (n_qb - 1)
    d_c = i[None, :] - (g[:, None] * NSA_CMP_STRIDE + NSA_CMP_LEN - 1)
    def lookup(dist):
        hit = _t5_bucket(dist)[..., None, None] == jnp.arange(N_BUCKETS)[:, None]
        return jnp.sum(jnp.where(hit, bias_f, 0.0), axis=-2)

    bias_c = lookup(d_c).transpose(0, 2, 1).reshape(g.shape[0], MIX_HEADS * Q_BLOCK)
    d0 = i[:, None] - i[None, :]
    tab = jnp.stack([lookup(d0), lookup(d0 + Q_BLOCK)]) - bias_f[N_BUCKETS - 1]
    tab = tab.transpose(0, 2, 3, 1).reshape(2, Q_BLOCK, MIX_HEADS * Q_BLOCK)
    cs = np.arange(n_cmp)[None, :] * NSA_CMP_STRIDE
    ss = np.arange(n_blk)[:, None] * NSA_SLC_LEN
    ovt = np.clip(np.minimum(cs + NSA_CMP_LEN, ss + NSA_SLC_LEN) - np.maximum(cs, ss), 0, None)
    ovt = ovt.astype(np.float32) / NSA_CMP_LEN
    ovt[:, (S - NSA_CMP_LEN) // NSA_CMP_STRIDE + 1:] = 0.0
    return bias_c * LOG2E, tab * LOG2E, jnp.asarray(ovt, BF16)


def _ret_tables(S):
    L = RET_CHUNK
    H = MIX_HEADS
    half = HEAD_DIM // 2
    pos = jnp.arange(S)
    inv_freq = ROPE_BASE ** (-jnp.arange(half, dtype=F32) / half)
    ang = pos.astype(F32)[:, None] * inv_freq[None, :]
    cos = jnp.tile(jnp.cos(ang), (1, H))
    sin = jnp.tile(jnp.sin(ang), (1, H))
    lg = jnp.log(1.0 - 2.0 ** (-5.0 - jnp.arange(H, dtype=F32)))
    n = jnp.arange(L, dtype=F32)
    diff = n[:, None] - n[None, :]
    inner = jnp.where(diff >= 0, jnp.exp(jnp.maximum(diff, 0.0)[None] * lg[:, None, None]), 0.0)
    indec = inner.reshape(H * L, L)
    q_decay = jnp.exp((n + 1.0)[None, :] * lg[:, None])
    k_decay = jnp.exp((L - 1.0 - n)[None, :] * lg[:, None])
    chunk_decay = jnp.exp(L * lg)
    lane = np.arange(256)
    head_v = lane // HEAD_DIM
    head_qk = (lane % 128) // half
    qdec = q_decay.T[:, head_v]
    kdec = k_decay.T[:, head_qk]
    cdec = chunk_decay[head_v][None, :]
    bd = jnp.asarray((head_qk[:, None] == head_v[None, :]).astype(np.float32))
    segm = jnp.asarray((head_v[:, None] == head_v[None, :]).astype(np.float32) / HEAD_DIM, BF16)
    return cos, sin, indec, qdec, kdec, cdec, bd, segm


def _rwkv_consts():
    C = RWKV_CHUNK
    t = np.arange(C)
    tri = (t[:, None] >= t[None, :]).astype(np.float32)
    r = np.arange(4 * C)
    bdl = ((r[:, None] // C == r[None, :] // C) & (r[:, None] % C > r[None, :] % C)).astype(np.float32)
    strict = (t[:, None] > (r[None, :] % C)).astype(np.float32)
    incl = (t[:, None] >= (r[None, :] % C)).astype(np.float32)
    lane = np.arange(256) // HEAD_DIM
    seg = (lane[:, None] == lane[None, :]).astype(np.float32)
    return (jnp.asarray(tri, BF16), jnp.asarray(bdl), jnp.asarray(np.stack([strict, incl])),
            jnp.asarray(seg, BF16), jnp.asarray(seg / HEAD_DIM, BF16))


def _hi_lo(w):
    hi = w.astype(BF16)
    lo = (w - hi.astype(F32)).astype(BF16)
    return jnp.stack([hi, lo], axis=1)


def _pad_rows(w, top, total):
    return jnp.pad(w, ((0, 0), (top, total - top - w.shape[1]), (0, 0)))


def _layout_w_in_kernel(w_ref, oz_ref, og_ref):
    w = w_ref[...]
    o = 0
    nsa_q = w[:, o:o + 256]; o += 256
    nsa_kv = w[:, o:o + 384]; o += 384
    nsa_g = w[:, o:o + 3 * MIX_HEADS]; o += 3 * MIX_HEADS
    ret = w[:, o:o + 1024]; o += 1024
    rwkv = w[:, o:o + RWKV_COLS]; o += RWKV_COLS
    conv = w[:, o:o + 768]; o += 768
    gate = w[:, o:o + N_BRANCHES * D_MODEL]

    def rot_perm(x):
        half = HEAD_DIM // 2
        return [x[:, h * HEAD_DIM + p * half:h * HEAD_DIM + (p + 1) * half]
                for p in range(2) for h in range(MIX_HEADS)]

    ret = rot_perm(ret[:, 0:256]) + rot_perm(ret[:, 256:512]) + [ret[:, 512:]]
    zeros = lambda n: jnp.zeros((w.shape[0], n), w.dtype)
    out = jnp.concatenate([conv, nsa_q] + ret + [rwkv, zeros(1024 - RWKV_COLS), nsa_kv, nsa_g,
                                                 zeros(128 - 3 * MIX_HEADS)], axis=1)
    oz_ref[...] = out.astype(BF16)
    og_ref[...] = gate.astype(BF16)


def _rowmajor_w_in_kernel(w_ref, o_ref, *, n_cols):
    bc = w_ref.shape[0]
    col = pl.program_id(0) * bc + lax.broadcasted_iota(jnp.int32, (bc, w_ref.shape[2]), 0)
    for l in range(w_ref.shape[1]):
        o_ref[l] = jnp.where(col < n_cols, w_ref[:, l, :], 0.0).T.astype(BF16)


def _layout_w_in(w_in, tr=256, bc=256):
    L, D, n_in = w_in.shape
    n_pad = pl.cdiv(n_in, bc) * bc
    w_rows = pl.pallas_call(
        functools.partial(_rowmajor_w_in_kernel, n_cols=n_in),
        grid=(n_pad // bc,),
        in_specs=[pl.BlockSpec((bc, L, D), lambda i: (i, 0, 0))],
        out_specs=pl.BlockSpec((L, D, bc), lambda i: (0, 0, i)),
        out_shape=jax.ShapeDtypeStruct((L, D, n_pad), BF16),
        compiler_params=_cparams(("parallel",)),
        name="rowmajor_w_in",
    )(jnp.transpose(w_in, (2, 0, 1)))
    n_gate = N_BRANCHES * D_MODEL
    blk = lambda n: pl.BlockSpec((None, tr, n), lambda l, i: (l, i, 0))
    return pl.pallas_call(
        _layout_w_in_kernel,
        grid=(L, D // tr),
        in_specs=[blk(n_pad)],
        out_specs=[blk(Z_COLS), blk(n_gate)],
        out_shape=[jax.ShapeDtypeStruct((L, D, Z_COLS), BF16), jax.ShapeDtypeStruct((L, D, n_gate), BF16)],
        compiler_params=_cparams(("parallel", "parallel")),
        name="layout_w_in",
    )(w_rows)


def _layout_cmp(cmp_w, cmp_pe):
    L = cmp_w.shape[0]
    wk = cmp_w[:, 0]
    wv = cmp_w[:, 1]
    zero = jnp.zeros_like(wk)
    blk = jnp.concatenate([jnp.concatenate([wk, zero], axis=3), jnp.concatenate([zero, wv], axis=3)], axis=2)
    pe2 = jnp.concatenate([cmp_pe, cmp_pe], axis=2)[:, :, None, :]
    s = NSA_CMP_STRIDE
    return blk[:, :s].astype(BF16), blk[:, s:].astype(BF16), pe2[:, :s], pe2[:, s:]


def kernel(x, mem, ln_mix_pre, w_in, nsa_cmp_w, nsa_cmp_pe, rel_bias, ret_norm_g, rwkv_mu, rwkv_w0, rwkv_w2, rwkv_a0, rwkv_a2, rwkv_g2, rwkv_k_k, rwkv_k_a, rwkv_r_k, rwkv_ln_g, rwkv_ln_b, conv_w, w_branch, w_mix_out, ln_mix_post, ln_xa_pre, ln_mem, xa_wq, xa_wkv, xa_wo, ln_xa_post, ln_mlp_pre, mlp_w1, mlp_w2, ln_mlp_post):
    B, S, D = x.shape
    M = mem.shape[1]
    depth = w_in.shape[0]
    row = lambda g: g[:, None, :]
    t = _tiles(B, S, M)

    w_in_b, w_gate_b = _layout_w_in(w_in, tr=t["w_layout_rows"], bc=t["w_layout_cols"])
    cmp_lo, cmp_hi, pe_lo, pe_hi = _layout_cmp(nsa_cmp_w, nsa_cmp_pe)
    nsa_tabs = _nsa_tables(rel_bias, S)
    ret_tabs = _ret_tables(S)
    rwkv_consts = _rwkv_consts()
    mu = jnp.pad(rwkv_mu, ((0, 0), (0, 1024 - RWKV_COLS)))[:, None, :]
    lora_w = _hi_lo(jnp.concatenate([_pad_rows(rwkv_w2, 0, 128), _pad_rows(rwkv_a2, RWKV_DECAY_LORA, 128),
                                     _pad_rows(rwkv_g2, RWKV_DECAY_LORA + RWKV_AAA_LORA, 128)], axis=2))
    lora_w = lora_w.reshape(depth, 256, 3 * MIX_WIDTH)
    vec = jnp.stack([rwkv_w0, rwkv_a0, rwkv_k_k, rwkv_k_a, rwkv_r_k, rwkv_ln_g, rwkv_ln_b,
                     jnp.zeros_like(rwkv_w0)], axis=1)
    conv_p = jnp.pad(conv_w, ((0, 0), (0, 8 - CONV_WIDTH), (0, 0)))
    w_branch_b = w_branch.astype(BF16)
    w_mix_out_b = w_mix_out.astype(BF16)
    xa_wq_b = xa_wq.astype(BF16)
    xa_wkv_b = xa_wkv.astype(BF16)
    xa_wo_b = xa_wo.astype(BF16)
    mlp_w1_b = mlp_w1.astype(BF16)
    mlp_w2_b = mlp_w2.astype(BF16)

    xf = x.reshape(B * S, D)
    memf = mem.reshape(B * M, D)
    for l in range(depth):
        z = norm_matmul(xf, row(ln_mix_pre), w_in_b, l, tm=t["in_proj_rows"], out_dtype=F32)
        prep = nsa_compress(z, cmp_lo, cmp_hi, pe_lo, pe_hi, l, B, S)
        z3 = z.reshape(B, S, Z_COLS)
        G = t["mixer_seqs"]
        o_nsa = nsa_attention(z3, prep, *nsa_tabs, B, S, G).reshape(B * S, MIX_WIDTH)
        o_ret = retention(z3, ret_tabs, row(ret_norm_g), l, B, S, G).reshape(B * S, MIX_WIDTH)
        o_rwkv = rwkv7(z3, mu, lora_w, vec, rwkv_consts, l, B, S, G).reshape(B * S, MIX_WIDTH)
        xf = merge_mix(xf, z, o_nsa, o_ret, o_rwkv, conv_p, row(ln_mix_pre), w_gate_b, w_branch_b,
                       w_mix_out_b, row(ln_mix_post), l, B, S, tm=t["merge_rows"])
        kvm = norm_matmul(memf, row(ln_mem), xa_wkv_b, l, tm=t["mem_rows"], out_dtype=BF16)
        xf = cross_attention(xf, kvm, row(ln_xa_pre), xa_wq_b, xa_wo_b, row(ln_xa_post), l, B, S, M,
                             tm=t["xattn_rows"])
        xf = mlp(xf, row(ln_mlp_pre), mlp_w1_b, mlp_w2_b, row(ln_mlp_post), l, tm=t["mlp_rows"])
    return xf.reshape(B, S, D)
```

```python
import functools
import math

import numpy as np
import jax
import jax.numpy as jnp
from jax import lax
from jax.experimental import pallas as pl
from jax.experimental.pallas import tpu as pltpu

F32 = jnp.float32
BF16 = jnp.bfloat16

D_MODEL = 1024
N_BRANCHES = 4
MIX_WIDTH = 256
HEAD_DIM = 64
MIX_HEADS = 4

NSA_CMP_LEN = 32
NSA_CMP_STRIDE = 16
NSA_SLC_LEN = 64
NSA_TOP_N = 8
NSA_WINDOW = 512
Q_BLOCK = 128
NSA_V_ROWS = HEAD_DIM + 16
FORCE_BONUS = 1e4
N_BUCKETS = 32
MAX_DISTANCE = 128

RET_CHUNK = 128
ROPE_BASE = 10000.0
RET_NORM_EPS = 1e-5

RWKV_DECAY_LORA = 32
RWKV_AAA_LORA = 32
RWKV_GATE_LORA = 64
RWKV_GN_EPS = 64e-5
RWKV_COLS = 3 * MIX_WIDTH + RWKV_DECAY_LORA + RWKV_AAA_LORA + RWKV_GATE_LORA
RWKV_CHUNK = 64

CONV_WIDTH = 3
XA_HEADS = 4
XA_HEAD_DIM = D_MODEL // XA_HEADS
D_FF = 4 * D_MODEL

RMS_EPS = 1e-6
LOG2E = math.log2(math.e)
NEG_INF = -1e30
NEG_BIG = -3e38

OFF_CONV = 0
OFF_NSAQ = 768
OFF_RET = 1024
OFF_RWKV = 2048
OFF_NSAKV = 3072
OFF_NSAG = 3456
Z_COLS = 3584

VMEM_LIMIT = 56 * 1024 * 1024
ROW_SUB_TILES = 2


def _tiles(B, S, M):
    rows = B * S
    return dict(
        in_proj_rows=min(1024, rows),
        merge_rows=min(512, S),
        xattn_rows=min(2048, S),
        mlp_rows=min(1024, rows),
        mem_rows=min(2 * M, B * M),
        mixer_seqs=min(8, B),
        w_layout_rows=256,
        w_layout_cols=256,
    )


def _cparams(sem):
    return pltpu.CompilerParams(dimension_semantics=sem, vmem_limit_bytes=VMEM_LIMIT)


def _dot(a, b):
    return jnp.dot(a, b, preferred_element_type=F32)


def _dot_nt(a, b):
    return lax.dot_general(a, b, (((1,), (1,)), ((), ())), preferred_element_type=F32)


def _dot_tn(a, b):
    return lax.dot_general(a, b, (((0,), (0,)), ((), ())), preferred_element_type=F32)


def _split2(x):
    hi = x.astype(BF16)
    lo = (x - hi.astype(F32)).astype(BF16)
    return hi, lo


def _dot_x2(x, w_bf16, *more):
    xs = (x,) + more
    rows = x.shape[0]
    parts = [p for xi in xs for p in _split2(xi)]
    out = _dot(jnp.concatenate(parts, axis=0), w_bf16)
    res = [out[2 * i * rows:(2 * i + 1) * rows] + out[(2 * i + 1) * rows:(2 * i + 2) * rows]
           for i in range(len(xs))]
    return res[0] if not more else res


def _rms(x, g):
    ms = jnp.mean(x * x, axis=-1, keepdims=True)
    return x * lax.rsqrt(ms + RMS_EPS) * g


def _norm_matmul_kernel(x_ref, g_ref, w_ref, o_ref):
    n_sub = ROW_SUB_TILES
    tm = x_ref.shape[0] // n_sub
    hs = [_rms(x_ref[pl.ds(i * tm, tm), :], g_ref[...]).astype(BF16) for i in range(n_sub)]
    for i in range(n_sub):
        o_ref[pl.ds(i * tm, tm), :] = _dot(hs[i], w_ref[...]).astype(o_ref.dtype)


def norm_matmul(x, g, w, layer, tm, out_dtype):
    M, D = x.shape
    N = w.shape[2]
    return pl.pallas_call(
        _norm_matmul_kernel,
        grid=(M // tm,),
        in_specs=[
            pl.BlockSpec((tm, D), lambda i: (i, 0)),
            pl.BlockSpec((None, 1, D), lambda i: (layer, 0, 0)),
            pl.BlockSpec((None, D, N), lambda i: (layer, 0, 0), pipeline_mode=pl.Buffered(1)),
        ],
        out_specs=pl.BlockSpec((tm, N), lambda i: (i, 0)),
        out_shape=jax.ShapeDtypeStruct((M, N), out_dtype),
        compiler_params=_cparams(("parallel",)),
        name="norm_matmul",
    )(x, g, w)


def norm_matmul_all_layers(x, g, w, tm, out_dtype):
    M, D = x.shape
    L, _, N = w.shape
    return pl.pallas_call(
        _norm_matmul_kernel,
        grid=(L, M // tm),
        in_specs=[
            pl.BlockSpec((tm, D), lambda l, i: (i, 0)),
            pl.BlockSpec((None, 1, D), lambda l, i: (l, 0, 0)),
            pl.BlockSpec((None, D, N), lambda l, i: (l, 0, 0)),
        ],
        out_specs=pl.BlockSpec((None, tm, N), lambda l, i: (l, i, 0)),
        out_shape=jax.ShapeDtypeStruct((L, M, N), out_dtype),
        compiler_params=_cparams(("parallel", "parallel")),
        name="norm_matmul_all_layers",
    )(x, g, w)


def _nsa_compress_kernel(kvc_ref, kvs_ref, kvw_ref, wlo_ref, whi_ref, pelo_ref, pehi_ref,
                         kc_ref, vct_ref, ks_ref, vst_ref, kw_ref, vwt_ref):
    nblk = kc_ref.shape[0]
    n_kt = vwt_ref.shape[0]
    ylo = jnp.zeros((nblk, 128), F32)
    yhi = jnp.zeros((nblk, 128), F32)
    for r in range(NSA_CMP_STRIDE):
        xr = kvc_ref[pl.ds(r, nblk, stride=NSA_CMP_STRIDE), :]
        ylo = ylo + _dot((xr + pelo_ref[r]).astype(BF16), wlo_ref[r])
        yhi = yhi + _dot((xr + pehi_ref[r]).astype(BF16), whi_ref[r])
    y = ylo + pltpu.roll(yhi, nblk - 1, axis=0)
    kc_ref[...] = y.astype(BF16)
    vct_ref[...] = y.T[HEAD_DIM:, :].astype(BF16)
    kw_ref[...] = kvw_ref[...].astype(BF16)
    ones = jnp.ones((NSA_V_ROWS - HEAD_DIM, 128), BF16)
    for kt in range(n_kt):
        rs = slice(kt * 128, (kt + 1) * 128)
        vst_ref[kt] = jnp.concatenate([kvs_ref[rs, :].T[HEAD_DIM:, :].astype(BF16), ones], axis=0)
        vwt_ref[kt] = jnp.concatenate([kvw_ref[rs, :].T[HEAD_DIM:, :].astype(BF16), ones], axis=0)
    S = kvs_ref.shape[0]
    key = lax.broadcasted_iota(jnp.int32, (S, 128), 0)
    lane = lax.broadcasted_iota(jnp.int32, (S, 128), 1)
    onehot = (lane == HEAD_DIM + key // NSA_SLC_LEN).astype(F32)
    ks_ref[0:S, :] = jnp.where(lane < HEAD_DIM, kvs_ref[...], onehot).astype(BF16)
    pad_lane = lax.broadcasted_iota(jnp.int32, (128, 128), 1)
    ks_ref[S:S + 128, :] = (pad_lane == HEAD_DIM + S // NSA_SLC_LEN).astype(BF16)
    vst_ref[n_kt] = jnp.zeros((NSA_V_ROWS, 128), BF16)


def nsa_compress(z, wlo, whi, pelo, pehi, layer, B, S):
    nblk = S // NSA_CMP_STRIDE
    n_kt = S // 128
    c0 = OFF_NSAKV // 128
    wmap = lambda b: (layer, 0, 0, 0)
    b3 = lambda b: (b, 0, 0)
    b4 = lambda b: (b, 0, 0, 0)
    return pl.pallas_call(
        _nsa_compress_kernel,
        grid=(B,),
        in_specs=[
            pl.BlockSpec((S, 128), lambda b: (b, c0)),
            pl.BlockSpec((S, 128), lambda b: (b, c0 + 1)),
            pl.BlockSpec((S, 128), lambda b: (b, c0 + 2)),
            pl.BlockSpec((None, NSA_CMP_STRIDE, 128, 128), wmap),
            pl.BlockSpec((None, NSA_CMP_STRIDE, 128, 128), wmap),
            pl.BlockSpec((None, NSA_CMP_STRIDE, 1, 128), wmap),
            pl.BlockSpec((None, NSA_CMP_STRIDE, 1, 128), wmap),
        ],
        out_specs=[
            pl.BlockSpec((None, nblk, 128), b3),
            pl.BlockSpec((None, HEAD_DIM, nblk), b3),
            pl.BlockSpec((None, S + 128, 128), b3),
            pl.BlockSpec((None, n_kt + 1, NSA_V_ROWS, 128), b4),
            pl.BlockSpec((None, S, 128), b3),
            pl.BlockSpec((None, n_kt, NSA_V_ROWS, 128), b4),
        ],
        out_shape=[
            jax.ShapeDtypeStruct((B, nblk, 128), BF16),
            jax.ShapeDtypeStruct((B, HEAD_DIM, nblk), BF16),
            jax.ShapeDtypeStruct((B, S + 128, 128), BF16),
            jax.ShapeDtypeStruct((B, n_kt + 1, NSA_V_ROWS, 128), BF16),
            jax.ShapeDtypeStruct((B, S, 128), BF16),
            jax.ShapeDtypeStruct((B, n_kt, NSA_V_ROWS, 128), BF16),
        ],
        compiler_params=_cparams(("parallel",)),
        name="nsa_compress",
    )(z, z, z, wlo, whi, pelo, pehi)


def _rowmax(s):
    return jnp.max(s, axis=-1, keepdims=True)


def _rowsum(s):
    return jnp.sum(s, axis=-1, keepdims=True)


def _colmax(s):
    return jnp.max(s, axis=0, keepdims=True)


def _colsum(s):
    return jnp.sum(s, axis=0, keepdims=True)


def _nsa_kernel(q_ref, gl_ref, kc_ref, vct_ref, ks_ref, vst_ref, kw_ref, vwt_ref, bc_ref, tab_ref,
                ovt_ref, o_ref):
    bi = pl.program_id(1)
    G = q_ref.shape[0]
    n_kt = vwt_ref.shape[1]
    n_blk = ovt_ref.shape[0]
    n_cmp = kc_ref.shape[1]
    QB = Q_BLOCK
    R = MIX_HEADS * QB
    n_off = NSA_WINDOW // 128

    lo = jnp.maximum(bi - n_off, 0)
    q4 = [None] * G
    q4_sel = [None] * G
    o_c = [None] * G
    carry_s = [None] * G
    carry_w = [None] * G

    def key_tile(kref, g, kt, lanes):
        return kref[g, pl.ds(pl.multiple_of(kt * 128, 128), 128), lanes]

    def pair_qk(g, k0, k1):
        keys = jnp.concatenate([key_tile(kw_ref, g, k0, slice(0, HEAD_DIM)),
                                key_tile(kw_ref, g, k1, slice(0, HEAD_DIM))], axis=0)
        qk = _dot(keys, q4[g])
        return qk[:128], qk[128:]

    def pair_qk_sel(g, k0, k1):
        keys = jnp.concatenate([key_tile(ks_ref, g, k0, slice(None)), key_tile(ks_ref, g, k1, slice(None))],
                               axis=0)
        qk = _dot(keys, q4_sel[g])
        return qk[:128], qk[128:]

    def head_cols(x, h):
        return x[:, h * QB:(h + 1) * QB]

    key1 = lax.broadcasted_iota(jnp.int32, (128, QB), 0)
    i1 = lax.broadcasted_iota(jnp.int32, (128, QB), 1)
    causal1 = key1 <= i1
    kp = jnp.maximum(bi - 1, 0)
    pen_prev = jnp.where(bi >= 1, 0.0, NEG_INF)

    def win_add(kt, valid):
        far = 2 * QB
        shift = jnp.where(valid, jnp.where(bi - kt < n_off, -far, 0), far)
        return jnp.where(key1 > i1 + shift, 0.0, NEG_INF)

    def diag_add(extra):
        return [jnp.where(causal1, head_cols(tab_ref[0], h) + extra, NEG_INF) for h in range(MIX_HEADS)]

    def prev_add(extra):
        return [head_cols(tab_ref[1], h) + extra for h in range(MIX_HEADS)]

    def pair_update(carry, qk0, add0, vt0, qk1, add1, vt1):
        p0s, p1s, ms, alphas = [], [], [], []
        for h in range(MIX_HEADS):
            a0 = head_cols(qk0, h) if add0 is None else head_cols(qk0, h) + add0[h]
            a1 = head_cols(qk1, h) if add1 is None else head_cols(qk1, h) + add1[h]
            m_new = jnp.maximum(_colmax(a0), _colmax(a1))
            if carry is not None:
                m_old = head_cols(carry[0], h)
                m_new = jnp.maximum(m_old, m_new)
                alphas.append(jnp.exp2(m_old - m_new))
            ms.append(m_new)
            p0s.append(jnp.exp2(a0 - m_new).astype(BF16))
            p1s.append(jnp.exp2(a1 - m_new).astype(BF16))
        yield
        probs = jnp.concatenate([jnp.concatenate(p0s, axis=1), jnp.concatenate(p1s, axis=1)], axis=0)
        acc = _dot(jnp.concatenate([vt0, vt1], axis=1), probs)
        if carry is not None:
            acc = jnp.concatenate(alphas, axis=1) * carry[1] + acc
        return jnp.concatenate(ms, axis=1), acc

    def head_part(g):
        qt = (q_ref[g] * (HEAD_DIM ** -0.5)).T
        q4f = jnp.concatenate([qt[h * HEAD_DIM:(h + 1) * HEAD_DIM] for h in range(MIX_HEADS)], axis=1)
        q4[g] = (q4f * LOG2E).astype(BF16)
        q4_cmp = q4f.astype(BF16)
        yield
        qk_d, qk_p = pair_qk(g, bi, kp)
        carry_w[g] = yield from pair_update(None, qk_d, diag_add(0.0), vwt_ref[g, bi],
                                            qk_p, prev_add(pen_prev), vwt_ref[g, kp])
        yield
        cmp_per_qb = QB // NSA_CMP_STRIDE
        g0 = pl.multiple_of((n_kt - 1 - bi) * cmp_per_qb, cmp_per_qb)
        qk = _dot(kc_ref[g, :, :HEAD_DIM], q4_cmp) * LOG2E
        nrow = lax.broadcasted_iota(jnp.int32, (n_cmp, QB), 0)
        tq = bi * QB + lax.broadcasted_iota(jnp.int32, (n_cmp, QB), 1)
        vis = tq - (nrow * NSA_CMP_STRIDE + (NSA_CMP_LEN - 1)) >= 0
        live = (tq >= NSA_CMP_LEN - 1).astype(F32)
        yield
        p_cs = []
        for h in range(MIX_HEADS):
            s = jnp.where(vis, head_cols(qk, h) + bc_ref[pl.ds(g0, n_cmp), h * QB:(h + 1) * QB], NEG_INF)
            e = jnp.exp2(s - _colmax(s))
            p_cs.append(e / _colsum(e) * live)
        yield
        o_c[g] = _dot(vct_ref[g], jnp.concatenate(p_cs, axis=1).astype(BF16))
        imp = _dot(ovt_ref[...], p_cs[0].astype(BF16))
        for h in range(1, MIX_HEADS):
            imp = imp + _dot(ovt_ref[...], p_cs[h].astype(BF16))
        yield
        blk = lax.broadcasted_iota(jnp.int32, (n_blk, QB), 0)
        cur = (bi * QB + lax.broadcasted_iota(jnp.int32, (n_blk, QB), 1)) // NSA_SLC_LEN
        forced = (blk == 0) | (blk == cur) | (blk == cur - 1)
        imp = jnp.where(forced, imp + FORCE_BONUS, imp)
        imp = jnp.where(blk <= cur, imp, NEG_INF)
        blk_f = blk.astype(F32)
        sel = jnp.zeros((n_blk, QB), F32)
        for _ in range(NSA_TOP_N):
            mx = _colmax(imp)
            idx = jnp.min(jnp.where(imp == mx, blk_f, 4096.0), axis=0, keepdims=True)
            pick = blk_f == idx
            sel = jnp.where(pick, 1.0, sel)
            imp = jnp.where(pick, NEG_BIG, imp)
            yield
        sel_add = (sel - 1.0) * (-NEG_INF)
        rows_left = 128 - HEAD_DIM - n_blk
        pad_rows = jnp.where(lax.broadcasted_iota(jnp.int32, (rows_left, QB), 0) == 0, NEG_INF, 0.0)
        mask_rows = jnp.concatenate([sel_add, pad_rows], axis=0).astype(BF16)
        q4_sel[g] = jnp.concatenate([q4[g], jnp.concatenate([mask_rows] * MIX_HEADS, axis=1)], axis=0)
        yield
        qk_d, qk_p = pair_qk_sel(g, bi, kp)
        carry_s[g] = yield from pair_update(None, qk_d, diag_add(0.0), vst_ref[g, bi],
                                            qk_p, prev_add(pen_prev), vst_ref[g, kp])

    _round_robin([head_part(g) for g in range(G)])

    n_old = jnp.maximum(bi - 1, 0)

    def sel_step(g, j, carry, out):
        k0 = 2 * j
        k1 = jnp.where(k0 + 1 < n_old, k0 + 1, n_kt)
        qk0, qk1 = pair_qk_sel(g, k0, k1)
        yield
        out[g] = yield from pair_update(carry, qk0, None, vst_ref[g, k0], qk1, None, vst_ref[g, k1])

    def win_step(g, j, carry, out):
        k0 = lo + 2 * j
        k1 = jnp.minimum(k0 + 1, bi)
        qk0, qk1 = pair_qk(g, k0, k1)
        yield
        out[g] = yield from pair_update(carry, qk0, [win_add(k0, True)] * MIX_HEADS, vwt_ref[g, k0],
                                        qk1, [win_add(k1, k0 + 1 < n_old)] * MIX_HEADS, vwt_ref[g, k1])

    def joint(step):
        def body(j, carries):
            out = [None] * G
            _round_robin([step(g, j, carries[g], out) for g in range(G)])
            return tuple(out)
        return body

    res_s = lax.fori_loop(0, (n_old + 1) // 2, joint(sel_step), tuple(carry_s))
    res_w = lax.fori_loop(0, (n_old - lo + 1) // 2, joint(win_step), tuple(carry_w))

    def tail_part(g):
        acc_s = res_s[g][1]
        acc_w = res_w[g][1]
        o_s = acc_s[:HEAD_DIM] * (1.0 / acc_s[HEAD_DIM:HEAD_DIM + 1])
        o_w = acc_w[:HEAD_DIM] * (1.0 / acc_w[HEAD_DIM:HEAD_DIM + 1])
        gate = jax.nn.sigmoid(gl_ref[g].T)
        yield
        outs = []
        for h in range(MIX_HEADS):
            cs = slice(h * QB, (h + 1) * QB)
            outs.append(gate[3 * h:3 * h + 1] * o_c[g][:, cs] + gate[3 * h + 1:3 * h + 2] * o_s[:, cs]
                        + gate[3 * h + 2:3 * h + 3] * o_w[:, cs])
        o_ref[g] = jnp.concatenate(outs, axis=0).T

    _round_robin([tail_part(g) for g in range(G)])


def nsa_attention(z, prep, bias_c, tab, ovt, B, S, G):
    n_qb = S // Q_BLOCK
    n_kt = S // 128
    n_cmp = S // NSA_CMP_STRIDE
    R = MIX_HEADS * Q_BLOCK
    kc, vct, ks, vst, kw, vwt = prep
    b3 = lambda b, i: (b, 0, 0)
    b4 = lambda b, i: (b, 0, 0, 0)
    return pl.pallas_call(
        _nsa_kernel,
        grid=(B // G, n_qb),
        in_specs=[
            pl.BlockSpec((G, Q_BLOCK, MIX_WIDTH), lambda b, i: (b, i, OFF_NSAQ // MIX_WIDTH)),
            pl.BlockSpec((G, Q_BLOCK, 128), lambda b, i: (b, i, OFF_NSAG // 128)),
            pl.BlockSpec((G, n_cmp, 128), b3),
            pl.BlockSpec((G, HEAD_DIM, n_cmp), b3),
            pl.BlockSpec((G, S + 128, 128), b3),
            pl.BlockSpec((G, n_kt + 1, NSA_V_ROWS, 128), b4),
            pl.BlockSpec((G, S, 128), b3),
            pl.BlockSpec((G, n_kt, NSA_V_ROWS, 128), b4),
            pl.BlockSpec(bias_c.shape, lambda b, i: (0, 0)),
            pl.BlockSpec((2, 128, R), lambda b, i: (0, 0, 0)),
            pl.BlockSpec(ovt.shape, lambda b, i: (0, 0)),
        ],
        out_specs=pl.BlockSpec((G, Q_BLOCK, MIX_WIDTH), lambda b, i: (b, i, 0)),
        out_shape=jax.ShapeDtypeStruct((B, S, MIX_WIDTH), F32),
        compiler_params=_cparams(("parallel", "arbitrary")),
        name="nsa_attention",
    )(z, z, kc, vct, ks, vst, kw, vwt, bias_c, tab, ovt)


def _stack_heads(x, head_of_lane):
    return jnp.concatenate([jnp.where(head_of_lane == h, x, jnp.zeros_like(x))
                            for h in range(MIX_HEADS)], axis=0)


def _head_norm(y, seg_mean, eps):
    mu = _dot_x2(y, seg_mean)
    d = y - mu
    var = _dot_x2(d * d, seg_mean)
    return d * lax.rsqrt(var + eps)


def _round_robin(chains):
    while chains:
        chains = [c for c in chains if next(c, StopIteration) is not StopIteration]


def _retention_kernel(z_ref, cos_ref, sin_ref, indec_ref, qdec_ref, kdec_ref, cdec_ref, bd_ref,
                      segm_ref, ng_ref, o_ref, st_ref):
    @pl.when(pl.program_id(1) == 0)
    def _():
        st_ref[...] = jnp.zeros_like(st_ref)

    _round_robin([_retention_chunk(z_ref.at[g], cos_ref, sin_ref, indec_ref, qdec_ref, kdec_ref, cdec_ref,
                                   bd_ref, segm_ref, ng_ref, o_ref.at[g], st_ref.at[g])
                  for g in range(z_ref.shape[0])])


def _retention_chunk(z_ref, cos_ref, sin_ref, indec_ref, qdec_ref, kdec_ref, cdec_ref, bd_ref,
                     segm_ref, ng_ref, o_ref, st_ref):
    L = RET_CHUNK
    z = z_ref[...]
    q = z[:, 0:256]
    k = z[:, 256:512]
    v = z[:, 512:768]
    g = z[:, 768:1024]
    cos = cos_ref[...]
    sin = sin_ref[...]

    def rot(u):
        u1 = u[:, :128]
        u2 = u[:, 128:]
        return jnp.concatenate([u1 * cos - u2 * sin, u2 * cos + u1 * sin], axis=1)

    qr = rot(q) * (HEAD_DIM ** -0.5)
    kr = rot(k)
    lane = lax.broadcasted_iota(jnp.int32, (L, 256), 1)
    head_qk = (lane & 127) >> 5
    head_v = lane >> 6
    qb = qr.astype(BF16)
    kb = kr.astype(BF16)
    vb = v.astype(BF16)
    yield

    att = _dot_nt(_stack_heads(qb, head_qk), kb) * indec_ref[...]
    yield
    o_st = _dot(att.astype(BF16), vb)
    o = jnp.zeros((L, 256), F32)
    for h in range(MIX_HEADS):
        o = o + jnp.where(head_v == h, o_st[h * L:(h + 1) * L], 0.0)
    yield
    state = st_ref[...]
    o = o + _dot(qb, state.astype(BF16)) * qdec_ref[...]
    st_ref[...] = state * cdec_ref[...] + _dot_tn((kr * kdec_ref[...]).astype(BF16), vb) * bd_ref[...]
    yield

    mu = _dot_x2(o, segm_ref[...])
    yield
    d = o - mu
    var = _dot_x2(d * d, segm_ref[...])
    yield
    o_ref[...] = g * jax.nn.sigmoid(g) * (d * lax.rsqrt(var + RET_NORM_EPS) * ng_ref[...])


def retention(z, tabs, ng, layer, B, S, G):
    L = RET_CHUNK
    nc = S // L
    cos, sin, indec, qdec, kdec, cdec, bd, segm = tabs
    const2 = lambda b, c: (0, 0)
    return pl.pallas_call(
        _retention_kernel,
        grid=(B // G, nc),
        in_specs=[
            pl.BlockSpec((G, L, 1024), lambda b, c: (b, c, OFF_RET // 1024)),
            pl.BlockSpec((L, 128), lambda b, c: (c, 0)),
            pl.BlockSpec((L, 128), lambda b, c: (c, 0)),
            pl.BlockSpec((MIX_HEADS * L, L), const2),
            pl.BlockSpec((L, 256), const2),
            pl.BlockSpec((L, 256), const2),
            pl.BlockSpec((1, 256), const2),
            pl.BlockSpec((256, 256), const2),
            pl.BlockSpec((256, 256), const2),
            pl.BlockSpec((None, 1, 256), lambda b, c: (layer, 0, 0)),
        ],
        out_specs=pl.BlockSpec((G, L, 256), lambda b, c: (b, c, 0)),
        out_shape=jax.ShapeDtypeStruct((B, S, 256), F32),
        scratch_shapes=[pltpu.VMEM((G, 256, 256), F32)],
        compiler_params=_cparams(("parallel", "arbitrary")),
        name="retention",
    )(z, cos, sin, indec, qdec, kdec, cdec, bd, segm, ng)


def _rwkv_kernel(z_ref, mu_ref, lw_ref, vec_ref, tri_ref, bdl_ref, wide_ref,
                 segs_ref, segm_ref, o_ref, st_ref, prev_ref):
    @pl.when(pl.program_id(1) == 0)
    def _():
        st_ref[...] = jnp.zeros_like(st_ref)
        prev_ref[...] = jnp.zeros_like(prev_ref)

    _round_robin([_rwkv_chunk(z_ref.at[g], mu_ref, lw_ref, vec_ref, tri_ref, bdl_ref,
                              wide_ref, segs_ref, segm_ref, o_ref.at[g], st_ref.at[g], prev_ref.at[g])
                  for g in range(z_ref.shape[0])])


def _rwkv_chunk(z_ref, mu_ref, lw_ref, vec_ref, tri_ref, bdl_ref, wide_ref,
                segs_ref, segm_ref, o_ref, st_ref, prev_ref):
    C = RWKV_CHUNK
    z = z_ref[...]
    rows = lax.broadcasted_iota(jnp.int32, (C, 1024), 0)
    zs = jnp.where(rows == 0, prev_ref[0:1, :], pltpu.roll(z, 1, axis=0))
    prev_ref[...] = jnp.broadcast_to(z[C - 1:C, :], prev_ref.shape)
    zf = z + (zs - z) * mu_ref[...]
    r = zf[:, 0:256]
    k = zf[:, 256:512]
    v = zf[:, 512:768]
    lora = zf[:, 768:896]

    vec = vec_ref[...]
    w0, a0, k_k, k_a, r_k, ln_g, ln_b = (vec[i:i + 1] for i in range(7))

    lane = lax.broadcasted_iota(jnp.int32, lora.shape, 1)
    lora_in = jnp.where(lane < RWKV_DECAY_LORA, jnp.tanh(lora),
                        jnp.where(lane < RWKV_DECAY_LORA + RWKV_AAA_LORA, lora, jax.nn.sigmoid(lora)))
    hi, lo = _split2(lora_in)
    lhs = jnp.concatenate([jnp.concatenate([hi, hi], axis=1),
                           jnp.concatenate([lo, jnp.zeros_like(lo)], axis=1)], axis=0)
    both = _dot(lhs, lw_ref[...])
    lora_out = both[:C] + both[C:]
    wpre = w0 + lora_out[:, 0:256]
    y = -wpre
    softplus = jnp.maximum(y, 0.0) + jnp.log(1.0 + jnp.exp(-jnp.abs(y)))
    w_log = -softplus - 0.5
    ld = -jnp.exp(w_log)
    a = jax.nn.sigmoid(a0 + lora_out[:, 256:512])
    gate = lora_out[:, 512:768]
    kk = k * k_k
    k2 = k * (1.0 + (a - 1.0) * k_a)
    kk_sq, bonus = _dot_x2(kk * kk, segs_ref[...], r * k2 * r_k)
    kk = kk / jnp.maximum(jnp.sqrt(kk_sq), 1e-12)
    yield

    tri = tri_ref[...]
    l1 = ld.astype(BF16)
    r1 = ld - l1.astype(F32)
    l2 = r1.astype(BF16)
    l3 = (r1 - l2.astype(F32)).astype(BF16)
    cs = _dot(jnp.concatenate([tri, tri, tri], axis=1), jnp.concatenate([l1, l2, l3], axis=0))
    cs_end = cs[C - 1:C, :]
    yield
    e_neg = jnp.exp(-cs)
    e_end = jnp.exp(cs_end - cs)
    kka = kk * a
    a_t = -kk * jnp.exp(cs - ld)
    r_t = r * jnp.exp(cs)
    b_t = kka * e_neg
    k_t = k2 * e_neg
    b_g = kka * e_end
    k_g = k2 * e_end

    lane = lax.broadcasted_iota(jnp.int32, (C, 256), 1)
    head = lane >> 6
    stack = lambda x: _stack_heads(x.astype(BF16), head)
    b_st, k_st, v_st = stack(b_t), stack(k_t), stack(v)

    ar = jnp.concatenate([a_t, r_t], axis=0).astype(BF16)
    bk = jnp.concatenate([b_st, k_st], axis=0)
    wide = _dot_nt(ar, bk)
    strict = wide_ref[0]
    incl = wide_ref[1]
    a_ak = (wide[0:C, 4 * C:] * strict).astype(BF16)
    a_rb = (wide[C:, 0:4 * C] * incl).astype(BF16)
    a_rk = (wide[C:, 4 * C:] * incl).astype(BF16)
    yield

    a_ab = wide[0:C, 0:4 * C]
    n_pow = jnp.concatenate([a_ab] * MIX_HEADS, axis=0) * bdl_ref[...]
    ri = lax.broadcasted_iota(jnp.int32, (4 * C, 4 * C), 0)
    ci = lax.broadcasted_iota(jnp.int32, (4 * C, 4 * C), 1)
    t_inv = jnp.where(ri == ci, 1.0, 0.0) + n_pow
    yield

    sq = 2
    while sq < C:
        nb = n_pow.astype(BF16)
        n_pow = _dot(nb, nb)
        yield
        t_inv = t_inv + _dot(t_inv.astype(BF16), n_pow.astype(BF16))
        sq *= 2
    yield

    state = st_ref[...]
    sb = state.astype(BF16)
    from_state = _dot_nt(ar, sb)
    x = from_state[0:C] + _dot(a_ak, v_st)
    yield
    u_st = _dot(t_inv.astype(BF16), stack(x)).astype(BF16)
    yield
    uv = jnp.concatenate([u_st, v_st], axis=0)
    yv = from_state[C:] + _dot(jnp.concatenate([a_rb, a_rk], axis=1), uv)
    bkg = jnp.concatenate([stack(b_g), stack(k_g)], axis=0)
    st_ref[...] = state * jnp.exp(cs_end) + _dot_tn(uv, bkg)
    yield

    yn = _head_norm(yv, segm_ref[...], RWKV_GN_EPS) * ln_g + ln_b
    yn = yn + bonus * v
    o_ref[...] = yn * gate


def rwkv7(z, mu, lora_w, vec, consts, layer, B, S, G):
    C = RWKV_CHUNK
    nc = S // C
    tri, bdl, wide, segs, segm = consts
    const2 = lambda b, c: (0, 0)
    return pl.pallas_call(
        _rwkv_kernel,
        grid=(B // G, nc),
        in_specs=[
            pl.BlockSpec((G, C, 1024), lambda b, c: (b, c, OFF_RWKV // 1024)),
            pl.BlockSpec((None, 1, 1024), lambda b, c: (layer, 0, 0)),
            pl.BlockSpec((None, 256, 768), lambda b, c: (layer, 0, 0)),
            pl.BlockSpec((None, 8, 256), lambda b, c: (layer, 0, 0)),
            pl.BlockSpec((C, C), const2),
            pl.BlockSpec((4 * C, 4 * C), const2),
            pl.BlockSpec((2, C, 4 * C), lambda b, c: (0, 0, 0)),
            pl.BlockSpec((256, 256), const2),
            pl.BlockSpec((256, 256), const2),
        ],
        out_specs=pl.BlockSpec((G, C, 256), lambda b, c: (b, c, 0)),
        out_shape=jax.ShapeDtypeStruct((B, S, 256), F32),
        scratch_shapes=[pltpu.VMEM((G, 256, 256), F32), pltpu.VMEM((G, 8, 1024), F32)],
        compiler_params=_cparams(("parallel", "arbitrary")),
        name="rwkv7",
    )(z, mu, lora_w, vec, tri, bdl, wide, segs, segm)


def _merge_kernel(x_ref, zc_ref, on_ref, or_ref, ow_ref, cw_ref, gpre_ref, wg_ref, wb_ref, wo_ref, g_ref,
                  o_ref, carry_ref):
    @pl.when(pl.program_id(1) == 0)
    def _():
        carry_ref[...] = jnp.zeros_like(carry_ref)

    n_sub = ROW_SUB_TILES
    tm = x_ref.shape[0] // n_sub

    def prologue(r0):
        rs = pl.ds(r0, tm)
        x = x_ref[rs, :]
        h = _rms(x, gpre_ref[...]).astype(BF16)
        zc = zc_ref[rs, :]
        b_g = zc[:, 0:256]
        u = zc[:, 256:512] * zc[:, 512:768]
        prev = carry_ref[...]
        rows = lax.broadcasted_iota(jnp.int32, (tm, 256), 0)
        u1 = jnp.where(rows == 0, prev[7:8], pltpu.roll(u, 1, axis=0))
        u2 = jnp.where(rows == 0, prev[6:7], jnp.where(rows == 1, prev[7:8], pltpu.roll(u, 2, axis=0)))
        carry_ref[...] = u[tm - 8:tm]
        cw = cw_ref[...]
        o_conv = b_g * (cw[0:1] * u2 + cw[1:2] * u1 + cw[2:3] * u)
        return x, h, (on_ref[rs, :], or_ref[rs, :], ow_ref[rs, :], o_conv)

    def project(r0, x, h, branches):
        merged = jnp.zeros((tm, D_MODEL), F32)
        for m in range(N_BRANCHES):
            gm = jax.nn.sigmoid(_dot(h, wg_ref[:, m * D_MODEL:(m + 1) * D_MODEL]))
            merged = merged + gm * _dot(branches[m].astype(BF16), wb_ref[m])
        y = _dot(merged.astype(BF16), wo_ref[...])
        o_ref[pl.ds(r0, tm), :] = x + _rms(y, g_ref[...])

    staged = [prologue(i * tm) for i in range(n_sub)]
    for i in range(n_sub):
        project(i * tm, *staged[i])


def merge_mix(x, z, o_nsa, o_ret, o_rwkv, conv_w, g_pre, w_gate, w_branch, w_out, g_post, layer, B, S, tm):
    nt = S // tm
    rowmap = lambda b, i: (b * nt + i, 0)
    lmap = lambda b, i: (layer, 0, 0)
    return pl.pallas_call(
        _merge_kernel,
        grid=(B, nt),
        in_specs=[
            pl.BlockSpec((tm, D_MODEL), rowmap),
            pl.BlockSpec((tm, 768), lambda b, i: (b * nt + i, OFF_CONV // 768)),
            pl.BlockSpec((tm, 256), rowmap),
            pl.BlockSpec((tm, 256), rowmap),
            pl.BlockSpec((tm, 256), rowmap),
            pl.BlockSpec((None, 8, 256), lmap),
            pl.BlockSpec((None, 1, D_MODEL), lmap),
            pl.BlockSpec((None, D_MODEL, N_BRANCHES * D_MODEL), lmap, pipeline_mode=pl.Buffered(1)),
            pl.BlockSpec((None, N_BRANCHES, MIX_WIDTH, D_MODEL), lambda b, i: (layer, 0, 0, 0),
                         pipeline_mode=pl.Buffered(1)),
            pl.BlockSpec((None, D_MODEL, D_MODEL), lmap, pipeline_mode=pl.Buffered(1)),
            pl.BlockSpec((None, 1, D_MODEL), lmap),
        ],
        out_specs=pl.BlockSpec((tm, D_MODEL), rowmap),
        out_shape=jax.ShapeDtypeStruct((B * S, D_MODEL), F32),
        scratch_shapes=[pltpu.VMEM((8, 256), F32)],
        compiler_params=_cparams(("parallel", "arbitrary")),
        name="merge_mix",
    )(x, z, o_nsa, o_ret, o_rwkv, conv_w, g_pre, w_gate, w_branch, w_out, g_post)


def _xattn_kernel(x_ref, kv_ref, gpre_ref, wq_ref, wo_ref, gpost_ref, o_ref):
    n_sub = ROW_SUB_TILES
    tm = x_ref.shape[0] // n_sub

    def prologue(r0):
        x = x_ref[pl.ds(r0, tm), :]
        return x, _rms(x, gpre_ref[...]).astype(BF16)

    def attend(r0, x, h):
        q = (_dot(h, wq_ref[...]) * (XA_HEAD_DIM ** -0.5)).astype(BF16)
        outs = []
        for hd in range(XA_HEADS):
            cs = slice(hd * XA_HEAD_DIM, (hd + 1) * XA_HEAD_DIM)
            kh = kv_ref[:, hd * XA_HEAD_DIM:(hd + 1) * XA_HEAD_DIM]
            vh = kv_ref[:, D_MODEL + hd * XA_HEAD_DIM:D_MODEL + (hd + 1) * XA_HEAD_DIM]
            s = _dot_nt(q[:, cs], kh)
            e = jnp.exp(s - _rowmax(s))
            outs.append((_dot(e.astype(BF16), vh) * (1.0 / _rowsum(e))).astype(BF16))
        y = _dot(jnp.concatenate(outs, axis=1), wo_ref[...])
        o_ref[pl.ds(r0, tm), :] = x + _rms(y, gpost_ref[...])

    staged = [prologue(i * tm) for i in range(n_sub)]
    for i in range(n_sub):
        attend(i * tm, *staged[i])


def cross_attention(x, kv, g_pre, wq, wo, g_post, layer, B, S, M, tm):
    nt = S // tm
    rowmap = lambda b, i: (b * nt + i, 0)
    lmap = lambda b, i: (layer, 0, 0)
    return pl.pallas_call(
        _xattn_kernel,
        grid=(B, nt),
        in_specs=[
            pl.BlockSpec((tm, D_MODEL), rowmap),
            pl.BlockSpec((None, M, 2 * D_MODEL), lambda b, i: (layer, b, 0)),
            pl.BlockSpec((None, 1, D_MODEL), lmap),
            pl.BlockSpec((None, D_MODEL, D_MODEL), lmap),
            pl.BlockSpec((None, D_MODEL, D_MODEL), lmap),
            pl.BlockSpec((None, 1, D_MODEL), lmap),
        ],
        out_specs=pl.BlockSpec((tm, D_MODEL), rowmap),
        out_shape=jax.ShapeDtypeStruct((B * S, D_MODEL), F32),
        compiler_params=_cparams(("parallel", "arbitrary")),
        name="cross_attention",
    )(x, kv, g_pre, wq, wo, g_post)


def _mlp_kernel(x_ref, gpre_ref, w1_ref, w2_ref, gpost_ref, o_ref):
    n_sub = ROW_SUB_TILES
    tm = x_ref.shape[0] // n_sub
    tf = 1024

    def prologue(r0):
        x = x_ref[pl.ds(r0, tm), :]
        return x, _rms(x, gpre_ref[...]).astype(BF16)

    def project(r0, x, h):
        acc = jnp.zeros(x.shape, F32)
        for f in range(D_FF // tf):
            a = jnp.maximum(_dot(h, w1_ref[:, f * tf:(f + 1) * tf]), 0.0)
            acc = acc + _dot((a * a).astype(BF16), w2_ref[f * tf:(f + 1) * tf, :])
        o_ref[pl.ds(r0, tm), :] = x + _rms(acc, gpost_ref[...])

    staged = [prologue(i * tm) for i in range(n_sub)]
    for i in range(n_sub):
        project(i * tm, *staged[i])


def mlp(x, g_pre, w1, w2, g_post, layer, tm):
    M = x.shape[0]
    lmap = lambda i: (layer, 0, 0)
    return pl.pallas_call(
        _mlp_kernel,
        grid=(M // tm,),
        in_specs=[
            pl.BlockSpec((tm, D_MODEL), lambda i: (i, 0)),
            pl.BlockSpec((None, 1, D_MODEL), lmap),
            pl.BlockSpec((None, D_MODEL, D_FF), lmap, pipeline_mode=pl.Buffered(1)),
            pl.BlockSpec((None, D_FF, D_MODEL), lmap, pipeline_mode=pl.Buffered(1)),
            pl.BlockSpec((None, 1, D_MODEL), lmap),
        ],
        out_specs=pl.BlockSpec((tm, D_MODEL), lambda i: (i, 0)),
        out_shape=jax.ShapeDtypeStruct((M, D_MODEL), F32),
        compiler_params=_cparams(("parallel",)),
        name="mlp",
    )(x, g_pre, w1, w2, g_post)


def _t5_bucket(dist):
    n = jnp.maximum(dist, 0)
    max_exact = N_BUCKETS // 2
    nf = jnp.maximum(n, 1).astype(F32)
    large = max_exact + (jnp.log(nf / max_exact) / math.log(MAX_DISTANCE / max_exact)
                         * (N_BUCKETS - max_exact)).astype(jnp.int32)
    large = jnp.minimum(large, N_BUCKETS - 1)
    return jnp.where(n < max_exact, n, large)


def _nsa_tables(rel_bias, S):
    n_qb = S // Q_BLOCK
    n_cmp = S // NSA_CMP_STRIDE
    n_blk = S // NSA_SLC_LEN
    bias_f = rel_bias.astype(F32)
    i = jnp.arange(Q_BLOCK)
    per_qb = Q_BLOCK // NSA_CMP_STRIDE
    g = jnp.arange(n_cmp + per_qb * (n_qb - 1)) - per_qb * (n_qb - 1)
    d_c = i[None, :] - (g[:, None] * NSA_CMP_STRIDE + NSA_CMP_LEN - 1)
    def lookup(dist):
        hit = _t5_bucket(dist)[..., None, None] == jnp.arange(N_BUCKETS)[:, None]
        return jnp.sum(jnp.where(hit, bias_f, 0.0), axis=-2)

    bias_c = lookup(d_c).transpose(0, 2, 1).reshape(g.shape[0], MIX_HEADS * Q_BLOCK)
    d0 = i[:, None] - i[None, :]
    tab = jnp.stack([lookup(d0), lookup(d0 + Q_BLOCK)]) - bias_f[N_BUCKETS - 1]
    tab = tab.transpose(0, 2, 3, 1).reshape(2, Q_BLOCK, MIX_HEADS * Q_BLOCK)
    cs = np.arange(n_cmp)[None, :] * NSA_CMP_STRIDE
    ss = np.arange(n_blk)[:, None] * NSA_SLC_LEN
    ovt = np.clip(np.minimum(cs + NSA_CMP_LEN, ss + NSA_SLC_LEN) - np.maximum(cs, ss), 0, None)
    ovt = ovt.astype(np.float32) / NSA_CMP_LEN
    ovt[:, (S - NSA_CMP_LEN) // NSA_CMP_STRIDE + 1:] = 0.0
    return bias_c * LOG2E, tab * LOG2E, jnp.asarray(ovt, BF16)


def _ret_tables(S):
    L = RET_CHUNK
    H = MIX_HEADS
    half = HEAD_DIM // 2
    pos = jnp.arange(S)
    inv_freq = ROPE_BASE ** (-jnp.arange(half, dtype=F32) / half)
    ang = pos.astype(F32)[:, None] * inv_freq[None, :]
    cos = jnp.tile(jnp.cos(ang), (1, H))
    sin = jnp.tile(jnp.sin(ang), (1, H))
    lg = jnp.log(1.0 - 2.0 ** (-5.0 - jnp.arange(H, dtype=F32)))
    n = jnp.arange(L, dtype=F32)
    diff = n[:, None] - n[None, :]
    inner = jnp.where(diff >= 0, jnp.exp(jnp.maximum(diff, 0.0)[None] * lg[:, None, None]), 0.0)
    indec = inner.reshape(H * L, L)
    q_decay = jnp.exp((n + 1.0)[None, :] * lg[:, None])
    k_decay = jnp.exp((L - 1.0 - n)[None, :] * lg[:, None])
    chunk_decay = jnp.exp(L * lg)
    lane = np.arange(256)
    head_v = lane // HEAD_DIM
    head_qk = (lane % 128) // half
    qdec = q_decay.T[:, head_v]
    kdec = k_decay.T[:, head_qk]
    cdec = chunk_decay[head_v][None, :]
    bd = jnp.asarray((head_qk[:, None] == head_v[None, :]).astype(np.float32))
    segm = jnp.asarray((head_v[:, None] == head_v[None, :]).astype(np.float32) / HEAD_DIM, BF16)
    return cos, sin, indec, qdec, kdec, cdec, bd, segm


def _rwkv_consts():
    C = RWKV_CHUNK
    t = np.arange(C)
    tri = (t[:, None] >= t[None, :]).astype(np.float32)
    r = np.arange(4 * C)
    bdl = ((r[:, None] // C == r[None, :] // C) & (r[:, None] % C > r[None, :] % C)).astype(np.float32)
    strict = (t[:, None] > (r[None, :] % C)).astype(np.float32)
    incl = (t[:, None] >= (r[None, :] % C)).astype(np.float32)
    lane = np.arange(256) // HEAD_DIM
    seg = (lane[:, None] == lane[None, :]).astype(np.float32)
    return (jnp.asarray(tri, BF16), jnp.asarray(bdl), jnp.asarray(np.stack([strict, incl])),
            jnp.asarray(seg, BF16), jnp.asarray(seg / HEAD_DIM, BF16))


def _hi_lo(w):
    hi = w.astype(BF16)
    lo = (w - hi.astype(F32)).astype(BF16)
    return jnp.stack([hi, lo], axis=1)


def _pad_rows(w, top, total):
    return jnp.pad(w, ((0, 0), (top, total - top - w.shape[1]), (0, 0)))


def _layout_w_in_kernel(w_ref, oz_ref, og_ref):
    w = w_ref[...]
    o = 0
    nsa_q = w[:, o:o + 256]; o += 256
    nsa_kv = w[:, o:o + 384]; o += 384
    nsa_g = w[:, o:o + 3 * MIX_HEADS]; o += 3 * MIX_HEADS
    ret = w[:, o:o + 1024]; o += 1024
    rwkv = w[:, o:o + RWKV_COLS]; o += RWKV_COLS
    conv = w[:, o:o + 768]; o += 768
    gate = w[:, o:o + N_BRANCHES * D_MODEL]

    def rot_perm(x):
        half = HEAD_DIM // 2
        return [x[:, h * HEAD_DIM + p * half:h * HEAD_DIM + (p + 1) * half]
                for p in range(2) for h in range(MIX_HEADS)]

    ret = rot_perm(ret[:, 0:256]) + rot_perm(ret[:, 256:512]) + [ret[:, 512:]]
    zeros = lambda n: jnp.zeros((w.shape[0], n), w.dtype)
    out = jnp.concatenate([conv, nsa_q] + ret + [rwkv, zeros(1024 - RWKV_COLS), nsa_kv, nsa_g,
                                                 zeros(128 - 3 * MIX_HEADS)], axis=1)
    oz_ref[...] = out.astype(BF16)
    og_ref[...] = gate.astype(BF16)


def _rowmajor_w_in_kernel(w_ref, o_ref, *, n_cols):
    bc = w_ref.shape[0]
    col = pl.program_id(0) * bc + lax.broadcasted_iota(jnp.int32, (bc, w_ref.shape[2]), 0)
    for l in range(w_ref.shape[1]):
        o_ref[l] = jnp.where(col < n_cols, w_ref[:, l, :], 0.0).T.astype(BF16)


def _layout_w_in(w_in, tr=256, bc=256):
    L, D, n_in = w_in.shape
    n_pad = pl.cdiv(n_in, bc) * bc
    w_rows = pl.pallas_call(
        functools.partial(_rowmajor_w_in_kernel, n_cols=n_in),
        grid=(n_pad // bc,),
        in_specs=[pl.BlockSpec((bc, L, D), lambda i: (i, 0, 0))],
        out_specs=pl.BlockSpec((L, D, bc), lambda i: (0, 0, i)),
        out_shape=jax.ShapeDtypeStruct((L, D, n_pad), BF16),
        compiler_params=_cparams(("parallel",)),
        name="rowmajor_w_in",
    )(jnp.transpose(w_in, (2, 0, 1)))
    n_gate = N_BRANCHES * D_MODEL
    blk = lambda n: pl.BlockSpec((None, tr, n), lambda l, i: (l, i, 0))
    return pl.pallas_call(
        _layout_w_in_kernel,
        grid=(L, D // tr),
        in_specs=[blk(n_pad)],
        out_specs=[blk(Z_COLS), blk(n_gate)],
        out_shape=[jax.ShapeDtypeStruct((L, D, Z_COLS), BF16), jax.ShapeDtypeStruct((L, D, n_gate), BF16)],
        compiler_params=_cparams(("parallel", "parallel")),
        name="layout_w_in",
    )(w_rows)


def _layout_cmp(cmp_w, cmp_pe):
    L = cmp_w.shape[0]
    wk = cmp_w[:, 0]
    wv = cmp_w[:, 1]
    zero = jnp.zeros_like(wk)
    blk = jnp.concatenate([jnp.concatenate([wk, zero], axis=3), jnp.concatenate([zero, wv], axis=3)], axis=2)
    pe2 = jnp.concatenate([cmp_pe, cmp_pe], axis=2)[:, :, None, :]
    s = NSA_CMP_STRIDE
    return blk[:, :s].astype(BF16), blk[:, s:].astype(BF16), pe2[:, :s], pe2[:, s:]


def kernel(x, mem, ln_mix_pre, w_in, nsa_cmp_w, nsa_cmp_pe, rel_bias, ret_norm_g, rwkv_mu, rwkv_w0, rwkv_w2, rwkv_a0, rwkv_a2, rwkv_g2, rwkv_k_k, rwkv_k_a, rwkv_r_k, rwkv_ln_g, rwkv_ln_b, conv_w, w_branch, w_mix_out, ln_mix_post, ln_xa_pre, ln_mem, xa_wq, xa_wkv, xa_wo, ln_xa_post, ln_mlp_pre, mlp_w1, mlp_w2, ln_mlp_post):
    B, S, D = x.shape
    M = mem.shape[1]
    depth = w_in.shape[0]
    row = lambda g: g[:, None, :]
    t = _tiles(B, S, M)

    w_in_b, w_gate_b = _layout_w_in(w_in, tr=t["w_layout_rows"], bc=t["w_layout_cols"])
    cmp_lo, cmp_hi, pe_lo, pe_hi = _layout_cmp(nsa_cmp_w, nsa_cmp_pe)
    nsa_tabs = _nsa_tables(rel_bias, S)
    ret_tabs = _ret_tables(S)
    rwkv_consts = _rwkv_consts()
    mu = jnp.pad(rwkv_mu, ((0, 0), (0, 1024 - RWKV_COLS)))[:, None, :]
    lora_w = _hi_lo(jnp.concatenate([_pad_rows(rwkv_w2, 0, 128), _pad_rows(rwkv_a2, RWKV_DECAY_LORA, 128),
                                     _pad_rows(rwkv_g2, RWKV_DECAY_LORA + RWKV_AAA_LORA, 128)], axis=2))
    lora_w = lora_w.reshape(depth, 256, 3 * MIX_WIDTH)
    vec = jnp.stack([rwkv_w0, rwkv_a0, rwkv_k_k, rwkv_k_a, rwkv_r_k, rwkv_ln_g, rwkv_ln_b,
                     jnp.zeros_like(rwkv_w0)], axis=1)
    conv_p = jnp.pad(conv_w, ((0, 0), (0, 8 - CONV_WIDTH), (0, 0)))
    w_branch_b = w_branch.astype(BF16)
    w_mix_out_b = w_mix_out.astype(BF16)
    xa_wq_b = xa_wq.astype(BF16)
    xa_wkv_b = xa_wkv.astype(BF16)
    xa_wo_b = xa_wo.astype(BF16)
    mlp_w1_b = mlp_w1.astype(BF16)
    mlp_w2_b = mlp_w2.astype(BF16)

    xf = x.reshape(B * S, D)
    kvm = norm_matmul_all_layers(mem.reshape(B * M, D), row(ln_mem), xa_wkv_b, tm=t["mem_rows"], out_dtype=BF16)
    for l in range(depth):
        z = norm_matmul(xf, row(ln_mix_pre), w_in_b, l, tm=t["in_proj_rows"], out_dtype=F32)
        prep = nsa_compress(z, cmp_lo, cmp_hi, pe_lo, pe_hi, l, B, S)
        z3 = z.reshape(B, S, Z_COLS)
        G = t["mixer_seqs"]
        o_nsa = nsa_attention(z3, prep, *nsa_tabs, B, S, G).reshape(B * S, MIX_WIDTH)
        o_ret = retention(z3, ret_tabs, row(ret_norm_g), l, B, S, G).reshape(B * S, MIX_WIDTH)
        o_rwkv = rwkv7(z3, mu, lora_w, vec, rwkv_consts, l, B, S, G).reshape(B * S, MIX_WIDTH)
        xf = merge_mix(xf, z, o_nsa, o_ret, o_rwkv, conv_p, row(ln_mix_pre), w_gate_b, w_branch_b,
                       w_mix_out_b, row(ln_mix_post), l, B, S, tm=t["merge_rows"])
        xf = cross_attention(xf, kvm, row(ln_xa_pre), xa_wq_b, xa_wo_b, row(ln_xa_post), l, B, S, M,
                             tm=t["xattn_rows"])
        xf = mlp(xf, row(ln_mlp_pre), mlp_w1_b, mlp_w2_b, row(ln_mlp_post), l, tm=t["mlp_rows"])
    return xf.reshape(B, S, D)
```

```python
import functools
import math

import numpy as np
import jax
import jax.numpy as jnp
from jax import lax
from jax.experimental import pallas as pl
from jax.experimental.pallas import tpu as pltpu

F32 = jnp.float32
BF16 = jnp.bfloat16

D_MODEL = 1024
N_BRANCHES = 4
MIX_WIDTH = 256
HEAD_DIM = 64
MIX_HEADS = 4

NSA_CMP_LEN = 32
NSA_CMP_STRIDE = 16
NSA_SLC_LEN = 64
NSA_TOP_N = 8
NSA_WINDOW = 512
Q_BLOCK = 128
NSA_V_ROWS = HEAD_DIM + 16
FORCE_BONUS = 1e4
N_BUCKETS = 32
MAX_DISTANCE = 128

RET_CHUNK = 128
ROPE_BASE = 10000.0
RET_NORM_EPS = 1e-5

RWKV_DECAY_LORA = 32
RWKV_AAA_LORA = 32
RWKV_GATE_LORA = 64
RWKV_GN_EPS = 64e-5
RWKV_COLS = 3 * MIX_WIDTH + RWKV_DECAY_LORA + RWKV_AAA_LORA + RWKV_GATE_LORA
RWKV_CHUNK = 64

CONV_WIDTH = 3
XA_HEADS = 4
XA_HEAD_DIM = D_MODEL // XA_HEADS
D_FF = 4 * D_MODEL

RMS_EPS = 1e-6
LOG2E = math.log2(math.e)
NEG_INF = -1e30
NEG_BIG = -3e38

OFF_CONV = 0
OFF_NSAQ = 768
OFF_RET = 1024
OFF_RWKV = 2048
OFF_NSAKV = 3072
OFF_NSAG = 3456
Z_COLS = 3584

VMEM_LIMIT = 56 * 1024 * 1024
ROW_SUB_TILES = 2


def _tiles(B, S, M):
    rows = B * S
    return dict(
        in_proj_rows=min(1024, rows),
        merge_rows=min(512, S),
        xattn_rows=min(2048, S),
        mlp_rows=min(1024, rows),
        mem_rows=min(2 * M, B * M),
        mixer_seqs=min(8, B),
        w_layout_rows=256,
        w_layout_cols=256,
    )


def _cparams(sem):
    return pltpu.CompilerParams(dimension_semantics=sem, vmem_limit_bytes=VMEM_LIMIT)


def _dot(a, b):
    return jnp.dot(a, b, preferred_element_type=F32)


def _dot_nt(a, b):
    return lax.dot_general(a, b, (((1,), (1,)), ((), ())), preferred_element_type=F32)


def _dot_tn(a, b):
    return lax.dot_general(a, b, (((0,), (0,)), ((), ())), preferred_element_type=F32)


def _split2(x):
    hi = x.astype(BF16)
    lo = (x - hi.astype(F32)).astype(BF16)
    return hi, lo


def _dot_x2(x, w_bf16, *more):
    xs = (x,) + more
    rows = x.shape[0]
    parts = [p for xi in xs for p in _split2(xi)]
    out = _dot(jnp.concatenate(parts, axis=0), w_bf16)
    res = [out[2 * i * rows:(2 * i + 1) * rows] + out[(2 * i + 1) * rows:(2 * i + 2) * rows]
           for i in range(len(xs))]
    return res[0] if not more else res


def _rms(x, g):
    ms = jnp.mean(x * x, axis=-1, keepdims=True)
    return x * lax.rsqrt(ms + RMS_EPS) * g


def _norm_matmul_kernel(x_ref, g_ref, w_ref, o_ref):
    n_sub = ROW_SUB_TILES
    tm = x_ref.shape[0] // n_sub
    hs = [_rms(x_ref[pl.ds(i * tm, tm), :], g_ref[...]).astype(BF16) for i in range(n_sub)]
    for i in range(n_sub):
        o_ref[pl.ds(i * tm, tm), :] = _dot(hs[i], w_ref[...]).astype(o_ref.dtype)


def norm_matmul(x, g, w, layer, tm, out_dtype):
    M, D = x.shape
    N = w.shape[2]
    return pl.pallas_call(
        _norm_matmul_kernel,
        grid=(M // tm,),
        in_specs=[
            pl.BlockSpec((tm, D), lambda i: (i, 0)),
            pl.BlockSpec((None, 1, D), lambda i: (layer, 0, 0)),
            pl.BlockSpec((None, D, N), lambda i: (layer, 0, 0), pipeline_mode=pl.Buffered(1)),
        ],
        out_specs=pl.BlockSpec((tm, N), lambda i: (i, 0)),
        out_shape=jax.ShapeDtypeStruct((M, N), out_dtype),
        compiler_params=_cparams(("parallel",)),
        name="norm_matmul",
    )(x, g, w)


def _nsa_compress_kernel(kvc_ref, kvs_ref, kvw_ref, wlo_ref, whi_ref, pelo_ref, pehi_ref,
                         kc_ref, vct_ref, ks_ref, vst_ref, kw_ref, vwt_ref):
    nblk = kc_ref.shape[0]
    n_kt = vwt_ref.shape[0]
    ylo = jnp.zeros((nblk, 128), F32)
    yhi = jnp.zeros((nblk, 128), F32)
    for r in range(NSA_CMP_STRIDE):
        xr = kvc_ref[pl.ds(r, nblk, stride=NSA_CMP_STRIDE), :]
        ylo = ylo + _dot((xr + pelo_ref[r]).astype(BF16), wlo_ref[r])
        yhi = yhi + _dot((xr + pehi_ref[r]).astype(BF16), whi_ref[r])
    y = ylo + pltpu.roll(yhi, nblk - 1, axis=0)
    kc_ref[...] = y.astype(BF16)
    vct_ref[...] = y.T[HEAD_DIM:, :].astype(BF16)
    kw_ref[...] = kvw_ref[...].astype(BF16)
    ones = jnp.ones((NSA_V_ROWS - HEAD_DIM, 128), BF16)
    for kt in range(n_kt):
        rs = slice(kt * 128, (kt + 1) * 128)
        vst_ref[kt] = jnp.concatenate([kvs_ref[rs, :].T[HEAD_DIM:, :].astype(BF16), ones], axis=0)
        vwt_ref[kt] = jnp.concatenate([kvw_ref[rs, :].T[HEAD_DIM:, :].astype(BF16), ones], axis=0)
    S = kvs_ref.shape[0]
    key = lax.broadcasted_iota(jnp.int32, (S, 128), 0)
    lane = lax.broadcasted_iota(jnp.int32, (S, 128), 1)
    onehot = (lane == HEAD_DIM + key // NSA_SLC_LEN).astype(F32)
    ks_ref[0:S, :] = jnp.where(lane < HEAD_DIM, kvs_ref[...], onehot).astype(BF16)
    pad_lane = lax.broadcasted_iota(jnp.int32, (128, 128), 1)
    ks_ref[S:S + 128, :] = (pad_lane == HEAD_DIM + S // NSA_SLC_LEN).astype(BF16)
    vst_ref[n_kt] = jnp.zeros((NSA_V_ROWS, 128), BF16)


def nsa_compress(z, wlo, whi, pelo, pehi, layer, B, S):
    nblk = S // NSA_CMP_STRIDE
    n_kt = S // 128
    c0 = OFF_NSAKV // 128
    wmap = lambda b: (layer, 0, 0, 0)
    b3 = lambda b: (b, 0, 0)
    b4 = lambda b: (b, 0, 0, 0)
    return pl.pallas_call(
        _nsa_compress_kernel,
        grid=(B,),
        in_specs=[
            pl.BlockSpec((S, 128), lambda b: (b, c0)),
            pl.BlockSpec((S, 128), lambda b: (b, c0 + 1)),
            pl.BlockSpec((S, 128), lambda b: (b, c0 + 2)),
            pl.BlockSpec((None, NSA_CMP_STRIDE, 128, 128), wmap),
            pl.BlockSpec((None, NSA_CMP_STRIDE, 128, 128), wmap),
            pl.BlockSpec((None, NSA_CMP_STRIDE, 1, 128), wmap),
            pl.BlockSpec((None, NSA_CMP_STRIDE, 1, 128), wmap),
        ],
        out_specs=[
            pl.BlockSpec((None, nblk, 128), b3),
            pl.BlockSpec((None, HEAD_DIM, nblk), b3),
            pl.BlockSpec((None, S + 128, 128), b3),
            pl.BlockSpec((None, n_kt + 1, NSA_V_ROWS, 128), b4),
            pl.BlockSpec((None, S, 128), b3),
            pl.BlockSpec((None, n_kt, NSA_V_ROWS, 128), b4),
        ],
        out_shape=[
            jax.ShapeDtypeStruct((B, nblk, 128), BF16),
            jax.ShapeDtypeStruct((B, HEAD_DIM, nblk), BF16),
            jax.ShapeDtypeStruct((B, S + 128, 128), BF16),
            jax.ShapeDtypeStruct((B, n_kt + 1, NSA_V_ROWS, 128), BF16),
            jax.ShapeDtypeStruct((B, S, 128), BF16),
            jax.ShapeDtypeStruct((B, n_kt, NSA_V_ROWS, 128), BF16),
        ],
        compiler_params=_cparams(("parallel",)),
        name="nsa_compress",
    )(z, z, z, wlo, whi, pelo, pehi)


def _rowmax(s):
    return jnp.max(s, axis=-1, keepdims=True)


def _rowsum(s):
    return jnp.sum(s, axis=-1, keepdims=True)


def _colmax(s):
    return jnp.max(s, axis=0, keepdims=True)


def _colsum(s):
    return jnp.sum(s, axis=0, keepdims=True)


def _nsa_kernel(q_ref, gl_ref, kc_ref, vct_ref, ks_ref, vst_ref, kw_ref, vwt_ref, bc_ref, tab_ref,
                ovt_ref, o_ref):
    bi = pl.program_id(1)
    G = q_ref.shape[0]
    n_kt = vwt_ref.shape[1]
    n_blk = ovt_ref.shape[0]
    n_cmp = kc_ref.shape[1]
    QB = Q_BLOCK
    R = MIX_HEADS * QB
    n_off = NSA_WINDOW // 128

    lo = jnp.maximum(bi - n_off, 0)
    q4 = [None] * G
    q4_sel = [None] * G
    o_c = [None] * G
    carry_s = [None] * G
    carry_w = [None] * G

    def key_tile(kref, g, kt, lanes):
        return kref[g, pl.ds(pl.multiple_of(kt * 128, 128), 128), lanes]

    def lazy_scores(keys, q):
        cache = {}

        def head_scores(tile, h):
            half = h // 2
            if half not in cache:
                cache[half] = _dot(keys, q[:, half * 2 * QB:(half + 1) * 2 * QB])
            c = (h % 2) * QB
            return cache[half][tile * 128:(tile + 1) * 128, c:c + QB]

        return functools.partial(head_scores, 0), functools.partial(head_scores, 1)

    def pair_qk(g, k0, k1):
        keys = jnp.concatenate([key_tile(kw_ref, g, k0, slice(0, HEAD_DIM)),
                                key_tile(kw_ref, g, k1, slice(0, HEAD_DIM))], axis=0)
        return lazy_scores(keys, q4[g])

    def pair_qk_sel(g, k0, k1):
        keys = jnp.concatenate([key_tile(ks_ref, g, k0, slice(None)), key_tile(ks_ref, g, k1, slice(None))],
                               axis=0)
        return lazy_scores(keys, q4_sel[g])

    def head_cols(x, h):
        return x[:, h * QB:(h + 1) * QB]

    key1 = lax.broadcasted_iota(jnp.int32, (128, QB), 0)
    i1 = lax.broadcasted_iota(jnp.int32, (128, QB), 1)
    causal1 = key1 <= i1
    kp = jnp.maximum(bi - 1, 0)
    pen_prev = jnp.where(bi >= 1, 0.0, NEG_INF)

    def win_add(kt, valid):
        far = 2 * QB
        shift = jnp.where(valid, jnp.where(bi - kt < n_off, -far, 0), far)
        return jnp.where(key1 > i1 + shift, 0.0, NEG_INF)

    def diag_add(extra):
        return [jnp.where(causal1, head_cols(tab_ref[0], h) + extra, NEG_INF) for h in range(MIX_HEADS)]

    def prev_add(extra):
        return [head_cols(tab_ref[1], h) + extra for h in range(MIX_HEADS)]

    def pair_update(carry, qk0, add0, vt0, qk1, add1, vt1):
        p0s, p1s, ms, alphas = [], [], [], []
        for h in range(MIX_HEADS):
            a0 = qk0(h) if add0 is None else qk0(h) + add0[h]
            a1 = qk1(h) if add1 is None else qk1(h) + add1[h]
            m_new = jnp.maximum(_colmax(a0), _colmax(a1))
            if carry is not None:
                m_old = head_cols(carry[0], h)
                m_new = jnp.maximum(m_old, m_new)
                alphas.append(jnp.exp2(m_old - m_new))
            ms.append(m_new)
            p0s.append(jnp.exp2(a0 - m_new).astype(BF16))
            p1s.append(jnp.exp2(a1 - m_new).astype(BF16))
        yield
        probs = jnp.concatenate([jnp.concatenate(p0s, axis=1), jnp.concatenate(p1s, axis=1)], axis=0)
        acc = _dot(jnp.concatenate([vt0, vt1], axis=1), probs)
        if carry is not None:
            acc = jnp.concatenate(alphas, axis=1) * carry[1] + acc
        return jnp.concatenate(ms, axis=1), acc

    def head_part(g):
        qt = (q_ref[g] * (HEAD_DIM ** -0.5)).T
        q4f = jnp.concatenate([qt[h * HEAD_DIM:(h + 1) * HEAD_DIM] for h in range(MIX_HEADS)], axis=1)
        q4[g] = (q4f * LOG2E).astype(BF16)
        q4_cmp = q4f.astype(BF16)
        yield
        qk_d, qk_p = pair_qk(g, bi, kp)
        carry_w[g] = yield from pair_update(None, qk_d, diag_add(0.0), vwt_ref[g, bi],
                                            qk_p, prev_add(pen_prev), vwt_ref[g, kp])
        yield
        cmp_per_qb = QB // NSA_CMP_STRIDE
        g0 = pl.multiple_of((n_kt - 1 - bi) * cmp_per_qb, cmp_per_qb)
        qk = _dot(kc_ref[g, :, :HEAD_DIM], q4_cmp) * LOG2E
        nrow = lax.broadcasted_iota(jnp.int32, (n_cmp, QB), 0)
        tq = bi * QB + lax.broadcasted_iota(jnp.int32, (n_cmp, QB), 1)
        vis = tq - (nrow * NSA_CMP_STRIDE + (NSA_CMP_LEN - 1)) >= 0
        live = (tq >= NSA_CMP_LEN - 1).astype(F32)
        yield
        p_cs = []
        for h in range(MIX_HEADS):
            s = jnp.where(vis, head_cols(qk, h) + bc_ref[pl.ds(g0, n_cmp), h * QB:(h + 1) * QB], NEG_INF)
            e = jnp.exp2(s - _colmax(s))
            p_cs.append(e / _colsum(e) * live)
        yield
        o_c[g] = _dot(vct_ref[g], jnp.concatenate(p_cs, axis=1).astype(BF16))
        imp = _dot(ovt_ref[...], p_cs[0].astype(BF16))
        for h in range(1, MIX_HEADS):
            imp = imp + _dot(ovt_ref[...], p_cs[h].astype(BF16))
        yield
        blk = lax.broadcasted_iota(jnp.int32, (n_blk, QB), 0)
        cur = (bi * QB + lax.broadcasted_iota(jnp.int32, (n_blk, QB), 1)) // NSA_SLC_LEN
        forced = (blk == 0) | (blk == cur) | (blk == cur - 1)
        imp = jnp.where(forced, imp + FORCE_BONUS, imp)
        imp = jnp.where(blk <= cur, imp, NEG_INF)
        blk_f = blk.astype(F32)
        sel = jnp.zeros((n_blk, QB), F32)
        for _ in range(NSA_TOP_N):
            mx = _colmax(imp)
            idx = jnp.min(jnp.where(imp == mx, blk_f, 4096.0), axis=0, keepdims=True)
            pick = blk_f == idx
            sel = jnp.where(pick, 1.0, sel)
            imp = jnp.where(pick, NEG_BIG, imp)
            yield
        sel_add = (sel - 1.0) * (-NEG_INF)
        rows_left = 128 - HEAD_DIM - n_blk
        pad_rows = jnp.where(lax.broadcasted_iota(jnp.int32, (rows_left, QB), 0) == 0, NEG_INF, 0.0)
        mask_rows = jnp.concatenate([sel_add, pad_rows], axis=0).astype(BF16)
        q4_sel[g] = jnp.concatenate([q4[g], jnp.concatenate([mask_rows] * MIX_HEADS, axis=1)], axis=0)
        yield
        qk_d, qk_p = pair_qk_sel(g, bi, kp)
        carry_s[g] = yield from pair_update(None, qk_d, diag_add(0.0), vst_ref[g, bi],
                                            qk_p, prev_add(pen_prev), vst_ref[g, kp])

    _round_robin([head_part(g) for g in range(G)])

    n_old = jnp.maximum(bi - 1, 0)

    def sel_step(g, j, carry, out):
        k0 = 2 * j
        k1 = jnp.where(k0 + 1 < n_old, k0 + 1, n_kt)
        qk0, qk1 = pair_qk_sel(g, k0, k1)
        yield
        out[g] = yield from pair_update(carry, qk0, None, vst_ref[g, k0], qk1, None, vst_ref[g, k1])

    def win_step(g, j, carry, out):
        k0 = lo + 2 * j
        k1 = jnp.minimum(k0 + 1, bi)
        qk0, qk1 = pair_qk(g, k0, k1)
        yield
        out[g] = yield from pair_update(carry, qk0, [win_add(k0, True)] * MIX_HEADS, vwt_ref[g, k0],
                                        qk1, [win_add(k1, k0 + 1 < n_old)] * MIX_HEADS, vwt_ref[g, k1])

    def joint(step):
        def body(j, carries):
            out = [None] * G
            _round_robin([step(g, j, carries[g], out) for g in range(G)])
            return tuple(out)
        return body

    res_s = lax.fori_loop(0, (n_old + 1) // 2, joint(sel_step), tuple(carry_s))
    res_w = lax.fori_loop(0, (n_old - lo + 1) // 2, joint(win_step), tuple(carry_w))

    def tail_part(g):
        acc_s = res_s[g][1]
        acc_w = res_w[g][1]
        o_s = acc_s[:HEAD_DIM] * (1.0 / acc_s[HEAD_DIM:HEAD_DIM + 1])
        o_w = acc_w[:HEAD_DIM] * (1.0 / acc_w[HEAD_DIM:HEAD_DIM + 1])
        gate = jax.nn.sigmoid(gl_ref[g].T)
        yield
        outs = []
        for h in range(MIX_HEADS):
            cs = slice(h * QB, (h + 1) * QB)
            outs.append(gate[3 * h:3 * h + 1] * o_c[g][:, cs] + gate[3 * h + 1:3 * h + 2] * o_s[:, cs]
                        + gate[3 * h + 2:3 * h + 3] * o_w[:, cs])
        o_ref[g] = jnp.concatenate(outs, axis=0).T

    _round_robin([tail_part(g) for g in range(G)])


def nsa_attention(z, prep, bias_c, tab, ovt, B, S, G):
    n_qb = S // Q_BLOCK
    n_kt = S // 128
    n_cmp = S // NSA_CMP_STRIDE
    R = MIX_HEADS * Q_BLOCK
    kc, vct, ks, vst, kw, vwt = prep
    b3 = lambda b, i: (b, 0, 0)
    b4 = lambda b, i: (b, 0, 0, 0)
    return pl.pallas_call(
        _nsa_kernel,
        grid=(B // G, n_qb),
        in_specs=[
            pl.BlockSpec((G, Q_BLOCK, MIX_WIDTH), lambda b, i: (b, i, OFF_NSAQ // MIX_WIDTH)),
            pl.BlockSpec((G, Q_BLOCK, 128), lambda b, i: (b, i, OFF_NSAG // 128)),
            pl.BlockSpec((G, n_cmp, 128), b3),
            pl.BlockSpec((G, HEAD_DIM, n_cmp), b3),
            pl.BlockSpec((G, S + 128, 128), b3),
            pl.BlockSpec((G, n_kt + 1, NSA_V_ROWS, 128), b4),
            pl.BlockSpec((G, S, 128), b3),
            pl.BlockSpec((G, n_kt, NSA_V_ROWS, 128), b4),
            pl.BlockSpec(bias_c.shape, lambda b, i: (0, 0)),
            pl.BlockSpec((2, 128, R), lambda b, i: (0, 0, 0)),
            pl.BlockSpec(ovt.shape, lambda b, i: (0, 0)),
        ],
        out_specs=pl.BlockSpec((G, Q_BLOCK, MIX_WIDTH), lambda b, i: (b, i, 0)),
        out_shape=jax.ShapeDtypeStruct((B, S, MIX_WIDTH), F32),
        compiler_params=_cparams(("parallel", "arbitrary")),
        name="nsa_attention",
    )(z, z, kc, vct, ks, vst, kw, vwt, bias_c, tab, ovt)


def _stack_heads(x, head_of_lane):
    return jnp.concatenate([jnp.where(head_of_lane == h, x, jnp.zeros_like(x))
                            for h in range(MIX_HEADS)], axis=0)


def _head_norm(y, seg_mean, eps):
    mu = _dot_x2(y, seg_mean)
    d = y - mu
    var = _dot_x2(d * d, seg_mean)
    return d * lax.rsqrt(var + eps)


def _round_robin(chains):
    while chains:
        chains = [c for c in chains if next(c, StopIteration) is not StopIteration]


def _retention_kernel(z_ref, cos_ref, sin_ref, indec_ref, qdec_ref, kdec_ref, cdec_ref, bd_ref,
                      segm_ref, ng_ref, o_ref, st_ref):
    @pl.when(pl.program_id(1) == 0)
    def _():
        st_ref[...] = jnp.zeros_like(st_ref)

    _round_robin([_retention_chunk(z_ref.at[g], cos_ref, sin_ref, indec_ref, qdec_ref, kdec_ref, cdec_ref,
                                   bd_ref, segm_ref, ng_ref, o_ref.at[g], st_ref.at[g])
                  for g in range(z_ref.shape[0])])


def _retention_chunk(z_ref, cos_ref, sin_ref, indec_ref, qdec_ref, kdec_ref, cdec_ref, bd_ref,
                     segm_ref, ng_ref, o_ref, st_ref):
    L = RET_CHUNK
    z = z_ref[...]
    q = z[:, 0:256]
    k = z[:, 256:512]
    v = z[:, 512:768]
    g = z[:, 768:1024]
    cos = cos_ref[...]
    sin = sin_ref[...]

    def rot(u):
        u1 = u[:, :128]
        u2 = u[:, 128:]
        return jnp.concatenate([u1 * cos - u2 * sin, u2 * cos + u1 * sin], axis=1)

    qr = rot(q) * (HEAD_DIM ** -0.5)
    kr = rot(k)
    lane = lax.broadcasted_iota(jnp.int32, (L, 256), 1)
    head_qk = (lane & 127) >> 5
    head_v = lane >> 6
    qb = qr.astype(BF16)
    kb = kr.astype(BF16)
    vb = v.astype(BF16)
    yield

    att = _dot_nt(_stack_heads(qb, head_qk), kb) * indec_ref[...]
    yield
    o_st = _dot(att.astype(BF16), vb)
    o = jnp.zeros((L, 256), F32)
    for h in range(MIX_HEADS):
        o = o + jnp.where(head_v == h, o_st[h * L:(h + 1) * L], 0.0)
    yield
    state = st_ref[...]
    o = o + _dot(qb, state.astype(BF16)) * qdec_ref[...]
    st_ref[...] = state * cdec_ref[...] + _dot_tn((kr * kdec_ref[...]).astype(BF16), vb) * bd_ref[...]
    yield

    mu = _dot_x2(o, segm_ref[...])
    yield
    d = o - mu
    var = _dot_x2(d * d, segm_ref[...])
    yield
    o_ref[...] = g * jax.nn.sigmoid(g) * (d * lax.rsqrt(var + RET_NORM_EPS) * ng_ref[...])


def retention(z, tabs, ng, layer, B, S, G):
    L = RET_CHUNK
    nc = S // L
    cos, sin, indec, qdec, kdec, cdec, bd, segm = tabs
    const2 = lambda b, c: (0, 0)
    return pl.pallas_call(
        _retention_kernel,
        grid=(B // G, nc),
        in_specs=[
            pl.BlockSpec((G, L, 1024), lambda b, c: (b, c, OFF_RET // 1024)),
            pl.BlockSpec((L, 128), lambda b, c: (c, 0)),
            pl.BlockSpec((L, 128), lambda b, c: (c, 0)),
            pl.BlockSpec((MIX_HEADS * L, L), const2),
            pl.BlockSpec((L, 256), const2),
            pl.BlockSpec((L, 256), const2),
            pl.BlockSpec((1, 256), const2),
            pl.BlockSpec((256, 256), const2),
            pl.BlockSpec((256, 256), const2),
            pl.BlockSpec((None, 1, 256), lambda b, c: (layer, 0, 0)),
        ],
        out_specs=pl.BlockSpec((G, L, 256), lambda b, c: (b, c, 0)),
        out_shape=jax.ShapeDtypeStruct((B, S, 256), F32),
        scratch_shapes=[pltpu.VMEM((G, 256, 256), F32)],
        compiler_params=_cparams(("parallel", "arbitrary")),
        name="retention",
    )(z, cos, sin, indec, qdec, kdec, cdec, bd, segm, ng)


def _rwkv_kernel(z_ref, mu_ref, lw_ref, vec_ref, tri_ref, bdl_ref, wide_ref,
                 segs_ref, segm_ref, o_ref, st_ref, prev_ref):
    @pl.when(pl.program_id(1) == 0)
    def _():
        st_ref[...] = jnp.zeros_like(st_ref)
        prev_ref[...] = jnp.zeros_like(prev_ref)

    _round_robin([_rwkv_chunk(z_ref.at[g], mu_ref, lw_ref, vec_ref, tri_ref, bdl_ref,
                              wide_ref, segs_ref, segm_ref, o_ref.at[g], st_ref.at[g], prev_ref.at[g])
                  for g in range(z_ref.shape[0])])


def _rwkv_chunk(z_ref, mu_ref, lw_ref, vec_ref, tri_ref, bdl_ref, wide_ref,
                segs_ref, segm_ref, o_ref, st_ref, prev_ref):
    C = RWKV_CHUNK
    z = z_ref[...]
    rows = lax.broadcasted_iota(jnp.int32, (C, 1024), 0)
    zs = jnp.where(rows == 0, prev_ref[0:1, :], pltpu.roll(z, 1, axis=0))
    prev_ref[...] = jnp.broadcast_to(z[C - 1:C, :], prev_ref.shape)
    zf = z + (zs - z) * mu_ref[...]
    r = zf[:, 0:256]
    k = zf[:, 256:512]
    v = zf[:, 512:768]
    lora = zf[:, 768:896]

    vec = vec_ref[...]
    w0, a0, k_k, k_a, r_k, ln_g, ln_b = (vec[i:i + 1] for i in range(7))

    lane = lax.broadcasted_iota(jnp.int32, lora.shape, 1)
    lora_in = jnp.where(lane < RWKV_DECAY_LORA, jnp.tanh(lora),
                        jnp.where(lane < RWKV_DECAY_LORA + RWKV_AAA_LORA, lora, jax.nn.sigmoid(lora)))
    hi, lo = _split2(lora_in)
    lhs = jnp.concatenate([jnp.concatenate([hi, hi], axis=1),
                           jnp.concatenate([lo, jnp.zeros_like(lo)], axis=1)], axis=0)
    both = _dot(lhs, lw_ref[...])
    lora_out = both[:C] + both[C:]
    wpre = w0 + lora_out[:, 0:256]
    y = -wpre
    softplus = jnp.maximum(y, 0.0) + jnp.log(1.0 + jnp.exp(-jnp.abs(y)))
    w_log = -softplus - 0.5
    ld = -jnp.exp(w_log)
    a = jax.nn.sigmoid(a0 + lora_out[:, 256:512])
    gate = lora_out[:, 512:768]
    kk = k * k_k
    k2 = k * (1.0 + (a - 1.0) * k_a)
    kk_sq, bonus = _dot_x2(kk * kk, segs_ref[...], r * k2 * r_k)
    kk = kk / jnp.maximum(jnp.sqrt(kk_sq), 1e-12)
    yield

    tri = tri_ref[...]
    l1 = ld.astype(BF16)
    r1 = ld - l1.astype(F32)
    l2 = r1.astype(BF16)
    l3 = (r1 - l2.astype(F32)).astype(BF16)
    cs = _dot(jnp.concatenate([tri, tri, tri], axis=1), jnp.concatenate([l1, l2, l3], axis=0))
    cs_end = cs[C - 1:C, :]
    yield
    e_neg = jnp.exp(-cs)
    e_end = jnp.exp(cs_end - cs)
    kka = kk * a
    a_t = -kk * jnp.exp(cs - ld)
    r_t = r * jnp.exp(cs)
    b_t = kka * e_neg
    k_t = k2 * e_neg
    b_g = kka * e_end
    k_g = k2 * e_end

    lane = lax.broadcasted_iota(jnp.int32, (C, 256), 1)
    head = lane >> 6
    stack = lambda x: _stack_heads(x.astype(BF16), head)
    b_st, k_st, v_st = stack(b_t), stack(k_t), stack(v)

    ar = jnp.concatenate([a_t, r_t], axis=0).astype(BF16)
    bk = jnp.concatenate([b_st, k_st], axis=0)
    wide = _dot_nt(ar, bk)
    strict = wide_ref[0]
    incl = wide_ref[1]
    a_ak = (wide[0:C, 4 * C:] * strict).astype(BF16)
    a_rb = (wide[C:, 0:4 * C] * incl).astype(BF16)
    a_rk = (wide[C:, 4 * C:] * incl).astype(BF16)
    yield

    a_ab = wide[0:C, 0:4 * C]
    n_pow = jnp.concatenate([a_ab] * MIX_HEADS, axis=0) * bdl_ref[...]
    ri = lax.broadcasted_iota(jnp.int32, (4 * C, 4 * C), 0)
    ci = lax.broadcasted_iota(jnp.int32, (4 * C, 4 * C), 1)
    t_inv = jnp.where(ri == ci, 1.0, 0.0) + n_pow
    yield

    sq = 2
    while sq < C:
        nb = n_pow.astype(BF16)
        n_pow = _dot(nb, nb)
        yield
        t_inv = t_inv + _dot(t_inv.astype(BF16), n_pow.astype(BF16))
        sq *= 2
    yield

    state = st_ref[...]
    sb = state.astype(BF16)
    from_state = _dot_nt(ar, sb)
    x = from_state[0:C] + _dot(a_ak, v_st)
    yield
    u_st = _dot(t_inv.astype(BF16), stack(x)).astype(BF16)
    yield
    uv = jnp.concatenate([u_st, v_st], axis=0)
    yv = from_state[C:] + _dot(jnp.concatenate([a_rb, a_rk], axis=1), uv)
    bkg = jnp.concatenate([stack(b_g), stack(k_g)], axis=0)
    st_ref[...] = state * jnp.exp(cs_end) + _dot_tn(uv, bkg)
    yield

    yn = _head_norm(yv, segm_ref[...], RWKV_GN_EPS) * ln_g + ln_b
    yn = yn + bonus * v
    o_ref[...] = yn * gate


def rwkv7(z, mu, lora_w, vec, consts, layer, B, S, G):
    C = RWKV_CHUNK
    nc = S // C
    tri, bdl, wide, segs, segm = consts
    const2 = lambda b, c: (0, 0)
    return pl.pallas_call(
        _rwkv_kernel,
        grid=(B // G, nc),
        in_specs=[
            pl.BlockSpec((G, C, 1024), lambda b, c: (b, c, OFF_RWKV // 1024)),
            pl.BlockSpec((None, 1, 1024), lambda b, c: (layer, 0, 0)),
            pl.BlockSpec((None, 256, 768), lambda b, c: (layer, 0, 0)),
            pl.BlockSpec((None, 8, 256), lambda b, c: (layer, 0, 0)),
            pl.BlockSpec((C, C), const2),
            pl.BlockSpec((4 * C, 4 * C), const2),
            pl.BlockSpec((2, C, 4 * C), lambda b, c: (0, 0, 0)),
            pl.BlockSpec((256, 256), const2),
            pl.BlockSpec((256, 256), const2),
        ],
        out_specs=pl.BlockSpec((G, C, 256), lambda b, c: (b, c, 0)),
        out_shape=jax.ShapeDtypeStruct((B, S, 256), F32),
        scratch_shapes=[pltpu.VMEM((G, 256, 256), F32), pltpu.VMEM((G, 8, 1024), F32)],
        compiler_params=_cparams(("parallel", "arbitrary")),
        name="rwkv7",
    )(z, mu, lora_w, vec, tri, bdl, wide, segs, segm)


def _merge_kernel(x_ref, zc_ref, on_ref, or_ref, ow_ref, cw_ref, gpre_ref, wg_ref, wb_ref, wo_ref, g_ref,
                  o_ref, carry_ref):
    @pl.when(pl.program_id(1) == 0)
    def _():
        carry_ref[...] = jnp.zeros_like(carry_ref)

    n_sub = ROW_SUB_TILES
    tm = x_ref.shape[0] // n_sub

    def prologue(r0):
        rs = pl.ds(r0, tm)
        x = x_ref[rs, :]
        h = _rms(x, gpre_ref[...]).astype(BF16)
        zc = zc_ref[rs, :]
        b_g = zc[:, 0:256]
        u = zc[:, 256:512] * zc[:, 512:768]
        prev = carry_ref[...]
        rows = lax.broadcasted_iota(jnp.int32, (tm, 256), 0)
        u1 = jnp.where(rows == 0, prev[7:8], pltpu.roll(u, 1, axis=0))
        u2 = jnp.where(rows == 0, prev[6:7], jnp.where(rows == 1, prev[7:8], pltpu.roll(u, 2, axis=0)))
        carry_ref[...] = u[tm - 8:tm]
        cw = cw_ref[...]
        o_conv = b_g * (cw[0:1] * u2 + cw[1:2] * u1 + cw[2:3] * u)
        return x, h, (on_ref[rs, :], or_ref[rs, :], ow_ref[rs, :], o_conv)

    def project(r0, x, h, branches):
        merged = jnp.zeros((tm, D_MODEL), F32)
        for m in range(N_BRANCHES):
            gm = jax.nn.sigmoid(_dot(h, wg_ref[:, m * D_MODEL:(m + 1) * D_MODEL]))
            merged = merged + gm * _dot(branches[m].astype(BF16), wb_ref[m])
        y = _dot(merged.astype(BF16), wo_ref[...])
        o_ref[pl.ds(r0, tm), :] = x + _rms(y, g_ref[...])

    staged = [prologue(i * tm) for i in range(n_sub)]
    for i in range(n_sub):
        project(i * tm, *staged[i])


def merge_mix(x, z, o_nsa, o_ret, o_rwkv, conv_w, g_pre, w_gate, w_branch, w_out, g_post, layer, B, S, tm):
    nt = S // tm
    rowmap = lambda b, i: (b * nt + i, 0)
    lmap = lambda b, i: (layer, 0, 0)
    return pl.pallas_call(
        _merge_kernel,
        grid=(B, nt),
        in_specs=[
            pl.BlockSpec((tm, D_MODEL), rowmap),
            pl.BlockSpec((tm, 768), lambda b, i: (b * nt + i, OFF_CONV // 768)),
            pl.BlockSpec((tm, 256), rowmap),
            pl.BlockSpec((tm, 256), rowmap),
            pl.BlockSpec((tm, 256), rowmap),
            pl.BlockSpec((None, 8, 256), lmap),
            pl.BlockSpec((None, 1, D_MODEL), lmap),
            pl.BlockSpec((None, D_MODEL, N_BRANCHES * D_MODEL), lmap, pipeline_mode=pl.Buffered(1)),
            pl.BlockSpec((None, N_BRANCHES, MIX_WIDTH, D_MODEL), lambda b, i: (layer, 0, 0, 0),
                         pipeline_mode=pl.Buffered(1)),
            pl.BlockSpec((None, D_MODEL, D_MODEL), lmap, pipeline_mode=pl.Buffered(1)),
            pl.BlockSpec((None, 1, D_MODEL), lmap),
        ],
        out_specs=pl.BlockSpec((tm, D_MODEL), rowmap),
        out_shape=jax.ShapeDtypeStruct((B * S, D_MODEL), F32),
        scratch_shapes=[pltpu.VMEM((8, 256), F32)],
        compiler_params=_cparams(("parallel", "arbitrary")),
        name="merge_mix",
    )(x, z, o_nsa, o_ret, o_rwkv, conv_w, g_pre, w_gate, w_branch, w_out, g_post)


def _xattn_kernel(x_ref, kv_ref, gpre_ref, wq_ref, wo_ref, gpost_ref, o_ref):
    n_sub = ROW_SUB_TILES
    tm = x_ref.shape[0] // n_sub

    def prologue(r0):
        x = x_ref[pl.ds(r0, tm), :]
        return x, _rms(x, gpre_ref[...]).astype(BF16)

    def attend(r0, x, h):
        q = (_dot(h, wq_ref[...]) * (XA_HEAD_DIM ** -0.5)).astype(BF16)
        outs = []
        for hd in range(XA_HEADS):
            cs = slice(hd * XA_HEAD_DIM, (hd + 1) * XA_HEAD_DIM)
            kh = kv_ref[:, hd * XA_HEAD_DIM:(hd + 1) * XA_HEAD_DIM]
            vh = kv_ref[:, D_MODEL + hd * XA_HEAD_DIM:D_MODEL + (hd + 1) * XA_HEAD_DIM]
            s = _dot_nt(q[:, cs], kh)
            e = jnp.exp(s - _rowmax(s))
            outs.append((_dot(e.astype(BF16), vh) * (1.0 / _rowsum(e))).astype(BF16))
        y = _dot(jnp.concatenate(outs, axis=1), wo_ref[...])
        o_ref[pl.ds(r0, tm), :] = x + _rms(y, gpost_ref[...])

    staged = [prologue(i * tm) for i in range(n_sub)]
    for i in range(n_sub):
        attend(i * tm, *staged[i])


def cross_attention(x, kv, g_pre, wq, wo, g_post, layer, B, S, M, tm):
    nt = S // tm
    rowmap = lambda b, i: (b * nt + i, 0)
    lmap = lambda b, i: (layer, 0, 0)
    return pl.pallas_call(
        _xattn_kernel,
        grid=(B, nt),
        in_specs=[
            pl.BlockSpec((tm, D_MODEL), rowmap),
            pl.BlockSpec((M, 2 * D_MODEL), lambda b, i: (b, 0)),
            pl.BlockSpec((None, 1, D_MODEL), lmap),
            pl.BlockSpec((None, D_MODEL, D_MODEL), lmap),
            pl.BlockSpec((None, D_MODEL, D_MODEL), lmap),
            pl.BlockSpec((None, 1, D_MODEL), lmap),
        ],
        out_specs=pl.BlockSpec((tm, D_MODEL), rowmap),
        out_shape=jax.ShapeDtypeStruct((B * S, D_MODEL), F32),
        compiler_params=_cparams(("parallel", "arbitrary")),
        name="cross_attention",
    )(x, kv, g_pre, wq, wo, g_post)


def _mlp_kernel(x_ref, gpre_ref, w1_ref, w2_ref, gpost_ref, o_ref):
    n_sub = ROW_SUB_TILES
    tm = x_ref.shape[0] // n_sub
    tf = 1024

    def prologue(r0):
        x = x_ref[pl.ds(r0, tm), :]
        return x, _rms(x, gpre_ref[...]).astype(BF16)

    def project(r0, x, h):
        acc = jnp.zeros(x.shape, F32)
        for f in range(D_FF // tf):
            a = jnp.maximum(_dot(h, w1_ref[:, f * tf:(f + 1) * tf]), 0.0)
            acc = acc + _dot((a * a).astype(BF16), w2_ref[f * tf:(f + 1) * tf, :])
        o_ref[pl.ds(r0, tm), :] = x + _rms(acc, gpost_ref[...])

    staged = [prologue(i * tm) for i in range(n_sub)]
    for i in range(n_sub):
        project(i * tm, *staged[i])


def mlp(x, g_pre, w1, w2, g_post, layer, tm):
    M = x.shape[0]
    lmap = lambda i: (layer, 0, 0)
    return pl.pallas_call(
        _mlp_kernel,
        grid=(M // tm,),
        in_specs=[
            pl.BlockSpec((tm, D_MODEL), lambda i: (i, 0)),
            pl.BlockSpec((None, 1, D_MODEL), lmap),
            pl.BlockSpec((None, D_MODEL, D_FF), lmap, pipeline_mode=pl.Buffered(1)),
            pl.BlockSpec((None, D_FF, D_MODEL), lmap, pipeline_mode=pl.Buffered(1)),
            pl.BlockSpec((None, 1, D_MODEL), lmap),
        ],
        out_specs=pl.BlockSpec((tm, D_MODEL), lambda i: (i, 0)),
        out_shape=jax.ShapeDtypeStruct((M, D_MODEL), F32),
        compiler_params=_cparams(("parallel",)),
        name="mlp",
    )(x, g_pre, w1, w2, g_post)


def _t5_bucket(dist):
    n = jnp.maximum(dist, 0)
    max_exact = N_BUCKETS // 2
    nf = jnp.maximum(n, 1).astype(F32)
    large = max_exact + (jnp.log(nf / max_exact) / math.log(MAX_DISTANCE / max_exact)
                         * (N_BUCKETS - max_exact)).astype(jnp.int32)
    large = jnp.minimum(large, N_BUCKETS - 1)
    return jnp.where(n < max_exact, n, large)


def _nsa_tables(rel_bias, S):
    n_qb = S // Q_BLOCK
    n_cmp = S // NSA_CMP_STRIDE
    n_blk = S // NSA_SLC_LEN
    bias_f = rel_bias.astype(F32)
    i = jnp.arange(Q_BLOCK)
    per_qb = Q_BLOCK // NSA_CMP_STRIDE
    g = jnp.arange(n_cmp + per_qb * (n_qb - 1)) - per_qb * (n_qb - 1)
    d_c = i[None, :] - (g[:, None] * NSA_CMP_STRIDE + NSA_CMP_LEN - 1)
    def lookup(dist):
        hit = _t5_bucket(dist)[..., None, None] == jnp.arange(N_BUCKETS)[:, None]
        return jnp.sum(jnp.where(hit, bias_f, 0.0), axis=-2)

    bias_c = lookup(d_c).transpose(0, 2, 1).reshape(g.shape[0], MIX_HEADS * Q_BLOCK)
    d0 = i[:, None] - i[None, :]
    tab = jnp.stack([lookup(d0), lookup(d0 + Q_BLOCK)]) - bias_f[N_BUCKETS - 1]
    tab = tab.transpose(0, 2, 3, 1).reshape(2, Q_BLOCK, MIX_HEADS * Q_BLOCK)
    cs = np.arange(n_cmp)[None, :] * NSA_CMP_STRIDE
    ss = np.arange(n_blk)[:, None] * NSA_SLC_LEN
    ovt = np.clip(np.minimum(cs + NSA_CMP_LEN, ss + NSA_SLC_LEN) - np.maximum(cs, ss), 0, None)
    ovt = ovt.astype(np.float32) / NSA_CMP_LEN
    ovt[:, (S - NSA_CMP_LEN) // NSA_CMP_STRIDE + 1:] = 0.0
    return bias_c * LOG2E, tab * LOG2E, jnp.asarray(ovt, BF16)


def _ret_tables(S):
    L = RET_CHUNK
    H = MIX_HEADS
    half = HEAD_DIM // 2
    pos = jnp.arange(S)
    inv_freq = ROPE_BASE ** (-jnp.arange(half, dtype=F32) / half)
    ang = pos.astype(F32)[:, None] * inv_freq[None, :]
    cos = jnp.tile(jnp.cos(ang), (1, H))
    sin = jnp.tile(jnp.sin(ang), (1, H))
    lg = jnp.log(1.0 - 2.0 ** (-5.0 - jnp.arange(H, dtype=F32)))
    n = jnp.arange(L, dtype=F32)
    diff = n[:, None] - n[None, :]
    inner = jnp.where(diff >= 0, jnp.exp(jnp.maximum(diff, 0.0)[None] * lg[:, None, None]), 0.0)
    indec = inner.reshape(H * L, L)
    q_decay = jnp.exp((n + 1.0)[None, :] * lg[:, None])
    k_decay = jnp.exp((L - 1.0 - n)[None, :] * lg[:, None])
    chunk_decay = jnp.exp(L * lg)
    lane = np.arange(256)
    head_v = lane // HEAD_DIM
    head_qk = (lane % 128) // half
    qdec = q_decay.T[:, head_v]
    kdec = k_decay.T[:, head_qk]
    cdec = chunk_decay[head_v][None, :]
    bd = jnp.asarray((head_qk[:, None] == head_v[None, :]).astype(np.float32))
    segm = jnp.asarray((head_v[:, None] == head_v[None, :]).astype(np.float32) / HEAD_DIM, BF16)
    return cos, sin, indec, qdec, kdec, cdec, bd, segm


def _rwkv_consts():
    C = RWKV_CHUNK
    t = np.arange(C)
    tri = (t[:, None] >= t[None, :]).astype(np.float32)
    r = np.arange(4 * C)
    bdl = ((r[:, None] // C == r[None, :] // C) & (r[:, None] % C > r[None, :] % C)).astype(np.float32)
    strict = (t[:, None] > (r[None, :] % C)).astype(np.float32)
    incl = (t[:, None] >= (r[None, :] % C)).astype(np.float32)
    lane = np.arange(256) // HEAD_DIM
    seg = (lane[:, None] == lane[None, :]).astype(np.float32)
    return (jnp.asarray(tri, BF16), jnp.asarray(bdl), jnp.asarray(np.stack([strict, incl])),
            jnp.asarray(seg, BF16), jnp.asarray(seg / HEAD_DIM, BF16))


def _hi_lo(w):
    hi = w.astype(BF16)
    lo = (w - hi.astype(F32)).astype(BF16)
    return jnp.stack([hi, lo], axis=1)


def _pad_rows(w, top, total):
    return jnp.pad(w, ((0, 0), (top, total - top - w.shape[1]), (0, 0)))


def _layout_w_in_kernel(w_ref, oz_ref, og_ref):
    w = w_ref[...]
    o = 0
    nsa_q = w[:, o:o + 256]; o += 256
    nsa_kv = w[:, o:o + 384]; o += 384
    nsa_g = w[:, o:o + 3 * MIX_HEADS]; o += 3 * MIX_HEADS
    ret = w[:, o:o + 1024]; o += 1024
    rwkv = w[:, o:o + RWKV_COLS]; o += RWKV_COLS
    conv = w[:, o:o + 768]; o += 768
    gate = w[:, o:o + N_BRANCHES * D_MODEL]

    def rot_perm(x):
        half = HEAD_DIM // 2
        return [x[:, h * HEAD_DIM + p * half:h * HEAD_DIM + (p + 1) * half]
                for p in range(2) for h in range(MIX_HEADS)]

    ret = rot_perm(ret[:, 0:256]) + rot_perm(ret[:, 256:512]) + [ret[:, 512:]]
    zeros = lambda n: jnp.zeros((w.shape[0], n), w.dtype)
    out = jnp.concatenate([conv, nsa_q] + ret + [rwkv, zeros(1024 - RWKV_COLS), nsa_kv, nsa_g,
                                                 zeros(128 - 3 * MIX_HEADS)], axis=1)
    oz_ref[...] = out.astype(BF16)
    og_ref[...] = gate.astype(BF16)


def _rowmajor_w_in_kernel(w_ref, o_ref, *, n_cols):
    bc = w_ref.shape[0]
    col = pl.program_id(0) * bc + lax.broadcasted_iota(jnp.int32, (bc, w_ref.shape[2]), 0)
    for l in range(w_ref.shape[1]):
        o_ref[l] = jnp.where(col < n_cols, w_ref[:, l, :], 0.0).T.astype(BF16)


def _layout_w_in(w_in, tr=256, bc=256):
    L, D, n_in = w_in.shape
    n_pad = pl.cdiv(n_in, bc) * bc
    w_rows = pl.pallas_call(
        functools.partial(_rowmajor_w_in_kernel, n_cols=n_in),
        grid=(n_pad // bc,),
        in_specs=[pl.BlockSpec((bc, L, D), lambda i: (i, 0, 0))],
        out_specs=pl.BlockSpec((L, D, bc), lambda i: (0, 0, i)),
        out_shape=jax.ShapeDtypeStruct((L, D, n_pad), BF16),
        compiler_params=_cparams(("parallel",)),
        name="rowmajor_w_in",
    )(jnp.transpose(w_in, (2, 0, 1)))
    n_gate = N_BRANCHES * D_MODEL
    blk = lambda n: pl.BlockSpec((None, tr, n), lambda l, i: (l, i, 0))
    return pl.pallas_call(
        _layout_w_in_kernel,
        grid=(L, D // tr),
        in_specs=[blk(n_pad)],
        out_specs=[blk(Z_COLS), blk(n_gate)],
        out_shape=[jax.ShapeDtypeStruct((L, D, Z_COLS), BF16), jax.ShapeDtypeStruct((L, D, n_gate), BF16)],
        compiler_params=_cparams(("parallel", "parallel")),
        name="layout_w_in",
    )(w_rows)


def _layout_cmp(cmp_w, cmp_pe):
    L = cmp_w.shape[0]
    wk = cmp_w[:, 0]
    wv = cmp_w[:, 1]
    zero = jnp.zeros_like(wk)
    blk = jnp.concatenate([jnp.concatenate([wk, zero], axis=3), jnp.concatenate([zero, wv], axis=3)], axis=2)
    pe2 = jnp.concatenate([cmp_pe, cmp_pe], axis=2)[:, :, None, :]
    s = NSA_CMP_STRIDE
    return blk[:, :s].astype(BF16), blk[:, s:].astype(BF16), pe2[:, :s], pe2[:, s:]


def kernel(x, mem, ln_mix_pre, w_in, nsa_cmp_w, nsa_cmp_pe, rel_bias, ret_norm_g, rwkv_mu, rwkv_w0, rwkv_w2, rwkv_a0, rwkv_a2, rwkv_g2, rwkv_k_k, rwkv_k_a, rwkv_r_k, rwkv_ln_g, rwkv_ln_b, conv_w, w_branch, w_mix_out, ln_mix_post, ln_xa_pre, ln_mem, xa_wq, xa_wkv, xa_wo, ln_xa_post, ln_mlp_pre, mlp_w1, mlp_w2, ln_mlp_post):
    B, S, D = x.shape
    M = mem.shape[1]
    depth = w_in.shape[0]
    row = lambda g: g[:, None, :]
    t = _tiles(B, S, M)

    w_in_b, w_gate_b = _layout_w_in(w_in, tr=t["w_layout_rows"], bc=t["w_layout_cols"])
    cmp_lo, cmp_hi, pe_lo, pe_hi = _layout_cmp(nsa_cmp_w, nsa_cmp_pe)
    nsa_tabs = _nsa_tables(rel_bias, S)
    ret_tabs = _ret_tables(S)
    rwkv_consts = _rwkv_consts()
    mu = jnp.pad(rwkv_mu, ((0, 0), (0, 1024 - RWKV_COLS)))[:, None, :]
    lora_w = _hi_lo(jnp.concatenate([_pad_rows(rwkv_w2, 0, 128), _pad_rows(rwkv_a2, RWKV_DECAY_LORA, 128),
                                     _pad_rows(rwkv_g2, RWKV_DECAY_LORA + RWKV_AAA_LORA, 128)], axis=2))
    lora_w = lora_w.reshape(depth, 256, 3 * MIX_WIDTH)
    vec = jnp.stack([rwkv_w0, rwkv_a0, rwkv_k_k, rwkv_k_a, rwkv_r_k, rwkv_ln_g, rwkv_ln_b,
                     jnp.zeros_like(rwkv_w0)], axis=1)
    conv_p = jnp.pad(conv_w, ((0, 0), (0, 8 - CONV_WIDTH), (0, 0)))
    w_branch_b = w_branch.astype(BF16)
    w_mix_out_b = w_mix_out.astype(BF16)
    xa_wq_b = xa_wq.astype(BF16)
    xa_wkv_b = xa_wkv.astype(BF16)
    xa_wo_b = xa_wo.astype(BF16)
    mlp_w1_b = mlp_w1.astype(BF16)
    mlp_w2_b = mlp_w2.astype(BF16)

    xf = x.reshape(B * S, D)
    memf = mem.reshape(B * M, D)
    for l in range(depth):
        z = norm_matmul(xf, row(ln_mix_pre), w_in_b, l, tm=t["in_proj_rows"], out_dtype=F32)
        prep = nsa_compress(z, cmp_lo, cmp_hi, pe_lo, pe_hi, l, B, S)
        z3 = z.reshape(B, S, Z_COLS)
        G = t["mixer_seqs"]
        o_nsa = nsa_attention(z3, prep, *nsa_tabs, B, S, G).reshape(B * S, MIX_WIDTH)
        o_ret = retention(z3, ret_tabs, row(ret_norm_g), l, B, S, G).reshape(B * S, MIX_WIDTH)
        o_rwkv = rwkv7(z3, mu, lora_w, vec, rwkv_consts, l, B, S, G).reshape(B * S, MIX_WIDTH)
        xf = merge_mix(xf, z, o_nsa, o_ret, o_rwkv, conv_p, row(ln_mix_pre), w_gate_b, w_branch_b,
                       w_mix_out_b, row(ln_mix_post), l, B, S, tm=t["merge_rows"])
        kvm = norm_matmul(memf, row(ln_mem), xa_wkv_b, l, tm=t["mem_rows"], out_dtype=BF16)
        xf = cross_attention(xf, kvm, row(ln_xa_pre), xa_wq_b, xa_wo_b, row(ln_xa_post), l, B, S, M,
                             tm=t["xattn_rows"])
        xf = mlp(xf, row(ln_mlp_pre), mlp_w1_b, mlp_w2_b, row(ln_mlp_post), l, tm=t["mlp_rows"])
    return xf.reshape(B, S, D)
```
